```python
import math
import jax
import jax.numpy as jnp
from jax import lax
import numpy as np

D_MODEL = 1024
BATCH = 32
SEQ = 256
DEPTH = 2
DEC_BATCH = 8
DEC_SEQ = 1024
PAST_LEN = 512

GRID_W = 64
Q_BLOCK = 128
HD = 64
A_HEADS = 8
A_KV = 2
A_GROUP = A_HEADS // A_KV
B_HEADS = 4
B_DK = 128
B_DV = 128
B_CHUNK = 128
C_HEADS = 8
NA_ROWS = 8
NA_COLS = 16
D_HEADS = 4
D_VDIM = 2 * HD
N_EXPERTS = 32
TOP_K = 4
D_FF = 1024
SWIGLU_LIMIT = 7.0
SWIGLU_ALPHA = 1.702
ROPE_THETA = 10000.0
EPS = 1e-6
N_EVEN = (DEPTH + 1) // 2
N_ODD = DEPTH // 2

EVEN_SIZES = (A_HEADS * HD, A_KV * HD, A_KV * HD, B_HEADS * B_DK, B_HEADS * B_DK, B_HEADS * B_DV, B_HEADS * B_DV, 4 * B_HEADS)
EVEN_IN = sum(EVEN_SIZES)
EVEN_OUT = A_HEADS * HD + B_HEADS * B_DV
ODD_SIZES = (C_HEADS * HD, C_HEADS * HD, C_HEADS * HD, D_HEADS * 2 * HD, D_HEADS * 2 * HD, D_HEADS * D_VDIM)
ODD_IN = sum(ODD_SIZES)
ODD_OUT = C_HEADS * HD + D_HEADS * D_VDIM

F32 = jnp.float32

kernel_name = 'hybrid_prefix_diffusion_step'


def split_cols(x, sizes):
    cuts = [int(v) for v in np.cumsum(sizes)[:-1]]
    return jnp.split(x, cuts, axis=-1)


def to_heads(x, n):
    b, s, _ = x.shape
    return x.reshape(b, s, n, -1).transpose(0, 2, 1, 3)


def from_heads(x):
    b, h, s, d = x.shape
    return x.transpose(0, 2, 1, 3).reshape(b, s, h * d)


def pair_heads(x):
    b, h2, s, d = x.shape
    return x.reshape(b, h2 // 2, 2, s, d)


def rms_norm(x, g):
    xf = x.astype(F32)
    y = xf * lax.rsqrt(jnp.mean(xf * xf, axis=-1, keepdims=True) + EPS)
    return (y * g.astype(F32)).astype(x.dtype)


def modulation(cond, w_mod, b_mod):
    m = jax.nn.silu(cond) @ w_mod + b_mod
    return jnp.split(m[:, None, :], 6, axis=-1)


def axial_rope_tables(n_tokens, dim):
    half = dim // 2
    freqs = 1.0 / (ROPE_THETA ** (jnp.arange(0, half, 2, dtype=F32) / half))
    t = jnp.arange(n_tokens)
    rows = (t // GRID_W).astype(F32)
    cols = (t % GRID_W).astype(F32)
    ang = jnp.concatenate([rows[:, None] * freqs, cols[:, None] * freqs], axis=-1)
    return jnp.cos(ang), jnp.sin(ang)


def apply_rope(x, cos, sin):
    x1 = x[..., 0::2].astype(F32)
    x2 = x[..., 1::2].astype(F32)
    o1 = x1 * cos - x2 * sin
    o2 = x1 * sin + x2 * cos
    return jnp.stack([o1, o2], axis=-1).reshape(x.shape).astype(x.dtype)


def sweep_query_blocks(fn, q):
    *lead, s, d = q.shape
    nb = s // Q_BLOCK
    qb = jnp.moveaxis(q.reshape(*lead, nb, Q_BLOCK, d), -3, 0)
    out = lax.map(fn, qb)
    out = jnp.moveaxis(out, 0, -3)
    return out.reshape(*out.shape[:-3], s, out.shape[-1])


def gqa_attend(q, k, v):
    scale = q.shape[-1] ** -0.5

    def block(qb):
        s = jnp.einsum('bhgqd,bhkd->bhgqk', qb, k, preferred_element_type=F32) * scale
        p = jax.nn.softmax(s, axis=-1).astype(v.dtype)
        return jnp.einsum('bhgqk,bhkd->bhgqd', p, v)

    return sweep_query_blocks(block, q)


def diff_attend(q, k, v, lam):
    scale = HD ** -0.5

    def block(qb):
        s = jnp.einsum('bhjqd,bhjkd->bhjqk', qb, k, preferred_element_type=F32) * scale
        p = jax.nn.softmax(s, axis=-1)
        pd = (p[:, :, 0] - lam * p[:, :, 1]).astype(v.dtype)
        return jnp.einsum('bhqk,bhkd->bhqd', pd, v)

    return sweep_query_blocks(block, q)


def neighbourhood_attend(q, k, v, rpb, k_ctx, v_ctx):
    b, h, s, d = q.shape
    rows = s // GRID_W
    wr = min(NA_ROWS, rows)
    n_loc = wr * GRID_W
    scale = d ** -0.5
    qr = q.reshape(b, h, rows, GRID_W, d)
    kr = k.reshape(b, h, rows, GRID_W, d)
    vr = v.reshape(b, h, rows, GRID_W, d)
    col = jnp.arange(GRID_W)
    c0 = jnp.clip(col - NA_COLS // 2, 0, GRID_W - NA_COLS)
    col_ok = (col[None, :] >= c0[:, None]) & (col[None, :] < c0[:, None] + NA_COLS)
    dc = jnp.clip(col[None, :] - col[:, None], 1 - NA_COLS, NA_COLS - 1) + (NA_COLS - 1)

    def row_block(r):
        r0 = jnp.clip(r - wr // 2, 0, rows - wr)
        qb = lax.dynamic_index_in_dim(qr, r, axis=2, keepdims=False)
        kb = lax.dynamic_slice_in_dim(kr, r0, wr, axis=2)
        vb = lax.dynamic_slice_in_dim(vr, r0, wr, axis=2).reshape(b, h, n_loc, d)
        dr = r0 + jnp.arange(wr) - r + (NA_ROWS - 1)
        bias = rpb[:, dr[None, :, None], dc[:, None, :]]
        s_loc = jnp.einsum('bhqd,bhwkd->bhqwk', qb, kb, preferred_element_type=F32) * scale + bias
        s_loc = jnp.where(col_ok[:, None, :], s_loc, -jnp.inf).reshape(b, h, GRID_W, n_loc)
        s_ctx = jnp.einsum('bhqd,bhkd->bhqk', qb, k_ctx, preferred_element_type=F32) * scale
        p = jax.nn.softmax(jnp.concatenate([s_loc, s_ctx], axis=-1), axis=-1).astype(v.dtype)
        return (jnp.einsum('bhqk,bhkd->bhqd', p[..., :n_loc], vb)
                + jnp.einsum('bhqk,bhkd->bhqd', p[..., n_loc:], v_ctx))

    out = lax.map(row_block, jnp.arange(rows))
    return jnp.moveaxis(out, 0, 2).reshape(b, h, s, d)


def mlstm_chunkwise(q, k, v, log_i, log_f, C0, n0, m0):
    b, h, s, _ = q.shape
    nc = s // B_CHUNK

    def chunked(x):
        x = x.astype(F32)
        return jnp.moveaxis(x.reshape(b, h, nc, B_CHUNK, *x.shape[3:]), 2, 0)

    xs = (chunked(q), chunked(k), chunked(v), chunked(log_i), chunked(log_f))
    causal = jnp.tril(jnp.ones((B_CHUNK, B_CHUNK), dtype=bool))

    def step(carry, inp):
        C, n, m = carry
        qc, kc, vc, ic, fc = inp
        bcum = jnp.cumsum(fc, axis=-1)
        dlog = jnp.where(causal, bcum[..., :, None] - bcum[..., None, :] + ic[..., None, :], -jnp.inf)
        inter = bcum + m[..., None]
        m_t = jnp.maximum(inter, jnp.max(dlog, axis=-1))
        w_intra = jnp.exp(dlog - m_t[..., None])
        w_inter = jnp.exp(inter - m_t)
        qk = jnp.einsum('bhtd,bhsd->bhts', qc, kc) * w_intra
        num = jnp.einsum('bhts,bhsv->bhtv', qk, vc) + w_inter[..., None] * jnp.einsum('bhtd,bhdv->bhtv', qc, C)
        den = jnp.sum(qk, axis=-1) + w_inter * jnp.einsum('bhtd,bhd->bht', qc, n)
        h_out = num / jnp.maximum(jnp.abs(den), jnp.exp(-m_t))[..., None]
        b_last = bcum[..., -1]
        end_log = b_last[..., None] - bcum + ic
        m_new = jnp.maximum(b_last + m, jnp.max(end_log, axis=-1))
        w_end = jnp.exp(end_log - m_new[..., None])
        decay = jnp.exp(b_last + m - m_new)
        C_new = decay[..., None, None] * C + jnp.einsum('bhs,bhsd,bhsv->bhdv', w_end, kc, vc)
        n_new = decay[..., None] * n + jnp.einsum('bhs,bhsd->bhd', w_end, kc)
        return (C_new, n_new, m_new), h_out

    init = (C0.astype(F32), n0.astype(F32), m0.astype(F32))
    (C, n, m), hs = lax.scan(step, init, xs)
    return jnp.moveaxis(hs, 0, 2).reshape(b, h, s, -1), C, n, m


def reverse_seq(t):
    return jnp.flip(t, axis=2)


def mlstm_bidir(q, k, v, gates, C0, n0, m0):
    hf, Cf, nf, mf = mlstm_chunkwise(q, k, v, gates[..., 0], jax.nn.log_sigmoid(gates[..., 1]),
                                     C0[:, 0], n0[:, 0], m0[:, 0])
    hb, Cb, nb, mb = mlstm_chunkwise(reverse_seq(q), reverse_seq(k), reverse_seq(v),
                                     reverse_seq(gates[..., 2]), reverse_seq(jax.nn.log_sigmoid(gates[..., 3])),
                                     C0[:, 1], n0[:, 1], m0[:, 1])
    h = hf + reverse_seq(hb)
    return h, jnp.stack([Cf, Cb], axis=1), jnp.stack([nf, nb], axis=1), jnp.stack([mf, mb], axis=1)


def moe_ffn(h, router_w, router_b, w_gu, b_gu, w_down, b_down):
    b, s, d = h.shape
    x = h.reshape(b * s, d)
    logits = (x @ router_w + router_b).astype(F32)
    top_v, top_i = lax.top_k(logits, TOP_K)
    top_p = jax.nn.softmax(top_v, axis=-1)
    combine = jnp.sum(jax.nn.one_hot(top_i, N_EXPERTS, dtype=F32) * top_p[..., None], axis=1)
    y = jnp.zeros((b * s, d), F32)
    for e in range(N_EXPERTS):
        gu = x @ w_gu[e] + b_gu[e]
        gate = jnp.minimum(gu[:, :D_FF], SWIGLU_LIMIT)
        up = jnp.clip(gu[:, D_FF:], -SWIGLU_LIMIT, SWIGLU_LIMIT)
        act = (up + 1.0) * gate * jax.nn.sigmoid(SWIGLU_ALPHA * gate)
        y = y + combine[:, e:e + 1] * (act @ w_down[e] + b_down[e])
    return y.reshape(b, s, d).astype(h.dtype)


def diff_lambda(lam_params, layer_idx):
    lam_init = 0.8 - 0.6 * math.exp(-0.3 * layer_idx)
    lp = lam_params.astype(F32)
    lam = jnp.exp(jnp.sum(lp[0] * lp[1])) - jnp.exp(jnp.sum(lp[2] * lp[3])) + lam_init
    return lam, lam_init


def even_project(h, w_in, q_gain, k_gain, gate_bias):
    aq, ak, av, bq, bk, bv, bo, bg = split_cols(h @ w_in, EVEN_SIZES)
    b, s, _ = h.shape
    qa = rms_norm(to_heads(aq, A_HEADS), q_gain)
    ka = rms_norm(to_heads(ak, A_KV), k_gain)
    va = to_heads(av, A_KV)
    qb = to_heads(bq, B_HEADS)
    kb = to_heads(bk, B_HEADS) * (B_DK ** -0.5)
    vb = to_heads(bv, B_HEADS)
    gates = (bg.reshape(b, s, 4, B_HEADS).astype(F32) + gate_bias.astype(F32)).transpose(0, 3, 1, 2)
    return qa, ka, va, qb, kb, vb, bo, gates


def even_merge(oa, hb, bo, norm_gain, w_out):
    b, _, s, _ = hb.shape
    oa = from_heads(oa.reshape(b, A_HEADS, s, HD))
    hb = rms_norm(hb, norm_gain.reshape(B_HEADS, 1, B_DV)).astype(bo.dtype)
    ob = jax.nn.sigmoid(bo) * from_heads(hb)
    return jnp.concatenate([oa, ob], axis=-1) @ w_out


def even_context(h, w_in, w_out, q_gain, k_gain, gate_bias, norm_gain):
    qa, ka, va, qb, kb, vb, bo, gates = even_project(h, w_in, q_gain, k_gain, gate_bias)
    b, s, _ = h.shape
    oa = gqa_attend(qa.reshape(b, A_KV, A_GROUP, s, HD), ka, va)
    C0 = jnp.zeros((b, 2, B_HEADS, B_DK, B_DV), F32)
    n0 = jnp.zeros((b, 2, B_HEADS, B_DK), F32)
    m0 = jnp.zeros((b, 2, B_HEADS), F32)
    hb, C, n, m = mlstm_bidir(qb, kb, vb, gates, C0, n0, m0)
    return even_merge(oa, hb, bo, norm_gain, w_out), ka, va, C, n, m


def even_latent(h, k_ctx, v_ctx, C0, n0, m0, w_in, w_out, q_gain, k_gain, gate_bias, norm_gain):
    qa, ka, va, qb, kb, vb, bo, gates = even_project(h, w_in, q_gain, k_gain, gate_bias)
    b, s, _ = h.shape
    cos, sin = axial_rope_tables(s, HD)
    qa = apply_rope(qa, cos, sin)
    ka = apply_rope(ka, cos, sin)
    k_all = jnp.concatenate([k_ctx, ka], axis=2)
    v_all = jnp.concatenate([v_ctx, va], axis=2)
    oa = gqa_attend(qa.reshape(b, A_KV, A_GROUP, s, HD), k_all, v_all)
    hb, _, _, _ = mlstm_bidir(qb, kb, vb, gates, C0, n0, m0)
    return even_merge(oa, hb, bo, norm_gain, w_out)


def odd_project(h, w_in):
    cq, ck, cv, dq, dk, dv = split_cols(h @ w_in, ODD_SIZES)
    return (to_heads(cq, C_HEADS), to_heads(ck, C_HEADS), to_heads(cv, C_HEADS),
            to_heads(dq, 2 * D_HEADS), to_heads(dk, 2 * D_HEADS), to_heads(dv, D_HEADS))


def odd_merge(oc, od, lam_init, d_gain, w_out):
    od = rms_norm(od, d_gain) * (1.0 - lam_init)
    return jnp.concatenate([from_heads(oc), from_heads(od)], axis=-1) @ w_out


def odd_context(h, w_in, w_out, lam, lam_init, d_gain):
    qc, kc, vc, qd, kd, vd = odd_project(h, w_in)
    oc = gqa_attend(qc[:, :, None], kc, vc)[:, :, 0]
    kd = pair_heads(kd)
    od = diff_attend(pair_heads(qd), kd, vd, lam)
    return odd_merge(oc, od, lam_init, d_gain, w_out), kc, vc, kd, vd


def odd_latent(h, k_ctx_c, v_ctx_c, k_ctx_d, v_ctx_d, w_in, w_out, rpb, lam, lam_init, d_gain):
    qc, kc, vc, qd, kd, vd = odd_project(h, w_in)
    oc = neighbourhood_attend(qc, kc, vc, rpb, k_ctx_c, v_ctx_c)
    cos, sin = axial_rope_tables(h.shape[1], HD)
    qd = pair_heads(apply_rope(qd, cos, sin))
    kd = pair_heads(apply_rope(kd, cos, sin))
    k_all = jnp.concatenate([k_ctx_d, kd], axis=3)
    v_all = jnp.concatenate([v_ctx_d, vd], axis=2)
    od = diff_attend(qd, k_all, v_all, lam)
    return odd_merge(oc, od, lam_init, d_gain, w_out)


def setup_inputs(seed: int = 0) -> dict:
    key = jax.random.key(seed)
    ks = iter(jax.random.split(key, 48))

    def nrm(shape, scale=1.0):
        return scale * jax.random.normal(next(ks), shape, F32)

    def gain(shape):
        return 1.0 + nrm(shape, 0.02)

    inputs = {}
    inputs['x_prompt'] = nrm((BATCH, SEQ, D_MODEL))
    inputs['x_sample'] = nrm((DEC_BATCH, DEC_SEQ, D_MODEL))
    inputs['c'] = nrm((DEC_BATCH, D_MODEL))
    inputs['cache_a_k'] = nrm((DEC_BATCH, N_EVEN, A_KV, PAST_LEN, HD))
    inputs['cache_a_v'] = nrm((DEC_BATCH, N_EVEN, A_KV, PAST_LEN, HD))
    inputs['state_b_C'] = nrm((DEC_BATCH, N_EVEN, 2, B_HEADS, B_DK, B_DV), 0.3)
    inputs['state_b_n'] = nrm((DEC_BATCH, N_EVEN, 2, B_HEADS, B_DK))
    inputs['state_b_m'] = nrm((DEC_BATCH, N_EVEN, 2, B_HEADS), 0.5)
    inputs['cache_c_k'] = nrm((DEC_BATCH, N_ODD, C_HEADS, PAST_LEN, HD))
    inputs['cache_c_v'] = nrm((DEC_BATCH, N_ODD, C_HEADS, PAST_LEN, HD))
    inputs['cache_d_k'] = nrm((DEC_BATCH, N_ODD, D_HEADS, 2, PAST_LEN, HD))
    inputs['cache_d_v'] = nrm((DEC_BATCH, N_ODD, D_HEADS, PAST_LEN, D_VDIM))
    inputs['c_ctx'] = nrm((D_MODEL,))
    inputs['w_mod'] = nrm((DEPTH, D_MODEL, 6 * D_MODEL), 0.5 * D_MODEL ** -0.5)
    inputs['b_mod'] = nrm((DEPTH, 6 * D_MODEL), 0.01)
    inputs['norm1_g'] = gain((DEPTH, D_MODEL))
    inputs['norm2_g'] = gain((DEPTH, D_MODEL))
    inputs['w_in_even'] = nrm((N_EVEN, D_MODEL, EVEN_IN), D_MODEL ** -0.5)
    inputs['w_out_even'] = nrm((N_EVEN, EVEN_OUT, D_MODEL), EVEN_OUT ** -0.5)
    inputs['a_q_gain'] = gain((N_EVEN, HD))
    inputs['a_k_gain'] = gain((N_EVEN, HD))
    inputs['b_gate_bias'] = nrm((N_EVEN, 4, B_HEADS), 0.1).at[:, 1::2].add(jnp.linspace(3.0, 6.0, B_HEADS))
    inputs['b_norm_gain'] = gain((N_EVEN, B_HEADS * B_DV))
    inputs['w_in_odd'] = nrm((N_ODD, D_MODEL, ODD_IN), D_MODEL ** -0.5)
    inputs['w_out_odd'] = nrm((N_ODD, ODD_OUT, D_MODEL), ODD_OUT ** -0.5)
    inputs['c_rpb'] = nrm((N_ODD, C_HEADS, 2 * NA_ROWS - 1, 2 * NA_COLS - 1), 0.1)
    inputs['d_lambda'] = nrm((N_ODD, 4, HD), 0.1)
    inputs['d_norm_gain'] = gain((N_ODD, D_VDIM))
    inputs['router_w'] = nrm((DEPTH, D_MODEL, N_EXPERTS), D_MODEL ** -0.5)
    inputs['router_b'] = nrm((DEPTH, N_EXPERTS), 0.01)
    inputs['expert_w_gu'] = nrm((DEPTH, N_EXPERTS, D_MODEL, 2 * D_FF), D_MODEL ** -0.5)
    inputs['expert_b_gu'] = nrm((DEPTH, N_EXPERTS, 2 * D_FF), 0.01)
    inputs['expert_w_down'] = nrm((DEPTH, N_EXPERTS, D_FF, D_MODEL), D_FF ** -0.5)
    inputs['expert_b_down'] = nrm((DEPTH, N_EXPERTS, D_MODEL), 0.01)
    inputs['final_norm_g'] = gain((D_MODEL,))
    return inputs


def reference(x_prompt, x_sample, c, cache_a_k, cache_a_v, state_b_C, state_b_n, state_b_m,
              cache_c_k, cache_c_v, cache_d_k, cache_d_v, c_ctx,
              w_mod, b_mod, norm1_g, norm2_g,
              w_in_even, w_out_even, a_q_gain, a_k_gain, b_gate_bias, b_norm_gain,
              w_in_odd, w_out_odd, c_rpb, d_lambda, d_norm_gain,
              router_w, router_b, expert_w_gu, expert_b_gu, expert_w_down, expert_b_down,
              final_norm_g):
    y = x_prompt
    new_a_k, new_a_v, new_b_C, new_b_n, new_b_m = [], [], [], [], []
    new_c_k, new_c_v, new_d_k, new_d_v = [], [], [], []
    for i in range(DEPTH):
        sh1, sc1, g1, sh2, sc2, g2 = modulation(c_ctx[None], w_mod[i], b_mod[i])
        h = rms_norm(y, norm1_g[i]) * (1.0 + sc1) + sh1
        j = i // 2
        if i % 2 == 0:
            out, ak, av, bC, bn, bm = even_context(h, w_in_even[j], w_out_even[j], a_q_gain[j], a_k_gain[j],
                                                   b_gate_bias[j], b_norm_gain[j])
            new_a_k.append(ak)
            new_a_v.append(av)
            new_b_C.append(bC)
            new_b_n.append(bn)
            new_b_m.append(bm)
        else:
            lam, lam_init = diff_lambda(d_lambda[j], i)
            out, ck, cv, dk, dv = odd_context(h, w_in_odd[j], w_out_odd[j], lam, lam_init, d_norm_gain[j])
            new_c_k.append(ck)
            new_c_v.append(cv)
            new_d_k.append(dk)
            new_d_v.append(dv)
        y = y + g1 * out
        h = rms_norm(y, norm2_g[i]) * (1.0 + sc2) + sh2
        y = y + g2 * moe_ffn(h, router_w[i], router_b[i], expert_w_gu[i], expert_b_gu[i],
                             expert_w_down[i], expert_b_down[i])
    y_prompt = rms_norm(y, final_norm_g)

    y = x_sample
    for i in range(DEPTH):
        sh1, sc1, g1, sh2, sc2, g2 = modulation(c, w_mod[i], b_mod[i])
        h = rms_norm(y, norm1_g[i]) * (1.0 + sc1) + sh1
        j = i // 2
        if i % 2 == 0:
            out = even_latent(h, cache_a_k[:, j], cache_a_v[:, j], state_b_C[:, j], state_b_n[:, j],
                              state_b_m[:, j], w_in_even[j], w_out_even[j], a_q_gain[j], a_k_gain[j],
                              b_gate_bias[j], b_norm_gain[j])
        else:
            lam, lam_init = diff_lambda(d_lambda[j], i)
            out = odd_latent(h, cache_c_k[:, j], cache_c_v[:, j], cache_d_k[:, j], cache_d_v[:, j],
                             w_in_odd[j], w_out_odd[j], c_rpb[j], lam, lam_init, d_norm_gain[j])
        y = y + g1 * out
        h = rms_norm(y, norm2_g[i]) * (1.0 + sc2) + sh2
        y = y + g2 * moe_ffn(h, router_w[i], router_b[i], expert_w_gu[i], expert_b_gu[i],
                             expert_w_down[i], expert_b_down[i])
    y_sample = rms_norm(y, final_norm_g)

    return (y_prompt, y_sample,
            jnp.stack(new_a_k, axis=1), jnp.stack(new_a_v, axis=1),
            jnp.stack(new_b_C, axis=1), jnp.stack(new_b_n, axis=1), jnp.stack(new_b_m, axis=1),
            jnp.stack(new_c_k, axis=1), jnp.stack(new_c_v, axis=1),
            jnp.stack(new_d_k, axis=1), jnp.stack(new_d_v, axis=1))
```

```python
import functools
import math

import numpy as np
import jax
import jax.numpy as jnp
from jax import lax
from jax.experimental import pallas as pl
from jax.experimental.pallas import tpu as pltpu

D_MODEL = 1024
BATCH = 32
SEQ = 256
DEPTH = 2
DEC_BATCH = 8
DEC_SEQ = 1024
PAST_LEN = 512
GRID_W = 64
HD = 64
A_HEADS = 8
A_KV = 2
B_HEADS = 4
B_DK = 128
B_DV = 128
B_CHUNK = 128
C_HEADS = 8
NA_ROWS = 8
NA_COLS = 16
D_HEADS = 4
D_VDIM = 2 * HD
N_EXPERTS = 32
TOP_K = 4
D_FF = 1024
SWIGLU_LIMIT = 7.0
SWIGLU_ALPHA = 1.702
ROPE_THETA = 10000.0
EPS = 1e-6

F32 = jnp.float32
BF16 = jnp.bfloat16
HIGHEST = lax.Precision.HIGHEST

T_CTX = BATCH * SEQ
T_LAT = DEC_BATCH * DEC_SEQ
T_ALL = T_CTX + T_LAT
TM = 256
CTX_TILES = T_CTX // TM
LAT_TILES_PER_BATCH = DEC_SEQ // TM
N_TILES = T_ALL // TM
MOD_ROWS = 16
LANES = 128
NEG_BIG = -1e30
MOE_TM = 256
N_ASSIGN = T_ALL * TOP_K
MOE_ROWS = N_ASSIGN + N_EXPERTS * MOE_TM
MOE_TILES = MOE_ROWS // MOE_TM
VMEM_LIMIT = 56 * 1024 * 1024

EV_AQ, EV_BO, EV_BQ, EV_BK, EV_BV, EV_AK, EV_AV, EV_BG = 0, 512, 1024, 1536, 2048, 2560, 2688, 2816
EV_N = 2944
OD_N = 3072


def _params(sem, vmem=VMEM_LIMIT):
    return pltpu.CompilerParams(dimension_semantics=sem, vmem_limit_bytes=vmem)


def _mod_row(i):
    return jnp.where(i < CTX_TILES, 0, 1 + (i - CTX_TILES) // LAT_TILES_PER_BATCH)


def _rope_block(i):
    return jnp.where(i < CTX_TILES, LAT_TILES_PER_BATCH, (i - CTX_TILES) % LAT_TILES_PER_BATCH)


def _dot(a, b, precision=None):
    return jnp.dot(a, b, preferred_element_type=F32, precision=precision)


def _dot_nt(a, b):
    return lax.dot_general(a, b, (((1,), (1,)), ((), ())), preferred_element_type=F32)


def _dot_tn(a, b):
    return lax.dot_general(a, b, (((0,), (0,)), ((), ())), preferred_element_type=F32)


def _modulation_kernel(c_ref, w_ref, b_ref, o_ref):
    c = c_ref[...]
    s = c * jax.nn.sigmoid(c)
    o_ref[...] = _dot(s, w_ref[...], HIGHEST) + b_ref[...]


def _modulation(cond, w_mod, b_mod):
    tn = 1536
    return pl.pallas_call(
        _modulation_kernel,
        out_shape=jax.ShapeDtypeStruct((DEPTH, MOD_ROWS, 6 * D_MODEL), F32),
        grid=(DEPTH, 6 * D_MODEL // tn),
        in_specs=[
            pl.BlockSpec((MOD_ROWS, D_MODEL), lambda l, j: (0, 0)),
            pl.BlockSpec((None, D_MODEL, tn), lambda l, j: (l, 0, j)),
            pl.BlockSpec((None, 1, tn), lambda l, j: (l, 0, j)),
        ],
        out_specs=pl.BlockSpec((None, MOD_ROWS, tn), lambda l, j: (l, 0, j)),
        compiler_params=_params(("parallel", "parallel")),
        name="modulation",
    )(cond, w_mod, b_mod.reshape(DEPTH, 1, 6 * D_MODEL))


def _norm_mod(y, g, sc, sh):
    ms = jnp.mean(y * y, axis=-1, keepdims=True)
    return (y * lax.rsqrt(ms + EPS) * g) * (1.0 + sc) + sh


def _norm_proj_kernel(y_ref, g_ref, sc_ref, sh_ref, w_ref, o_ref):
    h = _norm_mod(y_ref[...], g_ref[...], sc_ref[...], sh_ref[...])
    o_ref[...] = _dot(h.astype(BF16), w_ref[...])


def _norm_proj(y, gain, scale, shift, w_bf16):
    n = w_bf16.shape[1]
    vec = pl.BlockSpec((None, 1, D_MODEL), lambda i: (_mod_row(i), 0, 0))
    return pl.pallas_call(
        _norm_proj_kernel,
        out_shape=jax.ShapeDtypeStruct((T_ALL, n), F32),
        grid=(N_TILES,),
        in_specs=[
            pl.BlockSpec((TM, D_MODEL), lambda i: (i, 0)),
            pl.BlockSpec((1, D_MODEL), lambda i: (0, 0)),
            vec, vec,
            pl.BlockSpec((D_MODEL, n), lambda i: (0, 0)),
        ],
        out_specs=pl.BlockSpec((TM, n), lambda i: (i, 0)),
        compiler_params=_params(("parallel",)),
        name="norm_proj",
    )(y, gain.reshape(1, D_MODEL), scale, shift, w_bf16)


def _rope_rotate(x):
    w = x.shape[-1]
    lane = lax.broadcasted_iota(jnp.int32, x.shape, 1)
    nxt = pltpu.roll(x, w - 1, 1)
    prv = pltpu.roll(x, 1, 1)
    return jnp.where((lane & 1) == 0, -nxt, prv)


def _prep_kernel(*refs, specs, n_gain):
    p_ref, cos_ref, sin_ref, bd_ref = refs[:4]
    gain_refs = refs[4:4 + n_gain]
    out_refs = refs[4 + n_gain:]
    cos = cos_ref[...]
    sin = sin_ref[...]
    for (col, width, gi, rope, scale, _), o_ref in zip(specs, out_refs):
        for c0 in range(0, width, LANES):
            x = p_ref[:, col + c0:col + c0 + LANES]
            if gi is not None:
                ss = _dot(x * x, bd_ref[...], HIGHEST)
                x = x * lax.rsqrt(ss * (1.0 / HD) + EPS) * gain_refs[gi][...]
            if rope:
                x = x * cos + _rope_rotate(x) * sin
            if scale != 1.0:
                x = x * scale
            o_ref[:, c0:c0 + LANES] = x.astype(o_ref.dtype)


def _prep(p, rope_cos, rope_sin, gains, specs):
    n = p.shape[1]
    bd = jnp.asarray(np.kron(np.eye(LANES // HD), np.ones((HD, HD))), F32)
    rope_spec = pl.BlockSpec((TM, LANES), lambda i: (_rope_block(i), 0))
    in_specs = [pl.BlockSpec((TM, n), lambda i: (i, 0)), rope_spec, rope_spec,
                pl.BlockSpec((LANES, LANES), lambda i: (0, 0))]
    in_specs += [pl.BlockSpec((1, LANES), lambda i: (0, 0)) for _ in gains]
    return pl.pallas_call(
        functools.partial(_prep_kernel, specs=specs, n_gain=len(gains)),
        out_shape=[jax.ShapeDtypeStruct((T_ALL, s[1]), s[5]) for s in specs],
        grid=(N_TILES,),
        in_specs=in_specs,
        out_specs=[pl.BlockSpec((TM, s[1]), lambda i: (i, 0)) for s in specs],
        compiler_params=_params(("parallel",)),
        name="qkv_prep",
    )(p, rope_cos, rope_sin, bd, *gains)


def _rope_tables():
    half = HD // 2
    freqs = 1.0 / (ROPE_THETA ** (jnp.arange(0, half, 2, dtype=F32) / half))
    t = jnp.arange(DEC_SEQ)
    rows = (t // GRID_W).astype(F32)
    cols = (t % GRID_W).astype(F32)
    ang = jnp.concatenate([rows[:, None] * freqs, cols[:, None] * freqs], axis=-1)
    cos = jnp.repeat(jnp.cos(ang), 2, axis=-1)
    sin = jnp.repeat(jnp.sin(ang), 2, axis=-1)
    cos = jnp.concatenate([jnp.tile(cos, (1, LANES // HD)), jnp.ones((TM, LANES), F32)], axis=0)
    sin = jnp.concatenate([jnp.tile(sin, (1, LANES // HD)), jnp.zeros((TM, LANES), F32)], axis=0)
    return cos, sin


def _lane_slice(ref, h, width=HD):
    per = LANES // width
    blk = ref[:, (h // per) * LANES:(h // per + 1) * LANES]
    if per == 1:
        return blk
    return blk[:, (h % per) * width:(h % per + 1) * width]


def _softmax_parts(scores):
    m = None
    for s in scores:
        ms = jnp.max(s, axis=-1, keepdims=True)
        m = ms if m is None else jnp.maximum(m, ms)
    ps = [jnp.exp(s - m) for s in scores]
    l = None
    for p in ps:
        ls = jnp.sum(p, axis=-1, keepdims=True)
        l = ls if l is None else l + ls
    return ps, l


def _attn_std_kernel(*refs, group, n_kv, has_cache, has_bias, bq):
    it = iter(refs)
    q_ref, kn_ref, vn_ref = next(it), next(it), next(it)
    kc_ref = vc_ref = b_ref = None
    if has_cache:
        kc_ref, vc_ref = next(it), next(it)
    if has_bias:
        b_ref = next(it)
    o_ref = next(it)
    outs = []
    for g in range(n_kv):
        qs = jnp.concatenate([_lane_slice(q_ref, g * group + j) for j in range(group)], axis=0)
        kn = _lane_slice(kn_ref, g)
        vn = _lane_slice(vn_ref, g)
        s_new = _dot_nt(qs, kn)
        if has_bias:
            s_new = s_new + b_ref[g]
        scores = [s_new]
        if has_cache:
            scores.append(_dot_nt(qs, kc_ref[g]))
        ps, l = _softmax_parts(scores)
        o = _dot(ps[0].astype(BF16), vn)
        if has_cache:
            o = o + _dot(ps[1].astype(BF16), vc_ref[g])
        o = o / l
        for j in range(group):
            outs.append(o[j * bq:(j + 1) * bq])
    o_ref[...] = jnp.concatenate(outs, axis=1).astype(o_ref.dtype)


def _attn_diff_kernel(*refs, has_cache):
    it = iter(refs)
    lam_ref, q_ref, kn_ref, vn_ref = next(it), next(it), next(it), next(it)
    kc_ref = vc_ref = None
    if has_cache:
        kc_ref, vc_ref = next(it), next(it)
    g_ref, o_ref = next(it), next(it)
    lam = lam_ref[0]
    post = lam_ref[1]
    outs = []
    for h in range(D_HEADS):
        pd_new, pd_c = None, None
        for j in range(2):
            f = 2 * h + j
            qs = _lane_slice(q_ref, f)
            scores = [_dot_nt(qs, _lane_slice(kn_ref, f))]
            if has_cache:
                scores.append(_dot_nt(qs, kc_ref[f]))
            ps, l = _softmax_parts(scores)
            r = 1.0 / l
            if j == 0:
                pd_new = ps[0] * r
                pd_c = ps[1] * r if has_cache else None
            else:
                r = r * lam
                pd_new = pd_new - ps[0] * r
                pd_c = pd_c - ps[1] * r if has_cache else None
        o = _dot(pd_new.astype(BF16), _lane_slice(vn_ref, h, D_VDIM))
        if has_cache:
            o = o + _dot(pd_c.astype(BF16), vc_ref[h])
        ms = jnp.mean(o * o, axis=-1, keepdims=True)
        outs.append(o * lax.rsqrt(ms + EPS) * g_ref[...] * post)
    o_ref[...] = jnp.concatenate(outs, axis=1).astype(o_ref.dtype)


def _attention(q, kn, vn, *, ctx, group=1, n_kv=1, cache=None, bias=None, diff=None, bq=256):
    if ctx:
        nb, sq, row0 = BATCH, SEQ, 0
    else:
        nb, sq, row0 = DEC_BATCH, DEC_SEQ, T_CTX
    nq = sq // bq
    qb0 = row0 // bq
    kb0 = row0 // sq
    wq, wk, wv = q.shape[1], kn.shape[1], vn.shape[1]
    in_specs = [
        pl.BlockSpec((bq, wq), lambda b, i: (qb0 + b * nq + i, 0)),
        pl.BlockSpec((sq, wk), lambda b, i: (kb0 + b, 0)),
        pl.BlockSpec((sq, wv), lambda b, i: (kb0 + b, 0)),
    ]
    args = [q, kn, vn]
    if cache is not None:
        kc, vc = cache
        in_specs += [pl.BlockSpec((None,) + kc.shape[1:], lambda b, i: (b, 0, 0, 0)),
                     pl.BlockSpec((None,) + vc.shape[1:], lambda b, i: (b, 0, 0, 0))]
        args += [kc, vc]
    if diff is None:
        if bias is not None:
            in_specs.append(pl.BlockSpec((bias.shape[0], bq, sq), lambda b, i: (0, i, 0)))
            args.append(bias)
        body = functools.partial(_attn_std_kernel, group=group, n_kv=n_kv, has_cache=cache is not None,
                                 has_bias=bias is not None, bq=bq)
    else:
        lam_vec, gain = diff
        in_specs = [pl.BlockSpec(memory_space=pltpu.SMEM)] + in_specs
        args = [lam_vec] + args
        in_specs.append(pl.BlockSpec((1, D_VDIM), lambda b, i: (0, 0)))
        args.append(gain.reshape(1, D_VDIM))
        body = functools.partial(_attn_diff_kernel, has_cache=cache is not None)
    return pl.pallas_call(
        body,
        out_shape=jax.ShapeDtypeStruct((nb * sq, 512), BF16),
        grid=(nb, nq),
        in_specs=in_specs,
        out_specs=pl.BlockSpec((bq, 512), lambda b, i: (b * nq + i, 0)),
        compiler_params=_params(("parallel", "parallel")),
        name="attention",
    )(*args)


def _neighbourhood_bias(rpb):
    rows = DEC_SEQ // GRID_W
    wr = min(NA_ROWS, rows)
    t = jnp.arange(DEC_SEQ)
    r, c = t // GRID_W, t % GRID_W
    r0 = jnp.clip(r - wr // 2, 0, rows - wr)
    c0 = jnp.clip(c - NA_COLS // 2, 0, GRID_W - NA_COLS)
    row_ok = (r[None, :] >= r0[:, None]) & (r[None, :] < r0[:, None] + wr)
    col_ok = (c[None, :] >= c0[:, None]) & (c[None, :] < c0[:, None] + NA_COLS)
    dr = jnp.clip(r[None, :] - r[:, None] + (NA_ROWS - 1), 0, 2 * NA_ROWS - 2)
    dc = jnp.clip(c[None, :] - c[:, None], 1 - NA_COLS, NA_COLS - 1) + (NA_COLS - 1)
    return jnp.where((row_ok & col_ok)[None], rpb[:, dr, dc], NEG_BIG).astype(F32)


def _log_sigmoid(x):
    return jnp.minimum(x, 0.0) - jnp.log1p(jnp.exp(-jnp.abs(x)))


def _mlstm_kernel(q_ref, k_ref, v_ref, g_ref, gb_ref, c0_ref, n0_ref, m0_ref, h_ref, c_ref, n_ref, m_ref, *, seq):
    L = B_CHUNK
    nc = seq // L
    row = lax.broadcasted_iota(jnp.int32, (L, L), 0)
    col = lax.broadcasted_iota(jnp.int32, (L, L), 1)
    k_scale = B_DK ** -0.5
    for d in range(2):
        keep = (col <= row) if d == 0 else (col >= row)
        tri = keep.astype(F32)
        for h in range(B_HEADS):
            ci = (2 * d) * B_HEADS + h
            cf = (2 * d + 1) * B_HEADS + h
            hs = slice(h * B_DK, (h + 1) * B_DK)

            def chunk(j, carry, d=d, keep=keep, tri=tri, ci=ci, cf=cf, hs=hs):
                C, n, m = carry
                c = j if d == 0 else nc - 1 - j
                off = pl.multiple_of(c * L, L)
                qc = q_ref[pl.ds(off, L), hs]
                kc = k_ref[pl.ds(off, L), hs] * k_scale
                vc = v_ref[pl.ds(off, L), hs]
                gates = g_ref[pl.ds(off, L), :] + gb_ref[...]
                cum = _dot(tri, _log_sigmoid(gates), HIGHEST)
                b_col = cum[:, cf:cf + 1]
                i_col = gates[:, ci:ci + 1]
                b_row = cum.T[cf:cf + 1, :]
                i_row = gates.T[ci:ci + 1, :]
                dlog = jnp.where(keep, b_col - b_row + i_row, -jnp.inf)
                inter = b_col + m
                m_t = jnp.maximum(inter, jnp.max(dlog, axis=-1, keepdims=True))
                w_intra = jnp.exp(dlog - m_t)
                w_inter = jnp.exp(inter - m_t)
                qb = qc.astype(BF16)
                qk = _dot_nt(qb, kc.astype(BF16)) * w_intra
                num = _dot(qk.astype(BF16), vc.astype(BF16)) + w_inter * _dot(qb, C.astype(BF16))
                den = jnp.sum(qk, axis=-1, keepdims=True) + w_inter * jnp.sum(qc * n, axis=-1, keepdims=True)
                h_out = num / jnp.maximum(jnp.abs(den), jnp.exp(-m_t))
                if d == 0:
                    h_ref[pl.ds(off, L), hs] = h_out
                    b_last = b_col[L - 1:L, :]
                else:
                    h_ref[pl.ds(off, L), hs] = h_ref[pl.ds(off, L), hs] + h_out
                    b_last = b_col[0:1, :]
                end_col = b_last - b_col + i_col
                m_new = jnp.maximum(b_last + m, jnp.max(end_col, axis=0, keepdims=True))
                w_end = jnp.exp(end_col - m_new)
                decay = jnp.exp(b_last + m - m_new)
                kw = kc * w_end
                C_new = decay * C + _dot_tn(kw.astype(BF16), vc.astype(BF16))
                n_new = decay * n + jnp.sum(kw, axis=0, keepdims=True)
                return C_new, n_new, m_new

            init = (c0_ref[d, h], n0_ref[d, h], m0_ref[d, h][:, 0:1])
            C, n, m = lax.fori_loop(0, nc, chunk, init)
            c_ref[d, h] = C
            n_ref[d, h] = n
            m_ref[d, h] = jnp.broadcast_to(m, (1, LANES))


def _mlstm(p, gate_bias, c0, n0, m0, *, ctx):
    if ctx:
        nb, seq, blk0 = BATCH, SEQ, 0
    else:
        nb, seq, blk0 = DEC_BATCH, DEC_SEQ, T_CTX // DEC_SEQ
    w = B_HEADS * B_DK

    def cols(c0_, width):
        return pl.BlockSpec((seq, width), lambda b: (blk0 + b, c0_ // width))

    gb = jnp.zeros((1, LANES), F32).at[0, :4 * B_HEADS].set(gate_bias.reshape(-1).astype(F32))
    st = lambda shape: pl.BlockSpec((None,) + shape, lambda b: (b, 0, 0, 0, 0))
    return pl.pallas_call(
        functools.partial(_mlstm_kernel, seq=seq),
        out_shape=[
            jax.ShapeDtypeStruct((nb * seq, w), F32),
            jax.ShapeDtypeStruct((nb, 2, B_HEADS, B_DK, B_DV), F32),
            jax.ShapeDtypeStruct((nb, 2, B_HEADS, 1, B_DK), F32),
            jax.ShapeDtypeStruct((nb, 2, B_HEADS, 1, LANES), F32),
        ],
        grid=(nb,),
        in_specs=[
            cols(EV_BQ, w), cols(EV_BK, w), cols(EV_BV, w), cols(EV_BG, LANES),
            pl.BlockSpec((1, LANES), lambda b: (0, 0)),
            st((2, B_HEADS, B_DK, B_DV)), st((2, B_HEADS, 1, B_DK)), st((2, B_HEADS, 1, LANES)),
        ],
        out_specs=[
            pl.BlockSpec((seq, w), lambda b: (b, 0)),
            st((2, B_HEADS, B_DK, B_DV)), st((2, B_HEADS, 1, B_DK)), st((2, B_HEADS, 1, LANES)),
        ],
        compiler_params=_params(("parallel",)),
        name="mlstm",
    )(p, p, p, p, gb, c0, n0, m0)


def _merge_kernel(*refs, even):
    if even:
        a_ref, hb_ref, bo_ref, ng_ref, w_ref, y_ref, g_ref, o_ref = refs
        parts = [a_ref[...]]
        for h in range(B_HEADS):
            hs = slice(h * B_DV, (h + 1) * B_DV)
            x = hb_ref[:, hs]
            ms = jnp.mean(x * x, axis=-1, keepdims=True)
            xn = x * lax.rsqrt(ms + EPS) * ng_ref[:, hs]
            parts.append((jax.nn.sigmoid(bo_ref[:, hs]) * xn).astype(BF16))
    else:
        a_ref, b_ref, w_ref, y_ref, g_ref, o_ref = refs
        parts = [a_ref[...], b_ref[...]]
    cat = jnp.concatenate(parts, axis=1)
    o_ref[...] = y_ref[...] + g_ref[...] * _dot(cat, w_ref[...])


def _merge(y, gate, w_bf16, a, b, *, p=None, norm_gain=None):
    even = p is not None
    half = pl.BlockSpec((TM, 512), lambda i: (i, 0))
    in_specs = [half, half]
    args = [a, b]
    if even:
        in_specs += [pl.BlockSpec((TM, 512), lambda i: (i, EV_BO // 512)), pl.BlockSpec((1, 512), lambda i: (0, 0))]
        args += [p, norm_gain.reshape(1, 512)]
    in_specs += [
        pl.BlockSpec((D_MODEL, D_MODEL), lambda i: (0, 0)),
        pl.BlockSpec((TM, D_MODEL), lambda i: (i, 0)),
        pl.BlockSpec((None, 1, D_MODEL), lambda i: (_mod_row(i), 0, 0)),
    ]
    args += [w_bf16, y, gate]
    return pl.pallas_call(
        functools.partial(_merge_kernel, even=even),
        out_shape=jax.ShapeDtypeStruct((T_ALL, D_MODEL), F32),
        grid=(N_TILES,),
        in_specs=in_specs,
        out_specs=pl.BlockSpec((TM, D_MODEL), lambda i: (i, 0)),
        compiler_params=_params(("parallel",)),
        name="merge",
    )(*args)


def _router_kernel(y_ref, g_ref, sc_ref, sh_ref, w_ref, b_ref, h_ref, ti_ref, tp_ref):
    h = _norm_mod(y_ref[...], g_ref[...], sc_ref[...], sh_ref[...])
    h_ref[...] = h
    logits = _dot(h, w_ref[...], HIGHEST) + b_ref[...]
    lane = lax.broadcasted_iota(jnp.int32, logits.shape, 1)
    vals, idxs = [], []
    for _ in range(TOP_K):
        mx = jnp.max(logits, axis=-1, keepdims=True)
        ix = jnp.min(jnp.where(logits == mx, lane, LANES), axis=-1, keepdims=True)
        vals.append(mx)
        idxs.append(ix)
        logits = jnp.where(lane == ix, -jnp.inf, logits)
    es = [jnp.exp(v - vals[0]) for v in vals]
    tot = es[0] + es[1] + es[2] + es[3]
    ti = jnp.zeros(logits.shape, jnp.int32)
    tp = jnp.zeros(logits.shape, F32)
    for k in range(TOP_K):
        ti = jnp.where(lane == k, idxs[k], ti)
        tp = jnp.where(lane == k, es[k] / tot, tp)
    ti_ref[...] = ti
    tp_ref[...] = tp


def _router(y, gain, scale, shift, rw, rb):
    vec = pl.BlockSpec((None, 1, D_MODEL), lambda i: (_mod_row(i), 0, 0))
    rw_p = jnp.zeros((D_MODEL, LANES), F32).at[:, :N_EXPERTS].set(rw)
    rb_p = jnp.full((1, LANES), NEG_BIG, F32).at[0, :N_EXPERTS].set(rb)
    tile = lambda w: pl.BlockSpec((TM, w), lambda i: (i, 0))
    return pl.pallas_call(
        _router_kernel,
        out_shape=[jax.ShapeDtypeStruct((T_ALL, D_MODEL), F32),
                   jax.ShapeDtypeStruct((T_ALL, LANES), jnp.int32),
                   jax.ShapeDtypeStruct((T_ALL, LANES), F32)],
        grid=(N_TILES,),
        in_specs=[tile(D_MODEL), pl.BlockSpec((1, D_MODEL), lambda i: (0, 0)), vec, vec,
                  pl.BlockSpec((D_MODEL, LANES), lambda i: (0, 0)), pl.BlockSpec((1, LANES), lambda i: (0, 0))],
        out_specs=[tile(D_MODEL), tile(LANES), tile(LANES)],
        compiler_params=_params(("parallel",)),
        name="router",
    )(y, gain.reshape(1, D_MODEL), scale, shift, rw_p, rb_p)


def _route_plan(top_i):
    e = top_i.reshape(-1)
    onehot = (e[:, None] == jnp.arange(N_EXPERTS)[None, :]).astype(jnp.int32)
    csum = jnp.cumsum(onehot, axis=0)
    counts = csum[-1]
    rank = jnp.take_along_axis(csum, e[:, None], axis=1)[:, 0] - 1
    padded = ((counts + MOE_TM - 1) // MOE_TM) * MOE_TM
    seg_end = jnp.cumsum(padded)
    seg_start = seg_end - padded
    pos = seg_start[e] + rank
    slot_token = jnp.zeros((MOE_ROWS,), jnp.int32).at[pos].set(jnp.arange(N_ASSIGN, dtype=jnp.int32) // TOP_K)
    n_active = seg_end[-1] // MOE_TM
    tile_start = jnp.arange(MOE_TILES, dtype=jnp.int32) * MOE_TM
    tile_expert = jnp.minimum(jnp.searchsorted(seg_end, tile_start, side="right"), N_EXPERTS - 1).astype(jnp.int32)
    last = tile_expert[jnp.maximum(n_active - 1, 0)]
    tile_expert = jnp.where(jnp.arange(MOE_TILES) < n_active, tile_expert, last)
    return pos.reshape(T_ALL, TOP_K), slot_token, tile_expert, n_active.reshape(1).astype(jnp.int32)


def _moe_kernel(te_ref, na_ref, x_ref, wgu_ref, bgu_ref, wd_ref, bd_ref, o_ref, wgu_bf, wd_bf):
    i = pl.program_id(0)
    e = te_ref[i]
    prev = te_ref[jnp.maximum(i - 1, 0)]

    @pl.when((i == 0) | (e != prev))
    def _():
        wgu_bf[...] = wgu_ref[...].astype(BF16)
        wd_bf[...] = wd_ref[...].astype(BF16)

    @pl.when(i < na_ref[0])
    def _():
        gu = _dot(x_ref[...], wgu_bf[...]) + bgu_ref[...]
        gate = jnp.minimum(gu[:, :D_FF], SWIGLU_LIMIT)
        up = jnp.clip(gu[:, D_FF:], -SWIGLU_LIMIT, SWIGLU_LIMIT)
        act = (up + 1.0) * gate * jax.nn.sigmoid(SWIGLU_ALPHA * gate)
        o_ref[...] = _dot(act.astype(BF16), wd_bf[...]) + bd_ref[...]

    @pl.when(i >= na_ref[0])
    def _():
        o_ref[...] = jnp.zeros(o_ref.shape, o_ref.dtype)


def _moe_experts(layer, xs, tile_expert, n_active, w_gu, b_gu, w_down, b_down):
    grid_spec = pltpu.PrefetchScalarGridSpec(
        num_scalar_prefetch=2,
        grid=(MOE_TILES,),
        in_specs=[
            pl.BlockSpec((MOE_TM, D_MODEL), lambda i, te, na: (i, 0)),
            pl.BlockSpec((None, None, D_MODEL, 2 * D_FF), lambda i, te, na: (layer, te[i], 0, 0)),
            pl.BlockSpec((None, None, 1, 2 * D_FF), lambda i, te, na: (layer, te[i], 0, 0)),
            pl.BlockSpec((None, None, D_FF, D_MODEL), lambda i, te, na: (layer, te[i], 0, 0)),
            pl.BlockSpec((None, None, 1, D_MODEL), lambda i, te, na: (layer, te[i], 0, 0)),
        ],
        out_specs=pl.BlockSpec((MOE_TM, D_MODEL), lambda i, te, na: (i, 0)),
        scratch_shapes=[pltpu.VMEM((D_MODEL, 2 * D_FF), BF16), pltpu.VMEM((D_FF, D_MODEL), BF16)],
    )
    return pl.pallas_call(
        _moe_kernel,
        out_shape=jax.ShapeDtypeStruct((MOE_ROWS, D_MODEL), F32),
        grid_spec=grid_spec,
        compiler_params=_params(("arbitrary",)),
        name="moe_experts",
    )(tile_expert, n_active, xs, w_gu, b_gu.reshape(DEPTH, N_EXPERTS, 1, 2 * D_FF), w_down,
      b_down.reshape(DEPTH, N_EXPERTS, 1, D_MODEL))


def _residual_kernel(*refs, final):
    if final:
        y_ref, x_ref, g_ref, fg_ref, o_ref, n_ref = refs
    else:
        y_ref, x_ref, g_ref, o_ref = refs
    y = y_ref[...] + g_ref[...] * x_ref[...]
    o_ref[...] = y
    if final:
        ms = jnp.mean(y * y, axis=-1, keepdims=True)
        n_ref[...] = y * lax.rsqrt(ms + EPS) * fg_ref[...]


def _residual(y, x, gate, final_gain=None):
    final = final_gain is not None
    tile = pl.BlockSpec((TM, D_MODEL), lambda i: (i, 0))
    in_specs = [tile, tile, pl.BlockSpec((None, 1, D_MODEL), lambda i: (_mod_row(i), 0, 0))]
    args = [y, x, gate]
    out_shape = [jax.ShapeDtypeStruct((T_ALL, D_MODEL), F32)]
    out_specs = [tile]
    if final:
        in_specs.append(pl.BlockSpec((1, D_MODEL), lambda i: (0, 0)))
        args.append(final_gain.reshape(1, D_MODEL))
        out_shape.append(jax.ShapeDtypeStruct((T_ALL, D_MODEL), F32))
        out_specs.append(tile)
    return pl.pallas_call(
        functools.partial(_residual_kernel, final=final),
        out_shape=out_shape,
        grid=(N_TILES,),
        in_specs=in_specs,
        out_specs=out_specs,
        compiler_params=_params(("parallel",)),
        name="residual",
    )(*args)


def _moe_layer(layer, y, gain, scale, shift, gate, rw, rb, w_gu, b_gu, w_down, b_down, final_gain=None):
    h, top_i, top_p = _router(y, gain, scale, shift, rw, rb)
    top_i = top_i[:, :TOP_K]
    top_p = top_p[:, :TOP_K]
    pos, slot_token, tile_expert, n_active = _route_plan(top_i)
    xs = jnp.take(h, slot_token, axis=0).astype(BF16)
    out = _moe_experts(layer, xs, tile_expert, n_active, w_gu, b_gu, w_down, b_down)
    moe = jnp.sum(jnp.take(out, pos.reshape(-1), axis=0).reshape(T_ALL, TOP_K, D_MODEL) * top_p[..., None], axis=1)
    return _residual(y, moe, gate, final_gain)


def _heads(x, n):
    b = x.shape[0] // SEQ
    return x.reshape(b, SEQ, n, -1).transpose(0, 2, 1, 3)


def kernel(x_prompt, x_sample, c, cache_a_k, cache_a_v, state_b_C, state_b_n, state_b_m, cache_c_k, cache_c_v, cache_d_k, cache_d_v, c_ctx, w_mod, b_mod, norm1_g, norm2_g, w_in_even, w_out_even, a_q_gain, a_k_gain, b_gate_bias, b_norm_gain, w_in_odd, w_out_odd, c_rpb, d_lambda, d_norm_gain, router_w, router_b, expert_w_gu, expert_b_gu, expert_w_down, expert_b_down, final_norm_g):
    y = jnp.concatenate([x_prompt.reshape(T_CTX, D_MODEL), x_sample.reshape(T_LAT, D_MODEL)], axis=0)
    cond = jnp.zeros((MOD_ROWS, D_MODEL), F32).at[0].set(c_ctx).at[1:1 + DEC_BATCH].set(c)
    mod = _modulation(cond, w_mod, b_mod).reshape(DEPTH, MOD_ROWS, 6, 1, D_MODEL)
    rope_cos, rope_sin = _rope_tables()
    scale = HD ** -0.5
    outs = {}

    for layer in range(DEPTH):
        sh1, sc1, g1, sh2, sc2, g2 = (mod[layer, :, k] for k in range(6))
        j = layer // 2
        if layer % 2 == 0:
            w = w_in_even[j]
            sizes = np.cumsum([0, 512, 128, 128, 512, 512, 512, 512, 16])
            aq, ak, av, bq, bk, bv, bo, bg = (w[:, sizes[k]:sizes[k + 1]] for k in range(8))
            w_in = jnp.concatenate([aq, bo, bq, bk, bv, ak, av, bg, jnp.zeros((D_MODEL, EV_N - EV_BG - 16), F32)],
                                   axis=1).astype(BF16)
            p = _norm_proj(y, norm1_g[layer], sc1, sh1, w_in)
            qg = jnp.tile(a_q_gain[j], LANES // HD).reshape(1, LANES)
            kg = jnp.tile(a_k_gain[j], LANES // HD).reshape(1, LANES)
            specs = ((EV_AQ, 512, 0, True, scale, BF16), (EV_AK, 128, 1, False, 1.0, F32),
                     (EV_AK, 128, 1, True, 1.0, BF16), (EV_AV, 128, None, False, 1.0, BF16))
            qa, ka_f32, ka, va = _prep(p, rope_cos, rope_sin, [qg, kg], specs)
            oa_ctx = _attention(qa, ka, va, ctx=True, group=A_HEADS // A_KV, n_kv=A_KV)
            cache = (cache_a_k[:, j].astype(BF16), cache_a_v[:, j].astype(BF16))
            oa_lat = _attention(qa, ka, va, ctx=False, group=A_HEADS // A_KV, n_kv=A_KV, cache=cache)
            zc = jnp.zeros((BATCH, 2, B_HEADS, B_DK, B_DV), F32)
            zn = jnp.zeros((BATCH, 2, B_HEADS, 1, B_DK), F32)
            zm = jnp.zeros((BATCH, 2, B_HEADS, 1, LANES), F32)
            hb_ctx, bC, bn, bm = _mlstm(p, b_gate_bias[j], zc, zn, zm, ctx=True)
            m0 = jnp.broadcast_to(state_b_m[:, j][..., None, None], (DEC_BATCH, 2, B_HEADS, 1, LANES))
            hb_lat, _, _, _ = _mlstm(p, b_gate_bias[j], state_b_C[:, j], state_b_n[:, j][:, :, :, None, :], m0,
                                     ctx=False)
            oa = jnp.concatenate([oa_ctx, oa_lat], axis=0)
            hb = jnp.concatenate([hb_ctx, hb_lat], axis=0)
            y = _merge(y, g1, w_out_even[j].astype(BF16), oa, hb, p=p, norm_gain=b_norm_gain[j])
            outs.setdefault("a_k", []).append(_heads(ka_f32[:T_CTX], A_KV))
            outs.setdefault("a_v", []).append(_heads(p[:T_CTX, EV_AV:EV_AV + 128], A_KV))
            outs.setdefault("b_C", []).append(bC)
            outs.setdefault("b_n", []).append(bn[:, :, :, 0, :])
            outs.setdefault("b_m", []).append(bm[:, :, :, 0, 0])
        else:
            p = _norm_proj(y, norm1_g[layer], sc1, sh1, w_in_odd[j].astype(BF16))
            specs = ((0, 512, None, False, scale, BF16), (512, 512, None, False, 1.0, BF16),
                     (1024, 512, None, False, 1.0, BF16), (1536, 512, None, True, scale, BF16),
                     (2048, 512, None, True, 1.0, BF16), (2560, 512, None, False, 1.0, BF16))
            qc, kc, vc, qd, kd, vd = _prep(p, rope_cos, rope_sin, [], specs)
            lam_init = 0.8 - 0.6 * math.exp(-0.3 * layer)
            lp = d_lambda[j].astype(F32)
            lam = jnp.exp(jnp.sum(lp[0] * lp[1])) - jnp.exp(jnp.sum(lp[2] * lp[3])) + lam_init
            lam_vec = jnp.stack([lam, jnp.asarray(1.0 - lam_init, F32)]).astype(F32)
            diff = (lam_vec, d_norm_gain[j])
            oc_ctx = _attention(qc, kc, vc, ctx=True, n_kv=C_HEADS)
            od_ctx = _attention(qd, kd, vd, ctx=True, diff=diff)
            bias = _neighbourhood_bias(c_rpb[j])
            oc_lat = _attention(qc, kc, vc, ctx=False, n_kv=C_HEADS, bias=bias,
                                cache=(cache_c_k[:, j].astype(BF16), cache_c_v[:, j].astype(BF16)))
            kd_cache = cache_d_k[:, j].reshape(DEC_BATCH, 2 * D_HEADS, PAST_LEN, HD).astype(BF16)
            od_lat = _attention(qd, kd, vd, ctx=False, diff=diff, cache=(kd_cache, cache_d_v[:, j].astype(BF16)))
            oc = jnp.concatenate([oc_ctx, oc_lat], axis=0)
            od = jnp.concatenate([od_ctx, od_lat], axis=0)
            y = _merge(y, g1, w_out_odd[j].astype(BF16), oc, od)
            pc = p[:T_CTX]
            outs.setdefault("c_k", []).append(_heads(pc[:, 512:1024], C_HEADS))
            outs.setdefault("c_v", []).append(_heads(pc[:, 1024:1536], C_HEADS))
            outs.setdefault("d_k", []).append(_heads(pc[:, 2048:2560], 2 * D_HEADS).reshape(BATCH, D_HEADS, 2, SEQ, HD))
            outs.setdefault("d_v", []).append(_heads(pc[:, 2560:3072], D_HEADS))
        res = _moe_layer(layer, y, norm2_g[layer], sc2, sh2, g2, router_w[layer], router_b[layer],
                         expert_w_gu, expert_b_gu, expert_w_down, expert_b_down,
                         final_gain=final_norm_g if layer == DEPTH - 1 else None)
        y = res[0]
    y_norm = res[1]
    stack = lambda k: jnp.stack(outs[k], axis=1)
    return (y_norm[:T_CTX].reshape(BATCH, SEQ, D_MODEL), y_norm[T_CTX:].reshape(DEC_BATCH, DEC_SEQ, D_MODEL),
            stack("a_k"), stack("a_v"), stack("b_C"), stack("b_n"), stack("b_m"),
            stack("c_k"), stack("c_v"), stack("d_k"), stack("d_v"))
```

```python
import functools
import math

import numpy as np
import jax
import jax.numpy as jnp
from jax import lax
from jax.experimental import pallas as pl
from jax.experimental.pallas import tpu as pltpu

D_MODEL = 1024
BATCH = 32
SEQ = 256
DEPTH = 2
DEC_BATCH = 8
DEC_SEQ = 1024
PAST_LEN = 512
GRID_W = 64
HD = 64
A_HEADS = 8
A_KV = 2
B_HEADS = 4
B_DK = 128
B_DV = 128
B_CHUNK = 128
C_HEADS = 8
NA_ROWS = 8
NA_COLS = 16
D_HEADS = 4
D_VDIM = 2 * HD
N_EXPERTS = 32
TOP_K = 4
D_FF = 1024
SWIGLU_LIMIT = 7.0
SWIGLU_ALPHA = 1.702
ROPE_THETA = 10000.0
EPS = 1e-6

F32 = jnp.float32
BF16 = jnp.bfloat16
HIGHEST = lax.Precision.HIGHEST

T_CTX = BATCH * SEQ
T_LAT = DEC_BATCH * DEC_SEQ
T_ALL = T_CTX + T_LAT
TM = 256
CTX_TILES = T_CTX // TM
LAT_TILES_PER_BATCH = DEC_SEQ // TM
N_TILES = T_ALL // TM
MOD_ROWS = 16
LANES = 128
NEG_BIG = -1e30
MOE_TM = 256
N_ASSIGN = T_ALL * TOP_K
MOE_ROWS = N_ASSIGN + N_EXPERTS * MOE_TM
MOE_TILES = MOE_ROWS // MOE_TM
VMEM_LIMIT = 56 * 1024 * 1024

EV_AQ, EV_BO, EV_BQ, EV_BK, EV_BV, EV_AK, EV_AV, EV_BG = 0, 512, 1024, 1536, 2048, 2560, 2688, 2816
EV_N = 2944
OD_N = 3072


def _params(sem, vmem=VMEM_LIMIT):
    return pltpu.CompilerParams(dimension_semantics=sem, vmem_limit_bytes=vmem)


def _mod_row(i):
    return jnp.where(i < CTX_TILES, 0, 1 + (i - CTX_TILES) // LAT_TILES_PER_BATCH)


def _rope_block(i):
    return jnp.where(i < CTX_TILES, LAT_TILES_PER_BATCH, (i - CTX_TILES) % LAT_TILES_PER_BATCH)


def _dot(a, b, precision=None):
    return jnp.dot(a, b, preferred_element_type=F32, precision=precision)


def _dot_nt(a, b):
    return lax.dot_general(a, b, (((1,), (1,)), ((), ())), preferred_element_type=F32)


def _dot_tn(a, b):
    return lax.dot_general(a, b, (((0,), (0,)), ((), ())), preferred_element_type=F32)


def _modulation_kernel(c_ref, w_ref, b_ref, o_ref):
    c = c_ref[...]
    s = c * jax.nn.sigmoid(c)
    o_ref[...] = _dot(s, w_ref[...], HIGHEST) + b_ref[...]


def _modulation(cond, w_mod, b_mod):
    tn = 1536
    return pl.pallas_call(
        _modulation_kernel,
        out_shape=jax.ShapeDtypeStruct((DEPTH, MOD_ROWS, 6 * D_MODEL), F32),
        grid=(DEPTH, 6 * D_MODEL // tn),
        in_specs=[
            pl.BlockSpec((MOD_ROWS, D_MODEL), lambda l, j: (0, 0)),
            pl.BlockSpec((None, D_MODEL, tn), lambda l, j: (l, 0, j)),
            pl.BlockSpec((None, 1, tn), lambda l, j: (l, 0, j)),
        ],
        out_specs=pl.BlockSpec((None, MOD_ROWS, tn), lambda l, j: (l, 0, j)),
        compiler_params=_params(("parallel", "parallel")),
        name="modulation",
    )(cond, w_mod, b_mod.reshape(DEPTH, 1, 6 * D_MODEL))


def _norm_mod(y, g, sc, sh):
    ms = jnp.mean(y * y, axis=-1, keepdims=True)
    return (y * lax.rsqrt(ms + EPS) * g) * (1.0 + sc) + sh


def _norm_proj_kernel(y_ref, g_ref, sc_ref, sh_ref, w_ref, o_ref):
    h = _norm_mod(y_ref[...], g_ref[...], sc_ref[...], sh_ref[...])
    o_ref[...] = _dot(h.astype(BF16), w_ref[...])


def _norm_proj(y, gain, scale, shift, w_bf16):
    n = w_bf16.shape[1]
    vec = pl.BlockSpec((None, 1, D_MODEL), lambda i: (_mod_row(i), 0, 0))
    return pl.pallas_call(
        _norm_proj_kernel,
        out_shape=jax.ShapeDtypeStruct((T_ALL, n), F32),
        grid=(N_TILES,),
        in_specs=[
            pl.BlockSpec((TM, D_MODEL), lambda i: (i, 0)),
            pl.BlockSpec((1, D_MODEL), lambda i: (0, 0)),
            vec, vec,
            pl.BlockSpec((D_MODEL, n), lambda i: (0, 0)),
        ],
        out_specs=pl.BlockSpec((TM, n), lambda i: (i, 0)),
        compiler_params=_params(("parallel",)),
        name="norm_proj",
    )(y, gain.reshape(1, D_MODEL), scale, shift, w_bf16)


def _rope_rotate(x):
    w = x.shape[-1]
    lane = lax.broadcasted_iota(jnp.int32, x.shape, 1)
    nxt = pltpu.roll(x, w - 1, 1)
    prv = pltpu.roll(x, 1, 1)
    return jnp.where((lane & 1) == 0, -nxt, prv)


def _prep_kernel(*refs, specs, n_gain):
    p_ref, cos_ref, sin_ref, bd_ref = refs[:4]
    gain_refs = refs[4:4 + n_gain]
    out_refs = refs[4 + n_gain:]
    cos = cos_ref[...]
    sin = sin_ref[...]
    for (col, width, gi, rope, scale, _), o_ref in zip(specs, out_refs):
        for c0 in range(0, width, LANES):
            x = p_ref[:, col + c0:col + c0 + LANES]
            if gi is not None:
                ss = _dot(x * x, bd_ref[...], HIGHEST)
                x = x * lax.rsqrt(ss * (1.0 / HD) + EPS) * gain_refs[gi][...]
            if rope:
                x = x * cos + _rope_rotate(x) * sin
            if scale != 1.0:
                x = x * scale
            o_ref[:, c0:c0 + LANES] = x.astype(o_ref.dtype)


def _prep(p, rope_cos, rope_sin, gains, specs):
    n = p.shape[1]
    bd = jnp.asarray(np.kron(np.eye(LANES // HD), np.ones((HD, HD))), F32)
    rope_spec = pl.BlockSpec((TM, LANES), lambda i: (_rope_block(i), 0))
    in_specs = [pl.BlockSpec((TM, n), lambda i: (i, 0)), rope_spec, rope_spec,
                pl.BlockSpec((LANES, LANES), lambda i: (0, 0))]
    in_specs += [pl.BlockSpec((1, LANES), lambda i: (0, 0)) for _ in gains]
    return pl.pallas_call(
        functools.partial(_prep_kernel, specs=specs, n_gain=len(gains)),
        out_shape=[jax.ShapeDtypeStruct((T_ALL, s[1]), s[5]) for s in specs],
        grid=(N_TILES,),
        in_specs=in_specs,
        out_specs=[pl.BlockSpec((TM, s[1]), lambda i: (i, 0)) for s in specs],
        compiler_params=_params(("parallel",)),
        name="qkv_prep",
    )(p, rope_cos, rope_sin, bd, *gains)


def _rope_tables():
    half = HD // 2
    freqs = 1.0 / (ROPE_THETA ** (jnp.arange(0, half, 2, dtype=F32) / half))
    t = jnp.arange(DEC_SEQ)
    rows = (t // GRID_W).astype(F32)
    cols = (t % GRID_W).astype(F32)
    ang = jnp.concatenate([rows[:, None] * freqs, cols[:, None] * freqs], axis=-1)
    cos = jnp.repeat(jnp.cos(ang), 2, axis=-1)
    sin = jnp.repeat(jnp.sin(ang), 2, axis=-1)
    cos = jnp.concatenate([jnp.tile(cos, (1, LANES // HD)), jnp.ones((TM, LANES), F32)], axis=0)
    sin = jnp.concatenate([jnp.tile(sin, (1, LANES // HD)), jnp.zeros((TM, LANES), F32)], axis=0)
    return cos, sin


def _lane_slice(ref, h, width=HD):
    per = LANES // width
    blk = ref[:, (h // per) * LANES:(h // per + 1) * LANES]
    if per == 1:
        return blk
    return blk[:, (h % per) * width:(h % per + 1) * width]


def _softmax_parts(scores):
    m = None
    for s in scores:
        ms = jnp.max(s, axis=-1, keepdims=True)
        m = ms if m is None else jnp.maximum(m, ms)
    ps = [jnp.exp(s - m) for s in scores]
    l = None
    for p in ps:
        ls = jnp.sum(p, axis=-1, keepdims=True)
        l = ls if l is None else l + ls
    return ps, l


def _attn_std_kernel(*refs, group, n_kv, has_cache, has_bias, bq):
    it = iter(refs)
    q_ref, kn_ref, vn_ref = next(it), next(it), next(it)
    kc_ref = vc_ref = b_ref = None
    if has_cache:
        kc_ref, vc_ref = next(it), next(it)
    if has_bias:
        b_ref = next(it)
    o_ref = next(it)
    outs = []
    for g in range(n_kv):
        qs = jnp.concatenate([_lane_slice(q_ref, g * group + j) for j in range(group)], axis=0)
        kn = _lane_slice(kn_ref, g)
        vn = _lane_slice(vn_ref, g)
        s_new = _dot_nt(qs, kn)
        if has_bias:
            s_new = s_new + b_ref[g]
        scores = [s_new]
        if has_cache:
            scores.append(_dot_nt(qs, kc_ref[g]))
        ps, l = _softmax_parts(scores)
        o = _dot(ps[0].astype(BF16), vn)
        if has_cache:
            o = o + _dot(ps[1].astype(BF16), vc_ref[g])
        o = o / l
        for j in range(group):
            outs.append(o[j * bq:(j + 1) * bq])
    o_ref[...] = jnp.concatenate(outs, axis=1).astype(o_ref.dtype)


def _attn_diff_kernel(*refs, has_cache):
    it = iter(refs)
    lam_ref, q_ref, kn_ref, vn_ref = next(it), next(it), next(it), next(it)
    kc_ref = vc_ref = None
    if has_cache:
        kc_ref, vc_ref = next(it), next(it)
    g_ref, o_ref = next(it), next(it)
    lam = lam_ref[0]
    post = lam_ref[1]
    outs = []
    for h in range(D_HEADS):
        pd_new, pd_c = None, None
        for j in range(2):
            f = 2 * h + j
            qs = _lane_slice(q_ref, f)
            scores = [_dot_nt(qs, _lane_slice(kn_ref, f))]
            if has_cache:
                scores.append(_dot_nt(qs, kc_ref[f]))
            ps, l = _softmax_parts(scores)
            r = 1.0 / l
            if j == 0:
                pd_new = ps[0] * r
                pd_c = ps[1] * r if has_cache else None
            else:
                r = r * lam
                pd_new = pd_new - ps[0] * r
                pd_c = pd_c - ps[1] * r if has_cache else None
        o = _dot(pd_new.astype(BF16), _lane_slice(vn_ref, h, D_VDIM))
        if has_cache:
            o = o + _dot(pd_c.astype(BF16), vc_ref[h])
        ms = jnp.mean(o * o, axis=-1, keepdims=True)
        outs.append(o * lax.rsqrt(ms + EPS) * g_ref[...] * post)
    o_ref[...] = jnp.concatenate(outs, axis=1).astype(o_ref.dtype)


def _attention(q, kn, vn, *, ctx, group=1, n_kv=1, cache=None, bias=None, diff=None, bq=256):
    if ctx:
        nb, sq, row0 = BATCH, SEQ, 0
    else:
        nb, sq, row0 = DEC_BATCH, DEC_SEQ, T_CTX
    nq = sq // bq
    qb0 = row0 // bq
    kb0 = row0 // sq
    wq, wk, wv = q.shape[1], kn.shape[1], vn.shape[1]
    in_specs = [
        pl.BlockSpec((bq, wq), lambda b, i: (qb0 + b * nq + i, 0)),
        pl.BlockSpec((sq, wk), lambda b, i: (kb0 + b, 0)),
        pl.BlockSpec((sq, wv), lambda b, i: (kb0 + b, 0)),
    ]
    args = [q, kn, vn]
    if cache is not None:
        kc, vc = cache
        in_specs += [pl.BlockSpec((None,) + kc.shape[1:], lambda b, i: (b, 0, 0, 0)),
                     pl.BlockSpec((None,) + vc.shape[1:], lambda b, i: (b, 0, 0, 0))]
        args += [kc, vc]
    if diff is None:
        if bias is not None:
            in_specs.append(pl.BlockSpec((bias.shape[0], bq, sq), lambda b, i: (0, i, 0)))
            args.append(bias)
        body = functools.partial(_attn_std_kernel, group=group, n_kv=n_kv, has_cache=cache is not None,
                                 has_bias=bias is not None, bq=bq)
    else:
        lam_vec, gain = diff
        in_specs = [pl.BlockSpec(memory_space=pltpu.SMEM)] + in_specs
        args = [lam_vec] + args
        in_specs.append(pl.BlockSpec((1, D_VDIM), lambda b, i: (0, 0)))
        args.append(gain.reshape(1, D_VDIM))
        body = functools.partial(_attn_diff_kernel, has_cache=cache is not None)
    return pl.pallas_call(
        body,
        out_shape=jax.ShapeDtypeStruct((nb * sq, 512), BF16),
        grid=(nb, nq),
        in_specs=in_specs,
        out_specs=pl.BlockSpec((bq, 512), lambda b, i: (b * nq + i, 0)),
        compiler_params=_params(("parallel", "parallel")),
        name="attention",
    )(*args)


def _neighbourhood_bias(rpb):
    rows = DEC_SEQ // GRID_W
    wr = min(NA_ROWS, rows)
    r = np.arange(rows)
    c = np.arange(GRID_W)
    r0 = np.clip(r - wr // 2, 0, rows - wr)
    c0 = np.clip(c - NA_COLS // 2, 0, GRID_W - NA_COLS)
    row_ok = (r[None, :] >= r0[:, None]) & (r[None, :] < r0[:, None] + wr)
    col_ok = (c[None, :] >= c0[:, None]) & (c[None, :] < c0[:, None] + NA_COLS)
    dr = np.clip(r[None, :] - r[:, None] + (NA_ROWS - 1), 0, 2 * NA_ROWS - 2)
    dc = np.clip(c[None, :] - c[:, None], 1 - NA_COLS, NA_COLS - 1) + (NA_COLS - 1)
    oh_r = jnp.asarray(dr[..., None] == np.arange(2 * NA_ROWS - 1), F32)
    oh_c = jnp.asarray(dc[..., None] == np.arange(2 * NA_COLS - 1), F32)
    by_col = jnp.einsum("hrd,qkd->hrqk", rpb.astype(F32), oh_c, precision=HIGHEST)
    full = jnp.einsum("PKr,hrqk->hPqKk", oh_r, by_col, precision=HIGHEST)
    ok = jnp.asarray(row_ok[:, None, :, None] & col_ok[None, :, None, :])
    return jnp.where(ok[None], full, NEG_BIG).reshape(C_HEADS, DEC_SEQ, DEC_SEQ)


def _log_sigmoid(x):
    return jnp.minimum(x, 0.0) - jnp.log1p(jnp.exp(-jnp.abs(x)))


def _mlstm_kernel(q_ref, k_ref, v_ref, g_ref, gb_ref, c0_ref, n0_ref, m0_ref, h_ref, c_ref, n_ref, m_ref, *, seq):
    L = B_CHUNK
    nc = seq // L
    row = lax.broadcasted_iota(jnp.int32, (L, L), 0)
    col = lax.broadcasted_iota(jnp.int32, (L, L), 1)
    k_scale = B_DK ** -0.5
    for d in range(2):
        keep = (col <= row) if d == 0 else (col >= row)
        tri = keep.astype(F32)
        for h in range(B_HEADS):
            ci = (2 * d) * B_HEADS + h
            cf = (2 * d + 1) * B_HEADS + h
            hs = slice(h * B_DK, (h + 1) * B_DK)

            def chunk(j, carry, d=d, keep=keep, tri=tri, ci=ci, cf=cf, hs=hs):
                C, n, m = carry
                c = j if d == 0 else nc - 1 - j
                off = pl.multiple_of(c * L, L)
                qc = q_ref[pl.ds(off, L), hs]
                kc = k_ref[pl.ds(off, L), hs] * k_scale
                vc = v_ref[pl.ds(off, L), hs]
                gates = g_ref[pl.ds(off, L), :] + gb_ref[...]
                cum = _dot(tri, _log_sigmoid(gates), HIGHEST)
                b_col = cum[:, cf:cf + 1]
                i_col = gates[:, ci:ci + 1]
                b_row = cum.T[cf:cf + 1, :]
                i_row = gates.T[ci:ci + 1, :]
                dlog = jnp.where(keep, b_col - b_row + i_row, -jnp.inf)
                inter = b_col + m
                m_t = jnp.maximum(inter, jnp.max(dlog, axis=-1, keepdims=True))
                w_intra = jnp.exp(dlog - m_t)
                w_inter = jnp.exp(inter - m_t)
                qb = qc.astype(BF16)
                qk = _dot_nt(qb, kc.astype(BF16)) * w_intra
                num = _dot(qk.astype(BF16), vc.astype(BF16)) + w_inter * _dot(qb, C.astype(BF16))
                den = jnp.sum(qk, axis=-1, keepdims=True) + w_inter * jnp.sum(qc * n, axis=-1, keepdims=True)
                h_out = num / jnp.maximum(jnp.abs(den), jnp.exp(-m_t))
                if d == 0:
                    h_ref[pl.ds(off, L), hs] = h_out
                    b_last = b_col[L - 1:L, :]
                else:
                    h_ref[pl.ds(off, L), hs] = h_ref[pl.ds(off, L), hs] + h_out
                    b_last = b_col[0:1, :]
                end_col = b_last - b_col + i_col
                m_new = jnp.maximum(b_last + m, jnp.max(end_col, axis=0, keepdims=True))
                w_end = jnp.exp(end_col - m_new)
                decay = jnp.exp(b_last + m - m_new)
                kw = kc * w_end
                C_new = decay * C + _dot_tn(kw.astype(BF16), vc.astype(BF16))
                n_new = decay * n + jnp.sum(kw, axis=0, keepdims=True)
                return C_new, n_new, m_new

            init = (c0_ref[d, h], n0_ref[d, h], m0_ref[d, h][:, 0:1])
            C, n, m = lax.fori_loop(0, nc, chunk, init)
            c_ref[d, h] = C
            n_ref[d, h] = n
            m_ref[d, h] = jnp.broadcast_to(m, (1, LANES))


def _mlstm(p, gate_bias, c0, n0, m0, *, ctx):
    if ctx:
        nb, seq, blk0 = BATCH, SEQ, 0
    else:
        nb, seq, blk0 = DEC_BATCH, DEC_SEQ, T_CTX // DEC_SEQ
    w = B_HEADS * B_DK

    def cols(c0_, width):
        return pl.BlockSpec((seq, width), lambda b: (blk0 + b, c0_ // width))

    gb = jnp.zeros((1, LANES), F32).at[0, :4 * B_HEADS].set(gate_bias.reshape(-1).astype(F32))
    st = lambda shape: pl.BlockSpec((None,) + shape, lambda b: (b, 0, 0, 0, 0))
    return pl.pallas_call(
        functools.partial(_mlstm_kernel, seq=seq),
        out_shape=[
            jax.ShapeDtypeStruct((nb * seq, w), F32),
            jax.ShapeDtypeStruct((nb, 2, B_HEADS, B_DK, B_DV), F32),
            jax.ShapeDtypeStruct((nb, 2, B_HEADS, 1, B_DK), F32),
            jax.ShapeDtypeStruct((nb, 2, B_HEADS, 1, LANES), F32),
        ],
        grid=(nb,),
        in_specs=[
            cols(EV_BQ, w), cols(EV_BK, w), cols(EV_BV, w), cols(EV_BG, LANES),
            pl.BlockSpec((1, LANES), lambda b: (0, 0)),
            st((2, B_HEADS, B_DK, B_DV)), st((2, B_HEADS, 1, B_DK)), st((2, B_HEADS, 1, LANES)),
        ],
        out_specs=[
            pl.BlockSpec((seq, w), lambda b: (b, 0)),
            st((2, B_HEADS, B_DK, B_DV)), st((2, B_HEADS, 1, B_DK)), st((2, B_HEADS, 1, LANES)),
        ],
        compiler_params=_params(("parallel",)),
        name="mlstm",
    )(p, p, p, p, gb, c0, n0, m0)


def _merge_kernel(*refs, even):
    if even:
        a_ref, hb_ref, bo_ref, ng_ref, w_ref, y_ref, g_ref, o_ref = refs
        parts = [a_ref[...]]
        for h in range(B_HEADS):
            hs = slice(h * B_DV, (h + 1) * B_DV)
            x = hb_ref[:, hs]
            ms = jnp.mean(x * x, axis=-1, keepdims=True)
            xn = x * lax.rsqrt(ms + EPS) * ng_ref[:, hs]
            parts.append((jax.nn.sigmoid(bo_ref[:, hs]) * xn).astype(BF16))
    else:
        a_ref, b_ref, w_ref, y_ref, g_ref, o_ref = refs
        parts = [a_ref[...], b_ref[...]]
    cat = jnp.concatenate(parts, axis=1)
    o_ref[...] = y_ref[...] + g_ref[...] * _dot(cat, w_ref[...])


def _merge(y, gate, w_bf16, a, b, *, p=None, norm_gain=None):
    even = p is not None
    half = pl.BlockSpec((TM, 512), lambda i: (i, 0))
    in_specs = [half, half]
    args = [a, b]
    if even:
        in_specs += [pl.BlockSpec((TM, 512), lambda i: (i, EV_BO // 512)), pl.BlockSpec((1, 512), lambda i: (0, 0))]
        args += [p, norm_gain.reshape(1, 512)]
    in_specs += [
        pl.BlockSpec((D_MODEL, D_MODEL), lambda i: (0, 0)),
        pl.BlockSpec((TM, D_MODEL), lambda i: (i, 0)),
        pl.BlockSpec((None, 1, D_MODEL), lambda i: (_mod_row(i), 0, 0)),
    ]
    args += [w_bf16, y, gate]
    return pl.pallas_call(
        functools.partial(_merge_kernel, even=even),
        out_shape=jax.ShapeDtypeStruct((T_ALL, D_MODEL), F32),
        grid=(N_TILES,),
        in_specs=in_specs,
        out_specs=pl.BlockSpec((TM, D_MODEL), lambda i: (i, 0)),
        compiler_params=_params(("parallel",)),
        name="merge",
    )(*args)


def _router_kernel(y_ref, g_ref, sc_ref, sh_ref, w_ref, b_ref, h_ref, ti_ref, tp_ref, rk_ref, cnt_ref, base_ref):
    @pl.when(pl.program_id(0) == 0)
    def _():
        base_ref[...] = jnp.zeros(base_ref.shape, F32)

    h = _norm_mod(y_ref[...], g_ref[...], sc_ref[...], sh_ref[...])
    h_ref[...] = h
    logits = _dot(h, w_ref[...], HIGHEST) + b_ref[...]
    lane = lax.broadcasted_iota(jnp.int32, logits.shape, 1)
    vals, idxs = [], []
    for _ in range(TOP_K):
        mx = jnp.max(logits, axis=-1, keepdims=True)
        ix = jnp.min(jnp.where(logits == mx, lane, LANES), axis=-1, keepdims=True)
        vals.append(mx)
        idxs.append(ix)
        logits = jnp.where(lane == ix, -jnp.inf, logits)
    es = [jnp.exp(v - vals[0]) for v in vals]
    tot = es[0] + es[1] + es[2] + es[3]
    ti = jnp.zeros(logits.shape, jnp.int32)
    tp = jnp.zeros(logits.shape, F32)
    for k in range(TOP_K):
        ti = jnp.where(lane == k, idxs[k], ti)
        tp = jnp.where(lane == k, es[k] / tot, tp)
    ti_ref[...] = ti
    tp_ref[...] = tp
    onehots = [(lane == ix).astype(F32) for ix in idxs]
    cnt = onehots[0] + onehots[1] + onehots[2] + onehots[3]
    row = lax.broadcasted_iota(jnp.int32, (TM, TM), 0)
    col = lax.broadcasted_iota(jnp.int32, (TM, TM), 1)
    before = _dot((col < row).astype(BF16), cnt.astype(BF16)) + base_ref[...]
    rk = jnp.zeros(logits.shape, jnp.int32)
    for k in range(TOP_K):
        r_k = jnp.sum(onehots[k] * before, axis=-1, keepdims=True).astype(jnp.int32)
        rk = jnp.where(lane == k, r_k, rk)
    rk_ref[...] = rk
    base_ref[...] = base_ref[...] + jnp.sum(cnt, axis=0, keepdims=True)
    cnt_ref[...] = base_ref[...]


def _router(y, gain, scale, shift, rw, rb):
    vec = pl.BlockSpec((None, 1, D_MODEL), lambda i: (_mod_row(i), 0, 0))
    rw_p = jnp.zeros((D_MODEL, LANES), F32).at[:, :N_EXPERTS].set(rw)
    rb_p = jnp.full((1, LANES), NEG_BIG, F32).at[0, :N_EXPERTS].set(rb)
    tile = lambda w: pl.BlockSpec((TM, w), lambda i: (i, 0))
    return pl.pallas_call(
        _router_kernel,
        out_shape=[jax.ShapeDtypeStruct((T_ALL, D_MODEL), F32),
                   jax.ShapeDtypeStruct((T_ALL, LANES), jnp.int32),
                   jax.ShapeDtypeStruct((T_ALL, LANES), F32),
                   jax.ShapeDtypeStruct((T_ALL, LANES), jnp.int32),
                   jax.ShapeDtypeStruct((1, LANES), F32)],
        grid=(N_TILES,),
        in_specs=[tile(D_MODEL), pl.BlockSpec((1, D_MODEL), lambda i: (0, 0)), vec, vec,
                  pl.BlockSpec((D_MODEL, LANES), lambda i: (0, 0)), pl.BlockSpec((1, LANES), lambda i: (0, 0))],
        out_specs=[tile(D_MODEL), tile(LANES), tile(LANES), tile(LANES), pl.BlockSpec((1, LANES), lambda i: (0, 0))],
        scratch_shapes=[pltpu.VMEM((1, LANES), F32)],
        compiler_params=_params(("arbitrary",)),
        name="router",
    )(y, gain.reshape(1, D_MODEL), scale, shift, rw_p, rb_p)


def _route_plan(top_i, rank, counts):
    experts = jnp.arange(N_EXPERTS, dtype=jnp.int32)
    padded = ((counts + MOE_TM - 1) // MOE_TM) * MOE_TM
    seg_end = jnp.cumsum(padded)
    seg_start = seg_end - padded
    onehot = top_i[..., None] == experts
    pos = jnp.sum(jnp.where(onehot, seg_start, 0), axis=-1) + rank
    slot_token = jnp.zeros((MOE_ROWS,), jnp.int32).at[pos.reshape(-1)].set(
        jnp.arange(N_ASSIGN, dtype=jnp.int32) // TOP_K)
    n_active = seg_end[-1] // MOE_TM
    tile_start = jnp.arange(MOE_TILES, dtype=jnp.int32) * MOE_TM
    tile_expert = jnp.sum((seg_end[None, :] <= tile_start[:, None]).astype(jnp.int32), axis=1)
    last = jnp.sum((seg_end <= (n_active - 1) * MOE_TM).astype(jnp.int32))
    tile_expert = jnp.minimum(jnp.where(tile_start < seg_end[-1], tile_expert, last), N_EXPERTS - 1)
    return pos, slot_token, tile_expert.astype(jnp.int32), n_active.reshape(1).astype(jnp.int32)


def _moe_kernel(te_ref, na_ref, x_ref, wgu_ref, bgu_ref, wd_ref, bd_ref, o_ref, wgu_bf, wd_bf):
    i = pl.program_id(0)
    e = te_ref[i]
    prev = te_ref[jnp.maximum(i - 1, 0)]

    @pl.when((i == 0) | (e != prev))
    def _():
        wgu_bf[...] = wgu_ref[...].astype(BF16)
        wd_bf[...] = wd_ref[...].astype(BF16)

    @pl.when(i < na_ref[0])
    def _():
        gu = _dot(x_ref[...], wgu_bf[...]) + bgu_ref[...]
        gate = jnp.minimum(gu[:, :D_FF], SWIGLU_LIMIT)
        up = jnp.clip(gu[:, D_FF:], -SWIGLU_LIMIT, SWIGLU_LIMIT)
        act = (up + 1.0) * gate * jax.nn.sigmoid(SWIGLU_ALPHA * gate)
        o_ref[...] = _dot(act.astype(BF16), wd_bf[...]) + bd_ref[...]

    @pl.when(i >= na_ref[0])
    def _():
        o_ref[...] = jnp.zeros(o_ref.shape, o_ref.dtype)


def _moe_experts(layer, xs, tile_expert, n_active, w_gu, b_gu, w_down, b_down):
    grid_spec = pltpu.PrefetchScalarGridSpec(
        num_scalar_prefetch=2,
        grid=(MOE_TILES,),
        in_specs=[
            pl.BlockSpec((MOE_TM, D_MODEL), lambda i, te, na: (i, 0)),
            pl.BlockSpec((None, None, D_MODEL, 2 * D_FF), lambda i, te, na: (layer, te[i], 0, 0)),
            pl.BlockSpec((None, None, 1, 2 * D_FF), lambda i, te, na: (layer, te[i], 0, 0)),
            pl.BlockSpec((None, None, D_FF, D_MODEL), lambda i, te, na: (layer, te[i], 0, 0)),
            pl.BlockSpec((None, None, 1, D_MODEL), lambda i, te, na: (layer, te[i], 0, 0)),
        ],
        out_specs=pl.BlockSpec((MOE_TM, D_MODEL), lambda i, te, na: (i, 0)),
        scratch_shapes=[pltpu.VMEM((D_MODEL, 2 * D_FF), BF16), pltpu.VMEM((D_FF, D_MODEL), BF16)],
    )
    return pl.pallas_call(
        _moe_kernel,
        out_shape=jax.ShapeDtypeStruct((MOE_ROWS, D_MODEL), F32),
        grid_spec=grid_spec,
        compiler_params=_params(("arbitrary",)),
        name="moe_experts",
    )(tile_expert, n_active, xs, w_gu, b_gu.reshape(DEPTH, N_EXPERTS, 1, 2 * D_FF), w_down,
      b_down.reshape(DEPTH, N_EXPERTS, 1, D_MODEL))


def _residual_kernel(*refs, final):
    if final:
        y_ref, x_ref, g_ref, fg_ref, o_ref, n_ref = refs
    else:
        y_ref, x_ref, g_ref, o_ref = refs
    y = y_ref[...] + g_ref[...] * x_ref[...]
    o_ref[...] = y
    if final:
        ms = jnp.mean(y * y, axis=-1, keepdims=True)
        n_ref[...] = y * lax.rsqrt(ms + EPS) * fg_ref[...]


def _residual(y, x, gate, final_gain=None):
    final = final_gain is not None
    tile = pl.BlockSpec((TM, D_MODEL), lambda i: (i, 0))
    in_specs = [tile, tile, pl.BlockSpec((None, 1, D_MODEL), lambda i: (_mod_row(i), 0, 0))]
    args = [y, x, gate]
    out_shape = [jax.ShapeDtypeStruct((T_ALL, D_MODEL), F32)]
    out_specs = [tile]
    if final:
        in_specs.append(pl.BlockSpec((1, D_MODEL), lambda i: (0, 0)))
        args.append(final_gain.reshape(1, D_MODEL))
        out_shape.append(jax.ShapeDtypeStruct((T_ALL, D_MODEL), F32))
        out_specs.append(tile)
    return pl.pallas_call(
        functools.partial(_residual_kernel, final=final),
        out_shape=out_shape,
        grid=(N_TILES,),
        in_specs=in_specs,
        out_specs=out_specs,
        compiler_params=_params(("parallel",)),
        name="residual",
    )(*args)


def _moe_layer(layer, y, gain, scale, shift, gate, rw, rb, w_gu, b_gu, w_down, b_down, final_gain=None):
    h, top_i, top_p, rank, counts = _router(y, gain, scale, shift, rw, rb)
    top_i = top_i[:, :TOP_K]
    top_p = top_p[:, :TOP_K]
    pos, slot_token, tile_expert, n_active = _route_plan(top_i, rank[:, :TOP_K],
                                                         counts[0, :N_EXPERTS].astype(jnp.int32))
    xs = jnp.take(h, slot_token, axis=0).astype(BF16)
    out = _moe_experts(layer, xs, tile_expert, n_active, w_gu, b_gu, w_down, b_down)
    moe = jnp.sum(jnp.take(out, pos.reshape(-1), axis=0).reshape(T_ALL, TOP_K, D_MODEL) * top_p[..., None], axis=1)
    return _residual(y, moe, gate, final_gain)


def _heads(x, n):
    b = x.shape[0] // SEQ
    return x.reshape(b, SEQ, n, -1).transpose(0, 2, 1, 3)


def kernel(x_prompt, x_sample, c, cache_a_k, cache_a_v, state_b_C, state_b_n, state_b_m, cache_c_k, cache_c_v, cache_d_k, cache_d_v, c_ctx, w_mod, b_mod, norm1_g, norm2_g, w_in_even, w_out_even, a_q_gain, a_k_gain, b_gate_bias, b_norm_gain, w_in_odd, w_out_odd, c_rpb, d_lambda, d_norm_gain, router_w, router_b, expert_w_gu, expert_b_gu, expert_w_down, expert_b_down, final_norm_g):
    y = jnp.concatenate([x_prompt.reshape(T_CTX, D_MODEL), x_sample.reshape(T_LAT, D_MODEL)], axis=0)
    cond = jnp.zeros((MOD_ROWS, D_MODEL), F32).at[0].set(c_ctx).at[1:1 + DEC_BATCH].set(c)
    mod = _modulation(cond, w_mod, b_mod).reshape(DEPTH, MOD_ROWS, 6, 1, D_MODEL)
    rope_cos, rope_sin = _rope_tables()
    scale = HD ** -0.5
    outs = {}

    for layer in range(DEPTH):
        sh1, sc1, g1, sh2, sc2, g2 = (mod[layer, :, k] for k in range(6))
        j = layer // 2
        if layer % 2 == 0:
            w = w_in_even[j]
            sizes = np.cumsum([0, 512, 128, 128, 512, 512, 512, 512, 16])
            aq, ak, av, bq, bk, bv, bo, bg = (w[:, sizes[k]:sizes[k + 1]] for k in range(8))
            w_in = jnp.concatenate([aq, bo, bq, bk, bv, ak, av, bg, jnp.zeros((D_MODEL, EV_N - EV_BG - 16), F32)],
                                   axis=1).astype(BF16)
            p = _norm_proj(y, norm1_g[layer], sc1, sh1, w_in)
            qg = jnp.tile(a_q_gain[j], LANES // HD).reshape(1, LANES)
            kg = jnp.tile(a_k_gain[j], LANES // HD).reshape(1, LANES)
            specs = ((EV_AQ, 512, 0, True, scale, BF16), (EV_AK, 128, 1, False, 1.0, F32),
                     (EV_AK, 128, 1, True, 1.0, BF16), (EV_AV, 128, None, False, 1.0, BF16))
            qa, ka_f32, ka, va = _prep(p, rope_cos, rope_sin, [qg, kg], specs)
            oa_ctx = _attention(qa, ka, va, ctx=True, group=A_HEADS // A_KV, n_kv=A_KV)
            cache = (cache_a_k[:, j].astype(BF16), cache_a_v[:, j].astype(BF16))
            oa_lat = _attention(qa, ka, va, ctx=False, group=A_HEADS // A_KV, n_kv=A_KV, cache=cache)
            zc = jnp.zeros((BATCH, 2, B_HEADS, B_DK, B_DV), F32)
            zn = jnp.zeros((BATCH, 2, B_HEADS, 1, B_DK), F32)
            zm = jnp.zeros((BATCH, 2, B_HEADS, 1, LANES), F32)
            hb_ctx, bC, bn, bm = _mlstm(p, b_gate_bias[j], zc, zn, zm, ctx=True)
            m0 = jnp.broadcast_to(state_b_m[:, j][..., None, None], (DEC_BATCH, 2, B_HEADS, 1, LANES))
            hb_lat, _, _, _ = _mlstm(p, b_gate_bias[j], state_b_C[:, j], state_b_n[:, j][:, :, :, None, :], m0,
                                     ctx=False)
            oa = jnp.concatenate([oa_ctx, oa_lat], axis=0)
            hb = jnp.concatenate([hb_ctx, hb_lat], axis=0)
            y = _merge(y, g1, w_out_even[j].astype(BF16), oa, hb, p=p, norm_gain=b_norm_gain[j])
            outs.setdefault("a_k", []).append(_heads(ka_f32[:T_CTX], A_KV))
            outs.setdefault("a_v", []).append(_heads(p[:T_CTX, EV_AV:EV_AV + 128], A_KV))
            outs.setdefault("b_C", []).append(bC)
            outs.setdefault("b_n", []).append(bn[:, :, :, 0, :])
            outs.setdefault("b_m", []).append(bm[:, :, :, 0, 0])
        else:
            p = _norm_proj(y, norm1_g[layer], sc1, sh1, w_in_odd[j].astype(BF16))
            specs = ((0, 512, None, False, scale, BF16), (512, 512, None, False, 1.0, BF16),
                     (1024, 512, None, False, 1.0, BF16), (1536, 512, None, True, scale, BF16),
                     (2048, 512, None, True, 1.0, BF16), (2560, 512, None, False, 1.0, BF16))
            qc, kc, vc, qd, kd, vd = _prep(p, rope_cos, rope_sin, [], specs)
            lam_init = 0.8 - 0.6 * math.exp(-0.3 * layer)
            lp = d_lambda[j].astype(F32)
            lam = jnp.exp(jnp.sum(lp[0] * lp[1])) - jnp.exp(jnp.sum(lp[2] * lp[3])) + lam_init
            lam_vec = jnp.stack([lam, jnp.asarray(1.0 - lam_init, F32)]).astype(F32)
            diff = (lam_vec, d_norm_gain[j])
            oc_ctx = _attention(qc, kc, vc, ctx=True, n_kv=C_HEADS)
            od_ctx = _attention(qd, kd, vd, ctx=True, diff=diff)
            bias = _neighbourhood_bias(c_rpb[j])
            oc_lat = _attention(qc, kc, vc, ctx=False, n_kv=C_HEADS, bias=bias,
                                cache=(cache_c_k[:, j].astype(BF16), cache_c_v[:, j].astype(BF16)))
            kd_cache = cache_d_k[:, j].reshape(DEC_BATCH, 2 * D_HEADS, PAST_LEN, HD).astype(BF16)
            od_lat = _attention(qd, kd, vd, ctx=False, diff=diff, cache=(kd_cache, cache_d_v[:, j].astype(BF16)))
            oc = jnp.concatenate([oc_ctx, oc_lat], axis=0)
            od = jnp.concatenate([od_ctx, od_lat], axis=0)
            y = _merge(y, g1, w_out_odd[j].astype(BF16), oc, od)
            pc = p[:T_CTX]
            outs.setdefault("c_k", []).append(_heads(pc[:, 512:1024], C_HEADS))
            outs.setdefault("c_v", []).append(_heads(pc[:, 1024:1536], C_HEADS))
            outs.setdefault("d_k", []).append(_heads(pc[:, 2048:2560], 2 * D_HEADS).reshape(BATCH, D_HEADS, 2, SEQ, HD))
            outs.setdefault("d_v", []).append(_heads(pc[:, 2560:3072], D_HEADS))
        res = _moe_layer(layer, y, norm2_g[layer], sc2, sh2, g2, router_w[layer], router_b[layer],
                         expert_w_gu, expert_b_gu, expert_w_down, expert_b_down,
                         final_gain=final_norm_g if layer == DEPTH - 1 else None)
        y = res[0]
    y_norm = res[1]
    stack = lambda k: jnp.stack(outs[k], axis=1)
    return (y_norm[:T_CTX].reshape(BATCH, SEQ, D_MODEL), y_norm[T_CTX:].reshape(DEC_BATCH, DEC_SEQ, D_MODEL),
            stack("a_k"), stack("a_v"), stack("b_C"), stack("b_n"), stack("b_m"),
            stack("c_k"), stack("c_v"), stack("d_k"), stack("d_v"))
```

```python
import functools
import math

import numpy as np
import jax
import jax.numpy as jnp
from jax import lax
from jax.experimental import pallas as pl
from jax.experimental.pallas import tpu as pltpu

D_MODEL = 1024
BATCH = 32
SEQ = 256
DEPTH = 2
DEC_BATCH = 8
DEC_SEQ = 1024
PAST_LEN = 512
GRID_W = 64
HD = 64
A_HEADS = 8
A_KV = 2
B_HEADS = 4
B_DK = 128
B_DV = 128
B_CHUNK = 128
C_HEADS = 8
NA_ROWS = 8
NA_COLS = 16
D_HEADS = 4
D_VDIM = 2 * HD
N_EXPERTS = 32
TOP_K = 4
D_FF = 1024
SWIGLU_LIMIT = 7.0
SWIGLU_ALPHA = 1.702
ROPE_THETA = 10000.0
EPS = 1e-6

F32 = jnp.float32
BF16 = jnp.bfloat16
HIGHEST = lax.Precision.HIGHEST

T_CTX = BATCH * SEQ
T_LAT = DEC_BATCH * DEC_SEQ
T_ALL = T_CTX + T_LAT
TM = 256
CTX_TILES = T_CTX // TM
LAT_TILES_PER_BATCH = DEC_SEQ // TM
N_TILES = T_ALL // TM
MOD_ROWS = 16
LANES = 128
NEG_BIG = -1e30
MOE_TM = 256
N_ASSIGN = T_ALL * TOP_K
MOE_ROWS = N_ASSIGN + N_EXPERTS * MOE_TM
MOE_TILES = MOE_ROWS // MOE_TM
VMEM_LIMIT = 56 * 1024 * 1024

EV_AQ, EV_BO, EV_BQ, EV_BK, EV_BV, EV_AK, EV_AV, EV_BG = 0, 512, 1024, 1536, 2048, 2560, 2688, 2816
EV_N = 2944
OD_N = 3072


def _params(sem, vmem=VMEM_LIMIT):
    return pltpu.CompilerParams(dimension_semantics=sem, vmem_limit_bytes=vmem)


def _mod_row(i):
    return jnp.where(i < CTX_TILES, 0, 1 + (i - CTX_TILES) // LAT_TILES_PER_BATCH)


def _rope_block(i):
    return jnp.where(i < CTX_TILES, LAT_TILES_PER_BATCH, (i - CTX_TILES) % LAT_TILES_PER_BATCH)


def _dot(a, b, precision=None):
    return jnp.dot(a, b, preferred_element_type=F32, precision=precision)


def _dot_nt(a, b):
    return lax.dot_general(a, b, (((1,), (1,)), ((), ())), preferred_element_type=F32)


def _dot_tn(a, b):
    return lax.dot_general(a, b, (((0,), (0,)), ((), ())), preferred_element_type=F32)


def _modulation_kernel(c_ref, w_ref, b_ref, o_ref):
    c = c_ref[...]
    s = c * jax.nn.sigmoid(c)
    o_ref[...] = _dot(s, w_ref[...], HIGHEST) + b_ref[...]


def _modulation(cond, w_mod, b_mod):
    tn = 1536
    return pl.pallas_call(
        _modulation_kernel,
        out_shape=jax.ShapeDtypeStruct((DEPTH, MOD_ROWS, 6 * D_MODEL), F32),
        grid=(DEPTH, 6 * D_MODEL // tn),
        in_specs=[
            pl.BlockSpec((MOD_ROWS, D_MODEL), lambda l, j: (0, 0)),
            pl.BlockSpec((None, D_MODEL, tn), lambda l, j: (l, 0, j)),
            pl.BlockSpec((None, 1, tn), lambda l, j: (l, 0, j)),
        ],
        out_specs=pl.BlockSpec((None, MOD_ROWS, tn), lambda l, j: (l, 0, j)),
        compiler_params=_params(("parallel", "parallel")),
        name="modulation",
    )(cond, w_mod, b_mod.reshape(DEPTH, 1, 6 * D_MODEL))


def _norm_mod(y, g, sc, sh):
    ms = jnp.mean(y * y, axis=-1, keepdims=True)
    return (y * lax.rsqrt(ms + EPS) * g) * (1.0 + sc) + sh


def _norm_proj_kernel(y_ref, g_ref, sc_ref, sh_ref, w_ref, o_ref):
    h = _norm_mod(y_ref[...], g_ref[...], sc_ref[...], sh_ref[...])
    o_ref[...] = _dot(h.astype(BF16), w_ref[...])


def _norm_proj(y, gain, scale, shift, w_bf16):
    n = w_bf16.shape[1]
    vec = pl.BlockSpec((None, 1, D_MODEL), lambda i: (_mod_row(i), 0, 0))
    return pl.pallas_call(
        _norm_proj_kernel,
        out_shape=jax.ShapeDtypeStruct((T_ALL, n), F32),
        grid=(N_TILES,),
        in_specs=[
            pl.BlockSpec((TM, D_MODEL), lambda i: (i, 0)),
            pl.BlockSpec((1, D_MODEL), lambda i: (0, 0)),
            vec, vec,
            pl.BlockSpec((D_MODEL, n), lambda i: (0, 0)),
        ],
        out_specs=pl.BlockSpec((TM, n), lambda i: (i, 0)),
        compiler_params=_params(("parallel",)),
        name="norm_proj",
    )(y, gain.reshape(1, D_MODEL), scale, shift, w_bf16)


def _rope_rotate(x):
    w = x.shape[-1]
    lane = lax.broadcasted_iota(jnp.int32, x.shape, 1)
    nxt = pltpu.roll(x, w - 1, 1)
    prv = pltpu.roll(x, 1, 1)
    return jnp.where((lane & 1) == 0, -nxt, prv)


def _prep_kernel(*refs, specs, n_gain):
    p_ref, cos_ref, sin_ref, bd_ref = refs[:4]
    gain_refs = refs[4:4 + n_gain]
    out_refs = refs[4 + n_gain:]
    cos = cos_ref[...]
    sin = sin_ref[...]
    for (col, width, gi, rope, scale, _), o_ref in zip(specs, out_refs):
        for c0 in range(0, width, LANES):
            x = p_ref[:, col + c0:col + c0 + LANES]
            if gi is not None:
                ss = _dot(x * x, bd_ref[...], HIGHEST)
                x = x * lax.rsqrt(ss * (1.0 / HD) + EPS) * gain_refs[gi][...]
            if rope:
                x = x * cos + _rope_rotate(x) * sin
            if scale != 1.0:
                x = x * scale
            o_ref[:, c0:c0 + LANES] = x.astype(o_ref.dtype)


def _prep(p, rope_cos, rope_sin, gains, specs):
    n = p.shape[1]
    bd = jnp.asarray(np.kron(np.eye(LANES // HD), np.ones((HD, HD))), F32)
    rope_spec = pl.BlockSpec((TM, LANES), lambda i: (_rope_block(i), 0))
    in_specs = [pl.BlockSpec((TM, n), lambda i: (i, 0)), rope_spec, rope_spec,
                pl.BlockSpec((LANES, LANES), lambda i: (0, 0))]
    in_specs += [pl.BlockSpec((1, LANES), lambda i: (0, 0)) for _ in gains]
    return pl.pallas_call(
        functools.partial(_prep_kernel, specs=specs, n_gain=len(gains)),
        out_shape=[jax.ShapeDtypeStruct((T_ALL, s[1]), s[5]) for s in specs],
        grid=(N_TILES,),
        in_specs=in_specs,
        out_specs=[pl.BlockSpec((TM, s[1]), lambda i: (i, 0)) for s in specs],
        compiler_params=_params(("parallel",)),
        name="qkv_prep",
    )(p, rope_cos, rope_sin, bd, *gains)


def _rope_tables():
    half = HD // 2
    freqs = 1.0 / (ROPE_THETA ** (jnp.arange(0, half, 2, dtype=F32) / half))
    t = jnp.arange(DEC_SEQ)
    rows = (t // GRID_W).astype(F32)
    cols = (t % GRID_W).astype(F32)
    ang = jnp.concatenate([rows[:, None] * freqs, cols[:, None] * freqs], axis=-1)
    cos = jnp.repeat(jnp.cos(ang), 2, axis=-1)
    sin = jnp.repeat(jnp.sin(ang), 2, axis=-1)
    cos = jnp.concatenate([jnp.tile(cos, (1, LANES // HD)), jnp.ones((TM, LANES), F32)], axis=0)
    sin = jnp.concatenate([jnp.tile(sin, (1, LANES // HD)), jnp.zeros((TM, LANES), F32)], axis=0)
    return cos, sin


def _lane_slice(ref, h, width=HD):
    per = LANES // width
    blk = ref[:, (h // per) * LANES:(h // per + 1) * LANES]
    if per == 1:
        return blk
    return blk[:, (h % per) * width:(h % per + 1) * width]


def _softmax_parts(scores):
    m = None
    for s in scores:
        ms = jnp.max(s, axis=-1, keepdims=True)
        m = ms if m is None else jnp.maximum(m, ms)
    ps = [jnp.exp(s - m) for s in scores]
    l = None
    for p in ps:
        ls = jnp.sum(p, axis=-1, keepdims=True)
        l = ls if l is None else l + ls
    return ps, l


def _attn_std_kernel(*refs, group, n_kv, has_cache, has_bias, bq):
    it = iter(refs)
    q_ref, kn_ref, vn_ref = next(it), next(it), next(it)
    kc_ref = vc_ref = b_ref = None
    if has_cache:
        kc_ref, vc_ref = next(it), next(it)
    if has_bias:
        b_ref = next(it)
    o_ref = next(it)
    outs = []
    for g in range(n_kv):
        qs = jnp.concatenate([_lane_slice(q_ref, g * group + j) for j in range(group)], axis=0)
        kn = _lane_slice(kn_ref, g)
        vn = _lane_slice(vn_ref, g)
        s_new = _dot_nt(qs, kn)
        if has_bias:
            s_new = s_new + b_ref[g]
        scores = [s_new]
        if has_cache:
            scores.append(_dot_nt(qs, kc_ref[g]))
        ps, l = _softmax_parts(scores)
        o = _dot(ps[0].astype(BF16), vn)
        if has_cache:
            o = o + _dot(ps[1].astype(BF16), vc_ref[g])
        o = o / l
        for j in range(group):
            outs.append(o[j * bq:(j + 1) * bq])
    o_ref[...] = jnp.concatenate(outs, axis=1).astype(o_ref.dtype)


def _attn_diff_kernel(*refs, has_cache):
    it = iter(refs)
    lam_ref, q_ref, kn_ref, vn_ref = next(it), next(it), next(it), next(it)
    kc_ref = vc_ref = None
    if has_cache:
        kc_ref, vc_ref = next(it), next(it)
    g_ref, o_ref = next(it), next(it)
    lam = lam_ref[0]
    post = lam_ref[1]
    outs = []
    for h in range(D_HEADS):
        pd_new, pd_c = None, None
        for j in range(2):
            f = 2 * h + j
            qs = _lane_slice(q_ref, f)
            scores = [_dot_nt(qs, _lane_slice(kn_ref, f))]
            if has_cache:
                scores.append(_dot_nt(qs, kc_ref[f]))
            ps, l = _softmax_parts(scores)
            r = 1.0 / l
            if j == 0:
                pd_new = ps[0] * r
                pd_c = ps[1] * r if has_cache else None
            else:
                r = r * lam
                pd_new = pd_new - ps[0] * r
                pd_c = pd_c - ps[1] * r if has_cache else None
        o = _dot(pd_new.astype(BF16), _lane_slice(vn_ref, h, D_VDIM))
        if has_cache:
            o = o + _dot(pd_c.astype(BF16), vc_ref[h])
        ms = jnp.mean(o * o, axis=-1, keepdims=True)
        outs.append(o * lax.rsqrt(ms + EPS) * g_ref[...] * post)
    o_ref[...] = jnp.concatenate(outs, axis=1).astype(o_ref.dtype)


def _attention(q, kn, vn, *, ctx, group=1, n_kv=1, cache=None, bias=None, diff=None, bq=256):
    if ctx:
        nb, sq, row0 = BATCH, SEQ, 0
    else:
        nb, sq, row0 = DEC_BATCH, DEC_SEQ, T_CTX
    nq = sq // bq
    qb0 = row0 // bq
    kb0 = row0 // sq
    wq, wk, wv = q.shape[1], kn.shape[1], vn.shape[1]
    in_specs = [
        pl.BlockSpec((bq, wq), lambda b, i: (qb0 + b * nq + i, 0)),
        pl.BlockSpec((sq, wk), lambda b, i: (kb0 + b, 0)),
        pl.BlockSpec((sq, wv), lambda b, i: (kb0 + b, 0)),
    ]
    args = [q, kn, vn]
    if cache is not None:
        kc, vc = cache
        in_specs += [pl.BlockSpec((None,) + kc.shape[1:], lambda b, i: (b, 0, 0, 0)),
                     pl.BlockSpec((None,) + vc.shape[1:], lambda b, i: (b, 0, 0, 0))]
        args += [kc, vc]
    if diff is None:
        if bias is not None:
            in_specs.append(pl.BlockSpec((bias.shape[0], bq, sq), lambda b, i: (0, i, 0)))
            args.append(bias)
        body = functools.partial(_attn_std_kernel, group=group, n_kv=n_kv, has_cache=cache is not None,
                                 has_bias=bias is not None, bq=bq)
    else:
        lam_vec, gain = diff
        in_specs = [pl.BlockSpec(memory_space=pltpu.SMEM)] + in_specs
        args = [lam_vec] + args
        in_specs.append(pl.BlockSpec((1, D_VDIM), lambda b, i: (0, 0)))
        args.append(gain.reshape(1, D_VDIM))
        body = functools.partial(_attn_diff_kernel, has_cache=cache is not None)
    return pl.pallas_call(
        body,
        out_shape=jax.ShapeDtypeStruct((nb * sq, 512), BF16),
        grid=(nb, nq),
        in_specs=in_specs,
        out_specs=pl.BlockSpec((bq, 512), lambda b, i: (b * nq + i, 0)),
        compiler_params=_params(("parallel", "parallel")),
        name="attention",
    )(*args)


def _neighbourhood_bias(rpb):
    rows = DEC_SEQ // GRID_W
    wr = min(NA_ROWS, rows)
    r = np.arange(rows)
    c = np.arange(GRID_W)
    r0 = np.clip(r - wr // 2, 0, rows - wr)
    c0 = np.clip(c - NA_COLS // 2, 0, GRID_W - NA_COLS)
    row_ok = (r[None, :] >= r0[:, None]) & (r[None, :] < r0[:, None] + wr)
    col_ok = (c[None, :] >= c0[:, None]) & (c[None, :] < c0[:, None] + NA_COLS)
    dr = np.clip(r[None, :] - r[:, None] + (NA_ROWS - 1), 0, 2 * NA_ROWS - 2)
    dc = np.clip(c[None, :] - c[:, None], 1 - NA_COLS, NA_COLS - 1) + (NA_COLS - 1)
    oh_r = jnp.asarray(dr[..., None] == np.arange(2 * NA_ROWS - 1), F32)
    oh_c = jnp.asarray(dc[..., None] == np.arange(2 * NA_COLS - 1), F32)
    by_col = jnp.einsum("hrd,qkd->hrqk", rpb.astype(F32), oh_c, precision=HIGHEST)
    full = jnp.einsum("PKr,hrqk->hPqKk", oh_r, by_col, precision=HIGHEST)
    ok = jnp.asarray(row_ok[:, None, :, None] & col_ok[None, :, None, :])
    return jnp.where(ok[None], full, NEG_BIG).reshape(C_HEADS, DEC_SEQ, DEC_SEQ)


def _log_sigmoid(x):
    return jnp.minimum(x, 0.0) - jnp.log1p(jnp.exp(-jnp.abs(x)))


def _mlstm_kernel(q_ref, k_ref, v_ref, g_ref, gb_ref, c0_ref, n0_ref, m0_ref, h_ref, c_ref, n_ref, m_ref, *, seq):
    L = B_CHUNK
    nc = seq // L
    row = lax.broadcasted_iota(jnp.int32, (L, L), 0)
    col = lax.broadcasted_iota(jnp.int32, (L, L), 1)
    k_scale = B_DK ** -0.5
    for d in range(2):
        keep = (col <= row) if d == 0 else (col >= row)
        tri = keep.astype(F32)
        for h in range(B_HEADS):
            ci = (2 * d) * B_HEADS + h
            cf = (2 * d + 1) * B_HEADS + h
            hs = slice(h * B_DK, (h + 1) * B_DK)

            def chunk(j, carry, d=d, keep=keep, tri=tri, ci=ci, cf=cf, hs=hs):
                C, n, m = carry
                c = j if d == 0 else nc - 1 - j
                off = pl.multiple_of(c * L, L)
                qc = q_ref[pl.ds(off, L), hs]
                kc = k_ref[pl.ds(off, L), hs] * k_scale
                vc = v_ref[pl.ds(off, L), hs]
                gates = g_ref[pl.ds(off, L), :] + gb_ref[...]
                cum = _dot(tri, _log_sigmoid(gates), HIGHEST)
                b_col = cum[:, cf:cf + 1]
                i_col = gates[:, ci:ci + 1]
                b_row = cum.T[cf:cf + 1, :]
                i_row = gates.T[ci:ci + 1, :]
                dlog = jnp.where(keep, b_col - b_row + i_row, -jnp.inf)
                inter = b_col + m
                m_t = jnp.maximum(inter, jnp.max(dlog, axis=-1, keepdims=True))
                w_intra = jnp.exp(dlog - m_t)
                w_inter = jnp.exp(inter - m_t)
                qb = qc.astype(BF16)
                qk = _dot_nt(qb, kc.astype(BF16)) * w_intra
                num = _dot(qk.astype(BF16), vc.astype(BF16)) + w_inter * _dot(qb, C.astype(BF16))
                den = jnp.sum(qk, axis=-1, keepdims=True) + w_inter * jnp.sum(qc * n, axis=-1, keepdims=True)
                h_out = num / jnp.maximum(jnp.abs(den), jnp.exp(-m_t))
                if d == 0:
                    h_ref[pl.ds(off, L), hs] = h_out
                    b_last = b_col[L - 1:L, :]
                else:
                    h_ref[pl.ds(off, L), hs] = h_ref[pl.ds(off, L), hs] + h_out
                    b_last = b_col[0:1, :]
                end_col = b_last - b_col + i_col
                m_new = jnp.maximum(b_last + m, jnp.max(end_col, axis=0, keepdims=True))
                w_end = jnp.exp(end_col - m_new)
                decay = jnp.exp(b_last + m - m_new)
                kw = kc * w_end
                C_new = decay * C + _dot_tn(kw.astype(BF16), vc.astype(BF16))
                n_new = decay * n + jnp.sum(kw, axis=0, keepdims=True)
                return C_new, n_new, m_new

            init = (c0_ref[d, h], n0_ref[d, h], m0_ref[d, h][:, 0:1])
            C, n, m = lax.fori_loop(0, nc, chunk, init)
            c_ref[d, h] = C
            n_ref[d, h] = n
            m_ref[d, h] = jnp.broadcast_to(m, (1, LANES))


def _mlstm(p, gate_bias, c0, n0, m0, *, ctx):
    if ctx:
        nb, seq, blk0 = BATCH, SEQ, 0
    else:
        nb, seq, blk0 = DEC_BATCH, DEC_SEQ, T_CTX // DEC_SEQ
    w = B_HEADS * B_DK

    def cols(c0_, width):
        return pl.BlockSpec((seq, width), lambda b: (blk0 + b, c0_ // width))

    gb = jnp.zeros((1, LANES), F32).at[0, :4 * B_HEADS].set(gate_bias.reshape(-1).astype(F32))
    st = lambda shape: pl.BlockSpec((None,) + shape, lambda b: (b, 0, 0, 0, 0))
    return pl.pallas_call(
        functools.partial(_mlstm_kernel, seq=seq),
        out_shape=[
            jax.ShapeDtypeStruct((nb * seq, w), F32),
            jax.ShapeDtypeStruct((nb, 2, B_HEADS, B_DK, B_DV), F32),
            jax.ShapeDtypeStruct((nb, 2, B_HEADS, 1, B_DK), F32),
            jax.ShapeDtypeStruct((nb, 2, B_HEADS, 1, LANES), F32),
        ],
        grid=(nb,),
        in_specs=[
            cols(EV_BQ, w), cols(EV_BK, w), cols(EV_BV, w), cols(EV_BG, LANES),
            pl.BlockSpec((1, LANES), lambda b: (0, 0)),
            st((2, B_HEADS, B_DK, B_DV)), st((2, B_HEADS, 1, B_DK)), st((2, B_HEADS, 1, LANES)),
        ],
        out_specs=[
            pl.BlockSpec((seq, w), lambda b: (b, 0)),
            st((2, B_HEADS, B_DK, B_DV)), st((2, B_HEADS, 1, B_DK)), st((2, B_HEADS, 1, LANES)),
        ],
        compiler_params=_params(("parallel",)),
        name="mlstm",
    )(p, p, p, p, gb, c0, n0, m0)


def _merge_kernel(*refs, even):
    if even:
        a_ref, hb_ref, bo_ref, ng_ref, w_ref, y_ref, g_ref, o_ref = refs
        parts = [a_ref[...]]
        for h in range(B_HEADS):
            hs = slice(h * B_DV, (h + 1) * B_DV)
            x = hb_ref[:, hs]
            ms = jnp.mean(x * x, axis=-1, keepdims=True)
            xn = x * lax.rsqrt(ms + EPS) * ng_ref[:, hs]
            parts.append((jax.nn.sigmoid(bo_ref[:, hs]) * xn).astype(BF16))
    else:
        a_ref, b_ref, w_ref, y_ref, g_ref, o_ref = refs
        parts = [a_ref[...], b_ref[...]]
    cat = jnp.concatenate(parts, axis=1)
    o_ref[...] = y_ref[...] + g_ref[...] * _dot(cat, w_ref[...])


def _merge(y, gate, w_bf16, a, b, *, p=None, norm_gain=None):
    even = p is not None
    half = pl.BlockSpec((TM, 512), lambda i: (i, 0))
    in_specs = [half, half]
    args = [a, b]
    if even:
        in_specs += [pl.BlockSpec((TM, 512), lambda i: (i, EV_BO // 512)), pl.BlockSpec((1, 512), lambda i: (0, 0))]
        args += [p, norm_gain.reshape(1, 512)]
    in_specs += [
        pl.BlockSpec((D_MODEL, D_MODEL), lambda i: (0, 0)),
        pl.BlockSpec((TM, D_MODEL), lambda i: (i, 0)),
        pl.BlockSpec((None, 1, D_MODEL), lambda i: (_mod_row(i), 0, 0)),
    ]
    args += [w_bf16, y, gate]
    return pl.pallas_call(
        functools.partial(_merge_kernel, even=even),
        out_shape=jax.ShapeDtypeStruct((T_ALL, D_MODEL), F32),
        grid=(N_TILES,),
        in_specs=in_specs,
        out_specs=pl.BlockSpec((TM, D_MODEL), lambda i: (i, 0)),
        compiler_params=_params(("parallel",)),
        name="merge",
    )(*args)


SLAB = D_MODEL // LANES


def _load_slabs(ref, rows):
    return jnp.concatenate([ref[pl.ds(c, rows, stride=SLAB), :] for c in range(SLAB)], axis=1)


def _store_slabs(ref, x):
    for c in range(SLAB):
        ref[pl.ds(c, x.shape[0], stride=SLAB), :] = x[:, c * LANES:(c + 1) * LANES]


def _slab(ref, idx):
    return ref.at[pl.ds(pl.multiple_of(idx * SLAB, SLAB), SLAB)]


def _router_kernel(y_ref, g_ref, sc_ref, sh_ref, w_ref, b_ref, h_ref, ti_ref, tp_ref, rk_ref, cnt_ref, base_ref):
    @pl.when(pl.program_id(0) == 0)
    def _():
        base_ref[...] = jnp.zeros(base_ref.shape, F32)

    h = _norm_mod(y_ref[...], g_ref[...], sc_ref[...], sh_ref[...])
    _store_slabs(h_ref, h)
    logits = _dot(h, w_ref[...], HIGHEST) + b_ref[...]
    lane = lax.broadcasted_iota(jnp.int32, logits.shape, 1)
    vals, idxs = [], []
    for _ in range(TOP_K):
        mx = jnp.max(logits, axis=-1, keepdims=True)
        ix = jnp.min(jnp.where(logits == mx, lane, LANES), axis=-1, keepdims=True)
        vals.append(mx)
        idxs.append(ix)
        logits = jnp.where(lane == ix, -jnp.inf, logits)
    es = [jnp.exp(v - vals[0]) for v in vals]
    tot = es[0] + es[1] + es[2] + es[3]
    ti = jnp.zeros(logits.shape, jnp.int32)
    tp = jnp.zeros(logits.shape, F32)
    for k in range(TOP_K):
        ti = jnp.where(lane == k, idxs[k], ti)
        tp = jnp.where(lane == k, es[k] / tot, tp)
    ti_ref[...] = ti
    tp_ref[...] = tp
    onehots = [(lane == ix).astype(F32) for ix in idxs]
    cnt = onehots[0] + onehots[1] + onehots[2] + onehots[3]
    row = lax.broadcasted_iota(jnp.int32, (TM, TM), 0)
    col = lax.broadcasted_iota(jnp.int32, (TM, TM), 1)
    before = _dot((col < row).astype(BF16), cnt.astype(BF16)) + base_ref[...]
    rk = jnp.zeros(logits.shape, jnp.int32)
    for k in range(TOP_K):
        r_k = jnp.sum(onehots[k] * before, axis=-1, keepdims=True).astype(jnp.int32)
        rk = jnp.where(lane == k, r_k, rk)
    rk_ref[...] = rk
    base_ref[...] = base_ref[...] + jnp.sum(cnt, axis=0, keepdims=True)
    cnt_ref[...] = base_ref[...]


def _router(y, gain, scale, shift, rw, rb):
    vec = pl.BlockSpec((None, 1, D_MODEL), lambda i: (_mod_row(i), 0, 0))
    rw_p = jnp.zeros((D_MODEL, LANES), F32).at[:, :N_EXPERTS].set(rw)
    rb_p = jnp.full((1, LANES), NEG_BIG, F32).at[0, :N_EXPERTS].set(rb)
    tile = lambda w: pl.BlockSpec((TM, w), lambda i: (i, 0))
    return pl.pallas_call(
        _router_kernel,
        out_shape=[jax.ShapeDtypeStruct((T_ALL * SLAB, LANES), F32),
                   jax.ShapeDtypeStruct((T_ALL, LANES), jnp.int32),
                   jax.ShapeDtypeStruct((T_ALL, LANES), F32),
                   jax.ShapeDtypeStruct((T_ALL, LANES), jnp.int32),
                   jax.ShapeDtypeStruct((1, LANES), F32)],
        grid=(N_TILES,),
        in_specs=[tile(D_MODEL), pl.BlockSpec((1, D_MODEL), lambda i: (0, 0)), vec, vec,
                  pl.BlockSpec((D_MODEL, LANES), lambda i: (0, 0)), pl.BlockSpec((1, LANES), lambda i: (0, 0))],
        out_specs=[pl.BlockSpec((TM * SLAB, LANES), lambda i: (i, 0)), tile(LANES), tile(LANES), tile(LANES),
                   pl.BlockSpec((1, LANES), lambda i: (0, 0))],
        scratch_shapes=[pltpu.VMEM((1, LANES), F32)],
        compiler_params=_params(("arbitrary",)),
        name="router",
    )(y, gain.reshape(1, D_MODEL), scale, shift, rw_p, rb_p)


def _route_plan(top_i, rank, counts):
    experts = jnp.arange(N_EXPERTS, dtype=jnp.int32)
    padded = ((counts + MOE_TM - 1) // MOE_TM) * MOE_TM
    seg_end = jnp.cumsum(padded)
    seg_start = seg_end - padded
    onehot = top_i[..., None] == experts
    pos = jnp.sum(jnp.where(onehot, seg_start, 0), axis=-1) + rank
    n_active = seg_end[-1] // MOE_TM
    fill = jnp.concatenate([seg_start + counts, padded - counts, n_active[None]]).astype(jnp.int32)
    tile_start = jnp.arange(MOE_TILES, dtype=jnp.int32) * MOE_TM
    tile_expert = jnp.sum((seg_end[None, :] <= tile_start[:, None]).astype(jnp.int32), axis=1)
    last = jnp.sum((seg_end <= (n_active - 1) * MOE_TM).astype(jnp.int32))
    tile_expert = jnp.minimum(jnp.where(tile_start < seg_end[-1], tile_expert, last), N_EXPERTS - 1)
    return (pos.reshape(-1).astype(jnp.int32), fill, tile_expert.astype(jnp.int32),
            n_active.reshape(1).astype(jnp.int32))


DMA_UNROLL = 4


def _wait_slabs(ref, n_slabs, sem):
    view = ref.at[pl.ds(0, n_slabs * SLAB)]
    pltpu.make_async_copy(view, view, sem).wait()


def _dispatch_kernel(pos_ref, fill_ref, h_ref, xs_ref, sem):
    i = pl.program_id(0)
    base = i * TM

    def issue(j, carry):
        for u in range(DMA_UNROLL):
            t = j * DMA_UNROLL + u
            for k in range(TOP_K):
                dst = _slab(xs_ref, pos_ref[(base + t) * TOP_K + k])
                pltpu.make_async_copy(_slab(h_ref, t), dst, sem).start()
        return carry

    lax.fori_loop(0, TM // DMA_UNROLL, issue, 0)

    @pl.when(i == 0)
    def _():
        def per_expert(e, total):
            start = fill_ref[e]
            n = fill_ref[N_EXPERTS + e]

            def one(r, carry):
                pltpu.make_async_copy(_slab(h_ref, 0), _slab(xs_ref, start + r), sem).start()
                return carry

            lax.fori_loop(0, n, one, 0)
            return total + n

        total = lax.fori_loop(0, N_EXPERTS, per_expert, 0)

        n_active = fill_ref[2 * N_EXPERTS]

        def unused_tile(ti, carry):
            dst = xs_ref.at[pl.ds(pl.multiple_of(ti * (MOE_TM * SLAB), MOE_TM * SLAB), MOE_TM * SLAB)]
            pltpu.make_async_copy(h_ref, dst, sem).start()
            return carry

        lax.fori_loop(n_active, MOE_TILES, unused_tile, 0)
        total = total + (MOE_TILES - n_active) * MOE_TM

        @pl.when(total > 0)
        def _():
            _wait_slabs(xs_ref, total, sem)

    _wait_slabs(xs_ref, TM * TOP_K, sem)


def _dispatch(h_slabs, pos, fill):
    grid_spec = pltpu.PrefetchScalarGridSpec(
        num_scalar_prefetch=2,
        grid=(N_TILES,),
        in_specs=[pl.BlockSpec((TM * SLAB, LANES), lambda i, pos, fill: (i, 0))],
        out_specs=pl.BlockSpec(memory_space=pl.ANY),
        scratch_shapes=[pltpu.SemaphoreType.DMA],
    )
    return pl.pallas_call(
        _dispatch_kernel,
        out_shape=jax.ShapeDtypeStruct((MOE_ROWS * SLAB, LANES), F32),
        grid_spec=grid_spec,
        compiler_params=_params(("arbitrary",)),
        name="moe_dispatch",
    )(pos, fill, h_slabs)


def _moe_kernel(te_ref, na_ref, x_ref, wgu_ref, bgu_ref, wd_ref, bd_ref, o_ref, wgu_bf, wd_bf):
    i = pl.program_id(0)
    e = te_ref[i]
    prev = te_ref[jnp.maximum(i - 1, 0)]

    @pl.when((i == 0) | (e != prev))
    def _():
        wgu_bf[...] = wgu_ref[...].astype(BF16)
        wd_bf[...] = wd_ref[...].astype(BF16)

    @pl.when(i < na_ref[0])
    def _():
        x = _load_slabs(x_ref, MOE_TM).astype(BF16)
        gu = _dot(x, wgu_bf[...]) + bgu_ref[...]
        gate = jnp.minimum(gu[:, :D_FF], SWIGLU_LIMIT)
        up = jnp.clip(gu[:, D_FF:], -SWIGLU_LIMIT, SWIGLU_LIMIT)
        act = (up + 1.0) * gate * jax.nn.sigmoid(SWIGLU_ALPHA * gate)
        _store_slabs(o_ref, _dot(act.astype(BF16), wd_bf[...]) + bd_ref[...])

    @pl.when(i >= na_ref[0])
    def _():
        o_ref[...] = jnp.zeros(o_ref.shape, o_ref.dtype)


def _moe_experts(layer, xs, tile_expert, n_active, w_gu, b_gu, w_down, b_down):
    grid_spec = pltpu.PrefetchScalarGridSpec(
        num_scalar_prefetch=2,
        grid=(MOE_TILES,),
        in_specs=[
            pl.BlockSpec((MOE_TM * SLAB, LANES), lambda i, te, na: (jnp.minimum(i, na[0] - 1), 0)),
            pl.BlockSpec((None, None, D_MODEL, 2 * D_FF), lambda i, te, na: (layer, te[i], 0, 0)),
            pl.BlockSpec((None, None, 1, 2 * D_FF), lambda i, te, na: (layer, te[i], 0, 0)),
            pl.BlockSpec((None, None, D_FF, D_MODEL), lambda i, te, na: (layer, te[i], 0, 0)),
            pl.BlockSpec((None, None, 1, D_MODEL), lambda i, te, na: (layer, te[i], 0, 0)),
        ],
        out_specs=pl.BlockSpec((MOE_TM * SLAB, LANES), lambda i, te, na: (i, 0)),
        scratch_shapes=[pltpu.VMEM((D_MODEL, 2 * D_FF), BF16), pltpu.VMEM((D_FF, D_MODEL), BF16)],
    )
    return pl.pallas_call(
        _moe_kernel,
        out_shape=jax.ShapeDtypeStruct((MOE_ROWS * SLAB, LANES), F32),
        grid_spec=grid_spec,
        compiler_params=_params(("arbitrary",)),
        name="moe_experts",
    )(tile_expert, n_active, xs, w_gu, b_gu.reshape(DEPTH, N_EXPERTS, 1, 2 * D_FF), w_down,
      b_down.reshape(DEPTH, N_EXPERTS, 1, D_MODEL))


def _combine_kernel(pos_ref, out_ref, y_ref, tp_ref, g_ref, *rest, final):
    if final:
        fg_ref, o_ref, n_ref, buf, sem = rest
    else:
        o_ref, buf, sem = rest
    base = pl.program_id(0) * TM

    def issue(j, carry):
        for u in range(DMA_UNROLL):
            t = j * DMA_UNROLL + u
            for k in range(TOP_K):
                src = _slab(out_ref, pos_ref[(base + t) * TOP_K + k])
                pltpu.make_async_copy(src, _slab(buf.at[k], t), sem).start()
        return carry

    lax.fori_loop(0, TM // DMA_UNROLL, issue, 0)
    _wait_slabs(out_ref, TM * TOP_K, sem)

    tp = tp_ref[...]
    ss = jnp.zeros((TM, 1), F32)
    chunks = []
    for c in range(SLAB):
        cs = slice(c * LANES, (c + 1) * LANES)
        acc = tp[:, 0:1] * buf[0, pl.ds(c, TM, stride=SLAB), :]
        for k in range(1, TOP_K):
            acc = acc + tp[:, k:k + 1] * buf[k, pl.ds(c, TM, stride=SLAB), :]
        yc = y_ref[:, cs] + g_ref[:, cs] * acc
        o_ref[:, cs] = yc
        if final:
            ss = ss + jnp.sum(yc * yc, axis=-1, keepdims=True)
            chunks.append(yc)
    if final:
        inv = lax.rsqrt(ss * (1.0 / D_MODEL) + EPS)
        for c in range(SLAB):
            cs = slice(c * LANES, (c + 1) * LANES)
            n_ref[:, cs] = chunks[c] * inv * fg_ref[:, cs]


def _combine(y, out_slabs, pos, top_p, gate, final_gain=None):
    final = final_gain is not None
    tile = pl.BlockSpec((TM, D_MODEL), lambda i, pos: (i, 0))
    in_specs = [pl.BlockSpec(memory_space=pl.ANY), tile, pl.BlockSpec((TM, LANES), lambda i, pos: (i, 0)),
                pl.BlockSpec((None, 1, D_MODEL), lambda i, pos: (_mod_row(i), 0, 0))]
    args = [out_slabs, y, top_p, gate]
    out_shape = [jax.ShapeDtypeStruct((T_ALL, D_MODEL), F32)]
    out_specs = [tile]
    if final:
        in_specs.append(pl.BlockSpec((1, D_MODEL), lambda i, pos: (0, 0)))
        args.append(final_gain.reshape(1, D_MODEL))
        out_shape.append(jax.ShapeDtypeStruct((T_ALL, D_MODEL), F32))
        out_specs.append(tile)
    grid_spec = pltpu.PrefetchScalarGridSpec(
        num_scalar_prefetch=1,
        grid=(N_TILES,),
        in_specs=in_specs,
        out_specs=out_specs,
        scratch_shapes=[pltpu.VMEM((TOP_K, TM * SLAB, LANES), F32), pltpu.SemaphoreType.DMA],
    )
    return pl.pallas_call(
        functools.partial(_combine_kernel, final=final),
        out_shape=out_shape,
        grid_spec=grid_spec,
        compiler_params=_params(("arbitrary",)),
        name="moe_combine",
    )(pos, *args)


def _moe_layer(layer, y, gain, scale, shift, gate, rw, rb, w_gu, b_gu, w_down, b_down, final_gain=None):
    h_slabs, top_i, top_p, rank, counts = _router(y, gain, scale, shift, rw, rb)
    pos, fill, tile_expert, n_active = _route_plan(top_i[:, :TOP_K], rank[:, :TOP_K],
                                                   counts[0, :N_EXPERTS].astype(jnp.int32))
    xs = _dispatch(h_slabs, pos, fill)
    out = _moe_experts(layer, xs, tile_expert, n_active, w_gu, b_gu, w_down, b_down)
    return _combine(y, out, pos, top_p, gate, final_gain)


def _heads(x, n):
    b = x.shape[0] // SEQ
    return x.reshape(b, SEQ, n, -1).transpose(0, 2, 1, 3)


def kernel(x_prompt, x_sample, c, cache_a_k, cache_a_v, state_b_C, state_b_n, state_b_m, cache_c_k, cache_c_v, cache_d_k, cache_d_v, c_ctx, w_mod, b_mod, norm1_g, norm2_g, w_in_even, w_out_even, a_q_gain, a_k_gain, b_gate_bias, b_norm_gain, w_in_odd, w_out_odd, c_rpb, d_lambda, d_norm_gain, router_w, router_b, expert_w_gu, expert_b_gu, expert_w_down, expert_b_down, final_norm_g):
    y = jnp.concatenate([x_prompt.reshape(T_CTX, D_MODEL), x_sample.reshape(T_LAT, D_MODEL)], axis=0)
    cond = jnp.zeros((MOD_ROWS, D_MODEL), F32).at[0].set(c_ctx).at[1:1 + DEC_BATCH].set(c)
    mod = _modulation(cond, w_mod, b_mod).reshape(DEPTH, MOD_ROWS, 6, 1, D_MODEL)
    rope_cos, rope_sin = _rope_tables()
    scale = HD ** -0.5
    outs = {}

    for layer in range(DEPTH):
        sh1, sc1, g1, sh2, sc2, g2 = (mod[layer, :, k] for k in range(6))
        j = layer // 2
        if layer % 2 == 0:
            w = w_in_even[j]
            sizes = np.cumsum([0, 512, 128, 128, 512, 512, 512, 512, 16])
            aq, ak, av, bq, bk, bv, bo, bg = (w[:, sizes[k]:sizes[k + 1]] for k in range(8))
            w_in = jnp.concatenate([aq, bo, bq, bk, bv, ak, av, bg, jnp.zeros((D_MODEL, EV_N - EV_BG - 16), F32)],
                                   axis=1).astype(BF16)
            p = _norm_proj(y, norm1_g[layer], sc1, sh1, w_in)
            qg = jnp.tile(a_q_gain[j], LANES // HD).reshape(1, LANES)
            kg = jnp.tile(a_k_gain[j], LANES // HD).reshape(1, LANES)
            specs = ((EV_AQ, 512, 0, True, scale, BF16), (EV_AK, 128, 1, False, 1.0, F32),
                     (EV_AK, 128, 1, True, 1.0, BF16), (EV_AV, 128, None, False, 1.0, BF16))
            qa, ka_f32, ka, va = _prep(p, rope_cos, rope_sin, [qg, kg], specs)
            oa_ctx = _attention(qa, ka, va, ctx=True, group=A_HEADS // A_KV, n_kv=A_KV)
            cache = (cache_a_k[:, j].astype(BF16), cache_a_v[:, j].astype(BF16))
            oa_lat = _attention(qa, ka, va, ctx=False, group=A_HEADS // A_KV, n_kv=A_KV, cache=cache)
            zc = jnp.zeros((BATCH, 2, B_HEADS, B_DK, B_DV), F32)
            zn = jnp.zeros((BATCH, 2, B_HEADS, 1, B_DK), F32)
            zm = jnp.zeros((BATCH, 2, B_HEADS, 1, LANES), F32)
            hb_ctx, bC, bn, bm = _mlstm(p, b_gate_bias[j], zc, zn, zm, ctx=True)
            m0 = jnp.broadcast_to(state_b_m[:, j][..., None, None], (DEC_BATCH, 2, B_HEADS, 1, LANES))
            hb_lat, _, _, _ = _mlstm(p, b_gate_bias[j], state_b_C[:, j], state_b_n[:, j][:, :, :, None, :], m0,
                                     ctx=False)
            oa = jnp.concatenate([oa_ctx, oa_lat], axis=0)
            hb = jnp.concatenate([hb_ctx, hb_lat], axis=0)
            y = _merge(y, g1, w_out_even[j].astype(BF16), oa, hb, p=p, norm_gain=b_norm_gain[j])
            outs.setdefault("a_k", []).append(_heads(ka_f32[:T_CTX], A_KV))
            outs.setdefault("a_v", []).append(_heads(p[:T_CTX, EV_AV:EV_AV + 128], A_KV))
            outs.setdefault("b_C", []).append(bC)
            outs.setdefault("b_n", []).append(bn[:, :, :, 0, :])
            outs.setdefault("b_m", []).append(bm[:, :, :, 0, 0])
        else:
            p = _norm_proj(y, norm1_g[layer], sc1, sh1, w_in_odd[j].astype(BF16))
            specs = ((0, 512, None, False, scale, BF16), (512, 512, None, False, 1.0, BF16),
                     (1024, 512, None, False, 1.0, BF16), (1536, 512, None, True, scale, BF16),
                     (2048, 512, None, True, 1.0, BF16), (2560, 512, None, False, 1.0, BF16))
            qc, kc, vc, qd, kd, vd = _prep(p, rope_cos, rope_sin, [], specs)
            lam_init = 0.8 - 0.6 * math.exp(-0.3 * layer)
            lp = d_lambda[j].astype(F32)
            lam = jnp.exp(jnp.sum(lp[0] * lp[1])) - jnp.exp(jnp.sum(lp[2] * lp[3])) + lam_init
            lam_vec = jnp.stack([lam, jnp.asarray(1.0 - lam_init, F32)]).astype(F32)
            diff = (lam_vec, d_norm_gain[j])
            oc_ctx = _attention(qc, kc, vc, ctx=True, n_kv=C_HEADS)
            od_ctx = _attention(qd, kd, vd, ctx=True, diff=diff)
            bias = _neighbourhood_bias(c_rpb[j])
            oc_lat = _attention(qc, kc, vc, ctx=False, n_kv=C_HEADS, bias=bias,
                                cache=(cache_c_k[:, j].astype(BF16), cache_c_v[:, j].astype(BF16)))
            kd_cache = cache_d_k[:, j].reshape(DEC_BATCH, 2 * D_HEADS, PAST_LEN, HD).astype(BF16)
            od_lat = _attention(qd, kd, vd, ctx=False, diff=diff, cache=(kd_cache, cache_d_v[:, j].astype(BF16)))
            oc = jnp.concatenate([oc_ctx, oc_lat], axis=0)
            od = jnp.concatenate([od_ctx, od_lat], axis=0)
            y = _merge(y, g1, w_out_odd[j].astype(BF16), oc, od)
            pc = p[:T_CTX]
            outs.setdefault("c_k", []).append(_heads(pc[:, 512:1024], C_HEADS))
            outs.setdefault("c_v", []).append(_heads(pc[:, 1024:1536], C_HEADS))
            outs.setdefault("d_k", []).append(_heads(pc[:, 2048:2560], 2 * D_HEADS).reshape(BATCH, D_HEADS, 2, SEQ, HD))
            outs.setdefault("d_v", []).append(_heads(pc[:, 2560:3072], D_HEADS))
        res = _moe_layer(layer, y, norm2_g[layer], sc2, sh2, g2, router_w[layer], router_b[layer],
                         expert_w_gu, expert_b_gu, expert_w_down, expert_b_down,
                         final_gain=final_norm_g if layer == DEPTH - 1 else None)
        y = res[0]
    y_norm = res[1]
    stack = lambda k: jnp.stack(outs[k], axis=1)
    return (y_norm[:T_CTX].reshape(BATCH, SEQ, D_MODEL), y_norm[T_CTX:].reshape(DEC_BATCH, DEC_SEQ, D_MODEL),
            stack("a_k"), stack("a_v"), stack("b_C"), stack("b_n"), stack("b_m"),
            stack("c_k"), stack("c_v"), stack("d_k"), stack("d_v"))
```

```python
import functools
import math

import numpy as np
import jax
import jax.numpy as jnp
from jax import lax
from jax.experimental import pallas as pl
from jax.experimental.pallas import tpu as pltpu

D_MODEL = 1024
BATCH = 32
SEQ = 256
DEPTH = 2
DEC_BATCH = 8
DEC_SEQ = 1024
PAST_LEN = 512
GRID_W = 64
HD = 64
A_HEADS = 8
A_KV = 2
B_HEADS = 4
B_DK = 128
B_DV = 128
B_CHUNK = 128
C_HEADS = 8
NA_ROWS = 8
NA_COLS = 16
D_HEADS = 4
D_VDIM = 2 * HD
N_EXPERTS = 32
TOP_K = 4
D_FF = 1024
SWIGLU_LIMIT = 7.0
SWIGLU_ALPHA = 1.702
ROPE_THETA = 10000.0
EPS = 1e-6

F32 = jnp.float32
BF16 = jnp.bfloat16
HIGHEST = lax.Precision.HIGHEST

T_CTX = BATCH * SEQ
T_LAT = DEC_BATCH * DEC_SEQ
T_ALL = T_CTX + T_LAT
TM = 256
CTX_TILES = T_CTX // TM
LAT_TILES_PER_BATCH = DEC_SEQ // TM
N_TILES = T_ALL // TM
MOD_ROWS = 16
LANES = 128
NEG_BIG = -1e30
MOE_TM = 256
N_ASSIGN = T_ALL * TOP_K
MOE_ROWS = N_ASSIGN + N_EXPERTS * MOE_TM
MOE_TILES = MOE_ROWS // MOE_TM
VMEM_LIMIT = 56 * 1024 * 1024

EV_AQ, EV_BO, EV_BQ, EV_BK, EV_BV, EV_AK, EV_AV, EV_BG = 0, 512, 1024, 1536, 2048, 2560, 2688, 2816
EV_N = 2944
OD_N = 3072


def _params(sem, vmem=VMEM_LIMIT):
    return pltpu.CompilerParams(dimension_semantics=sem, vmem_limit_bytes=vmem)


def _mod_row(i):
    return jnp.where(i < CTX_TILES, 0, 1 + (i - CTX_TILES) // LAT_TILES_PER_BATCH)


def _rope_block(i):
    return jnp.where(i < CTX_TILES, LAT_TILES_PER_BATCH, (i - CTX_TILES) % LAT_TILES_PER_BATCH)


def _dot(a, b, precision=None):
    return jnp.dot(a, b, preferred_element_type=F32, precision=precision)


def _dot_nt(a, b):
    return lax.dot_general(a, b, (((1,), (1,)), ((), ())), preferred_element_type=F32)


def _dot_tn(a, b):
    return lax.dot_general(a, b, (((0,), (0,)), ((), ())), preferred_element_type=F32)


def _modulation_kernel(c_ref, w_ref, b_ref, o_ref):
    c = c_ref[...]
    s = c * jax.nn.sigmoid(c)
    o_ref[...] = _dot(s, w_ref[...], HIGHEST) + b_ref[...]


def _modulation(cond, w_mod, b_mod):
    tn = 1536
    return pl.pallas_call(
        _modulation_kernel,
        out_shape=jax.ShapeDtypeStruct((DEPTH, MOD_ROWS, 6 * D_MODEL), F32),
        grid=(DEPTH, 6 * D_MODEL // tn),
        in_specs=[
            pl.BlockSpec((MOD_ROWS, D_MODEL), lambda l, j: (0, 0)),
            pl.BlockSpec((None, D_MODEL, tn), lambda l, j: (l, 0, j)),
            pl.BlockSpec((None, 1, tn), lambda l, j: (l, 0, j)),
        ],
        out_specs=pl.BlockSpec((None, MOD_ROWS, tn), lambda l, j: (l, 0, j)),
        compiler_params=_params(("parallel", "parallel")),
        name="modulation",
    )(cond, w_mod, b_mod.reshape(DEPTH, 1, 6 * D_MODEL))


def _norm_mod(y, g, sc, sh):
    ms = jnp.mean(y * y, axis=-1, keepdims=True)
    return (y * lax.rsqrt(ms + EPS) * g) * (1.0 + sc) + sh


def _norm_proj_kernel(y_ref, g_ref, sc_ref, sh_ref, w_ref, o_ref):
    h = _norm_mod(y_ref[...], g_ref[...], sc_ref[...], sh_ref[...])
    o_ref[...] = _dot(h.astype(BF16), w_ref[...])


def _norm_proj(y, gain, scale, shift, w_bf16):
    n = w_bf16.shape[1]
    vec = pl.BlockSpec((None, 1, D_MODEL), lambda i: (_mod_row(i), 0, 0))
    return pl.pallas_call(
        _norm_proj_kernel,
        out_shape=jax.ShapeDtypeStruct((T_ALL, n), F32),
        grid=(N_TILES,),
        in_specs=[
            pl.BlockSpec((TM, D_MODEL), lambda i: (i, 0)),
            pl.BlockSpec((1, D_MODEL), lambda i: (0, 0)),
            vec, vec,
            pl.BlockSpec((D_MODEL, n), lambda i: (0, 0)),
        ],
        out_specs=pl.BlockSpec((TM, n), lambda i: (i, 0)),
        compiler_params=_params(("parallel",)),
        name="norm_proj",
    )(y, gain.reshape(1, D_MODEL), scale, shift, w_bf16)


def _rope_rotate(x):
    w = x.shape[-1]
    lane = lax.broadcasted_iota(jnp.int32, x.shape, 1)
    nxt = pltpu.roll(x, w - 1, 1)
    prv = pltpu.roll(x, 1, 1)
    return jnp.where((lane & 1) == 0, -nxt, prv)


def _prep_kernel(*refs, specs, n_gain):
    p_ref, cos_ref, sin_ref, bd_ref = refs[:4]
    gain_refs = refs[4:4 + n_gain]
    out_refs = refs[4 + n_gain:]
    cos = cos_ref[...]
    sin = sin_ref[...]
    for (col, width, gi, rope, scale, _), o_ref in zip(specs, out_refs):
        for c0 in range(0, width, LANES):
            x = p_ref[:, col + c0:col + c0 + LANES]
            if gi is not None:
                ss = _dot(x * x, bd_ref[...], HIGHEST)
                x = x * lax.rsqrt(ss * (1.0 / HD) + EPS) * gain_refs[gi][...]
            if rope:
                x = x * cos + _rope_rotate(x) * sin
            if scale != 1.0:
                x = x * scale
            o_ref[:, c0:c0 + LANES] = x.astype(o_ref.dtype)


def _prep(p, rope_cos, rope_sin, gains, specs):
    n = p.shape[1]
    bd = jnp.asarray(np.kron(np.eye(LANES // HD), np.ones((HD, HD))), F32)
    rope_spec = pl.BlockSpec((TM, LANES), lambda i: (_rope_block(i), 0))
    in_specs = [pl.BlockSpec((TM, n), lambda i: (i, 0)), rope_spec, rope_spec,
                pl.BlockSpec((LANES, LANES), lambda i: (0, 0))]
    in_specs += [pl.BlockSpec((1, LANES), lambda i: (0, 0)) for _ in gains]
    return pl.pallas_call(
        functools.partial(_prep_kernel, specs=specs, n_gain=len(gains)),
        out_shape=[jax.ShapeDtypeStruct((T_ALL, s[1]), s[5]) for s in specs],
        grid=(N_TILES,),
        in_specs=in_specs,
        out_specs=[pl.BlockSpec((TM, s[1]), lambda i: (i, 0)) for s in specs],
        compiler_params=_params(("parallel",)),
        name="qkv_prep",
    )(p, rope_cos, rope_sin, bd, *gains)


def _rope_tables():
    half = HD // 2
    freqs = 1.0 / (ROPE_THETA ** (jnp.arange(0, half, 2, dtype=F32) / half))
    t = jnp.arange(DEC_SEQ)
    rows = (t // GRID_W).astype(F32)
    cols = (t % GRID_W).astype(F32)
    ang = jnp.concatenate([rows[:, None] * freqs, cols[:, None] * freqs], axis=-1)
    cos = jnp.repeat(jnp.cos(ang), 2, axis=-1)
    sin = jnp.repeat(jnp.sin(ang), 2, axis=-1)
    cos = jnp.concatenate([jnp.tile(cos, (1, LANES // HD)), jnp.ones((TM, LANES), F32)], axis=0)
    sin = jnp.concatenate([jnp.tile(sin, (1, LANES // HD)), jnp.zeros((TM, LANES), F32)], axis=0)
    return cos, sin


def _lane_slice(ref, h, width=HD):
    per = LANES // width
    blk = ref[:, (h // per) * LANES:(h // per + 1) * LANES]
    if per == 1:
        return blk
    return blk[:, (h % per) * width:(h % per + 1) * width]


def _softmax_parts(scores):
    m = None
    for s in scores:
        ms = jnp.max(s, axis=-1, keepdims=True)
        m = ms if m is None else jnp.maximum(m, ms)
    ps = [jnp.exp(s - m) for s in scores]
    l = None
    for p in ps:
        ls = jnp.sum(p, axis=-1, keepdims=True)
        l = ls if l is None else l + ls
    return ps, l


def _attn_std_kernel(*refs, group, n_kv, has_cache, has_bias, bq):
    it = iter(refs)
    q_ref, kn_ref, vn_ref = next(it), next(it), next(it)
    kc_ref = vc_ref = b_ref = None
    if has_cache:
        kc_ref, vc_ref = next(it), next(it)
    if has_bias:
        b_ref = next(it)
    o_ref = next(it)
    outs = []
    for g in range(n_kv):
        qs = jnp.concatenate([_lane_slice(q_ref, g * group + j) for j in range(group)], axis=0)
        kn = _lane_slice(kn_ref, g)
        vn = _lane_slice(vn_ref, g)
        s_new = _dot_nt(qs, kn)
        if has_bias:
            s_new = s_new + b_ref[g]
        scores = [s_new]
        if has_cache:
            scores.append(_dot_nt(qs, kc_ref[g]))
        ps, l = _softmax_parts(scores)
        o = _dot(ps[0].astype(BF16), vn)
        if has_cache:
            o = o + _dot(ps[1].astype(BF16), vc_ref[g])
        o = o / l
        for j in range(group):
            outs.append(o[j * bq:(j + 1) * bq])
    o_ref[...] = jnp.concatenate(outs, axis=1).astype(o_ref.dtype)


def _attn_diff_kernel(*refs, has_cache):
    it = iter(refs)
    lam_ref, q_ref, kn_ref, vn_ref = next(it), next(it), next(it), next(it)
    kc_ref = vc_ref = None
    if has_cache:
        kc_ref, vc_ref = next(it), next(it)
    g_ref, o_ref = next(it), next(it)
    lam = lam_ref[0]
    post = lam_ref[1]
    outs = []
    for h in range(D_HEADS):
        pd_new, pd_c = None, None
        for j in range(2):
            f = 2 * h + j
            qs = _lane_slice(q_ref, f)
            scores = [_dot_nt(qs, _lane_slice(kn_ref, f))]
            if has_cache:
                scores.append(_dot_nt(qs, kc_ref[f]))
            ps, l = _softmax_parts(scores)
            r = 1.0 / l
            if j == 0:
                pd_new = ps[0] * r
                pd_c = ps[1] * r if has_cache else None
            else:
                r = r * lam
                pd_new = pd_new - ps[0] * r
                pd_c = pd_c - ps[1] * r if has_cache else None
        o = _dot(pd_new.astype(BF16), _lane_slice(vn_ref, h, D_VDIM))
        if has_cache:
            o = o + _dot(pd_c.astype(BF16), vc_ref[h])
        ms = jnp.mean(o * o, axis=-1, keepdims=True)
        outs.append(o * lax.rsqrt(ms + EPS) * g_ref[...] * post)
    o_ref[...] = jnp.concatenate(outs, axis=1).astype(o_ref.dtype)


def _attention(q, kn, vn, *, ctx, group=1, n_kv=1, cache=None, bias=None, diff=None, bq=256):
    if ctx:
        nb, sq, row0 = BATCH, SEQ, 0
    else:
        nb, sq, row0 = DEC_BATCH, DEC_SEQ, T_CTX
    nq = sq // bq
    qb0 = row0 // bq
    kb0 = row0 // sq
    wq, wk, wv = q.shape[1], kn.shape[1], vn.shape[1]
    in_specs = [
        pl.BlockSpec((bq, wq), lambda b, i: (qb0 + b * nq + i, 0)),
        pl.BlockSpec((sq, wk), lambda b, i: (kb0 + b, 0)),
        pl.BlockSpec((sq, wv), lambda b, i: (kb0 + b, 0)),
    ]
    args = [q, kn, vn]
    if cache is not None:
        kc, vc = cache
        in_specs += [pl.BlockSpec((None,) + kc.shape[1:], lambda b, i: (b, 0, 0, 0)),
                     pl.BlockSpec((None,) + vc.shape[1:], lambda b, i: (b, 0, 0, 0))]
        args += [kc, vc]
    if diff is None:
        if bias is not None:
            in_specs.append(pl.BlockSpec((bias.shape[0], bq, sq), lambda b, i: (0, i, 0)))
            args.append(bias)
        body = functools.partial(_attn_std_kernel, group=group, n_kv=n_kv, has_cache=cache is not None,
                                 has_bias=bias is not None, bq=bq)
    else:
        lam_vec, gain = diff
        in_specs = [pl.BlockSpec(memory_space=pltpu.SMEM)] + in_specs
        args = [lam_vec] + args
        in_specs.append(pl.BlockSpec((1, D_VDIM), lambda b, i: (0, 0)))
        args.append(gain.reshape(1, D_VDIM))
        body = functools.partial(_attn_diff_kernel, has_cache=cache is not None)
    return pl.pallas_call(
        body,
        out_shape=jax.ShapeDtypeStruct((nb * sq, 512), BF16),
        grid=(nb, nq),
        in_specs=in_specs,
        out_specs=pl.BlockSpec((bq, 512), lambda b, i: (b * nq + i, 0)),
        compiler_params=_params(("parallel", "parallel")),
        name="attention",
    )(*args)


def _neighbourhood_bias(rpb):
    rows = DEC_SEQ // GRID_W
    wr = min(NA_ROWS, rows)
    r = np.arange(rows)
    c = np.arange(GRID_W)
    r0 = np.clip(r - wr // 2, 0, rows - wr)
    c0 = np.clip(c - NA_COLS // 2, 0, GRID_W - NA_COLS)
    row_ok = (r[None, :] >= r0[:, None]) & (r[None, :] < r0[:, None] + wr)
    col_ok = (c[None, :] >= c0[:, None]) & (c[None, :] < c0[:, None] + NA_COLS)
    dr = np.clip(r[None, :] - r[:, None] + (NA_ROWS - 1), 0, 2 * NA_ROWS - 2)
    dc = np.clip(c[None, :] - c[:, None], 1 - NA_COLS, NA_COLS - 1) + (NA_COLS - 1)
    oh_r = jnp.asarray(dr[..., None] == np.arange(2 * NA_ROWS - 1), F32)
    oh_c = jnp.asarray(dc[..., None] == np.arange(2 * NA_COLS - 1), F32)
    by_col = jnp.einsum("hrd,qkd->hrqk", rpb.astype(F32), oh_c, precision=HIGHEST)
    full = jnp.einsum("PKr,hrqk->hPqKk", oh_r, by_col, precision=HIGHEST)
    ok = jnp.asarray(row_ok[:, None, :, None] & col_ok[None, :, None, :])
    return jnp.where(ok[None], full, NEG_BIG).reshape(C_HEADS, DEC_SEQ, DEC_SEQ)


def _log_sigmoid(x):
    return jnp.minimum(x, 0.0) - jnp.log1p(jnp.exp(-jnp.abs(x)))


def _mlstm_kernel(q_ref, k_ref, v_ref, g_ref, gb_ref, c0_ref, n0_ref, m0_ref, h_ref, c_ref, n_ref, m_ref, *, seq):
    L = B_CHUNK
    nc = seq // L
    row = lax.broadcasted_iota(jnp.int32, (L, L), 0)
    col = lax.broadcasted_iota(jnp.int32, (L, L), 1)
    k_scale = B_DK ** -0.5
    for d in range(2):
        keep = (col <= row) if d == 0 else (col >= row)
        tri = keep.astype(F32)
        for h in range(B_HEADS):
            ci = (2 * d) * B_HEADS + h
            cf = (2 * d + 1) * B_HEADS + h
            hs = slice(h * B_DK, (h + 1) * B_DK)

            def chunk(j, carry, d=d, keep=keep, tri=tri, ci=ci, cf=cf, hs=hs):
                C, n, m = carry
                c = j if d == 0 else nc - 1 - j
                off = pl.multiple_of(c * L, L)
                qc = q_ref[pl.ds(off, L), hs]
                kc = k_ref[pl.ds(off, L), hs] * k_scale
                vc = v_ref[pl.ds(off, L), hs]
                gates = g_ref[pl.ds(off, L), :] + gb_ref[...]
                cum = _dot(tri, _log_sigmoid(gates), HIGHEST)
                b_col = cum[:, cf:cf + 1]
                i_col = gates[:, ci:ci + 1]
                b_row = cum.T[cf:cf + 1, :]
                i_row = gates.T[ci:ci + 1, :]
                dlog = jnp.where(keep, b_col - b_row + i_row, -jnp.inf)
                inter = b_col + m
                m_t = jnp.maximum(inter, jnp.max(dlog, axis=-1, keepdims=True))
                w_intra = jnp.exp(dlog - m_t)
                w_inter = jnp.exp(inter - m_t)
                qb = qc.astype(BF16)
                qk = _dot_nt(qb, kc.astype(BF16)) * w_intra
                num = _dot(qk.astype(BF16), vc.astype(BF16)) + w_inter * _dot(qb, C.astype(BF16))
                den = jnp.sum(qk, axis=-1, keepdims=True) + w_inter * jnp.sum(qc * n, axis=-1, keepdims=True)
                h_out = num / jnp.maximum(jnp.abs(den), jnp.exp(-m_t))
                if d == 0:
                    h_ref[pl.ds(off, L), hs] = h_out
                    b_last = b_col[L - 1:L, :]
                else:
                    h_ref[pl.ds(off, L), hs] = h_ref[pl.ds(off, L), hs] + h_out
                    b_last = b_col[0:1, :]
                end_col = b_last - b_col + i_col
                m_new = jnp.maximum(b_last + m, jnp.max(end_col, axis=0, keepdims=True))
                w_end = jnp.exp(end_col - m_new)
                decay = jnp.exp(b_last + m - m_new)
                kw = kc * w_end
                C_new = decay * C + _dot_tn(kw.astype(BF16), vc.astype(BF16))
                n_new = decay * n + jnp.sum(kw, axis=0, keepdims=True)
                return C_new, n_new, m_new

            init = (c0_ref[d, h], n0_ref[d, h], m0_ref[d, h][:, 0:1])
            C, n, m = lax.fori_loop(0, nc, chunk, init)
            c_ref[d, h] = C
            n_ref[d, h] = n
            m_ref[d, h] = jnp.broadcast_to(m, (1, LANES))


def _mlstm(p, gate_bias, c0, n0, m0, *, ctx):
    if ctx:
        nb, seq, blk0 = BATCH, SEQ, 0
    else:
        nb, seq, blk0 = DEC_BATCH, DEC_SEQ, T_CTX // DEC_SEQ
    w = B_HEADS * B_DK

    def cols(c0_, width):
        return pl.BlockSpec((seq, width), lambda b: (blk0 + b, c0_ // width))

    gb = jnp.zeros((1, LANES), F32).at[0, :4 * B_HEADS].set(gate_bias.reshape(-1).astype(F32))
    st = lambda shape: pl.BlockSpec((None,) + shape, lambda b: (b, 0, 0, 0, 0))
    return pl.pallas_call(
        functools.partial(_mlstm_kernel, seq=seq),
        out_shape=[
            jax.ShapeDtypeStruct((nb * seq, w), F32),
            jax.ShapeDtypeStruct((nb, 2, B_HEADS, B_DK, B_DV), F32),
            jax.ShapeDtypeStruct((nb, 2, B_HEADS, 1, B_DK), F32),
            jax.ShapeDtypeStruct((nb, 2, B_HEADS, 1, LANES), F32),
        ],
        grid=(nb,),
        in_specs=[
            cols(EV_BQ, w), cols(EV_BK, w), cols(EV_BV, w), cols(EV_BG, LANES),
            pl.BlockSpec((1, LANES), lambda b: (0, 0)),
            st((2, B_HEADS, B_DK, B_DV)), st((2, B_HEADS, 1, B_DK)), st((2, B_HEADS, 1, LANES)),
        ],
        out_specs=[
            pl.BlockSpec((seq, w), lambda b: (b, 0)),
            st((2, B_HEADS, B_DK, B_DV)), st((2, B_HEADS, 1, B_DK)), st((2, B_HEADS, 1, LANES)),
        ],
        compiler_params=_params(("parallel",)),
        name="mlstm",
    )(p, p, p, p, gb, c0, n0, m0)


def _merge_kernel(*refs, even):
    if even:
        a_ref, hb_ref, bo_ref, ng_ref, w_ref, y_ref, g_ref, o_ref = refs
        parts = [a_ref[...]]
        for h in range(B_HEADS):
            hs = slice(h * B_DV, (h + 1) * B_DV)
            x = hb_ref[:, hs]
            ms = jnp.mean(x * x, axis=-1, keepdims=True)
            xn = x * lax.rsqrt(ms + EPS) * ng_ref[:, hs]
            parts.append((jax.nn.sigmoid(bo_ref[:, hs]) * xn).astype(BF16))
    else:
        a_ref, b_ref, w_ref, y_ref, g_ref, o_ref = refs
        parts = [a_ref[...], b_ref[...]]
    cat = jnp.concatenate(parts, axis=1)
    o_ref[...] = y_ref[...] + g_ref[...] * _dot(cat, w_ref[...])


def _merge(y, gate, w_bf16, a, b, *, p=None, norm_gain=None):
    even = p is not None
    half = pl.BlockSpec((TM, 512), lambda i: (i, 0))
    in_specs = [half, half]
    args = [a, b]
    if even:
        in_specs += [pl.BlockSpec((TM, 512), lambda i: (i, EV_BO // 512)), pl.BlockSpec((1, 512), lambda i: (0, 0))]
        args += [p, norm_gain.reshape(1, 512)]
    in_specs += [
        pl.BlockSpec((D_MODEL, D_MODEL), lambda i: (0, 0)),
        pl.BlockSpec((TM, D_MODEL), lambda i: (i, 0)),
        pl.BlockSpec((None, 1, D_MODEL), lambda i: (_mod_row(i), 0, 0)),
    ]
    args += [w_bf16, y, gate]
    return pl.pallas_call(
        functools.partial(_merge_kernel, even=even),
        out_shape=jax.ShapeDtypeStruct((T_ALL, D_MODEL), F32),
        grid=(N_TILES,),
        in_specs=in_specs,
        out_specs=pl.BlockSpec((TM, D_MODEL), lambda i: (i, 0)),
        compiler_params=_params(("parallel",)),
        name="merge",
    )(*args)


SLAB = D_MODEL // LANES


def _load_slabs(ref, rows):
    return jnp.concatenate([ref[pl.ds(c, rows, stride=SLAB), :] for c in range(SLAB)], axis=1)


def _store_slabs(ref, x):
    for c in range(SLAB):
        ref[pl.ds(c, x.shape[0], stride=SLAB), :] = x[:, c * LANES:(c + 1) * LANES]


def _slab(ref, idx):
    return ref.at[pl.ds(pl.multiple_of(idx * SLAB, SLAB), SLAB)]


def _router_kernel(y_ref, g_ref, sc_ref, sh_ref, w_ref, b_ref, h_ref, ti_ref, tp_ref, rk_ref, cnt_ref, base_ref):
    @pl.when(pl.program_id(0) == 0)
    def _():
        base_ref[...] = jnp.zeros(base_ref.shape, F32)

    h = _norm_mod(y_ref[...], g_ref[...], sc_ref[...], sh_ref[...])
    _store_slabs(h_ref, h)
    logits = _dot(h, w_ref[...], HIGHEST) + b_ref[...]
    lane = lax.broadcasted_iota(jnp.int32, logits.shape, 1)
    vals, idxs = [], []
    for _ in range(TOP_K):
        mx = jnp.max(logits, axis=-1, keepdims=True)
        ix = jnp.min(jnp.where(logits == mx, lane, LANES), axis=-1, keepdims=True)
        vals.append(mx)
        idxs.append(ix)
        logits = jnp.where(lane == ix, -jnp.inf, logits)
    es = [jnp.exp(v - vals[0]) for v in vals]
    tot = es[0] + es[1] + es[2] + es[3]
    ti = jnp.zeros(logits.shape, jnp.int32)
    tp = jnp.zeros(logits.shape, F32)
    for k in range(TOP_K):
        ti = jnp.where(lane == k, idxs[k], ti)
        tp = jnp.where(lane == k, es[k] / tot, tp)
    ti_ref[...] = ti
    tp_ref[...] = tp
    onehots = [(lane == ix).astype(F32) for ix in idxs]
    cnt = onehots[0] + onehots[1] + onehots[2] + onehots[3]
    row = lax.broadcasted_iota(jnp.int32, (TM, TM), 0)
    col = lax.broadcasted_iota(jnp.int32, (TM, TM), 1)
    before = _dot((col < row).astype(BF16), cnt.astype(BF16)) + base_ref[...]
    rk = jnp.zeros(logits.shape, jnp.int32)
    for k in range(TOP_K):
        r_k = jnp.sum(onehots[k] * before, axis=-1, keepdims=True).astype(jnp.int32)
        rk = jnp.where(lane == k, r_k, rk)
    rk_ref[...] = rk
    base_ref[...] = base_ref[...] + jnp.sum(cnt, axis=0, keepdims=True)
    cnt_ref[...] = base_ref[...]


def _router(y, gain, scale, shift, rw, rb):
    vec = pl.BlockSpec((None, 1, D_MODEL), lambda i: (_mod_row(i), 0, 0))
    rw_p = jnp.zeros((D_MODEL, LANES), F32).at[:, :N_EXPERTS].set(rw)
    rb_p = jnp.full((1, LANES), NEG_BIG, F32).at[0, :N_EXPERTS].set(rb)
    tile = lambda w: pl.BlockSpec((TM, w), lambda i: (i, 0))
    return pl.pallas_call(
        _router_kernel,
        out_shape=[jax.ShapeDtypeStruct((T_ALL * SLAB, LANES), F32),
                   jax.ShapeDtypeStruct((T_ALL, LANES), jnp.int32),
                   jax.ShapeDtypeStruct((T_ALL, LANES), F32),
                   jax.ShapeDtypeStruct((T_ALL, LANES), jnp.int32),
                   jax.ShapeDtypeStruct((1, LANES), F32)],
        grid=(N_TILES,),
        in_specs=[tile(D_MODEL), pl.BlockSpec((1, D_MODEL), lambda i: (0, 0)), vec, vec,
                  pl.BlockSpec((D_MODEL, LANES), lambda i: (0, 0)), pl.BlockSpec((1, LANES), lambda i: (0, 0))],
        out_specs=[pl.BlockSpec((TM * SLAB, LANES), lambda i: (i, 0)), tile(LANES), tile(LANES), tile(LANES),
                   pl.BlockSpec((1, LANES), lambda i: (0, 0))],
        scratch_shapes=[pltpu.VMEM((1, LANES), F32)],
        compiler_params=_params(("arbitrary",)),
        name="router",
    )(y, gain.reshape(1, D_MODEL), scale, shift, rw_p, rb_p)


def _route_plan(top_i, rank, counts):
    experts = jnp.arange(N_EXPERTS, dtype=jnp.int32)
    padded = ((counts + MOE_TM - 1) // MOE_TM) * MOE_TM
    seg_end = jnp.cumsum(padded)
    seg_start = seg_end - padded
    onehot = top_i[..., None] == experts
    pos = jnp.sum(jnp.where(onehot, seg_start, 0), axis=-1) + rank
    n_active = seg_end[-1] // MOE_TM
    fill = jnp.concatenate([seg_start + counts, padded - counts, n_active[None]]).astype(jnp.int32)
    tile_start = jnp.arange(MOE_TILES, dtype=jnp.int32) * MOE_TM
    tile_expert = jnp.sum((seg_end[None, :] <= tile_start[:, None]).astype(jnp.int32), axis=1)
    last = jnp.sum((seg_end <= (n_active - 1) * MOE_TM).astype(jnp.int32))
    tile_expert = jnp.minimum(jnp.where(tile_start < seg_end[-1], tile_expert, last), N_EXPERTS - 1)
    owns = (padded > 0).astype(jnp.int32)
    run_of_expert = jnp.cumsum(owns) - 1
    run_expert = jnp.sum(jnp.where((run_of_expert[None, :] == experts[:, None]) & (owns[None, :] > 0),
                                   experts[None, :], 0), axis=1)
    runs = jnp.concatenate([run_expert, jnp.sum(owns)[None]]).astype(jnp.int32)
    tile_run = jnp.sum(jnp.where(tile_expert[:, None] == experts[None, :], run_of_expert[None, :], 0), axis=1)
    experts_plan = (tile_expert.astype(jnp.int32), n_active.reshape(1).astype(jnp.int32),
                    tile_run.astype(jnp.int32), runs)
    return pos.reshape(-1).astype(jnp.int32), fill, experts_plan


DMA_UNROLL = 4
DMA_QUEUES = 2


def _wait_slabs(ref, n_slabs, sem):
    view = ref.at[pl.ds(0, n_slabs * SLAB)]
    pltpu.make_async_copy(view, view, sem).wait()


def _dispatch_kernel(pos_ref, fill_ref, h_ref, xs_ref, sem):
    i = pl.program_id(0)
    base = i * TM

    def issue(j, carry):
        for u in range(DMA_UNROLL):
            t = j * DMA_UNROLL + u
            for k in range(TOP_K):
                dst = _slab(xs_ref, pos_ref[(base + t) * TOP_K + k])
                pltpu.make_async_copy(_slab(h_ref, t), dst, sem).start(priority=k % DMA_QUEUES)
        return carry

    lax.fori_loop(0, TM // DMA_UNROLL, issue, 0)

    @pl.when(i == 0)
    def _():
        def per_expert(e, total):
            start = fill_ref[e]
            n = fill_ref[N_EXPERTS + e]

            def one(r, carry):
                pltpu.make_async_copy(_slab(h_ref, 0), _slab(xs_ref, start + r), sem).start()
                return carry

            lax.fori_loop(0, n, one, 0)
            return total + n

        total = lax.fori_loop(0, N_EXPERTS, per_expert, 0)

        n_active = fill_ref[2 * N_EXPERTS]

        def unused_tile(ti, carry):
            dst = xs_ref.at[pl.ds(pl.multiple_of(ti * (MOE_TM * SLAB), MOE_TM * SLAB), MOE_TM * SLAB)]
            pltpu.make_async_copy(h_ref, dst, sem).start()
            return carry

        lax.fori_loop(n_active, MOE_TILES, unused_tile, 0)
        total = total + (MOE_TILES - n_active) * MOE_TM

        @pl.when(total > 0)
        def _():
            _wait_slabs(xs_ref, total, sem)

    _wait_slabs(xs_ref, TM * TOP_K, sem)


def _dispatch(h_slabs, pos, fill):
    grid_spec = pltpu.PrefetchScalarGridSpec(
        num_scalar_prefetch=2,
        grid=(N_TILES,),
        in_specs=[pl.BlockSpec((TM * SLAB, LANES), lambda i, pos, fill: (i, 0))],
        out_specs=pl.BlockSpec(memory_space=pl.ANY),
        scratch_shapes=[pltpu.SemaphoreType.DMA],
    )
    return pl.pallas_call(
        _dispatch_kernel,
        out_shape=jax.ShapeDtypeStruct((MOE_ROWS * SLAB, LANES), F32),
        grid_spec=grid_spec,
        compiler_params=_params(("arbitrary",)),
        name="moe_dispatch",
    )(pos, fill, h_slabs)


def _moe_kernel(te_ref, na_ref, ts_ref, ex_ref, x_ref, wgu_hbm, bgu_ref, wd_hbm, bd_ref, o_ref,
                wgu_f32, wd_f32, wgu_bf, wd_bf, sems, *, layer):
    i = pl.program_id(0)
    s = ts_ref[i]
    first = (i == 0) | (s != ts_ref[jnp.maximum(i - 1, 0)])

    def weight_copies(slot):
        e = ex_ref[slot]
        b = slot % 2
        return (pltpu.make_async_copy(wgu_hbm.at[layer, e], wgu_f32.at[b], sems.at[0, b]),
                pltpu.make_async_copy(wd_hbm.at[layer, e], wd_f32.at[b], sems.at[1, b]))

    @pl.when(i == 0)
    def _():
        for cp in weight_copies(0):
            cp.start()

    @pl.when(first)
    def _():
        for cp in weight_copies(s):
            cp.wait()

        @pl.when(s + 1 < ex_ref[N_EXPERTS])
        def _():
            for cp in weight_copies(s + 1):
                cp.start()

        b = s % 2
        wgu_bf[...] = wgu_f32[b].astype(BF16)
        wd_bf[...] = wd_f32[b].astype(BF16)

    @pl.when(i < na_ref[0])
    def _():
        x = _load_slabs(x_ref, MOE_TM).astype(BF16)
        gu = _dot(x, wgu_bf[...]) + bgu_ref[...]
        gate = jnp.minimum(gu[:, :D_FF], SWIGLU_LIMIT)
        up = jnp.clip(gu[:, D_FF:], -SWIGLU_LIMIT, SWIGLU_LIMIT)
        act = (up + 1.0) * gate * jax.nn.sigmoid(SWIGLU_ALPHA * gate)
        _store_slabs(o_ref, _dot(act.astype(BF16), wd_bf[...]) + bd_ref[...])

    @pl.when(i >= na_ref[0])
    def _():
        o_ref[...] = jnp.zeros(o_ref.shape, o_ref.dtype)


def _moe_experts(layer, xs, plan, w_gu, b_gu, w_down, b_down):
    grid_spec = pltpu.PrefetchScalarGridSpec(
        num_scalar_prefetch=4,
        grid=(MOE_TILES,),
        in_specs=[
            pl.BlockSpec((MOE_TM * SLAB, LANES), lambda i, te, na, ts, ex: (jnp.minimum(i, na[0] - 1), 0)),
            pl.BlockSpec(memory_space=pl.ANY),
            pl.BlockSpec((None, None, 1, 2 * D_FF), lambda i, te, na, ts, ex: (layer, te[i], 0, 0)),
            pl.BlockSpec(memory_space=pl.ANY),
            pl.BlockSpec((None, None, 1, D_MODEL), lambda i, te, na, ts, ex: (layer, te[i], 0, 0)),
        ],
        out_specs=pl.BlockSpec((MOE_TM * SLAB, LANES), lambda i, te, na, ts, ex: (i, 0)),
        scratch_shapes=[pltpu.VMEM((2, D_MODEL, 2 * D_FF), F32), pltpu.VMEM((2, D_FF, D_MODEL), F32),
                        pltpu.VMEM((D_MODEL, 2 * D_FF), BF16), pltpu.VMEM((D_FF, D_MODEL), BF16),
                        pltpu.SemaphoreType.DMA((2, 2))],
    )
    return pl.pallas_call(
        functools.partial(_moe_kernel, layer=layer),
        out_shape=jax.ShapeDtypeStruct((MOE_ROWS * SLAB, LANES), F32),
        grid_spec=grid_spec,
        compiler_params=_params(("arbitrary",)),
        name="moe_experts",
    )(*plan, xs, w_gu, b_gu.reshape(DEPTH, N_EXPERTS, 1, 2 * D_FF), w_down,
      b_down.reshape(DEPTH, N_EXPERTS, 1, D_MODEL))


def _combine_kernel(pos_ref, out_ref, y_ref, tp_ref, g_ref, *rest, final):
    if final:
        fg_ref, o_ref, n_ref, buf, sem = rest
    else:
        o_ref, buf, sem = rest
    base = pl.program_id(0) * TM

    def issue(j, carry):
        for u in range(DMA_UNROLL):
            t = j * DMA_UNROLL + u
            for k in range(TOP_K):
                src = _slab(out_ref, pos_ref[(base + t) * TOP_K + k])
                pltpu.make_async_copy(src, _slab(buf.at[k], t), sem).start(priority=k % DMA_QUEUES)
        return carry

    lax.fori_loop(0, TM // DMA_UNROLL, issue, 0)
    _wait_slabs(out_ref, TM * TOP_K, sem)

    tp = tp_ref[...]
    ss = jnp.zeros((TM, 1), F32)
    chunks = []
    for c in range(SLAB):
        cs = slice(c * LANES, (c + 1) * LANES)
        acc = tp[:, 0:1] * buf[0, pl.ds(c, TM, stride=SLAB), :]
        for k in range(1, TOP_K):
            acc = acc + tp[:, k:k + 1] * buf[k, pl.ds(c, TM, stride=SLAB), :]
        yc = y_ref[:, cs] + g_ref[:, cs] * acc
        o_ref[:, cs] = yc
        if final:
            ss = ss + jnp.sum(yc * yc, axis=-1, keepdims=True)
            chunks.append(yc)
    if final:
        inv = lax.rsqrt(ss * (1.0 / D_MODEL) + EPS)
        for c in range(SLAB):
            cs = slice(c * LANES, (c + 1) * LANES)
            n_ref[:, cs] = chunks[c] * inv * fg_ref[:, cs]


def _combine(y, out_slabs, pos, top_p, gate, final_gain=None):
    final = final_gain is not None
    tile = pl.BlockSpec((TM, D_MODEL), lambda i, pos: (i, 0))
    in_specs = [pl.BlockSpec(memory_space=pl.ANY), tile, pl.BlockSpec((TM, LANES), lambda i, pos: (i, 0)),
                pl.BlockSpec((None, 1, D_MODEL), lambda i, pos: (_mod_row(i), 0, 0))]
    args = [out_slabs, y, top_p, gate]
    out_shape = [jax.ShapeDtypeStruct((T_ALL, D_MODEL), F32)]
    out_specs = [tile]
    if final:
        in_specs.append(pl.BlockSpec((1, D_MODEL), lambda i, pos: (0, 0)))
        args.append(final_gain.reshape(1, D_MODEL))
        out_shape.append(jax.ShapeDtypeStruct((T_ALL, D_MODEL), F32))
        out_specs.append(tile)
    grid_spec = pltpu.PrefetchScalarGridSpec(
        num_scalar_prefetch=1,
        grid=(N_TILES,),
        in_specs=in_specs,
        out_specs=out_specs,
        scratch_shapes=[pltpu.VMEM((TOP_K, TM * SLAB, LANES), F32), pltpu.SemaphoreType.DMA],
    )
    return pl.pallas_call(
        functools.partial(_combine_kernel, final=final),
        out_shape=out_shape,
        grid_spec=grid_spec,
        compiler_params=_params(("arbitrary",)),
        name="moe_combine",
    )(pos, *args)


def _moe_layer(layer, y, gain, scale, shift, gate, rw, rb, w_gu, b_gu, w_down, b_down, final_gain=None):
    h_slabs, top_i, top_p, rank, counts = _router(y, gain, scale, shift, rw, rb)
    pos, fill, experts_plan = _route_plan(top_i[:, :TOP_K], rank[:, :TOP_K],
                                          counts[0, :N_EXPERTS].astype(jnp.int32))
    xs = _dispatch(h_slabs, pos, fill)
    out = _moe_experts(layer, xs, experts_plan, w_gu, b_gu, w_down, b_down)
    return _combine(y, out, pos, top_p, gate, final_gain)


def _heads(x, n):
    b = x.shape[0] // SEQ
    return x.reshape(b, SEQ, n, -1).transpose(0, 2, 1, 3)


def kernel(x_prompt, x_sample, c, cache_a_k, cache_a_v, state_b_C, state_b_n, state_b_m, cache_c_k, cache_c_v, cache_d_k, cache_d_v, c_ctx, w_mod, b_mod, norm1_g, norm2_g, w_in_even, w_out_even, a_q_gain, a_k_gain, b_gate_bias, b_norm_gain, w_in_odd, w_out_odd, c_rpb, d_lambda, d_norm_gain, router_w, router_b, expert_w_gu, expert_b_gu, expert_w_down, expert_b_down, final_norm_g):
    y = jnp.concatenate([x_prompt.reshape(T_CTX, D_MODEL), x_sample.reshape(T_LAT, D_MODEL)], axis=0)
    cond = jnp.zeros((MOD_ROWS, D_MODEL), F32).at[0].set(c_ctx).at[1:1 + DEC_BATCH].set(c)
    mod = _modulation(cond, w_mod, b_mod).reshape(DEPTH, MOD_ROWS, 6, 1, D_MODEL)
    rope_cos, rope_sin = _rope_tables()
    scale = HD ** -0.5
    outs = {}

    for layer in range(DEPTH):
        sh1, sc1, g1, sh2, sc2, g2 = (mod[layer, :, k] for k in range(6))
        j = layer // 2
        if layer % 2 == 0:
            w = w_in_even[j]
            sizes = np.cumsum([0, 512, 128, 128, 512, 512, 512, 512, 16])
            aq, ak, av, bq, bk, bv, bo, bg = (w[:, sizes[k]:sizes[k + 1]] for k in range(8))
            w_in = jnp.concatenate([aq, bo, bq, bk, bv, ak, av, bg, jnp.zeros((D_MODEL, EV_N - EV_BG - 16), F32)],
                                   axis=1).astype(BF16)
            p = _norm_proj(y, norm1_g[layer], sc1, sh1, w_in)
            qg = jnp.tile(a_q_gain[j], LANES // HD).reshape(1, LANES)
            kg = jnp.tile(a_k_gain[j], LANES // HD).reshape(1, LANES)
            specs = ((EV_AQ, 512, 0, True, scale, BF16), (EV_AK, 128, 1, False, 1.0, F32),
                     (EV_AK, 128, 1, True, 1.0, BF16), (EV_AV, 128, None, False, 1.0, BF16))
            qa, ka_f32, ka, va = _prep(p, rope_cos, rope_sin, [qg, kg], specs)
            oa_ctx = _attention(qa, ka, va, ctx=True, group=A_HEADS // A_KV, n_kv=A_KV)
            cache = (cache_a_k[:, j].astype(BF16), cache_a_v[:, j].astype(BF16))
            oa_lat = _attention(qa, ka, va, ctx=False, group=A_HEADS // A_KV, n_kv=A_KV, cache=cache)
            zc = jnp.zeros((BATCH, 2, B_HEADS, B_DK, B_DV), F32)
            zn = jnp.zeros((BATCH, 2, B_HEADS, 1, B_DK), F32)
            zm = jnp.zeros((BATCH, 2, B_HEADS, 1, LANES), F32)
            hb_ctx, bC, bn, bm = _mlstm(p, b_gate_bias[j], zc, zn, zm, ctx=True)
            m0 = jnp.broadcast_to(state_b_m[:, j][..., None, None], (DEC_BATCH, 2, B_HEADS, 1, LANES))
            hb_lat, _, _, _ = _mlstm(p, b_gate_bias[j], state_b_C[:, j], state_b_n[:, j][:, :, :, None, :], m0,
                                     ctx=False)
            oa = jnp.concatenate([oa_ctx, oa_lat], axis=0)
            hb = jnp.concatenate([hb_ctx, hb_lat], axis=0)
            y = _merge(y, g1, w_out_even[j].astype(BF16), oa, hb, p=p, norm_gain=b_norm_gain[j])
            outs.setdefault("a_k", []).append(_heads(ka_f32[:T_CTX], A_KV))
            outs.setdefault("a_v", []).append(_heads(p[:T_CTX, EV_AV:EV_AV + 128], A_KV))
            outs.setdefault("b_C", []).append(bC)
            outs.setdefault("b_n", []).append(bn[:, :, :, 0, :])
            outs.setdefault("b_m", []).append(bm[:, :, :, 0, 0])
        else:
            p = _norm_proj(y, norm1_g[layer], sc1, sh1, w_in_odd[j].astype(BF16))
            specs = ((0, 512, None, False, scale, BF16), (512, 512, None, False, 1.0, BF16),
                     (1024, 512, None, False, 1.0, BF16), (1536, 512, None, True, scale, BF16),
                     (2048, 512, None, True, 1.0, BF16), (2560, 512, None, False, 1.0, BF16))
            qc, kc, vc, qd, kd, vd = _prep(p, rope_cos, rope_sin, [], specs)
            lam_init = 0.8 - 0.6 * math.exp(-0.3 * layer)
            lp = d_lambda[j].astype(F32)
            lam = jnp.exp(jnp.sum(lp[0] * lp[1])) - jnp.exp(jnp.sum(lp[2] * lp[3])) + lam_init
            lam_vec = jnp.stack([lam, jnp.asarray(1.0 - lam_init, F32)]).astype(F32)
            diff = (lam_vec, d_norm_gain[j])
            oc_ctx = _attention(qc, kc, vc, ctx=True, n_kv=C_HEADS)
            od_ctx = _attention(qd, kd, vd, ctx=True, diff=diff)
            bias = _neighbourhood_bias(c_rpb[j])
            oc_lat = _attention(qc, kc, vc, ctx=False, n_kv=C_HEADS, bias=bias,
                                cache=(cache_c_k[:, j].astype(BF16), cache_c_v[:, j].astype(BF16)))
            kd_cache = cache_d_k[:, j].reshape(DEC_BATCH, 2 * D_HEADS, PAST_LEN, HD).astype(BF16)
            od_lat = _attention(qd, kd, vd, ctx=False, diff=diff, cache=(kd_cache, cache_d_v[:, j].astype(BF16)))
            oc = jnp.concatenate([oc_ctx, oc_lat], axis=0)
            od = jnp.concatenate([od_ctx, od_lat], axis=0)
            y = _merge(y, g1, w_out_odd[j].astype(BF16), oc, od)
            pc = p[:T_CTX]
            outs.setdefault("c_k", []).append(_heads(pc[:, 512:1024], C_HEADS))
            outs.setdefault("c_v", []).append(_heads(pc[:, 1024:1536], C_HEADS))
            outs.setdefault("d_k", []).append(_heads(pc[:, 2048:2560], 2 * D_HEADS).reshape(BATCH, D_HEADS, 2, SEQ, HD))
            outs.setdefault("d_v", []).append(_heads(pc[:, 2560:3072], D_HEADS))
        res = _moe_layer(layer, y, norm2_g[layer], sc2, sh2, g2, router_w[layer], router_b[layer],
                         expert_w_gu, expert_b_gu, expert_w_down, expert_b_down,
                         final_gain=final_norm_g if layer == DEPTH - 1 else None)
        y = res[0]
    y_norm = res[1]
    stack = lambda k: jnp.stack(outs[k], axis=1)
    return (y_norm[:T_CTX].reshape(BATCH, SEQ, D_MODEL), y_norm[T_CTX:].reshape(DEC_BATCH, DEC_SEQ, D_MODEL),
            stack("a_k"), stack("a_v"), stack("b_C"), stack("b_n"), stack("b_m"),
            stack("c_k"), stack("c_v"), stack("d_k"), stack("d_v"))
```

```python
import functools
import math

import numpy as np
import jax
import jax.numpy as jnp
from jax import lax
from jax.experimental import pallas as pl
from jax.experimental.pallas import tpu as pltpu

D_MODEL = 1024
BATCH = 32
SEQ = 256
DEPTH = 2
DEC_BATCH = 8
DEC_SEQ = 1024
PAST_LEN = 512
GRID_W = 64
HD = 64
A_HEADS = 8
A_KV = 2
B_HEADS = 4
B_DK = 128
B_DV = 128
B_CHUNK = 128
C_HEADS = 8
NA_ROWS = 8
NA_COLS = 16
D_HEADS = 4
D_VDIM = 2 * HD
N_EXPERTS = 32
TOP_K = 4
D_FF = 1024
SWIGLU_LIMIT = 7.0
SWIGLU_ALPHA = 1.702
ROPE_THETA = 10000.0
EPS = 1e-6

F32 = jnp.float32
BF16 = jnp.bfloat16
HIGHEST = lax.Precision.HIGHEST

T_CTX = BATCH * SEQ
T_LAT = DEC_BATCH * DEC_SEQ
T_ALL = T_CTX + T_LAT
TM = 256
CTX_TILES = T_CTX // TM
LAT_TILES_PER_BATCH = DEC_SEQ // TM
N_TILES = T_ALL // TM
MOD_ROWS = 16
LANES = 128
NEG_BIG = -1e30
MOE_TM = 256
N_ASSIGN = T_ALL * TOP_K
MOE_ROWS = N_ASSIGN + N_EXPERTS * MOE_TM
MOE_TILES = MOE_ROWS // MOE_TM
VMEM_LIMIT = 56 * 1024 * 1024

EV_AQ, EV_BO, EV_BQ, EV_BK, EV_BV, EV_AK, EV_AV, EV_BG = 0, 512, 1024, 1536, 2048, 2560, 2688, 2816
EV_N = 2944
OD_N = 3072


def _params(sem, vmem=VMEM_LIMIT):
    return pltpu.CompilerParams(dimension_semantics=sem, vmem_limit_bytes=vmem)


def _mod_row(i):
    return jnp.where(i < CTX_TILES, 0, 1 + (i - CTX_TILES) // LAT_TILES_PER_BATCH)


def _rope_block(i):
    return jnp.where(i < CTX_TILES, LAT_TILES_PER_BATCH, (i - CTX_TILES) % LAT_TILES_PER_BATCH)


def _dot(a, b, precision=None):
    return jnp.dot(a, b, preferred_element_type=F32, precision=precision)


def _dot_nt(a, b):
    return lax.dot_general(a, b, (((1,), (1,)), ((), ())), preferred_element_type=F32)


def _dot_tn(a, b):
    return lax.dot_general(a, b, (((0,), (0,)), ((), ())), preferred_element_type=F32)


def _modulation_kernel(c_ref, w_ref, b_ref, o_ref):
    c = c_ref[...]
    s = c * jax.nn.sigmoid(c)
    o_ref[...] = _dot(s, w_ref[...], HIGHEST) + b_ref[...]


def _modulation(cond, w_mod, b_mod):
    tn = 1536
    return pl.pallas_call(
        _modulation_kernel,
        out_shape=jax.ShapeDtypeStruct((DEPTH, MOD_ROWS, 6 * D_MODEL), F32),
        grid=(DEPTH, 6 * D_MODEL // tn),
        in_specs=[
            pl.BlockSpec((MOD_ROWS, D_MODEL), lambda l, j: (0, 0)),
            pl.BlockSpec((None, D_MODEL, tn), lambda l, j: (l, 0, j)),
            pl.BlockSpec((None, 1, tn), lambda l, j: (l, 0, j)),
        ],
        out_specs=pl.BlockSpec((None, MOD_ROWS, tn), lambda l, j: (l, 0, j)),
        compiler_params=_params(("parallel", "parallel")),
        name="modulation",
    )(cond, w_mod, b_mod.reshape(DEPTH, 1, 6 * D_MODEL))


def _norm_mod(y, g, sc, sh):
    ms = jnp.mean(y * y, axis=-1, keepdims=True)
    return (y * lax.rsqrt(ms + EPS) * g) * (1.0 + sc) + sh


def _norm_proj_kernel(y_ref, g_ref, sc_ref, sh_ref, w_ref, o_ref):
    h = _norm_mod(y_ref[...], g_ref[...], sc_ref[...], sh_ref[...])
    o_ref[...] = _dot(h.astype(BF16), w_ref[...])


def _norm_proj(y, gain, scale, shift, w_bf16):
    n = w_bf16.shape[1]
    vec = pl.BlockSpec((None, 1, D_MODEL), lambda i: (_mod_row(i), 0, 0))
    return pl.pallas_call(
        _norm_proj_kernel,
        out_shape=jax.ShapeDtypeStruct((T_ALL, n), F32),
        grid=(N_TILES,),
        in_specs=[
            pl.BlockSpec((TM, D_MODEL), lambda i: (i, 0)),
            pl.BlockSpec((1, D_MODEL), lambda i: (0, 0)),
            vec, vec,
            pl.BlockSpec((D_MODEL, n), lambda i: (0, 0)),
        ],
        out_specs=pl.BlockSpec((TM, n), lambda i: (i, 0)),
        compiler_params=_params(("parallel",)),
        name="norm_proj",
    )(y, gain.reshape(1, D_MODEL), scale, shift, w_bf16)


def _rope_rotate(x):
    w = x.shape[-1]
    lane = lax.broadcasted_iota(jnp.int32, x.shape, 1)
    nxt = pltpu.roll(x, w - 1, 1)
    prv = pltpu.roll(x, 1, 1)
    return jnp.where((lane & 1) == 0, -nxt, prv)


def _prep_kernel(*refs, specs, n_gain):
    p_ref, cos_ref, sin_ref, bd_ref = refs[:4]
    gain_refs = refs[4:4 + n_gain]
    out_refs = refs[4 + n_gain:]
    cos = cos_ref[...]
    sin = sin_ref[...]
    for (col, width, gi, rope, scale, _), o_ref in zip(specs, out_refs):
        for c0 in range(0, width, LANES):
            x = p_ref[:, col + c0:col + c0 + LANES]
            if gi is not None:
                ss = _dot(x * x, bd_ref[...], HIGHEST)
                x = x * lax.rsqrt(ss * (1.0 / HD) + EPS) * gain_refs[gi][...]
            if rope:
                x = x * cos + _rope_rotate(x) * sin
            if scale != 1.0:
                x = x * scale
            o_ref[:, c0:c0 + LANES] = x.astype(o_ref.dtype)


def _prep(p, rope_cos, rope_sin, gains, specs):
    n = p.shape[1]
    bd = jnp.asarray(np.kron(np.eye(LANES // HD), np.ones((HD, HD))), F32)
    rope_spec = pl.BlockSpec((TM, LANES), lambda i: (_rope_block(i), 0))
    in_specs = [pl.BlockSpec((TM, n), lambda i: (i, 0)), rope_spec, rope_spec,
                pl.BlockSpec((LANES, LANES), lambda i: (0, 0))]
    in_specs += [pl.BlockSpec((1, LANES), lambda i: (0, 0)) for _ in gains]
    return pl.pallas_call(
        functools.partial(_prep_kernel, specs=specs, n_gain=len(gains)),
        out_shape=[jax.ShapeDtypeStruct((T_ALL, s[1]), s[5]) for s in specs],
        grid=(N_TILES,),
        in_specs=in_specs,
        out_specs=[pl.BlockSpec((TM, s[1]), lambda i: (i, 0)) for s in specs],
        compiler_params=_params(("parallel",)),
        name="qkv_prep",
    )(p, rope_cos, rope_sin, bd, *gains)


def _rope_tables():
    half = HD // 2
    freqs = 1.0 / (ROPE_THETA ** (jnp.arange(0, half, 2, dtype=F32) / half))
    t = jnp.arange(DEC_SEQ)
    rows = (t // GRID_W).astype(F32)
    cols = (t % GRID_W).astype(F32)
    ang = jnp.concatenate([rows[:, None] * freqs, cols[:, None] * freqs], axis=-1)
    cos = jnp.repeat(jnp.cos(ang), 2, axis=-1)
    sin = jnp.repeat(jnp.sin(ang), 2, axis=-1)
    cos = jnp.concatenate([jnp.tile(cos, (1, LANES // HD)), jnp.ones((TM, LANES), F32)], axis=0)
    sin = jnp.concatenate([jnp.tile(sin, (1, LANES // HD)), jnp.zeros((TM, LANES), F32)], axis=0)
    return cos, sin


def _lane_slice(ref, h, width=HD):
    per = LANES // width
    blk = ref[:, (h // per) * LANES:(h // per + 1) * LANES]
    if per == 1:
        return blk
    return blk[:, (h % per) * width:(h % per + 1) * width]


def _softmax_parts(scores):
    m = None
    for s in scores:
        ms = jnp.max(s, axis=-1, keepdims=True)
        m = ms if m is None else jnp.maximum(m, ms)
    ps = [jnp.exp(s - m) for s in scores]
    l = None
    for p in ps:
        ls = jnp.sum(p, axis=-1, keepdims=True)
        l = ls if l is None else l + ls
    return ps, l


def _attn_std_kernel(*refs, group, n_kv, has_cache, has_bias, bq):
    it = iter(refs)
    q_ref, kn_ref, vn_ref = next(it), next(it), next(it)
    kc_ref = vc_ref = b_ref = None
    if has_cache:
        kc_ref, vc_ref = next(it), next(it)
    if has_bias:
        b_ref = next(it)
    o_ref = next(it)
    outs = []
    for g in range(n_kv):
        qs = jnp.concatenate([_lane_slice(q_ref, g * group + j) for j in range(group)], axis=0)
        kn = _lane_slice(kn_ref, g)
        vn = _lane_slice(vn_ref, g)
        s_new = _dot_nt(qs, kn)
        if has_bias:
            s_new = s_new + b_ref[g]
        scores = [s_new]
        if has_cache:
            scores.append(_dot_nt(qs, kc_ref[g]))
        ps, l = _softmax_parts(scores)
        o = _dot(ps[0].astype(BF16), vn)
        if has_cache:
            o = o + _dot(ps[1].astype(BF16), vc_ref[g])
        o = o / l
        for j in range(group):
            outs.append(o[j * bq:(j + 1) * bq])
    o_ref[...] = jnp.concatenate(outs, axis=1).astype(o_ref.dtype)


def _attn_diff_kernel(*refs, has_cache):
    it = iter(refs)
    lam_ref, q_ref, kn_ref, vn_ref = next(it), next(it), next(it), next(it)
    kc_ref = vc_ref = None
    if has_cache:
        kc_ref, vc_ref = next(it), next(it)
    g_ref, o_ref = next(it), next(it)
    lam = lam_ref[0]
    post = lam_ref[1]
    outs = []
    for h in range(D_HEADS):
        pd_new, pd_c = None, None
        for j in range(2):
            f = 2 * h + j
            qs = _lane_slice(q_ref, f)
            scores = [_dot_nt(qs, _lane_slice(kn_ref, f))]
            if has_cache:
                scores.append(_dot_nt(qs, kc_ref[f]))
            ps, l = _softmax_parts(scores)
            r = 1.0 / l
            if j == 0:
                pd_new = ps[0] * r
                pd_c = ps[1] * r if has_cache else None
            else:
                r = r * lam
                pd_new = pd_new - ps[0] * r
                pd_c = pd_c - ps[1] * r if has_cache else None
        o = _dot(pd_new.astype(BF16), _lane_slice(vn_ref, h, D_VDIM))
        if has_cache:
            o = o + _dot(pd_c.astype(BF16), vc_ref[h])
        ms = jnp.mean(o * o, axis=-1, keepdims=True)
        outs.append(o * lax.rsqrt(ms + EPS) * g_ref[...] * post)
    o_ref[...] = jnp.concatenate(outs, axis=1).astype(o_ref.dtype)


def _attention(q, kn, vn, *, ctx, group=1, n_kv=1, cache=None, bias=None, diff=None, bq=256):
    if ctx:
        nb, sq, row0 = BATCH, SEQ, 0
    else:
        nb, sq, row0 = DEC_BATCH, DEC_SEQ, T_CTX
    nq = sq // bq
    qb0 = row0 // bq
    kb0 = row0 // sq
    wq, wk, wv = q.shape[1], kn.shape[1], vn.shape[1]
    in_specs = [
        pl.BlockSpec((bq, wq), lambda b, i: (qb0 + b * nq + i, 0)),
        pl.BlockSpec((sq, wk), lambda b, i: (kb0 + b, 0)),
        pl.BlockSpec((sq, wv), lambda b, i: (kb0 + b, 0)),
    ]
    args = [q, kn, vn]
    if cache is not None:
        kc, vc = cache
        in_specs += [pl.BlockSpec((None,) + kc.shape[1:], lambda b, i: (b, 0, 0, 0)),
                     pl.BlockSpec((None,) + vc.shape[1:], lambda b, i: (b, 0, 0, 0))]
        args += [kc, vc]
    if diff is None:
        if bias is not None:
            in_specs.append(pl.BlockSpec((bias.shape[0], bq, sq), lambda b, i: (0, i, 0)))
            args.append(bias)
        body = functools.partial(_attn_std_kernel, group=group, n_kv=n_kv, has_cache=cache is not None,
                                 has_bias=bias is not None, bq=bq)
    else:
        lam_vec, gain = diff
        in_specs = [pl.BlockSpec(memory_space=pltpu.SMEM)] + in_specs
        args = [lam_vec] + args
        in_specs.append(pl.BlockSpec((1, D_VDIM), lambda b, i: (0, 0)))
        args.append(gain.reshape(1, D_VDIM))
        body = functools.partial(_attn_diff_kernel, has_cache=cache is not None)
    return pl.pallas_call(
        body,
        out_shape=jax.ShapeDtypeStruct((nb * sq, 512), BF16),
        grid=(nb, nq),
        in_specs=in_specs,
        out_specs=pl.BlockSpec((bq, 512), lambda b, i: (b * nq + i, 0)),
        compiler_params=_params(("parallel", "parallel")),
        name="attention",
    )(*args)


def _neighbourhood_bias(rpb):
    rows = DEC_SEQ // GRID_W
    wr = min(NA_ROWS, rows)
    r = np.arange(rows)
    c = np.arange(GRID_W)
    r0 = np.clip(r - wr // 2, 0, rows - wr)
    c0 = np.clip(c - NA_COLS // 2, 0, GRID_W - NA_COLS)
    row_ok = (r[None, :] >= r0[:, None]) & (r[None, :] < r0[:, None] + wr)
    col_ok = (c[None, :] >= c0[:, None]) & (c[None, :] < c0[:, None] + NA_COLS)
    dr = np.clip(r[None, :] - r[:, None] + (NA_ROWS - 1), 0, 2 * NA_ROWS - 2)
    dc = np.clip(c[None, :] - c[:, None], 1 - NA_COLS, NA_COLS - 1) + (NA_COLS - 1)
    oh_r = jnp.asarray(dr[..., None] == np.arange(2 * NA_ROWS - 1), F32)
    oh_c = jnp.asarray(dc[..., None] == np.arange(2 * NA_COLS - 1), F32)
    by_col = jnp.einsum("hrd,qkd->hrqk", rpb.astype(F32), oh_c, precision=HIGHEST)
    full = jnp.einsum("PKr,hrqk->hPqKk", oh_r, by_col, precision=HIGHEST)
    ok = jnp.asarray(row_ok[:, None, :, None] & col_ok[None, :, None, :])
    return jnp.where(ok[None], full, NEG_BIG).reshape(C_HEADS, DEC_SEQ, DEC_SEQ)


def _log_sigmoid(x):
    return jnp.minimum(x, 0.0) - jnp.log1p(jnp.exp(-jnp.abs(x)))


def _mlstm_kernel(q_ref, k_ref, v_ref, g_ref, gb_ref, c0_ref, n0_ref, m0_ref, h_ref, c_ref, n_ref, m_ref, hb_ref,
                  *, seq):
    L = B_CHUNK
    nc = seq // L
    row = lax.broadcasted_iota(jnp.int32, (L, L), 0)
    col = lax.broadcasted_iota(jnp.int32, (L, L), 1)
    keeps = (col <= row, col >= row)
    k_scale = B_DK ** -0.5
    c_ref[...] = c0_ref[...]
    n_ref[...] = n0_ref[...]
    m_ref[...] = m0_ref[...]

    def step(j):
        for d in range(2):
            keep = keeps[d]
            c = j if d == 0 else nc - 1 - j
            off = c * L if isinstance(c, int) else pl.multiple_of(c * L, L)
            gates = g_ref[pl.ds(off, L), :] + gb_ref[...]
            cum = _dot(keep.astype(F32), _log_sigmoid(gates), HIGHEST)
            cum_t = cum.T
            gates_t = gates.T
            out_ref = h_ref if d == 0 else hb_ref
            for h in range(B_HEADS):
                ci = (2 * d) * B_HEADS + h
                cf = (2 * d + 1) * B_HEADS + h
                hs = slice(h * B_DK, (h + 1) * B_DK)
                C = c_ref[d, h]
                n = n_ref[d, h]
                m = m_ref[d, h][:, 0:1]
                qc = q_ref[pl.ds(off, L), hs]
                kc = k_ref[pl.ds(off, L), hs] * k_scale
                vc = v_ref[pl.ds(off, L), hs]
                b_col = cum[:, cf:cf + 1]
                i_col = gates[:, ci:ci + 1]
                b_row = cum_t[cf:cf + 1, :]
                i_row = gates_t[ci:ci + 1, :]
                dlog = jnp.where(keep, b_col - b_row + i_row, -jnp.inf)
                inter = b_col + m
                m_t = jnp.maximum(inter, jnp.max(dlog, axis=-1, keepdims=True))
                w_intra = jnp.exp(dlog - m_t)
                w_inter = jnp.exp(inter - m_t)
                qb = qc.astype(BF16)
                vb = vc.astype(BF16)
                qk = _dot_nt(qb, kc.astype(BF16)) * w_intra
                num = _dot(qk.astype(BF16), vb) + w_inter * _dot(qb, C.astype(BF16))
                den = jnp.sum(qk, axis=-1, keepdims=True) + w_inter * jnp.sum(qc * n, axis=-1, keepdims=True)
                out_ref[pl.ds(off, L), hs] = num / jnp.maximum(jnp.abs(den), jnp.exp(-m_t))
                b_last = b_col[L - 1:L, :] if d == 0 else b_col[0:1, :]
                end_col = b_last - b_col + i_col
                m_new = jnp.maximum(b_last + m, jnp.max(end_col, axis=0, keepdims=True))
                w_end = jnp.exp(end_col - m_new)
                decay = jnp.exp(b_last + m - m_new)
                kw = kc * w_end
                c_ref[d, h] = decay * C + _dot_tn(kw.astype(BF16), vb)
                n_ref[d, h] = decay * n + jnp.sum(kw, axis=0, keepdims=True)
                m_ref[d, h] = jnp.broadcast_to(m_new, (1, LANES))

    if nc <= 2:
        for j in range(nc):
            step(j)
    else:
        def body(j, carry):
            step(j)
            return carry

        lax.fori_loop(0, nc, body, 0)
    h_ref[...] = h_ref[...] + hb_ref[...]


def _mlstm(p, gate_bias, c0, n0, m0, *, ctx):
    if ctx:
        nb, seq, blk0 = BATCH, SEQ, 0
    else:
        nb, seq, blk0 = DEC_BATCH, DEC_SEQ, T_CTX // DEC_SEQ
    w = B_HEADS * B_DK

    def cols(c0_, width):
        return pl.BlockSpec((seq, width), lambda b: (blk0 + b, c0_ // width))

    gb = jnp.zeros((1, LANES), F32).at[0, :4 * B_HEADS].set(gate_bias.reshape(-1).astype(F32))
    st = lambda shape: pl.BlockSpec((None,) + shape, lambda b: (b, 0, 0, 0, 0))
    return pl.pallas_call(
        functools.partial(_mlstm_kernel, seq=seq),
        out_shape=[
            jax.ShapeDtypeStruct((nb * seq, w), F32),
            jax.ShapeDtypeStruct((nb, 2, B_HEADS, B_DK, B_DV), F32),
            jax.ShapeDtypeStruct((nb, 2, B_HEADS, 1, B_DK), F32),
            jax.ShapeDtypeStruct((nb, 2, B_HEADS, 1, LANES), F32),
        ],
        grid=(nb,),
        in_specs=[
            cols(EV_BQ, w), cols(EV_BK, w), cols(EV_BV, w), cols(EV_BG, LANES),
            pl.BlockSpec((1, LANES), lambda b: (0, 0)),
            st((2, B_HEADS, B_DK, B_DV)), st((2, B_HEADS, 1, B_DK)), st((2, B_HEADS, 1, LANES)),
        ],
        out_specs=[
            pl.BlockSpec((seq, w), lambda b: (b, 0)),
            st((2, B_HEADS, B_DK, B_DV)), st((2, B_HEADS, 1, B_DK)), st((2, B_HEADS, 1, LANES)),
        ],
        scratch_shapes=[pltpu.VMEM((seq, w), F32)],
        compiler_params=_params(("parallel",)),
        name="mlstm",
    )(p, p, p, p, gb, c0, n0, m0)


def _merge_kernel(*refs, even):
    if even:
        a_ref, hb_ref, bo_ref, ng_ref, w_ref, y_ref, g_ref, o_ref = refs
        parts = [a_ref[...]]
        for h in range(B_HEADS):
            hs = slice(h * B_DV, (h + 1) * B_DV)
            x = hb_ref[:, hs]
            ms = jnp.mean(x * x, axis=-1, keepdims=True)
            xn = x * lax.rsqrt(ms + EPS) * ng_ref[:, hs]
            parts.append((jax.nn.sigmoid(bo_ref[:, hs]) * xn).astype(BF16))
    else:
        a_ref, b_ref, w_ref, y_ref, g_ref, o_ref = refs
        parts = [a_ref[...], b_ref[...]]
    cat = jnp.concatenate(parts, axis=1)
    o_ref[...] = y_ref[...] + g_ref[...] * _dot(cat, w_ref[...])


def _merge(y, gate, w_bf16, a, b, *, p=None, norm_gain=None):
    even = p is not None
    half = pl.BlockSpec((TM, 512), lambda i: (i, 0))
    in_specs = [half, half]
    args = [a, b]
    if even:
        in_specs += [pl.BlockSpec((TM, 512), lambda i: (i, EV_BO // 512)), pl.BlockSpec((1, 512), lambda i: (0, 0))]
        args += [p, norm_gain.reshape(1, 512)]
    in_specs += [
        pl.BlockSpec((D_MODEL, D_MODEL), lambda i: (0, 0)),
        pl.BlockSpec((TM, D_MODEL), lambda i: (i, 0)),
        pl.BlockSpec((None, 1, D_MODEL), lambda i: (_mod_row(i), 0, 0)),
    ]
    args += [w_bf16, y, gate]
    return pl.pallas_call(
        functools.partial(_merge_kernel, even=even),
        out_shape=jax.ShapeDtypeStruct((T_ALL, D_MODEL), F32),
        grid=(N_TILES,),
        in_specs=in_specs,
        out_specs=pl.BlockSpec((TM, D_MODEL), lambda i: (i, 0)),
        compiler_params=_params(("parallel",)),
        name="merge",
    )(*args)


SLAB = D_MODEL // LANES


def _load_slabs(ref, rows):
    return jnp.concatenate([ref[pl.ds(c, rows, stride=SLAB), :] for c in range(SLAB)], axis=1)


def _store_slabs(ref, x):
    for c in range(SLAB):
        ref[pl.ds(c, x.shape[0], stride=SLAB), :] = x[:, c * LANES:(c + 1) * LANES]


def _slab(ref, idx):
    return ref.at[pl.ds(pl.multiple_of(idx * SLAB, SLAB), SLAB)]


def _router_kernel(y_ref, g_ref, sc_ref, sh_ref, whi_ref, wlo_ref, b_ref, h_ref, ti_ref, tp_ref, rk_ref, cnt_ref,
                   base_ref):
    @pl.when(pl.program_id(0) == 0)
    def _():
        base_ref[...] = jnp.zeros(base_ref.shape, F32)

    h = _norm_mod(y_ref[...], g_ref[...], sc_ref[...], sh_ref[...])
    _store_slabs(h_ref, h)
    h_hi = h.astype(BF16)
    h_lo = (h - h_hi.astype(F32)).astype(BF16)
    logits = (_dot(h_hi, whi_ref[...]) + (_dot(h_hi, wlo_ref[...]) + _dot(h_lo, whi_ref[...]))
              + b_ref[...])
    lane = lax.broadcasted_iota(jnp.int32, logits.shape, 1)
    lane_f = lane.astype(F32)
    vals, idxs = [], []
    for _ in range(TOP_K):
        mx = jnp.max(logits, axis=-1, keepdims=True)
        ix = jnp.min(jnp.where(logits == mx, lane_f, float(LANES)), axis=-1, keepdims=True)
        vals.append(mx)
        idxs.append(ix)
        logits = jnp.where(lane_f == ix, -jnp.inf, logits)
    es = [jnp.exp(v - vals[0]) for v in vals]
    tot = es[0] + es[1] + es[2] + es[3]
    ti = jnp.zeros(logits.shape, F32)
    tp = jnp.zeros(logits.shape, F32)
    for k in range(TOP_K):
        ti = jnp.where(lane == k, idxs[k], ti)
        tp = jnp.where(lane == k, es[k] / tot, tp)
    ti_ref[...] = ti.astype(jnp.int32)
    tp_ref[...] = tp
    onehots = [(lane_f == ix).astype(F32) for ix in idxs]
    cnt = onehots[0] + onehots[1] + onehots[2] + onehots[3]
    row = lax.broadcasted_iota(jnp.int32, (TM, TM), 0)
    col = lax.broadcasted_iota(jnp.int32, (TM, TM), 1)
    before = _dot((col < row).astype(BF16), cnt.astype(BF16)) + base_ref[...]
    rk = jnp.zeros(logits.shape, jnp.int32)
    for k in range(TOP_K):
        r_k = jnp.sum(onehots[k] * before, axis=-1, keepdims=True).astype(jnp.int32)
        rk = jnp.where(lane == k, r_k, rk)
    rk_ref[...] = rk
    base_ref[...] = base_ref[...] + jnp.sum(cnt, axis=0, keepdims=True)
    cnt_ref[...] = base_ref[...]


def _router(y, gain, scale, shift, rw, rb):
    vec = pl.BlockSpec((None, 1, D_MODEL), lambda i: (_mod_row(i), 0, 0))
    rw_p = jnp.zeros((D_MODEL, LANES), F32).at[:, :N_EXPERTS].set(rw)
    rb_p = jnp.full((1, LANES), NEG_BIG, F32).at[0, :N_EXPERTS].set(rb)
    rw_hi = rw_p.astype(BF16)
    tile = lambda w: pl.BlockSpec((TM, w), lambda i: (i, 0))
    return pl.pallas_call(
        _router_kernel,
        out_shape=[jax.ShapeDtypeStruct((T_ALL * SLAB, LANES), F32),
                   jax.ShapeDtypeStruct((T_ALL, LANES), jnp.int32),
                   jax.ShapeDtypeStruct((T_ALL, LANES), F32),
                   jax.ShapeDtypeStruct((T_ALL, LANES), jnp.int32),
                   jax.ShapeDtypeStruct((1, LANES), F32)],
        grid=(N_TILES,),
        in_specs=[tile(D_MODEL), pl.BlockSpec((1, D_MODEL), lambda i: (0, 0)), vec, vec,
                  pl.BlockSpec((D_MODEL, LANES), lambda i: (0, 0)), pl.BlockSpec((D_MODEL, LANES), lambda i: (0, 0)),
                  pl.BlockSpec((1, LANES), lambda i: (0, 0))],
        out_specs=[pl.BlockSpec((TM * SLAB, LANES), lambda i: (i, 0)), tile(LANES), tile(LANES), tile(LANES),
                   pl.BlockSpec((1, LANES), lambda i: (0, 0))],
        scratch_shapes=[pltpu.VMEM((1, LANES), F32)],
        compiler_params=_params(("arbitrary",)),
        name="router",
    )(y, gain.reshape(1, D_MODEL), scale, shift, rw_hi, (rw_p - rw_hi.astype(F32)).astype(BF16), rb_p)


def _route_plan(top_i, rank, counts):
    experts = jnp.arange(N_EXPERTS, dtype=jnp.int32)
    padded = ((counts + MOE_TM - 1) // MOE_TM) * MOE_TM
    seg_end = jnp.cumsum(padded)
    seg_start = seg_end - padded
    onehot = top_i[..., None] == experts
    pos = jnp.sum(jnp.where(onehot, seg_start, 0), axis=-1) + rank
    n_active = seg_end[-1] // MOE_TM
    fill = jnp.concatenate([seg_start + counts, padded - counts, n_active[None]]).astype(jnp.int32)
    tile_start = jnp.arange(MOE_TILES, dtype=jnp.int32) * MOE_TM
    tile_expert = jnp.sum((seg_end[None, :] <= tile_start[:, None]).astype(jnp.int32), axis=1)
    last = jnp.sum((seg_end <= (n_active - 1) * MOE_TM).astype(jnp.int32))
    tile_expert = jnp.minimum(jnp.where(tile_start < seg_end[-1], tile_expert, last), N_EXPERTS - 1)
    owns = (padded > 0).astype(jnp.int32)
    run_of_expert = jnp.cumsum(owns) - 1
    run_expert = jnp.sum(jnp.where((run_of_expert[None, :] == experts[:, None]) & (owns[None, :] > 0),
                                   experts[None, :], 0), axis=1)
    runs = jnp.concatenate([run_expert, jnp.sum(owns)[None]]).astype(jnp.int32)
    tile_run = jnp.sum(jnp.where(tile_expert[:, None] == experts[None, :], run_of_expert[None, :], 0), axis=1)
    experts_plan = (tile_expert.astype(jnp.int32), n_active.reshape(1).astype(jnp.int32),
                    tile_run.astype(jnp.int32), runs)
    return pos.reshape(-1).astype(jnp.int32), fill, experts_plan


DMA_UNROLL = 4
DMA_QUEUES = 2


def _wait_slabs(ref, n_slabs, sem):
    view = ref.at[pl.ds(0, n_slabs * SLAB)]
    pltpu.make_async_copy(view, view, sem).wait()


def _dispatch_kernel(pos_ref, fill_ref, h_ref, xs_ref, sem):
    i = pl.program_id(0)
    base = i * TM

    def issue(j, carry):
        for u in range(DMA_UNROLL):
            t = j * DMA_UNROLL + u
            for k in range(TOP_K):
                dst = _slab(xs_ref, pos_ref[(base + t) * TOP_K + k])
                pltpu.make_async_copy(_slab(h_ref, t), dst, sem).start(priority=k % DMA_QUEUES)
        return carry

    lax.fori_loop(0, TM // DMA_UNROLL, issue, 0)

    @pl.when(i == 0)
    def _():
        def per_expert(e, total):
            start = fill_ref[e]
            n = fill_ref[N_EXPERTS + e]

            def one(r, carry):
                pltpu.make_async_copy(_slab(h_ref, 0), _slab(xs_ref, start + r), sem).start()
                return carry

            lax.fori_loop(0, n, one, 0)
            return total + n

        total = lax.fori_loop(0, N_EXPERTS, per_expert, 0)

        n_active = fill_ref[2 * N_EXPERTS]

        def unused_tile(ti, carry):
            dst = xs_ref.at[pl.ds(pl.multiple_of(ti * (MOE_TM * SLAB), MOE_TM * SLAB), MOE_TM * SLAB)]
            pltpu.make_async_copy(h_ref, dst, sem).start()
            return carry

        lax.fori_loop(n_active, MOE_TILES, unused_tile, 0)
        total = total + (MOE_TILES - n_active) * MOE_TM

        @pl.when(total > 0)
        def _():
            _wait_slabs(xs_ref, total, sem)

    _wait_slabs(xs_ref, TM * TOP_K, sem)


def _dispatch(h_slabs, pos, fill):
    grid_spec = pltpu.PrefetchScalarGridSpec(
        num_scalar_prefetch=2,
        grid=(N_TILES,),
        in_specs=[pl.BlockSpec((TM * SLAB, LANES), lambda i, pos, fill: (i, 0))],
        out_specs=pl.BlockSpec(memory_space=pl.ANY),
        scratch_shapes=[pltpu.SemaphoreType.DMA],
    )
    return pl.pallas_call(
        _dispatch_kernel,
        out_shape=jax.ShapeDtypeStruct((MOE_ROWS * SLAB, LANES), F32),
        grid_spec=grid_spec,
        compiler_params=_params(("arbitrary",)),
        name="moe_dispatch",
    )(pos, fill, h_slabs)


def _moe_kernel(te_ref, na_ref, ts_ref, ex_ref, x_ref, wgu_hbm, bgu_ref, wd_hbm, bd_ref, o_ref,
                wgu_f32, wd_f32, wgu_bf, wd_bf, sems, *, layer):
    i = pl.program_id(0)
    s = ts_ref[i]
    first = (i == 0) | (s != ts_ref[jnp.maximum(i - 1, 0)])

    def weight_copies(slot):
        e = ex_ref[slot]
        b = slot % 2
        return (pltpu.make_async_copy(wgu_hbm.at[layer, e], wgu_f32.at[b], sems.at[0, b]),
                pltpu.make_async_copy(wd_hbm.at[layer, e], wd_f32.at[b], sems.at[1, b]))

    @pl.when(i == 0)
    def _():
        for cp in weight_copies(0):
            cp.start()

    @pl.when(first)
    def _():
        for cp in weight_copies(s):
            cp.wait()

        @pl.when(s + 1 < ex_ref[N_EXPERTS])
        def _():
            for cp in weight_copies(s + 1):
                cp.start()

        b = s % 2
        wgu_bf[...] = wgu_f32[b].astype(BF16)
        wd_bf[...] = wd_f32[b].astype(BF16)

    @pl.when(i < na_ref[0])
    def _():
        x = _load_slabs(x_ref, MOE_TM).astype(BF16)
        gu = _dot(x, wgu_bf[...]) + bgu_ref[...]
        gate = jnp.minimum(gu[:, :D_FF], SWIGLU_LIMIT)
        up = jnp.clip(gu[:, D_FF:], -SWIGLU_LIMIT, SWIGLU_LIMIT)
        act = (up + 1.0) * gate * jax.nn.sigmoid(SWIGLU_ALPHA * gate)
        _store_slabs(o_ref, _dot(act.astype(BF16), wd_bf[...]) + bd_ref[...])

    @pl.when(i >= na_ref[0])
    def _():
        o_ref[...] = jnp.zeros(o_ref.shape, o_ref.dtype)


def _moe_experts(layer, xs, plan, w_gu, b_gu, w_down, b_down):
    grid_spec = pltpu.PrefetchScalarGridSpec(
        num_scalar_prefetch=4,
        grid=(MOE_TILES,),
        in_specs=[
            pl.BlockSpec((MOE_TM * SLAB, LANES), lambda i, te, na, ts, ex: (jnp.minimum(i, na[0] - 1), 0)),
            pl.BlockSpec(memory_space=pl.ANY),
            pl.BlockSpec((None, None, 1, 2 * D_FF), lambda i, te, na, ts, ex: (layer, te[i], 0, 0)),
            pl.BlockSpec(memory_space=pl.ANY),
            pl.BlockSpec((None, None, 1, D_MODEL), lambda i, te, na, ts, ex: (layer, te[i], 0, 0)),
        ],
        out_specs=pl.BlockSpec((MOE_TM * SLAB, LANES), lambda i, te, na, ts, ex: (i, 0)),
        scratch_shapes=[pltpu.VMEM((2, D_MODEL, 2 * D_FF), F32), pltpu.VMEM((2, D_FF, D_MODEL), F32),
                        pltpu.VMEM((D_MODEL, 2 * D_FF), BF16), pltpu.VMEM((D_FF, D_MODEL), BF16),
                        pltpu.SemaphoreType.DMA((2, 2))],
    )
    return pl.pallas_call(
        functools.partial(_moe_kernel, layer=layer),
        out_shape=jax.ShapeDtypeStruct((MOE_ROWS * SLAB, LANES), F32),
        grid_spec=grid_spec,
        compiler_params=_params(("arbitrary",)),
        name="moe_experts",
    )(*plan, xs, w_gu, b_gu.reshape(DEPTH, N_EXPERTS, 1, 2 * D_FF), w_down,
      b_down.reshape(DEPTH, N_EXPERTS, 1, D_MODEL))


def _combine_kernel(pos_ref, out_ref, y_ref, tp_ref, g_ref, *rest, final):
    if final:
        fg_ref, o_ref, n_ref, buf, sem = rest
    else:
        o_ref, buf, sem = rest
    base = pl.program_id(0) * TM

    def issue(j, carry):
        for u in range(DMA_UNROLL):
            t = j * DMA_UNROLL + u
            for k in range(TOP_K):
                src = _slab(out_ref, pos_ref[(base + t) * TOP_K + k])
                pltpu.make_async_copy(src, _slab(buf.at[k], t), sem).start(priority=k % DMA_QUEUES)
        return carry

    lax.fori_loop(0, TM // DMA_UNROLL, issue, 0)
    _wait_slabs(out_ref, TM * TOP_K, sem)

    tp = tp_ref[...]
    ss = jnp.zeros((TM, 1), F32)
    chunks = []
    for c in range(SLAB):
        cs = slice(c * LANES, (c + 1) * LANES)
        acc = tp[:, 0:1] * buf[0, pl.ds(c, TM, stride=SLAB), :]
        for k in range(1, TOP_K):
            acc = acc + tp[:, k:k + 1] * buf[k, pl.ds(c, TM, stride=SLAB), :]
        yc = y_ref[:, cs] + g_ref[:, cs] * acc
        o_ref[:, cs] = yc
        if final:
            ss = ss + jnp.sum(yc * yc, axis=-1, keepdims=True)
            chunks.append(yc)
    if final:
        inv = lax.rsqrt(ss * (1.0 / D_MODEL) + EPS)
        for c in range(SLAB):
            cs = slice(c * LANES, (c + 1) * LANES)
            n_ref[:, cs] = chunks[c] * inv * fg_ref[:, cs]


def _combine(y, out_slabs, pos, top_p, gate, final_gain=None):
    final = final_gain is not None
    tile = pl.BlockSpec((TM, D_MODEL), lambda i, pos: (i, 0))
    in_specs = [pl.BlockSpec(memory_space=pl.ANY), tile, pl.BlockSpec((TM, LANES), lambda i, pos: (i, 0)),
                pl.BlockSpec((None, 1, D_MODEL), lambda i, pos: (_mod_row(i), 0, 0))]
    args = [out_slabs, y, top_p, gate]
    out_shape = [jax.ShapeDtypeStruct((T_ALL, D_MODEL), F32)]
    out_specs = [tile]
    if final:
        in_specs.append(pl.BlockSpec((1, D_MODEL), lambda i, pos: (0, 0)))
        args.append(final_gain.reshape(1, D_MODEL))
        out_shape.append(jax.ShapeDtypeStruct((T_ALL, D_MODEL), F32))
        out_specs.append(tile)
    grid_spec = pltpu.PrefetchScalarGridSpec(
        num_scalar_prefetch=1,
        grid=(N_TILES,),
        in_specs=in_specs,
        out_specs=out_specs,
        scratch_shapes=[pltpu.VMEM((TOP_K, TM * SLAB, LANES), F32), pltpu.SemaphoreType.DMA],
    )
    return pl.pallas_call(
        functools.partial(_combine_kernel, final=final),
        out_shape=out_shape,
        grid_spec=grid_spec,
        compiler_params=_params(("arbitrary",)),
        name="moe_combine",
    )(pos, *args)


def _moe_layer(layer, y, gain, scale, shift, gate, rw, rb, w_gu, b_gu, w_down, b_down, final_gain=None):
    h_slabs, top_i, top_p, rank, counts = _router(y, gain, scale, shift, rw, rb)
    pos, fill, experts_plan = _route_plan(top_i[:, :TOP_K], rank[:, :TOP_K],
                                          counts[0, :N_EXPERTS].astype(jnp.int32))
    xs = _dispatch(h_slabs, pos, fill)
    out = _moe_experts(layer, xs, experts_plan, w_gu, b_gu, w_down, b_down)
    return _combine(y, out, pos, top_p, gate, final_gain)


def _heads(x, n):
    b = x.shape[0] // SEQ
    return x.reshape(b, SEQ, n, -1).transpose(0, 2, 1, 3)


def kernel(x_prompt, x_sample, c, cache_a_k, cache_a_v, state_b_C, state_b_n, state_b_m, cache_c_k, cache_c_v, cache_d_k, cache_d_v, c_ctx, w_mod, b_mod, norm1_g, norm2_g, w_in_even, w_out_even, a_q_gain, a_k_gain, b_gate_bias, b_norm_gain, w_in_odd, w_out_odd, c_rpb, d_lambda, d_norm_gain, router_w, router_b, expert_w_gu, expert_b_gu, expert_w_down, expert_b_down, final_norm_g):
    y = jnp.concatenate([x_prompt.reshape(T_CTX, D_MODEL), x_sample.reshape(T_LAT, D_MODEL)], axis=0)
    cond = jnp.zeros((MOD_ROWS, D_MODEL), F32).at[0].set(c_ctx).at[1:1 + DEC_BATCH].set(c)
    mod = _modulation(cond, w_mod, b_mod).reshape(DEPTH, MOD_ROWS, 6, 1, D_MODEL)
    rope_cos, rope_sin = _rope_tables()
    scale = HD ** -0.5
    outs = {}

    for layer in range(DEPTH):
        sh1, sc1, g1, sh2, sc2, g2 = (mod[layer, :, k] for k in range(6))
        j = layer // 2
        if layer % 2 == 0:
            w = w_in_even[j]
            sizes = np.cumsum([0, 512, 128, 128, 512, 512, 512, 512, 16])
            aq, ak, av, bq, bk, bv, bo, bg = (w[:, sizes[k]:sizes[k + 1]] for k in range(8))
            w_in = jnp.concatenate([aq, bo, bq, bk, bv, ak, av, bg, jnp.zeros((D_MODEL, EV_N - EV_BG - 16), F32)],
                                   axis=1).astype(BF16)
            p = _norm_proj(y, norm1_g[layer], sc1, sh1, w_in)
            qg = jnp.tile(a_q_gain[j], LANES // HD).reshape(1, LANES)
            kg = jnp.tile(a_k_gain[j], LANES // HD).reshape(1, LANES)
            specs = ((EV_AQ, 512, 0, True, scale, BF16), (EV_AK, 128, 1, False, 1.0, F32),
                     (EV_AK, 128, 1, True, 1.0, BF16), (EV_AV, 128, None, False, 1.0, BF16))
            qa, ka_f32, ka, va = _prep(p, rope_cos, rope_sin, [qg, kg], specs)
            oa_ctx = _attention(qa, ka, va, ctx=True, group=A_HEADS // A_KV, n_kv=A_KV)
            cache = (cache_a_k[:, j].astype(BF16), cache_a_v[:, j].astype(BF16))
            oa_lat = _attention(qa, ka, va, ctx=False, group=A_HEADS // A_KV, n_kv=A_KV, cache=cache)
            zc = jnp.zeros((BATCH, 2, B_HEADS, B_DK, B_DV), F32)
            zn = jnp.zeros((BATCH, 2, B_HEADS, 1, B_DK), F32)
            zm = jnp.zeros((BATCH, 2, B_HEADS, 1, LANES), F32)
            hb_ctx, bC, bn, bm = _mlstm(p, b_gate_bias[j], zc, zn, zm, ctx=True)
            m0 = jnp.broadcast_to(state_b_m[:, j][..., None, None], (DEC_BATCH, 2, B_HEADS, 1, LANES))
            hb_lat, _, _, _ = _mlstm(p, b_gate_bias[j], state_b_C[:, j], state_b_n[:, j][:, :, :, None, :], m0,
                                     ctx=False)
            oa = jnp.concatenate([oa_ctx, oa_lat], axis=0)
            hb = jnp.concatenate([hb_ctx, hb_lat], axis=0)
            y = _merge(y, g1, w_out_even[j].astype(BF16), oa, hb, p=p, norm_gain=b_norm_gain[j])
            outs.setdefault("a_k", []).append(_heads(ka_f32[:T_CTX], A_KV))
            outs.setdefault("a_v", []).append(_heads(p[:T_CTX, EV_AV:EV_AV + 128], A_KV))
            outs.setdefault("b_C", []).append(bC)
            outs.setdefault("b_n", []).append(bn[:, :, :, 0, :])
            outs.setdefault("b_m", []).append(bm[:, :, :, 0, 0])
        else:
            p = _norm_proj(y, norm1_g[layer], sc1, sh1, w_in_odd[j].astype(BF16))
            specs = ((0, 512, None, False, scale, BF16), (512, 512, None, False, 1.0, BF16),
                     (1024, 512, None, False, 1.0, BF16), (1536, 512, None, True, scale, BF16),
                     (2048, 512, None, True, 1.0, BF16), (2560, 512, None, False, 1.0, BF16))
            qc, kc, vc, qd, kd, vd = _prep(p, rope_cos, rope_sin, [], specs)
            lam_init = 0.8 - 0.6 * math.exp(-0.3 * layer)
            lp = d_lambda[j].astype(F32)
            lam = jnp.exp(jnp.sum(lp[0] * lp[1])) - jnp.exp(jnp.sum(lp[2] * lp[3])) + lam_init
            lam_vec = jnp.stack([lam, jnp.asarray(1.0 - lam_init, F32)]).astype(F32)
            diff = (lam_vec, d_norm_gain[j])
            oc_ctx = _attention(qc, kc, vc, ctx=True, n_kv=C_HEADS)
            od_ctx = _attention(qd, kd, vd, ctx=True, diff=diff)
            bias = _neighbourhood_bias(c_rpb[j])
            oc_lat = _attention(qc, kc, vc, ctx=False, n_kv=C_HEADS, bias=bias,
                                cache=(cache_c_k[:, j].astype(BF16), cache_c_v[:, j].astype(BF16)))
            kd_cache = cache_d_k[:, j].reshape(DEC_BATCH, 2 * D_HEADS, PAST_LEN, HD).astype(BF16)
            od_lat = _attention(qd, kd, vd, ctx=False, diff=diff, cache=(kd_cache, cache_d_v[:, j].astype(BF16)))
            oc = jnp.concatenate([oc_ctx, oc_lat], axis=0)
            od = jnp.concatenate([od_ctx, od_lat], axis=0)
            y = _merge(y, g1, w_out_odd[j].astype(BF16), oc, od)
            pc = p[:T_CTX]
            outs.setdefault("c_k", []).append(_heads(pc[:, 512:1024], C_HEADS))
            outs.setdefault("c_v", []).append(_heads(pc[:, 1024:1536], C_HEADS))
            outs.setdefault("d_k", []).append(_heads(pc[:, 2048:2560], 2 * D_HEADS).reshape(BATCH, D_HEADS, 2, SEQ, HD))
            outs.setdefault("d_v", []).append(_heads(pc[:, 2560:3072], D_HEADS))
        res = _moe_layer(layer, y, norm2_g[layer], sc2, sh2, g2, router_w[layer], router_b[layer],
                         expert_w_gu, expert_b_gu, expert_w_down, expert_b_down,
                         final_gain=final_norm_g if layer == DEPTH - 1 else None)
        y = res[0]
    y_norm = res[1]
    stack = lambda k: jnp.stack(outs[k], axis=1)
    return (y_norm[:T_CTX].reshape(BATCH, SEQ, D_MODEL), y_norm[T_CTX:].reshape(DEC_BATCH, DEC_SEQ, D_MODEL),
            stack("a_k"), stack("a_v"), stack("b_C"), stack("b_n"), stack("b_m"),
            stack("c_k"), stack("c_v"), stack("d_k"), stack("d_v"))
```

```python
import functools
import math

import numpy as np
import jax
import jax.numpy as jnp
from jax import lax
from jax.experimental import pallas as pl
from jax.experimental.pallas import tpu as pltpu

D_MODEL = 1024
BATCH = 32
SEQ = 256
DEPTH = 2
DEC_BATCH = 8
DEC_SEQ = 1024
PAST_LEN = 512
GRID_W = 64
HD = 64
A_HEADS = 8
A_KV = 2
B_HEADS = 4
B_DK = 128
B_DV = 128
B_CHUNK = 128
C_HEADS = 8
NA_ROWS = 8
NA_COLS = 16
D_HEADS = 4
D_VDIM = 2 * HD
N_EXPERTS = 32
TOP_K = 4
D_FF = 1024
SWIGLU_LIMIT = 7.0
SWIGLU_ALPHA = 1.702
ROPE_THETA = 10000.0
EPS = 1e-6

F32 = jnp.float32
BF16 = jnp.bfloat16
HIGHEST = lax.Precision.HIGHEST

T_CTX = BATCH * SEQ
T_LAT = DEC_BATCH * DEC_SEQ
T_ALL = T_CTX + T_LAT
TM = 256
CTX_TILES = T_CTX // TM
LAT_TILES_PER_BATCH = DEC_SEQ // TM
N_TILES = T_ALL // TM
MOD_ROWS = 16
LANES = 128
NEG_BIG = -1e30
MOE_TM = 256
N_ASSIGN = T_ALL * TOP_K
MOE_ROWS = N_ASSIGN + N_EXPERTS * MOE_TM
MOE_TILES = MOE_ROWS // MOE_TM
VMEM_LIMIT = 56 * 1024 * 1024

EV_AQ, EV_BO, EV_BQ, EV_BK, EV_BV, EV_AK, EV_AV, EV_BG = 0, 512, 1024, 1536, 2048, 2560, 2688, 2816
EV_N = 2944
OD_N = 3072


def _params(sem, vmem=VMEM_LIMIT):
    return pltpu.CompilerParams(dimension_semantics=sem, vmem_limit_bytes=vmem)


def _mod_row(i):
    return jnp.where(i < CTX_TILES, 0, 1 + (i - CTX_TILES) // LAT_TILES_PER_BATCH)


def _rope_block(i):
    return jnp.where(i < CTX_TILES, LAT_TILES_PER_BATCH, (i - CTX_TILES) % LAT_TILES_PER_BATCH)


def _dot(a, b, precision=None):
    return jnp.dot(a, b, preferred_element_type=F32, precision=precision)


def _dot_nt(a, b):
    return lax.dot_general(a, b, (((1,), (1,)), ((), ())), preferred_element_type=F32)


def _dot_tn(a, b):
    return lax.dot_general(a, b, (((0,), (0,)), ((), ())), preferred_element_type=F32)


def _modulation_kernel(c_ref, w_ref, b_ref, o_ref):
    c = c_ref[...]
    s = c * jax.nn.sigmoid(c)
    o_ref[...] = _dot(s, w_ref[...], HIGHEST) + b_ref[...]


def _modulation(cond, w_mod, b_mod):
    tn = 1536
    return pl.pallas_call(
        _modulation_kernel,
        out_shape=jax.ShapeDtypeStruct((DEPTH, MOD_ROWS, 6 * D_MODEL), F32),
        grid=(DEPTH, 6 * D_MODEL // tn),
        in_specs=[
            pl.BlockSpec((MOD_ROWS, D_MODEL), lambda l, j: (0, 0)),
            pl.BlockSpec((None, D_MODEL, tn), lambda l, j: (l, 0, j)),
            pl.BlockSpec((None, 1, tn), lambda l, j: (l, 0, j)),
        ],
        out_specs=pl.BlockSpec((None, MOD_ROWS, tn), lambda l, j: (l, 0, j)),
        compiler_params=_params(("parallel", "parallel")),
        name="modulation",
    )(cond, w_mod, b_mod.reshape(DEPTH, 1, 6 * D_MODEL))


def _norm_mod(y, g, sc, sh):
    ms = jnp.mean(y * y, axis=-1, keepdims=True)
    return (y * lax.rsqrt(ms + EPS) * g) * (1.0 + sc) + sh


def _rope_rotate(x):
    w = x.shape[-1]
    lane = lax.broadcasted_iota(jnp.int32, x.shape, 1)
    nxt = pltpu.roll(x, w - 1, 1)
    prv = pltpu.roll(x, 1, 1)
    return jnp.where((lane & 1) == 0, -nxt, prv)


def _norm_proj_kernel(*refs, specs, n_gain):
    y_ref, g_ref, sc_ref, sh_ref, w_ref, cos_ref, sin_ref, bd_ref = refs[:8]
    gain_refs = refs[8:8 + n_gain]
    p_ref = refs[8 + n_gain]
    out_refs = refs[9 + n_gain:]
    h = _norm_mod(y_ref[...], g_ref[...], sc_ref[...], sh_ref[...])
    p_ref[...] = _dot(h.astype(BF16), w_ref[...])
    cos = cos_ref[...]
    sin = sin_ref[...]
    for (col, width, gi, rope, scale, _), o_ref in zip(specs, out_refs):
        for c0 in range(0, width, LANES):
            x = p_ref[:, col + c0:col + c0 + LANES]
            if gi is not None:
                ss = _dot(x * x, bd_ref[...], HIGHEST)
                x = x * lax.rsqrt(ss * (1.0 / HD) + EPS) * gain_refs[gi][...]
            if rope:
                x = x * cos + _rope_rotate(x) * sin
            if scale != 1.0:
                x = x * scale
            o_ref[:, c0:c0 + LANES] = x.astype(o_ref.dtype)


def _norm_proj(y, gain, scale, shift, w_bf16, rope_cos, rope_sin, gains, specs):
    n = w_bf16.shape[1]
    bd = jnp.asarray(np.kron(np.eye(LANES // HD), np.ones((HD, HD))), F32)
    vec = pl.BlockSpec((None, 1, D_MODEL), lambda i: (_mod_row(i), 0, 0))
    rope_spec = pl.BlockSpec((TM, LANES), lambda i: (_rope_block(i), 0))
    in_specs = [pl.BlockSpec((TM, D_MODEL), lambda i: (i, 0)), pl.BlockSpec((1, D_MODEL), lambda i: (0, 0)),
                vec, vec, pl.BlockSpec((D_MODEL, n), lambda i: (0, 0)),
                rope_spec, rope_spec, pl.BlockSpec((LANES, LANES), lambda i: (0, 0))]
    in_specs += [pl.BlockSpec((1, LANES), lambda i: (0, 0)) for _ in gains]
    return pl.pallas_call(
        functools.partial(_norm_proj_kernel, specs=specs, n_gain=len(gains)),
        out_shape=[jax.ShapeDtypeStruct((T_ALL, n), F32)]
        + [jax.ShapeDtypeStruct((T_ALL, s[1]), s[5]) for s in specs],
        grid=(N_TILES,),
        in_specs=in_specs,
        out_specs=[pl.BlockSpec((TM, n), lambda i: (i, 0))]
        + [pl.BlockSpec((TM, s[1]), lambda i: (i, 0)) for s in specs],
        compiler_params=_params(("parallel",)),
        name="norm_proj",
    )(y, gain.reshape(1, D_MODEL), scale, shift, w_bf16, rope_cos, rope_sin, bd, *gains)


def _rope_tables():
    half = HD // 2
    freqs = 1.0 / (ROPE_THETA ** (jnp.arange(0, half, 2, dtype=F32) / half))
    t = jnp.arange(DEC_SEQ)
    rows = (t // GRID_W).astype(F32)
    cols = (t % GRID_W).astype(F32)
    ang = jnp.concatenate([rows[:, None] * freqs, cols[:, None] * freqs], axis=-1)
    cos = jnp.repeat(jnp.cos(ang), 2, axis=-1)
    sin = jnp.repeat(jnp.sin(ang), 2, axis=-1)
    cos = jnp.concatenate([jnp.tile(cos, (1, LANES // HD)), jnp.ones((TM, LANES), F32)], axis=0)
    sin = jnp.concatenate([jnp.tile(sin, (1, LANES // HD)), jnp.zeros((TM, LANES), F32)], axis=0)
    return cos, sin


def _lane_slice(ref, h, width=HD):
    per = LANES // width
    blk = ref[:, (h // per) * LANES:(h // per + 1) * LANES]
    if per == 1:
        return blk
    return blk[:, (h % per) * width:(h % per + 1) * width]


def _softmax_parts(scores):
    m = None
    for s in scores:
        ms = jnp.max(s, axis=-1, keepdims=True)
        m = ms if m is None else jnp.maximum(m, ms)
    ps = [jnp.exp(s - m) for s in scores]
    l = None
    for p in ps:
        ls = jnp.sum(p, axis=-1, keepdims=True)
        l = ls if l is None else l + ls
    return ps, l


def _attn_std_kernel(*refs, group, n_kv, has_cache, has_bias, bq):
    it = iter(refs)
    q_ref, kn_ref, vn_ref = next(it), next(it), next(it)
    kc_ref = vc_ref = b_ref = None
    if has_cache:
        kc_ref, vc_ref = next(it), next(it)
    if has_bias:
        b_ref = next(it)
    o_ref = next(it)
    outs = []
    for g in range(n_kv):
        qs = jnp.concatenate([_lane_slice(q_ref, g * group + j) for j in range(group)], axis=0)
        kn = _lane_slice(kn_ref, g)
        vn = _lane_slice(vn_ref, g)
        s_new = _dot_nt(qs, kn)
        if has_bias:
            s_new = s_new + b_ref[g]
        scores = [s_new]
        if has_cache:
            scores.append(_dot_nt(qs, kc_ref[g]))
        ps, l = _softmax_parts(scores)
        o = _dot(ps[0].astype(BF16), vn)
        if has_cache:
            o = o + _dot(ps[1].astype(BF16), vc_ref[g])
        o = o / l
        for j in range(group):
            outs.append(o[j * bq:(j + 1) * bq])
    o_ref[...] = jnp.concatenate(outs, axis=1).astype(o_ref.dtype)


def _attn_diff_kernel(*refs, has_cache):
    it = iter(refs)
    lam_ref, q_ref, kn_ref, vn_ref = next(it), next(it), next(it), next(it)
    kc_ref = vc_ref = None
    if has_cache:
        kc_ref, vc_ref = next(it), next(it)
    g_ref, o_ref = next(it), next(it)
    lam = lam_ref[0]
    post = lam_ref[1]
    outs = []
    for h in range(D_HEADS):
        pd_new, pd_c = None, None
        for j in range(2):
            f = 2 * h + j
            qs = _lane_slice(q_ref, f)
            scores = [_dot_nt(qs, _lane_slice(kn_ref, f))]
            if has_cache:
                scores.append(_dot_nt(qs, kc_ref[f]))
            ps, l = _softmax_parts(scores)
            r = 1.0 / l
            if j == 0:
                pd_new = ps[0] * r
                pd_c = ps[1] * r if has_cache else None
            else:
                r = r * lam
                pd_new = pd_new - ps[0] * r
                pd_c = pd_c - ps[1] * r if has_cache else None
        o = _dot(pd_new.astype(BF16), _lane_slice(vn_ref, h, D_VDIM))
        if has_cache:
            o = o + _dot(pd_c.astype(BF16), vc_ref[h])
        ms = jnp.mean(o * o, axis=-1, keepdims=True)
        outs.append(o * lax.rsqrt(ms + EPS) * g_ref[...] * post)
    o_ref[...] = jnp.concatenate(outs, axis=1).astype(o_ref.dtype)


def _attention(q, kn, vn, *, ctx, group=1, n_kv=1, cache=None, bias=None, diff=None, bq=256):
    if ctx:
        nb, sq, row0 = BATCH, SEQ, 0
    else:
        nb, sq, row0 = DEC_BATCH, DEC_SEQ, T_CTX
    nq = sq // bq
    qb0 = row0 // bq
    kb0 = row0 // sq
    wq, wk, wv = q.shape[1], kn.shape[1], vn.shape[1]
    in_specs = [
        pl.BlockSpec((bq, wq), lambda b, i: (qb0 + b * nq + i, 0)),
        pl.BlockSpec((sq, wk), lambda b, i: (kb0 + b, 0)),
        pl.BlockSpec((sq, wv), lambda b, i: (kb0 + b, 0)),
    ]
    args = [q, kn, vn]
    if cache is not None:
        kc, vc = cache
        in_specs += [pl.BlockSpec((None,) + kc.shape[1:], lambda b, i: (b, 0, 0, 0)),
                     pl.BlockSpec((None,) + vc.shape[1:], lambda b, i: (b, 0, 0, 0))]
        args += [kc, vc]
    if diff is None:
        if bias is not None:
            in_specs.append(pl.BlockSpec((bias.shape[0], bq, sq), lambda b, i: (0, i, 0)))
            args.append(bias)
        body = functools.partial(_attn_std_kernel, group=group, n_kv=n_kv, has_cache=cache is not None,
                                 has_bias=bias is not None, bq=bq)
    else:
        lam_vec, gain = diff
        in_specs = [pl.BlockSpec(memory_space=pltpu.SMEM)] + in_specs
        args = [lam_vec] + args
        in_specs.append(pl.BlockSpec((1, D_VDIM), lambda b, i: (0, 0)))
        args.append(gain.reshape(1, D_VDIM))
        body = functools.partial(_attn_diff_kernel, has_cache=cache is not None)
    return pl.pallas_call(
        body,
        out_shape=jax.ShapeDtypeStruct((nb * sq, 512), BF16),
        grid=(nb, nq),
        in_specs=in_specs,
        out_specs=pl.BlockSpec((bq, 512), lambda b, i: (b * nq + i, 0)),
        compiler_params=_params(("parallel", "parallel")),
        name="attention",
    )(*args)


def _neighbourhood_bias(rpb):
    rows = DEC_SEQ // GRID_W
    wr = min(NA_ROWS, rows)
    r = np.arange(rows)
    c = np.arange(GRID_W)
    r0 = np.clip(r - wr // 2, 0, rows - wr)
    c0 = np.clip(c - NA_COLS // 2, 0, GRID_W - NA_COLS)
    row_ok = (r[None, :] >= r0[:, None]) & (r[None, :] < r0[:, None] + wr)
    col_ok = (c[None, :] >= c0[:, None]) & (c[None, :] < c0[:, None] + NA_COLS)
    dr = np.clip(r[None, :] - r[:, None] + (NA_ROWS - 1), 0, 2 * NA_ROWS - 2)
    dc = np.clip(c[None, :] - c[:, None], 1 - NA_COLS, NA_COLS - 1) + (NA_COLS - 1)
    oh_r = jnp.asarray(dr[..., None] == np.arange(2 * NA_ROWS - 1), F32)
    oh_c = jnp.asarray(dc[..., None] == np.arange(2 * NA_COLS - 1), F32)
    by_col = jnp.einsum("hrd,qkd->hrqk", rpb.astype(F32), oh_c, precision=HIGHEST)
    full = jnp.einsum("PKr,hrqk->hPqKk", oh_r, by_col, precision=HIGHEST)
    ok = jnp.asarray(row_ok[:, None, :, None] & col_ok[None, :, None, :])
    return jnp.where(ok[None], full, NEG_BIG).reshape(C_HEADS, DEC_SEQ, DEC_SEQ)


def _log_sigmoid(x):
    return jnp.minimum(x, 0.0) - jnp.log1p(jnp.exp(-jnp.abs(x)))


def _mlstm_kernel(q_ref, k_ref, v_ref, g_ref, gb_ref, c0_ref, n0_ref, m0_ref, h_ref, c_ref, n_ref, m_ref, hb_ref,
                  *, seq):
    L = B_CHUNK
    nc = seq // L
    row = lax.broadcasted_iota(jnp.int32, (L, L), 0)
    col = lax.broadcasted_iota(jnp.int32, (L, L), 1)
    keeps = (col <= row, col >= row)
    k_scale = B_DK ** -0.5
    c_ref[...] = c0_ref[...]
    n_ref[...] = n0_ref[...]
    m_ref[...] = m0_ref[...]

    def step(j):
        for d in range(2):
            keep = keeps[d]
            c = j if d == 0 else nc - 1 - j
            off = c * L if isinstance(c, int) else pl.multiple_of(c * L, L)
            gates = g_ref[pl.ds(off, L), :] + gb_ref[...]
            cum = _dot(keep.astype(F32), _log_sigmoid(gates), HIGHEST)
            cum_t = cum.T
            gates_t = gates.T
            out_ref = h_ref if d == 0 else hb_ref
            for h in range(B_HEADS):
                ci = (2 * d) * B_HEADS + h
                cf = (2 * d + 1) * B_HEADS + h
                hs = slice(h * B_DK, (h + 1) * B_DK)
                C = c_ref[d, h]
                n = n_ref[d, h]
                m = m_ref[d, h][:, 0:1]
                qc = q_ref[pl.ds(off, L), hs]
                kc = k_ref[pl.ds(off, L), hs] * k_scale
                vc = v_ref[pl.ds(off, L), hs]
                b_col = cum[:, cf:cf + 1]
                i_col = gates[:, ci:ci + 1]
                b_row = cum_t[cf:cf + 1, :]
                i_row = gates_t[ci:ci + 1, :]
                dlog = jnp.where(keep, b_col - b_row + i_row, -jnp.inf)
                inter = b_col + m
                m_t = jnp.maximum(inter, jnp.max(dlog, axis=-1, keepdims=True))
                w_intra = jnp.exp(dlog - m_t)
                w_inter = jnp.exp(inter - m_t)
                qb = qc.astype(BF16)
                vb = vc.astype(BF16)
                qk = _dot_nt(qb, kc.astype(BF16)) * w_intra
                num = _dot(qk.astype(BF16), vb) + w_inter * _dot(qb, C.astype(BF16))
                den = jnp.sum(qk, axis=-1, keepdims=True) + w_inter * jnp.sum(qc * n, axis=-1, keepdims=True)
                out_ref[pl.ds(off, L), hs] = num / jnp.maximum(jnp.abs(den), jnp.exp(-m_t))
                b_last = b_col[L - 1:L, :] if d == 0 else b_col[0:1, :]
                end_col = b_last - b_col + i_col
                m_new = jnp.maximum(b_last + m, jnp.max(end_col, axis=0, keepdims=True))
                w_end = jnp.exp(end_col - m_new)
                decay = jnp.exp(b_last + m - m_new)
                kw = kc * w_end
                c_ref[d, h] = decay * C + _dot_tn(kw.astype(BF16), vb)
                n_ref[d, h] = decay * n + jnp.sum(kw, axis=0, keepdims=True)
                m_ref[d, h] = jnp.broadcast_to(m_new, (1, LANES))

    if nc <= 2:
        for j in range(nc):
            step(j)
    else:
        def body(j, carry):
            step(j)
            return carry

        lax.fori_loop(0, nc, body, 0)
    h_ref[...] = h_ref[...] + hb_ref[...]


def _mlstm(p, gate_bias, c0, n0, m0, *, ctx):
    if ctx:
        nb, seq, blk0 = BATCH, SEQ, 0
    else:
        nb, seq, blk0 = DEC_BATCH, DEC_SEQ, T_CTX // DEC_SEQ
    w = B_HEADS * B_DK

    def cols(c0_, width):
        return pl.BlockSpec((seq, width), lambda b: (blk0 + b, c0_ // width))

    gb = jnp.zeros((1, LANES), F32).at[0, :4 * B_HEADS].set(gate_bias.reshape(-1).astype(F32))
    st = lambda shape: pl.BlockSpec((None,) + shape, lambda b: (b, 0, 0, 0, 0))
    return pl.pallas_call(
        functools.partial(_mlstm_kernel, seq=seq),
        out_shape=[
            jax.ShapeDtypeStruct((nb * seq, w), F32),
            jax.ShapeDtypeStruct((nb, 2, B_HEADS, B_DK, B_DV), F32),
            jax.ShapeDtypeStruct((nb, 2, B_HEADS, 1, B_DK), F32),
            jax.ShapeDtypeStruct((nb, 2, B_HEADS, 1, LANES), F32),
        ],
        grid=(nb,),
        in_specs=[
            cols(EV_BQ, w), cols(EV_BK, w), cols(EV_BV, w), cols(EV_BG, LANES),
            pl.BlockSpec((1, LANES), lambda b: (0, 0)),
            st((2, B_HEADS, B_DK, B_DV)), st((2, B_HEADS, 1, B_DK)), st((2, B_HEADS, 1, LANES)),
        ],
        out_specs=[
            pl.BlockSpec((seq, w), lambda b: (b, 0)),
            st((2, B_HEADS, B_DK, B_DV)), st((2, B_HEADS, 1, B_DK)), st((2, B_HEADS, 1, LANES)),
        ],
        scratch_shapes=[pltpu.VMEM((seq, w), F32)],
        compiler_params=_params(("parallel",)),
        name="mlstm",
    )(p, p, p, p, gb, c0, n0, m0)


def _merge_value(refs, even):
    if even:
        a_ref, hb_ref, bo_ref, ng_ref, w_ref, y_ref, g_ref = refs
        parts = [a_ref[...]]
        for h in range(B_HEADS):
            hs = slice(h * B_DV, (h + 1) * B_DV)
            x = hb_ref[:, hs]
            ms = jnp.mean(x * x, axis=-1, keepdims=True)
            xn = x * lax.rsqrt(ms + EPS) * ng_ref[:, hs]
            parts.append((jax.nn.sigmoid(bo_ref[:, hs]) * xn).astype(BF16))
    else:
        a_ref, b_ref, w_ref, y_ref, g_ref = refs
        parts = [a_ref[...], b_ref[...]]
    cat = jnp.concatenate(parts, axis=1)
    return y_ref[...] + g_ref[...] * _dot(cat, w_ref[...])


def _merge_specs(y, gate, w_bf16, a, b, p, norm_gain):
    even = p is not None
    half = pl.BlockSpec((TM, 512), lambda i: (i, 0))
    in_specs = [half, half]
    args = [a, b]
    if even:
        in_specs += [pl.BlockSpec((TM, 512), lambda i: (i, EV_BO // 512)), pl.BlockSpec((1, 512), lambda i: (0, 0))]
        args += [p, norm_gain.reshape(1, 512)]
    in_specs += [
        pl.BlockSpec((D_MODEL, D_MODEL), lambda i: (0, 0)),
        pl.BlockSpec((TM, D_MODEL), lambda i: (i, 0)),
        pl.BlockSpec((None, 1, D_MODEL), lambda i: (_mod_row(i), 0, 0)),
    ]
    args += [w_bf16, y, gate]
    return in_specs, args


SLAB = D_MODEL // LANES


def _load_slabs(ref, rows):
    return jnp.concatenate([ref[pl.ds(c, rows, stride=SLAB), :] for c in range(SLAB)], axis=1)


def _store_slabs(ref, x):
    for c in range(SLAB):
        ref[pl.ds(c, x.shape[0], stride=SLAB), :] = x[:, c * LANES:(c + 1) * LANES]


def _slab(ref, idx):
    return ref.at[pl.ds(pl.multiple_of(idx * SLAB, SLAB), SLAB)]


def _merge_router_kernel(*refs, even):
    n_merge = 7 if even else 5
    merge_refs = refs[:n_merge]
    (g_ref, sc_ref, sh_ref, whi_ref, wlo_ref, b_ref,
     y_out_ref, h_ref, ti_ref, tp_ref, rk_ref, cnt_ref, base_ref) = refs[n_merge:]

    @pl.when(pl.program_id(0) == 0)
    def _():
        base_ref[...] = jnp.zeros(base_ref.shape, F32)

    y = _merge_value(merge_refs, even)
    y_out_ref[...] = y
    h = _norm_mod(y, g_ref[...], sc_ref[...], sh_ref[...])
    _store_slabs(h_ref, h)
    h_hi = h.astype(BF16)
    h_lo = (h - h_hi.astype(F32)).astype(BF16)
    logits = (_dot(h_hi, whi_ref[...]) + (_dot(h_hi, wlo_ref[...]) + _dot(h_lo, whi_ref[...]))
              + b_ref[...])
    lane = lax.broadcasted_iota(jnp.int32, logits.shape, 1)
    lane_f = lane.astype(F32)
    vals, idxs = [], []
    for _ in range(TOP_K):
        mx = jnp.max(logits, axis=-1, keepdims=True)
        ix = jnp.min(jnp.where(logits == mx, lane_f, float(LANES)), axis=-1, keepdims=True)
        vals.append(mx)
        idxs.append(ix)
        logits = jnp.where(lane_f == ix, -jnp.inf, logits)
    es = [jnp.exp(v - vals[0]) for v in vals]
    tot = es[0] + es[1] + es[2] + es[3]
    ti = jnp.zeros(logits.shape, F32)
    tp = jnp.zeros(logits.shape, F32)
    for k in range(TOP_K):
        ti = jnp.where(lane == k, idxs[k], ti)
        tp = jnp.where(lane == k, es[k] / tot, tp)
    ti_ref[...] = ti.astype(jnp.int32)
    tp_ref[...] = tp
    onehots = [(lane_f == ix).astype(F32) for ix in idxs]
    cnt = onehots[0] + onehots[1] + onehots[2] + onehots[3]
    row = lax.broadcasted_iota(jnp.int32, (TM, TM), 0)
    col = lax.broadcasted_iota(jnp.int32, (TM, TM), 1)
    before = _dot((col < row).astype(BF16), cnt.astype(BF16)) + base_ref[...]
    rk = jnp.zeros(logits.shape, jnp.int32)
    for k in range(TOP_K):
        r_k = jnp.sum(onehots[k] * before, axis=-1, keepdims=True).astype(jnp.int32)
        rk = jnp.where(lane == k, r_k, rk)
    rk_ref[...] = rk
    base_ref[...] = base_ref[...] + jnp.sum(cnt, axis=0, keepdims=True)
    cnt_ref[...] = base_ref[...]


def _merge_router(y, gate, w_out_bf16, a, b, gain, scale, shift, rw, rb, *, p=None, norm_gain=None):
    merge_in_specs, merge_args = _merge_specs(y, gate, w_out_bf16, a, b, p, norm_gain)
    vec = pl.BlockSpec((None, 1, D_MODEL), lambda i: (_mod_row(i), 0, 0))
    rw_p = jnp.zeros((D_MODEL, LANES), F32).at[:, :N_EXPERTS].set(rw)
    rb_p = jnp.full((1, LANES), NEG_BIG, F32).at[0, :N_EXPERTS].set(rb)
    rw_hi = rw_p.astype(BF16)
    tile = lambda w: pl.BlockSpec((TM, w), lambda i: (i, 0))
    return pl.pallas_call(
        functools.partial(_merge_router_kernel, even=p is not None),
        out_shape=[jax.ShapeDtypeStruct((T_ALL, D_MODEL), F32),
                   jax.ShapeDtypeStruct((T_ALL * SLAB, LANES), F32),
                   jax.ShapeDtypeStruct((T_ALL, LANES), jnp.int32),
                   jax.ShapeDtypeStruct((T_ALL, LANES), F32),
                   jax.ShapeDtypeStruct((T_ALL, LANES), jnp.int32),
                   jax.ShapeDtypeStruct((1, LANES), F32)],
        grid=(N_TILES,),
        in_specs=merge_in_specs + [
            pl.BlockSpec((1, D_MODEL), lambda i: (0, 0)), vec, vec,
            pl.BlockSpec((D_MODEL, LANES), lambda i: (0, 0)), pl.BlockSpec((D_MODEL, LANES), lambda i: (0, 0)),
            pl.BlockSpec((1, LANES), lambda i: (0, 0))],
        out_specs=[tile(D_MODEL), pl.BlockSpec((TM * SLAB, LANES), lambda i: (i, 0)), tile(LANES), tile(LANES),
                   tile(LANES), pl.BlockSpec((1, LANES), lambda i: (0, 0))],
        scratch_shapes=[pltpu.VMEM((1, LANES), F32)],
        compiler_params=_params(("arbitrary",)),
        name="merge_router",
    )(*merge_args, gain.reshape(1, D_MODEL), scale, shift, rw_hi, (rw_p - rw_hi.astype(F32)).astype(BF16), rb_p)


def _route_plan(top_i, rank, counts):
    experts = jnp.arange(N_EXPERTS, dtype=jnp.int32)
    padded = ((counts + MOE_TM - 1) // MOE_TM) * MOE_TM
    seg_end = jnp.cumsum(padded)
    seg_start = seg_end - padded
    onehot = top_i[..., None] == experts
    pos = jnp.sum(jnp.where(onehot, seg_start, 0), axis=-1) + rank
    n_active = seg_end[-1] // MOE_TM
    fill = jnp.concatenate([seg_start + counts, padded - counts, n_active[None]]).astype(jnp.int32)
    tile_start = jnp.arange(MOE_TILES, dtype=jnp.int32) * MOE_TM
    tile_expert = jnp.sum((seg_end[None, :] <= tile_start[:, None]).astype(jnp.int32), axis=1)
    last = jnp.sum((seg_end <= (n_active - 1) * MOE_TM).astype(jnp.int32))
    tile_expert = jnp.minimum(jnp.where(tile_start < seg_end[-1], tile_expert, last), N_EXPERTS - 1)
    owns = (padded > 0).astype(jnp.int32)
    run_of_expert = jnp.cumsum(owns) - 1
    run_expert = jnp.sum(jnp.where((run_of_expert[None, :] == experts[:, None]) & (owns[None, :] > 0),
                                   experts[None, :], 0), axis=1)
    runs = jnp.concatenate([run_expert, jnp.sum(owns)[None]]).astype(jnp.int32)
    tile_run = jnp.sum(jnp.where(tile_expert[:, None] == experts[None, :], run_of_expert[None, :], 0), axis=1)
    experts_plan = (tile_expert.astype(jnp.int32), n_active.reshape(1).astype(jnp.int32),
                    tile_run.astype(jnp.int32), runs)
    return pos.reshape(-1).astype(jnp.int32), fill, experts_plan


DMA_UNROLL = 4
DMA_QUEUES = 2


def _wait_slabs(ref, n_slabs, sem):
    view = ref.at[pl.ds(0, n_slabs * SLAB)]
    pltpu.make_async_copy(view, view, sem).wait()


def _dispatch_kernel(pos_ref, fill_ref, h_ref, xs_ref, sem):
    i = pl.program_id(0)
    base = i * TM

    def issue(j, carry):
        for u in range(DMA_UNROLL):
            t = j * DMA_UNROLL + u
            for k in range(TOP_K):
                dst = _slab(xs_ref, pos_ref[(base + t) * TOP_K + k])
                pltpu.make_async_copy(_slab(h_ref, t), dst, sem).start(priority=k % DMA_QUEUES)
        return carry

    lax.fori_loop(0, TM // DMA_UNROLL, issue, 0)

    @pl.when(i == 0)
    def _():
        def per_expert(e, total):
            start = fill_ref[e]
            n = fill_ref[N_EXPERTS + e]

            def one(r, carry):
                pltpu.make_async_copy(_slab(h_ref, 0), _slab(xs_ref, start + r), sem).start()
                return carry

            lax.fori_loop(0, n, one, 0)
            return total + n

        total = lax.fori_loop(0, N_EXPERTS, per_expert, 0)

        n_active = fill_ref[2 * N_EXPERTS]

        def unused_tile(ti, carry):
            dst = xs_ref.at[pl.ds(pl.multiple_of(ti * (MOE_TM * SLAB), MOE_TM * SLAB), MOE_TM * SLAB)]
            pltpu.make_async_copy(h_ref, dst, sem).start()
            return carry

        lax.fori_loop(n_active, MOE_TILES, unused_tile, 0)
        total = total + (MOE_TILES - n_active) * MOE_TM

        @pl.when(total > 0)
        def _():
            _wait_slabs(xs_ref, total, sem)

    _wait_slabs(xs_ref, TM * TOP_K, sem)


def _dispatch(h_slabs, pos, fill):
    grid_spec = pltpu.PrefetchScalarGridSpec(
        num_scalar_prefetch=2,
        grid=(N_TILES,),
        in_specs=[pl.BlockSpec((TM * SLAB, LANES), lambda i, pos, fill: (i, 0))],
        out_specs=pl.BlockSpec(memory_space=pl.ANY),
        scratch_shapes=[pltpu.SemaphoreType.DMA],
    )
    return pl.pallas_call(
        _dispatch_kernel,
        out_shape=jax.ShapeDtypeStruct((MOE_ROWS * SLAB, LANES), F32),
        grid_spec=grid_spec,
        compiler_params=_params(("arbitrary",)),
        name="moe_dispatch",
    )(pos, fill, h_slabs)


def _moe_kernel(te_ref, na_ref, ts_ref, ex_ref, x_ref, wgu_hbm, bgu_ref, wd_hbm, bd_ref, o_ref,
                wgu_f32, wd_f32, wgu_bf, wd_bf, sems, *, layer):
    i = pl.program_id(0)
    s = ts_ref[i]
    first = (i == 0) | (s != ts_ref[jnp.maximum(i - 1, 0)])

    def weight_copies(slot):
        e = ex_ref[slot]
        b = slot % 2
        return (pltpu.make_async_copy(wgu_hbm.at[layer, e], wgu_f32.at[b], sems.at[0, b]),
                pltpu.make_async_copy(wd_hbm.at[layer, e], wd_f32.at[b], sems.at[1, b]))

    @pl.when(i == 0)
    def _():
        for cp in weight_copies(0):
            cp.start()

    @pl.when(first)
    def _():
        for cp in weight_copies(s):
            cp.wait()

        @pl.when(s + 1 < ex_ref[N_EXPERTS])
        def _():
            for cp in weight_copies(s + 1):
                cp.start()

        b = s % 2
        wgu_bf[...] = wgu_f32[b].astype(BF16)
        wd_bf[...] = wd_f32[b].astype(BF16)

    @pl.when(i < na_ref[0])
    def _():
        x = _load_slabs(x_ref, MOE_TM).astype(BF16)
        gu = _dot(x, wgu_bf[...]) + bgu_ref[...]
        gate = jnp.minimum(gu[:, :D_FF], SWIGLU_LIMIT)
        up = jnp.clip(gu[:, D_FF:], -SWIGLU_LIMIT, SWIGLU_LIMIT)
        act = (up + 1.0) * gate * jax.nn.sigmoid(SWIGLU_ALPHA * gate)
        _store_slabs(o_ref, _dot(act.astype(BF16), wd_bf[...]) + bd_ref[...])

    @pl.when(i >= na_ref[0])
    def _():
        o_ref[...] = jnp.zeros(o_ref.shape, o_ref.dtype)


def _moe_experts(layer, xs, plan, w_gu, b_gu, w_down, b_down):
    grid_spec = pltpu.PrefetchScalarGridSpec(
        num_scalar_prefetch=4,
        grid=(MOE_TILES,),
        in_specs=[
            pl.BlockSpec((MOE_TM * SLAB, LANES), lambda i, te, na, ts, ex: (jnp.minimum(i, na[0] - 1), 0)),
            pl.BlockSpec(memory_space=pl.ANY),
            pl.BlockSpec((None, None, 1, 2 * D_FF), lambda i, te, na, ts, ex: (layer, te[i], 0, 0)),
            pl.BlockSpec(memory_space=pl.ANY),
            pl.BlockSpec((None, None, 1, D_MODEL), lambda i, te, na, ts, ex: (layer, te[i], 0, 0)),
        ],
        out_specs=pl.BlockSpec((MOE_TM * SLAB, LANES), lambda i, te, na, ts, ex: (i, 0)),
        scratch_shapes=[pltpu.VMEM((2, D_MODEL, 2 * D_FF), F32), pltpu.VMEM((2, D_FF, D_MODEL), F32),
                        pltpu.VMEM((D_MODEL, 2 * D_FF), BF16), pltpu.VMEM((D_FF, D_MODEL), BF16),
                        pltpu.SemaphoreType.DMA((2, 2))],
    )
    return pl.pallas_call(
        functools.partial(_moe_kernel, layer=layer),
        out_shape=jax.ShapeDtypeStruct((MOE_ROWS * SLAB, LANES), F32),
        grid_spec=grid_spec,
        compiler_params=_params(("arbitrary",)),
        name="moe_experts",
    )(*plan, xs, w_gu, b_gu.reshape(DEPTH, N_EXPERTS, 1, 2 * D_FF), w_down,
      b_down.reshape(DEPTH, N_EXPERTS, 1, D_MODEL))


def _combine_kernel(pos_ref, out_ref, y_ref, tp_ref, g_ref, *rest, final):
    if final:
        fg_ref, o_ref, n_ref, buf, sem = rest
    else:
        o_ref, buf, sem = rest
    base = pl.program_id(0) * TM

    def issue(j, carry):
        for u in range(DMA_UNROLL):
            t = j * DMA_UNROLL + u
            for k in range(TOP_K):
                src = _slab(out_ref, pos_ref[(base + t) * TOP_K + k])
                pltpu.make_async_copy(src, _slab(buf.at[k], t), sem).start(priority=k % DMA_QUEUES)
        return carry

    lax.fori_loop(0, TM // DMA_UNROLL, issue, 0)
    _wait_slabs(out_ref, TM * TOP_K, sem)

    tp = tp_ref[...]
    ss = jnp.zeros((TM, 1), F32)
    chunks = []
    for c in range(SLAB):
        cs = slice(c * LANES, (c + 1) * LANES)
        acc = tp[:, 0:1] * buf[0, pl.ds(c, TM, stride=SLAB), :]
        for k in range(1, TOP_K):
            acc = acc + tp[:, k:k + 1] * buf[k, pl.ds(c, TM, stride=SLAB), :]
        yc = y_ref[:, cs] + g_ref[:, cs] * acc
        o_ref[:, cs] = yc
        if final:
            ss = ss + jnp.sum(yc * yc, axis=-1, keepdims=True)
            chunks.append(yc)
    if final:
        inv = lax.rsqrt(ss * (1.0 / D_MODEL) + EPS)
        for c in range(SLAB):
            cs = slice(c * LANES, (c + 1) * LANES)
            n_ref[:, cs] = chunks[c] * inv * fg_ref[:, cs]


def _combine(y, out_slabs, pos, top_p, gate, final_gain=None):
    final = final_gain is not None
    tile = pl.BlockSpec((TM, D_MODEL), lambda i, pos: (i, 0))
    in_specs = [pl.BlockSpec(memory_space=pl.ANY), tile, pl.BlockSpec((TM, LANES), lambda i, pos: (i, 0)),
                pl.BlockSpec((None, 1, D_MODEL), lambda i, pos: (_mod_row(i), 0, 0))]
    args = [out_slabs, y, top_p, gate]
    out_shape = [jax.ShapeDtypeStruct((T_ALL, D_MODEL), F32)]
    out_specs = [tile]
    if final:
        in_specs.append(pl.BlockSpec((1, D_MODEL), lambda i, pos: (0, 0)))
        args.append(final_gain.reshape(1, D_MODEL))
        out_shape.append(jax.ShapeDtypeStruct((T_ALL, D_MODEL), F32))
        out_specs.append(tile)
    grid_spec = pltpu.PrefetchScalarGridSpec(
        num_scalar_prefetch=1,
        grid=(N_TILES,),
        in_specs=in_specs,
        out_specs=out_specs,
        scratch_shapes=[pltpu.VMEM((TOP_K, TM * SLAB, LANES), F32), pltpu.SemaphoreType.DMA],
    )
    return pl.pallas_call(
        functools.partial(_combine_kernel, final=final),
        out_shape=out_shape,
        grid_spec=grid_spec,
        compiler_params=_params(("arbitrary",)),
        name="moe_combine",
    )(pos, *args)


def _moe_layer(layer, y, routed, gate, w_gu, b_gu, w_down, b_down, final_gain=None):
    h_slabs, top_i, top_p, rank, counts = routed
    pos, fill, experts_plan = _route_plan(top_i[:, :TOP_K], rank[:, :TOP_K],
                                          counts[0, :N_EXPERTS].astype(jnp.int32))
    xs = _dispatch(h_slabs, pos, fill)
    out = _moe_experts(layer, xs, experts_plan, w_gu, b_gu, w_down, b_down)
    return _combine(y, out, pos, top_p, gate, final_gain)


def _heads(x, n):
    b = x.shape[0] // SEQ
    return x.reshape(b, SEQ, n, -1).transpose(0, 2, 1, 3)


def kernel(x_prompt, x_sample, c, cache_a_k, cache_a_v, state_b_C, state_b_n, state_b_m, cache_c_k, cache_c_v, cache_d_k, cache_d_v, c_ctx, w_mod, b_mod, norm1_g, norm2_g, w_in_even, w_out_even, a_q_gain, a_k_gain, b_gate_bias, b_norm_gain, w_in_odd, w_out_odd, c_rpb, d_lambda, d_norm_gain, router_w, router_b, expert_w_gu, expert_b_gu, expert_w_down, expert_b_down, final_norm_g):
    y = jnp.concatenate([x_prompt.reshape(T_CTX, D_MODEL), x_sample.reshape(T_LAT, D_MODEL)], axis=0)
    cond = jnp.zeros((MOD_ROWS, D_MODEL), F32).at[0].set(c_ctx).at[1:1 + DEC_BATCH].set(c)
    mod = _modulation(cond, w_mod, b_mod).reshape(DEPTH, MOD_ROWS, 6, 1, D_MODEL)
    rope_cos, rope_sin = _rope_tables()
    scale = HD ** -0.5
    outs = {}

    for layer in range(DEPTH):
        sh1, sc1, g1, sh2, sc2, g2 = (mod[layer, :, k] for k in range(6))
        j = layer // 2
        if layer % 2 == 0:
            w = w_in_even[j]
            sizes = np.cumsum([0, 512, 128, 128, 512, 512, 512, 512, 16])
            aq, ak, av, bq, bk, bv, bo, bg = (w[:, sizes[k]:sizes[k + 1]] for k in range(8))
            w_in = jnp.concatenate([aq, bo, bq, bk, bv, ak, av, bg, jnp.zeros((D_MODEL, EV_N - EV_BG - 16), F32)],
                                   axis=1).astype(BF16)
            qg = jnp.tile(a_q_gain[j], LANES // HD).reshape(1, LANES)
            kg = jnp.tile(a_k_gain[j], LANES // HD).reshape(1, LANES)
            specs = ((EV_AQ, 512, 0, True, scale, BF16), (EV_AK, 128, 1, False, 1.0, F32),
                     (EV_AK, 128, 1, True, 1.0, BF16), (EV_AV, 128, None, False, 1.0, BF16))
            p, qa, ka_f32, ka, va = _norm_proj(y, norm1_g[layer], sc1, sh1, w_in, rope_cos, rope_sin, [qg, kg],
                                               specs)
            oa_ctx = _attention(qa, ka, va, ctx=True, group=A_HEADS // A_KV, n_kv=A_KV)
            cache = (cache_a_k[:, j].astype(BF16), cache_a_v[:, j].astype(BF16))
            oa_lat = _attention(qa, ka, va, ctx=False, group=A_HEADS // A_KV, n_kv=A_KV, cache=cache)
            zc = jnp.zeros((BATCH, 2, B_HEADS, B_DK, B_DV), F32)
            zn = jnp.zeros((BATCH, 2, B_HEADS, 1, B_DK), F32)
            zm = jnp.zeros((BATCH, 2, B_HEADS, 1, LANES), F32)
            hb_ctx, bC, bn, bm = _mlstm(p, b_gate_bias[j], zc, zn, zm, ctx=True)
            m0 = jnp.broadcast_to(state_b_m[:, j][..., None, None], (DEC_BATCH, 2, B_HEADS, 1, LANES))
            hb_lat, _, _, _ = _mlstm(p, b_gate_bias[j], state_b_C[:, j], state_b_n[:, j][:, :, :, None, :], m0,
                                     ctx=False)
            oa = jnp.concatenate([oa_ctx, oa_lat], axis=0)
            hb = jnp.concatenate([hb_ctx, hb_lat], axis=0)
            y, *routed = _merge_router(y, g1, w_out_even[j].astype(BF16), oa, hb, norm2_g[layer], sc2, sh2,
                                       router_w[layer], router_b[layer], p=p, norm_gain=b_norm_gain[j])
            outs.setdefault("a_k", []).append(_heads(ka_f32[:T_CTX], A_KV))
            outs.setdefault("a_v", []).append(_heads(p[:T_CTX, EV_AV:EV_AV + 128], A_KV))
            outs.setdefault("b_C", []).append(bC)
            outs.setdefault("b_n", []).append(bn[:, :, :, 0, :])
            outs.setdefault("b_m", []).append(bm[:, :, :, 0, 0])
        else:
            specs = ((0, 512, None, False, scale, BF16), (512, 512, None, False, 1.0, BF16),
                     (1024, 512, None, False, 1.0, BF16), (1536, 512, None, True, scale, BF16),
                     (2048, 512, None, True, 1.0, BF16), (2560, 512, None, False, 1.0, BF16))
            p, qc, kc, vc, qd, kd, vd = _norm_proj(y, norm1_g[layer], sc1, sh1, w_in_odd[j].astype(BF16),
                                                   rope_cos, rope_sin, [], specs)
            lam_init = 0.8 - 0.6 * math.exp(-0.3 * layer)
            lp = d_lambda[j].astype(F32)
            lam = jnp.exp(jnp.sum(lp[0] * lp[1])) - jnp.exp(jnp.sum(lp[2] * lp[3])) + lam_init
            lam_vec = jnp.stack([lam, jnp.asarray(1.0 - lam_init, F32)]).astype(F32)
            diff = (lam_vec, d_norm_gain[j])
            oc_ctx = _attention(qc, kc, vc, ctx=True, n_kv=C_HEADS)
            od_ctx = _attention(qd, kd, vd, ctx=True, diff=diff)
            bias = _neighbourhood_bias(c_rpb[j])
            oc_lat = _attention(qc, kc, vc, ctx=False, n_kv=C_HEADS, bias=bias,
                                cache=(cache_c_k[:, j].astype(BF16), cache_c_v[:, j].astype(BF16)))
            kd_cache = cache_d_k[:, j].reshape(DEC_BATCH, 2 * D_HEADS, PAST_LEN, HD).astype(BF16)
            od_lat = _attention(qd, kd, vd, ctx=False, diff=diff, cache=(kd_cache, cache_d_v[:, j].astype(BF16)))
            oc = jnp.concatenate([oc_ctx, oc_lat], axis=0)
            od = jnp.concatenate([od_ctx, od_lat], axis=0)
            y, *routed = _merge_router(y, g1, w_out_odd[j].astype(BF16), oc, od, norm2_g[layer], sc2, sh2,
                                       router_w[layer], router_b[layer])
            pc = p[:T_CTX]
            outs.setdefault("c_k", []).append(_heads(pc[:, 512:1024], C_HEADS))
            outs.setdefault("c_v", []).append(_heads(pc[:, 1024:1536], C_HEADS))
            outs.setdefault("d_k", []).append(_heads(pc[:, 2048:2560], 2 * D_HEADS).reshape(BATCH, D_HEADS, 2, SEQ, HD))
            outs.setdefault("d_v", []).append(_heads(pc[:, 2560:3072], D_HEADS))
        res = _moe_layer(layer, y, routed, g2, expert_w_gu, expert_b_gu, expert_w_down, expert_b_down,
                         final_gain=final_norm_g if layer == DEPTH - 1 else None)
        y = res[0]
    y_norm = res[1]
    stack = lambda k: jnp.stack(outs[k], axis=1)
    return (y_norm[:T_CTX].reshape(BATCH, SEQ, D_MODEL), y_norm[T_CTX:].reshape(DEC_BATCH, DEC_SEQ, D_MODEL),
            stack("a_k"), stack("a_v"), stack("b_C"), stack("b_n"), stack("b_m"),
            stack("c_k"), stack("c_v"), stack("d_k"), stack("d_v"))
```

```python
import functools
import math

import numpy as np
import jax
import jax.numpy as jnp
from jax import lax
from jax.experimental import pallas as pl
from jax.experimental.pallas import tpu as pltpu

D_MODEL = 1024
BATCH = 32
SEQ = 256
DEPTH = 2
DEC_BATCH = 8
DEC_SEQ = 1024
PAST_LEN = 512
GRID_W = 64
HD = 64
A_HEADS = 8
A_KV = 2
B_HEADS = 4
B_DK = 128
B_DV = 128
B_CHUNK = 128
C_HEADS = 8
NA_ROWS = 8
NA_COLS = 16
D_HEADS = 4
D_VDIM = 2 * HD
N_EXPERTS = 32
TOP_K = 4
D_FF = 1024
SWIGLU_LIMIT = 7.0
SWIGLU_ALPHA = 1.702
ROPE_THETA = 10000.0
EPS = 1e-6

F32 = jnp.float32
BF16 = jnp.bfloat16
HIGHEST = lax.Precision.HIGHEST

T_CTX = BATCH * SEQ
T_LAT = DEC_BATCH * DEC_SEQ
T_ALL = T_CTX + T_LAT
TM = 256
CTX_TILES = T_CTX // TM
LAT_TILES_PER_BATCH = DEC_SEQ // TM
N_TILES = T_ALL // TM
MOD_ROWS = 16
LANES = 128
NEG_BIG = -1e30
MOE_TM = 256
N_ASSIGN = T_ALL * TOP_K
MOE_ROWS = N_ASSIGN + N_EXPERTS * MOE_TM
MOE_TILES = MOE_ROWS // MOE_TM
VMEM_LIMIT = 56 * 1024 * 1024

EV_AQ, EV_BO, EV_BQ, EV_BK, EV_BV, EV_AK, EV_AV, EV_BG = 0, 512, 1024, 1536, 2048, 2560, 2688, 2816
EV_N = 2944
OD_N = 3072


def _params(sem, vmem=VMEM_LIMIT):
    return pltpu.CompilerParams(dimension_semantics=sem, vmem_limit_bytes=vmem)


def _mod_row(i):
    return jnp.where(i < CTX_TILES, 0, 1 + (i - CTX_TILES) // LAT_TILES_PER_BATCH)


def _rope_block(i):
    return jnp.where(i < CTX_TILES, LAT_TILES_PER_BATCH, (i - CTX_TILES) % LAT_TILES_PER_BATCH)


def _dot(a, b, precision=None):
    return jnp.dot(a, b, preferred_element_type=F32, precision=precision)


def _dot_nt(a, b):
    return lax.dot_general(a, b, (((1,), (1,)), ((), ())), preferred_element_type=F32)


def _dot_tn(a, b):
    return lax.dot_general(a, b, (((0,), (0,)), ((), ())), preferred_element_type=F32)


def _modulation_kernel(c_ref, w_ref, b_ref, o_ref):
    c = c_ref[...]
    s = c * jax.nn.sigmoid(c)
    o_ref[...] = _dot(s, w_ref[...], HIGHEST) + b_ref[...]


def _modulation(cond, w_mod, b_mod):
    tn = 1536
    return pl.pallas_call(
        _modulation_kernel,
        out_shape=jax.ShapeDtypeStruct((DEPTH, MOD_ROWS, 6 * D_MODEL), F32),
        grid=(DEPTH, 6 * D_MODEL // tn),
        in_specs=[
            pl.BlockSpec((MOD_ROWS, D_MODEL), lambda l, j: (0, 0)),
            pl.BlockSpec((None, D_MODEL, tn), lambda l, j: (l, 0, j)),
            pl.BlockSpec((None, 1, tn), lambda l, j: (l, 0, j)),
        ],
        out_specs=pl.BlockSpec((None, MOD_ROWS, tn), lambda l, j: (l, 0, j)),
        compiler_params=_params(("parallel", "parallel")),
        name="modulation",
    )(cond, w_mod, b_mod.reshape(DEPTH, 1, 6 * D_MODEL))


def _norm_mod(y, g, sc, sh):
    ms = jnp.mean(y * y, axis=-1, keepdims=True)
    return (y * lax.rsqrt(ms + EPS) * g) * (1.0 + sc) + sh


def _rope_rotate(x):
    w = x.shape[-1]
    lane = lax.broadcasted_iota(jnp.int32, x.shape, 1)
    nxt = pltpu.roll(x, w - 1, 1)
    prv = pltpu.roll(x, 1, 1)
    return jnp.where((lane & 1) == 0, -nxt, prv)


def _is_ctx_tile():
    return pl.program_id(0) < CTX_TILES


def _pair_specs(width):
    return [pl.BlockSpec((TM, width), lambda i: (jnp.minimum(i, CTX_TILES - 1), 0)),
            pl.BlockSpec((TM, width), lambda i: (jnp.maximum(i - CTX_TILES, 0), 0))]


def _pair_value(ctx_ref, lat_ref, cols=slice(None)):
    return jnp.where(_is_ctx_tile(), ctx_ref[:, cols], lat_ref[:, cols])


def _norm_proj_kernel(*refs, specs, n_gain, y_pair):
    n_y = 2 if y_pair else 1
    y = _pair_value(refs[0], refs[1]) if y_pair else refs[0][...]
    g_ref, sc_ref, sh_ref, w_ref, cos_ref, sin_ref, bd_ref = refs[n_y:n_y + 7]
    gain_refs = refs[n_y + 7:n_y + 7 + n_gain]
    p_ref = refs[n_y + 7 + n_gain]
    out_refs = refs[n_y + 8 + n_gain:]
    h = _norm_mod(y, g_ref[...], sc_ref[...], sh_ref[...])
    p_ref[...] = _dot(h.astype(BF16), w_ref[...])
    cos = cos_ref[...]
    sin = sin_ref[...]
    for (col, width, gi, rope, scale, _, heads), o_ref in zip(specs, out_refs):
        for c0 in range(0, width, LANES):
            x = p_ref[:, col + c0:col + c0 + LANES]
            if gi is not None:
                ss = _dot(x * x, bd_ref[...], HIGHEST)
                x = x * lax.rsqrt(ss * (1.0 / HD) + EPS) * gain_refs[gi][...]
            if rope:
                x = x * cos + _rope_rotate(x) * sin
            if scale != 1.0:
                x = x * scale
            if heads is None:
                o_ref[:, c0:c0 + LANES] = x.astype(o_ref.dtype)
            else:
                hw = width // heads
                per = LANES // hw

                @pl.when(_is_ctx_tile())
                def _(x=x, o_ref=o_ref, c0=c0, hw=hw, per=per):
                    for u in range(per):
                        o_ref[(c0 // LANES) * per + u] = x[:, u * hw:(u + 1) * hw].astype(o_ref.dtype)


def _norm_proj(y, gain, scale, shift, w_bf16, rope_cos, rope_sin, gains, specs):
    n = w_bf16.shape[1]
    y_pair = isinstance(y, tuple)
    bd = jnp.asarray(np.kron(np.eye(LANES // HD), np.ones((HD, HD))), F32)
    vec = pl.BlockSpec((None, 1, D_MODEL), lambda i: (_mod_row(i), 0, 0))
    rope_spec = pl.BlockSpec((TM, LANES), lambda i: (_rope_block(i), 0))
    in_specs = _pair_specs(D_MODEL) if y_pair else [pl.BlockSpec((TM, D_MODEL), lambda i: (i, 0))]
    in_specs += [pl.BlockSpec((1, D_MODEL), lambda i: (0, 0)),
                 vec, vec, pl.BlockSpec((D_MODEL, n), lambda i: (0, 0)),
                 rope_spec, rope_spec, pl.BlockSpec((LANES, LANES), lambda i: (0, 0))]
    in_specs += [pl.BlockSpec((1, LANES), lambda i: (0, 0)) for _ in gains]
    out_shape = [jax.ShapeDtypeStruct((T_ALL, n), F32)]
    out_specs = [pl.BlockSpec((TM, n), lambda i: (i, 0))]
    for (_, width, _, _, _, dtype, heads) in specs:
        if heads is None:
            out_shape.append(jax.ShapeDtypeStruct((T_ALL, width), dtype))
            out_specs.append(pl.BlockSpec((TM, width), lambda i: (i, 0)))
        else:
            out_shape.append(jax.ShapeDtypeStruct((BATCH, heads, SEQ, width // heads), dtype))
            out_specs.append(pl.BlockSpec((None, heads, SEQ, width // heads),
                                          lambda i: (jnp.minimum(i, CTX_TILES - 1), 0, 0, 0)))
    ys = list(y) if y_pair else [y]
    return pl.pallas_call(
        functools.partial(_norm_proj_kernel, specs=specs, n_gain=len(gains), y_pair=y_pair),
        out_shape=out_shape,
        grid=(N_TILES,),
        in_specs=in_specs,
        out_specs=out_specs,
        compiler_params=_params(("arbitrary",)),
        name="norm_proj",
    )(*ys, gain.reshape(1, D_MODEL), scale, shift, w_bf16, rope_cos, rope_sin, bd, *gains)


def _rope_tables():
    half = HD // 2
    freqs = 1.0 / (ROPE_THETA ** (jnp.arange(0, half, 2, dtype=F32) / half))
    t = jnp.arange(DEC_SEQ)
    rows = (t // GRID_W).astype(F32)
    cols = (t % GRID_W).astype(F32)
    ang = jnp.concatenate([rows[:, None] * freqs, cols[:, None] * freqs], axis=-1)
    cos = jnp.repeat(jnp.cos(ang), 2, axis=-1)
    sin = jnp.repeat(jnp.sin(ang), 2, axis=-1)
    cos = jnp.concatenate([jnp.tile(cos, (1, LANES // HD)), jnp.ones((TM, LANES), F32)], axis=0)
    sin = jnp.concatenate([jnp.tile(sin, (1, LANES // HD)), jnp.zeros((TM, LANES), F32)], axis=0)
    return cos, sin


def _lane_slice(ref, h, width=HD):
    per = LANES // width
    blk = ref[:, (h // per) * LANES:(h // per + 1) * LANES]
    if per == 1:
        return blk
    return blk[:, (h % per) * width:(h % per + 1) * width]


def _softmax_parts(scores):
    m = None
    for s in scores:
        ms = jnp.max(s, axis=-1, keepdims=True)
        m = ms if m is None else jnp.maximum(m, ms)
    ps = [jnp.exp(s - m) for s in scores]
    l = None
    for p in ps:
        ls = jnp.sum(p, axis=-1, keepdims=True)
        l = ls if l is None else l + ls
    return ps, l


def _attn_std_kernel(*refs, group, n_kv, has_cache, has_bias, bq):
    it = iter(refs)
    q_ref, kn_ref, vn_ref = next(it), next(it), next(it)
    kc_ref = vc_ref = b_ref = None
    if has_cache:
        kc_ref, vc_ref = next(it), next(it)
    if has_bias:
        b_ref = next(it)
    o_ref = next(it)
    outs = []
    for g in range(n_kv):
        qs = jnp.concatenate([_lane_slice(q_ref, g * group + j) for j in range(group)], axis=0)
        kn = _lane_slice(kn_ref, g)
        vn = _lane_slice(vn_ref, g)
        s_new = _dot_nt(qs, kn)
        if has_bias:
            s_new = s_new + b_ref[g]
        scores = [s_new]
        if has_cache:
            scores.append(_dot_nt(qs, kc_ref[g]))
        ps, l = _softmax_parts(scores)
        o = _dot(ps[0].astype(BF16), vn)
        if has_cache:
            o = o + _dot(ps[1].astype(BF16), vc_ref[g])
        o = o / l
        for j in range(group):
            outs.append(o[j * bq:(j + 1) * bq])
    o_ref[...] = jnp.concatenate(outs, axis=1).astype(o_ref.dtype)


def _attn_diff_kernel(*refs, has_cache):
    it = iter(refs)
    lam_ref, q_ref, kn_ref, vn_ref = next(it), next(it), next(it), next(it)
    kc_ref = vc_ref = None
    if has_cache:
        kc_ref, vc_ref = next(it), next(it)
    g_ref, o_ref = next(it), next(it)
    lam = lam_ref[0]
    post = lam_ref[1]
    outs = []
    for h in range(D_HEADS):
        pd_new, pd_c = None, None
        for j in range(2):
            f = 2 * h + j
            qs = _lane_slice(q_ref, f)
            scores = [_dot_nt(qs, _lane_slice(kn_ref, f))]
            if has_cache:
                scores.append(_dot_nt(qs, kc_ref[f]))
            ps, l = _softmax_parts(scores)
            r = 1.0 / l
            if j == 0:
                pd_new = ps[0] * r
                pd_c = ps[1] * r if has_cache else None
            else:
                r = r * lam
                pd_new = pd_new - ps[0] * r
                pd_c = pd_c - ps[1] * r if has_cache else None
        o = _dot(pd_new.astype(BF16), _lane_slice(vn_ref, h, D_VDIM))
        if has_cache:
            o = o + _dot(pd_c.astype(BF16), vc_ref[h])
        ms = jnp.mean(o * o, axis=-1, keepdims=True)
        outs.append(o * lax.rsqrt(ms + EPS) * g_ref[...] * post)
    o_ref[...] = jnp.concatenate(outs, axis=1).astype(o_ref.dtype)


def _attention(q, kn, vn, *, ctx, group=1, n_kv=1, cache=None, bias=None, diff=None, bq=256):
    if ctx:
        nb, sq, row0 = BATCH, SEQ, 0
    else:
        nb, sq, row0 = DEC_BATCH, DEC_SEQ, T_CTX
    nq = sq // bq
    qb0 = row0 // bq
    kb0 = row0 // sq
    wq, wk, wv = q.shape[1], kn.shape[1], vn.shape[1]
    in_specs = [
        pl.BlockSpec((bq, wq), lambda b, i: (qb0 + b * nq + i, 0)),
        pl.BlockSpec((sq, wk), lambda b, i: (kb0 + b, 0)),
        pl.BlockSpec((sq, wv), lambda b, i: (kb0 + b, 0)),
    ]
    args = [q, kn, vn]
    if cache is not None:
        kc, vc = cache
        in_specs += [pl.BlockSpec((None,) + kc.shape[1:], lambda b, i: (b, 0, 0, 0)),
                     pl.BlockSpec((None,) + vc.shape[1:], lambda b, i: (b, 0, 0, 0))]
        args += [kc, vc]
    if diff is None:
        if bias is not None:
            in_specs.append(pl.BlockSpec((bias.shape[0], bq, sq), lambda b, i: (0, i, 0)))
            args.append(bias)
        body = functools.partial(_attn_std_kernel, group=group, n_kv=n_kv, has_cache=cache is not None,
                                 has_bias=bias is not None, bq=bq)
    else:
        lam_vec, gain = diff
        in_specs = [pl.BlockSpec(memory_space=pltpu.SMEM)] + in_specs
        args = [lam_vec] + args
        in_specs.append(pl.BlockSpec((1, D_VDIM), lambda b, i: (0, 0)))
        args.append(gain.reshape(1, D_VDIM))
        body = functools.partial(_attn_diff_kernel, has_cache=cache is not None)
    return pl.pallas_call(
        body,
        out_shape=jax.ShapeDtypeStruct((nb * sq, 512), BF16),
        grid=(nb, nq),
        in_specs=in_specs,
        out_specs=pl.BlockSpec((bq, 512), lambda b, i: (b * nq + i, 0)),
        compiler_params=_params(("parallel", "parallel")),
        name="attention",
    )(*args)


GRID_ROWS = DEC_SEQ // GRID_W
NA_WIN_ROWS = min(NA_ROWS, GRID_ROWS)


def _na_bias_kernel(t_ref, o_ref):
    qr = pl.program_id(1)
    r0 = jnp.clip(qr - NA_WIN_ROWS // 2, 0, GRID_ROWS - NA_WIN_ROWS)
    parts = []
    for kr in range(GRID_ROWS):
        dr = jnp.clip(kr - qr + (NA_ROWS - 1), 0, 2 * NA_ROWS - 2)
        ok = (kr >= r0) & (kr < r0 + NA_WIN_ROWS)
        parts.append(jnp.where(ok, t_ref[dr], NEG_BIG))
    o_ref[...] = jnp.concatenate(parts, axis=1)


def _neighbourhood_bias(rpb):
    c = np.arange(GRID_W)
    c0 = np.clip(c - NA_COLS // 2, 0, GRID_W - NA_COLS)
    col_ok = (c[None, :] >= c0[:, None]) & (c[None, :] < c0[:, None] + NA_COLS)
    dc = np.clip(c[None, :] - c[:, None], 1 - NA_COLS, NA_COLS - 1) + (NA_COLS - 1)
    oh_c = jnp.asarray(dc[..., None] == np.arange(2 * NA_COLS - 1), F32)
    by_col = jnp.einsum("hrd,qkd->hrqk", rpb.astype(F32), oh_c, precision=HIGHEST)
    by_col = jnp.where(jnp.asarray(col_ok), by_col, NEG_BIG)
    n_dr = 2 * NA_ROWS - 1
    return pl.pallas_call(
        _na_bias_kernel,
        out_shape=jax.ShapeDtypeStruct((C_HEADS, DEC_SEQ, DEC_SEQ), F32),
        grid=(C_HEADS, GRID_ROWS),
        in_specs=[pl.BlockSpec((None, n_dr, GRID_W, GRID_W), lambda h, r: (h, 0, 0, 0))],
        out_specs=pl.BlockSpec((None, GRID_W, DEC_SEQ), lambda h, r: (h, r, 0)),
        compiler_params=_params(("parallel", "parallel")),
        name="na_bias",
    )(by_col)


def _log_sigmoid(x):
    return jnp.minimum(x, 0.0) - jnp.log1p(jnp.exp(-jnp.abs(x)))


def _mlstm_kernel(q_ref, k_ref, v_ref, g_ref, gb_ref, c0_ref, n0_ref, m0_ref, h_ref, c_ref, n_ref, m_ref, hb_ref,
                  *, seq):
    L = B_CHUNK
    nc = seq // L
    row = lax.broadcasted_iota(jnp.int32, (L, L), 0)
    col = lax.broadcasted_iota(jnp.int32, (L, L), 1)
    keeps = (col <= row, col >= row)
    k_scale = B_DK ** -0.5
    c_ref[...] = c0_ref[...]
    n_ref[...] = n0_ref[...]
    m_ref[...] = m0_ref[...]

    def step(j):
        for d in range(2):
            keep = keeps[d]
            c = j if d == 0 else nc - 1 - j
            off = c * L if isinstance(c, int) else pl.multiple_of(c * L, L)
            gates = g_ref[pl.ds(off, L), :] + gb_ref[...]
            cum = _dot(keep.astype(F32), _log_sigmoid(gates), HIGHEST)
            cum_t = cum.T
            gates_t = gates.T
            out_ref = h_ref if d == 0 else hb_ref
            for h in range(B_HEADS):
                ci = (2 * d) * B_HEADS + h
                cf = (2 * d + 1) * B_HEADS + h
                hs = slice(h * B_DK, (h + 1) * B_DK)
                C = c_ref[d, h]
                n = n_ref[d, h]
                m = m_ref[d, h][:, 0:1]
                qc = q_ref[pl.ds(off, L), hs]
                kc = k_ref[pl.ds(off, L), hs] * k_scale
                vc = v_ref[pl.ds(off, L), hs]
                b_col = cum[:, cf:cf + 1]
                i_col = gates[:, ci:ci + 1]
                b_row = cum_t[cf:cf + 1, :]
                i_row = gates_t[ci:ci + 1, :]
                dlog = jnp.where(keep, b_col - b_row + i_row, -jnp.inf)
                inter = b_col + m
                m_t = jnp.maximum(inter, jnp.max(dlog, axis=-1, keepdims=True))
                w_intra = jnp.exp(dlog - m_t)
                w_inter = jnp.exp(inter - m_t)
                qb = qc.astype(BF16)
                vb = vc.astype(BF16)
                qk = _dot_nt(qb, kc.astype(BF16)) * w_intra
                num = _dot(qk.astype(BF16), vb) + w_inter * _dot(qb, C.astype(BF16))
                den = jnp.sum(qk, axis=-1, keepdims=True) + w_inter * jnp.sum(qc * n, axis=-1, keepdims=True)
                out_ref[pl.ds(off, L), hs] = num / jnp.maximum(jnp.abs(den), jnp.exp(-m_t))
                b_last = b_col[L - 1:L, :] if d == 0 else b_col[0:1, :]
                end_col = b_last - b_col + i_col
                m_new = jnp.maximum(b_last + m, jnp.max(end_col, axis=0, keepdims=True))
                w_end = jnp.exp(end_col - m_new)
                decay = jnp.exp(b_last + m - m_new)
                kw = kc * w_end
                c_ref[d, h] = decay * C + _dot_tn(kw.astype(BF16), vb)
                n_ref[d, h] = decay * n + jnp.sum(kw, axis=0, keepdims=True)
                m_ref[d, h] = jnp.broadcast_to(m_new, (1, LANES))

    if nc <= 2:
        for j in range(nc):
            step(j)
    else:
        def body(j, carry):
            step(j)
            return carry

        lax.fori_loop(0, nc, body, 0)
    h_ref[...] = h_ref[...] + hb_ref[...]


def _mlstm(p, gate_bias, c0, n0, m0, *, ctx):
    if ctx:
        nb, seq, blk0 = BATCH, SEQ, 0
    else:
        nb, seq, blk0 = DEC_BATCH, DEC_SEQ, T_CTX // DEC_SEQ
    w = B_HEADS * B_DK

    def cols(c0_, width):
        return pl.BlockSpec((seq, width), lambda b: (blk0 + b, c0_ // width))

    gb = jnp.zeros((1, LANES), F32).at[0, :4 * B_HEADS].set(gate_bias.reshape(-1).astype(F32))
    st = lambda shape: pl.BlockSpec((None,) + shape, lambda b: (b, 0, 0, 0, 0))
    return pl.pallas_call(
        functools.partial(_mlstm_kernel, seq=seq),
        out_shape=[
            jax.ShapeDtypeStruct((nb * seq, w), F32),
            jax.ShapeDtypeStruct((nb, 2, B_HEADS, B_DK, B_DV), F32),
            jax.ShapeDtypeStruct((nb, 2, B_HEADS, 1, B_DK), F32),
            jax.ShapeDtypeStruct((nb, 2, B_HEADS, 1, LANES), F32),
        ],
        grid=(nb,),
        in_specs=[
            cols(EV_BQ, w), cols(EV_BK, w), cols(EV_BV, w), cols(EV_BG, LANES),
            pl.BlockSpec((1, LANES), lambda b: (0, 0)),
            st((2, B_HEADS, B_DK, B_DV)), st((2, B_HEADS, 1, B_DK)), st((2, B_HEADS, 1, LANES)),
        ],
        out_specs=[
            pl.BlockSpec((seq, w), lambda b: (b, 0)),
            st((2, B_HEADS, B_DK, B_DV)), st((2, B_HEADS, 1, B_DK)), st((2, B_HEADS, 1, LANES)),
        ],
        scratch_shapes=[pltpu.VMEM((seq, w), F32)],
        compiler_params=_params(("parallel",)),
        name="mlstm",
    )(p, p, p, p, gb, c0, n0, m0)


def _merge_value(refs, even, y_pair):
    a_ctx, a_lat, b_ctx, b_lat = refs[:4]
    rest = refs[4:]
    if even:
        bo_ref, ng_ref = rest[:2]
        rest = rest[2:]
        parts = [_pair_value(a_ctx, a_lat)]
        for h in range(B_HEADS):
            hs = slice(h * B_DV, (h + 1) * B_DV)
            x = _pair_value(b_ctx, b_lat, hs)
            ms = jnp.mean(x * x, axis=-1, keepdims=True)
            xn = x * lax.rsqrt(ms + EPS) * ng_ref[:, hs]
            parts.append((jax.nn.sigmoid(bo_ref[:, hs]) * xn).astype(BF16))
    else:
        parts = [_pair_value(a_ctx, a_lat), _pair_value(b_ctx, b_lat)]
    w_ref = rest[0]
    y = _pair_value(rest[1], rest[2]) if y_pair else rest[1][...]
    g_ref = rest[-1]
    cat = jnp.concatenate(parts, axis=1)
    return y + g_ref[...] * _dot(cat, w_ref[...])


def _merge_specs(y, gate, w_bf16, a, b, p, norm_gain):
    in_specs = _pair_specs(512) + _pair_specs(512)
    args = [*a, *b]
    if p is not None:
        in_specs += [pl.BlockSpec((TM, 512), lambda i: (i, EV_BO // 512)), pl.BlockSpec((1, 512), lambda i: (0, 0))]
        args += [p, norm_gain.reshape(1, 512)]
    in_specs.append(pl.BlockSpec((D_MODEL, D_MODEL), lambda i: (0, 0)))
    args.append(w_bf16)
    if isinstance(y, tuple):
        in_specs += _pair_specs(D_MODEL)
        args += list(y)
    else:
        in_specs.append(pl.BlockSpec((TM, D_MODEL), lambda i: (i, 0)))
        args.append(y)
    in_specs.append(pl.BlockSpec((None, 1, D_MODEL), lambda i: (_mod_row(i), 0, 0)))
    args.append(gate)
    return in_specs, args


SLAB = D_MODEL // LANES


def _load_slabs(ref, rows):
    return jnp.concatenate([ref[pl.ds(c, rows, stride=SLAB), :] for c in range(SLAB)], axis=1)


def _store_slabs(ref, x):
    for c in range(SLAB):
        ref[pl.ds(c, x.shape[0], stride=SLAB), :] = x[:, c * LANES:(c + 1) * LANES]


def _slab(ref, idx):
    return ref.at[pl.ds(pl.multiple_of(idx * SLAB, SLAB), SLAB)]


def _merge_router_kernel(*refs, even, y_pair):
    n_merge = 4 + (2 if even else 0) + 1 + (2 if y_pair else 1) + 1
    merge_refs = refs[:n_merge]
    (g_ref, sc_ref, sh_ref, whi_ref, wlo_ref, b_ref,
     y_out_ref, h_ref, ti_ref, tp_ref, rk_ref, cnt_ref, base_ref) = refs[n_merge:]

    @pl.when(pl.program_id(0) == 0)
    def _():
        base_ref[...] = jnp.zeros(base_ref.shape, F32)

    y = _merge_value(merge_refs, even, y_pair)
    y_out_ref[...] = y
    h = _norm_mod(y, g_ref[...], sc_ref[...], sh_ref[...])
    _store_slabs(h_ref, h)
    h_hi = h.astype(BF16)
    h_lo = (h - h_hi.astype(F32)).astype(BF16)
    logits = (_dot(h_hi, whi_ref[...]) + (_dot(h_hi, wlo_ref[...]) + _dot(h_lo, whi_ref[...]))
              + b_ref[...])
    lane = lax.broadcasted_iota(jnp.int32, logits.shape, 1)
    lane_f = lane.astype(F32)
    vals, idxs = [], []
    for _ in range(TOP_K):
        mx = jnp.max(logits, axis=-1, keepdims=True)
        ix = jnp.min(jnp.where(logits == mx, lane_f, float(LANES)), axis=-1, keepdims=True)
        vals.append(mx)
        idxs.append(ix)
        logits = jnp.where(lane_f == ix, -jnp.inf, logits)
    es = [jnp.exp(v - vals[0]) for v in vals]
    tot = es[0] + es[1] + es[2] + es[3]
    ti = jnp.zeros(logits.shape, F32)
    tp = jnp.zeros(logits.shape, F32)
    for k in range(TOP_K):
        ti = jnp.where(lane == k, idxs[k], ti)
        tp = jnp.where(lane == k, es[k] / tot, tp)
    ti_ref[...] = ti.T[0:8, :].astype(jnp.int32)
    tp_ref[...] = tp
    onehots = [(lane_f == ix).astype(F32) for ix in idxs]
    cnt = onehots[0] + onehots[1] + onehots[2] + onehots[3]
    row = lax.broadcasted_iota(jnp.int32, (TM, TM), 0)
    col = lax.broadcasted_iota(jnp.int32, (TM, TM), 1)
    before = _dot((col < row).astype(BF16), cnt.astype(BF16)) + base_ref[...]
    rk = jnp.zeros(logits.shape, F32)
    for k in range(TOP_K):
        rk = jnp.where(lane == k, jnp.sum(onehots[k] * before, axis=-1, keepdims=True), rk)
    rk_ref[...] = rk.T[0:8, :].astype(jnp.int32)
    base_ref[...] = base_ref[...] + jnp.sum(cnt, axis=0, keepdims=True)
    cnt_ref[...] = base_ref[...]


def _merge_router(y, gate, w_out_bf16, a, b, gain, scale, shift, rw, rb, *, p=None, norm_gain=None):
    merge_in_specs, merge_args = _merge_specs(y, gate, w_out_bf16, a, b, p, norm_gain)
    vec = pl.BlockSpec((None, 1, D_MODEL), lambda i: (_mod_row(i), 0, 0))
    rw_p = jnp.zeros((D_MODEL, LANES), F32).at[:, :N_EXPERTS].set(rw)
    rb_p = jnp.full((1, LANES), NEG_BIG, F32).at[0, :N_EXPERTS].set(rb)
    rw_hi = rw_p.astype(BF16)
    tile = lambda w: pl.BlockSpec((TM, w), lambda i: (i, 0))
    by_choice = pl.BlockSpec((None, 8, TM), lambda i: (i, 0, 0))
    return pl.pallas_call(
        functools.partial(_merge_router_kernel, even=p is not None, y_pair=isinstance(y, tuple)),
        out_shape=[jax.ShapeDtypeStruct((T_ALL, D_MODEL), F32),
                   jax.ShapeDtypeStruct((T_ALL * SLAB, LANES), F32),
                   jax.ShapeDtypeStruct((N_TILES, 8, TM), jnp.int32),
                   jax.ShapeDtypeStruct((T_ALL, LANES), F32),
                   jax.ShapeDtypeStruct((N_TILES, 8, TM), jnp.int32),
                   jax.ShapeDtypeStruct((1, LANES), F32)],
        grid=(N_TILES,),
        in_specs=merge_in_specs + [
            pl.BlockSpec((1, D_MODEL), lambda i: (0, 0)), vec, vec,
            pl.BlockSpec((D_MODEL, LANES), lambda i: (0, 0)), pl.BlockSpec((D_MODEL, LANES), lambda i: (0, 0)),
            pl.BlockSpec((1, LANES), lambda i: (0, 0))],
        out_specs=[tile(D_MODEL), pl.BlockSpec((TM * SLAB, LANES), lambda i: (i, 0)), by_choice, tile(LANES),
                   by_choice, pl.BlockSpec((1, LANES), lambda i: (0, 0))],
        scratch_shapes=[pltpu.VMEM((1, LANES), F32)],
        compiler_params=_params(("arbitrary",)),
        name="merge_router",
    )(*merge_args, gain.reshape(1, D_MODEL), scale, shift, rw_hi, (rw_p - rw_hi.astype(F32)).astype(BF16), rb_p)


def _route_plan(top_i, rank, counts):
    experts = jnp.arange(N_EXPERTS, dtype=jnp.int32)
    padded = ((counts + MOE_TM - 1) // MOE_TM) * MOE_TM
    seg_end = jnp.cumsum(padded)
    seg_start = seg_end - padded
    pos = rank
    for e in range(N_EXPERTS - 1):
        pos = pos + jnp.where(top_i > e, padded[e], 0)
    n_active = seg_end[-1] // MOE_TM
    fill = jnp.concatenate([seg_start + counts, padded - counts, n_active[None]]).astype(jnp.int32)
    tile_start = jnp.arange(MOE_TILES, dtype=jnp.int32) * MOE_TM
    tile_expert = jnp.sum((seg_end[None, :] <= tile_start[:, None]).astype(jnp.int32), axis=1)
    last = jnp.sum((seg_end <= (n_active - 1) * MOE_TM).astype(jnp.int32))
    tile_expert = jnp.minimum(jnp.where(tile_start < seg_end[-1], tile_expert, last), N_EXPERTS - 1)
    owns = (padded > 0).astype(jnp.int32)
    run_of_expert = jnp.cumsum(owns) - 1
    run_expert = jnp.sum(jnp.where((run_of_expert[None, :] == experts[:, None]) & (owns[None, :] > 0),
                                   experts[None, :], 0), axis=1)
    runs = jnp.concatenate([run_expert, jnp.sum(owns)[None]]).astype(jnp.int32)
    tile_run = jnp.sum(jnp.where(tile_expert[:, None] == experts[None, :], run_of_expert[None, :], 0), axis=1)
    experts_plan = (tile_expert.astype(jnp.int32), n_active.reshape(1).astype(jnp.int32),
                    tile_run.astype(jnp.int32), runs)
    return pos.astype(jnp.int32), fill, experts_plan


DMA_UNROLL = 4
DMA_QUEUES = 2


def _wait_slabs(ref, n_slabs, sem):
    view = ref.at[pl.ds(0, n_slabs * SLAB)]
    pltpu.make_async_copy(view, view, sem).wait()


def _dispatch_kernel(pos_ref, fill_ref, h_ref, xs_ref, sem):
    i = pl.program_id(0)
    base = i * TM

    def issue(j, carry):
        for u in range(DMA_UNROLL):
            t = j * DMA_UNROLL + u
            for k in range(TOP_K):
                dst = _slab(xs_ref, pos_ref[base * TOP_K + k * TM + t])
                pltpu.make_async_copy(_slab(h_ref, t), dst, sem).start(priority=k % DMA_QUEUES)
        return carry

    lax.fori_loop(0, TM // DMA_UNROLL, issue, 0)

    @pl.when(i == 0)
    def _():
        def per_expert(e, total):
            start = fill_ref[e]
            n = fill_ref[N_EXPERTS + e]

            def one(r, carry):
                pltpu.make_async_copy(_slab(h_ref, 0), _slab(xs_ref, start + r), sem).start()
                return carry

            lax.fori_loop(0, n, one, 0)
            return total + n

        total = lax.fori_loop(0, N_EXPERTS, per_expert, 0)

        n_active = fill_ref[2 * N_EXPERTS]

        def unused_tile(ti, carry):
            dst = xs_ref.at[pl.ds(pl.multiple_of(ti * (MOE_TM * SLAB), MOE_TM * SLAB), MOE_TM * SLAB)]
            pltpu.make_async_copy(h_ref, dst, sem).start()
            return carry

        lax.fori_loop(n_active, MOE_TILES, unused_tile, 0)
        total = total + (MOE_TILES - n_active) * MOE_TM

        @pl.when(total > 0)
        def _():
            _wait_slabs(xs_ref, total, sem)

    _wait_slabs(xs_ref, TM * TOP_K, sem)


def _dispatch(h_slabs, pos, fill):
    grid_spec = pltpu.PrefetchScalarGridSpec(
        num_scalar_prefetch=2,
        grid=(N_TILES,),
        in_specs=[pl.BlockSpec((TM * SLAB, LANES), lambda i, pos, fill: (i, 0))],
        out_specs=pl.BlockSpec(memory_space=pl.ANY),
        scratch_shapes=[pltpu.SemaphoreType.DMA],
    )
    return pl.pallas_call(
        _dispatch_kernel,
        out_shape=jax.ShapeDtypeStruct((MOE_ROWS * SLAB, LANES), F32),
        grid_spec=grid_spec,
        compiler_params=_params(("arbitrary",)),
        name="moe_dispatch",
    )(pos, fill, h_slabs)


def _moe_kernel(te_ref, na_ref, ts_ref, ex_ref, x_ref, wgu_hbm, bgu_ref, wd_hbm, bd_ref, o_ref,
                wgu_f32, wd_f32, wgu_bf, wd_bf, sems, *, layer):
    i = pl.program_id(0)
    s = ts_ref[i]
    first = (i == 0) | (s != ts_ref[jnp.maximum(i - 1, 0)])

    def weight_copies(slot):
        e = ex_ref[slot]
        b = slot % 2
        return (pltpu.make_async_copy(wgu_hbm.at[layer, e], wgu_f32.at[b], sems.at[0, b]),
                pltpu.make_async_copy(wd_hbm.at[layer, e], wd_f32.at[b], sems.at[1, b]))

    @pl.when(i == 0)
    def _():
        for cp in weight_copies(0):
            cp.start()

    @pl.when(first)
    def _():
        for cp in weight_copies(s):
            cp.wait()

        @pl.when(s + 1 < ex_ref[N_EXPERTS])
        def _():
            for cp in weight_copies(s + 1):
                cp.start()

        b = s % 2
        wgu_bf[...] = wgu_f32[b].astype(BF16)
        wd_bf[...] = wd_f32[b].astype(BF16)

    @pl.when(i < na_ref[0])
    def _():
        x = _load_slabs(x_ref, MOE_TM).astype(BF16)
        gu = _dot(x, wgu_bf[...]) + bgu_ref[...]
        gate = jnp.minimum(gu[:, :D_FF], SWIGLU_LIMIT)
        up = jnp.clip(gu[:, D_FF:], -SWIGLU_LIMIT, SWIGLU_LIMIT)
        act = (up + 1.0) * gate * jax.nn.sigmoid(SWIGLU_ALPHA * gate)
        _store_slabs(o_ref, _dot(act.astype(BF16), wd_bf[...]) + bd_ref[...])

    @pl.when(i >= na_ref[0])
    def _():
        o_ref[...] = jnp.zeros(o_ref.shape, o_ref.dtype)


def _moe_experts(layer, xs, plan, w_gu, b_gu, w_down, b_down):
    grid_spec = pltpu.PrefetchScalarGridSpec(
        num_scalar_prefetch=4,
        grid=(MOE_TILES,),
        in_specs=[
            pl.BlockSpec((MOE_TM * SLAB, LANES), lambda i, te, na, ts, ex: (jnp.minimum(i, na[0] - 1), 0)),
            pl.BlockSpec(memory_space=pl.ANY),
            pl.BlockSpec((None, None, 1, 2 * D_FF), lambda i, te, na, ts, ex: (layer, te[i], 0, 0)),
            pl.BlockSpec(memory_space=pl.ANY),
            pl.BlockSpec((None, None, 1, D_MODEL), lambda i, te, na, ts, ex: (layer, te[i], 0, 0)),
        ],
        out_specs=pl.BlockSpec((MOE_TM * SLAB, LANES), lambda i, te, na, ts, ex: (i, 0)),
        scratch_shapes=[pltpu.VMEM((2, D_MODEL, 2 * D_FF), F32), pltpu.VMEM((2, D_FF, D_MODEL), F32),
                        pltpu.VMEM((D_MODEL, 2 * D_FF), BF16), pltpu.VMEM((D_FF, D_MODEL), BF16),
                        pltpu.SemaphoreType.DMA((2, 2))],
    )
    return pl.pallas_call(
        functools.partial(_moe_kernel, layer=layer),
        out_shape=jax.ShapeDtypeStruct((MOE_ROWS * SLAB, LANES), F32),
        grid_spec=grid_spec,
        compiler_params=_params(("arbitrary",)),
        name="moe_experts",
    )(*plan, xs, w_gu, b_gu.reshape(DEPTH, N_EXPERTS, 1, 2 * D_FF), w_down,
      b_down.reshape(DEPTH, N_EXPERTS, 1, D_MODEL))


def _combine_kernel(pos_ref, out_ref, y_ref, tp_ref, g_ref, *rest, final):
    if final:
        fg_ref, n_ctx_ref, n_lat_ref, buf, sem = rest
    else:
        o_ref, buf, sem = rest
    base = pl.program_id(0) * TM

    def issue(j, carry):
        for u in range(DMA_UNROLL):
            t = j * DMA_UNROLL + u
            for k in range(TOP_K):
                src = _slab(out_ref, pos_ref[base * TOP_K + k * TM + t])
                pltpu.make_async_copy(src, _slab(buf.at[k], t), sem).start(priority=k % DMA_QUEUES)
        return carry

    lax.fori_loop(0, TM // DMA_UNROLL, issue, 0)
    _wait_slabs(out_ref, TM * TOP_K, sem)

    tp = tp_ref[...]
    ss = jnp.zeros((TM, 1), F32)
    chunks = []
    for c in range(SLAB):
        cs = slice(c * LANES, (c + 1) * LANES)
        acc = tp[:, 0:1] * buf[0, pl.ds(c, TM, stride=SLAB), :]
        for k in range(1, TOP_K):
            acc = acc + tp[:, k:k + 1] * buf[k, pl.ds(c, TM, stride=SLAB), :]
        yc = y_ref[:, cs] + g_ref[:, cs] * acc
        if final:
            ss = ss + jnp.sum(yc * yc, axis=-1, keepdims=True)
            chunks.append(yc)
        else:
            o_ref[:, cs] = yc
    if final:
        inv = lax.rsqrt(ss * (1.0 / D_MODEL) + EPS)
        normed = jnp.concatenate([chunks[c] * inv * fg_ref[:, c * LANES:(c + 1) * LANES] for c in range(SLAB)],
                                 axis=1)

        @pl.when(_is_ctx_tile())
        def _():
            n_ctx_ref[...] = normed

        @pl.when(jnp.logical_not(_is_ctx_tile()))
        def _():
            n_lat_ref[...] = normed


def _combine(y, out_slabs, pos, top_p, gate, final_gain=None):
    final = final_gain is not None
    tile = pl.BlockSpec((TM, D_MODEL), lambda i, pos: (i, 0))
    in_specs = [pl.BlockSpec(memory_space=pl.ANY), tile, pl.BlockSpec((TM, LANES), lambda i, pos: (i, 0)),
                pl.BlockSpec((None, 1, D_MODEL), lambda i, pos: (_mod_row(i), 0, 0))]
    args = [out_slabs, y, top_p, gate]
    if final:
        in_specs.append(pl.BlockSpec((1, D_MODEL), lambda i, pos: (0, 0)))
        args.append(final_gain.reshape(1, D_MODEL))
        out_shape = [jax.ShapeDtypeStruct((T_CTX, D_MODEL), F32), jax.ShapeDtypeStruct((T_LAT, D_MODEL), F32)]
        out_specs = [pl.BlockSpec((TM, D_MODEL), lambda i, pos: (jnp.minimum(i, CTX_TILES - 1), 0)),
                     pl.BlockSpec((TM, D_MODEL), lambda i, pos: (jnp.maximum(i - CTX_TILES, 0), 0))]
    else:
        out_shape = [jax.ShapeDtypeStruct((T_ALL, D_MODEL), F32)]
        out_specs = [tile]
    grid_spec = pltpu.PrefetchScalarGridSpec(
        num_scalar_prefetch=1,
        grid=(N_TILES,),
        in_specs=in_specs,
        out_specs=out_specs,
        scratch_shapes=[pltpu.VMEM((TOP_K, TM * SLAB, LANES), F32), pltpu.SemaphoreType.DMA],
    )
    return pl.pallas_call(
        functools.partial(_combine_kernel, final=final),
        out_shape=out_shape,
        grid_spec=grid_spec,
        compiler_params=_params(("arbitrary",)),
        name="moe_combine",
    )(pos, *args)


def _moe_layer(layer, y, routed, gate, w_gu, b_gu, w_down, b_down, final_gain=None):
    h_slabs, top_i, top_p, rank, counts = routed
    pos, fill, experts_plan = _route_plan(top_i[:, :TOP_K].reshape(-1), rank[:, :TOP_K].reshape(-1),
                                          counts[0, :N_EXPERTS].astype(jnp.int32))
    xs = _dispatch(h_slabs, pos, fill)
    out = _moe_experts(layer, xs, experts_plan, w_gu, b_gu, w_down, b_down)
    return _combine(y, out, pos, top_p, gate, final_gain)


def kernel(x_prompt, x_sample, c, cache_a_k, cache_a_v, state_b_C, state_b_n, state_b_m, cache_c_k, cache_c_v, cache_d_k, cache_d_v, c_ctx, w_mod, b_mod, norm1_g, norm2_g, w_in_even, w_out_even, a_q_gain, a_k_gain, b_gate_bias, b_norm_gain, w_in_odd, w_out_odd, c_rpb, d_lambda, d_norm_gain, router_w, router_b, expert_w_gu, expert_b_gu, expert_w_down, expert_b_down, final_norm_g):
    y = (x_prompt.reshape(T_CTX, D_MODEL), x_sample.reshape(T_LAT, D_MODEL))
    cond = jnp.zeros((MOD_ROWS, D_MODEL), F32).at[0].set(c_ctx).at[1:1 + DEC_BATCH].set(c)
    mod = _modulation(cond, w_mod, b_mod).reshape(DEPTH, MOD_ROWS, 6, 1, D_MODEL)
    rope_cos, rope_sin = _rope_tables()
    scale = HD ** -0.5
    outs = {}

    for layer in range(DEPTH):
        sh1, sc1, g1, sh2, sc2, g2 = (mod[layer, :, k] for k in range(6))
        j = layer // 2
        if layer % 2 == 0:
            w = w_in_even[j]
            sizes = np.cumsum([0, 512, 128, 128, 512, 512, 512, 512, 16])
            aq, ak, av, bq, bk, bv, bo, bg = (w[:, sizes[k]:sizes[k + 1]] for k in range(8))
            w_in = jnp.concatenate([aq, bo, bq, bk, bv, ak, av, bg, jnp.zeros((D_MODEL, EV_N - EV_BG - 16), F32)],
                                   axis=1).astype(BF16)
            qg = jnp.tile(a_q_gain[j], LANES // HD).reshape(1, LANES)
            kg = jnp.tile(a_k_gain[j], LANES // HD).reshape(1, LANES)
            specs = ((EV_AQ, 512, 0, True, scale, BF16, None), (EV_AK, 128, 1, True, 1.0, BF16, None),
                     (EV_AV, 128, None, False, 1.0, BF16, None),
                     (EV_AK, 128, 1, False, 1.0, F32, A_KV), (EV_AV, 128, None, False, 1.0, F32, A_KV))
            p, qa, ka, va, new_ak, new_av = _norm_proj(y, norm1_g[layer], sc1, sh1, w_in, rope_cos, rope_sin,
                                                       [qg, kg], specs)
            oa_ctx = _attention(qa, ka, va, ctx=True, group=A_HEADS // A_KV, n_kv=A_KV)
            cache = (cache_a_k[:, j].astype(BF16), cache_a_v[:, j].astype(BF16))
            oa_lat = _attention(qa, ka, va, ctx=False, group=A_HEADS // A_KV, n_kv=A_KV, cache=cache)
            zc = jnp.zeros((BATCH, 2, B_HEADS, B_DK, B_DV), F32)
            zn = jnp.zeros((BATCH, 2, B_HEADS, 1, B_DK), F32)
            zm = jnp.zeros((BATCH, 2, B_HEADS, 1, LANES), F32)
            hb_ctx, bC, bn, bm = _mlstm(p, b_gate_bias[j], zc, zn, zm, ctx=True)
            m0 = jnp.broadcast_to(state_b_m[:, j][..., None, None], (DEC_BATCH, 2, B_HEADS, 1, LANES))
            hb_lat, _, _, _ = _mlstm(p, b_gate_bias[j], state_b_C[:, j], state_b_n[:, j][:, :, :, None, :], m0,
                                     ctx=False)
            y, *routed = _merge_router(y, g1, w_out_even[j].astype(BF16), (oa_ctx, oa_lat), (hb_ctx, hb_lat),
                                       norm2_g[layer], sc2, sh2, router_w[layer], router_b[layer],
                                       p=p, norm_gain=b_norm_gain[j])
            outs.setdefault("a_k", []).append(new_ak)
            outs.setdefault("a_v", []).append(new_av)
            outs.setdefault("b_C", []).append(bC)
            outs.setdefault("b_n", []).append(bn[:, :, :, 0, :])
            outs.setdefault("b_m", []).append(bm[:, :, :, 0, 0])
        else:
            specs = ((0, 512, None, False, scale, BF16, None), (512, 512, None, False, 1.0, BF16, None),
                     (1024, 512, None, False, 1.0, BF16, None), (1536, 512, None, True, scale, BF16, None),
                     (2048, 512, None, True, 1.0, BF16, None), (2560, 512, None, False, 1.0, BF16, None),
                     (512, 512, None, False, 1.0, F32, C_HEADS), (1024, 512, None, False, 1.0, F32, C_HEADS),
                     (2048, 512, None, False, 1.0, F32, 2 * D_HEADS), (2560, 512, None, False, 1.0, F32, D_HEADS))
            p, qc, kc, vc, qd, kd, vd, new_ck, new_cv, new_dk, new_dv = _norm_proj(
                y, norm1_g[layer], sc1, sh1, w_in_odd[j].astype(BF16), rope_cos, rope_sin, [], specs)
            lam_init = 0.8 - 0.6 * math.exp(-0.3 * layer)
            lp = d_lambda[j].astype(F32)
            lam = jnp.exp(jnp.sum(lp[0] * lp[1])) - jnp.exp(jnp.sum(lp[2] * lp[3])) + lam_init
            lam_vec = jnp.stack([lam, jnp.asarray(1.0 - lam_init, F32)]).astype(F32)
            diff = (lam_vec, d_norm_gain[j])
            oc_ctx = _attention(qc, kc, vc, ctx=True, n_kv=C_HEADS)
            od_ctx = _attention(qd, kd, vd, ctx=True, diff=diff)
            bias = _neighbourhood_bias(c_rpb[j])
            oc_lat = _attention(qc, kc, vc, ctx=False, n_kv=C_HEADS, bias=bias,
                                cache=(cache_c_k[:, j].astype(BF16), cache_c_v[:, j].astype(BF16)))
            kd_cache = cache_d_k[:, j].reshape(DEC_BATCH, 2 * D_HEADS, PAST_LEN, HD).astype(BF16)
            od_lat = _attention(qd, kd, vd, ctx=False, diff=diff, cache=(kd_cache, cache_d_v[:, j].astype(BF16)))
            y, *routed = _merge_router(y, g1, w_out_odd[j].astype(BF16), (oc_ctx, oc_lat), (od_ctx, od_lat),
                                       norm2_g[layer], sc2, sh2, router_w[layer], router_b[layer])
            outs.setdefault("c_k", []).append(new_ck)
            outs.setdefault("c_v", []).append(new_cv)
            outs.setdefault("d_k", []).append(new_dk.reshape(BATCH, D_HEADS, 2, SEQ, HD))
            outs.setdefault("d_v", []).append(new_dv)
        res = _moe_layer(layer, y, routed, g2, expert_w_gu, expert_b_gu, expert_w_down, expert_b_down,
                         final_gain=final_norm_g if layer == DEPTH - 1 else None)
        y = res[0]
    y_prompt, y_sample = res
    stack = lambda k: jnp.stack(outs[k], axis=1)
    return (y_prompt.reshape(BATCH, SEQ, D_MODEL), y_sample.reshape(DEC_BATCH, DEC_SEQ, D_MODEL),
            stack("a_k"), stack("a_v"), stack("b_C"), stack("b_n"), stack("b_m"),
            stack("c_k"), stack("c_v"), stack("d_k"), stack("d_v"))
```

```python
import functools
import math

import numpy as np
import jax
import jax.numpy as jnp
from jax import lax
from jax.experimental import pallas as pl
from jax.experimental.pallas import tpu as pltpu

D_MODEL = 1024
BATCH = 32
SEQ = 256
DEPTH = 2
DEC_BATCH = 8
DEC_SEQ = 1024
PAST_LEN = 512
GRID_W = 64
HD = 64
A_HEADS = 8
A_KV = 2
B_HEADS = 4
B_DK = 128
B_DV = 128
B_CHUNK = 128
C_HEADS = 8
NA_ROWS = 8
NA_COLS = 16
D_HEADS = 4
D_VDIM = 2 * HD
N_EXPERTS = 32
TOP_K = 4
D_FF = 1024
SWIGLU_LIMIT = 7.0
SWIGLU_ALPHA = 1.702
ROPE_THETA = 10000.0
EPS = 1e-6

F32 = jnp.float32
BF16 = jnp.bfloat16
HIGHEST = lax.Precision.HIGHEST

T_CTX = BATCH * SEQ
T_LAT = DEC_BATCH * DEC_SEQ
T_ALL = T_CTX + T_LAT
TM = 256
CTX_TILES = T_CTX // TM
LAT_TILES_PER_BATCH = DEC_SEQ // TM
N_TILES = T_ALL // TM
MOD_ROWS = 16
LANES = 128
NEG_BIG = -1e30
MOE_TM = 256
N_ASSIGN = T_ALL * TOP_K
MOE_ROWS = N_ASSIGN + N_EXPERTS * MOE_TM
MOE_TILES = MOE_ROWS // MOE_TM
VMEM_LIMIT = 56 * 1024 * 1024

EV_AQ, EV_BO, EV_BQ, EV_BK, EV_BV, EV_AK, EV_AV, EV_BG = 0, 512, 1024, 1536, 2048, 2560, 2688, 2816
EV_N = 2944
OD_N = 3072


def _params(sem, vmem=VMEM_LIMIT):
    return pltpu.CompilerParams(dimension_semantics=sem, vmem_limit_bytes=vmem)


def _mod_row(i):
    return jnp.where(i < CTX_TILES, 0, 1 + (i - CTX_TILES) // LAT_TILES_PER_BATCH)


def _rope_block(i):
    return jnp.where(i < CTX_TILES, LAT_TILES_PER_BATCH, (i - CTX_TILES) % LAT_TILES_PER_BATCH)


def _dot(a, b, precision=None):
    return jnp.dot(a, b, preferred_element_type=F32, precision=precision)


def _dot_nt(a, b):
    return lax.dot_general(a, b, (((1,), (1,)), ((), ())), preferred_element_type=F32)


def _dot_tn(a, b):
    return lax.dot_general(a, b, (((0,), (0,)), ((), ())), preferred_element_type=F32)


def _modulation_kernel(c_ref, w_ref, b_ref, o_ref):
    c = c_ref[...]
    s = c * jax.nn.sigmoid(c)
    o_ref[...] = _dot(s, w_ref[...], HIGHEST) + b_ref[...]


def _modulation(cond, w_mod, b_mod):
    tn = 1536
    return pl.pallas_call(
        _modulation_kernel,
        out_shape=jax.ShapeDtypeStruct((DEPTH, MOD_ROWS, 6 * D_MODEL), F32),
        grid=(DEPTH, 6 * D_MODEL // tn),
        in_specs=[
            pl.BlockSpec((MOD_ROWS, D_MODEL), lambda l, j: (0, 0)),
            pl.BlockSpec((None, D_MODEL, tn), lambda l, j: (l, 0, j)),
            pl.BlockSpec((None, 1, tn), lambda l, j: (l, 0, j)),
        ],
        out_specs=pl.BlockSpec((None, MOD_ROWS, tn), lambda l, j: (l, 0, j)),
        compiler_params=_params(("parallel", "parallel")),
        name="modulation",
    )(cond, w_mod, b_mod.reshape(DEPTH, 1, 6 * D_MODEL))


def _norm_mod(y, g, sc, sh):
    ms = jnp.mean(y * y, axis=-1, keepdims=True)
    return (y * lax.rsqrt(ms + EPS) * g) * (1.0 + sc) + sh


def _rope_rotate(x):
    w = x.shape[-1]
    lane = lax.broadcasted_iota(jnp.int32, x.shape, 1)
    nxt = pltpu.roll(x, w - 1, 1)
    prv = pltpu.roll(x, 1, 1)
    return jnp.where((lane & 1) == 0, -nxt, prv)


def _is_ctx_tile():
    return pl.program_id(0) < CTX_TILES


def _pair_specs(width):
    return [pl.BlockSpec((TM, width), lambda i: (jnp.minimum(i, CTX_TILES - 1), 0)),
            pl.BlockSpec((TM, width), lambda i: (jnp.maximum(i - CTX_TILES, 0), 0))]


def _pair_value(ctx_ref, lat_ref, cols=slice(None)):
    return jnp.where(_is_ctx_tile(), ctx_ref[:, cols], lat_ref[:, cols])


def _norm_proj_kernel(*refs, specs, n_gain, y_pair):
    n_y = 2 if y_pair else 1
    y = _pair_value(refs[0], refs[1]) if y_pair else refs[0][...]
    g_ref, sc_ref, sh_ref, w_ref, cos_ref, sin_ref, bd_ref = refs[n_y:n_y + 7]
    gain_refs = refs[n_y + 7:n_y + 7 + n_gain]
    p_ref = refs[n_y + 7 + n_gain]
    out_refs = refs[n_y + 8 + n_gain:]
    h = _norm_mod(y, g_ref[...], sc_ref[...], sh_ref[...])
    p_ref[...] = _dot(h.astype(BF16), w_ref[...])
    cos = cos_ref[...]
    sin = sin_ref[...]
    for (col, width, gi, rope, scale, _, heads), o_ref in zip(specs, out_refs):
        for c0 in range(0, width, LANES):
            x = p_ref[:, col + c0:col + c0 + LANES]
            if gi is not None:
                ss = _dot(x * x, bd_ref[...], HIGHEST)
                x = x * lax.rsqrt(ss * (1.0 / HD) + EPS) * gain_refs[gi][...]
            if rope:
                x = x * cos + _rope_rotate(x) * sin
            if scale != 1.0:
                x = x * scale
            if heads is None:
                o_ref[:, c0:c0 + LANES] = x.astype(o_ref.dtype)
            else:
                hw = width // heads
                per = LANES // hw

                @pl.when(_is_ctx_tile())
                def _(x=x, o_ref=o_ref, c0=c0, hw=hw, per=per):
                    for u in range(per):
                        o_ref[(c0 // LANES) * per + u] = x[:, u * hw:(u + 1) * hw].astype(o_ref.dtype)


def _norm_proj(y, gain, scale, shift, w_bf16, rope_cos, rope_sin, gains, specs):
    n = w_bf16.shape[1]
    y_pair = isinstance(y, tuple)
    bd = jnp.asarray(np.kron(np.eye(LANES // HD), np.ones((HD, HD))), F32)
    vec = pl.BlockSpec((None, 1, D_MODEL), lambda i: (_mod_row(i), 0, 0))
    rope_spec = pl.BlockSpec((TM, LANES), lambda i: (_rope_block(i), 0))
    in_specs = _pair_specs(D_MODEL) if y_pair else [pl.BlockSpec((TM, D_MODEL), lambda i: (i, 0))]
    in_specs += [pl.BlockSpec((1, D_MODEL), lambda i: (0, 0)),
                 vec, vec, pl.BlockSpec((D_MODEL, n), lambda i: (0, 0)),
                 rope_spec, rope_spec, pl.BlockSpec((LANES, LANES), lambda i: (0, 0))]
    in_specs += [pl.BlockSpec((1, LANES), lambda i: (0, 0)) for _ in gains]
    out_shape = [jax.ShapeDtypeStruct((T_ALL, n), F32)]
    out_specs = [pl.BlockSpec((TM, n), lambda i: (i, 0))]
    for (_, width, _, _, _, dtype, heads) in specs:
        if heads is None:
            out_shape.append(jax.ShapeDtypeStruct((T_ALL, width), dtype))
            out_specs.append(pl.BlockSpec((TM, width), lambda i: (i, 0)))
        else:
            out_shape.append(jax.ShapeDtypeStruct((BATCH, heads, SEQ, width // heads), dtype))
            out_specs.append(pl.BlockSpec((None, heads, SEQ, width // heads),
                                          lambda i: (jnp.minimum(i, CTX_TILES - 1), 0, 0, 0)))
    ys = list(y) if y_pair else [y]
    return pl.pallas_call(
        functools.partial(_norm_proj_kernel, specs=specs, n_gain=len(gains), y_pair=y_pair),
        out_shape=out_shape,
        grid=(N_TILES,),
        in_specs=in_specs,
        out_specs=out_specs,
        compiler_params=_params(("arbitrary",)),
        name="norm_proj",
    )(*ys, gain.reshape(1, D_MODEL), scale, shift, w_bf16, rope_cos, rope_sin, bd, *gains)


def _rope_tables():
    half = HD // 2
    freqs = 1.0 / (ROPE_THETA ** (jnp.arange(0, half, 2, dtype=F32) / half))
    t = jnp.arange(DEC_SEQ)
    rows = (t // GRID_W).astype(F32)
    cols = (t % GRID_W).astype(F32)
    ang = jnp.concatenate([rows[:, None] * freqs, cols[:, None] * freqs], axis=-1)
    cos = jnp.repeat(jnp.cos(ang), 2, axis=-1)
    sin = jnp.repeat(jnp.sin(ang), 2, axis=-1)
    cos = jnp.concatenate([jnp.tile(cos, (1, LANES // HD)), jnp.ones((TM, LANES), F32)], axis=0)
    sin = jnp.concatenate([jnp.tile(sin, (1, LANES // HD)), jnp.zeros((TM, LANES), F32)], axis=0)
    return cos, sin


def _lane_slice(ref, h, width=HD):
    per = LANES // width
    blk = ref[:, (h // per) * LANES:(h // per + 1) * LANES]
    if per == 1:
        return blk
    return blk[:, (h % per) * width:(h % per + 1) * width]


def _softmax_parts(scores):
    m = None
    for s in scores:
        ms = jnp.max(s, axis=-1, keepdims=True)
        m = ms if m is None else jnp.maximum(m, ms)
    ps = [jnp.exp(s - m) for s in scores]
    l = None
    for p in ps:
        ls = jnp.sum(p, axis=-1, keepdims=True)
        l = ls if l is None else l + ls
    return ps, l


def _attn_std_kernel(*refs, group, n_kv, has_cache, has_bias, bq):
    it = iter(refs)
    q_ref, kn_ref, vn_ref = next(it), next(it), next(it)
    kc_ref = vc_ref = b_ref = None
    if has_cache:
        kc_ref, vc_ref = next(it), next(it)
    if has_bias:
        b_ref = next(it)
    o_ref = next(it)
    outs = []
    for g in range(n_kv):
        qs = jnp.concatenate([_lane_slice(q_ref, g * group + j) for j in range(group)], axis=0)
        kn = _lane_slice(kn_ref, g)
        vn = _lane_slice(vn_ref, g)
        s_new = _dot_nt(qs, kn)
        if has_bias:
            s_new = s_new + b_ref[g]
        scores = [s_new]
        if has_cache:
            scores.append(_dot_nt(qs, kc_ref[g]))
        ps, l = _softmax_parts(scores)
        o = _dot(ps[0].astype(BF16), vn)
        if has_cache:
            o = o + _dot(ps[1].astype(BF16), vc_ref[g])
        o = o / l
        for j in range(group):
            outs.append(o[j * bq:(j + 1) * bq])
    o_ref[...] = jnp.concatenate(outs, axis=1).astype(o_ref.dtype)


def _attn_diff_kernel(*refs, has_cache):
    it = iter(refs)
    lam_ref, q_ref, kn_ref, vn_ref = next(it), next(it), next(it), next(it)
    kc_ref = vc_ref = None
    if has_cache:
        kc_ref, vc_ref = next(it), next(it)
    g_ref, o_ref = next(it), next(it)
    lam = lam_ref[0]
    post = lam_ref[1]
    outs = []
    for h in range(D_HEADS):
        pd_new, pd_c = None, None
        for j in range(2):
            f = 2 * h + j
            qs = _lane_slice(q_ref, f)
            scores = [_dot_nt(qs, _lane_slice(kn_ref, f))]
            if has_cache:
                scores.append(_dot_nt(qs, kc_ref[f]))
            ps, l = _softmax_parts(scores)
            r = 1.0 / l
            if j == 0:
                pd_new = ps[0] * r
                pd_c = ps[1] * r if has_cache else None
            else:
                r = r * lam
                pd_new = pd_new - ps[0] * r
                pd_c = pd_c - ps[1] * r if has_cache else None
        o = _dot(pd_new.astype(BF16), _lane_slice(vn_ref, h, D_VDIM))
        if has_cache:
            o = o + _dot(pd_c.astype(BF16), vc_ref[h])
        ms = jnp.mean(o * o, axis=-1, keepdims=True)
        outs.append(o * lax.rsqrt(ms + EPS) * g_ref[...] * post)
    o_ref[...] = jnp.concatenate(outs, axis=1).astype(o_ref.dtype)


def _attention(q, kn, vn, *, ctx, group=1, n_kv=1, cache=None, bias=None, diff=None, bq=256):
    if ctx:
        nb, sq, row0 = BATCH, SEQ, 0
    else:
        nb, sq, row0 = DEC_BATCH, DEC_SEQ, T_CTX
    nq = sq // bq
    qb0 = row0 // bq
    kb0 = row0 // sq
    wq, wk, wv = q.shape[1], kn.shape[1], vn.shape[1]
    in_specs = [
        pl.BlockSpec((bq, wq), lambda b, i: (qb0 + b * nq + i, 0)),
        pl.BlockSpec((sq, wk), lambda b, i: (kb0 + b, 0)),
        pl.BlockSpec((sq, wv), lambda b, i: (kb0 + b, 0)),
    ]
    args = [q, kn, vn]
    if cache is not None:
        kc, vc = cache
        in_specs += [pl.BlockSpec((None,) + kc.shape[1:], lambda b, i: (b, 0, 0, 0)),
                     pl.BlockSpec((None,) + vc.shape[1:], lambda b, i: (b, 0, 0, 0))]
        args += [kc, vc]
    if diff is None:
        if bias is not None:
            in_specs.append(pl.BlockSpec((bias.shape[0], bq, sq), lambda b, i: (0, i, 0)))
            args.append(bias)
        body = functools.partial(_attn_std_kernel, group=group, n_kv=n_kv, has_cache=cache is not None,
                                 has_bias=bias is not None, bq=bq)
    else:
        lam_vec, gain = diff
        in_specs = [pl.BlockSpec(memory_space=pltpu.SMEM)] + in_specs
        args = [lam_vec] + args
        in_specs.append(pl.BlockSpec((1, D_VDIM), lambda b, i: (0, 0)))
        args.append(gain.reshape(1, D_VDIM))
        body = functools.partial(_attn_diff_kernel, has_cache=cache is not None)
    return pl.pallas_call(
        body,
        out_shape=jax.ShapeDtypeStruct((nb * sq, 512), BF16),
        grid=(nb, nq),
        in_specs=in_specs,
        out_specs=pl.BlockSpec((bq, 512), lambda b, i: (b * nq + i, 0)),
        compiler_params=_params(("parallel", "parallel")),
        name="attention",
    )(*args)


GRID_ROWS = DEC_SEQ // GRID_W
NA_WIN_ROWS = min(NA_ROWS, GRID_ROWS)


def _na_bias_kernel(t_ref, o_ref):
    qr = pl.program_id(1)
    r0 = jnp.clip(qr - NA_WIN_ROWS // 2, 0, GRID_ROWS - NA_WIN_ROWS)
    parts = []
    for kr in range(GRID_ROWS):
        dr = jnp.clip(kr - qr + (NA_ROWS - 1), 0, 2 * NA_ROWS - 2)
        ok = (kr >= r0) & (kr < r0 + NA_WIN_ROWS)
        parts.append(jnp.where(ok, t_ref[dr], NEG_BIG))
    o_ref[...] = jnp.concatenate(parts, axis=1)


def _neighbourhood_bias(rpb):
    c = np.arange(GRID_W)
    c0 = np.clip(c - NA_COLS // 2, 0, GRID_W - NA_COLS)
    col_ok = (c[None, :] >= c0[:, None]) & (c[None, :] < c0[:, None] + NA_COLS)
    dc = np.clip(c[None, :] - c[:, None], 1 - NA_COLS, NA_COLS - 1) + (NA_COLS - 1)
    oh_c = jnp.asarray(dc[..., None] == np.arange(2 * NA_COLS - 1), F32)
    by_col = jnp.einsum("hrd,qkd->hrqk", rpb.astype(F32), oh_c, precision=HIGHEST)
    by_col = jnp.where(jnp.asarray(col_ok), by_col, NEG_BIG)
    n_dr = 2 * NA_ROWS - 1
    return pl.pallas_call(
        _na_bias_kernel,
        out_shape=jax.ShapeDtypeStruct((C_HEADS, DEC_SEQ, DEC_SEQ), F32),
        grid=(C_HEADS, GRID_ROWS),
        in_specs=[pl.BlockSpec((None, n_dr, GRID_W, GRID_W), lambda h, r: (h, 0, 0, 0))],
        out_specs=pl.BlockSpec((None, GRID_W, DEC_SEQ), lambda h, r: (h, r, 0)),
        compiler_params=_params(("parallel", "parallel")),
        name="na_bias",
    )(by_col)


def _log_sigmoid(x):
    return jnp.minimum(x, 0.0) - jnp.log1p(jnp.exp(-jnp.abs(x)))


def _mlstm_kernel(q_ref, k_ref, v_ref, g_ref, gb_ref, c0_ref, n0_ref, m0_ref, h_ref, c_ref, n_ref, m_ref, hb_ref,
                  *, seq):
    L = B_CHUNK
    nc = seq // L
    row = lax.broadcasted_iota(jnp.int32, (L, L), 0)
    col = lax.broadcasted_iota(jnp.int32, (L, L), 1)
    keeps = (col <= row, col >= row)
    k_scale = B_DK ** -0.5
    c_ref[...] = c0_ref[...]
    n_ref[...] = n0_ref[...]
    m_ref[...] = m0_ref[...]

    def step(j):
        for d in range(2):
            keep = keeps[d]
            c = j if d == 0 else nc - 1 - j
            off = c * L if isinstance(c, int) else pl.multiple_of(c * L, L)
            gates = g_ref[pl.ds(off, L), :] + gb_ref[...]
            cum = _dot(keep.astype(F32), _log_sigmoid(gates), HIGHEST)
            cum_t = cum.T
            gates_t = gates.T
            out_ref = h_ref if d == 0 else hb_ref
            for h in range(B_HEADS):
                ci = (2 * d) * B_HEADS + h
                cf = (2 * d + 1) * B_HEADS + h
                hs = slice(h * B_DK, (h + 1) * B_DK)
                C = c_ref[d, h]
                n = n_ref[d, h]
                m = m_ref[d, h][:, 0:1]
                qc = q_ref[pl.ds(off, L), hs]
                kc = k_ref[pl.ds(off, L), hs] * k_scale
                vc = v_ref[pl.ds(off, L), hs]
                b_col = cum[:, cf:cf + 1]
                i_col = gates[:, ci:ci + 1]
                b_row = cum_t[cf:cf + 1, :]
                i_row = gates_t[ci:ci + 1, :]
                dlog = jnp.where(keep, b_col - b_row + i_row, -jnp.inf)
                inter = b_col + m
                m_t = jnp.maximum(inter, jnp.max(dlog, axis=-1, keepdims=True))
                w_intra = jnp.exp(dlog - m_t)
                w_inter = jnp.exp(inter - m_t)
                qb = qc.astype(BF16)
                vb = vc.astype(BF16)
                qk = _dot_nt(qb, kc.astype(BF16)) * w_intra
                num = _dot(qk.astype(BF16), vb) + w_inter * _dot(qb, C.astype(BF16))
                den = jnp.sum(qk, axis=-1, keepdims=True) + w_inter * jnp.sum(qc * n, axis=-1, keepdims=True)
                out_ref[pl.ds(off, L), hs] = num / jnp.maximum(jnp.abs(den), jnp.exp(-m_t))
                b_last = b_col[L - 1:L, :] if d == 0 else b_col[0:1, :]
                end_col = b_last - b_col + i_col
                m_new = jnp.maximum(b_last + m, jnp.max(end_col, axis=0, keepdims=True))
                w_end = jnp.exp(end_col - m_new)
                decay = jnp.exp(b_last + m - m_new)
                kw = kc * w_end
                c_ref[d, h] = decay * C + _dot_tn(kw.astype(BF16), vb)
                n_ref[d, h] = decay * n + jnp.sum(kw, axis=0, keepdims=True)
                m_ref[d, h] = jnp.broadcast_to(m_new, (1, LANES))

    if nc <= 2:
        for j in range(nc):
            step(j)
    else:
        def body(j, carry):
            step(j)
            return carry

        lax.fori_loop(0, nc, body, 0)
    h_ref[...] = h_ref[...] + hb_ref[...]


def _mlstm(p, gate_bias, c0, n0, m0, *, ctx):
    if ctx:
        nb, seq, blk0 = BATCH, SEQ, 0
    else:
        nb, seq, blk0 = DEC_BATCH, DEC_SEQ, T_CTX // DEC_SEQ
    w = B_HEADS * B_DK

    def cols(c0_, width):
        return pl.BlockSpec((seq, width), lambda b: (blk0 + b, c0_ // width))

    gb = jnp.zeros((1, LANES), F32).at[0, :4 * B_HEADS].set(gate_bias.reshape(-1).astype(F32))
    st = lambda shape: pl.BlockSpec((None,) + shape, lambda b: (b, 0, 0, 0, 0))
    return pl.pallas_call(
        functools.partial(_mlstm_kernel, seq=seq),
        out_shape=[
            jax.ShapeDtypeStruct((nb * seq, w), F32),
            jax.ShapeDtypeStruct((nb, 2, B_HEADS, B_DK, B_DV), F32),
            jax.ShapeDtypeStruct((nb, 2, B_HEADS, 1, B_DK), F32),
            jax.ShapeDtypeStruct((nb, 2, B_HEADS, 1, LANES), F32),
        ],
        grid=(nb,),
        in_specs=[
            cols(EV_BQ, w), cols(EV_BK, w), cols(EV_BV, w), cols(EV_BG, LANES),
            pl.BlockSpec((1, LANES), lambda b: (0, 0)),
            st((2, B_HEADS, B_DK, B_DV)), st((2, B_HEADS, 1, B_DK)), st((2, B_HEADS, 1, LANES)),
        ],
        out_specs=[
            pl.BlockSpec((seq, w), lambda b: (b, 0)),
            st((2, B_HEADS, B_DK, B_DV)), st((2, B_HEADS, 1, B_DK)), st((2, B_HEADS, 1, LANES)),
        ],
        scratch_shapes=[pltpu.VMEM((seq, w), F32)],
        compiler_params=_params(("parallel",)),
        name="mlstm",
    )(p, p, p, p, gb, c0, n0, m0)


def _merge_value(refs, even, y_pair):
    a_ctx, a_lat, b_ctx, b_lat = refs[:4]
    rest = refs[4:]
    if even:
        bo_ref, ng_ref = rest[:2]
        rest = rest[2:]
        parts = [_pair_value(a_ctx, a_lat)]
        for h in range(B_HEADS):
            hs = slice(h * B_DV, (h + 1) * B_DV)
            x = _pair_value(b_ctx, b_lat, hs)
            ms = jnp.mean(x * x, axis=-1, keepdims=True)
            xn = x * lax.rsqrt(ms + EPS) * ng_ref[:, hs]
            parts.append((jax.nn.sigmoid(bo_ref[:, hs]) * xn).astype(BF16))
    else:
        parts = [_pair_value(a_ctx, a_lat), _pair_value(b_ctx, b_lat)]
    w_ref = rest[0]
    y = _pair_value(rest[1], rest[2]) if y_pair else rest[1][...]
    g_ref = rest[-1]
    cat = jnp.concatenate(parts, axis=1)
    return y + g_ref[...] * _dot(cat, w_ref[...])


def _merge_specs(y, gate, w_bf16, a, b, p, norm_gain):
    in_specs = _pair_specs(512) + _pair_specs(512)
    args = [*a, *b]
    if p is not None:
        in_specs += [pl.BlockSpec((TM, 512), lambda i: (i, EV_BO // 512)), pl.BlockSpec((1, 512), lambda i: (0, 0))]
        args += [p, norm_gain.reshape(1, 512)]
    in_specs.append(pl.BlockSpec((D_MODEL, D_MODEL), lambda i: (0, 0)))
    args.append(w_bf16)
    if isinstance(y, tuple):
        in_specs += _pair_specs(D_MODEL)
        args += list(y)
    else:
        in_specs.append(pl.BlockSpec((TM, D_MODEL), lambda i: (i, 0)))
        args.append(y)
    in_specs.append(pl.BlockSpec((None, 1, D_MODEL), lambda i: (_mod_row(i), 0, 0)))
    args.append(gate)
    return in_specs, args


SLAB = D_MODEL // LANES


def _load_slabs(ref, rows):
    return jnp.concatenate([ref[pl.ds(c, rows, stride=SLAB), :] for c in range(SLAB)], axis=1)


def _store_slabs(ref, x):
    for c in range(SLAB):
        ref[pl.ds(c, x.shape[0], stride=SLAB), :] = x[:, c * LANES:(c + 1) * LANES]


def _slab(ref, idx):
    return ref.at[pl.ds(pl.multiple_of(idx * SLAB, SLAB), SLAB)]


def _merge_router_kernel(*refs, even, y_pair):
    n_merge = 4 + (2 if even else 0) + 1 + (2 if y_pair else 1) + 1
    merge_refs = refs[:n_merge]
    (g_ref, sc_ref, sh_ref, whi_ref, wlo_ref, b_ref,
     y_out_ref, h_ref, ti_ref, tp_ref, rk_ref, cnt_ref, base_ref) = refs[n_merge:]

    @pl.when(pl.program_id(0) == 0)
    def _():
        base_ref[...] = jnp.zeros(base_ref.shape, F32)

    y = _merge_value(merge_refs, even, y_pair)
    y_out_ref[...] = y
    h = _norm_mod(y, g_ref[...], sc_ref[...], sh_ref[...])
    _store_slabs(h_ref, h)
    h_hi = h.astype(BF16)
    h_lo = (h - h_hi.astype(F32)).astype(BF16)
    logits = (_dot(h_hi, whi_ref[...]) + (_dot(h_hi, wlo_ref[...]) + _dot(h_lo, whi_ref[...]))
              + b_ref[...])
    lane = lax.broadcasted_iota(jnp.int32, logits.shape, 1)
    lane_f = lane.astype(F32)
    vals, idxs = [], []
    for _ in range(TOP_K):
        mx = jnp.max(logits, axis=-1, keepdims=True)
        ix = jnp.min(jnp.where(logits == mx, lane_f, float(LANES)), axis=-1, keepdims=True)
        vals.append(mx)
        idxs.append(ix)
        logits = jnp.where(lane_f == ix, -jnp.inf, logits)
    es = [jnp.exp(v - vals[0]) for v in vals]
    tot = es[0] + es[1] + es[2] + es[3]
    ti = jnp.zeros(logits.shape, F32)
    tp = jnp.zeros(logits.shape, F32)
    for k in range(TOP_K):
        ti = jnp.where(lane == k, idxs[k], ti)
        tp = jnp.where(lane == k, es[k] / tot, tp)
    ti_ref[...] = ti.T[0:8, :].astype(jnp.int32)
    tp_ref[...] = tp
    onehots = [(lane_f == ix).astype(F32) for ix in idxs]
    cnt = onehots[0] + onehots[1] + onehots[2] + onehots[3]
    row = lax.broadcasted_iota(jnp.int32, (TM, TM), 0)
    col = lax.broadcasted_iota(jnp.int32, (TM, TM), 1)
    before = _dot((col < row).astype(BF16), cnt.astype(BF16)) + base_ref[...]
    rk = jnp.zeros(logits.shape, F32)
    for k in range(TOP_K):
        rk = jnp.where(lane == k, jnp.sum(onehots[k] * before, axis=-1, keepdims=True), rk)
    rk_ref[...] = rk.T[0:8, :].astype(jnp.int32)
    base_ref[...] = base_ref[...] + jnp.sum(cnt, axis=0, keepdims=True)
    cnt_ref[...] = base_ref[...]


def _merge_router(y, gate, w_out_bf16, a, b, gain, scale, shift, rw, rb, *, p=None, norm_gain=None):
    merge_in_specs, merge_args = _merge_specs(y, gate, w_out_bf16, a, b, p, norm_gain)
    vec = pl.BlockSpec((None, 1, D_MODEL), lambda i: (_mod_row(i), 0, 0))
    rw_p = jnp.zeros((D_MODEL, LANES), F32).at[:, :N_EXPERTS].set(rw)
    rb_p = jnp.full((1, LANES), NEG_BIG, F32).at[0, :N_EXPERTS].set(rb)
    rw_hi = rw_p.astype(BF16)
    tile = lambda w: pl.BlockSpec((TM, w), lambda i: (i, 0))
    by_choice = pl.BlockSpec((None, 8, TM), lambda i: (i, 0, 0))
    return pl.pallas_call(
        functools.partial(_merge_router_kernel, even=p is not None, y_pair=isinstance(y, tuple)),
        out_shape=[jax.ShapeDtypeStruct((T_ALL, D_MODEL), F32),
                   jax.ShapeDtypeStruct((T_ALL * SLAB, LANES), F32),
                   jax.ShapeDtypeStruct((N_TILES, 8, TM), jnp.int32),
                   jax.ShapeDtypeStruct((T_ALL, LANES), F32),
                   jax.ShapeDtypeStruct((N_TILES, 8, TM), jnp.int32),
                   jax.ShapeDtypeStruct((1, LANES), F32)],
        grid=(N_TILES,),
        in_specs=merge_in_specs + [
            pl.BlockSpec((1, D_MODEL), lambda i: (0, 0)), vec, vec,
            pl.BlockSpec((D_MODEL, LANES), lambda i: (0, 0)), pl.BlockSpec((D_MODEL, LANES), lambda i: (0, 0)),
            pl.BlockSpec((1, LANES), lambda i: (0, 0))],
        out_specs=[tile(D_MODEL), pl.BlockSpec((TM * SLAB, LANES), lambda i: (i, 0)), by_choice, tile(LANES),
                   by_choice, pl.BlockSpec((1, LANES), lambda i: (0, 0))],
        scratch_shapes=[pltpu.VMEM((1, LANES), F32)],
        compiler_params=_params(("arbitrary",)),
        name="merge_router",
    )(*merge_args, gain.reshape(1, D_MODEL), scale, shift, rw_hi, (rw_p - rw_hi.astype(F32)).astype(BF16), rb_p)


def _route_plan(top_i, rank, counts):
    experts = jnp.arange(N_EXPERTS, dtype=jnp.int32)
    padded = ((counts + MOE_TM - 1) // MOE_TM) * MOE_TM
    seg_end = jnp.cumsum(padded)
    seg_start = seg_end - padded
    pos = rank
    for e in range(N_EXPERTS - 1):
        pos = pos + jnp.where(top_i > e, padded[e], 0)
    n_active = seg_end[-1] // MOE_TM
    fill = jnp.concatenate([seg_start + counts, padded - counts, n_active[None]]).astype(jnp.int32)
    tile_start = jnp.arange(MOE_TILES, dtype=jnp.int32) * MOE_TM
    tile_expert = jnp.sum((seg_end[None, :] <= tile_start[:, None]).astype(jnp.int32), axis=1)
    last = jnp.sum((seg_end <= (n_active - 1) * MOE_TM).astype(jnp.int32))
    tile_expert = jnp.minimum(jnp.where(tile_start < seg_end[-1], tile_expert, last), N_EXPERTS - 1)
    owns = (padded > 0).astype(jnp.int32)
    run_of_expert = jnp.cumsum(owns) - 1
    run_expert = jnp.sum(jnp.where((run_of_expert[None, :] == experts[:, None]) & (owns[None, :] > 0),
                                   experts[None, :], 0), axis=1)
    runs = jnp.concatenate([run_expert, jnp.sum(owns)[None]]).astype(jnp.int32)
    tile_run = jnp.sum(jnp.where(tile_expert[:, None] == experts[None, :], run_of_expert[None, :], 0), axis=1)
    experts_plan = (tile_expert.astype(jnp.int32), n_active.reshape(1).astype(jnp.int32),
                    tile_run.astype(jnp.int32), runs)
    return pos.astype(jnp.int32), fill, experts_plan


DMA_UNROLL = 4
DMA_QUEUES = 2


def _wait_slabs(ref, n_slabs, sem):
    view = ref.at[pl.ds(0, n_slabs * SLAB)]
    pltpu.make_async_copy(view, view, sem).wait()


def _dispatch_kernel(pos_ref, fill_ref, h_ref, xs_ref, inv_ref, sem):
    i = pl.program_id(0)
    base = i * TM

    def issue(j, carry):
        for u in range(DMA_UNROLL):
            t = j * DMA_UNROLL + u
            for k in range(TOP_K):
                a = base * TOP_K + k * TM + t
                slot = pos_ref[a]
                inv_ref[slot] = a
                pltpu.make_async_copy(_slab(h_ref, t), _slab(xs_ref, slot), sem).start(priority=k % DMA_QUEUES)
        return carry

    lax.fori_loop(0, TM // DMA_UNROLL, issue, 0)

    @pl.when(i == 0)
    def _():
        def per_expert(e, total):
            start = fill_ref[e]
            n = fill_ref[N_EXPERTS + e]

            def one(r, carry):
                slot = start + r
                inv_ref[slot] = N_ASSIGN + slot % MOE_TM
                pltpu.make_async_copy(_slab(h_ref, 0), _slab(xs_ref, slot), sem).start()
                return carry

            lax.fori_loop(0, n, one, 0)
            return total + n

        total = lax.fori_loop(0, N_EXPERTS, per_expert, 0)

        n_active = fill_ref[2 * N_EXPERTS]

        def unused_tile(ti, carry):
            dst = xs_ref.at[pl.ds(pl.multiple_of(ti * (MOE_TM * SLAB), MOE_TM * SLAB), MOE_TM * SLAB)]
            pltpu.make_async_copy(h_ref, dst, sem).start()
            return carry

        lax.fori_loop(n_active, MOE_TILES, unused_tile, 0)

        def unused_slot(slot, carry):
            inv_ref[slot] = N_ASSIGN
            return carry

        lax.fori_loop(n_active * MOE_TM, MOE_ROWS, unused_slot, 0)
        total = total + (MOE_TILES - n_active) * MOE_TM

        @pl.when(total > 0)
        def _():
            _wait_slabs(xs_ref, total, sem)

    _wait_slabs(xs_ref, TM * TOP_K, sem)


def _dispatch(h_slabs, pos, fill):
    grid_spec = pltpu.PrefetchScalarGridSpec(
        num_scalar_prefetch=2,
        grid=(N_TILES,),
        in_specs=[pl.BlockSpec((TM * SLAB, LANES), lambda i, pos, fill: (i, 0))],
        out_specs=[pl.BlockSpec(memory_space=pl.ANY), pl.BlockSpec(memory_space=pltpu.SMEM)],
        scratch_shapes=[pltpu.SemaphoreType.DMA],
    )
    return pl.pallas_call(
        _dispatch_kernel,
        out_shape=[jax.ShapeDtypeStruct((MOE_ROWS * SLAB, LANES), F32),
                   jax.ShapeDtypeStruct((MOE_ROWS,), jnp.int32)],
        grid_spec=grid_spec,
        compiler_params=_params(("arbitrary",)),
        name="moe_dispatch",
    )(pos, fill, h_slabs)


MOE_HALF = MOE_TM // 2


def _moe_kernel(te_ref, na_ref, ts_ref, ex_ref, inv_ref, x_ref, wgu_hbm, bgu_ref, wd_hbm, bd_ref, out_hbm,
                wgu_f32, wd_f32, wgu_bf, wd_bf, obuf0, obuf1, sems, osems, *, layer):
    i = pl.program_id(0)
    s = ts_ref[i]
    first = (i == 0) | (s != ts_ref[jnp.maximum(i - 1, 0)])
    n_active = na_ref[0]

    def start_rows(buf, half, tile):
        for r in range(MOE_HALF):
            dst = _slab(out_hbm, inv_ref[tile * MOE_TM + half * MOE_HALF + r])
            pltpu.make_async_copy(buf.at[pl.ds(r * SLAB, SLAB)], dst, osems.at[half]).start(
                priority=r % DMA_QUEUES)

    def wait_rows(half):
        _wait_slabs(out_hbm, MOE_HALF, osems.at[half])

    def ffn_tile(deferred):
        if deferred:
            wait_rows(0)
            start_rows(obuf1, 1, i - 1)
        x = _load_slabs(x_ref, MOE_TM).astype(BF16)
        gu = _dot(x, wgu_bf[...]) + bgu_ref[...]
        gate = jnp.minimum(gu[:, :D_FF], SWIGLU_LIMIT)
        up = jnp.clip(gu[:, D_FF:], -SWIGLU_LIMIT, SWIGLU_LIMIT)
        act = ((up + 1.0) * gate * jax.nn.sigmoid(SWIGLU_ALPHA * gate)).astype(BF16)
        _store_slabs(obuf0, _dot(act[:MOE_HALF], wd_bf[...]) + bd_ref[...])
        if deferred:
            wait_rows(1)
        start_rows(obuf0, 0, i)
        _store_slabs(obuf1, _dot(act[MOE_HALF:], wd_bf[...]) + bd_ref[...])

    def weight_copies(slot):
        e = ex_ref[slot]
        b = slot % 2
        return (pltpu.make_async_copy(wgu_hbm.at[layer, e], wgu_f32.at[b], sems.at[0, b]),
                pltpu.make_async_copy(wd_hbm.at[layer, e], wd_f32.at[b], sems.at[1, b]))

    @pl.when(i == 0)
    def _():
        for cp in weight_copies(0):
            cp.start()

    @pl.when(first)
    def _():
        for cp in weight_copies(s):
            cp.wait()

        @pl.when(s + 1 < ex_ref[N_EXPERTS])
        def _():
            for cp in weight_copies(s + 1):
                cp.start()

        b = s % 2
        wgu_bf[...] = wgu_f32[b].astype(BF16)
        wd_bf[...] = wd_f32[b].astype(BF16)

    @pl.when(i == 0)
    def _():
        ffn_tile(False)

    @pl.when((i > 0) & (i < n_active))
    def _():
        ffn_tile(True)

    @pl.when(i == n_active)
    def _():
        wait_rows(0)
        start_rows(obuf1, 1, i - 1)
        wait_rows(1)
        for half, buf in enumerate((obuf0, obuf1)):
            dst = out_hbm.at[pl.ds((N_ASSIGN + half * MOE_HALF) * SLAB, MOE_HALF * SLAB)]
            pltpu.make_async_copy(buf, dst, osems.at[half]).start()
        wait_rows(0)
        wait_rows(1)


def _moe_experts(layer, xs, plan, inv, w_gu, b_gu, w_down, b_down):
    const = lambda i, te, na, ts, ex, inv: (layer, te[i], 0, 0)
    grid_spec = pltpu.PrefetchScalarGridSpec(
        num_scalar_prefetch=5,
        grid=(MOE_TILES,),
        in_specs=[
            pl.BlockSpec((MOE_TM * SLAB, LANES), lambda i, te, na, ts, ex, inv: (jnp.minimum(i, na[0] - 1), 0)),
            pl.BlockSpec(memory_space=pl.ANY),
            pl.BlockSpec((None, None, 1, 2 * D_FF), const),
            pl.BlockSpec(memory_space=pl.ANY),
            pl.BlockSpec((None, None, 1, D_MODEL), const),
        ],
        out_specs=pl.BlockSpec(memory_space=pl.ANY),
        scratch_shapes=[pltpu.VMEM((2, D_MODEL, 2 * D_FF), F32), pltpu.VMEM((2, D_FF, D_MODEL), F32),
                        pltpu.VMEM((D_MODEL, 2 * D_FF), BF16), pltpu.VMEM((D_FF, D_MODEL), BF16),
                        pltpu.VMEM((MOE_HALF * SLAB, LANES), F32), pltpu.VMEM((MOE_HALF * SLAB, LANES), F32),
                        pltpu.SemaphoreType.DMA((2, 2)), pltpu.SemaphoreType.DMA((2,))],
    )
    return pl.pallas_call(
        functools.partial(_moe_kernel, layer=layer),
        out_shape=jax.ShapeDtypeStruct(((N_ASSIGN + MOE_TM) * SLAB, LANES), F32),
        grid_spec=grid_spec,
        compiler_params=_params(("arbitrary",)),
        name="moe_experts",
    )(*plan, inv, xs, w_gu, b_gu.reshape(DEPTH, N_EXPERTS, 1, 2 * D_FF), w_down,
      b_down.reshape(DEPTH, N_EXPERTS, 1, D_MODEL))


def _combine_kernel(out_ref, y_ref, tp_ref, g_ref, *rest, final):
    if final:
        fg_ref, n_ctx_ref, n_lat_ref = rest
    else:
        (o_ref,) = rest

    tp = tp_ref[...]
    ss = jnp.zeros((TM, 1), F32)
    chunks = []
    for c in range(SLAB):
        cs = slice(c * LANES, (c + 1) * LANES)
        acc = tp[:, 0:1] * out_ref[pl.ds(c, TM, stride=SLAB), :]
        for k in range(1, TOP_K):
            acc = acc + tp[:, k:k + 1] * out_ref[pl.ds(k * TM * SLAB + c, TM, stride=SLAB), :]
        yc = y_ref[:, cs] + g_ref[:, cs] * acc
        if final:
            ss = ss + jnp.sum(yc * yc, axis=-1, keepdims=True)
            chunks.append(yc)
        else:
            o_ref[:, cs] = yc
    if final:
        inv = lax.rsqrt(ss * (1.0 / D_MODEL) + EPS)
        normed = jnp.concatenate([chunks[c] * inv * fg_ref[:, c * LANES:(c + 1) * LANES] for c in range(SLAB)],
                                 axis=1)

        @pl.when(_is_ctx_tile())
        def _():
            n_ctx_ref[...] = normed

        @pl.when(jnp.logical_not(_is_ctx_tile()))
        def _():
            n_lat_ref[...] = normed


def _combine(y, out_slabs, top_p, gate, final_gain=None):
    final = final_gain is not None
    tile = pl.BlockSpec((TM, D_MODEL), lambda i: (i, 0))
    in_specs = [pl.BlockSpec((TOP_K * TM * SLAB, LANES), lambda i: (i, 0)), tile,
                pl.BlockSpec((TM, LANES), lambda i: (i, 0)),
                pl.BlockSpec((None, 1, D_MODEL), lambda i: (_mod_row(i), 0, 0))]
    args = [out_slabs, y, top_p, gate]
    if final:
        in_specs.append(pl.BlockSpec((1, D_MODEL), lambda i: (0, 0)))
        args.append(final_gain.reshape(1, D_MODEL))
        out_shape = [jax.ShapeDtypeStruct((T_CTX, D_MODEL), F32), jax.ShapeDtypeStruct((T_LAT, D_MODEL), F32)]
        out_specs = [pl.BlockSpec((TM, D_MODEL), lambda i: (jnp.minimum(i, CTX_TILES - 1), 0)),
                     pl.BlockSpec((TM, D_MODEL), lambda i: (jnp.maximum(i - CTX_TILES, 0), 0))]
    else:
        out_shape = [jax.ShapeDtypeStruct((T_ALL, D_MODEL), F32)]
        out_specs = [tile]
    return pl.pallas_call(
        functools.partial(_combine_kernel, final=final),
        out_shape=out_shape,
        grid=(N_TILES,),
        in_specs=in_specs,
        out_specs=out_specs,
        compiler_params=_params(("arbitrary",)),
        name="moe_combine",
    )(*args)


def _moe_layer(layer, y, routed, gate, w_gu, b_gu, w_down, b_down, final_gain=None):
    h_slabs, top_i, top_p, rank, counts = routed
    pos, fill, experts_plan = _route_plan(top_i[:, :TOP_K].reshape(-1), rank[:, :TOP_K].reshape(-1),
                                          counts[0, :N_EXPERTS].astype(jnp.int32))
    xs, inv = _dispatch(h_slabs, pos, fill)
    out = _moe_experts(layer, xs, experts_plan, inv, w_gu, b_gu, w_down, b_down)
    return _combine(y, out, top_p, gate, final_gain)


def kernel(x_prompt, x_sample, c, cache_a_k, cache_a_v, state_b_C, state_b_n, state_b_m, cache_c_k, cache_c_v, cache_d_k, cache_d_v, c_ctx, w_mod, b_mod, norm1_g, norm2_g, w_in_even, w_out_even, a_q_gain, a_k_gain, b_gate_bias, b_norm_gain, w_in_odd, w_out_odd, c_rpb, d_lambda, d_norm_gain, router_w, router_b, expert_w_gu, expert_b_gu, expert_w_down, expert_b_down, final_norm_g):
    y = (x_prompt.reshape(T_CTX, D_MODEL), x_sample.reshape(T_LAT, D_MODEL))
    cond = jnp.zeros((MOD_ROWS, D_MODEL), F32).at[0].set(c_ctx).at[1:1 + DEC_BATCH].set(c)
    mod = _modulation(cond, w_mod, b_mod).reshape(DEPTH, MOD_ROWS, 6, 1, D_MODEL)
    rope_cos, rope_sin = _rope_tables()
    scale = HD ** -0.5
    outs = {}

    for layer in range(DEPTH):
        sh1, sc1, g1, sh2, sc2, g2 = (mod[layer, :, k] for k in range(6))
        j = layer // 2
        if layer % 2 == 0:
            w = w_in_even[j]
            sizes = np.cumsum([0, 512, 128, 128, 512, 512, 512, 512, 16])
            aq, ak, av, bq, bk, bv, bo, bg = (w[:, sizes[k]:sizes[k + 1]] for k in range(8))
            w_in = jnp.concatenate([aq, bo, bq, bk, bv, ak, av, bg, jnp.zeros((D_MODEL, EV_N - EV_BG - 16), F32)],
                                   axis=1).astype(BF16)
            qg = jnp.tile(a_q_gain[j], LANES // HD).reshape(1, LANES)
            kg = jnp.tile(a_k_gain[j], LANES // HD).reshape(1, LANES)
            specs = ((EV_AQ, 512, 0, True, scale, BF16, None), (EV_AK, 128, 1, True, 1.0, BF16, None),
                     (EV_AV, 128, None, False, 1.0, BF16, None),
                     (EV_AK, 128, 1, False, 1.0, F32, A_KV), (EV_AV, 128, None, False, 1.0, F32, A_KV))
            p, qa, ka, va, new_ak, new_av = _norm_proj(y, norm1_g[layer], sc1, sh1, w_in, rope_cos, rope_sin,
                                                       [qg, kg], specs)
            oa_ctx = _attention(qa, ka, va, ctx=True, group=A_HEADS // A_KV, n_kv=A_KV)
            cache = (cache_a_k[:, j].astype(BF16), cache_a_v[:, j].astype(BF16))
            oa_lat = _attention(qa, ka, va, ctx=False, group=A_HEADS // A_KV, n_kv=A_KV, cache=cache)
            zc = jnp.zeros((BATCH, 2, B_HEADS, B_DK, B_DV), F32)
            zn = jnp.zeros((BATCH, 2, B_HEADS, 1, B_DK), F32)
            zm = jnp.zeros((BATCH, 2, B_HEADS, 1, LANES), F32)
            hb_ctx, bC, bn, bm = _mlstm(p, b_gate_bias[j], zc, zn, zm, ctx=True)
            m0 = jnp.broadcast_to(state_b_m[:, j][..., None, None], (DEC_BATCH, 2, B_HEADS, 1, LANES))
            hb_lat, _, _, _ = _mlstm(p, b_gate_bias[j], state_b_C[:, j], state_b_n[:, j][:, :, :, None, :], m0,
                                     ctx=False)
            y, *routed = _merge_router(y, g1, w_out_even[j].astype(BF16), (oa_ctx, oa_lat), (hb_ctx, hb_lat),
                                       norm2_g[layer], sc2, sh2, router_w[layer], router_b[layer],
                                       p=p, norm_gain=b_norm_gain[j])
            outs.setdefault("a_k", []).append(new_ak)
            outs.setdefault("a_v", []).append(new_av)
            outs.setdefault("b_C", []).append(bC)
            outs.setdefault("b_n", []).append(bn[:, :, :, 0, :])
            outs.setdefault("b_m", []).append(bm[:, :, :, 0, 0])
        else:
            specs = ((0, 512, None, False, scale, BF16, None), (512, 512, None, False, 1.0, BF16, None),
                     (1024, 512, None, False, 1.0, BF16, None), (1536, 512, None, True, scale, BF16, None),
                     (2048, 512, None, True, 1.0, BF16, None), (2560, 512, None, False, 1.0, BF16, None),
                     (512, 512, None, False, 1.0, F32, C_HEADS), (1024, 512, None, False, 1.0, F32, C_HEADS),
                     (2048, 512, None, False, 1.0, F32, 2 * D_HEADS), (2560, 512, None, False, 1.0, F32, D_HEADS))
            p, qc, kc, vc, qd, kd, vd, new_ck, new_cv, new_dk, new_dv = _norm_proj(
                y, norm1_g[layer], sc1, sh1, w_in_odd[j].astype(BF16), rope_cos, rope_sin, [], specs)
            lam_init = 0.8 - 0.6 * math.exp(-0.3 * layer)
            lp = d_lambda[j].astype(F32)
            lam = jnp.exp(jnp.sum(lp[0] * lp[1])) - jnp.exp(jnp.sum(lp[2] * lp[3])) + lam_init
            lam_vec = jnp.stack([lam, jnp.asarray(1.0 - lam_init, F32)]).astype(F32)
            diff = (lam_vec, d_norm_gain[j])
            oc_ctx = _attention(qc, kc, vc, ctx=True, n_kv=C_HEADS)
            od_ctx = _attention(qd, kd, vd, ctx=True, diff=diff)
            bias = _neighbourhood_bias(c_rpb[j])
            oc_lat = _attention(qc, kc, vc, ctx=False, n_kv=C_HEADS, bias=bias,
                                cache=(cache_c_k[:, j].astype(BF16), cache_c_v[:, j].astype(BF16)))
            kd_cache = cache_d_k[:, j].reshape(DEC_BATCH, 2 * D_HEADS, PAST_LEN, HD).astype(BF16)
            od_lat = _attention(qd, kd, vd, ctx=False, diff=diff, cache=(kd_cache, cache_d_v[:, j].astype(BF16)))
            y, *routed = _merge_router(y, g1, w_out_odd[j].astype(BF16), (oc_ctx, oc_lat), (od_ctx, od_lat),
                                       norm2_g[layer], sc2, sh2, router_w[layer], router_b[layer])
            outs.setdefault("c_k", []).append(new_ck)
            outs.setdefault("c_v", []).append(new_cv)
            outs.setdefault("d_k", []).append(new_dk.reshape(BATCH, D_HEADS, 2, SEQ, HD))
            outs.setdefault("d_v", []).append(new_dv)
        res = _moe_layer(layer, y, routed, g2, expert_w_gu, expert_b_gu, expert_w_down, expert_b_down,
                         final_gain=final_norm_g if layer == DEPTH - 1 else None)
        y = res[0]
    y_prompt, y_sample = res
    stack = lambda k: jnp.stack(outs[k], axis=1)
    return (y_prompt.reshape(BATCH, SEQ, D_MODEL), y_sample.reshape(DEC_BATCH, DEC_SEQ, D_MODEL),
            stack("a_k"), stack("a_v"), stack("b_C"), stack("b_n"), stack("b_m"),
            stack("c_k"), stack("c_v"), stack("d_k"), stack("d_v"))
```

```python
import functools
import math

import numpy as np
import jax
import jax.numpy as jnp
from jax import lax
from jax.experimental import pallas as pl
from jax.experimental.pallas import tpu as pltpu

D_MODEL = 1024
BATCH = 32
SEQ = 256
DEPTH = 2
DEC_BATCH = 8
DEC_SEQ = 1024
PAST_LEN = 512
GRID_W = 64
HD = 64
A_HEADS = 8
A_KV = 2
B_HEADS = 4
B_DK = 128
B_DV = 128
B_CHUNK = 128
C_HEADS = 8
NA_ROWS = 8
NA_COLS = 16
D_HEADS = 4
D_VDIM = 2 * HD
N_EXPERTS = 32
TOP_K = 4
D_FF = 1024
SWIGLU_LIMIT = 7.0
SWIGLU_ALPHA = 1.702
ROPE_THETA = 10000.0
EPS = 1e-6

F32 = jnp.float32
BF16 = jnp.bfloat16
HIGHEST = lax.Precision.HIGHEST

T_CTX = BATCH * SEQ
T_LAT = DEC_BATCH * DEC_SEQ
T_ALL = T_CTX + T_LAT
TM = 256
CTX_TILES = T_CTX // TM
LAT_TILES_PER_BATCH = DEC_SEQ // TM
N_TILES = T_ALL // TM
MOD_ROWS = 16
LANES = 128
NEG_BIG = -1e30
MOE_TM = 256
N_ASSIGN = T_ALL * TOP_K
MOE_ROWS = N_ASSIGN + N_EXPERTS * MOE_TM
MOE_TILES = MOE_ROWS // MOE_TM
VMEM_LIMIT = 56 * 1024 * 1024

EV_AQ, EV_BO, EV_BQ, EV_BK, EV_BV, EV_AK, EV_AV, EV_BG = 0, 512, 1024, 1536, 2048, 2560, 2688, 2816
EV_N = 2944
OD_N = 3072


def _params(sem, vmem=VMEM_LIMIT):
    return pltpu.CompilerParams(dimension_semantics=sem, vmem_limit_bytes=vmem)


def _mod_row(i):
    return jnp.where(i < CTX_TILES, 0, 1 + (i - CTX_TILES) // LAT_TILES_PER_BATCH)


def _rope_block(i):
    return jnp.where(i < CTX_TILES, LAT_TILES_PER_BATCH, (i - CTX_TILES) % LAT_TILES_PER_BATCH)


def _dot(a, b, precision=None):
    return jnp.dot(a, b, preferred_element_type=F32, precision=precision)


def _dot_nt(a, b):
    return lax.dot_general(a, b, (((1,), (1,)), ((), ())), preferred_element_type=F32)


def _dot_tn(a, b):
    return lax.dot_general(a, b, (((0,), (0,)), ((), ())), preferred_element_type=F32)


def _modulation_kernel(c_ref, w_ref, b_ref, o_ref):
    c = c_ref[...]
    s = c * jax.nn.sigmoid(c)
    o_ref[...] = _dot(s, w_ref[...], HIGHEST) + b_ref[...]


def _modulation(cond, w_mod, b_mod):
    tn = 1536
    return pl.pallas_call(
        _modulation_kernel,
        out_shape=jax.ShapeDtypeStruct((DEPTH, MOD_ROWS, 6 * D_MODEL), F32),
        grid=(DEPTH, 6 * D_MODEL // tn),
        in_specs=[
            pl.BlockSpec((MOD_ROWS, D_MODEL), lambda l, j: (0, 0)),
            pl.BlockSpec((None, D_MODEL, tn), lambda l, j: (l, 0, j)),
            pl.BlockSpec((None, 1, tn), lambda l, j: (l, 0, j)),
        ],
        out_specs=pl.BlockSpec((None, MOD_ROWS, tn), lambda l, j: (l, 0, j)),
        compiler_params=_params(("parallel", "parallel")),
        name="modulation",
    )(cond, w_mod, b_mod.reshape(DEPTH, 1, 6 * D_MODEL))


def _norm_mod(y, g, sc, sh):
    ms = jnp.mean(y * y, axis=-1, keepdims=True)
    return (y * lax.rsqrt(ms + EPS) * g) * (1.0 + sc) + sh


def _rope_rotate(x):
    w = x.shape[-1]
    lane = lax.broadcasted_iota(jnp.int32, x.shape, 1)
    nxt = pltpu.roll(x, w - 1, 1)
    prv = pltpu.roll(x, 1, 1)
    return jnp.where((lane & 1) == 0, -nxt, prv)


def _is_ctx_tile():
    return pl.program_id(0) < CTX_TILES


def _pair_specs(width):
    return [pl.BlockSpec((TM, width), lambda i: (jnp.minimum(i, CTX_TILES - 1), 0)),
            pl.BlockSpec((TM, width), lambda i: (jnp.maximum(i - CTX_TILES, 0), 0))]


def _pair_value(ctx_ref, lat_ref, cols=slice(None)):
    return jnp.where(_is_ctx_tile(), ctx_ref[:, cols], lat_ref[:, cols])


def _norm_proj_kernel(*refs, specs, n_gain, y_pair):
    n_y = 2 if y_pair else 1
    y = _pair_value(refs[0], refs[1]) if y_pair else refs[0][...]
    g_ref, sc_ref, sh_ref, w_ref, cos_ref, sin_ref, bd_ref = refs[n_y:n_y + 7]
    gain_refs = refs[n_y + 7:n_y + 7 + n_gain]
    p_ref = refs[n_y + 7 + n_gain]
    out_refs = refs[n_y + 8 + n_gain:]
    h = _norm_mod(y, g_ref[...], sc_ref[...], sh_ref[...])
    p_ref[...] = _dot(h.astype(BF16), w_ref[...])
    cos = cos_ref[...]
    sin = sin_ref[...]
    for (col, width, gi, rope, scale, _, heads), o_ref in zip(specs, out_refs):
        for c0 in range(0, width, LANES):
            x = p_ref[:, col + c0:col + c0 + LANES]
            if gi is not None:
                ss = _dot(x * x, bd_ref[...], HIGHEST)
                x = x * lax.rsqrt(ss * (1.0 / HD) + EPS) * gain_refs[gi][...]
            if rope:
                x = x * cos + _rope_rotate(x) * sin
            if scale != 1.0:
                x = x * scale
            if heads is None:
                o_ref[:, c0:c0 + LANES] = x.astype(o_ref.dtype)
            else:
                hw = width // heads
                per = LANES // hw

                @pl.when(_is_ctx_tile())
                def _(x=x, o_ref=o_ref, c0=c0, hw=hw, per=per):
                    for u in range(per):
                        o_ref[(c0 // LANES) * per + u] = x[:, u * hw:(u + 1) * hw].astype(o_ref.dtype)


def _norm_proj(y, gain, scale, shift, w_bf16, rope_cos, rope_sin, gains, specs):
    n = w_bf16.shape[1]
    y_pair = isinstance(y, tuple)
    bd = jnp.asarray(np.kron(np.eye(LANES // HD), np.ones((HD, HD))), F32)
    vec = pl.BlockSpec((None, 1, D_MODEL), lambda i: (_mod_row(i), 0, 0))
    rope_spec = pl.BlockSpec((TM, LANES), lambda i: (_rope_block(i), 0))
    in_specs = _pair_specs(D_MODEL) if y_pair else [pl.BlockSpec((TM, D_MODEL), lambda i: (i, 0))]
    in_specs += [pl.BlockSpec((1, D_MODEL), lambda i: (0, 0)),
                 vec, vec, pl.BlockSpec((D_MODEL, n), lambda i: (0, 0)),
                 rope_spec, rope_spec, pl.BlockSpec((LANES, LANES), lambda i: (0, 0))]
    in_specs += [pl.BlockSpec((1, LANES), lambda i: (0, 0)) for _ in gains]
    out_shape = [jax.ShapeDtypeStruct((T_ALL, n), F32)]
    out_specs = [pl.BlockSpec((TM, n), lambda i: (i, 0))]
    for (_, width, _, _, _, dtype, heads) in specs:
        if heads is None:
            out_shape.append(jax.ShapeDtypeStruct((T_ALL, width), dtype))
            out_specs.append(pl.BlockSpec((TM, width), lambda i: (i, 0)))
        else:
            out_shape.append(jax.ShapeDtypeStruct((BATCH, heads, SEQ, width // heads), dtype))
            out_specs.append(pl.BlockSpec((None, heads, SEQ, width // heads),
                                          lambda i: (jnp.minimum(i, CTX_TILES - 1), 0, 0, 0)))
    ys = list(y) if y_pair else [y]
    return pl.pallas_call(
        functools.partial(_norm_proj_kernel, specs=specs, n_gain=len(gains), y_pair=y_pair),
        out_shape=out_shape,
        grid=(N_TILES,),
        in_specs=in_specs,
        out_specs=out_specs,
        compiler_params=_params(("arbitrary",)),
        name="norm_proj",
    )(*ys, gain.reshape(1, D_MODEL), scale, shift, w_bf16, rope_cos, rope_sin, bd, *gains)


def _rope_tables():
    half = HD // 2
    freqs = 1.0 / (ROPE_THETA ** (jnp.arange(0, half, 2, dtype=F32) / half))
    t = jnp.arange(DEC_SEQ)
    rows = (t // GRID_W).astype(F32)
    cols = (t % GRID_W).astype(F32)
    ang = jnp.concatenate([rows[:, None] * freqs, cols[:, None] * freqs], axis=-1)
    cos = jnp.repeat(jnp.cos(ang), 2, axis=-1)
    sin = jnp.repeat(jnp.sin(ang), 2, axis=-1)
    cos = jnp.concatenate([jnp.tile(cos, (1, LANES // HD)), jnp.ones((TM, LANES), F32)], axis=0)
    sin = jnp.concatenate([jnp.tile(sin, (1, LANES // HD)), jnp.zeros((TM, LANES), F32)], axis=0)
    return cos, sin


def _lane_slice(ref, h, width=HD):
    per = LANES // width
    blk = ref[:, (h // per) * LANES:(h // per + 1) * LANES]
    if per == 1:
        return blk
    return blk[:, (h % per) * width:(h % per + 1) * width]


def _softmax_parts(scores):
    m = None
    for s in scores:
        ms = jnp.max(s, axis=-1, keepdims=True)
        m = ms if m is None else jnp.maximum(m, ms)
    ps = [jnp.exp(s - m) for s in scores]
    l = None
    for p in ps:
        ls = jnp.sum(p, axis=-1, keepdims=True)
        l = ls if l is None else l + ls
    return ps, l


def _attn_std_kernel(*refs, group, n_kv, has_cache, has_bias, bq):
    it = iter(refs)
    q_ref, kn_ref, vn_ref = next(it), next(it), next(it)
    kc_ref = vc_ref = b_ref = None
    if has_cache:
        kc_ref, vc_ref = next(it), next(it)
    if has_bias:
        b_ref = next(it)
    o_ref = next(it)
    outs = []
    for g in range(n_kv):
        qs = jnp.concatenate([_lane_slice(q_ref, g * group + j) for j in range(group)], axis=0)
        kn = _lane_slice(kn_ref, g)
        vn = _lane_slice(vn_ref, g)
        s_new = _dot_nt(qs, kn)
        if has_bias:
            s_new = s_new + b_ref[g]
        scores = [s_new]
        if has_cache:
            scores.append(_dot_nt(qs, kc_ref[g]))
        ps, l = _softmax_parts(scores)
        o = _dot(ps[0].astype(BF16), vn)
        if has_cache:
            o = o + _dot(ps[1].astype(BF16), vc_ref[g])
        o = o / l
        for j in range(group):
            outs.append(o[j * bq:(j + 1) * bq])
    o_ref[...] = jnp.concatenate(outs, axis=1).astype(o_ref.dtype)


def _attn_diff_kernel(*refs, has_cache):
    it = iter(refs)
    lam_ref, q_ref, kn_ref, vn_ref = next(it), next(it), next(it), next(it)
    kc_ref = vc_ref = None
    if has_cache:
        kc_ref, vc_ref = next(it), next(it)
    g_ref, o_ref = next(it), next(it)
    lam = lam_ref[0]
    post = lam_ref[1]
    outs = []
    for h in range(D_HEADS):
        pd_new, pd_c = None, None
        for j in range(2):
            f = 2 * h + j
            qs = _lane_slice(q_ref, f)
            scores = [_dot_nt(qs, _lane_slice(kn_ref, f))]
            if has_cache:
                scores.append(_dot_nt(qs, kc_ref[f]))
            ps, l = _softmax_parts(scores)
            r = 1.0 / l
            if j == 0:
                pd_new = ps[0] * r
                pd_c = ps[1] * r if has_cache else None
            else:
                r = r * lam
                pd_new = pd_new - ps[0] * r
                pd_c = pd_c - ps[1] * r if has_cache else None
        o = _dot(pd_new.astype(BF16), _lane_slice(vn_ref, h, D_VDIM))
        if has_cache:
            o = o + _dot(pd_c.astype(BF16), vc_ref[h])
        ms = jnp.mean(o * o, axis=-1, keepdims=True)
        outs.append(o * lax.rsqrt(ms + EPS) * g_ref[...] * post)
    o_ref[...] = jnp.concatenate(outs, axis=1).astype(o_ref.dtype)


def _attention(q, kn, vn, *, ctx, group=1, n_kv=1, cache=None, bias=None, diff=None, bq=256):
    if ctx:
        nb, sq, row0 = BATCH, SEQ, 0
    else:
        nb, sq, row0 = DEC_BATCH, DEC_SEQ, T_CTX
    nq = sq // bq
    qb0 = row0 // bq
    kb0 = row0 // sq
    wq, wk, wv = q.shape[1], kn.shape[1], vn.shape[1]
    in_specs = [
        pl.BlockSpec((bq, wq), lambda b, i: (qb0 + b * nq + i, 0)),
        pl.BlockSpec((sq, wk), lambda b, i: (kb0 + b, 0)),
        pl.BlockSpec((sq, wv), lambda b, i: (kb0 + b, 0)),
    ]
    args = [q, kn, vn]
    if cache is not None:
        kc, vc = cache
        in_specs += [pl.BlockSpec((None,) + kc.shape[1:], lambda b, i: (b, 0, 0, 0)),
                     pl.BlockSpec((None,) + vc.shape[1:], lambda b, i: (b, 0, 0, 0))]
        args += [kc, vc]
    if diff is None:
        if bias is not None:
            in_specs.append(pl.BlockSpec((bias.shape[0], bq, sq), lambda b, i: (0, i, 0)))
            args.append(bias)
        body = functools.partial(_attn_std_kernel, group=group, n_kv=n_kv, has_cache=cache is not None,
                                 has_bias=bias is not None, bq=bq)
    else:
        lam_vec, gain = diff
        in_specs = [pl.BlockSpec(memory_space=pltpu.SMEM)] + in_specs
        args = [lam_vec] + args
        in_specs.append(pl.BlockSpec((1, D_VDIM), lambda b, i: (0, 0)))
        args.append(gain.reshape(1, D_VDIM))
        body = functools.partial(_attn_diff_kernel, has_cache=cache is not None)
    return pl.pallas_call(
        body,
        out_shape=jax.ShapeDtypeStruct((nb * sq, 512), BF16),
        grid=(nb, nq),
        in_specs=in_specs,
        out_specs=pl.BlockSpec((bq, 512), lambda b, i: (b * nq + i, 0)),
        compiler_params=_params(("parallel", "parallel")),
        name="attention",
    )(*args)


GRID_ROWS = DEC_SEQ // GRID_W
NA_WIN_ROWS = min(NA_ROWS, GRID_ROWS)


def _na_bias_kernel(t_ref, o_ref):
    qr = pl.program_id(1)
    r0 = jnp.clip(qr - NA_WIN_ROWS // 2, 0, GRID_ROWS - NA_WIN_ROWS)
    parts = []
    for kr in range(GRID_ROWS):
        dr = jnp.clip(kr - qr + (NA_ROWS - 1), 0, 2 * NA_ROWS - 2)
        ok = (kr >= r0) & (kr < r0 + NA_WIN_ROWS)
        parts.append(jnp.where(ok, t_ref[dr], NEG_BIG))
    o_ref[...] = jnp.concatenate(parts, axis=1)


def _neighbourhood_bias(rpb):
    c = np.arange(GRID_W)
    c0 = np.clip(c - NA_COLS // 2, 0, GRID_W - NA_COLS)
    col_ok = (c[None, :] >= c0[:, None]) & (c[None, :] < c0[:, None] + NA_COLS)
    dc = np.clip(c[None, :] - c[:, None], 1 - NA_COLS, NA_COLS - 1) + (NA_COLS - 1)
    oh_c = jnp.asarray(dc[..., None] == np.arange(2 * NA_COLS - 1), F32)
    by_col = jnp.einsum("hrd,qkd->hrqk", rpb.astype(F32), oh_c, precision=HIGHEST)
    by_col = jnp.where(jnp.asarray(col_ok), by_col, NEG_BIG)
    n_dr = 2 * NA_ROWS - 1
    return pl.pallas_call(
        _na_bias_kernel,
        out_shape=jax.ShapeDtypeStruct((C_HEADS, DEC_SEQ, DEC_SEQ), F32),
        grid=(C_HEADS, GRID_ROWS),
        in_specs=[pl.BlockSpec((None, n_dr, GRID_W, GRID_W), lambda h, r: (h, 0, 0, 0))],
        out_specs=pl.BlockSpec((None, GRID_W, DEC_SEQ), lambda h, r: (h, r, 0)),
        compiler_params=_params(("parallel", "parallel")),
        name="na_bias",
    )(by_col)


def _log_sigmoid(x):
    return jnp.minimum(x, 0.0) - jnp.log1p(jnp.exp(-jnp.abs(x)))


def _mlstm_kernel(q_ref, k_ref, v_ref, g_ref, gb_ref, c0_ref, n0_ref, m0_ref, h_ref, c_ref, n_ref, m_ref, hb_ref,
                  *, seq):
    L = B_CHUNK
    nc = seq // L
    row = lax.broadcasted_iota(jnp.int32, (L, L), 0)
    col = lax.broadcasted_iota(jnp.int32, (L, L), 1)
    keeps = (col <= row, col >= row)
    k_scale = B_DK ** -0.5
    c_ref[...] = c0_ref[...]
    n_ref[...] = n0_ref[...]
    m_ref[...] = m0_ref[...]

    def step(j):
        for d in range(2):
            keep = keeps[d]
            c = j if d == 0 else nc - 1 - j
            off = c * L if isinstance(c, int) else pl.multiple_of(c * L, L)
            gates = g_ref[pl.ds(off, L), :] + gb_ref[...]
            cum = _dot(keep.astype(F32), _log_sigmoid(gates), HIGHEST)
            cum_t = cum.T
            gates_t = gates.T
            out_ref = h_ref if d == 0 else hb_ref
            for h in range(B_HEADS):
                ci = (2 * d) * B_HEADS + h
                cf = (2 * d + 1) * B_HEADS + h
                hs = slice(h * B_DK, (h + 1) * B_DK)
                C = c_ref[d, h]
                n = n_ref[d, h]
                m = m_ref[d, h][:, 0:1]
                qc = q_ref[pl.ds(off, L), hs]
                kc = k_ref[pl.ds(off, L), hs] * k_scale
                vc = v_ref[pl.ds(off, L), hs]
                b_col = cum[:, cf:cf + 1]
                i_col = gates[:, ci:ci + 1]
                b_row = cum_t[cf:cf + 1, :]
                i_row = gates_t[ci:ci + 1, :]
                dlog = jnp.where(keep, b_col - b_row + i_row, -jnp.inf)
                inter = b_col + m
                m_t = jnp.maximum(inter, jnp.max(dlog, axis=-1, keepdims=True))
                w_intra = jnp.exp(dlog - m_t)
                w_inter = jnp.exp(inter - m_t)
                qb = qc.astype(BF16)
                vb = vc.astype(BF16)
                qk = _dot_nt(qb, kc.astype(BF16)) * w_intra
                num = _dot(qk.astype(BF16), vb) + w_inter * _dot(qb, C.astype(BF16))
                den = jnp.sum(qk, axis=-1, keepdims=True) + w_inter * jnp.sum(qc * n, axis=-1, keepdims=True)
                out_ref[pl.ds(off, L), hs] = num / jnp.maximum(jnp.abs(den), jnp.exp(-m_t))
                b_last = b_col[L - 1:L, :] if d == 0 else b_col[0:1, :]
                end_col = b_last - b_col + i_col
                m_new = jnp.maximum(b_last + m, jnp.max(end_col, axis=0, keepdims=True))
                w_end = jnp.exp(end_col - m_new)
                decay = jnp.exp(b_last + m - m_new)
                kw = kc * w_end
                c_ref[d, h] = decay * C + _dot_tn(kw.astype(BF16), vb)
                n_ref[d, h] = decay * n + jnp.sum(kw, axis=0, keepdims=True)
                m_ref[d, h] = jnp.broadcast_to(m_new, (1, LANES))

    if nc <= 2:
        for j in range(nc):
            step(j)
    else:
        def body(j, carry):
            step(j)
            return carry

        lax.fori_loop(0, nc, body, 0)
    h_ref[...] = h_ref[...] + hb_ref[...]


def _mlstm(p, gate_bias, c0, n0, m0, *, ctx):
    if ctx:
        nb, seq, blk0 = BATCH, SEQ, 0
    else:
        nb, seq, blk0 = DEC_BATCH, DEC_SEQ, T_CTX // DEC_SEQ
    w = B_HEADS * B_DK

    def cols(c0_, width):
        return pl.BlockSpec((seq, width), lambda b: (blk0 + b, c0_ // width))

    gb = jnp.zeros((1, LANES), F32).at[0, :4 * B_HEADS].set(gate_bias.reshape(-1).astype(F32))
    st = lambda shape: pl.BlockSpec((None,) + shape, lambda b: (b, 0, 0, 0, 0))
    return pl.pallas_call(
        functools.partial(_mlstm_kernel, seq=seq),
        out_shape=[
            jax.ShapeDtypeStruct((nb * seq, w), F32),
            jax.ShapeDtypeStruct((nb, 2, B_HEADS, B_DK, B_DV), F32),
            jax.ShapeDtypeStruct((nb, 2, B_HEADS, 1, B_DK), F32),
            jax.ShapeDtypeStruct((nb, 2, B_HEADS, 1, LANES), F32),
        ],
        grid=(nb,),
        in_specs=[
            cols(EV_BQ, w), cols(EV_BK, w), cols(EV_BV, w), cols(EV_BG, LANES),
            pl.BlockSpec((1, LANES), lambda b: (0, 0)),
            st((2, B_HEADS, B_DK, B_DV)), st((2, B_HEADS, 1, B_DK)), st((2, B_HEADS, 1, LANES)),
        ],
        out_specs=[
            pl.BlockSpec((seq, w), lambda b: (b, 0)),
            st((2, B_HEADS, B_DK, B_DV)), st((2, B_HEADS, 1, B_DK)), st((2, B_HEADS, 1, LANES)),
        ],
        scratch_shapes=[pltpu.VMEM((seq, w), F32)],
        compiler_params=_params(("parallel",)),
        name="mlstm",
    )(p, p, p, p, gb, c0, n0, m0)


def _merge_value(refs, even, y_pair):
    a_ctx, a_lat, b_ctx, b_lat = refs[:4]
    rest = refs[4:]
    if even:
        bo_ref, ng_ref = rest[:2]
        rest = rest[2:]
        parts = [_pair_value(a_ctx, a_lat)]
        for h in range(B_HEADS):
            hs = slice(h * B_DV, (h + 1) * B_DV)
            x = _pair_value(b_ctx, b_lat, hs)
            ms = jnp.mean(x * x, axis=-1, keepdims=True)
            xn = x * lax.rsqrt(ms + EPS) * ng_ref[:, hs]
            parts.append((jax.nn.sigmoid(bo_ref[:, hs]) * xn).astype(BF16))
    else:
        parts = [_pair_value(a_ctx, a_lat), _pair_value(b_ctx, b_lat)]
    w_ref = rest[0]
    y = _pair_value(rest[1], rest[2]) if y_pair else rest[1][...]
    g_ref = rest[-1]
    cat = jnp.concatenate(parts, axis=1)
    return y + g_ref[...] * _dot(cat, w_ref[...])


def _merge_specs(y, gate, w_bf16, a, b, p, norm_gain):
    in_specs = _pair_specs(512) + _pair_specs(512)
    args = [*a, *b]
    if p is not None:
        in_specs += [pl.BlockSpec((TM, 512), lambda i: (i, EV_BO // 512)), pl.BlockSpec((1, 512), lambda i: (0, 0))]
        args += [p, norm_gain.reshape(1, 512)]
    in_specs.append(pl.BlockSpec((D_MODEL, D_MODEL), lambda i: (0, 0)))
    args.append(w_bf16)
    if isinstance(y, tuple):
        in_specs += _pair_specs(D_MODEL)
        args += list(y)
    else:
        in_specs.append(pl.BlockSpec((TM, D_MODEL), lambda i: (i, 0)))
        args.append(y)
    in_specs.append(pl.BlockSpec((None, 1, D_MODEL), lambda i: (_mod_row(i), 0, 0)))
    args.append(gate)
    return in_specs, args


SLAB = D_MODEL // LANES


def _load_slabs(ref, rows):
    return jnp.concatenate([ref[pl.ds(c, rows, stride=SLAB), :] for c in range(SLAB)], axis=1)


def _store_slabs(ref, x):
    for c in range(SLAB):
        ref[pl.ds(c, x.shape[0], stride=SLAB), :] = x[:, c * LANES:(c + 1) * LANES]


def _slab(ref, idx):
    return ref.at[pl.ds(pl.multiple_of(idx * SLAB, SLAB), SLAB)]


def _merge_router_kernel(*refs, even, y_pair):
    n_merge = 4 + (2 if even else 0) + 1 + (2 if y_pair else 1) + 1
    merge_refs = refs[:n_merge]
    (g_ref, sc_ref, sh_ref, whi_ref, wlo_ref, b_ref,
     y_out_ref, h_ref, ti_ref, tp_ref, rk_ref, cnt_ref, base_ref) = refs[n_merge:]

    @pl.when(pl.program_id(0) == 0)
    def _():
        base_ref[...] = jnp.zeros(base_ref.shape, F32)

    y = _merge_value(merge_refs, even, y_pair)
    y_out_ref[...] = y
    h = _norm_mod(y, g_ref[...], sc_ref[...], sh_ref[...])
    _store_slabs(h_ref, h)
    h_hi = h.astype(BF16)
    h_lo = (h - h_hi.astype(F32)).astype(BF16)
    logits = (_dot(h_hi, whi_ref[...]) + (_dot(h_hi, wlo_ref[...]) + _dot(h_lo, whi_ref[...]))
              + b_ref[...])
    lane = lax.broadcasted_iota(jnp.int32, logits.shape, 1)
    lane_f = lane.astype(F32)
    vals, idxs = [], []
    for _ in range(TOP_K):
        mx = jnp.max(logits, axis=-1, keepdims=True)
        ix = jnp.min(jnp.where(logits == mx, lane_f, float(LANES)), axis=-1, keepdims=True)
        vals.append(mx)
        idxs.append(ix)
        logits = jnp.where(lane_f == ix, -jnp.inf, logits)
    es = [jnp.exp(v - vals[0]) for v in vals]
    tot = es[0] + es[1] + es[2] + es[3]
    ti = jnp.zeros(logits.shape, F32)
    tp = jnp.zeros(logits.shape, F32)
    for k in range(TOP_K):
        ti = jnp.where(lane == k, idxs[k], ti)
        tp = jnp.where(lane == k, es[k] / tot, tp)
    ti_ref[...] = ti.T[0:8, :].astype(jnp.int32)
    tp_ref[...] = tp
    onehots = [(lane_f == ix).astype(F32) for ix in idxs]
    cnt = onehots[0] + onehots[1] + onehots[2] + onehots[3]
    row = lax.broadcasted_iota(jnp.int32, (TM, TM), 0)
    col = lax.broadcasted_iota(jnp.int32, (TM, TM), 1)
    before = _dot((col < row).astype(BF16), cnt.astype(BF16)) + base_ref[...]
    rk = jnp.zeros(logits.shape, F32)
    for k in range(TOP_K):
        rk = jnp.where(lane == k, jnp.sum(onehots[k] * before, axis=-1, keepdims=True), rk)
    rk_ref[...] = rk.T[0:8, :].astype(jnp.int32)
    base_ref[...] = base_ref[...] + jnp.sum(cnt, axis=0, keepdims=True)
    cnt_ref[...] = base_ref[...]


def _merge_router(y, gate, w_out_bf16, a, b, gain, scale, shift, rw, rb, *, p=None, norm_gain=None):
    merge_in_specs, merge_args = _merge_specs(y, gate, w_out_bf16, a, b, p, norm_gain)
    vec = pl.BlockSpec((None, 1, D_MODEL), lambda i: (_mod_row(i), 0, 0))
    rw_p = jnp.zeros((D_MODEL, LANES), F32).at[:, :N_EXPERTS].set(rw)
    rb_p = jnp.full((1, LANES), NEG_BIG, F32).at[0, :N_EXPERTS].set(rb)
    rw_hi = rw_p.astype(BF16)
    tile = lambda w: pl.BlockSpec((TM, w), lambda i: (i, 0))
    by_choice = pl.BlockSpec((None, 8, TM), lambda i: (i, 0, 0))
    return pl.pallas_call(
        functools.partial(_merge_router_kernel, even=p is not None, y_pair=isinstance(y, tuple)),
        out_shape=[jax.ShapeDtypeStruct((T_ALL, D_MODEL), F32),
                   jax.ShapeDtypeStruct((T_ALL * SLAB, LANES), F32),
                   jax.ShapeDtypeStruct((N_TILES, 8, TM), jnp.int32),
                   jax.ShapeDtypeStruct((T_ALL, LANES), F32),
                   jax.ShapeDtypeStruct((N_TILES, 8, TM), jnp.int32),
                   jax.ShapeDtypeStruct((1, LANES), F32)],
        grid=(N_TILES,),
        in_specs=merge_in_specs + [
            pl.BlockSpec((1, D_MODEL), lambda i: (0, 0)), vec, vec,
            pl.BlockSpec((D_MODEL, LANES), lambda i: (0, 0)), pl.BlockSpec((D_MODEL, LANES), lambda i: (0, 0)),
            pl.BlockSpec((1, LANES), lambda i: (0, 0))],
        out_specs=[tile(D_MODEL), pl.BlockSpec((TM * SLAB, LANES), lambda i: (i, 0)), by_choice, tile(LANES),
                   by_choice, pl.BlockSpec((1, LANES), lambda i: (0, 0))],
        scratch_shapes=[pltpu.VMEM((1, LANES), F32)],
        compiler_params=_params(("arbitrary",)),
        name="merge_router",
    )(*merge_args, gain.reshape(1, D_MODEL), scale, shift, rw_hi, (rw_p - rw_hi.astype(F32)).astype(BF16), rb_p)


def _route_plan(top_i, rank, counts):
    experts = jnp.arange(N_EXPERTS, dtype=jnp.int32)
    padded = ((counts + MOE_TM - 1) // MOE_TM) * MOE_TM
    seg_end = jnp.cumsum(padded)
    seg_start = seg_end - padded
    pos = rank
    for e in range(N_EXPERTS - 1):
        pos = pos + jnp.where(top_i > e, padded[e], 0)
    n_active = seg_end[-1] // MOE_TM
    fill = jnp.concatenate([seg_start + counts, padded - counts, n_active[None]]).astype(jnp.int32)
    tile_start = jnp.arange(MOE_TILES, dtype=jnp.int32) * MOE_TM
    tile_expert = jnp.sum((seg_end[None, :] <= tile_start[:, None]).astype(jnp.int32), axis=1)
    last = jnp.sum((seg_end <= (n_active - 1) * MOE_TM).astype(jnp.int32))
    tile_expert = jnp.minimum(jnp.where(tile_start < seg_end[-1], tile_expert, last), N_EXPERTS - 1)
    owns = (padded > 0).astype(jnp.int32)
    run_of_expert = jnp.cumsum(owns) - 1
    run_expert = jnp.sum(jnp.where((run_of_expert[None, :] == experts[:, None]) & (owns[None, :] > 0),
                                   experts[None, :], 0), axis=1)
    runs = jnp.concatenate([run_expert, jnp.sum(owns)[None]]).astype(jnp.int32)
    tile_run = jnp.sum(jnp.where(tile_expert[:, None] == experts[None, :], run_of_expert[None, :], 0), axis=1)
    experts_plan = (tile_expert.astype(jnp.int32), n_active.reshape(1).astype(jnp.int32),
                    tile_run.astype(jnp.int32), runs)
    return pos.astype(jnp.int32), fill, experts_plan


DMA_UNROLL = 4
DMA_QUEUES = 2


def _wait_slabs(ref, n_slabs, sem):
    view = ref.at[pl.ds(0, n_slabs * SLAB)]
    pltpu.make_async_copy(view, view, sem).wait()


def _dispatch_kernel(pos_ref, fill_ref, h_ref, xs_ref, inv_ref, sem):
    i = pl.program_id(0)
    base = i * TM

    def issue(j, carry):
        for u in range(DMA_UNROLL):
            t = j * DMA_UNROLL + u
            for k in range(TOP_K):
                a = base * TOP_K + k * TM + t
                slot = pos_ref[a]
                inv_ref[slot] = a
                pltpu.make_async_copy(_slab(h_ref, t), _slab(xs_ref, slot), sem).start(priority=k % DMA_QUEUES)
        return carry

    lax.fori_loop(0, TM // DMA_UNROLL, issue, 0)

    @pl.when(i == 0)
    def _():
        def per_expert(e, total):
            start = fill_ref[e]
            n = fill_ref[N_EXPERTS + e]

            def one(r, carry):
                slot = start + r
                inv_ref[slot] = N_ASSIGN + slot % MOE_TM
                pltpu.make_async_copy(_slab(h_ref, 0), _slab(xs_ref, slot), sem).start()
                return carry

            lax.fori_loop(0, n, one, 0)
            return total + n

        total = lax.fori_loop(0, N_EXPERTS, per_expert, 0)

        n_active = fill_ref[2 * N_EXPERTS]

        def unused_tile(ti, carry):
            dst = xs_ref.at[pl.ds(pl.multiple_of(ti * (MOE_TM * SLAB), MOE_TM * SLAB), MOE_TM * SLAB)]
            pltpu.make_async_copy(h_ref, dst, sem).start()
            return carry

        lax.fori_loop(n_active, MOE_TILES, unused_tile, 0)

        def unused_slot(slot, carry):
            inv_ref[slot] = N_ASSIGN
            return carry

        lax.fori_loop(n_active * MOE_TM, MOE_ROWS, unused_slot, 0)
        total = total + (MOE_TILES - n_active) * MOE_TM

        @pl.when(total > 0)
        def _():
            _wait_slabs(xs_ref, total, sem)

    _wait_slabs(xs_ref, TM * TOP_K, sem)


def _dispatch(h_slabs, pos, fill):
    grid_spec = pltpu.PrefetchScalarGridSpec(
        num_scalar_prefetch=2,
        grid=(N_TILES,),
        in_specs=[pl.BlockSpec((TM * SLAB, LANES), lambda i, pos, fill: (i, 0))],
        out_specs=[pl.BlockSpec(memory_space=pl.ANY), pl.BlockSpec(memory_space=pltpu.SMEM)],
        scratch_shapes=[pltpu.SemaphoreType.DMA],
    )
    return pl.pallas_call(
        _dispatch_kernel,
        out_shape=[jax.ShapeDtypeStruct((MOE_ROWS * SLAB, LANES), F32),
                   jax.ShapeDtypeStruct((MOE_ROWS,), jnp.int32)],
        grid_spec=grid_spec,
        compiler_params=_params(("arbitrary",)),
        name="moe_dispatch",
    )(pos, fill, h_slabs)


MOE_HALF = MOE_TM // 2


def _moe_kernel(te_ref, na_ref, ts_ref, ex_ref, inv_ref, x_ref, wgu_hbm, bgu_ref, wd_hbm, bd_ref, out_hbm,
                wgu_f32, wd_f32, wgu_bf, wd_bf, obuf0, obuf1, sems, osems, *, layer):
    i = pl.program_id(0)
    s = ts_ref[i]
    first = (i == 0) | (s != ts_ref[jnp.maximum(i - 1, 0)])
    n_active = na_ref[0]

    def start_rows(buf, half, tile):
        for r in range(MOE_HALF):
            dst = _slab(out_hbm, inv_ref[tile * MOE_TM + half * MOE_HALF + r])
            pltpu.make_async_copy(buf.at[pl.ds(r * SLAB, SLAB)], dst, osems.at[half]).start(
                priority=r % DMA_QUEUES)

    def wait_rows(half):
        _wait_slabs(out_hbm, MOE_HALF, osems.at[half])

    def ffn_tile(deferred):
        if deferred:
            start_rows(obuf1, 1, i - 1)
        x = _load_slabs(x_ref, MOE_TM).astype(BF16)
        gu = _dot(x, wgu_bf[...]) + bgu_ref[...]
        gate = jnp.minimum(gu[:, :D_FF], SWIGLU_LIMIT)
        up = jnp.clip(gu[:, D_FF:], -SWIGLU_LIMIT, SWIGLU_LIMIT)
        act = ((up + 1.0) * gate * jax.nn.sigmoid(SWIGLU_ALPHA * gate)).astype(BF16)
        o0 = _dot(act[:MOE_HALF], wd_bf[...]) + bd_ref[...]
        if deferred:
            wait_rows(0)
        _store_slabs(obuf0, o0)
        start_rows(obuf0, 0, i)
        o1 = _dot(act[MOE_HALF:], wd_bf[...]) + bd_ref[...]
        if deferred:
            wait_rows(1)
        _store_slabs(obuf1, o1)

    def weight_copies(slot):
        e = ex_ref[slot]
        b = slot % 2
        return (pltpu.make_async_copy(wgu_hbm.at[layer, e], wgu_f32.at[b], sems.at[0, b]),
                pltpu.make_async_copy(wd_hbm.at[layer, e], wd_f32.at[b], sems.at[1, b]))

    @pl.when(i == 0)
    def _():
        for cp in weight_copies(0):
            cp.start()

    @pl.when(first)
    def _():
        for cp in weight_copies(s):
            cp.wait()

        @pl.when(s + 1 < ex_ref[N_EXPERTS])
        def _():
            for cp in weight_copies(s + 1):
                cp.start()

        b = s % 2
        wgu_bf[...] = wgu_f32[b].astype(BF16)
        wd_bf[...] = wd_f32[b].astype(BF16)

    @pl.when(i == 0)
    def _():
        ffn_tile(False)

    @pl.when((i > 0) & (i < n_active))
    def _():
        ffn_tile(True)

    @pl.when(i == n_active)
    def _():
        wait_rows(0)
        start_rows(obuf1, 1, i - 1)
        wait_rows(1)
        for half, buf in enumerate((obuf0, obuf1)):
            dst = out_hbm.at[pl.ds((N_ASSIGN + half * MOE_HALF) * SLAB, MOE_HALF * SLAB)]
            pltpu.make_async_copy(buf, dst, osems.at[half]).start()
        wait_rows(0)
        wait_rows(1)


def _moe_experts(layer, xs, plan, inv, w_gu, b_gu, w_down, b_down):
    const = lambda i, te, na, ts, ex, inv: (layer, te[i], 0, 0)
    grid_spec = pltpu.PrefetchScalarGridSpec(
        num_scalar_prefetch=5,
        grid=(MOE_TILES,),
        in_specs=[
            pl.BlockSpec((MOE_TM * SLAB, LANES), lambda i, te, na, ts, ex, inv: (jnp.minimum(i, na[0] - 1), 0)),
            pl.BlockSpec(memory_space=pl.ANY),
            pl.BlockSpec((None, None, 1, 2 * D_FF), const),
            pl.BlockSpec(memory_space=pl.ANY),
            pl.BlockSpec((None, None, 1, D_MODEL), const),
        ],
        out_specs=pl.BlockSpec(memory_space=pl.ANY),
        scratch_shapes=[pltpu.VMEM((2, D_MODEL, 2 * D_FF), F32), pltpu.VMEM((2, D_FF, D_MODEL), F32),
                        pltpu.VMEM((D_MODEL, 2 * D_FF), BF16), pltpu.VMEM((D_FF, D_MODEL), BF16),
                        pltpu.VMEM((MOE_HALF * SLAB, LANES), F32), pltpu.VMEM((MOE_HALF * SLAB, LANES), F32),
                        pltpu.SemaphoreType.DMA((2, 2)), pltpu.SemaphoreType.DMA((2,))],
    )
    return pl.pallas_call(
        functools.partial(_moe_kernel, layer=layer),
        out_shape=jax.ShapeDtypeStruct(((N_ASSIGN + MOE_TM) * SLAB, LANES), F32),
        grid_spec=grid_spec,
        compiler_params=_params(("arbitrary",)),
        name="moe_experts",
    )(*plan, inv, xs, w_gu, b_gu.reshape(DEPTH, N_EXPERTS, 1, 2 * D_FF), w_down,
      b_down.reshape(DEPTH, N_EXPERTS, 1, D_MODEL))


def _combine_kernel(out_ref, y_ref, tp_ref, g_ref, *rest, final):
    if final:
        fg_ref, n_ctx_ref, n_lat_ref = rest
    else:
        (o_ref,) = rest

    tp = tp_ref[...]
    ss = jnp.zeros((TM, 1), F32)
    chunks = []
    for c in range(SLAB):
        cs = slice(c * LANES, (c + 1) * LANES)
        acc = tp[:, 0:1] * out_ref[pl.ds(c, TM, stride=SLAB), :]
        for k in range(1, TOP_K):
            acc = acc + tp[:, k:k + 1] * out_ref[pl.ds(k * TM * SLAB + c, TM, stride=SLAB), :]
        yc = y_ref[:, cs] + g_ref[:, cs] * acc
        if final:
            ss = ss + jnp.sum(yc * yc, axis=-1, keepdims=True)
            chunks.append(yc)
        else:
            o_ref[:, cs] = yc
    if final:
        inv = lax.rsqrt(ss * (1.0 / D_MODEL) + EPS)
        normed = jnp.concatenate([chunks[c] * inv * fg_ref[:, c * LANES:(c + 1) * LANES] for c in range(SLAB)],
                                 axis=1)

        @pl.when(_is_ctx_tile())
        def _():
            n_ctx_ref[...] = normed

        @pl.when(jnp.logical_not(_is_ctx_tile()))
        def _():
            n_lat_ref[...] = normed


def _combine(y, out_slabs, top_p, gate, final_gain=None):
    final = final_gain is not None
    tile = pl.BlockSpec((TM, D_MODEL), lambda i: (i, 0))
    in_specs = [pl.BlockSpec((TOP_K * TM * SLAB, LANES), lambda i: (i, 0)), tile,
                pl.BlockSpec((TM, LANES), lambda i: (i, 0)),
                pl.BlockSpec((None, 1, D_MODEL), lambda i: (_mod_row(i), 0, 0))]
    args = [out_slabs, y, top_p, gate]
    if final:
        in_specs.append(pl.BlockSpec((1, D_MODEL), lambda i: (0, 0)))
        args.append(final_gain.reshape(1, D_MODEL))
        out_shape = [jax.ShapeDtypeStruct((T_CTX, D_MODEL), F32), jax.ShapeDtypeStruct((T_LAT, D_MODEL), F32)]
        out_specs = [pl.BlockSpec((TM, D_MODEL), lambda i: (jnp.minimum(i, CTX_TILES - 1), 0)),
                     pl.BlockSpec((TM, D_MODEL), lambda i: (jnp.maximum(i - CTX_TILES, 0), 0))]
    else:
        out_shape = [jax.ShapeDtypeStruct((T_ALL, D_MODEL), F32)]
        out_specs = [tile]
    return pl.pallas_call(
        functools.partial(_combine_kernel, final=final),
        out_shape=out_shape,
        grid=(N_TILES,),
        in_specs=in_specs,
        out_specs=out_specs,
        compiler_params=_params(("arbitrary",)),
        name="moe_combine",
    )(*args)


def _moe_layer(layer, y, routed, gate, w_gu, b_gu, w_down, b_down, final_gain=None):
    h_slabs, top_i, top_p, rank, counts = routed
    pos, fill, experts_plan = _route_plan(top_i[:, :TOP_K].reshape(-1), rank[:, :TOP_K].reshape(-1),
                                          counts[0, :N_EXPERTS].astype(jnp.int32))
    xs, inv = _dispatch(h_slabs, pos, fill)
    out = _moe_experts(layer, xs, experts_plan, inv, w_gu, b_gu, w_down, b_down)
    return _combine(y, out, top_p, gate, final_gain)


def kernel(x_prompt, x_sample, c, cache_a_k, cache_a_v, state_b_C, state_b_n, state_b_m, cache_c_k, cache_c_v, cache_d_k, cache_d_v, c_ctx, w_mod, b_mod, norm1_g, norm2_g, w_in_even, w_out_even, a_q_gain, a_k_gain, b_gate_bias, b_norm_gain, w_in_odd, w_out_odd, c_rpb, d_lambda, d_norm_gain, router_w, router_b, expert_w_gu, expert_b_gu, expert_w_down, expert_b_down, final_norm_g):
    y = (x_prompt.reshape(T_CTX, D_MODEL), x_sample.reshape(T_LAT, D_MODEL))
    cond = jnp.zeros((MOD_ROWS, D_MODEL), F32).at[0].set(c_ctx).at[1:1 + DEC_BATCH].set(c)
    mod = _modulation(cond, w_mod, b_mod).reshape(DEPTH, MOD_ROWS, 6, 1, D_MODEL)
    rope_cos, rope_sin = _rope_tables()
    scale = HD ** -0.5
    outs = {}

    for layer in range(DEPTH):
        sh1, sc1, g1, sh2, sc2, g2 = (mod[layer, :, k] for k in range(6))
        j = layer // 2
        if layer % 2 == 0:
            w = w_in_even[j]
            sizes = np.cumsum([0, 512, 128, 128, 512, 512, 512, 512, 16])
            aq, ak, av, bq, bk, bv, bo, bg = (w[:, sizes[k]:sizes[k + 1]] for k in range(8))
            w_in = jnp.concatenate([aq, bo, bq, bk, bv, ak, av, bg, jnp.zeros((D_MODEL, EV_N - EV_BG - 16), F32)],
                                   axis=1).astype(BF16)
            qg = jnp.tile(a_q_gain[j], LANES // HD).reshape(1, LANES)
            kg = jnp.tile(a_k_gain[j], LANES // HD).reshape(1, LANES)
            specs = ((EV_AQ, 512, 0, True, scale, BF16, None), (EV_AK, 128, 1, True, 1.0, BF16, None),
                     (EV_AV, 128, None, False, 1.0, BF16, None),
                     (EV_AK, 128, 1, False, 1.0, F32, A_KV), (EV_AV, 128, None, False, 1.0, F32, A_KV))
            p, qa, ka, va, new_ak, new_av = _norm_proj(y, norm1_g[layer], sc1, sh1, w_in, rope_cos, rope_sin,
                                                       [qg, kg], specs)
            oa_ctx = _attention(qa, ka, va, ctx=True, group=A_HEADS // A_KV, n_kv=A_KV)
            cache = (cache_a_k[:, j].astype(BF16), cache_a_v[:, j].astype(BF16))
            oa_lat = _attention(qa, ka, va, ctx=False, group=A_HEADS // A_KV, n_kv=A_KV, cache=cache)
            zc = jnp.zeros((BATCH, 2, B_HEADS, B_DK, B_DV), F32)
            zn = jnp.zeros((BATCH, 2, B_HEADS, 1, B_DK), F32)
            zm = jnp.zeros((BATCH, 2, B_HEADS, 1, LANES), F32)
            hb_ctx, bC, bn, bm = _mlstm(p, b_gate_bias[j], zc, zn, zm, ctx=True)
            m0 = jnp.broadcast_to(state_b_m[:, j][..., None, None], (DEC_BATCH, 2, B_HEADS, 1, LANES))
            hb_lat, _, _, _ = _mlstm(p, b_gate_bias[j], state_b_C[:, j], state_b_n[:, j][:, :, :, None, :], m0,
                                     ctx=False)
            y, *routed = _merge_router(y, g1, w_out_even[j].astype(BF16), (oa_ctx, oa_lat), (hb_ctx, hb_lat),
                                       norm2_g[layer], sc2, sh2, router_w[layer], router_b[layer],
                                       p=p, norm_gain=b_norm_gain[j])
            outs.setdefault("a_k", []).append(new_ak)
            outs.setdefault("a_v", []).append(new_av)
            outs.setdefault("b_C", []).append(bC)
            outs.setdefault("b_n", []).append(bn[:, :, :, 0, :])
            outs.setdefault("b_m", []).append(bm[:, :, :, 0, 0])
        else:
            specs = ((0, 512, None, False, scale, BF16, None), (512, 512, None, False, 1.0, BF16, None),
                     (1024, 512, None, False, 1.0, BF16, None), (1536, 512, None, True, scale, BF16, None),
                     (2048, 512, None, True, 1.0, BF16, None), (2560, 512, None, False, 1.0, BF16, None),
                     (512, 512, None, False, 1.0, F32, C_HEADS), (1024, 512, None, False, 1.0, F32, C_HEADS),
                     (2048, 512, None, False, 1.0, F32, 2 * D_HEADS), (2560, 512, None, False, 1.0, F32, D_HEADS))
            p, qc, kc, vc, qd, kd, vd, new_ck, new_cv, new_dk, new_dv = _norm_proj(
                y, norm1_g[layer], sc1, sh1, w_in_odd[j].astype(BF16), rope_cos, rope_sin, [], specs)
            lam_init = 0.8 - 0.6 * math.exp(-0.3 * layer)
            lp = d_lambda[j].astype(F32)
            lam = jnp.exp(jnp.sum(lp[0] * lp[1])) - jnp.exp(jnp.sum(lp[2] * lp[3])) + lam_init
            lam_vec = jnp.stack([lam, jnp.asarray(1.0 - lam_init, F32)]).astype(F32)
            diff = (lam_vec, d_norm_gain[j])
            oc_ctx = _attention(qc, kc, vc, ctx=True, n_kv=C_HEADS)
            od_ctx = _attention(qd, kd, vd, ctx=True, diff=diff)
            bias = _neighbourhood_bias(c_rpb[j])
            oc_lat = _attention(qc, kc, vc, ctx=False, n_kv=C_HEADS, bias=bias,
                                cache=(cache_c_k[:, j].astype(BF16), cache_c_v[:, j].astype(BF16)))
            kd_cache = cache_d_k[:, j].reshape(DEC_BATCH, 2 * D_HEADS, PAST_LEN, HD).astype(BF16)
            od_lat = _attention(qd, kd, vd, ctx=False, diff=diff, cache=(kd_cache, cache_d_v[:, j].astype(BF16)))
            y, *routed = _merge_router(y, g1, w_out_odd[j].astype(BF16), (oc_ctx, oc_lat), (od_ctx, od_lat),
                                       norm2_g[layer], sc2, sh2, router_w[layer], router_b[layer])
            outs.setdefault("c_k", []).append(new_ck)
            outs.setdefault("c_v", []).append(new_cv)
            outs.setdefault("d_k", []).append(new_dk.reshape(BATCH, D_HEADS, 2, SEQ, HD))
            outs.setdefault("d_v", []).append(new_dv)
        res = _moe_layer(layer, y, routed, g2, expert_w_gu, expert_b_gu, expert_w_down, expert_b_down,
                         final_gain=final_norm_g if layer == DEPTH - 1 else None)
        y = res[0]
    y_prompt, y_sample = res
    stack = lambda k: jnp.stack(outs[k], axis=1)
    return (y_prompt.reshape(BATCH, SEQ, D_MODEL), y_sample.reshape(DEC_BATCH, DEC_SEQ, D_MODEL),
            stack("a_k"), stack("a_v"), stack("b_C"), stack("b_n"), stack("b_m"),
            stack("c_k"), stack("c_v"), stack("d_k"), stack("d_v"))
```

```python
import functools
import math

import numpy as np
import jax
import jax.numpy as jnp
from jax import lax
from jax.experimental import pallas as pl
from jax.experimental.pallas import tpu as pltpu

D_MODEL = 1024
BATCH = 32
SEQ = 256
DEPTH = 2
DEC_BATCH = 8
DEC_SEQ = 1024
PAST_LEN = 512
GRID_W = 64
HD = 64
A_HEADS = 8
A_KV = 2
B_HEADS = 4
B_DK = 128
B_DV = 128
B_CHUNK = 128
C_HEADS = 8
NA_ROWS = 8
NA_COLS = 16
D_HEADS = 4
D_VDIM = 2 * HD
N_EXPERTS = 32
TOP_K = 4
D_FF = 1024
SWIGLU_LIMIT = 7.0
SWIGLU_ALPHA = 1.702
ROPE_THETA = 10000.0
EPS = 1e-6

F32 = jnp.float32
BF16 = jnp.bfloat16
HIGHEST = lax.Precision.HIGHEST

T_CTX = BATCH * SEQ
T_LAT = DEC_BATCH * DEC_SEQ
T_ALL = T_CTX + T_LAT
TM = 256
CTX_TILES = T_CTX // TM
LAT_TILES_PER_BATCH = DEC_SEQ // TM
N_TILES = T_ALL // TM
MOD_ROWS = 16
LANES = 128
NEG_BIG = -1e30
MOE_TM = 256
N_ASSIGN = T_ALL * TOP_K
MOE_ROWS = N_ASSIGN + N_EXPERTS * MOE_TM
MOE_TILES = MOE_ROWS // MOE_TM
VMEM_LIMIT = 56 * 1024 * 1024

EV_AQ, EV_BO, EV_BQ, EV_BK, EV_BV, EV_AK, EV_AV, EV_BG = 0, 512, 1024, 1536, 2048, 2560, 2688, 2816
EV_N = 2944
OD_N = 3072


def _params(sem, vmem=VMEM_LIMIT):
    return pltpu.CompilerParams(dimension_semantics=sem, vmem_limit_bytes=vmem)


def _mod_row(i):
    return jnp.where(i < CTX_TILES, 0, 1 + (i - CTX_TILES) // LAT_TILES_PER_BATCH)


def _rope_block(i):
    return jnp.where(i < CTX_TILES, LAT_TILES_PER_BATCH, (i - CTX_TILES) % LAT_TILES_PER_BATCH)


def _dot(a, b, precision=None):
    return jnp.dot(a, b, preferred_element_type=F32, precision=precision)


def _dot_nt(a, b):
    return lax.dot_general(a, b, (((1,), (1,)), ((), ())), preferred_element_type=F32)


def _dot_tn(a, b):
    return lax.dot_general(a, b, (((0,), (0,)), ((), ())), preferred_element_type=F32)


def _modulation_kernel(c_ref, w_ref, b_ref, o_ref):
    c = c_ref[...]
    s = c * jax.nn.sigmoid(c)
    o_ref[...] = _dot(s, w_ref[...], HIGHEST) + b_ref[...]


def _modulation(cond, w_mod, b_mod):
    tn = 1536
    return pl.pallas_call(
        _modulation_kernel,
        out_shape=jax.ShapeDtypeStruct((DEPTH, MOD_ROWS, 6 * D_MODEL), F32),
        grid=(DEPTH, 6 * D_MODEL // tn),
        in_specs=[
            pl.BlockSpec((MOD_ROWS, D_MODEL), lambda l, j: (0, 0)),
            pl.BlockSpec((None, D_MODEL, tn), lambda l, j: (l, 0, j)),
            pl.BlockSpec((None, 1, tn), lambda l, j: (l, 0, j)),
        ],
        out_specs=pl.BlockSpec((None, MOD_ROWS, tn), lambda l, j: (l, 0, j)),
        compiler_params=_params(("parallel", "parallel")),
        name="modulation",
    )(cond, w_mod, b_mod.reshape(DEPTH, 1, 6 * D_MODEL))


def _norm_mod(y, g, sc, sh):
    ms = jnp.mean(y * y, axis=-1, keepdims=True)
    return (y * lax.rsqrt(ms + EPS) * g) * (1.0 + sc) + sh


def _rope_rotate(x):
    w = x.shape[-1]
    lane = lax.broadcasted_iota(jnp.int32, x.shape, 1)
    nxt = pltpu.roll(x, w - 1, 1)
    prv = pltpu.roll(x, 1, 1)
    return jnp.where((lane & 1) == 0, -nxt, prv)


def _is_ctx_tile():
    return pl.program_id(0) < CTX_TILES


def _pair_specs(width):
    return [pl.BlockSpec((TM, width), lambda i: (jnp.minimum(i, CTX_TILES - 1), 0)),
            pl.BlockSpec((TM, width), lambda i: (jnp.maximum(i - CTX_TILES, 0), 0))]


def _pair_value(ctx_ref, lat_ref, cols=slice(None)):
    return jnp.where(_is_ctx_tile(), ctx_ref[:, cols], lat_ref[:, cols])


def _norm_proj_kernel(*refs, specs, n_gain, y_pair):
    n_y = 2 if y_pair else 1
    y = _pair_value(refs[0], refs[1]) if y_pair else refs[0][...]
    g_ref, sc_ref, sh_ref, w_ref, cos_ref, sin_ref, bd_ref = refs[n_y:n_y + 7]
    gain_refs = refs[n_y + 7:n_y + 7 + n_gain]
    p_ref = refs[n_y + 7 + n_gain]
    out_refs = refs[n_y + 8 + n_gain:]
    h = _norm_mod(y, g_ref[...], sc_ref[...], sh_ref[...])
    p_ref[...] = _dot(h.astype(BF16), w_ref[...])
    cos = cos_ref[...]
    sin = sin_ref[...]
    for (col, width, gi, rope, scale, _, heads), o_ref in zip(specs, out_refs):
        for c0 in range(0, width, LANES):
            x = p_ref[:, col + c0:col + c0 + LANES]
            if gi is not None:
                xx = x * x
                xx_hi = xx.astype(BF16)
                xx_lo = (xx - xx_hi.astype(F32)).astype(BF16)
                ss = _dot(xx_hi, bd_ref[...]) + _dot(xx_lo, bd_ref[...])
                x = x * lax.rsqrt(ss * (1.0 / HD) + EPS) * gain_refs[gi][...]
            if rope:
                x = x * cos + _rope_rotate(x) * sin
            if scale != 1.0:
                x = x * scale
            if heads is None:
                o_ref[:, c0:c0 + LANES] = x.astype(o_ref.dtype)
            else:
                hw = width // heads
                per = LANES // hw

                @pl.when(_is_ctx_tile())
                def _(x=x, o_ref=o_ref, c0=c0, hw=hw, per=per):
                    for u in range(per):
                        o_ref[(c0 // LANES) * per + u] = x[:, u * hw:(u + 1) * hw].astype(o_ref.dtype)


def _norm_proj(y, gain, scale, shift, w_bf16, rope_cos, rope_sin, gains, specs):
    n = w_bf16.shape[1]
    y_pair = isinstance(y, tuple)
    bd = jnp.asarray(np.kron(np.eye(LANES // HD), np.ones((HD, HD))), BF16)
    vec = pl.BlockSpec((None, 1, D_MODEL), lambda i: (_mod_row(i), 0, 0))
    rope_spec = pl.BlockSpec((TM, LANES), lambda i: (_rope_block(i), 0))
    in_specs = _pair_specs(D_MODEL) if y_pair else [pl.BlockSpec((TM, D_MODEL), lambda i: (i, 0))]
    in_specs += [pl.BlockSpec((1, D_MODEL), lambda i: (0, 0)),
                 vec, vec, pl.BlockSpec((D_MODEL, n), lambda i: (0, 0)),
                 rope_spec, rope_spec, pl.BlockSpec((LANES, LANES), lambda i: (0, 0))]
    in_specs += [pl.BlockSpec((1, LANES), lambda i: (0, 0)) for _ in gains]
    out_shape = [jax.ShapeDtypeStruct((T_ALL, n), F32)]
    out_specs = [pl.BlockSpec((TM, n), lambda i: (i, 0))]
    for (_, width, _, _, _, dtype, heads) in specs:
        if heads is None:
            out_shape.append(jax.ShapeDtypeStruct((T_ALL, width), dtype))
            out_specs.append(pl.BlockSpec((TM, width), lambda i: (i, 0)))
        else:
            out_shape.append(jax.ShapeDtypeStruct((BATCH, heads, SEQ, width // heads), dtype))
            out_specs.append(pl.BlockSpec((None, heads, SEQ, width // heads),
                                          lambda i: (jnp.minimum(i, CTX_TILES - 1), 0, 0, 0)))
    ys = list(y) if y_pair else [y]
    return pl.pallas_call(
        functools.partial(_norm_proj_kernel, specs=specs, n_gain=len(gains), y_pair=y_pair),
        out_shape=out_shape,
        grid=(N_TILES,),
        in_specs=in_specs,
        out_specs=out_specs,
        compiler_params=_params(("arbitrary",)),
        name="norm_proj",
    )(*ys, gain.reshape(1, D_MODEL), scale, shift, w_bf16, rope_cos, rope_sin, bd, *gains)


def _rope_tables():
    half = HD // 2
    freqs = 1.0 / (ROPE_THETA ** (jnp.arange(0, half, 2, dtype=F32) / half))
    t = jnp.arange(DEC_SEQ)
    rows = (t // GRID_W).astype(F32)
    cols = (t % GRID_W).astype(F32)
    ang = jnp.concatenate([rows[:, None] * freqs, cols[:, None] * freqs], axis=-1)
    cos = jnp.repeat(jnp.cos(ang), 2, axis=-1)
    sin = jnp.repeat(jnp.sin(ang), 2, axis=-1)
    cos = jnp.concatenate([jnp.tile(cos, (1, LANES // HD)), jnp.ones((TM, LANES), F32)], axis=0)
    sin = jnp.concatenate([jnp.tile(sin, (1, LANES // HD)), jnp.zeros((TM, LANES), F32)], axis=0)
    return cos, sin


def _lane_slice(ref, h, width=HD):
    per = LANES // width
    blk = ref[:, (h // per) * LANES:(h // per + 1) * LANES]
    if per == 1:
        return blk
    return blk[:, (h % per) * width:(h % per + 1) * width]


def _softmax_parts(scores):
    m = None
    for s in scores:
        ms = jnp.max(s, axis=-1, keepdims=True)
        m = ms if m is None else jnp.maximum(m, ms)
    ps = [jnp.exp(s - m) for s in scores]
    l = None
    for p in ps:
        ls = jnp.sum(p, axis=-1, keepdims=True)
        l = ls if l is None else l + ls
    return ps, l


def _attn_std_kernel(*refs, group, n_kv, has_cache, has_bias, bq):
    it = iter(refs)
    q_ref, kn_ref, vn_ref = next(it), next(it), next(it)
    kc_ref = vc_ref = b_ref = None
    if has_cache:
        kc_ref, vc_ref = next(it), next(it)
    if has_bias:
        b_ref = next(it)
    o_ref = next(it)
    outs = []
    for g in range(n_kv):
        qs = jnp.concatenate([_lane_slice(q_ref, g * group + j) for j in range(group)], axis=0)
        kn = _lane_slice(kn_ref, g)
        vn = _lane_slice(vn_ref, g)
        s_new = _dot_nt(qs, kn)
        if has_bias:
            s_new = s_new + b_ref[g]
        scores = [s_new]
        if has_cache:
            scores.append(_dot_nt(qs, kc_ref[g]))
        ps, l = _softmax_parts(scores)
        o = _dot(ps[0].astype(BF16), vn)
        if has_cache:
            o = o + _dot(ps[1].astype(BF16), vc_ref[g])
        o = o / l
        for j in range(group):
            outs.append(o[j * bq:(j + 1) * bq])
    o_ref[...] = jnp.concatenate(outs, axis=1).astype(o_ref.dtype)


def _attn_diff_kernel(*refs, has_cache):
    it = iter(refs)
    lam_ref, q_ref, kn_ref, vn_ref = next(it), next(it), next(it), next(it)
    kc_ref = vc_ref = None
    if has_cache:
        kc_ref, vc_ref = next(it), next(it)
    g_ref, o_ref = next(it), next(it)
    lam = lam_ref[0]
    post = lam_ref[1]
    outs = []
    for h in range(D_HEADS):
        pd_new, pd_c = None, None
        for j in range(2):
            f = 2 * h + j
            qs = _lane_slice(q_ref, f)
            scores = [_dot_nt(qs, _lane_slice(kn_ref, f))]
            if has_cache:
                scores.append(_dot_nt(qs, kc_ref[f]))
            ps, l = _softmax_parts(scores)
            r = 1.0 / l
            if j == 0:
                pd_new = ps[0] * r
                pd_c = ps[1] * r if has_cache else None
            else:
                r = r * lam
                pd_new = pd_new - ps[0] * r
                pd_c = pd_c - ps[1] * r if has_cache else None
        o = _dot(pd_new.astype(BF16), _lane_slice(vn_ref, h, D_VDIM))
        if has_cache:
            o = o + _dot(pd_c.astype(BF16), vc_ref[h])
        ms = jnp.mean(o * o, axis=-1, keepdims=True)
        outs.append(o * lax.rsqrt(ms + EPS) * g_ref[...] * post)
    o_ref[...] = jnp.concatenate(outs, axis=1).astype(o_ref.dtype)


def _attention(q, kn, vn, *, ctx, group=1, n_kv=1, cache=None, bias=None, diff=None, bq=256):
    if ctx:
        nb, sq, row0 = BATCH, SEQ, 0
    else:
        nb, sq, row0 = DEC_BATCH, DEC_SEQ, T_CTX
    nq = sq // bq
    qb0 = row0 // bq
    kb0 = row0 // sq
    wq, wk, wv = q.shape[1], kn.shape[1], vn.shape[1]
    in_specs = [
        pl.BlockSpec((bq, wq), lambda b, i: (qb0 + b * nq + i, 0)),
        pl.BlockSpec((sq, wk), lambda b, i: (kb0 + b, 0)),
        pl.BlockSpec((sq, wv), lambda b, i: (kb0 + b, 0)),
    ]
    args = [q, kn, vn]
    if cache is not None:
        kc, vc = cache
        in_specs += [pl.BlockSpec((None,) + kc.shape[1:], lambda b, i: (b, 0, 0, 0)),
                     pl.BlockSpec((None,) + vc.shape[1:], lambda b, i: (b, 0, 0, 0))]
        args += [kc, vc]
    if diff is None:
        if bias is not None:
            in_specs.append(pl.BlockSpec((bias.shape[0], bq, sq), lambda b, i: (0, i, 0)))
            args.append(bias)
        body = functools.partial(_attn_std_kernel, group=group, n_kv=n_kv, has_cache=cache is not None,
                                 has_bias=bias is not None, bq=bq)
    else:
        lam_vec, gain = diff
        in_specs = [pl.BlockSpec(memory_space=pltpu.SMEM)] + in_specs
        args = [lam_vec] + args
        in_specs.append(pl.BlockSpec((1, D_VDIM), lambda b, i: (0, 0)))
        args.append(gain.reshape(1, D_VDIM))
        body = functools.partial(_attn_diff_kernel, has_cache=cache is not None)
    return pl.pallas_call(
        body,
        out_shape=jax.ShapeDtypeStruct((nb * sq, 512), BF16),
        grid=(nb, nq),
        in_specs=in_specs,
        out_specs=pl.BlockSpec((bq, 512), lambda b, i: (b * nq + i, 0)),
        compiler_params=_params(("parallel", "parallel")),
        name="attention",
    )(*args)


GRID_ROWS = DEC_SEQ // GRID_W
NA_WIN_ROWS = min(NA_ROWS, GRID_ROWS)


def _na_bias_kernel(t_ref, o_ref):
    outside = jnp.full((GRID_W, GRID_W), NEG_BIG, F32)
    for qr in range(GRID_ROWS):
        r0 = min(max(qr - NA_WIN_ROWS // 2, 0), GRID_ROWS - NA_WIN_ROWS)
        parts = [t_ref[kr - qr + (NA_ROWS - 1)] if r0 <= kr < r0 + NA_WIN_ROWS else outside
                 for kr in range(GRID_ROWS)]
        o_ref[qr * GRID_W:(qr + 1) * GRID_W, :] = jnp.concatenate(parts, axis=1)


def _neighbourhood_bias(rpb):
    c = np.arange(GRID_W)
    c0 = np.clip(c - NA_COLS // 2, 0, GRID_W - NA_COLS)
    col_ok = (c[None, :] >= c0[:, None]) & (c[None, :] < c0[:, None] + NA_COLS)
    dc = np.clip(c[None, :] - c[:, None], 1 - NA_COLS, NA_COLS - 1) + (NA_COLS - 1)
    oh_c = jnp.asarray(dc[..., None] == np.arange(2 * NA_COLS - 1), F32)
    by_col = jnp.einsum("hrd,qkd->hrqk", rpb.astype(F32), oh_c, precision=HIGHEST)
    by_col = jnp.where(jnp.asarray(col_ok), by_col, NEG_BIG)
    n_dr = 2 * NA_ROWS - 1
    return pl.pallas_call(
        _na_bias_kernel,
        out_shape=jax.ShapeDtypeStruct((C_HEADS, DEC_SEQ, DEC_SEQ), F32),
        grid=(C_HEADS,),
        in_specs=[pl.BlockSpec((None, n_dr, GRID_W, GRID_W), lambda h: (h, 0, 0, 0))],
        out_specs=pl.BlockSpec((None, DEC_SEQ, DEC_SEQ), lambda h: (h, 0, 0)),
        compiler_params=_params(("parallel",)),
        name="na_bias",
    )(by_col)


def _log_sigmoid(x):
    return jnp.minimum(x, 0.0) - jnp.log1p(jnp.exp(-jnp.abs(x)))


def _mlstm_kernel(q_ref, k_ref, v_ref, g_ref, gb_ref, c0_ref, n0_ref, m0_ref, h_ref, c_ref, n_ref, m_ref, hb_ref,
                  *, seq):
    L = B_CHUNK
    nc = seq // L
    row = lax.broadcasted_iota(jnp.int32, (L, L), 0)
    col = lax.broadcasted_iota(jnp.int32, (L, L), 1)
    keeps = (col <= row, col >= row)
    k_scale = B_DK ** -0.5
    c_ref[...] = c0_ref[...]
    n_ref[...] = n0_ref[...]
    m_ref[...] = m0_ref[...]

    def step(j):
        for d in range(2):
            keep = keeps[d]
            c = j if d == 0 else nc - 1 - j
            off = c * L if isinstance(c, int) else pl.multiple_of(c * L, L)
            gates = g_ref[pl.ds(off, L), :] + gb_ref[...]
            cum = _dot(keep.astype(F32), _log_sigmoid(gates), HIGHEST)
            cum_t = cum.T
            gates_t = gates.T
            out_ref = h_ref if d == 0 else hb_ref
            for h in range(B_HEADS):
                ci = (2 * d) * B_HEADS + h
                cf = (2 * d + 1) * B_HEADS + h
                hs = slice(h * B_DK, (h + 1) * B_DK)
                C = c_ref[d, h]
                n = n_ref[d, h]
                m = m_ref[d, h][:, 0:1]
                qc = q_ref[pl.ds(off, L), hs]
                kc = k_ref[pl.ds(off, L), hs] * k_scale
                vc = v_ref[pl.ds(off, L), hs]
                b_col = cum[:, cf:cf + 1]
                i_col = gates[:, ci:ci + 1]
                b_row = cum_t[cf:cf + 1, :]
                i_row = gates_t[ci:ci + 1, :]
                dlog = jnp.where(keep, b_col - b_row + i_row, -jnp.inf)
                inter = b_col + m
                m_t = jnp.maximum(inter, jnp.max(dlog, axis=-1, keepdims=True))
                w_intra = jnp.exp(dlog - m_t)
                w_inter = jnp.exp(inter - m_t)
                qb = qc.astype(BF16)
                vb = vc.astype(BF16)
                qk = _dot_nt(qb, kc.astype(BF16)) * w_intra
                num = _dot(qk.astype(BF16), vb) + w_inter * _dot(qb, C.astype(BF16))
                den = jnp.sum(qk, axis=-1, keepdims=True) + w_inter * jnp.sum(qc * n, axis=-1, keepdims=True)
                out_ref[pl.ds(off, L), hs] = num / jnp.maximum(jnp.abs(den), jnp.exp(-m_t))
                b_last = b_col[L - 1:L, :] if d == 0 else b_col[0:1, :]
                end_col = b_last - b_col + i_col
                m_new = jnp.maximum(b_last + m, jnp.max(end_col, axis=0, keepdims=True))
                w_end = jnp.exp(end_col - m_new)
                decay = jnp.exp(b_last + m - m_new)
                kw = kc * w_end
                c_ref[d, h] = decay * C + _dot_tn(kw.astype(BF16), vb)
                n_ref[d, h] = decay * n + jnp.sum(kw, axis=0, keepdims=True)
                m_ref[d, h] = jnp.broadcast_to(m_new, (1, LANES))

    if nc <= 2:
        for j in range(nc):
            step(j)
    else:
        def body(j, carry):
            step(j)
            return carry

        lax.fori_loop(0, nc, body, 0)
    h_ref[...] = h_ref[...] + hb_ref[...]


def _mlstm(p, gate_bias, c0, n0, m0, *, ctx):
    if ctx:
        nb, seq, blk0 = BATCH, SEQ, 0
    else:
        nb, seq, blk0 = DEC_BATCH, DEC_SEQ, T_CTX // DEC_SEQ
    w = B_HEADS * B_DK

    def cols(c0_, width):
        return pl.BlockSpec((seq, width), lambda b: (blk0 + b, c0_ // width))

    gb = jnp.zeros((1, LANES), F32).at[0, :4 * B_HEADS].set(gate_bias.reshape(-1).astype(F32))
    st = lambda shape: pl.BlockSpec((None,) + shape, lambda b: (b, 0, 0, 0, 0))
    return pl.pallas_call(
        functools.partial(_mlstm_kernel, seq=seq),
        out_shape=[
            jax.ShapeDtypeStruct((nb * seq, w), F32),
            jax.ShapeDtypeStruct((nb, 2, B_HEADS, B_DK, B_DV), F32),
            jax.ShapeDtypeStruct((nb, 2, B_HEADS, 1, B_DK), F32),
            jax.ShapeDtypeStruct((nb, 2, B_HEADS, 1, LANES), F32),
        ],
        grid=(nb,),
        in_specs=[
            cols(EV_BQ, w), cols(EV_BK, w), cols(EV_BV, w), cols(EV_BG, LANES),
            pl.BlockSpec((1, LANES), lambda b: (0, 0)),
            st((2, B_HEADS, B_DK, B_DV)), st((2, B_HEADS, 1, B_DK)), st((2, B_HEADS, 1, LANES)),
        ],
        out_specs=[
            pl.BlockSpec((seq, w), lambda b: (b, 0)),
            st((2, B_HEADS, B_DK, B_DV)), st((2, B_HEADS, 1, B_DK)), st((2, B_HEADS, 1, LANES)),
        ],
        scratch_shapes=[pltpu.VMEM((seq, w), F32)],
        compiler_params=_params(("parallel",)),
        name="mlstm",
    )(p, p, p, p, gb, c0, n0, m0)


def _merge_value(refs, even, y_pair):
    a_ctx, a_lat, b_ctx, b_lat = refs[:4]
    rest = refs[4:]
    if even:
        bo_ref, ng_ref = rest[:2]
        rest = rest[2:]
        parts = [_pair_value(a_ctx, a_lat)]
        for h in range(B_HEADS):
            hs = slice(h * B_DV, (h + 1) * B_DV)
            x = _pair_value(b_ctx, b_lat, hs)
            ms = jnp.mean(x * x, axis=-1, keepdims=True)
            xn = x * lax.rsqrt(ms + EPS) * ng_ref[:, hs]
            parts.append((jax.nn.sigmoid(bo_ref[:, hs]) * xn).astype(BF16))
    else:
        parts = [_pair_value(a_ctx, a_lat), _pair_value(b_ctx, b_lat)]
    w_ref = rest[0]
    y = _pair_value(rest[1], rest[2]) if y_pair else rest[1][...]
    g_ref = rest[-1]
    cat = jnp.concatenate(parts, axis=1)
    return y + g_ref[...] * _dot(cat, w_ref[...])


def _merge_specs(y, gate, w_bf16, a, b, p, norm_gain):
    in_specs = _pair_specs(512) + _pair_specs(512)
    args = [*a, *b]
    if p is not None:
        in_specs += [pl.BlockSpec((TM, 512), lambda i: (i, EV_BO // 512)), pl.BlockSpec((1, 512), lambda i: (0, 0))]
        args += [p, norm_gain.reshape(1, 512)]
    in_specs.append(pl.BlockSpec((D_MODEL, D_MODEL), lambda i: (0, 0)))
    args.append(w_bf16)
    if isinstance(y, tuple):
        in_specs += _pair_specs(D_MODEL)
        args += list(y)
    else:
        in_specs.append(pl.BlockSpec((TM, D_MODEL), lambda i: (i, 0)))
        args.append(y)
    in_specs.append(pl.BlockSpec((None, 1, D_MODEL), lambda i: (_mod_row(i), 0, 0)))
    args.append(gate)
    return in_specs, args


SLAB = D_MODEL // LANES


def _load_slabs(ref, rows):
    return jnp.concatenate([ref[pl.ds(c, rows, stride=SLAB), :] for c in range(SLAB)], axis=1)


def _store_slabs(ref, x):
    for c in range(SLAB):
        ref[pl.ds(c, x.shape[0], stride=SLAB), :] = x[:, c * LANES:(c + 1) * LANES]


def _slab(ref, idx):
    return ref.at[pl.ds(pl.multiple_of(idx * SLAB, SLAB), SLAB)]


def _merge_router_kernel(*refs, even, y_pair):
    n_merge = 4 + (2 if even else 0) + 1 + (2 if y_pair else 1) + 1
    merge_refs = refs[:n_merge]
    (g_ref, sc_ref, sh_ref, whi_ref, wlo_ref, b_ref,
     y_out_ref, h_ref, ti_ref, tp_ref, rk_ref, cnt_ref, base_ref) = refs[n_merge:]

    @pl.when(pl.program_id(0) == 0)
    def _():
        base_ref[...] = jnp.zeros(base_ref.shape, F32)

    y = _merge_value(merge_refs, even, y_pair)
    y_out_ref[...] = y
    h = _norm_mod(y, g_ref[...], sc_ref[...], sh_ref[...])
    _store_slabs(h_ref, h)
    h_hi = h.astype(BF16)
    h_lo = (h - h_hi.astype(F32)).astype(BF16)
    logits = (_dot(h_hi, whi_ref[...]) + (_dot(h_hi, wlo_ref[...]) + _dot(h_lo, whi_ref[...]))
              + b_ref[...])
    lane = lax.broadcasted_iota(jnp.int32, logits.shape, 1)
    lane_f = lane.astype(F32)
    vals, idxs = [], []
    for _ in range(TOP_K):
        mx = jnp.max(logits, axis=-1, keepdims=True)
        ix = jnp.min(jnp.where(logits == mx, lane_f, float(LANES)), axis=-1, keepdims=True)
        vals.append(mx)
        idxs.append(ix)
        logits = jnp.where(lane_f == ix, -jnp.inf, logits)
    es = [jnp.exp(v - vals[0]) for v in vals]
    tot = es[0] + es[1] + es[2] + es[3]
    ti = jnp.zeros(logits.shape, F32)
    tp = jnp.zeros(logits.shape, F32)
    for k in range(TOP_K):
        ti = jnp.where(lane == k, idxs[k], ti)
        tp = jnp.where(lane == k, es[k] / tot, tp)
    ti_ref[...] = ti.T[0:8, :].astype(jnp.int32)
    tp_ref[...] = tp
    onehots = [(lane_f == ix).astype(F32) for ix in idxs]
    cnt = onehots[0] + onehots[1] + onehots[2] + onehots[3]
    row = lax.broadcasted_iota(jnp.int32, (TM, TM), 0)
    col = lax.broadcasted_iota(jnp.int32, (TM, TM), 1)
    before = _dot((col < row).astype(BF16), cnt.astype(BF16)) + base_ref[...]
    rk = jnp.zeros(logits.shape, F32)
    for k in range(TOP_K):
        rk = jnp.where(lane == k, jnp.sum(onehots[k] * before, axis=-1, keepdims=True), rk)
    rk_ref[...] = rk.T[0:8, :].astype(jnp.int32)
    base_ref[...] = base_ref[...] + jnp.sum(cnt, axis=0, keepdims=True)
    cnt_ref[...] = base_ref[...]


def _merge_router(y, gate, w_out_bf16, a, b, gain, scale, shift, rw, rb, *, p=None, norm_gain=None):
    merge_in_specs, merge_args = _merge_specs(y, gate, w_out_bf16, a, b, p, norm_gain)
    vec = pl.BlockSpec((None, 1, D_MODEL), lambda i: (_mod_row(i), 0, 0))
    rw_p = jnp.zeros((D_MODEL, LANES), F32).at[:, :N_EXPERTS].set(rw)
    rb_p = jnp.full((1, LANES), NEG_BIG, F32).at[0, :N_EXPERTS].set(rb)
    rw_hi = rw_p.astype(BF16)
    tile = lambda w: pl.BlockSpec((TM, w), lambda i: (i, 0))
    by_choice = pl.BlockSpec((None, 8, TM), lambda i: (i, 0, 0))
    return pl.pallas_call(
        functools.partial(_merge_router_kernel, even=p is not None, y_pair=isinstance(y, tuple)),
        out_shape=[jax.ShapeDtypeStruct((T_ALL, D_MODEL), F32),
                   jax.ShapeDtypeStruct((T_ALL * SLAB, LANES), F32),
                   jax.ShapeDtypeStruct((N_TILES, 8, TM), jnp.int32),
                   jax.ShapeDtypeStruct((T_ALL, LANES), F32),
                   jax.ShapeDtypeStruct((N_TILES, 8, TM), jnp.int32),
                   jax.ShapeDtypeStruct((1, LANES), F32)],
        grid=(N_TILES,),
        in_specs=merge_in_specs + [
            pl.BlockSpec((1, D_MODEL), lambda i: (0, 0)), vec, vec,
            pl.BlockSpec((D_MODEL, LANES), lambda i: (0, 0)), pl.BlockSpec((D_MODEL, LANES), lambda i: (0, 0)),
            pl.BlockSpec((1, LANES), lambda i: (0, 0))],
        out_specs=[tile(D_MODEL), pl.BlockSpec((TM * SLAB, LANES), lambda i: (i, 0)), by_choice, tile(LANES),
                   by_choice, pl.BlockSpec((1, LANES), lambda i: (0, 0))],
        scratch_shapes=[pltpu.VMEM((1, LANES), F32)],
        compiler_params=_params(("arbitrary",)),
        name="merge_router",
    )(*merge_args, gain.reshape(1, D_MODEL), scale, shift, rw_hi, (rw_p - rw_hi.astype(F32)).astype(BF16), rb_p)


def _route_plan(top_i, rank, counts):
    experts = jnp.arange(N_EXPERTS, dtype=jnp.int32)
    padded = ((counts + MOE_TM - 1) // MOE_TM) * MOE_TM
    seg_end = jnp.cumsum(padded)
    seg_start = seg_end - padded
    pos = rank
    for e in range(N_EXPERTS - 1):
        pos = pos + jnp.where(top_i > e, padded[e], 0)
    n_active = seg_end[-1] // MOE_TM
    fill = jnp.concatenate([seg_start + counts, padded - counts, n_active[None]]).astype(jnp.int32)
    tile_start = jnp.arange(MOE_TILES, dtype=jnp.int32) * MOE_TM
    tile_expert = jnp.sum((seg_end[None, :] <= tile_start[:, None]).astype(jnp.int32), axis=1)
    last = jnp.sum((seg_end <= (n_active - 1) * MOE_TM).astype(jnp.int32))
    tile_expert = jnp.minimum(jnp.where(tile_start < seg_end[-1], tile_expert, last), N_EXPERTS - 1)
    owns = (padded > 0).astype(jnp.int32)
    run_of_expert = jnp.cumsum(owns) - 1
    run_expert = jnp.sum(jnp.where((run_of_expert[None, :] == experts[:, None]) & (owns[None, :] > 0),
                                   experts[None, :], 0), axis=1)
    runs = jnp.concatenate([run_expert, jnp.sum(owns)[None]]).astype(jnp.int32)
    tile_run = jnp.sum(jnp.where(tile_expert[:, None] == experts[None, :], run_of_expert[None, :], 0), axis=1)
    experts_plan = (tile_expert.astype(jnp.int32), n_active.reshape(1).astype(jnp.int32),
                    tile_run.astype(jnp.int32), runs)
    return pos.astype(jnp.int32), fill, experts_plan


DMA_UNROLL = 4
DMA_QUEUES = 2


def _wait_slabs(ref, n_slabs, sem):
    view = ref.at[pl.ds(0, n_slabs * SLAB)]
    pltpu.make_async_copy(view, view, sem).wait()


def _dispatch_kernel(fill_ref, pos_ref, h_ref, xs_ref, inv_ref, sem):
    i = pl.program_id(0)
    base = i * TM

    def issue(j, carry):
        for u in range(DMA_UNROLL):
            t = j * DMA_UNROLL + u
            for k in range(TOP_K):
                slot = pos_ref[k * TM + t]
                inv_ref[slot] = base * TOP_K + k * TM + t
                pltpu.make_async_copy(_slab(h_ref, t), _slab(xs_ref, slot), sem).start(priority=k % DMA_QUEUES)
        return carry

    lax.fori_loop(0, TM // DMA_UNROLL, issue, 0)

    @pl.when(i == 0)
    def _():
        def per_expert(e, total):
            start = fill_ref[e]
            n = fill_ref[N_EXPERTS + e]

            def one(r, carry):
                slot = start + r
                inv_ref[slot] = N_ASSIGN + slot % MOE_TM
                pltpu.make_async_copy(_slab(h_ref, 0), _slab(xs_ref, slot), sem).start()
                return carry

            lax.fori_loop(0, n, one, 0)
            return total + n

        total = lax.fori_loop(0, N_EXPERTS, per_expert, 0)

        n_active = fill_ref[2 * N_EXPERTS]

        def unused_tile(ti, carry):
            dst = xs_ref.at[pl.ds(pl.multiple_of(ti * (MOE_TM * SLAB), MOE_TM * SLAB), MOE_TM * SLAB)]
            pltpu.make_async_copy(h_ref, dst, sem).start()
            return carry

        lax.fori_loop(n_active, MOE_TILES, unused_tile, 0)

        def unused_slot(slot, carry):
            inv_ref[slot] = N_ASSIGN
            return carry

        lax.fori_loop(n_active * MOE_TM, MOE_ROWS, unused_slot, 0)
        total = total + (MOE_TILES - n_active) * MOE_TM

        @pl.when(total > 0)
        def _():
            _wait_slabs(xs_ref, total, sem)

    _wait_slabs(xs_ref, TM * TOP_K, sem)


def _dispatch(h_slabs, pos, fill):
    grid_spec = pltpu.PrefetchScalarGridSpec(
        num_scalar_prefetch=1,
        grid=(N_TILES,),
        in_specs=[pl.BlockSpec((TM * TOP_K,), lambda i, fill: (i,), memory_space=pltpu.SMEM),
                  pl.BlockSpec((TM * SLAB, LANES), lambda i, fill: (i, 0))],
        out_specs=[pl.BlockSpec(memory_space=pl.ANY), pl.BlockSpec(memory_space=pltpu.SMEM)],
        scratch_shapes=[pltpu.SemaphoreType.DMA],
    )
    return pl.pallas_call(
        _dispatch_kernel,
        out_shape=[jax.ShapeDtypeStruct((MOE_ROWS * SLAB, LANES), F32),
                   jax.ShapeDtypeStruct((MOE_ROWS,), jnp.int32)],
        grid_spec=grid_spec,
        compiler_params=_params(("arbitrary",)),
        name="moe_dispatch",
    )(fill, pos, h_slabs)


MOE_HALF = MOE_TM // 2


INV_TILES = 4


def _moe_kernel(te_ref, na_ref, ts_ref, ex_ref, inv_prev, inv_cur, x_ref, wgu_hbm, bgu_ref, wd_hbm, bd_ref,
                out_hbm, wgu_f32, wd_f32, wgu_bf, wd_bf, obuf0, obuf1, sems, osems, *, layer):
    i = pl.program_id(0)
    s = ts_ref[i]
    first = (i == 0) | (s != ts_ref[jnp.maximum(i - 1, 0)])
    n_active = na_ref[0]

    def start_rows(buf, half, tile):
        table = inv_cur if half == 0 else inv_prev
        off = (tile % INV_TILES) * MOE_TM + half * MOE_HALF
        for r in range(MOE_HALF):
            dst = _slab(out_hbm, table[off + r])
            pltpu.make_async_copy(buf.at[pl.ds(r * SLAB, SLAB)], dst, osems.at[half]).start(
                priority=r % DMA_QUEUES)

    def wait_rows(half):
        _wait_slabs(out_hbm, MOE_HALF, osems.at[half])

    def ffn_tile(deferred):
        if deferred:
            start_rows(obuf1, 1, i - 1)
        x = _load_slabs(x_ref, MOE_TM).astype(BF16)
        gu = _dot(x, wgu_bf[...]) + bgu_ref[...]
        gate = jnp.minimum(gu[:, :D_FF], SWIGLU_LIMIT)
        up = jnp.clip(gu[:, D_FF:], -SWIGLU_LIMIT, SWIGLU_LIMIT)
        act = ((up + 1.0) * gate * jax.nn.sigmoid(SWIGLU_ALPHA * gate)).astype(BF16)
        o0 = _dot(act[:MOE_HALF], wd_bf[...]) + bd_ref[...]
        if deferred:
            wait_rows(0)
        _store_slabs(obuf0, o0)
        start_rows(obuf0, 0, i)
        o1 = _dot(act[MOE_HALF:], wd_bf[...]) + bd_ref[...]
        if deferred:
            wait_rows(1)
        _store_slabs(obuf1, o1)

    def weight_copies(slot):
        e = ex_ref[slot]
        b = slot % 2
        return (pltpu.make_async_copy(wgu_hbm.at[layer, e], wgu_f32.at[b], sems.at[0, b]),
                pltpu.make_async_copy(wd_hbm.at[layer, e], wd_f32.at[b], sems.at[1, b]))

    @pl.when(i == 0)
    def _():
        for cp in weight_copies(0):
            cp.start()

    @pl.when(first)
    def _():
        for cp in weight_copies(s):
            cp.wait()

        @pl.when(s + 1 < ex_ref[N_EXPERTS])
        def _():
            for cp in weight_copies(s + 1):
                cp.start()

        b = s % 2
        wgu_bf[...] = wgu_f32[b].astype(BF16)
        wd_bf[...] = wd_f32[b].astype(BF16)

    @pl.when(i == 0)
    def _():
        ffn_tile(False)

    @pl.when((i > 0) & (i < n_active))
    def _():
        ffn_tile(True)

    @pl.when(i == n_active)
    def _():
        wait_rows(0)
        start_rows(obuf1, 1, i - 1)
        wait_rows(1)
        for half, buf in enumerate((obuf0, obuf1)):
            dst = out_hbm.at[pl.ds((N_ASSIGN + half * MOE_HALF) * SLAB, MOE_HALF * SLAB)]
            pltpu.make_async_copy(buf, dst, osems.at[half]).start()
        wait_rows(0)
        wait_rows(1)


def _moe_experts(layer, xs, plan, inv, w_gu, b_gu, w_down, b_down):
    const = lambda i, te, na, ts, ex: (layer, te[i], 0, 0)
    inv_block = INV_TILES * MOE_TM
    grid_spec = pltpu.PrefetchScalarGridSpec(
        num_scalar_prefetch=4,
        grid=(MOE_TILES,),
        in_specs=[
            pl.BlockSpec((inv_block,), lambda i, te, na, ts, ex: (jnp.maximum(i - 1, 0) // INV_TILES,),
                         memory_space=pltpu.SMEM),
            pl.BlockSpec((inv_block,), lambda i, te, na, ts, ex: (i // INV_TILES,), memory_space=pltpu.SMEM),
            pl.BlockSpec((MOE_TM * SLAB, LANES), lambda i, te, na, ts, ex: (jnp.minimum(i, na[0] - 1), 0)),
            pl.BlockSpec(memory_space=pl.ANY),
            pl.BlockSpec((None, None, 1, 2 * D_FF), const),
            pl.BlockSpec(memory_space=pl.ANY),
            pl.BlockSpec((None, None, 1, D_MODEL), const),
        ],
        out_specs=pl.BlockSpec(memory_space=pl.ANY),
        scratch_shapes=[pltpu.VMEM((2, D_MODEL, 2 * D_FF), F32), pltpu.VMEM((2, D_FF, D_MODEL), F32),
                        pltpu.VMEM((D_MODEL, 2 * D_FF), BF16), pltpu.VMEM((D_FF, D_MODEL), BF16),
                        pltpu.VMEM((MOE_HALF * SLAB, LANES), F32), pltpu.VMEM((MOE_HALF * SLAB, LANES), F32),
                        pltpu.SemaphoreType.DMA((2, 2)), pltpu.SemaphoreType.DMA((2,))],
    )
    return pl.pallas_call(
        functools.partial(_moe_kernel, layer=layer),
        out_shape=jax.ShapeDtypeStruct(((N_ASSIGN + MOE_TM) * SLAB, LANES), F32),
        grid_spec=grid_spec,
        compiler_params=_params(("arbitrary",)),
        name="moe_experts",
    )(*plan, inv, inv, xs, w_gu, b_gu.reshape(DEPTH, N_EXPERTS, 1, 2 * D_FF), w_down,
      b_down.reshape(DEPTH, N_EXPERTS, 1, D_MODEL))


def _combine_kernel(out_ref, y_ref, tp_ref, g_ref, *rest, final):
    if final:
        fg_ref, n_ctx_ref, n_lat_ref = rest
    else:
        (o_ref,) = rest

    tp = tp_ref[...]
    ss = jnp.zeros((TM, 1), F32)
    chunks = []
    for c in range(SLAB):
        cs = slice(c * LANES, (c + 1) * LANES)
        acc = tp[:, 0:1] * out_ref[pl.ds(c, TM, stride=SLAB), :]
        for k in range(1, TOP_K):
            acc = acc + tp[:, k:k + 1] * out_ref[pl.ds(k * TM * SLAB + c, TM, stride=SLAB), :]
        yc = y_ref[:, cs] + g_ref[:, cs] * acc
        if final:
            ss = ss + jnp.sum(yc * yc, axis=-1, keepdims=True)
            chunks.append(yc)
        else:
            o_ref[:, cs] = yc
    if final:
        inv = lax.rsqrt(ss * (1.0 / D_MODEL) + EPS)
        normed = jnp.concatenate([chunks[c] * inv * fg_ref[:, c * LANES:(c + 1) * LANES] for c in range(SLAB)],
                                 axis=1)

        @pl.when(_is_ctx_tile())
        def _():
            n_ctx_ref[...] = normed

        @pl.when(jnp.logical_not(_is_ctx_tile()))
        def _():
            n_lat_ref[...] = normed


def _combine(y, out_slabs, top_p, gate, final_gain=None):
    final = final_gain is not None
    tile = pl.BlockSpec((TM, D_MODEL), lambda i: (i, 0))
    in_specs = [pl.BlockSpec((TOP_K * TM * SLAB, LANES), lambda i: (i, 0)), tile,
                pl.BlockSpec((TM, LANES), lambda i: (i, 0)),
                pl.BlockSpec((None, 1, D_MODEL), lambda i: (_mod_row(i), 0, 0))]
    args = [out_slabs, y, top_p, gate]
    if final:
        in_specs.append(pl.BlockSpec((1, D_MODEL), lambda i: (0, 0)))
        args.append(final_gain.reshape(1, D_MODEL))
        out_shape = [jax.ShapeDtypeStruct((T_CTX, D_MODEL), F32), jax.ShapeDtypeStruct((T_LAT, D_MODEL), F32)]
        out_specs = [pl.BlockSpec((TM, D_MODEL), lambda i: (jnp.minimum(i, CTX_TILES - 1), 0)),
                     pl.BlockSpec((TM, D_MODEL), lambda i: (jnp.maximum(i - CTX_TILES, 0), 0))]
    else:
        out_shape = [jax.ShapeDtypeStruct((T_ALL, D_MODEL), F32)]
        out_specs = [tile]
    return pl.pallas_call(
        functools.partial(_combine_kernel, final=final),
        out_shape=out_shape,
        grid=(N_TILES,),
        in_specs=in_specs,
        out_specs=out_specs,
        compiler_params=_params(("arbitrary",)),
        name="moe_combine",
    )(*args)


def _moe_layer(layer, y, routed, gate, w_gu, b_gu, w_down, b_down, final_gain=None):
    h_slabs, top_i, top_p, rank, counts = routed
    pos, fill, experts_plan = _route_plan(top_i[:, :TOP_K].reshape(-1), rank[:, :TOP_K].reshape(-1),
                                          counts[0, :N_EXPERTS].astype(jnp.int32))
    xs, inv = _dispatch(h_slabs, pos, fill)
    out = _moe_experts(layer, xs, experts_plan, inv, w_gu, b_gu, w_down, b_down)
    return _combine(y, out, top_p, gate, final_gain)


def kernel(x_prompt, x_sample, c, cache_a_k, cache_a_v, state_b_C, state_b_n, state_b_m, cache_c_k, cache_c_v, cache_d_k, cache_d_v, c_ctx, w_mod, b_mod, norm1_g, norm2_g, w_in_even, w_out_even, a_q_gain, a_k_gain, b_gate_bias, b_norm_gain, w_in_odd, w_out_odd, c_rpb, d_lambda, d_norm_gain, router_w, router_b, expert_w_gu, expert_b_gu, expert_w_down, expert_b_down, final_norm_g):
    y = (x_prompt.reshape(T_CTX, D_MODEL), x_sample.reshape(T_LAT, D_MODEL))
    cond = jnp.zeros((MOD_ROWS, D_MODEL), F32).at[0].set(c_ctx).at[1:1 + DEC_BATCH].set(c)
    mod = _modulation(cond, w_mod, b_mod).reshape(DEPTH, MOD_ROWS, 6, 1, D_MODEL)
    rope_cos, rope_sin = _rope_tables()
    scale = HD ** -0.5
    outs = {}

    for layer in range(DEPTH):
        sh1, sc1, g1, sh2, sc2, g2 = (mod[layer, :, k] for k in range(6))
        j = layer // 2
        if layer % 2 == 0:
            w = w_in_even[j]
            sizes = np.cumsum([0, 512, 128, 128, 512, 512, 512, 512, 16])
            aq, ak, av, bq, bk, bv, bo, bg = (w[:, sizes[k]:sizes[k + 1]] for k in range(8))
            w_in = jnp.concatenate([aq, bo, bq, bk, bv, ak, av, bg, jnp.zeros((D_MODEL, EV_N - EV_BG - 16), F32)],
                                   axis=1).astype(BF16)
            qg = jnp.tile(a_q_gain[j], LANES // HD).reshape(1, LANES)
            kg = jnp.tile(a_k_gain[j], LANES // HD).reshape(1, LANES)
            specs = ((EV_AQ, 512, 0, True, scale, BF16, None), (EV_AK, 128, 1, True, 1.0, BF16, None),
                     (EV_AV, 128, None, False, 1.0, BF16, None),
                     (EV_AK, 128, 1, False, 1.0, F32, A_KV), (EV_AV, 128, None, False, 1.0, F32, A_KV))
            p, qa, ka, va, new_ak, new_av = _norm_proj(y, norm1_g[layer], sc1, sh1, w_in, rope_cos, rope_sin,
                                                       [qg, kg], specs)
            oa_ctx = _attention(qa, ka, va, ctx=True, group=A_HEADS // A_KV, n_kv=A_KV)
            cache = (cache_a_k[:, j].astype(BF16), cache_a_v[:, j].astype(BF16))
            oa_lat = _attention(qa, ka, va, ctx=False, group=A_HEADS // A_KV, n_kv=A_KV, cache=cache)
            zc = jnp.zeros((BATCH, 2, B_HEADS, B_DK, B_DV), F32)
            zn = jnp.zeros((BATCH, 2, B_HEADS, 1, B_DK), F32)
            zm = jnp.zeros((BATCH, 2, B_HEADS, 1, LANES), F32)
            hb_ctx, bC, bn, bm = _mlstm(p, b_gate_bias[j], zc, zn, zm, ctx=True)
            m0 = jnp.broadcast_to(state_b_m[:, j][..., None, None], (DEC_BATCH, 2, B_HEADS, 1, LANES))
            hb_lat, _, _, _ = _mlstm(p, b_gate_bias[j], state_b_C[:, j], state_b_n[:, j][:, :, :, None, :], m0,
                                     ctx=False)
            y, *routed = _merge_router(y, g1, w_out_even[j].astype(BF16), (oa_ctx, oa_lat), (hb_ctx, hb_lat),
                                       norm2_g[layer], sc2, sh2, router_w[layer], router_b[layer],
                                       p=p, norm_gain=b_norm_gain[j])
            outs.setdefault("a_k", []).append(new_ak)
            outs.setdefault("a_v", []).append(new_av)
            outs.setdefault("b_C", []).append(bC)
            outs.setdefault("b_n", []).append(bn[:, :, :, 0, :])
            outs.setdefault("b_m", []).append(bm[:, :, :, 0, 0])
        else:
            specs = ((0, 512, None, False, scale, BF16, None), (512, 512, None, False, 1.0, BF16, None),
                     (1024, 512, None, False, 1.0, BF16, None), (1536, 512, None, True, scale, BF16, None),
                     (2048, 512, None, True, 1.0, BF16, None), (2560, 512, None, False, 1.0, BF16, None),
                     (512, 512, None, False, 1.0, F32, C_HEADS), (1024, 512, None, False, 1.0, F32, C_HEADS),
                     (2048, 512, None, False, 1.0, F32, 2 * D_HEADS), (2560, 512, None, False, 1.0, F32, D_HEADS))
            p, qc, kc, vc, qd, kd, vd, new_ck, new_cv, new_dk, new_dv = _norm_proj(
                y, norm1_g[layer], sc1, sh1, w_in_odd[j].astype(BF16), rope_cos, rope_sin, [], specs)
            lam_init = 0.8 - 0.6 * math.exp(-0.3 * layer)
            lp = d_lambda[j].astype(F32)
            lam = jnp.exp(jnp.sum(lp[0] * lp[1])) - jnp.exp(jnp.sum(lp[2] * lp[3])) + lam_init
            lam_vec = jnp.stack([lam, jnp.asarray(1.0 - lam_init, F32)]).astype(F32)
            diff = (lam_vec, d_norm_gain[j])
            oc_ctx = _attention(qc, kc, vc, ctx=True, n_kv=C_HEADS)
            od_ctx = _attention(qd, kd, vd, ctx=True, diff=diff)
            bias = _neighbourhood_bias(c_rpb[j])
            oc_lat = _attention(qc, kc, vc, ctx=False, n_kv=C_HEADS, bias=bias,
                                cache=(cache_c_k[:, j].astype(BF16), cache_c_v[:, j].astype(BF16)))
            kd_cache = cache_d_k[:, j].reshape(DEC_BATCH, 2 * D_HEADS, PAST_LEN, HD).astype(BF16)
            od_lat = _attention(qd, kd, vd, ctx=False, diff=diff, cache=(kd_cache, cache_d_v[:, j].astype(BF16)))
            y, *routed = _merge_router(y, g1, w_out_odd[j].astype(BF16), (oc_ctx, oc_lat), (od_ctx, od_lat),
                                       norm2_g[layer], sc2, sh2, router_w[layer], router_b[layer])
            outs.setdefault("c_k", []).append(new_ck)
            outs.setdefault("c_v", []).append(new_cv)
            outs.setdefault("d_k", []).append(new_dk.reshape(BATCH, D_HEADS, 2, SEQ, HD))
            outs.setdefault("d_v", []).append(new_dv)
        res = _moe_layer(layer, y, routed, g2, expert_w_gu, expert_b_gu, expert_w_down, expert_b_down,
                         final_gain=final_norm_g if layer == DEPTH - 1 else None)
        y = res[0]
    y_prompt, y_sample = res
    stack = lambda k: jnp.stack(outs[k], axis=1)
    return (y_prompt.reshape(BATCH, SEQ, D_MODEL), y_sample.reshape(DEC_BATCH, DEC_SEQ, D_MODEL),
            stack("a_k"), stack("a_v"), stack("b_C"), stack("b_n"), stack("b_m"),
            stack("c_k"), stack("c_v"), stack("d_k"), stack("d_v"))
```

```python
import functools
import math

import numpy as np
import jax
import jax.numpy as jnp
from jax import lax
from jax.experimental import pallas as pl
from jax.experimental.pallas import tpu as pltpu

D_MODEL = 1024
BATCH = 32
SEQ = 256
DEPTH = 2
DEC_BATCH = 8
DEC_SEQ = 1024
PAST_LEN = 512
GRID_W = 64
HD = 64
A_HEADS = 8
A_KV = 2
B_HEADS = 4
B_DK = 128
B_DV = 128
B_CHUNK = 128
C_HEADS = 8
NA_ROWS = 8
NA_COLS = 16
D_HEADS = 4
D_VDIM = 2 * HD
N_EXPERTS = 32
TOP_K = 4
D_FF = 1024
SWIGLU_LIMIT = 7.0
SWIGLU_ALPHA = 1.702
ROPE_THETA = 10000.0
EPS = 1e-6

F32 = jnp.float32
BF16 = jnp.bfloat16
HIGHEST = lax.Precision.HIGHEST

T_CTX = BATCH * SEQ
T_LAT = DEC_BATCH * DEC_SEQ
T_ALL = T_CTX + T_LAT
TM = 256
CTX_TILES = T_CTX // TM
LAT_TILES_PER_BATCH = DEC_SEQ // TM
N_TILES = T_ALL // TM
MOD_ROWS = 16
LANES = 128
NEG_BIG = -1e30
MOE_TM = 256
N_ASSIGN = T_ALL * TOP_K
MOE_ROWS = N_ASSIGN + N_EXPERTS * MOE_TM
MOE_TILES = MOE_ROWS // MOE_TM
VMEM_LIMIT = 56 * 1024 * 1024

EV_AQ, EV_BO, EV_BQ, EV_BK, EV_BV, EV_AK, EV_AV, EV_BG = 0, 512, 1024, 1536, 2048, 2560, 2688, 2816
EV_N = 2944
OD_N = 3072


def _params(sem, vmem=VMEM_LIMIT):
    return pltpu.CompilerParams(dimension_semantics=sem, vmem_limit_bytes=vmem)


def _mod_row(i):
    return jnp.where(i < CTX_TILES, 0, 1 + (i - CTX_TILES) // LAT_TILES_PER_BATCH)


def _rope_block(i):
    return jnp.where(i < CTX_TILES, LAT_TILES_PER_BATCH, (i - CTX_TILES) % LAT_TILES_PER_BATCH)


def _dot(a, b, precision=None):
    return jnp.dot(a, b, preferred_element_type=F32, precision=precision)


def _dot_nt(a, b):
    return lax.dot_general(a, b, (((1,), (1,)), ((), ())), preferred_element_type=F32)


def _dot_tn(a, b):
    return lax.dot_general(a, b, (((0,), (0,)), ((), ())), preferred_element_type=F32)


def _modulation_kernel(c_ref, w_ref, b_ref, o_ref):
    c = c_ref[...]
    s = c * jax.nn.sigmoid(c)
    o_ref[...] = _dot(s, w_ref[...], HIGHEST) + b_ref[...]


def _modulation(cond, w_mod, b_mod):
    tn = 1536
    return pl.pallas_call(
        _modulation_kernel,
        out_shape=jax.ShapeDtypeStruct((DEPTH, MOD_ROWS, 6 * D_MODEL), F32),
        grid=(DEPTH, 6 * D_MODEL // tn),
        in_specs=[
            pl.BlockSpec((MOD_ROWS, D_MODEL), lambda l, j: (0, 0)),
            pl.BlockSpec((None, D_MODEL, tn), lambda l, j: (l, 0, j)),
            pl.BlockSpec((None, 1, tn), lambda l, j: (l, 0, j)),
        ],
        out_specs=pl.BlockSpec((None, MOD_ROWS, tn), lambda l, j: (l, 0, j)),
        compiler_params=_params(("parallel", "parallel")),
        name="modulation",
    )(cond, w_mod, b_mod.reshape(DEPTH, 1, 6 * D_MODEL))


def _norm_mod(y, g, sc, sh):
    ms = jnp.mean(y * y, axis=-1, keepdims=True)
    return (y * lax.rsqrt(ms + EPS) * g) * (1.0 + sc) + sh


def _rope_rotate(x):
    w = x.shape[-1]
    lane = lax.broadcasted_iota(jnp.int32, x.shape, 1)
    nxt = pltpu.roll(x, w - 1, 1)
    prv = pltpu.roll(x, 1, 1)
    return jnp.where((lane & 1) == 0, -nxt, prv)


def _is_ctx_tile():
    return pl.program_id(0) < CTX_TILES


def _pair_specs(width):
    return [pl.BlockSpec((TM, width), lambda i: (jnp.minimum(i, CTX_TILES - 1), 0)),
            pl.BlockSpec((TM, width), lambda i: (jnp.maximum(i - CTX_TILES, 0), 0))]


def _pair_value(ctx_ref, lat_ref, cols=slice(None)):
    return jnp.where(_is_ctx_tile(), ctx_ref[:, cols], lat_ref[:, cols])


def _norm_proj_kernel(*refs, specs, n_gain, y_pair):
    n_y = 2 if y_pair else 1
    y = _pair_value(refs[0], refs[1]) if y_pair else refs[0][...]
    g_ref, sc_ref, sh_ref, w_ref, cos_ref, sin_ref, bd_ref = refs[n_y:n_y + 7]
    gain_refs = refs[n_y + 7:n_y + 7 + n_gain]
    p_ref = refs[n_y + 7 + n_gain]
    out_refs = refs[n_y + 8 + n_gain:]
    h = _norm_mod(y, g_ref[...], sc_ref[...], sh_ref[...])
    p_ref[...] = _dot(h.astype(BF16), w_ref[...])
    cos = cos_ref[...]
    sin = sin_ref[...]
    for (col, width, gi, rope, scale, _, heads), o_ref in zip(specs, out_refs):
        for c0 in range(0, width, LANES):
            x = p_ref[:, col + c0:col + c0 + LANES]
            if gi is not None:
                xx = x * x
                xx_hi = xx.astype(BF16)
                xx_lo = (xx - xx_hi.astype(F32)).astype(BF16)
                ss = _dot(xx_hi, bd_ref[...]) + _dot(xx_lo, bd_ref[...])
                x = x * lax.rsqrt(ss * (1.0 / HD) + EPS) * gain_refs[gi][...]
            if rope:
                x = x * cos + _rope_rotate(x) * sin
            if scale != 1.0:
                x = x * scale
            if heads is None:
                o_ref[:, c0:c0 + LANES] = x.astype(o_ref.dtype)
            else:
                hw = width // heads
                per = LANES // hw

                @pl.when(_is_ctx_tile())
                def _(x=x, o_ref=o_ref, c0=c0, hw=hw, per=per):
                    for u in range(per):
                        o_ref[(c0 // LANES) * per + u] = x[:, u * hw:(u + 1) * hw].astype(o_ref.dtype)


def _norm_proj(y, gain, scale, shift, w_bf16, rope_cos, rope_sin, gains, specs):
    n = w_bf16.shape[1]
    y_pair = isinstance(y, tuple)
    bd = jnp.asarray(np.kron(np.eye(LANES // HD), np.ones((HD, HD))), BF16)
    vec = pl.BlockSpec((None, 1, D_MODEL), lambda i: (_mod_row(i), 0, 0))
    rope_spec = pl.BlockSpec((TM, LANES), lambda i: (_rope_block(i), 0))
    in_specs = _pair_specs(D_MODEL) if y_pair else [pl.BlockSpec((TM, D_MODEL), lambda i: (i, 0))]
    in_specs += [pl.BlockSpec((1, D_MODEL), lambda i: (0, 0)),
                 vec, vec, pl.BlockSpec((D_MODEL, n), lambda i: (0, 0)),
                 rope_spec, rope_spec, pl.BlockSpec((LANES, LANES), lambda i: (0, 0))]
    in_specs += [pl.BlockSpec((1, LANES), lambda i: (0, 0)) for _ in gains]
    out_shape = [jax.ShapeDtypeStruct((T_ALL, n), F32)]
    out_specs = [pl.BlockSpec((TM, n), lambda i: (i, 0))]
    for (_, width, _, _, _, dtype, heads) in specs:
        if heads is None:
            out_shape.append(jax.ShapeDtypeStruct((T_ALL, width), dtype))
            out_specs.append(pl.BlockSpec((TM, width), lambda i: (i, 0)))
        else:
            out_shape.append(jax.ShapeDtypeStruct((BATCH, heads, SEQ, width // heads), dtype))
            out_specs.append(pl.BlockSpec((None, heads, SEQ, width // heads),
                                          lambda i: (jnp.minimum(i, CTX_TILES - 1), 0, 0, 0)))
    ys = list(y) if y_pair else [y]
    return pl.pallas_call(
        functools.partial(_norm_proj_kernel, specs=specs, n_gain=len(gains), y_pair=y_pair),
        out_shape=out_shape,
        grid=(N_TILES,),
        in_specs=in_specs,
        out_specs=out_specs,
        compiler_params=_params(("arbitrary",)),
        name="norm_proj",
    )(*ys, gain.reshape(1, D_MODEL), scale, shift, w_bf16, rope_cos, rope_sin, bd, *gains)


def _rope_tables():
    half = HD // 2
    freqs = 1.0 / (ROPE_THETA ** (jnp.arange(0, half, 2, dtype=F32) / half))
    t = jnp.arange(DEC_SEQ)
    rows = (t // GRID_W).astype(F32)
    cols = (t % GRID_W).astype(F32)
    ang = jnp.concatenate([rows[:, None] * freqs, cols[:, None] * freqs], axis=-1)
    cos = jnp.repeat(jnp.cos(ang), 2, axis=-1)
    sin = jnp.repeat(jnp.sin(ang), 2, axis=-1)
    cos = jnp.concatenate([jnp.tile(cos, (1, LANES // HD)), jnp.ones((TM, LANES), F32)], axis=0)
    sin = jnp.concatenate([jnp.tile(sin, (1, LANES // HD)), jnp.zeros((TM, LANES), F32)], axis=0)
    return cos, sin


def _lane_slice(ref, h, width=HD):
    per = LANES // width
    blk = ref[:, (h // per) * LANES:(h // per + 1) * LANES]
    if per == 1:
        return blk
    return blk[:, (h % per) * width:(h % per + 1) * width]


def _softmax_parts(scores):
    m = None
    for s in scores:
        ms = jnp.max(s, axis=-1, keepdims=True)
        m = ms if m is None else jnp.maximum(m, ms)
    ps = [jnp.exp(s - m) for s in scores]
    l = None
    for p in ps:
        ls = jnp.sum(p, axis=-1, keepdims=True)
        l = ls if l is None else l + ls
    return ps, l


def _attn_std_kernel(*refs, group, n_kv, has_cache, has_bias, bq):
    it = iter(refs)
    q_ref, kn_ref, vn_ref = next(it), next(it), next(it)
    kc_ref = vc_ref = b_ref = None
    if has_cache:
        kc_ref, vc_ref = next(it), next(it)
    if has_bias:
        b_ref = next(it)
    o_ref = next(it)
    outs = []
    for g in range(n_kv):
        qs = jnp.concatenate([_lane_slice(q_ref, g * group + j) for j in range(group)], axis=0)
        kn = _lane_slice(kn_ref, g)
        vn = _lane_slice(vn_ref, g)
        s_new = _dot_nt(qs, kn)
        if has_bias:
            s_new = s_new + b_ref[g]
        scores = [s_new]
        if has_cache:
            scores.append(_dot_nt(qs, kc_ref[g]))
        ps, l = _softmax_parts(scores)
        o = _dot(ps[0].astype(BF16), vn)
        if has_cache:
            o = o + _dot(ps[1].astype(BF16), vc_ref[g])
        o = o / l
        for j in range(group):
            outs.append(o[j * bq:(j + 1) * bq])
    o_ref[...] = jnp.concatenate(outs, axis=1).astype(o_ref.dtype)


def _attn_diff_kernel(*refs, has_cache):
    it = iter(refs)
    lam_ref, q_ref, kn_ref, vn_ref = next(it), next(it), next(it), next(it)
    kc_ref = vc_ref = None
    if has_cache:
        kc_ref, vc_ref = next(it), next(it)
    g_ref, o_ref = next(it), next(it)
    lam = lam_ref[0]
    post = lam_ref[1]
    outs = []
    for h in range(D_HEADS):
        pd_new, pd_c = None, None
        for j in range(2):
            f = 2 * h + j
            qs = _lane_slice(q_ref, f)
            scores = [_dot_nt(qs, _lane_slice(kn_ref, f))]
            if has_cache:
                scores.append(_dot_nt(qs, kc_ref[f]))
            ps, l = _softmax_parts(scores)
            r = 1.0 / l
            if j == 0:
                pd_new = ps[0] * r
                pd_c = ps[1] * r if has_cache else None
            else:
                r = r * lam
                pd_new = pd_new - ps[0] * r
                pd_c = pd_c - ps[1] * r if has_cache else None
        o = _dot(pd_new.astype(BF16), _lane_slice(vn_ref, h, D_VDIM))
        if has_cache:
            o = o + _dot(pd_c.astype(BF16), vc_ref[h])
        ms = jnp.mean(o * o, axis=-1, keepdims=True)
        outs.append(o * lax.rsqrt(ms + EPS) * g_ref[...] * post)
    o_ref[...] = jnp.concatenate(outs, axis=1).astype(o_ref.dtype)


def _attention(q, kn, vn, *, ctx, group=1, n_kv=1, cache=None, bias=None, diff=None, bq=256):
    if ctx:
        nb, sq, row0 = BATCH, SEQ, 0
    else:
        nb, sq, row0 = DEC_BATCH, DEC_SEQ, T_CTX
    nq = sq // bq
    qb0 = row0 // bq
    kb0 = row0 // sq
    wq, wk, wv = q.shape[1], kn.shape[1], vn.shape[1]
    in_specs = [
        pl.BlockSpec((bq, wq), lambda b, i: (qb0 + b * nq + i, 0)),
        pl.BlockSpec((sq, wk), lambda b, i: (kb0 + b, 0)),
        pl.BlockSpec((sq, wv), lambda b, i: (kb0 + b, 0)),
    ]
    args = [q, kn, vn]
    if cache is not None:
        kc, vc = cache
        in_specs += [pl.BlockSpec((None,) + kc.shape[1:], lambda b, i: (b, 0, 0, 0)),
                     pl.BlockSpec((None,) + vc.shape[1:], lambda b, i: (b, 0, 0, 0))]
        args += [kc, vc]
    if diff is None:
        if bias is not None:
            in_specs.append(pl.BlockSpec((bias.shape[0], bq, sq), lambda b, i: (0, i, 0)))
            args.append(bias)
        body = functools.partial(_attn_std_kernel, group=group, n_kv=n_kv, has_cache=cache is not None,
                                 has_bias=bias is not None, bq=bq)
    else:
        lam_vec, gain = diff
        in_specs = [pl.BlockSpec(memory_space=pltpu.SMEM)] + in_specs
        args = [lam_vec] + args
        in_specs.append(pl.BlockSpec((1, D_VDIM), lambda b, i: (0, 0)))
        args.append(gain.reshape(1, D_VDIM))
        body = functools.partial(_attn_diff_kernel, has_cache=cache is not None)
    return pl.pallas_call(
        body,
        out_shape=jax.ShapeDtypeStruct((nb * sq, 512), BF16),
        grid=(nb, nq),
        in_specs=in_specs,
        out_specs=pl.BlockSpec((bq, 512), lambda b, i: (b * nq + i, 0)),
        compiler_params=_params(("parallel", "parallel")),
        name="attention",
    )(*args)


GRID_ROWS = DEC_SEQ // GRID_W
NA_WIN_ROWS = min(NA_ROWS, GRID_ROWS)


def _na_bias_kernel(t_ref, o_ref):
    outside = jnp.full((GRID_W, GRID_W), NEG_BIG, F32)
    for qr in range(GRID_ROWS):
        r0 = min(max(qr - NA_WIN_ROWS // 2, 0), GRID_ROWS - NA_WIN_ROWS)
        parts = [t_ref[kr - qr + (NA_ROWS - 1)] if r0 <= kr < r0 + NA_WIN_ROWS else outside
                 for kr in range(GRID_ROWS)]
        o_ref[qr * GRID_W:(qr + 1) * GRID_W, :] = jnp.concatenate(parts, axis=1)


def _neighbourhood_bias(rpb):
    c = np.arange(GRID_W)
    c0 = np.clip(c - NA_COLS // 2, 0, GRID_W - NA_COLS)
    col_ok = (c[None, :] >= c0[:, None]) & (c[None, :] < c0[:, None] + NA_COLS)
    dc = np.clip(c[None, :] - c[:, None], 1 - NA_COLS, NA_COLS - 1) + (NA_COLS - 1)
    oh_c = jnp.asarray(dc[..., None] == np.arange(2 * NA_COLS - 1), F32)
    by_col = jnp.einsum("hrd,qkd->hrqk", rpb.astype(F32), oh_c, precision=HIGHEST)
    by_col = jnp.where(jnp.asarray(col_ok), by_col, NEG_BIG)
    n_dr = 2 * NA_ROWS - 1
    return pl.pallas_call(
        _na_bias_kernel,
        out_shape=jax.ShapeDtypeStruct((C_HEADS, DEC_SEQ, DEC_SEQ), F32),
        grid=(C_HEADS,),
        in_specs=[pl.BlockSpec((None, n_dr, GRID_W, GRID_W), lambda h: (h, 0, 0, 0))],
        out_specs=pl.BlockSpec((None, DEC_SEQ, DEC_SEQ), lambda h: (h, 0, 0)),
        compiler_params=_params(("parallel",)),
        name="na_bias",
    )(by_col)


def _log_sigmoid(x):
    return jnp.minimum(x, 0.0) - jnp.log1p(jnp.exp(-jnp.abs(x)))


def _mlstm_kernel(q_ref, k_ref, v_ref, g_ref, gb_ref, c0_ref, n0_ref, m0_ref, h_ref, c_ref, n_ref, m_ref, hb_ref,
                  *, seq):
    L = B_CHUNK
    nc = seq // L
    row = lax.broadcasted_iota(jnp.int32, (L, L), 0)
    col = lax.broadcasted_iota(jnp.int32, (L, L), 1)
    keeps = (col <= row, col >= row)
    k_scale = B_DK ** -0.5
    c_ref[...] = c0_ref[...]
    n_ref[...] = n0_ref[...]
    m_ref[...] = m0_ref[...]

    def step(j):
        for d in range(2):
            keep = keeps[d]
            c = j if d == 0 else nc - 1 - j
            off = c * L if isinstance(c, int) else pl.multiple_of(c * L, L)
            gates = g_ref[pl.ds(off, L), :] + gb_ref[...]
            cum = _dot(keep.astype(F32), _log_sigmoid(gates), HIGHEST)
            cum_t = cum.T
            gates_t = gates.T
            out_ref = h_ref if d == 0 else hb_ref
            for h in range(B_HEADS):
                ci = (2 * d) * B_HEADS + h
                cf = (2 * d + 1) * B_HEADS + h
                hs = slice(h * B_DK, (h + 1) * B_DK)
                C = c_ref[d, h]
                n = n_ref[d, h]
                m = m_ref[d, h][:, 0:1]
                qc = q_ref[pl.ds(off, L), hs]
                kc = k_ref[pl.ds(off, L), hs] * k_scale
                vc = v_ref[pl.ds(off, L), hs]
                b_col = cum[:, cf:cf + 1]
                i_col = gates[:, ci:ci + 1]
                b_row = cum_t[cf:cf + 1, :]
                i_row = gates_t[ci:ci + 1, :]
                dlog = jnp.where(keep, b_col - b_row + i_row, -jnp.inf)
                inter = b_col + m
                m_t = jnp.maximum(inter, jnp.max(dlog, axis=-1, keepdims=True))
                w_intra = jnp.exp(dlog - m_t)
                w_inter = jnp.exp(inter - m_t)
                qb = qc.astype(BF16)
                vb = vc.astype(BF16)
                qk = _dot_nt(qb, kc.astype(BF16)) * w_intra
                num = _dot(qk.astype(BF16), vb) + w_inter * _dot(qb, C.astype(BF16))
                den = jnp.sum(qk, axis=-1, keepdims=True) + w_inter * jnp.sum(qc * n, axis=-1, keepdims=True)
                out_ref[pl.ds(off, L), hs] = num / jnp.maximum(jnp.abs(den), jnp.exp(-m_t))
                b_last = b_col[L - 1:L, :] if d == 0 else b_col[0:1, :]
                end_col = b_last - b_col + i_col
                m_new = jnp.maximum(b_last + m, jnp.max(end_col, axis=0, keepdims=True))
                w_end = jnp.exp(end_col - m_new)
                decay = jnp.exp(b_last + m - m_new)
                kw = kc * w_end
                c_ref[d, h] = decay * C + _dot_tn(kw.astype(BF16), vb)
                n_ref[d, h] = decay * n + jnp.sum(kw, axis=0, keepdims=True)
                m_ref[d, h] = jnp.broadcast_to(m_new, (1, LANES))

    if nc <= 2:
        for j in range(nc):
            step(j)
    else:
        def body(j, carry):
            step(j)
            return carry

        lax.fori_loop(0, nc, body, 0)
    h_ref[...] = h_ref[...] + hb_ref[...]


def _mlstm(p, gate_bias, c0, n0, m0, *, ctx):
    if ctx:
        nb, seq, blk0 = BATCH, SEQ, 0
    else:
        nb, seq, blk0 = DEC_BATCH, DEC_SEQ, T_CTX // DEC_SEQ
    w = B_HEADS * B_DK

    def cols(c0_, width):
        return pl.BlockSpec((seq, width), lambda b: (blk0 + b, c0_ // width))

    gb = jnp.zeros((1, LANES), F32).at[0, :4 * B_HEADS].set(gate_bias.reshape(-1).astype(F32))
    st = lambda shape: pl.BlockSpec((None,) + shape, lambda b: (b, 0, 0, 0, 0))
    return pl.pallas_call(
        functools.partial(_mlstm_kernel, seq=seq),
        out_shape=[
            jax.ShapeDtypeStruct((nb * seq, w), F32),
            jax.ShapeDtypeStruct((nb, 2, B_HEADS, B_DK, B_DV), F32),
            jax.ShapeDtypeStruct((nb, 2, B_HEADS, 1, B_DK), F32),
            jax.ShapeDtypeStruct((nb, 2, B_HEADS, 1, LANES), F32),
        ],
        grid=(nb,),
        in_specs=[
            cols(EV_BQ, w), cols(EV_BK, w), cols(EV_BV, w), cols(EV_BG, LANES),
            pl.BlockSpec((1, LANES), lambda b: (0, 0)),
            st((2, B_HEADS, B_DK, B_DV)), st((2, B_HEADS, 1, B_DK)), st((2, B_HEADS, 1, LANES)),
        ],
        out_specs=[
            pl.BlockSpec((seq, w), lambda b: (b, 0)),
            st((2, B_HEADS, B_DK, B_DV)), st((2, B_HEADS, 1, B_DK)), st((2, B_HEADS, 1, LANES)),
        ],
        scratch_shapes=[pltpu.VMEM((seq, w), F32)],
        compiler_params=_params(("parallel",)),
        name="mlstm",
    )(p, p, p, p, gb, c0, n0, m0)


def _merge_value(refs, even, y_pair):
    a_ctx, a_lat, b_ctx, b_lat = refs[:4]
    rest = refs[4:]
    if even:
        bo_ref, ng_ref = rest[:2]
        rest = rest[2:]
        parts = [_pair_value(a_ctx, a_lat)]
        for h in range(B_HEADS):
            hs = slice(h * B_DV, (h + 1) * B_DV)
            x = _pair_value(b_ctx, b_lat, hs)
            ms = jnp.mean(x * x, axis=-1, keepdims=True)
            xn = x * lax.rsqrt(ms + EPS) * ng_ref[:, hs]
            parts.append((jax.nn.sigmoid(bo_ref[:, hs]) * xn).astype(BF16))
    else:
        parts = [_pair_value(a_ctx, a_lat), _pair_value(b_ctx, b_lat)]
    w_ref = rest[0]
    y = _pair_value(rest[1], rest[2]) if y_pair else rest[1][...]
    g_ref = rest[-1]
    cat = jnp.concatenate(parts, axis=1)
    return y + g_ref[...] * _dot(cat, w_ref[...])


def _merge_specs(y, gate, w_bf16, a, b, p, norm_gain):
    in_specs = _pair_specs(512) + _pair_specs(512)
    args = [*a, *b]
    if p is not None:
        in_specs += [pl.BlockSpec((TM, 512), lambda i: (i, EV_BO // 512)), pl.BlockSpec((1, 512), lambda i: (0, 0))]
        args += [p, norm_gain.reshape(1, 512)]
    in_specs.append(pl.BlockSpec((D_MODEL, D_MODEL), lambda i: (0, 0)))
    args.append(w_bf16)
    if isinstance(y, tuple):
        in_specs += _pair_specs(D_MODEL)
        args += list(y)
    else:
        in_specs.append(pl.BlockSpec((TM, D_MODEL), lambda i: (i, 0)))
        args.append(y)
    in_specs.append(pl.BlockSpec((None, 1, D_MODEL), lambda i: (_mod_row(i), 0, 0)))
    args.append(gate)
    return in_specs, args


SLAB = D_MODEL // LANES


def _load_slabs(ref, rows):
    return jnp.concatenate([ref[pl.ds(c, rows, stride=SLAB), :] for c in range(SLAB)], axis=1)


def _store_slabs(ref, x):
    for c in range(SLAB):
        ref[pl.ds(c, x.shape[0], stride=SLAB), :] = x[:, c * LANES:(c + 1) * LANES]


def _slab(ref, idx):
    return ref.at[pl.ds(pl.multiple_of(idx * SLAB, SLAB), SLAB)]


def _merge_router_kernel(*refs, even, y_pair):
    n_merge = 4 + (2 if even else 0) + 1 + (2 if y_pair else 1) + 1
    merge_refs = refs[:n_merge]
    (g_ref, sc_ref, sh_ref, whi_ref, wlo_ref, b_ref,
     y_out_ref, h_ref, ti_ref, tp_ref, rk_ref, cnt_ref, base_ref) = refs[n_merge:]

    @pl.when(pl.program_id(0) == 0)
    def _():
        base_ref[...] = jnp.zeros(base_ref.shape, F32)

    y = _merge_value(merge_refs, even, y_pair)
    y_out_ref[...] = y
    h = _norm_mod(y, g_ref[...], sc_ref[...], sh_ref[...])
    _store_slabs(h_ref, h)
    h_hi = h.astype(BF16)
    h_lo = (h - h_hi.astype(F32)).astype(BF16)
    logits = (_dot(h_hi, whi_ref[...]) + (_dot(h_hi, wlo_ref[...]) + _dot(h_lo, whi_ref[...]))
              + b_ref[...])
    lane = lax.broadcasted_iota(jnp.int32, logits.shape, 1)
    lane_f = lane.astype(F32)
    vals, idxs = [], []
    for _ in range(TOP_K):
        mx = jnp.max(logits, axis=-1, keepdims=True)
        ix = jnp.min(jnp.where(logits == mx, lane_f, float(LANES)), axis=-1, keepdims=True)
        vals.append(mx)
        idxs.append(ix)
        logits = jnp.where(lane_f == ix, -jnp.inf, logits)
    es = [jnp.exp(v - vals[0]) for v in vals]
    tot = es[0] + es[1] + es[2] + es[3]
    ti = jnp.zeros(logits.shape, F32)
    tp = jnp.zeros(logits.shape, F32)
    for k in range(TOP_K):
        ti = jnp.where(lane == k, idxs[k], ti)
        tp = jnp.where(lane == k, es[k] / tot, tp)
    ti_ref[...] = ti.T[0:8, :].astype(jnp.int32)
    tp_ref[...] = tp
    onehots = [(lane_f == ix).astype(F32) for ix in idxs]
    cnt = onehots[0] + onehots[1] + onehots[2] + onehots[3]
    row = lax.broadcasted_iota(jnp.int32, (TM, TM), 0)
    col = lax.broadcasted_iota(jnp.int32, (TM, TM), 1)
    before = _dot((col < row).astype(BF16), cnt.astype(BF16)) + base_ref[...]
    rk = jnp.zeros(logits.shape, F32)
    for k in range(TOP_K):
        rk = jnp.where(lane == k, jnp.sum(onehots[k] * before, axis=-1, keepdims=True), rk)
    rk_ref[...] = rk.T[0:8, :].astype(jnp.int32)
    base_ref[...] = base_ref[...] + jnp.sum(cnt, axis=0, keepdims=True)
    cnt_ref[...] = base_ref[...]


def _merge_router(y, gate, w_out_bf16, a, b, gain, scale, shift, rw, rb, *, p=None, norm_gain=None):
    merge_in_specs, merge_args = _merge_specs(y, gate, w_out_bf16, a, b, p, norm_gain)
    vec = pl.BlockSpec((None, 1, D_MODEL), lambda i: (_mod_row(i), 0, 0))
    rw_p = jnp.zeros((D_MODEL, LANES), F32).at[:, :N_EXPERTS].set(rw)
    rb_p = jnp.full((1, LANES), NEG_BIG, F32).at[0, :N_EXPERTS].set(rb)
    rw_hi = rw_p.astype(BF16)
    tile = lambda w: pl.BlockSpec((TM, w), lambda i: (i, 0))
    by_choice = pl.BlockSpec((None, 8, TM), lambda i: (i, 0, 0))
    return pl.pallas_call(
        functools.partial(_merge_router_kernel, even=p is not None, y_pair=isinstance(y, tuple)),
        out_shape=[jax.ShapeDtypeStruct((T_ALL, D_MODEL), F32),
                   jax.ShapeDtypeStruct((T_ALL * SLAB, LANES), F32),
                   jax.ShapeDtypeStruct((N_TILES, 8, TM), jnp.int32),
                   jax.ShapeDtypeStruct((T_ALL, LANES), F32),
                   jax.ShapeDtypeStruct((N_TILES, 8, TM), jnp.int32),
                   jax.ShapeDtypeStruct((1, LANES), F32)],
        grid=(N_TILES,),
        in_specs=merge_in_specs + [
            pl.BlockSpec((1, D_MODEL), lambda i: (0, 0)), vec, vec,
            pl.BlockSpec((D_MODEL, LANES), lambda i: (0, 0)), pl.BlockSpec((D_MODEL, LANES), lambda i: (0, 0)),
            pl.BlockSpec((1, LANES), lambda i: (0, 0))],
        out_specs=[tile(D_MODEL), pl.BlockSpec((TM * SLAB, LANES), lambda i: (i, 0)), by_choice, tile(LANES),
                   by_choice, pl.BlockSpec((1, LANES), lambda i: (0, 0))],
        scratch_shapes=[pltpu.VMEM((1, LANES), F32)],
        compiler_params=_params(("arbitrary",)),
        name="merge_router",
    )(*merge_args, gain.reshape(1, D_MODEL), scale, shift, rw_hi, (rw_p - rw_hi.astype(F32)).astype(BF16), rb_p)


def _route_plan(top_i, rank, counts):
    experts = jnp.arange(N_EXPERTS, dtype=jnp.int32)
    padded = ((counts + MOE_TM - 1) // MOE_TM) * MOE_TM
    seg_end = jnp.cumsum(padded)
    seg_start = seg_end - padded
    pos = rank
    for e in range(N_EXPERTS - 1):
        pos = pos + jnp.where(top_i > e, padded[e], 0)
    n_active = seg_end[-1] // MOE_TM
    fill = jnp.concatenate([seg_start + counts, padded - counts, n_active[None]]).astype(jnp.int32)
    tile_start = jnp.arange(MOE_TILES, dtype=jnp.int32) * MOE_TM
    tile_expert = jnp.sum((seg_end[None, :] <= tile_start[:, None]).astype(jnp.int32), axis=1)
    last = jnp.sum((seg_end <= (n_active - 1) * MOE_TM).astype(jnp.int32))
    tile_expert = jnp.minimum(jnp.where(tile_start < seg_end[-1], tile_expert, last), N_EXPERTS - 1)
    owns = (padded > 0).astype(jnp.int32)
    run_of_expert = jnp.cumsum(owns) - 1
    run_expert = jnp.sum(jnp.where((run_of_expert[None, :] == experts[:, None]) & (owns[None, :] > 0),
                                   experts[None, :], 0), axis=1)
    runs = jnp.concatenate([run_expert, jnp.sum(owns)[None]]).astype(jnp.int32)
    tile_run = jnp.sum(jnp.where(tile_expert[:, None] == experts[None, :], run_of_expert[None, :], 0), axis=1)
    experts_plan = (tile_expert.astype(jnp.int32), n_active.reshape(1).astype(jnp.int32),
                    tile_run.astype(jnp.int32), runs)
    return pos.astype(jnp.int32), fill, experts_plan


DMA_UNROLL = 4
DMA_QUEUES = 2


def _wait_slabs(ref, n_slabs, sem):
    view = ref.at[pl.ds(0, n_slabs * SLAB)]
    pltpu.make_async_copy(view, view, sem).wait()


def _dispatch_kernel(pos_ref, fill_ref, h_ref, xs_ref, inv_ref, sem):
    i = pl.program_id(0)
    base = i * (TM * TOP_K)

    def issue(j, carry):
        for u in range(DMA_UNROLL):
            t = j * DMA_UNROLL + u
            for k in range(TOP_K):
                dst = _slab(xs_ref, pos_ref[base + k * TM + t])
                pltpu.make_async_copy(_slab(h_ref, t), dst, sem).start(priority=k % DMA_QUEUES)
        return carry

    lax.fori_loop(0, TM // DMA_UNROLL, issue, 0)

    def invert(j, carry):
        for u in range(4 * DMA_UNROLL):
            a = base + j * (4 * DMA_UNROLL) + u
            inv_ref[pos_ref[a]] = a
        return carry

    lax.fori_loop(0, TM * TOP_K // (4 * DMA_UNROLL), invert, 0)

    @pl.when(i == 0)
    def _():
        def per_expert(e, total):
            start = fill_ref[e]
            n = fill_ref[N_EXPERTS + e]

            def one(r, carry):
                slot = start + r
                inv_ref[slot] = N_ASSIGN + slot % MOE_TM
                pltpu.make_async_copy(_slab(h_ref, 0), _slab(xs_ref, slot), sem).start()
                return carry

            lax.fori_loop(0, n, one, 0)
            return total + n

        total = lax.fori_loop(0, N_EXPERTS, per_expert, 0)

        n_active = fill_ref[2 * N_EXPERTS]

        def unused_tile(ti, carry):
            dst = xs_ref.at[pl.ds(pl.multiple_of(ti * (MOE_TM * SLAB), MOE_TM * SLAB), MOE_TM * SLAB)]
            pltpu.make_async_copy(h_ref, dst, sem).start()
            return carry

        lax.fori_loop(n_active, MOE_TILES, unused_tile, 0)

        def unused_slot(slot, carry):
            inv_ref[slot] = N_ASSIGN
            return carry

        lax.fori_loop(n_active * MOE_TM, MOE_ROWS, unused_slot, 0)
        total = total + (MOE_TILES - n_active) * MOE_TM

        @pl.when(total > 0)
        def _():
            _wait_slabs(xs_ref, total, sem)

    _wait_slabs(xs_ref, TM * TOP_K, sem)


def _dispatch(h_slabs, pos, fill):
    grid_spec = pltpu.PrefetchScalarGridSpec(
        num_scalar_prefetch=2,
        grid=(N_TILES,),
        in_specs=[pl.BlockSpec((TM * SLAB, LANES), lambda i, pos, fill: (i, 0))],
        out_specs=[pl.BlockSpec(memory_space=pl.ANY), pl.BlockSpec(memory_space=pltpu.SMEM)],
        scratch_shapes=[pltpu.SemaphoreType.DMA],
    )
    return pl.pallas_call(
        _dispatch_kernel,
        out_shape=[jax.ShapeDtypeStruct((MOE_ROWS * SLAB, LANES), F32),
                   jax.ShapeDtypeStruct((MOE_ROWS,), jnp.int32)],
        grid_spec=grid_spec,
        compiler_params=_params(("arbitrary",)),
        name="moe_dispatch",
    )(pos, fill, h_slabs)


MOE_HALF = MOE_TM // 2


def _moe_kernel(te_ref, na_ref, ts_ref, ex_ref, inv_ref, x_ref, wgu_hbm, bgu_ref, wd_hbm, bd_ref, out_hbm,
                wgu_f32, wd_f32, wgu_bf, wd_bf, obuf0, obuf1, sems, osems, *, layer):
    i = pl.program_id(0)
    s = ts_ref[i]
    first = (i == 0) | (s != ts_ref[jnp.maximum(i - 1, 0)])
    n_active = na_ref[0]

    def start_rows(buf, half, tile):
        for r in range(MOE_HALF):
            dst = _slab(out_hbm, inv_ref[tile * MOE_TM + half * MOE_HALF + r])
            pltpu.make_async_copy(buf.at[pl.ds(r * SLAB, SLAB)], dst, osems.at[half]).start(
                priority=r % DMA_QUEUES)

    def wait_rows(half):
        _wait_slabs(out_hbm, MOE_HALF, osems.at[half])

    def ffn_tile(deferred):
        if deferred:
            start_rows(obuf1, 1, i - 1)
        x = _load_slabs(x_ref, MOE_TM).astype(BF16)
        gu = _dot(x, wgu_bf[...]) + bgu_ref[...]
        gate = jnp.minimum(gu[:, :D_FF], SWIGLU_LIMIT)
        up = jnp.clip(gu[:, D_FF:], -SWIGLU_LIMIT, SWIGLU_LIMIT)
        act = ((up + 1.0) * gate * jax.nn.sigmoid(SWIGLU_ALPHA * gate)).astype(BF16)
        o0 = _dot(act[:MOE_HALF], wd_bf[...]) + bd_ref[...]
        if deferred:
            wait_rows(0)
        _store_slabs(obuf0, o0)
        start_rows(obuf0, 0, i)
        o1 = _dot(act[MOE_HALF:], wd_bf[...]) + bd_ref[...]
        if deferred:
            wait_rows(1)
        _store_slabs(obuf1, o1)

    def weight_copies(slot):
        e = ex_ref[slot]
        b = slot % 2
        return (pltpu.make_async_copy(wgu_hbm.at[layer, e], wgu_f32.at[b], sems.at[0, b]),
                pltpu.make_async_copy(wd_hbm.at[layer, e], wd_f32.at[b], sems.at[1, b]))

    @pl.when(i == 0)
    def _():
        for cp in weight_copies(0):
            cp.start()

    @pl.when(first)
    def _():
        for cp in weight_copies(s):
            cp.wait()

        @pl.when(s + 1 < ex_ref[N_EXPERTS])
        def _():
            for cp in weight_copies(s + 1):
                cp.start()

        b = s % 2
        wgu_bf[...] = wgu_f32[b].astype(BF16)
        wd_bf[...] = wd_f32[b].astype(BF16)

    @pl.when(i == 0)
    def _():
        ffn_tile(False)

    @pl.when((i > 0) & (i < n_active))
    def _():
        ffn_tile(True)

    @pl.when(i == n_active)
    def _():
        wait_rows(0)
        start_rows(obuf1, 1, i - 1)
        wait_rows(1)
        for half, buf in enumerate((obuf0, obuf1)):
            dst = out_hbm.at[pl.ds((N_ASSIGN + half * MOE_HALF) * SLAB, MOE_HALF * SLAB)]
            pltpu.make_async_copy(buf, dst, osems.at[half]).start()
        wait_rows(0)
        wait_rows(1)


def _moe_experts(layer, xs, plan, inv, w_gu, b_gu, w_down, b_down):
    const = lambda i, te, na, ts, ex, inv: (layer, te[i], 0, 0)
    grid_spec = pltpu.PrefetchScalarGridSpec(
        num_scalar_prefetch=5,
        grid=(MOE_TILES,),
        in_specs=[
            pl.BlockSpec((MOE_TM * SLAB, LANES), lambda i, te, na, ts, ex, inv: (jnp.minimum(i, na[0] - 1), 0)),
            pl.BlockSpec(memory_space=pl.ANY),
            pl.BlockSpec((None, None, 1, 2 * D_FF), const),
            pl.BlockSpec(memory_space=pl.ANY),
            pl.BlockSpec((None, None, 1, D_MODEL), const),
        ],
        out_specs=pl.BlockSpec(memory_space=pl.ANY),
        scratch_shapes=[pltpu.VMEM((2, D_MODEL, 2 * D_FF), F32), pltpu.VMEM((2, D_FF, D_MODEL), F32),
                        pltpu.VMEM((D_MODEL, 2 * D_FF), BF16), pltpu.VMEM((D_FF, D_MODEL), BF16),
                        pltpu.VMEM((MOE_HALF * SLAB, LANES), F32), pltpu.VMEM((MOE_HALF * SLAB, LANES), F32),
                        pltpu.SemaphoreType.DMA((2, 2)), pltpu.SemaphoreType.DMA((2,))],
    )
    return pl.pallas_call(
        functools.partial(_moe_kernel, layer=layer),
        out_shape=jax.ShapeDtypeStruct(((N_ASSIGN + MOE_TM) * SLAB, LANES), F32),
        grid_spec=grid_spec,
        compiler_params=_params(("arbitrary",)),
        name="moe_experts",
    )(*plan, inv, xs, w_gu, b_gu.reshape(DEPTH, N_EXPERTS, 1, 2 * D_FF), w_down,
      b_down.reshape(DEPTH, N_EXPERTS, 1, D_MODEL))


def _combine_kernel(out_ref, y_ref, tp_ref, g_ref, *rest, final):
    if final:
        fg_ref, n_ctx_ref, n_lat_ref = rest
    else:
        (o_ref,) = rest

    tp = tp_ref[...]
    ss = jnp.zeros((TM, 1), F32)
    chunks = []
    for c in range(SLAB):
        cs = slice(c * LANES, (c + 1) * LANES)
        acc = tp[:, 0:1] * out_ref[pl.ds(c, TM, stride=SLAB), :]
        for k in range(1, TOP_K):
            acc = acc + tp[:, k:k + 1] * out_ref[pl.ds(k * TM * SLAB + c, TM, stride=SLAB), :]
        yc = y_ref[:, cs] + g_ref[:, cs] * acc
        if final:
            ss = ss + jnp.sum(yc * yc, axis=-1, keepdims=True)
            chunks.append(yc)
        else:
            o_ref[:, cs] = yc
    if final:
        inv = lax.rsqrt(ss * (1.0 / D_MODEL) + EPS)
        normed = jnp.concatenate([chunks[c] * inv * fg_ref[:, c * LANES:(c + 1) * LANES] for c in range(SLAB)],
                                 axis=1)

        @pl.when(_is_ctx_tile())
        def _():
            n_ctx_ref[...] = normed

        @pl.when(jnp.logical_not(_is_ctx_tile()))
        def _():
            n_lat_ref[...] = normed


def _combine(y, out_slabs, top_p, gate, final_gain=None):
    final = final_gain is not None
    tile = pl.BlockSpec((TM, D_MODEL), lambda i: (i, 0))
    in_specs = [pl.BlockSpec((TOP_K * TM * SLAB, LANES), lambda i: (i, 0)), tile,
                pl.BlockSpec((TM, LANES), lambda i: (i, 0)),
                pl.BlockSpec((None, 1, D_MODEL), lambda i: (_mod_row(i), 0, 0))]
    args = [out_slabs, y, top_p, gate]
    if final:
        in_specs.append(pl.BlockSpec((1, D_MODEL), lambda i: (0, 0)))
        args.append(final_gain.reshape(1, D_MODEL))
        out_shape = [jax.ShapeDtypeStruct((T_CTX, D_MODEL), F32), jax.ShapeDtypeStruct((T_LAT, D_MODEL), F32)]
        out_specs = [pl.BlockSpec((TM, D_MODEL), lambda i: (jnp.minimum(i, CTX_TILES - 1), 0)),
                     pl.BlockSpec((TM, D_MODEL), lambda i: (jnp.maximum(i - CTX_TILES, 0), 0))]
    else:
        out_shape = [jax.ShapeDtypeStruct((T_ALL, D_MODEL), F32)]
        out_specs = [tile]
    return pl.pallas_call(
        functools.partial(_combine_kernel, final=final),
        out_shape=out_shape,
        grid=(N_TILES,),
        in_specs=in_specs,
        out_specs=out_specs,
        compiler_params=_params(("arbitrary",)),
        name="moe_combine",
    )(*args)


def _moe_layer(layer, y, routed, gate, w_gu, b_gu, w_down, b_down, final_gain=None):
    h_slabs, top_i, top_p, rank, counts = routed
    pos, fill, experts_plan = _route_plan(top_i[:, :TOP_K].reshape(-1), rank[:, :TOP_K].reshape(-1),
                                          counts[0, :N_EXPERTS].astype(jnp.int32))
    xs, inv = _dispatch(h_slabs, pos, fill)
    out = _moe_experts(layer, xs, experts_plan, inv, w_gu, b_gu, w_down, b_down)
    return _combine(y, out, top_p, gate, final_gain)


def kernel(x_prompt, x_sample, c, cache_a_k, cache_a_v, state_b_C, state_b_n, state_b_m, cache_c_k, cache_c_v, cache_d_k, cache_d_v, c_ctx, w_mod, b_mod, norm1_g, norm2_g, w_in_even, w_out_even, a_q_gain, a_k_gain, b_gate_bias, b_norm_gain, w_in_odd, w_out_odd, c_rpb, d_lambda, d_norm_gain, router_w, router_b, expert_w_gu, expert_b_gu, expert_w_down, expert_b_down, final_norm_g):
    y = (x_prompt.reshape(T_CTX, D_MODEL), x_sample.reshape(T_LAT, D_MODEL))
    cond = jnp.zeros((MOD_ROWS, D_MODEL), F32).at[0].set(c_ctx).at[1:1 + DEC_BATCH].set(c)
    mod = _modulation(cond, w_mod, b_mod).reshape(DEPTH, MOD_ROWS, 6, 1, D_MODEL)
    rope_cos, rope_sin = _rope_tables()
    scale = HD ** -0.5
    outs = {}

    for layer in range(DEPTH):
        sh1, sc1, g1, sh2, sc2, g2 = (mod[layer, :, k] for k in range(6))
        j = layer // 2
        if layer % 2 == 0:
            w = w_in_even[j]
            sizes = np.cumsum([0, 512, 128, 128, 512, 512, 512, 512, 16])
            aq, ak, av, bq, bk, bv, bo, bg = (w[:, sizes[k]:sizes[k + 1]] for k in range(8))
            w_in = jnp.concatenate([aq, bo, bq, bk, bv, ak, av, bg, jnp.zeros((D_MODEL, EV_N - EV_BG - 16), F32)],
                                   axis=1).astype(BF16)
            qg = jnp.tile(a_q_gain[j], LANES // HD).reshape(1, LANES)
            kg = jnp.tile(a_k_gain[j], LANES // HD).reshape(1, LANES)
            specs = ((EV_AQ, 512, 0, True, scale, BF16, None), (EV_AK, 128, 1, True, 1.0, BF16, None),
                     (EV_AV, 128, None, False, 1.0, BF16, None),
                     (EV_AK, 128, 1, False, 1.0, F32, A_KV), (EV_AV, 128, None, False, 1.0, F32, A_KV))
            p, qa, ka, va, new_ak, new_av = _norm_proj(y, norm1_g[layer], sc1, sh1, w_in, rope_cos, rope_sin,
                                                       [qg, kg], specs)
            oa_ctx = _attention(qa, ka, va, ctx=True, group=A_HEADS // A_KV, n_kv=A_KV)
            cache = (cache_a_k[:, j].astype(BF16), cache_a_v[:, j].astype(BF16))
            oa_lat = _attention(qa, ka, va, ctx=False, group=A_HEADS // A_KV, n_kv=A_KV, cache=cache)
            zc = jnp.zeros((BATCH, 2, B_HEADS, B_DK, B_DV), F32)
            zn = jnp.zeros((BATCH, 2, B_HEADS, 1, B_DK), F32)
            zm = jnp.zeros((BATCH, 2, B_HEADS, 1, LANES), F32)
            hb_ctx, bC, bn, bm = _mlstm(p, b_gate_bias[j], zc, zn, zm, ctx=True)
            m0 = jnp.broadcast_to(state_b_m[:, j][..., None, None], (DEC_BATCH, 2, B_HEADS, 1, LANES))
            hb_lat, _, _, _ = _mlstm(p, b_gate_bias[j], state_b_C[:, j], state_b_n[:, j][:, :, :, None, :], m0,
                                     ctx=False)
            y, *routed = _merge_router(y, g1, w_out_even[j].astype(BF16), (oa_ctx, oa_lat), (hb_ctx, hb_lat),
                                       norm2_g[layer], sc2, sh2, router_w[layer], router_b[layer],
                                       p=p, norm_gain=b_norm_gain[j])
            outs.setdefault("a_k", []).append(new_ak)
            outs.setdefault("a_v", []).append(new_av)
            outs.setdefault("b_C", []).append(bC)
            outs.setdefault("b_n", []).append(bn[:, :, :, 0, :])
            outs.setdefault("b_m", []).append(bm[:, :, :, 0, 0])
        else:
            specs = ((0, 512, None, False, scale, BF16, None), (512, 512, None, False, 1.0, BF16, None),
                     (1024, 512, None, False, 1.0, BF16, None), (1536, 512, None, True, scale, BF16, None),
                     (2048, 512, None, True, 1.0, BF16, None), (2560, 512, None, False, 1.0, BF16, None),
                     (512, 512, None, False, 1.0, F32, C_HEADS), (1024, 512, None, False, 1.0, F32, C_HEADS),
                     (2048, 512, None, False, 1.0, F32, 2 * D_HEADS), (2560, 512, None, False, 1.0, F32, D_HEADS))
            p, qc, kc, vc, qd, kd, vd, new_ck, new_cv, new_dk, new_dv = _norm_proj(
                y, norm1_g[layer], sc1, sh1, w_in_odd[j].astype(BF16), rope_cos, rope_sin, [], specs)
            lam_init = 0.8 - 0.6 * math.exp(-0.3 * layer)
            lp = d_lambda[j].astype(F32)
            lam = jnp.exp(jnp.sum(lp[0] * lp[1])) - jnp.exp(jnp.sum(lp[2] * lp[3])) + lam_init
            lam_vec = jnp.stack([lam, jnp.asarray(1.0 - lam_init, F32)]).astype(F32)
            diff = (lam_vec, d_norm_gain[j])
            oc_ctx = _attention(qc, kc, vc, ctx=True, n_kv=C_HEADS)
            od_ctx = _attention(qd, kd, vd, ctx=True, diff=diff)
            bias = _neighbourhood_bias(c_rpb[j])
            oc_lat = _attention(qc, kc, vc, ctx=False, n_kv=C_HEADS, bias=bias,
                                cache=(cache_c_k[:, j].astype(BF16), cache_c_v[:, j].astype(BF16)))
            kd_cache = cache_d_k[:, j].reshape(DEC_BATCH, 2 * D_HEADS, PAST_LEN, HD).astype(BF16)
            od_lat = _attention(qd, kd, vd, ctx=False, diff=diff, cache=(kd_cache, cache_d_v[:, j].astype(BF16)))
            y, *routed = _merge_router(y, g1, w_out_odd[j].astype(BF16), (oc_ctx, oc_lat), (od_ctx, od_lat),
                                       norm2_g[layer], sc2, sh2, router_w[layer], router_b[layer])
            outs.setdefault("c_k", []).append(new_ck)
            outs.setdefault("c_v", []).append(new_cv)
            outs.setdefault("d_k", []).append(new_dk.reshape(BATCH, D_HEADS, 2, SEQ, HD))
            outs.setdefault("d_v", []).append(new_dv)
        res = _moe_layer(layer, y, routed, g2, expert_w_gu, expert_b_gu, expert_w_down, expert_b_down,
                         final_gain=final_norm_g if layer == DEPTH - 1 else None)
        y = res[0]
    y_prompt, y_sample = res
    stack = lambda k: jnp.stack(outs[k], axis=1)
    return (y_prompt.reshape(BATCH, SEQ, D_MODEL), y_sample.reshape(DEC_BATCH, DEC_SEQ, D_MODEL),
            stack("a_k"), stack("a_v"), stack("b_C"), stack("b_n"), stack("b_m"),
            stack("c_k"), stack("c_v"), stack("d_k"), stack("d_v"))
```

```python
import functools
import math

import numpy as np
import jax
import jax.numpy as jnp
from jax import lax
from jax.experimental import pallas as pl
from jax.experimental.pallas import tpu as pltpu

D_MODEL = 1024
BATCH = 32
SEQ = 256
DEPTH = 2
DEC_BATCH = 8
DEC_SEQ = 1024
PAST_LEN = 512
GRID_W = 64
HD = 64
A_HEADS = 8
A_KV = 2
B_HEADS = 4
B_DK = 128
B_DV = 128
B_CHUNK = 128
C_HEADS = 8
NA_ROWS = 8
NA_COLS = 16
D_HEADS = 4
D_VDIM = 2 * HD
N_EXPERTS = 32
TOP_K = 4
D_FF = 1024
SWIGLU_LIMIT = 7.0
SWIGLU_ALPHA = 1.702
ROPE_THETA = 10000.0
EPS = 1e-6

F32 = jnp.float32
BF16 = jnp.bfloat16
HIGHEST = lax.Precision.HIGHEST

T_CTX = BATCH * SEQ
T_LAT = DEC_BATCH * DEC_SEQ
T_ALL = T_CTX + T_LAT
TM = 256
CTX_TILES = T_CTX // TM
LAT_TILES_PER_BATCH = DEC_SEQ // TM
N_TILES = T_ALL // TM
MOD_ROWS = 16
LANES = 128
NEG_BIG = -1e30
MOE_TM = 256
N_ASSIGN = T_ALL * TOP_K
MOE_ROWS = N_ASSIGN + N_EXPERTS * MOE_TM
MOE_TILES = MOE_ROWS // MOE_TM
VMEM_LIMIT = 56 * 1024 * 1024
assert TM == SEQ and DEC_SEQ % TM == 0 and MOE_TM == TM

EV_AQ, EV_BO, EV_BQ, EV_BK, EV_BV, EV_AK, EV_AV, EV_BG = 0, 512, 1024, 1536, 2048, 2560, 2688, 2816
EV_N = 2944
OD_N = 3072


def _params(sem, vmem=VMEM_LIMIT):
    return pltpu.CompilerParams(dimension_semantics=sem, vmem_limit_bytes=vmem)


def _mod_row(i):
    return jnp.where(i < CTX_TILES, 0, 1 + (i - CTX_TILES) // LAT_TILES_PER_BATCH)


def _rope_block(i):
    return jnp.where(i < CTX_TILES, LAT_TILES_PER_BATCH, (i - CTX_TILES) % LAT_TILES_PER_BATCH)


def _dot(a, b, precision=None):
    return jnp.dot(a, b, preferred_element_type=F32, precision=precision)


def _dot_nt(a, b):
    return lax.dot_general(a, b, (((1,), (1,)), ((), ())), preferred_element_type=F32)


def _dot_tn(a, b):
    return lax.dot_general(a, b, (((0,), (0,)), ((), ())), preferred_element_type=F32)


def _modulation_kernel(c_ref, w_ref, b_ref, o_ref):
    c = c_ref[...]
    s = c * jax.nn.sigmoid(c)
    o_ref[...] = _dot(s, w_ref[...], HIGHEST) + b_ref[...]


def _modulation(cond, w_mod, b_mod):
    tn = 1536
    return pl.pallas_call(
        _modulation_kernel,
        out_shape=jax.ShapeDtypeStruct((DEPTH, MOD_ROWS, 6 * D_MODEL), F32),
        grid=(DEPTH, 6 * D_MODEL // tn),
        in_specs=[
            pl.BlockSpec((MOD_ROWS, D_MODEL), lambda l, j: (0, 0)),
            pl.BlockSpec((None, D_MODEL, tn), lambda l, j: (l, 0, j)),
            pl.BlockSpec((None, 1, tn), lambda l, j: (l, 0, j)),
        ],
        out_specs=pl.BlockSpec((None, MOD_ROWS, tn), lambda l, j: (l, 0, j)),
        compiler_params=_params(("parallel", "parallel")),
        name="modulation",
    )(cond, w_mod, b_mod.reshape(DEPTH, 1, 6 * D_MODEL))


def _norm_mod(y, g, sc, sh):
    ms = jnp.mean(y * y, axis=-1, keepdims=True)
    return (y * lax.rsqrt(ms + EPS) * g) * (1.0 + sc) + sh


def _rope_rotate(x):
    w = x.shape[-1]
    lane = lax.broadcasted_iota(jnp.int32, x.shape, 1)
    nxt = pltpu.roll(x, w - 1, 1)
    prv = pltpu.roll(x, 1, 1)
    return jnp.where((lane & 1) == 0, -nxt, prv)


def _is_ctx_tile():
    return pl.program_id(0) < CTX_TILES


def _pair_specs(width):
    return [pl.BlockSpec((TM, width), lambda i: (jnp.minimum(i, CTX_TILES - 1), 0)),
            pl.BlockSpec((TM, width), lambda i: (jnp.maximum(i - CTX_TILES, 0), 0))]


def _pair_value(ctx_ref, lat_ref, cols=slice(None)):
    return jnp.where(_is_ctx_tile(), ctx_ref[:, cols], lat_ref[:, cols])


def _moe_combine_chunks(out_ref, y_ref, tp_ref, g_ref):
    tp = tp_ref[...]
    chunks = []
    for c in range(SLAB):
        cs = slice(c * LANES, (c + 1) * LANES)
        acc = tp[:, 0:1] * out_ref[pl.ds(c, TM, stride=SLAB), :]
        for k in range(1, TOP_K):
            acc = acc + tp[:, k:k + 1] * out_ref[pl.ds(k * TM * SLAB + c, TM, stride=SLAB), :]
        chunks.append(y_ref[:, cs] + g_ref[:, cs] * acc)
    return chunks


def _norm_proj_kernel(*refs, specs, n_gain, y_mode):
    n_y = {"single": 1, "pair": 2, "combine": 4}[y_mode]
    g_ref, sc_ref, sh_ref, w_ref, cos_ref, sin_ref, bd_ref = refs[n_y:n_y + 7]
    gain_refs = refs[n_y + 7:n_y + 7 + n_gain]
    p_ref = refs[n_y + 7 + n_gain]
    out_refs = refs[n_y + 8 + n_gain:]
    if y_mode == "combine":
        y = jnp.concatenate(_moe_combine_chunks(*refs[:4]), axis=1)
        out_refs[0][...] = y
        out_refs = out_refs[1:]
    elif y_mode == "pair":
        y = _pair_value(refs[0], refs[1])
    else:
        y = refs[0][...]
    h = _norm_mod(y, g_ref[...], sc_ref[...], sh_ref[...])
    p_ref[...] = _dot(h.astype(BF16), w_ref[...])
    cos = cos_ref[...]
    sin = sin_ref[...]
    for (col, width, gi, rope, scale, _, heads), o_ref in zip(specs, out_refs):
        for c0 in range(0, width, LANES):
            x = p_ref[:, col + c0:col + c0 + LANES]
            if gi is not None:
                xx = x * x
                xx_hi = xx.astype(BF16)
                xx_lo = (xx - xx_hi.astype(F32)).astype(BF16)
                ss = _dot(xx_hi, bd_ref[...]) + _dot(xx_lo, bd_ref[...])
                x = x * lax.rsqrt(ss * (1.0 / HD) + EPS) * gain_refs[gi][...]
            if rope:
                x = x * cos + _rope_rotate(x) * sin
            if scale != 1.0:
                x = x * scale
            if heads is None:
                o_ref[:, c0:c0 + LANES] = x.astype(o_ref.dtype)
            else:
                hw = width // heads
                per = LANES // hw

                @pl.when(_is_ctx_tile())
                def _(x=x, o_ref=o_ref, c0=c0, hw=hw, per=per):
                    for u in range(per):
                        o_ref[(c0 // LANES) * per + u] = x[:, u * hw:(u + 1) * hw].astype(o_ref.dtype)


def _norm_proj(y, gain, scale, shift, w_bf16, rope_cos, rope_sin, gains, specs):
    n = w_bf16.shape[1]
    tile = pl.BlockSpec((TM, D_MODEL), lambda i: (i, 0))
    vec = pl.BlockSpec((None, 1, D_MODEL), lambda i: (_mod_row(i), 0, 0))
    if not isinstance(y, tuple):
        y_mode, ys, in_specs = "single", [y], [tile]
    elif len(y) == 2:
        y_mode, ys, in_specs = "pair", list(y), _pair_specs(D_MODEL)
    else:
        y_prev, out_slabs, top_p, gate = y
        y_mode, ys = "combine", [out_slabs, y_prev, top_p, gate]
        in_specs = [pl.BlockSpec((TOP_K * TM * SLAB, LANES), lambda i: (i, 0)), tile,
                    pl.BlockSpec((TM, LANES), lambda i: (i, 0)), vec]
    bd = jnp.asarray(np.kron(np.eye(LANES // HD), np.ones((HD, HD))), BF16)
    rope_spec = pl.BlockSpec((TM, LANES), lambda i: (_rope_block(i), 0))
    in_specs += [pl.BlockSpec((1, D_MODEL), lambda i: (0, 0)),
                 vec, vec, pl.BlockSpec((D_MODEL, n), lambda i: (0, 0)),
                 rope_spec, rope_spec, pl.BlockSpec((LANES, LANES), lambda i: (0, 0))]
    in_specs += [pl.BlockSpec((1, LANES), lambda i: (0, 0)) for _ in gains]
    out_shape = [jax.ShapeDtypeStruct((T_ALL, n), F32)]
    out_specs = [pl.BlockSpec((TM, n), lambda i: (i, 0))]
    if y_mode == "combine":
        out_shape.append(jax.ShapeDtypeStruct((T_ALL, D_MODEL), F32))
        out_specs.append(tile)
    for (_, width, _, _, _, dtype, heads) in specs:
        if heads is None:
            out_shape.append(jax.ShapeDtypeStruct((T_ALL, width), dtype))
            out_specs.append(pl.BlockSpec((TM, width), lambda i: (i, 0)))
        else:
            out_shape.append(jax.ShapeDtypeStruct((BATCH, heads, SEQ, width // heads), dtype))
            out_specs.append(pl.BlockSpec((None, heads, SEQ, width // heads),
                                          lambda i: (jnp.minimum(i, CTX_TILES - 1), 0, 0, 0)))
    return pl.pallas_call(
        functools.partial(_norm_proj_kernel, specs=specs, n_gain=len(gains), y_mode=y_mode),
        out_shape=out_shape,
        grid=(N_TILES,),
        in_specs=in_specs,
        out_specs=out_specs,
        compiler_params=_params(("arbitrary",)),
        name="norm_proj",
    )(*ys, gain.reshape(1, D_MODEL), scale, shift, w_bf16, rope_cos, rope_sin, bd, *gains)


def _project(y, *args):
    res = _norm_proj(y, *args)
    if isinstance(y, tuple) and len(y) == 4:
        return res[1], res[0], res[2:]
    return y, res[0], res[1:]


def _rope_tables():
    half = HD // 2
    freqs = 1.0 / (ROPE_THETA ** (jnp.arange(0, half, 2, dtype=F32) / half))
    t = jnp.arange(DEC_SEQ)
    rows = (t // GRID_W).astype(F32)
    cols = (t % GRID_W).astype(F32)
    ang = jnp.concatenate([rows[:, None] * freqs, cols[:, None] * freqs], axis=-1)
    cos = jnp.repeat(jnp.cos(ang), 2, axis=-1)
    sin = jnp.repeat(jnp.sin(ang), 2, axis=-1)
    cos = jnp.concatenate([jnp.tile(cos, (1, LANES // HD)), jnp.ones((TM, LANES), F32)], axis=0)
    sin = jnp.concatenate([jnp.tile(sin, (1, LANES // HD)), jnp.zeros((TM, LANES), F32)], axis=0)
    return cos, sin


def _lane_slice(ref, h, width=HD):
    per = LANES // width
    blk = ref[:, (h // per) * LANES:(h // per + 1) * LANES]
    if per == 1:
        return blk
    return blk[:, (h % per) * width:(h % per + 1) * width]


def _softmax_parts(scores):
    m = None
    for s in scores:
        ms = jnp.max(s, axis=-1, keepdims=True)
        m = ms if m is None else jnp.maximum(m, ms)
    ps = [jnp.exp(s - m) for s in scores]
    l = None
    for p in ps:
        ls = jnp.sum(p, axis=-1, keepdims=True)
        l = ls if l is None else l + ls
    return ps, l


def _attn_std_kernel(*refs, group, n_kv, has_cache, has_bias, bq):
    it = iter(refs)
    q_ref, kn_ref, vn_ref = next(it), next(it), next(it)
    kc_ref = vc_ref = b_ref = None
    if has_cache:
        kc_ref, vc_ref = next(it), next(it)
    if has_bias:
        b_ref = next(it)
    o_ref = next(it)
    outs = []
    for g in range(n_kv):
        qs = jnp.concatenate([_lane_slice(q_ref, g * group + j) for j in range(group)], axis=0)
        kn = _lane_slice(kn_ref, g)
        vn = _lane_slice(vn_ref, g)
        s_new = _dot_nt(qs, kn)
        if has_bias:
            s_new = s_new + b_ref[g]
        scores = [s_new]
        if has_cache:
            scores.append(_dot_nt(qs, kc_ref[g]))
        ps, l = _softmax_parts(scores)
        o = _dot(ps[0].astype(BF16), vn)
        if has_cache:
            o = o + _dot(ps[1].astype(BF16), vc_ref[g])
        o = o / l
        for j in range(group):
            outs.append(o[j * bq:(j + 1) * bq])
    o_ref[...] = jnp.concatenate(outs, axis=1).astype(o_ref.dtype)


def _attn_diff_kernel(*refs, has_cache):
    it = iter(refs)
    lam_ref, q_ref, kn_ref, vn_ref = next(it), next(it), next(it), next(it)
    kc_ref = vc_ref = None
    if has_cache:
        kc_ref, vc_ref = next(it), next(it)
    g_ref, o_ref = next(it), next(it)
    lam = lam_ref[0]
    post = lam_ref[1]
    outs = []
    for h in range(D_HEADS):
        pd_new, pd_c = None, None
        for j in range(2):
            f = 2 * h + j
            qs = _lane_slice(q_ref, f)
            scores = [_dot_nt(qs, _lane_slice(kn_ref, f))]
            if has_cache:
                scores.append(_dot_nt(qs, kc_ref[f]))
            ps, l = _softmax_parts(scores)
            r = 1.0 / l
            if j == 0:
                pd_new = ps[0] * r
                pd_c = ps[1] * r if has_cache else None
            else:
                r = r * lam
                pd_new = pd_new - ps[0] * r
                pd_c = pd_c - ps[1] * r if has_cache else None
        o = _dot(pd_new.astype(BF16), _lane_slice(vn_ref, h, D_VDIM))
        if has_cache:
            o = o + _dot(pd_c.astype(BF16), vc_ref[h])
        ms = jnp.mean(o * o, axis=-1, keepdims=True)
        outs.append(o * lax.rsqrt(ms + EPS) * g_ref[...] * post)
    o_ref[...] = jnp.concatenate(outs, axis=1).astype(o_ref.dtype)


def _attention(q, kn, vn, *, ctx, group=1, n_kv=1, cache=None, bias=None, diff=None, bq=256):
    if ctx:
        nb, sq, row0 = BATCH, SEQ, 0
    else:
        nb, sq, row0 = DEC_BATCH, DEC_SEQ, T_CTX
    nq = sq // bq
    qb0 = row0 // bq
    kb0 = row0 // sq
    wq, wk, wv = q.shape[1], kn.shape[1], vn.shape[1]
    in_specs = [
        pl.BlockSpec((bq, wq), lambda b, i: (qb0 + b * nq + i, 0)),
        pl.BlockSpec((sq, wk), lambda b, i: (kb0 + b, 0)),
        pl.BlockSpec((sq, wv), lambda b, i: (kb0 + b, 0)),
    ]
    args = [q, kn, vn]
    if cache is not None:
        kc, vc = cache
        in_specs += [pl.BlockSpec((None,) + kc.shape[1:], lambda b, i: (b, 0, 0, 0)),
                     pl.BlockSpec((None,) + vc.shape[1:], lambda b, i: (b, 0, 0, 0))]
        args += [kc, vc]
    if diff is None:
        if bias is not None:
            in_specs.append(pl.BlockSpec((bias.shape[0], bq, sq), lambda b, i: (0, i, 0)))
            args.append(bias)
        body = functools.partial(_attn_std_kernel, group=group, n_kv=n_kv, has_cache=cache is not None,
                                 has_bias=bias is not None, bq=bq)
    else:
        lam_vec, gain = diff
        in_specs = [pl.BlockSpec(memory_space=pltpu.SMEM)] + in_specs
        args = [lam_vec] + args
        in_specs.append(pl.BlockSpec((1, D_VDIM), lambda b, i: (0, 0)))
        args.append(gain.reshape(1, D_VDIM))
        body = functools.partial(_attn_diff_kernel, has_cache=cache is not None)
    return pl.pallas_call(
        body,
        out_shape=jax.ShapeDtypeStruct((nb * sq, 512), BF16),
        grid=(nb, nq),
        in_specs=in_specs,
        out_specs=pl.BlockSpec((bq, 512), lambda b, i: (b * nq + i, 0)),
        compiler_params=_params(("parallel", "parallel")),
        name="attention",
    )(*args)


GRID_ROWS = DEC_SEQ // GRID_W
NA_WIN_ROWS = min(NA_ROWS, GRID_ROWS)


def _na_bias_kernel(t_ref, o_ref):
    outside = jnp.full((GRID_W, GRID_W), NEG_BIG, F32)
    for qr in range(GRID_ROWS):
        r0 = min(max(qr - NA_WIN_ROWS // 2, 0), GRID_ROWS - NA_WIN_ROWS)
        parts = [t_ref[kr - qr + (NA_ROWS - 1)] if r0 <= kr < r0 + NA_WIN_ROWS else outside
                 for kr in range(GRID_ROWS)]
        o_ref[qr * GRID_W:(qr + 1) * GRID_W, :] = jnp.concatenate(parts, axis=1)


def _neighbourhood_bias(rpb):
    c = np.arange(GRID_W)
    c0 = np.clip(c - NA_COLS // 2, 0, GRID_W - NA_COLS)
    col_ok = (c[None, :] >= c0[:, None]) & (c[None, :] < c0[:, None] + NA_COLS)
    dc = np.clip(c[None, :] - c[:, None], 1 - NA_COLS, NA_COLS - 1) + (NA_COLS - 1)
    oh_c = jnp.asarray(dc[..., None] == np.arange(2 * NA_COLS - 1), F32)
    by_col = jnp.einsum("hrd,qkd->hrqk", rpb.astype(F32), oh_c, precision=HIGHEST)
    by_col = jnp.where(jnp.asarray(col_ok), by_col, NEG_BIG)
    n_dr = 2 * NA_ROWS - 1
    return pl.pallas_call(
        _na_bias_kernel,
        out_shape=jax.ShapeDtypeStruct((C_HEADS, DEC_SEQ, DEC_SEQ), F32),
        grid=(C_HEADS,),
        in_specs=[pl.BlockSpec((None, n_dr, GRID_W, GRID_W), lambda h: (h, 0, 0, 0))],
        out_specs=pl.BlockSpec((None, DEC_SEQ, DEC_SEQ), lambda h: (h, 0, 0)),
        compiler_params=_params(("parallel",)),
        name="na_bias",
    )(by_col)


def _log_sigmoid(x):
    return jnp.minimum(x, 0.0) - jnp.log1p(jnp.exp(-jnp.abs(x)))


def _mlstm_kernel(q_ref, k_ref, v_ref, g_ref, gb_ref, c0_ref, n0_ref, m0_ref, h_ref, c_ref, n_ref, m_ref, hb_ref,
                  *, seq):
    L = B_CHUNK
    nc = seq // L
    row = lax.broadcasted_iota(jnp.int32, (L, L), 0)
    col = lax.broadcasted_iota(jnp.int32, (L, L), 1)
    keeps = (col <= row, col >= row)
    k_scale = B_DK ** -0.5
    c_ref[...] = c0_ref[...]
    n_ref[...] = n0_ref[...]
    m_ref[...] = m0_ref[...]

    def step(j):
        for d in range(2):
            keep = keeps[d]
            c = j if d == 0 else nc - 1 - j
            off = c * L if isinstance(c, int) else pl.multiple_of(c * L, L)
            gates = g_ref[pl.ds(off, L), :] + gb_ref[...]
            cum = _dot(keep.astype(F32), _log_sigmoid(gates), HIGHEST)
            cum_t = cum.T
            gates_t = gates.T
            out_ref = h_ref if d == 0 else hb_ref
            for h in range(B_HEADS):
                ci = (2 * d) * B_HEADS + h
                cf = (2 * d + 1) * B_HEADS + h
                hs = slice(h * B_DK, (h + 1) * B_DK)
                C = c_ref[d, h]
                n = n_ref[d, h]
                m = m_ref[d, h][:, 0:1]
                qc = q_ref[pl.ds(off, L), hs]
                kc = k_ref[pl.ds(off, L), hs] * k_scale
                vc = v_ref[pl.ds(off, L), hs]
                b_col = cum[:, cf:cf + 1]
                i_col = gates[:, ci:ci + 1]
                b_row = cum_t[cf:cf + 1, :]
                i_row = gates_t[ci:ci + 1, :]
                dlog = jnp.where(keep, b_col - b_row + i_row, -jnp.inf)
                inter = b_col + m
                m_t = jnp.maximum(inter, jnp.max(dlog, axis=-1, keepdims=True))
                w_intra = jnp.exp(dlog - m_t)
                w_inter = jnp.exp(inter - m_t)
                qb = qc.astype(BF16)
                vb = vc.astype(BF16)
                qk = _dot_nt(qb, kc.astype(BF16)) * w_intra
                num = _dot(qk.astype(BF16), vb) + w_inter * _dot(qb, C.astype(BF16))
                den = jnp.sum(qk, axis=-1, keepdims=True) + w_inter * jnp.sum(qc * n, axis=-1, keepdims=True)
                out_ref[pl.ds(off, L), hs] = num / jnp.maximum(jnp.abs(den), jnp.exp(-m_t))
                b_last = b_col[L - 1:L, :] if d == 0 else b_col[0:1, :]
                end_col = b_last - b_col + i_col
                m_new = jnp.maximum(b_last + m, jnp.max(end_col, axis=0, keepdims=True))
                w_end = jnp.exp(end_col - m_new)
                decay = jnp.exp(b_last + m - m_new)
                kw = kc * w_end
                c_ref[d, h] = decay * C + _dot_tn(kw.astype(BF16), vb)
                n_ref[d, h] = decay * n + jnp.sum(kw, axis=0, keepdims=True)
                m_ref[d, h] = jnp.broadcast_to(m_new, (1, LANES))

    if nc <= 2:
        for j in range(nc):
            step(j)
    else:
        def body(j, carry):
            step(j)
            return carry

        lax.fori_loop(0, nc, body, 0)
    h_ref[...] = h_ref[...] + hb_ref[...]


def _mlstm(p, gate_bias, c0, n0, m0, *, ctx):
    if ctx:
        nb, seq, blk0 = BATCH, SEQ, 0
    else:
        nb, seq, blk0 = DEC_BATCH, DEC_SEQ, T_CTX // DEC_SEQ
    w = B_HEADS * B_DK

    def cols(c0_, width):
        return pl.BlockSpec((seq, width), lambda b: (blk0 + b, c0_ // width))

    gb = jnp.zeros((1, LANES), F32).at[0, :4 * B_HEADS].set(gate_bias.reshape(-1).astype(F32))
    st = lambda shape: pl.BlockSpec((None,) + shape, lambda b: (b, 0, 0, 0, 0))
    return pl.pallas_call(
        functools.partial(_mlstm_kernel, seq=seq),
        out_shape=[
            jax.ShapeDtypeStruct((nb * seq, w), F32),
            jax.ShapeDtypeStruct((nb, 2, B_HEADS, B_DK, B_DV), F32),
            jax.ShapeDtypeStruct((nb, 2, B_HEADS, 1, B_DK), F32),
            jax.ShapeDtypeStruct((nb, 2, B_HEADS, 1, LANES), F32),
        ],
        grid=(nb,),
        in_specs=[
            cols(EV_BQ, w), cols(EV_BK, w), cols(EV_BV, w), cols(EV_BG, LANES),
            pl.BlockSpec((1, LANES), lambda b: (0, 0)),
            st((2, B_HEADS, B_DK, B_DV)), st((2, B_HEADS, 1, B_DK)), st((2, B_HEADS, 1, LANES)),
        ],
        out_specs=[
            pl.BlockSpec((seq, w), lambda b: (b, 0)),
            st((2, B_HEADS, B_DK, B_DV)), st((2, B_HEADS, 1, B_DK)), st((2, B_HEADS, 1, LANES)),
        ],
        scratch_shapes=[pltpu.VMEM((seq, w), F32)],
        compiler_params=_params(("parallel",)),
        name="mlstm",
    )(p, p, p, p, gb, c0, n0, m0)


def _merge_value(refs, even, y_pair):
    a_ctx, a_lat, b_ctx, b_lat = refs[:4]
    rest = refs[4:]
    if even:
        bo_ref, ng_ref = rest[:2]
        rest = rest[2:]
        parts = [_pair_value(a_ctx, a_lat)]
        for h in range(B_HEADS):
            hs = slice(h * B_DV, (h + 1) * B_DV)
            x = _pair_value(b_ctx, b_lat, hs)
            ms = jnp.mean(x * x, axis=-1, keepdims=True)
            xn = x * lax.rsqrt(ms + EPS) * ng_ref[:, hs]
            parts.append((jax.nn.sigmoid(bo_ref[:, hs]) * xn).astype(BF16))
    else:
        parts = [_pair_value(a_ctx, a_lat), _pair_value(b_ctx, b_lat)]
    w_ref = rest[0]
    y = _pair_value(rest[1], rest[2]) if y_pair else rest[1][...]
    g_ref = rest[-1]
    cat = jnp.concatenate(parts, axis=1)
    return y + g_ref[...] * _dot(cat, w_ref[...])


def _merge_specs(y, gate, w_bf16, a, b, p, norm_gain):
    in_specs = _pair_specs(512) + _pair_specs(512)
    args = [*a, *b]
    if p is not None:
        in_specs += [pl.BlockSpec((TM, 512), lambda i: (i, EV_BO // 512)), pl.BlockSpec((1, 512), lambda i: (0, 0))]
        args += [p, norm_gain.reshape(1, 512)]
    in_specs.append(pl.BlockSpec((D_MODEL, D_MODEL), lambda i: (0, 0)))
    args.append(w_bf16)
    if isinstance(y, tuple):
        in_specs += _pair_specs(D_MODEL)
        args += list(y)
    else:
        in_specs.append(pl.BlockSpec((TM, D_MODEL), lambda i: (i, 0)))
        args.append(y)
    in_specs.append(pl.BlockSpec((None, 1, D_MODEL), lambda i: (_mod_row(i), 0, 0)))
    args.append(gate)
    return in_specs, args


SLAB = D_MODEL // LANES


def _load_slabs(ref, rows):
    return jnp.concatenate([ref[pl.ds(c, rows, stride=SLAB), :] for c in range(SLAB)], axis=1)


def _store_slabs(ref, x):
    for c in range(SLAB):
        ref[pl.ds(c, x.shape[0], stride=SLAB), :] = x[:, c * LANES:(c + 1) * LANES]


def _slab(ref, idx):
    return ref.at[pl.ds(pl.multiple_of(idx * SLAB, SLAB), SLAB)]


def _merge_router_kernel(*refs, even, y_pair):
    n_merge = 4 + (2 if even else 0) + 1 + (2 if y_pair else 1) + 1
    merge_refs = refs[:n_merge]
    (g_ref, sc_ref, sh_ref, whi_ref, wlo_ref, b_ref,
     y_out_ref, h_ref, ti_ref, tp_ref, rk_ref, cnt_ref, base_ref) = refs[n_merge:]

    @pl.when(pl.program_id(0) == 0)
    def _():
        base_ref[...] = jnp.zeros(base_ref.shape, F32)

    y = _merge_value(merge_refs, even, y_pair)
    y_out_ref[...] = y
    h = _norm_mod(y, g_ref[...], sc_ref[...], sh_ref[...])
    _store_slabs(h_ref, h)
    h_hi = h.astype(BF16)
    h_lo = (h - h_hi.astype(F32)).astype(BF16)
    logits = (_dot(h_hi, whi_ref[...]) + (_dot(h_hi, wlo_ref[...]) + _dot(h_lo, whi_ref[...]))
              + b_ref[...])
    lane = lax.broadcasted_iota(jnp.int32, logits.shape, 1)
    lane_f = lane.astype(F32)
    vals, idxs = [], []
    for _ in range(TOP_K):
        mx = jnp.max(logits, axis=-1, keepdims=True)
        ix = jnp.min(jnp.where(logits == mx, lane_f, float(LANES)), axis=-1, keepdims=True)
        vals.append(mx)
        idxs.append(ix)
        logits = jnp.where(lane_f == ix, -jnp.inf, logits)
    es = [jnp.exp(v - vals[0]) for v in vals]
    tot = es[0] + es[1] + es[2] + es[3]
    ti = jnp.zeros(logits.shape, F32)
    tp = jnp.zeros(logits.shape, F32)
    for k in range(TOP_K):
        ti = jnp.where(lane == k, idxs[k], ti)
        tp = jnp.where(lane == k, es[k] / tot, tp)
    ti_ref[...] = ti.T[0:8, :].astype(jnp.int32)
    tp_ref[...] = tp
    onehots = [(lane_f == ix).astype(F32) for ix in idxs]
    cnt = onehots[0] + onehots[1] + onehots[2] + onehots[3]
    row = lax.broadcasted_iota(jnp.int32, (TM, TM), 0)
    col = lax.broadcasted_iota(jnp.int32, (TM, TM), 1)
    before = _dot((col < row).astype(BF16), cnt.astype(BF16)) + base_ref[...]
    rk = jnp.zeros(logits.shape, F32)
    for k in range(TOP_K):
        rk = jnp.where(lane == k, jnp.sum(onehots[k] * before, axis=-1, keepdims=True), rk)
    rk_ref[...] = rk.T[0:8, :].astype(jnp.int32)
    base_ref[...] = base_ref[...] + jnp.sum(cnt, axis=0, keepdims=True)
    cnt_ref[...] = base_ref[...]


def _merge_router(y, gate, w_out_bf16, a, b, gain, scale, shift, rw, rb, *, p=None, norm_gain=None):
    merge_in_specs, merge_args = _merge_specs(y, gate, w_out_bf16, a, b, p, norm_gain)
    vec = pl.BlockSpec((None, 1, D_MODEL), lambda i: (_mod_row(i), 0, 0))
    rw_p = jnp.zeros((D_MODEL, LANES), F32).at[:, :N_EXPERTS].set(rw)
    rb_p = jnp.full((1, LANES), NEG_BIG, F32).at[0, :N_EXPERTS].set(rb)
    rw_hi = rw_p.astype(BF16)
    tile = lambda w: pl.BlockSpec((TM, w), lambda i: (i, 0))
    by_choice = pl.BlockSpec((None, 8, TM), lambda i: (i, 0, 0))
    return pl.pallas_call(
        functools.partial(_merge_router_kernel, even=p is not None, y_pair=isinstance(y, tuple)),
        out_shape=[jax.ShapeDtypeStruct((T_ALL, D_MODEL), F32),
                   jax.ShapeDtypeStruct((T_ALL * SLAB, LANES), F32),
                   jax.ShapeDtypeStruct((N_TILES, 8, TM), jnp.int32),
                   jax.ShapeDtypeStruct((T_ALL, LANES), F32),
                   jax.ShapeDtypeStruct((N_TILES, 8, TM), jnp.int32),
                   jax.ShapeDtypeStruct((1, LANES), F32)],
        grid=(N_TILES,),
        in_specs=merge_in_specs + [
            pl.BlockSpec((1, D_MODEL), lambda i: (0, 0)), vec, vec,
            pl.BlockSpec((D_MODEL, LANES), lambda i: (0, 0)), pl.BlockSpec((D_MODEL, LANES), lambda i: (0, 0)),
            pl.BlockSpec((1, LANES), lambda i: (0, 0))],
        out_specs=[tile(D_MODEL), pl.BlockSpec((TM * SLAB, LANES), lambda i: (i, 0)), by_choice, tile(LANES),
                   by_choice, pl.BlockSpec((1, LANES), lambda i: (0, 0))],
        scratch_shapes=[pltpu.VMEM((1, LANES), F32)],
        compiler_params=_params(("arbitrary",)),
        name="merge_router",
    )(*merge_args, gain.reshape(1, D_MODEL), scale, shift, rw_hi, (rw_p - rw_hi.astype(F32)).astype(BF16), rb_p)


def _route_plan(top_i, rank, counts):
    experts = jnp.arange(N_EXPERTS, dtype=jnp.int32)
    padded = ((counts + MOE_TM - 1) // MOE_TM) * MOE_TM
    seg_end = jnp.cumsum(padded)
    seg_start = seg_end - padded
    pos = rank
    for e in range(N_EXPERTS - 1):
        pos = pos + jnp.where(top_i > e, padded[e], 0)
    n_active = seg_end[-1] // MOE_TM
    fill = jnp.concatenate([seg_start + counts, padded - counts, n_active[None]]).astype(jnp.int32)
    tile_start = jnp.arange(MOE_TILES, dtype=jnp.int32) * MOE_TM
    tile_expert = jnp.sum((seg_end[None, :] <= tile_start[:, None]).astype(jnp.int32), axis=1)
    last = jnp.sum((seg_end <= (n_active - 1) * MOE_TM).astype(jnp.int32))
    tile_expert = jnp.minimum(jnp.where(tile_start < seg_end[-1], tile_expert, last), N_EXPERTS - 1)
    owns = (padded > 0).astype(jnp.int32)
    run_of_expert = jnp.cumsum(owns) - 1
    run_expert = jnp.sum(jnp.where((run_of_expert[None, :] == experts[:, None]) & (owns[None, :] > 0),
                                   experts[None, :], 0), axis=1)
    runs = jnp.concatenate([run_expert, jnp.sum(owns)[None]]).astype(jnp.int32)
    tile_run = jnp.sum(jnp.where(tile_expert[:, None] == experts[None, :], run_of_expert[None, :], 0), axis=1)
    experts_plan = (tile_expert.astype(jnp.int32), n_active.reshape(1).astype(jnp.int32),
                    tile_run.astype(jnp.int32), runs)
    return pos.astype(jnp.int32), fill, experts_plan


DMA_UNROLL = 4
DMA_QUEUES = 2


def _wait_slabs(ref, n_slabs, sem):
    view = ref.at[pl.ds(0, n_slabs * SLAB)]
    pltpu.make_async_copy(view, view, sem).wait()


def _dispatch_kernel(pos_ref, fill_ref, h_ref, xs_ref, inv_ref, sem):
    i = pl.program_id(0)
    base = i * (TM * TOP_K)

    def issue(j, carry):
        for u in range(DMA_UNROLL):
            t = j * DMA_UNROLL + u
            for k in range(TOP_K):
                a = base + k * TM + t
                slot = pos_ref[a]
                inv_ref[slot] = a
                pltpu.make_async_copy(_slab(h_ref, t), _slab(xs_ref, slot), sem).start(priority=k % DMA_QUEUES)
        return carry

    lax.fori_loop(0, TM // DMA_UNROLL, issue, 0)

    @pl.when(i == 0)
    def _():
        def per_expert(e, total):
            start = fill_ref[e]
            n = fill_ref[N_EXPERTS + e]

            def one(r, carry):
                slot = start + r
                inv_ref[slot] = N_ASSIGN + slot % MOE_TM
                pltpu.make_async_copy(_slab(h_ref, 0), _slab(xs_ref, slot), sem).start()
                return carry

            lax.fori_loop(0, n, one, 0)
            return total + n

        total = lax.fori_loop(0, N_EXPERTS, per_expert, 0)

        n_active = fill_ref[2 * N_EXPERTS]

        def unused_tile(ti, carry):
            dst = xs_ref.at[pl.ds(pl.multiple_of(ti * (MOE_TM * SLAB), MOE_TM * SLAB), MOE_TM * SLAB)]
            pltpu.make_async_copy(h_ref, dst, sem).start()
            return carry

        lax.fori_loop(n_active, MOE_TILES, unused_tile, 0)

        def unused_slot(slot, carry):
            inv_ref[slot] = N_ASSIGN
            return carry

        lax.fori_loop(n_active * MOE_TM, MOE_ROWS, unused_slot, 0)
        total = total + (MOE_TILES - n_active) * MOE_TM

        @pl.when(total > 0)
        def _():
            _wait_slabs(xs_ref, total, sem)

    _wait_slabs(xs_ref, TM * TOP_K, sem)


def _dispatch(h_slabs, pos, fill):
    grid_spec = pltpu.PrefetchScalarGridSpec(
        num_scalar_prefetch=2,
        grid=(N_TILES,),
        in_specs=[pl.BlockSpec((TM * SLAB, LANES), lambda i, pos, fill: (i, 0))],
        out_specs=[pl.BlockSpec(memory_space=pl.ANY), pl.BlockSpec(memory_space=pltpu.SMEM)],
        scratch_shapes=[pltpu.SemaphoreType.DMA],
    )
    return pl.pallas_call(
        _dispatch_kernel,
        out_shape=[jax.ShapeDtypeStruct((MOE_ROWS * SLAB, LANES), F32),
                   jax.ShapeDtypeStruct((MOE_ROWS,), jnp.int32)],
        grid_spec=grid_spec,
        compiler_params=_params(("arbitrary",)),
        name="moe_dispatch",
    )(pos, fill, h_slabs)


MOE_HALF = MOE_TM // 2


def _moe_kernel(te_ref, na_ref, ts_ref, ex_ref, inv_ref, x_ref, wgu_hbm, bgu_ref, wd_hbm, bd_ref, out_hbm,
                wgu_f32, wd_f32, wgu_bf, wd_bf, obuf0, obuf1, sems, osems, *, layer):
    i = pl.program_id(0)
    s = ts_ref[i]
    first = (i == 0) | (s != ts_ref[jnp.maximum(i - 1, 0)])
    n_active = na_ref[0]

    def start_rows(buf, half, tile):
        for r in range(MOE_HALF):
            dst = _slab(out_hbm, inv_ref[tile * MOE_TM + half * MOE_HALF + r])
            pltpu.make_async_copy(buf.at[pl.ds(r * SLAB, SLAB)], dst, osems.at[half]).start(
                priority=r % DMA_QUEUES)

    def wait_rows(half):
        _wait_slabs(out_hbm, MOE_HALF, osems.at[half])

    def ffn_tile(deferred):
        if deferred:
            start_rows(obuf1, 1, i - 1)
        x = _load_slabs(x_ref, MOE_TM).astype(BF16)
        gu = _dot(x, wgu_bf[...]) + bgu_ref[...]
        gate = jnp.minimum(gu[:, :D_FF], SWIGLU_LIMIT)
        up = jnp.clip(gu[:, D_FF:], -SWIGLU_LIMIT, SWIGLU_LIMIT)
        act = ((up + 1.0) * gate * jax.nn.sigmoid(SWIGLU_ALPHA * gate)).astype(BF16)
        o0 = _dot(act[:MOE_HALF], wd_bf[...]) + bd_ref[...]
        if deferred:
            wait_rows(0)
        _store_slabs(obuf0, o0)
        start_rows(obuf0, 0, i)
        o1 = _dot(act[MOE_HALF:], wd_bf[...]) + bd_ref[...]
        if deferred:
            wait_rows(1)
        _store_slabs(obuf1, o1)

    def weight_copies(slot):
        e = ex_ref[slot]
        b = slot % 2
        return (pltpu.make_async_copy(wgu_hbm.at[layer, e], wgu_f32.at[b], sems.at[0, b]),
                pltpu.make_async_copy(wd_hbm.at[layer, e], wd_f32.at[b], sems.at[1, b]))

    @pl.when(i == 0)
    def _():
        for cp in weight_copies(0):
            cp.start()

    @pl.when(first)
    def _():
        for cp in weight_copies(s):
            cp.wait()

        @pl.when(s + 1 < ex_ref[N_EXPERTS])
        def _():
            for cp in weight_copies(s + 1):
                cp.start()

        b = s % 2
        wgu_bf[...] = wgu_f32[b].astype(BF16)
        wd_bf[...] = wd_f32[b].astype(BF16)

    @pl.when(i == 0)
    def _():
        ffn_tile(False)

    @pl.when((i > 0) & (i < n_active))
    def _():
        ffn_tile(True)

    @pl.when(i == n_active)
    def _():
        wait_rows(0)
        start_rows(obuf1, 1, i - 1)
        wait_rows(1)
        for half, buf in enumerate((obuf0, obuf1)):
            dst = out_hbm.at[pl.ds((N_ASSIGN + half * MOE_HALF) * SLAB, MOE_HALF * SLAB)]
            pltpu.make_async_copy(buf, dst, osems.at[half]).start()
        wait_rows(0)
        wait_rows(1)


def _moe_experts(layer, xs, plan, inv, w_gu, b_gu, w_down, b_down):
    const = lambda i, te, na, ts, ex, inv: (layer, te[i], 0, 0)
    grid_spec = pltpu.PrefetchScalarGridSpec(
        num_scalar_prefetch=5,
        grid=(MOE_TILES,),
        in_specs=[
            pl.BlockSpec((MOE_TM * SLAB, LANES), lambda i, te, na, ts, ex, inv: (jnp.minimum(i, na[0] - 1), 0)),
            pl.BlockSpec(memory_space=pl.ANY),
            pl.BlockSpec((None, None, 1, 2 * D_FF), const),
            pl.BlockSpec(memory_space=pl.ANY),
            pl.BlockSpec((None, None, 1, D_MODEL), const),
        ],
        out_specs=pl.BlockSpec(memory_space=pl.ANY),
        scratch_shapes=[pltpu.VMEM((2, D_MODEL, 2 * D_FF), F32), pltpu.VMEM((2, D_FF, D_MODEL), F32),
                        pltpu.VMEM((D_MODEL, 2 * D_FF), BF16), pltpu.VMEM((D_FF, D_MODEL), BF16),
                        pltpu.VMEM((MOE_HALF * SLAB, LANES), F32), pltpu.VMEM((MOE_HALF * SLAB, LANES), F32),
                        pltpu.SemaphoreType.DMA((2, 2)), pltpu.SemaphoreType.DMA((2,))],
    )
    return pl.pallas_call(
        functools.partial(_moe_kernel, layer=layer),
        out_shape=jax.ShapeDtypeStruct(((N_ASSIGN + MOE_TM) * SLAB, LANES), F32),
        grid_spec=grid_spec,
        compiler_params=_params(("arbitrary",)),
        name="moe_experts",
    )(*plan, inv, xs, w_gu, b_gu.reshape(DEPTH, N_EXPERTS, 1, 2 * D_FF), w_down,
      b_down.reshape(DEPTH, N_EXPERTS, 1, D_MODEL))


def _final_combine_kernel(out_ref, y_ref, tp_ref, g_ref, fg_ref, n_ctx_ref, n_lat_ref):
    chunks = _moe_combine_chunks(out_ref, y_ref, tp_ref, g_ref)
    ss = jnp.zeros((TM, 1), F32)
    for yc in chunks:
        ss = ss + jnp.sum(yc * yc, axis=-1, keepdims=True)
    inv = lax.rsqrt(ss * (1.0 / D_MODEL) + EPS)

    def store(n_ref):
        for c in range(SLAB):
            cs = slice(c * LANES, (c + 1) * LANES)
            n_ref[:, cs] = chunks[c] * inv * fg_ref[:, cs]

    @pl.when(_is_ctx_tile())
    def _():
        store(n_ctx_ref)

    @pl.when(jnp.logical_not(_is_ctx_tile()))
    def _():
        store(n_lat_ref)


def _final_combine(y, out_slabs, top_p, gate, final_gain):
    tile = pl.BlockSpec((TM, D_MODEL), lambda i: (i, 0))
    return pl.pallas_call(
        _final_combine_kernel,
        out_shape=[jax.ShapeDtypeStruct((T_CTX, D_MODEL), F32), jax.ShapeDtypeStruct((T_LAT, D_MODEL), F32)],
        grid=(N_TILES,),
        in_specs=[pl.BlockSpec((TOP_K * TM * SLAB, LANES), lambda i: (i, 0)), tile,
                  pl.BlockSpec((TM, LANES), lambda i: (i, 0)),
                  pl.BlockSpec((None, 1, D_MODEL), lambda i: (_mod_row(i), 0, 0)),
                  pl.BlockSpec((1, D_MODEL), lambda i: (0, 0))],
        out_specs=[pl.BlockSpec((TM, D_MODEL), lambda i: (jnp.minimum(i, CTX_TILES - 1), 0)),
                   pl.BlockSpec((TM, D_MODEL), lambda i: (jnp.maximum(i - CTX_TILES, 0), 0))],
        compiler_params=_params(("arbitrary",)),
        name="moe_combine",
    )(out_slabs, y, top_p, gate, final_gain.reshape(1, D_MODEL))


def _moe_experts_layer(layer, routed, w_gu, b_gu, w_down, b_down):
    h_slabs, top_i, top_p, rank, counts = routed
    pos, fill, experts_plan = _route_plan(top_i[:, :TOP_K].reshape(-1), rank[:, :TOP_K].reshape(-1),
                                          counts[0, :N_EXPERTS].astype(jnp.int32))
    xs, inv = _dispatch(h_slabs, pos, fill)
    return _moe_experts(layer, xs, experts_plan, inv, w_gu, b_gu, w_down, b_down), top_p


def kernel(x_prompt, x_sample, c, cache_a_k, cache_a_v, state_b_C, state_b_n, state_b_m, cache_c_k, cache_c_v, cache_d_k, cache_d_v, c_ctx, w_mod, b_mod, norm1_g, norm2_g, w_in_even, w_out_even, a_q_gain, a_k_gain, b_gate_bias, b_norm_gain, w_in_odd, w_out_odd, c_rpb, d_lambda, d_norm_gain, router_w, router_b, expert_w_gu, expert_b_gu, expert_w_down, expert_b_down, final_norm_g):
    y = (x_prompt.reshape(T_CTX, D_MODEL), x_sample.reshape(T_LAT, D_MODEL))
    cond = jnp.zeros((MOD_ROWS, D_MODEL), F32).at[0].set(c_ctx).at[1:1 + DEC_BATCH].set(c)
    mod = _modulation(cond, w_mod, b_mod).reshape(DEPTH, MOD_ROWS, 6, 1, D_MODEL)
    rope_cos, rope_sin = _rope_tables()
    scale = HD ** -0.5
    outs = {}

    for layer in range(DEPTH):
        sh1, sc1, g1, sh2, sc2, g2 = (mod[layer, :, k] for k in range(6))
        j = layer // 2
        if layer % 2 == 0:
            w = w_in_even[j]
            sizes = np.cumsum([0, 512, 128, 128, 512, 512, 512, 512, 16])
            aq, ak, av, bq, bk, bv, bo, bg = (w[:, sizes[k]:sizes[k + 1]] for k in range(8))
            w_in = jnp.concatenate([aq, bo, bq, bk, bv, ak, av, bg, jnp.zeros((D_MODEL, EV_N - EV_BG - 16), F32)],
                                   axis=1).astype(BF16)
            qg = jnp.tile(a_q_gain[j], LANES // HD).reshape(1, LANES)
            kg = jnp.tile(a_k_gain[j], LANES // HD).reshape(1, LANES)
            specs = ((EV_AQ, 512, 0, True, scale, BF16, None), (EV_AK, 128, 1, True, 1.0, BF16, None),
                     (EV_AV, 128, None, False, 1.0, BF16, None),
                     (EV_AK, 128, 1, False, 1.0, F32, A_KV), (EV_AV, 128, None, False, 1.0, F32, A_KV))
            y, p, (qa, ka, va, new_ak, new_av) = _project(y, norm1_g[layer], sc1, sh1, w_in, rope_cos, rope_sin,
                                                          [qg, kg], specs)
            oa_ctx = _attention(qa, ka, va, ctx=True, group=A_HEADS // A_KV, n_kv=A_KV)
            cache = (cache_a_k[:, j].astype(BF16), cache_a_v[:, j].astype(BF16))
            oa_lat = _attention(qa, ka, va, ctx=False, group=A_HEADS // A_KV, n_kv=A_KV, cache=cache)
            zc = jnp.zeros((BATCH, 2, B_HEADS, B_DK, B_DV), F32)
            zn = jnp.zeros((BATCH, 2, B_HEADS, 1, B_DK), F32)
            zm = jnp.zeros((BATCH, 2, B_HEADS, 1, LANES), F32)
            hb_ctx, bC, bn, bm = _mlstm(p, b_gate_bias[j], zc, zn, zm, ctx=True)
            m0 = jnp.broadcast_to(state_b_m[:, j][..., None, None], (DEC_BATCH, 2, B_HEADS, 1, LANES))
            hb_lat, _, _, _ = _mlstm(p, b_gate_bias[j], state_b_C[:, j], state_b_n[:, j][:, :, :, None, :], m0,
                                     ctx=False)
            y, *routed = _merge_router(y, g1, w_out_even[j].astype(BF16), (oa_ctx, oa_lat), (hb_ctx, hb_lat),
                                       norm2_g[layer], sc2, sh2, router_w[layer], router_b[layer],
                                       p=p, norm_gain=b_norm_gain[j])
            outs.setdefault("a_k", []).append(new_ak)
            outs.setdefault("a_v", []).append(new_av)
            outs.setdefault("b_C", []).append(bC)
            outs.setdefault("b_n", []).append(bn[:, :, :, 0, :])
            outs.setdefault("b_m", []).append(bm[:, :, :, 0, 0])
        else:
            specs = ((0, 512, None, False, scale, BF16, None), (512, 512, None, False, 1.0, BF16, None),
                     (1024, 512, None, False, 1.0, BF16, None), (1536, 512, None, True, scale, BF16, None),
                     (2048, 512, None, True, 1.0, BF16, None), (2560, 512, None, False, 1.0, BF16, None),
                     (512, 512, None, False, 1.0, F32, C_HEADS), (1024, 512, None, False, 1.0, F32, C_HEADS),
                     (2048, 512, None, False, 1.0, F32, 2 * D_HEADS), (2560, 512, None, False, 1.0, F32, D_HEADS))
            y, p, (qc, kc, vc, qd, kd, vd, new_ck, new_cv, new_dk, new_dv) = _project(
                y, norm1_g[layer], sc1, sh1, w_in_odd[j].astype(BF16), rope_cos, rope_sin, [], specs)
            lam_init = 0.8 - 0.6 * math.exp(-0.3 * layer)
            lp = d_lambda[j].astype(F32)
            lam = jnp.exp(jnp.sum(lp[0] * lp[1])) - jnp.exp(jnp.sum(lp[2] * lp[3])) + lam_init
            lam_vec = jnp.stack([lam, jnp.asarray(1.0 - lam_init, F32)]).astype(F32)
            diff = (lam_vec, d_norm_gain[j])
            oc_ctx = _attention(qc, kc, vc, ctx=True, n_kv=C_HEADS)
            od_ctx = _attention(qd, kd, vd, ctx=True, diff=diff)
            bias = _neighbourhood_bias(c_rpb[j])
            oc_lat = _attention(qc, kc, vc, ctx=False, n_kv=C_HEADS, bias=bias,
                                cache=(cache_c_k[:, j].astype(BF16), cache_c_v[:, j].astype(BF16)))
            kd_cache = cache_d_k[:, j].reshape(DEC_BATCH, 2 * D_HEADS, PAST_LEN, HD).astype(BF16)
            od_lat = _attention(qd, kd, vd, ctx=False, diff=diff, cache=(kd_cache, cache_d_v[:, j].astype(BF16)))
            y, *routed = _merge_router(y, g1, w_out_odd[j].astype(BF16), (oc_ctx, oc_lat), (od_ctx, od_lat),
                                       norm2_g[layer], sc2, sh2, router_w[layer], router_b[layer])
            outs.setdefault("c_k", []).append(new_ck)
            outs.setdefault("c_v", []).append(new_cv)
            outs.setdefault("d_k", []).append(new_dk.reshape(BATCH, D_HEADS, 2, SEQ, HD))
            outs.setdefault("d_v", []).append(new_dv)
        out_slabs, top_p = _moe_experts_layer(layer, routed, expert_w_gu, expert_b_gu, expert_w_down,
                                              expert_b_down)
        y = (y, out_slabs, top_p, g2)
    y_prompt, y_sample = _final_combine(*y, final_norm_g)
    stack = lambda k: jnp.stack(outs[k], axis=1)
    return (y_prompt.reshape(BATCH, SEQ, D_MODEL), y_sample.reshape(DEC_BATCH, DEC_SEQ, D_MODEL),
            stack("a_k"), stack("a_v"), stack("b_C"), stack("b_n"), stack("b_m"),
            stack("c_k"), stack("c_v"), stack("d_k"), stack("d_v"))
```

```python
import functools
import math

import numpy as np
import jax
import jax.numpy as jnp
from jax import lax
from jax.experimental import pallas as pl
from jax.experimental.pallas import tpu as pltpu

D_MODEL = 1024
BATCH = 32
SEQ = 256
DEPTH = 2
DEC_BATCH = 8
DEC_SEQ = 1024
PAST_LEN = 512
GRID_W = 64
HD = 64
A_HEADS = 8
A_KV = 2
B_HEADS = 4
B_DK = 128
B_DV = 128
B_CHUNK = 128
C_HEADS = 8
NA_ROWS = 8
NA_COLS = 16
D_HEADS = 4
D_VDIM = 2 * HD
N_EXPERTS = 32
TOP_K = 4
D_FF = 1024
SWIGLU_LIMIT = 7.0
SWIGLU_ALPHA = 1.702
ROPE_THETA = 10000.0
EPS = 1e-6

F32 = jnp.float32
BF16 = jnp.bfloat16
HIGHEST = lax.Precision.HIGHEST

T_CTX = BATCH * SEQ
T_LAT = DEC_BATCH * DEC_SEQ
T_ALL = T_CTX + T_LAT
TM = 256
CTX_TILES = T_CTX // TM
LAT_TILES_PER_BATCH = DEC_SEQ // TM
N_TILES = T_ALL // TM
MOD_ROWS = 16
LANES = 128
NEG_BIG = -1e30
MOE_TM = 256
N_ASSIGN = T_ALL * TOP_K
MOE_ROWS = N_ASSIGN + N_EXPERTS * MOE_TM
MOE_TILES = MOE_ROWS // MOE_TM
VMEM_LIMIT = 56 * 1024 * 1024
assert TM == SEQ and DEC_SEQ % TM == 0 and MOE_TM == TM

EV_AQ, EV_BO, EV_BQ, EV_BK, EV_BV, EV_AK, EV_AV, EV_BG = 0, 512, 1024, 1536, 2048, 2560, 2688, 2816
EV_N = 2944


def _params(sem, vmem=VMEM_LIMIT):
    return pltpu.CompilerParams(dimension_semantics=sem, vmem_limit_bytes=vmem)


def _mod_row(i):
    return jnp.where(i < CTX_TILES, 0, 1 + (i - CTX_TILES) // LAT_TILES_PER_BATCH)


def _rope_block(i):
    return jnp.where(i < CTX_TILES, LAT_TILES_PER_BATCH, (i - CTX_TILES) % LAT_TILES_PER_BATCH)


def _dot(a, b, precision=None):
    return jnp.dot(a, b, preferred_element_type=F32, precision=precision)


def _dot_nt(a, b):
    return lax.dot_general(a, b, (((1,), (1,)), ((), ())), preferred_element_type=F32)


def _dot_tn(a, b):
    return lax.dot_general(a, b, (((0,), (0,)), ((), ())), preferred_element_type=F32)


def _modulation_kernel(c_ref, w_ref, b_ref, o_ref):
    c = c_ref[...]
    s = c * jax.nn.sigmoid(c)
    o_ref[...] = _dot(s, w_ref[...], HIGHEST) + b_ref[...]


def _modulation(cond, w_mod, b_mod):
    tn = 1536
    return pl.pallas_call(
        _modulation_kernel,
        out_shape=jax.ShapeDtypeStruct((DEPTH, MOD_ROWS, 6 * D_MODEL), F32),
        grid=(DEPTH, 6 * D_MODEL // tn),
        in_specs=[
            pl.BlockSpec((MOD_ROWS, D_MODEL), lambda l, j: (0, 0)),
            pl.BlockSpec((None, D_MODEL, tn), lambda l, j: (l, 0, j)),
            pl.BlockSpec((None, 1, tn), lambda l, j: (l, 0, j)),
        ],
        out_specs=pl.BlockSpec((None, MOD_ROWS, tn), lambda l, j: (l, 0, j)),
        compiler_params=_params(("parallel", "parallel")),
        name="modulation",
    )(cond, w_mod, b_mod.reshape(DEPTH, 1, 6 * D_MODEL))


def _norm_mod(y, g, sc, sh):
    ms = jnp.mean(y * y, axis=-1, keepdims=True)
    return (y * lax.rsqrt(ms + EPS) * g) * (1.0 + sc) + sh


def _rope_rotate(x):
    w = x.shape[-1]
    lane = lax.broadcasted_iota(jnp.int32, x.shape, 1)
    nxt = pltpu.roll(x, w - 1, 1)
    prv = pltpu.roll(x, 1, 1)
    return jnp.where((lane & 1) == 0, -nxt, prv)


def _is_ctx_tile():
    return pl.program_id(0) < CTX_TILES


def _pair_specs(width):
    return [pl.BlockSpec((TM, width), lambda i: (jnp.minimum(i, CTX_TILES - 1), 0)),
            pl.BlockSpec((TM, width), lambda i: (jnp.maximum(i - CTX_TILES, 0), 0))]


def _pair_value(ctx_ref, lat_ref, cols=slice(None)):
    return jnp.where(_is_ctx_tile(), ctx_ref[:, cols], lat_ref[:, cols])


def _moe_combine_chunks(out_ref, y_ref, tp_ref, g_ref):
    tp = tp_ref[...]
    chunks = []
    for c in range(SLAB):
        cs = slice(c * LANES, (c + 1) * LANES)
        acc = tp[:, 0:1] * out_ref[pl.ds(c, TM, stride=SLAB), :]
        for k in range(1, TOP_K):
            acc = acc + tp[:, k:k + 1] * out_ref[pl.ds(k * TM * SLAB + c, TM, stride=SLAB), :]
        chunks.append(y_ref[:, cs] + g_ref[:, cs] * acc)
    return chunks


def _norm_proj_kernel(*refs, specs, n_gain, y_mode):
    n_y = {"single": 1, "pair": 2, "combine": 4}[y_mode]
    g_ref, sc_ref, sh_ref, w_ref, cos_ref, sin_ref, bd_ref = refs[n_y:n_y + 7]
    gain_refs = refs[n_y + 7:n_y + 7 + n_gain]
    p_ref = refs[n_y + 7 + n_gain]
    out_refs = refs[n_y + 8 + n_gain:]
    if y_mode == "combine":
        y = jnp.concatenate(_moe_combine_chunks(*refs[:4]), axis=1)
        out_refs[0][...] = y
        out_refs = out_refs[1:]
    elif y_mode == "pair":
        y = _pair_value(refs[0], refs[1])
    else:
        y = refs[0][...]
    h = _norm_mod(y, g_ref[...], sc_ref[...], sh_ref[...])
    p_ref[...] = _dot(h.astype(BF16), w_ref[...])
    cos = cos_ref[...]
    sin = sin_ref[...]
    for (col, width, gi, rope, scale, _, heads), o_ref in zip(specs, out_refs):
        for c0 in range(0, width, LANES):
            x = p_ref[:, col + c0:col + c0 + LANES]
            if gi is not None:
                xx = x * x
                xx_hi = xx.astype(BF16)
                xx_lo = (xx - xx_hi.astype(F32)).astype(BF16)
                ss = _dot(xx_hi, bd_ref[...]) + _dot(xx_lo, bd_ref[...])
                x = x * lax.rsqrt(ss * (1.0 / HD) + EPS) * gain_refs[gi][...]
            if rope:
                x = x * cos + _rope_rotate(x) * sin
            if scale != 1.0:
                x = x * scale
            if heads is None:
                o_ref[:, c0:c0 + LANES] = x.astype(o_ref.dtype)
            else:
                hw = width // heads
                per = LANES // hw

                @pl.when(_is_ctx_tile())
                def _(x=x, o_ref=o_ref, c0=c0, hw=hw, per=per):
                    for u in range(per):
                        o_ref[(c0 // LANES) * per + u] = x[:, u * hw:(u + 1) * hw].astype(o_ref.dtype)


def _norm_proj(y, gain, scale, shift, w_bf16, rope_cos, rope_sin, gains, specs):
    n = w_bf16.shape[1]
    tile = pl.BlockSpec((TM, D_MODEL), lambda i: (i, 0))
    vec = pl.BlockSpec((None, 1, D_MODEL), lambda i: (_mod_row(i), 0, 0))
    if not isinstance(y, tuple):
        y_mode, ys, in_specs = "single", [y], [tile]
    elif len(y) == 2:
        y_mode, ys, in_specs = "pair", list(y), _pair_specs(D_MODEL)
    else:
        y_prev, out_slabs, top_p, gate = y
        y_mode, ys = "combine", [out_slabs, y_prev, top_p, gate]
        in_specs = [pl.BlockSpec((TOP_K * TM * SLAB, LANES), lambda i: (i, 0)), tile,
                    pl.BlockSpec((TM, LANES), lambda i: (i, 0)), vec]
    bd = jnp.asarray(np.kron(np.eye(LANES // HD), np.ones((HD, HD))), BF16)
    rope_spec = pl.BlockSpec((TM, LANES), lambda i: (_rope_block(i), 0))
    in_specs += [pl.BlockSpec((1, D_MODEL), lambda i: (0, 0)),
                 vec, vec, pl.BlockSpec((D_MODEL, n), lambda i: (0, 0)),
                 rope_spec, rope_spec, pl.BlockSpec((LANES, LANES), lambda i: (0, 0))]
    in_specs += [pl.BlockSpec((1, LANES), lambda i: (0, 0)) for _ in gains]
    out_shape = [jax.ShapeDtypeStruct((T_ALL, n), F32)]
    out_specs = [pl.BlockSpec((TM, n), lambda i: (i, 0))]
    if y_mode == "combine":
        out_shape.append(jax.ShapeDtypeStruct((T_ALL, D_MODEL), F32))
        out_specs.append(tile)
    for (_, width, _, _, _, dtype, heads) in specs:
        if heads is None:
            out_shape.append(jax.ShapeDtypeStruct((T_ALL, width), dtype))
            out_specs.append(pl.BlockSpec((TM, width), lambda i: (i, 0)))
        else:
            out_shape.append(jax.ShapeDtypeStruct((BATCH, heads, SEQ, width // heads), dtype))
            out_specs.append(pl.BlockSpec((None, heads, SEQ, width // heads),
                                          lambda i: (jnp.minimum(i, CTX_TILES - 1), 0, 0, 0)))
    return pl.pallas_call(
        functools.partial(_norm_proj_kernel, specs=specs, n_gain=len(gains), y_mode=y_mode),
        out_shape=out_shape,
        grid=(N_TILES,),
        in_specs=in_specs,
        out_specs=out_specs,
        compiler_params=_params(("arbitrary",)),
        name="norm_proj",
    )(*ys, gain.reshape(1, D_MODEL), scale, shift, w_bf16, rope_cos, rope_sin, bd, *gains)


def _project(y, *args):
    res = _norm_proj(y, *args)
    if isinstance(y, tuple) and len(y) == 4:
        return res[1], res[0], res[2:]
    return y, res[0], res[1:]


def _rope_tables():
    half = HD // 2
    freqs = 1.0 / (ROPE_THETA ** (jnp.arange(0, half, 2, dtype=F32) / half))
    t = jnp.arange(DEC_SEQ)
    rows = (t // GRID_W).astype(F32)
    cols = (t % GRID_W).astype(F32)
    ang = jnp.concatenate([rows[:, None] * freqs, cols[:, None] * freqs], axis=-1)
    cos = jnp.repeat(jnp.cos(ang), 2, axis=-1)
    sin = jnp.repeat(jnp.sin(ang), 2, axis=-1)
    cos = jnp.concatenate([jnp.tile(cos, (1, LANES // HD)), jnp.ones((TM, LANES), F32)], axis=0)
    sin = jnp.concatenate([jnp.tile(sin, (1, LANES // HD)), jnp.zeros((TM, LANES), F32)], axis=0)
    return cos, sin


def _lane_slice(ref, h, width=HD):
    per = LANES // width
    blk = ref[:, (h // per) * LANES:(h // per + 1) * LANES]
    if per == 1:
        return blk
    return blk[:, (h % per) * width:(h % per + 1) * width]


def _softmax_parts(scores):
    m = None
    for s in scores:
        ms = jnp.max(s, axis=-1, keepdims=True)
        m = ms if m is None else jnp.maximum(m, ms)
    ps = [jnp.exp(s - m) for s in scores]
    l = None
    for p in ps:
        ls = jnp.sum(p, axis=-1, keepdims=True)
        l = ls if l is None else l + ls
    return ps, l


def _attn_std_kernel(*refs, group, n_kv, has_cache, has_bias, bq):
    it = iter(refs)
    q_ref, kn_ref, vn_ref = next(it), next(it), next(it)
    kc_ref = vc_ref = b_ref = None
    if has_cache:
        kc_ref, vc_ref = next(it), next(it)
    if has_bias:
        b_ref = next(it)
    o_ref = next(it)
    outs = []
    for g in range(n_kv):
        qs = jnp.concatenate([_lane_slice(q_ref, g * group + j) for j in range(group)], axis=0)
        kn = _lane_slice(kn_ref, g)
        vn = _lane_slice(vn_ref, g)
        s_new = _dot_nt(qs, kn)
        if has_bias:
            s_new = s_new + b_ref[g]
        scores = [s_new]
        if has_cache:
            scores.append(_dot_nt(qs, kc_ref[g].astype(BF16)))
        ps, l = _softmax_parts(scores)
        o = _dot(ps[0].astype(BF16), vn)
        if has_cache:
            o = o + _dot(ps[1].astype(BF16), vc_ref[g].astype(BF16))
        o = o / l
        for j in range(group):
            outs.append(o[j * bq:(j + 1) * bq])
    o_ref[...] = jnp.concatenate(outs, axis=1).astype(o_ref.dtype)


def _attn_diff_kernel(*refs, has_cache):
    it = iter(refs)
    lam_ref, q_ref, kn_ref, vn_ref = next(it), next(it), next(it), next(it)
    kc_ref = vc_ref = None
    if has_cache:
        kc_ref, vc_ref = next(it), next(it)
    g_ref, o_ref = next(it), next(it)
    lam = lam_ref[0]
    post = lam_ref[1]
    outs = []
    for h in range(D_HEADS):
        pd_new, pd_c = None, None
        for j in range(2):
            f = 2 * h + j
            qs = _lane_slice(q_ref, f)
            scores = [_dot_nt(qs, _lane_slice(kn_ref, f))]
            if has_cache:
                scores.append(_dot_nt(qs, kc_ref[f].astype(BF16)))
            ps, l = _softmax_parts(scores)
            r = 1.0 / l
            if j == 0:
                pd_new = ps[0] * r
                pd_c = ps[1] * r if has_cache else None
            else:
                r = r * lam
                pd_new = pd_new - ps[0] * r
                pd_c = pd_c - ps[1] * r if has_cache else None
        o = _dot(pd_new.astype(BF16), _lane_slice(vn_ref, h, D_VDIM))
        if has_cache:
            o = o + _dot(pd_c.astype(BF16), vc_ref[h].astype(BF16))
        ms = jnp.mean(o * o, axis=-1, keepdims=True)
        outs.append(o * lax.rsqrt(ms + EPS) * g_ref[...] * post)
    o_ref[...] = jnp.concatenate(outs, axis=1).astype(o_ref.dtype)


def _attention(q, kn, vn, *, ctx, group=1, n_kv=1, cache=None, bias=None, diff=None, bq=256):
    if ctx:
        nb, sq, row0 = BATCH, SEQ, 0
    else:
        nb, sq, row0 = DEC_BATCH, DEC_SEQ, T_CTX
    nq = sq // bq
    qb0 = row0 // bq
    kb0 = row0 // sq
    wq, wk, wv = q.shape[1], kn.shape[1], vn.shape[1]
    in_specs = [
        pl.BlockSpec((bq, wq), lambda b, i: (qb0 + b * nq + i, 0)),
        pl.BlockSpec((sq, wk), lambda b, i: (kb0 + b, 0)),
        pl.BlockSpec((sq, wv), lambda b, i: (kb0 + b, 0)),
    ]
    args = [q, kn, vn]
    if cache is not None:
        kc, vc, slot = cache
        in_specs += [pl.BlockSpec((None, None) + kc.shape[2:], lambda b, i: (b, slot, 0, 0, 0)),
                     pl.BlockSpec((None, None) + vc.shape[2:], lambda b, i: (b, slot, 0, 0, 0))]
        args += [kc, vc]
    if diff is None:
        if bias is not None:
            in_specs.append(pl.BlockSpec((bias.shape[0], bq, sq), lambda b, i: (0, i, 0)))
            args.append(bias)
        body = functools.partial(_attn_std_kernel, group=group, n_kv=n_kv, has_cache=cache is not None,
                                 has_bias=bias is not None, bq=bq)
    else:
        lam_vec, gain = diff
        in_specs = [pl.BlockSpec(memory_space=pltpu.SMEM)] + in_specs
        args = [lam_vec] + args
        in_specs.append(pl.BlockSpec((1, D_VDIM), lambda b, i: (0, 0)))
        args.append(gain.reshape(1, D_VDIM))
        body = functools.partial(_attn_diff_kernel, has_cache=cache is not None)
    return pl.pallas_call(
        body,
        out_shape=jax.ShapeDtypeStruct((nb * sq, 512), BF16),
        grid=(nb, nq),
        in_specs=in_specs,
        out_specs=pl.BlockSpec((bq, 512), lambda b, i: (b * nq + i, 0)),
        compiler_params=_params(("parallel", "parallel")),
        name="attention",
    )(*args)


GRID_ROWS = DEC_SEQ // GRID_W
NA_WIN_ROWS = min(NA_ROWS, GRID_ROWS)


def _na_bias_kernel(t_ref, o_ref):
    outside = jnp.full((GRID_W, GRID_W), NEG_BIG, F32)
    for qr in range(GRID_ROWS):
        r0 = min(max(qr - NA_WIN_ROWS // 2, 0), GRID_ROWS - NA_WIN_ROWS)
        parts = [t_ref[kr - qr + (NA_ROWS - 1)] if r0 <= kr < r0 + NA_WIN_ROWS else outside
                 for kr in range(GRID_ROWS)]
        o_ref[qr * GRID_W:(qr + 1) * GRID_W, :] = jnp.concatenate(parts, axis=1)


def _neighbourhood_bias(rpb):
    c = np.arange(GRID_W)
    c0 = np.clip(c - NA_COLS // 2, 0, GRID_W - NA_COLS)
    col_ok = (c[None, :] >= c0[:, None]) & (c[None, :] < c0[:, None] + NA_COLS)
    dc = np.clip(c[None, :] - c[:, None], 1 - NA_COLS, NA_COLS - 1) + (NA_COLS - 1)
    oh_c = jnp.asarray(dc[..., None] == np.arange(2 * NA_COLS - 1), F32)
    by_col = jnp.einsum("hrd,qkd->hrqk", rpb.astype(F32), oh_c, precision=HIGHEST)
    by_col = jnp.where(jnp.asarray(col_ok), by_col, NEG_BIG)
    n_dr = 2 * NA_ROWS - 1
    return pl.pallas_call(
        _na_bias_kernel,
        out_shape=jax.ShapeDtypeStruct((C_HEADS, DEC_SEQ, DEC_SEQ), F32),
        grid=(C_HEADS,),
        in_specs=[pl.BlockSpec((None, n_dr, GRID_W, GRID_W), lambda h: (h, 0, 0, 0))],
        out_specs=pl.BlockSpec((None, DEC_SEQ, DEC_SEQ), lambda h: (h, 0, 0)),
        compiler_params=_params(("parallel",)),
        name="na_bias",
    )(by_col)


def _log_sigmoid(x):
    return jnp.minimum(x, 0.0) - jnp.log1p(jnp.exp(-jnp.abs(x)))


def _mlstm_kernel(q_ref, k_ref, v_ref, g_ref, gb_ref, c0_ref, n0_ref, m0_ref, h_ref, c_ref, n_ref, m_ref, hb_ref,
                  *, seq):
    L = B_CHUNK
    nc = seq // L
    row = lax.broadcasted_iota(jnp.int32, (L, L), 0)
    col = lax.broadcasted_iota(jnp.int32, (L, L), 1)
    keeps = (col <= row, col >= row)
    k_scale = B_DK ** -0.5
    c_ref[...] = c0_ref[...]
    n_ref[...] = n0_ref[...]
    m_ref[...] = m0_ref[...]

    def step(j):
        for d in range(2):
            keep = keeps[d]
            c = j if d == 0 else nc - 1 - j
            off = c * L if isinstance(c, int) else pl.multiple_of(c * L, L)
            gates = g_ref[pl.ds(off, L), :] + gb_ref[...]
            cum = _dot(keep.astype(F32), _log_sigmoid(gates), HIGHEST)
            cum_t = cum.T
            gates_t = gates.T
            out_ref = h_ref if d == 0 else hb_ref
            for h in range(B_HEADS):
                ci = (2 * d) * B_HEADS + h
                cf = (2 * d + 1) * B_HEADS + h
                hs = slice(h * B_DK, (h + 1) * B_DK)
                C = c_ref[d, h]
                n = n_ref[d, h]
                m = m_ref[d, h][:, 0:1]
                qc = q_ref[pl.ds(off, L), hs]
                kc = k_ref[pl.ds(off, L), hs] * k_scale
                vc = v_ref[pl.ds(off, L), hs]
                b_col = cum[:, cf:cf + 1]
                i_col = gates[:, ci:ci + 1]
                b_row = cum_t[cf:cf + 1, :]
                i_row = gates_t[ci:ci + 1, :]
                dlog = jnp.where(keep, b_col - b_row + i_row, -jnp.inf)
                inter = b_col + m
                m_t = jnp.maximum(inter, jnp.max(dlog, axis=-1, keepdims=True))
                w_intra = jnp.exp(dlog - m_t)
                w_inter = jnp.exp(inter - m_t)
                qb = qc.astype(BF16)
                vb = vc.astype(BF16)
                qk = _dot_nt(qb, kc.astype(BF16)) * w_intra
                num = _dot(qk.astype(BF16), vb) + w_inter * _dot(qb, C.astype(BF16))
                den = jnp.sum(qk, axis=-1, keepdims=True) + w_inter * jnp.sum(qc * n, axis=-1, keepdims=True)
                out_ref[pl.ds(off, L), hs] = num / jnp.maximum(jnp.abs(den), jnp.exp(-m_t))
                b_last = b_col[L - 1:L, :] if d == 0 else b_col[0:1, :]
                end_col = b_last - b_col + i_col
                m_new = jnp.maximum(b_last + m, jnp.max(end_col, axis=0, keepdims=True))
                w_end = jnp.exp(end_col - m_new)
                decay = jnp.exp(b_last + m - m_new)
                kw = kc * w_end
                c_ref[d, h] = decay * C + _dot_tn(kw.astype(BF16), vb)
                n_ref[d, h] = decay * n + jnp.sum(kw, axis=0, keepdims=True)
                m_ref[d, h] = jnp.broadcast_to(m_new, (1, LANES))

    if nc <= 2:
        for j in range(nc):
            step(j)
    else:
        def body(j, carry):
            step(j)
            return carry

        lax.fori_loop(0, nc, body, 0)
    h_ref[...] = h_ref[...] + hb_ref[...]


def _mlstm(p, gate_bias, c0, n0, m0, *, ctx):
    if ctx:
        nb, seq, blk0 = BATCH, SEQ, 0
    else:
        nb, seq, blk0 = DEC_BATCH, DEC_SEQ, T_CTX // DEC_SEQ
    w = B_HEADS * B_DK

    def cols(c0_, width):
        return pl.BlockSpec((seq, width), lambda b: (blk0 + b, c0_ // width))

    gb = jnp.zeros((1, LANES), F32).at[0, :4 * B_HEADS].set(gate_bias.reshape(-1).astype(F32))
    st = lambda shape: pl.BlockSpec((None,) + shape, lambda b: (b, 0, 0, 0, 0))
    return pl.pallas_call(
        functools.partial(_mlstm_kernel, seq=seq),
        out_shape=[
            jax.ShapeDtypeStruct((nb * seq, w), F32),
            jax.ShapeDtypeStruct((nb, 2, B_HEADS, B_DK, B_DV), F32),
            jax.ShapeDtypeStruct((nb, 2, B_HEADS, 1, B_DK), F32),
            jax.ShapeDtypeStruct((nb, 2, B_HEADS, 1, LANES), F32),
        ],
        grid=(nb,),
        in_specs=[
            cols(EV_BQ, w), cols(EV_BK, w), cols(EV_BV, w), cols(EV_BG, LANES),
            pl.BlockSpec((1, LANES), lambda b: (0, 0)),
            st((2, B_HEADS, B_DK, B_DV)), st((2, B_HEADS, 1, B_DK)), st((2, B_HEADS, 1, LANES)),
        ],
        out_specs=[
            pl.BlockSpec((seq, w), lambda b: (b, 0)),
            st((2, B_HEADS, B_DK, B_DV)), st((2, B_HEADS, 1, B_DK)), st((2, B_HEADS, 1, LANES)),
        ],
        scratch_shapes=[pltpu.VMEM((seq, w), F32)],
        compiler_params=_params(("parallel",)),
        name="mlstm",
    )(p, p, p, p, gb, c0, n0, m0)


def _merge_value(refs, even, y_pair):
    a_ctx, a_lat, b_ctx, b_lat = refs[:4]
    rest = refs[4:]
    if even:
        bo_ref, ng_ref = rest[:2]
        rest = rest[2:]
        parts = [_pair_value(a_ctx, a_lat)]
        for h in range(B_HEADS):
            hs = slice(h * B_DV, (h + 1) * B_DV)
            x = _pair_value(b_ctx, b_lat, hs)
            ms = jnp.mean(x * x, axis=-1, keepdims=True)
            xn = x * lax.rsqrt(ms + EPS) * ng_ref[:, hs]
            parts.append((jax.nn.sigmoid(bo_ref[:, hs]) * xn).astype(BF16))
    else:
        parts = [_pair_value(a_ctx, a_lat), _pair_value(b_ctx, b_lat)]
    w_ref = rest[0]
    y = _pair_value(rest[1], rest[2]) if y_pair else rest[1][...]
    g_ref = rest[-1]
    cat = jnp.concatenate(parts, axis=1)
    return y + g_ref[...] * _dot(cat, w_ref[...])


def _merge_specs(y, gate, w_bf16, a, b, p, norm_gain):
    in_specs = _pair_specs(512) + _pair_specs(512)
    args = [*a, *b]
    if p is not None:
        in_specs += [pl.BlockSpec((TM, 512), lambda i: (i, EV_BO // 512)), pl.BlockSpec((1, 512), lambda i: (0, 0))]
        args += [p, norm_gain.reshape(1, 512)]
    in_specs.append(pl.BlockSpec((D_MODEL, D_MODEL), lambda i: (0, 0)))
    args.append(w_bf16)
    if isinstance(y, tuple):
        in_specs += _pair_specs(D_MODEL)
        args += list(y)
    else:
        in_specs.append(pl.BlockSpec((TM, D_MODEL), lambda i: (i, 0)))
        args.append(y)
    in_specs.append(pl.BlockSpec((None, 1, D_MODEL), lambda i: (_mod_row(i), 0, 0)))
    args.append(gate)
    return in_specs, args


SLAB = D_MODEL // LANES


def _load_slabs(ref, rows):
    return jnp.concatenate([ref[pl.ds(c, rows, stride=SLAB), :] for c in range(SLAB)], axis=1)


def _store_slabs(ref, x):
    for c in range(SLAB):
        ref[pl.ds(c, x.shape[0], stride=SLAB), :] = x[:, c * LANES:(c + 1) * LANES]


def _slab(ref, idx):
    return ref.at[pl.ds(pl.multiple_of(idx * SLAB, SLAB), SLAB)]


def _merge_router_kernel(*refs, even, y_pair):
    n_merge = 4 + (2 if even else 0) + 1 + (2 if y_pair else 1) + 1
    merge_refs = refs[:n_merge]
    (g_ref, sc_ref, sh_ref, whi_ref, wlo_ref, b_ref,
     y_out_ref, h_ref, ti_ref, tp_ref, rk_ref, cnt_ref, base_ref) = refs[n_merge:]

    @pl.when(pl.program_id(0) == 0)
    def _():
        base_ref[...] = jnp.zeros(base_ref.shape, F32)

    y = _merge_value(merge_refs, even, y_pair)
    y_out_ref[...] = y
    h = _norm_mod(y, g_ref[...], sc_ref[...], sh_ref[...])
    _store_slabs(h_ref, h)
    h_hi = h.astype(BF16)
    h_lo = (h - h_hi.astype(F32)).astype(BF16)
    logits = (_dot(h_hi, whi_ref[...]) + (_dot(h_hi, wlo_ref[...]) + _dot(h_lo, whi_ref[...]))
              + b_ref[...])
    lane = lax.broadcasted_iota(jnp.int32, logits.shape, 1)
    lane_f = lane.astype(F32)
    vals, idxs = [], []
    for _ in range(TOP_K):
        mx = jnp.max(logits, axis=-1, keepdims=True)
        ix = jnp.min(jnp.where(logits == mx, lane_f, float(LANES)), axis=-1, keepdims=True)
        vals.append(mx)
        idxs.append(ix)
        logits = jnp.where(lane_f == ix, -jnp.inf, logits)
    es = [jnp.exp(v - vals[0]) for v in vals]
    tot = es[0] + es[1] + es[2] + es[3]
    ti = jnp.zeros(logits.shape, F32)
    tp = jnp.zeros(logits.shape, F32)
    for k in range(TOP_K):
        ti = jnp.where(lane == k, idxs[k], ti)
        tp = jnp.where(lane == k, es[k] / tot, tp)
    ti_ref[...] = ti.T[0:8, :].astype(jnp.int32)
    tp_ref[...] = tp
    onehots = [(lane_f == ix).astype(F32) for ix in idxs]
    cnt = onehots[0] + onehots[1] + onehots[2] + onehots[3]
    row = lax.broadcasted_iota(jnp.int32, (TM, TM), 0)
    col = lax.broadcasted_iota(jnp.int32, (TM, TM), 1)
    before = _dot((col < row).astype(BF16), cnt.astype(BF16)) + base_ref[...]
    rk = jnp.zeros(logits.shape, F32)
    for k in range(TOP_K):
        rk = jnp.where(lane == k, jnp.sum(onehots[k] * before, axis=-1, keepdims=True), rk)
    rk_ref[...] = rk.T[0:8, :].astype(jnp.int32)
    base_ref[...] = base_ref[...] + jnp.sum(cnt, axis=0, keepdims=True)
    cnt_ref[...] = base_ref[...]


def _merge_router(y, gate, w_out_bf16, a, b, gain, scale, shift, rw, rb, *, p=None, norm_gain=None):
    merge_in_specs, merge_args = _merge_specs(y, gate, w_out_bf16, a, b, p, norm_gain)
    vec = pl.BlockSpec((None, 1, D_MODEL), lambda i: (_mod_row(i), 0, 0))
    rw_p = jnp.zeros((D_MODEL, LANES), F32).at[:, :N_EXPERTS].set(rw)
    rb_p = jnp.full((1, LANES), NEG_BIG, F32).at[0, :N_EXPERTS].set(rb)
    rw_hi = rw_p.astype(BF16)
    tile = lambda w: pl.BlockSpec((TM, w), lambda i: (i, 0))
    by_choice = pl.BlockSpec((None, 8, TM), lambda i: (i, 0, 0))
    return pl.pallas_call(
        functools.partial(_merge_router_kernel, even=p is not None, y_pair=isinstance(y, tuple)),
        out_shape=[jax.ShapeDtypeStruct((T_ALL, D_MODEL), F32),
                   jax.ShapeDtypeStruct((T_ALL * SLAB, LANES), F32),
                   jax.ShapeDtypeStruct((N_TILES, 8, TM), jnp.int32),
                   jax.ShapeDtypeStruct((T_ALL, LANES), F32),
                   jax.ShapeDtypeStruct((N_TILES, 8, TM), jnp.int32),
                   jax.ShapeDtypeStruct((1, LANES), F32)],
        grid=(N_TILES,),
        in_specs=merge_in_specs + [
            pl.BlockSpec((1, D_MODEL), lambda i: (0, 0)), vec, vec,
            pl.BlockSpec((D_MODEL, LANES), lambda i: (0, 0)), pl.BlockSpec((D_MODEL, LANES), lambda i: (0, 0)),
            pl.BlockSpec((1, LANES), lambda i: (0, 0))],
        out_specs=[tile(D_MODEL), pl.BlockSpec((TM * SLAB, LANES), lambda i: (i, 0)), by_choice, tile(LANES),
                   by_choice, pl.BlockSpec((1, LANES), lambda i: (0, 0))],
        scratch_shapes=[pltpu.VMEM((1, LANES), F32)],
        compiler_params=_params(("arbitrary",)),
        name="merge_router",
    )(*merge_args, gain.reshape(1, D_MODEL), scale, shift, rw_hi, (rw_p - rw_hi.astype(F32)).astype(BF16), rb_p)


def _route_plan(top_i, rank, counts):
    experts = jnp.arange(N_EXPERTS, dtype=jnp.int32)
    padded = ((counts + MOE_TM - 1) // MOE_TM) * MOE_TM
    seg_end = jnp.cumsum(padded)
    seg_start = seg_end - padded
    pos = rank
    for e in range(N_EXPERTS - 1):
        pos = pos + jnp.where(top_i > e, padded[e], 0)
    n_active = seg_end[-1] // MOE_TM
    fill = jnp.concatenate([seg_start + counts, padded - counts, n_active[None]]).astype(jnp.int32)
    tile_start = jnp.arange(MOE_TILES, dtype=jnp.int32) * MOE_TM
    tile_expert = jnp.sum((seg_end[None, :] <= tile_start[:, None]).astype(jnp.int32), axis=1)
    last = jnp.sum((seg_end <= (n_active - 1) * MOE_TM).astype(jnp.int32))
    tile_expert = jnp.minimum(jnp.where(tile_start < seg_end[-1], tile_expert, last), N_EXPERTS - 1)
    owns = (padded > 0).astype(jnp.int32)
    run_of_expert = jnp.cumsum(owns) - 1
    run_expert = jnp.sum(jnp.where((run_of_expert[None, :] == experts[:, None]) & (owns[None, :] > 0),
                                   experts[None, :], 0), axis=1)
    runs = jnp.concatenate([run_expert, jnp.sum(owns)[None]]).astype(jnp.int32)
    tile_run = jnp.sum(jnp.where(tile_expert[:, None] == experts[None, :], run_of_expert[None, :], 0), axis=1)
    experts_plan = (tile_expert.astype(jnp.int32), n_active.reshape(1).astype(jnp.int32),
                    tile_run.astype(jnp.int32), runs)
    return pos.astype(jnp.int32), fill, experts_plan


DMA_UNROLL = 4
DMA_QUEUES = 2


def _wait_slabs(ref, n_slabs, sem):
    view = ref.at[pl.ds(0, n_slabs * SLAB)]
    pltpu.make_async_copy(view, view, sem).wait()


def _dispatch_kernel(pos_ref, fill_ref, h_ref, xs_ref, inv_ref, sem):
    i = pl.program_id(0)
    base = i * (TM * TOP_K)

    def issue(j, carry):
        for u in range(DMA_UNROLL):
            t = j * DMA_UNROLL + u
            for k in range(TOP_K):
                a = base + k * TM + t
                slot = pos_ref[a]
                inv_ref[slot] = a
                pltpu.make_async_copy(_slab(h_ref, t), _slab(xs_ref, slot), sem).start(priority=k % DMA_QUEUES)
        return carry

    lax.fori_loop(0, TM // DMA_UNROLL, issue, 0)

    @pl.when(i == 0)
    def _():
        def per_expert(e, total):
            start = fill_ref[e]
            n = fill_ref[N_EXPERTS + e]

            def one(r, carry):
                slot = start + r
                inv_ref[slot] = N_ASSIGN + slot % MOE_TM
                pltpu.make_async_copy(_slab(h_ref, 0), _slab(xs_ref, slot), sem).start()
                return carry

            lax.fori_loop(0, n, one, 0)
            return total + n

        total = lax.fori_loop(0, N_EXPERTS, per_expert, 0)

        n_active = fill_ref[2 * N_EXPERTS]

        def unused_tile(ti, carry):
            dst = xs_ref.at[pl.ds(pl.multiple_of(ti * (MOE_TM * SLAB), MOE_TM * SLAB), MOE_TM * SLAB)]
            pltpu.make_async_copy(h_ref, dst, sem).start()
            return carry

        lax.fori_loop(n_active, MOE_TILES, unused_tile, 0)

        def unused_slot(slot, carry):
            inv_ref[slot] = N_ASSIGN
            return carry

        lax.fori_loop(n_active * MOE_TM, MOE_ROWS, unused_slot, 0)
        total = total + (MOE_TILES - n_active) * MOE_TM

        @pl.when(total > 0)
        def _():
            _wait_slabs(xs_ref, total, sem)

    _wait_slabs(xs_ref, TM * TOP_K, sem)


def _dispatch(h_slabs, pos, fill):
    grid_spec = pltpu.PrefetchScalarGridSpec(
        num_scalar_prefetch=2,
        grid=(N_TILES,),
        in_specs=[pl.BlockSpec((TM * SLAB, LANES), lambda i, pos, fill: (i, 0))],
        out_specs=[pl.BlockSpec(memory_space=pl.ANY), pl.BlockSpec(memory_space=pltpu.SMEM)],
        scratch_shapes=[pltpu.SemaphoreType.DMA],
    )
    return pl.pallas_call(
        _dispatch_kernel,
        out_shape=[jax.ShapeDtypeStruct((MOE_ROWS * SLAB, LANES), F32),
                   jax.ShapeDtypeStruct((MOE_ROWS,), jnp.int32)],
        grid_spec=grid_spec,
        compiler_params=_params(("arbitrary",)),
        name="moe_dispatch",
    )(pos, fill, h_slabs)


MOE_HALF = MOE_TM // 2


def _moe_kernel(te_ref, na_ref, ts_ref, ex_ref, inv_ref, x_ref, wgu_hbm, bgu_ref, wd_hbm, bd_ref, out_hbm,
                wgu_f32, wd_f32, wgu_bf, wd_bf, obuf0, obuf1, sems, osems, *, layer):
    i = pl.program_id(0)
    s = ts_ref[i]
    first = (i == 0) | (s != ts_ref[jnp.maximum(i - 1, 0)])
    n_active = na_ref[0]

    def start_rows(buf, half, tile):
        for r in range(MOE_HALF):
            dst = _slab(out_hbm, inv_ref[tile * MOE_TM + half * MOE_HALF + r])
            pltpu.make_async_copy(buf.at[pl.ds(r * SLAB, SLAB)], dst, osems.at[half]).start(
                priority=r % DMA_QUEUES)

    def wait_rows(half):
        _wait_slabs(out_hbm, MOE_HALF, osems.at[half])

    def ffn_tile(deferred):
        if deferred:
            start_rows(obuf1, 1, i - 1)
            wait_rows(0)
        x = _load_slabs(x_ref, MOE_TM).astype(BF16)
        gu = _dot(x, wgu_bf[...]) + bgu_ref[...]
        gate = jnp.minimum(gu[:, :D_FF], SWIGLU_LIMIT)
        up = jnp.clip(gu[:, D_FF:], -SWIGLU_LIMIT, SWIGLU_LIMIT)
        act = ((up + 1.0) * gate * jax.nn.sigmoid(SWIGLU_ALPHA * gate)).astype(BF16)
        o0 = _dot(act[:MOE_HALF], wd_bf[...]) + bd_ref[...]
        _store_slabs(obuf0, o0)
        start_rows(obuf0, 0, i)
        o1 = _dot(act[MOE_HALF:], wd_bf[...]) + bd_ref[...]
        if deferred:
            wait_rows(1)
        _store_slabs(obuf1, o1)

    def weight_copies(slot):
        e = ex_ref[slot]
        b = slot % 2
        return (pltpu.make_async_copy(wgu_hbm.at[layer, e], wgu_f32.at[b], sems.at[0, b]),
                pltpu.make_async_copy(wd_hbm.at[layer, e], wd_f32.at[b], sems.at[1, b]))

    @pl.when(i == 0)
    def _():
        for cp in weight_copies(0):
            cp.start()

    @pl.when(first)
    def _():
        for cp in weight_copies(s):
            cp.wait()

        @pl.when(s + 1 < ex_ref[N_EXPERTS])
        def _():
            for cp in weight_copies(s + 1):
                cp.start()

        b = s % 2
        wgu_bf[...] = wgu_f32[b].astype(BF16)
        wd_bf[...] = wd_f32[b].astype(BF16)

    @pl.when(i == 0)
    def _():
        ffn_tile(False)

    @pl.when((i > 0) & (i < n_active))
    def _():
        ffn_tile(True)

    @pl.when(i == n_active)
    def _():
        wait_rows(0)
        start_rows(obuf1, 1, i - 1)
        wait_rows(1)
        for half, buf in enumerate((obuf0, obuf1)):
            dst = out_hbm.at[pl.ds((N_ASSIGN + half * MOE_HALF) * SLAB, MOE_HALF * SLAB)]
            pltpu.make_async_copy(buf, dst, osems.at[half]).start()
        wait_rows(0)
        wait_rows(1)


def _moe_experts(layer, xs, plan, inv, w_gu, b_gu, w_down, b_down):
    const = lambda i, te, na, ts, ex, inv: (layer, te[i], 0, 0)
    grid_spec = pltpu.PrefetchScalarGridSpec(
        num_scalar_prefetch=5,
        grid=(MOE_TILES,),
        in_specs=[
            pl.BlockSpec((MOE_TM * SLAB, LANES), lambda i, te, na, ts, ex, inv: (jnp.minimum(i, na[0] - 1), 0)),
            pl.BlockSpec(memory_space=pl.ANY),
            pl.BlockSpec((None, None, 1, 2 * D_FF), const),
            pl.BlockSpec(memory_space=pl.ANY),
            pl.BlockSpec((None, None, 1, D_MODEL), const),
        ],
        out_specs=pl.BlockSpec(memory_space=pl.ANY),
        scratch_shapes=[pltpu.VMEM((2, D_MODEL, 2 * D_FF), F32), pltpu.VMEM((2, D_FF, D_MODEL), F32),
                        pltpu.VMEM((D_MODEL, 2 * D_FF), BF16), pltpu.VMEM((D_FF, D_MODEL), BF16),
                        pltpu.VMEM((MOE_HALF * SLAB, LANES), F32), pltpu.VMEM((MOE_HALF * SLAB, LANES), F32),
                        pltpu.SemaphoreType.DMA((2, 2)), pltpu.SemaphoreType.DMA((2,))],
    )
    return pl.pallas_call(
        functools.partial(_moe_kernel, layer=layer),
        out_shape=jax.ShapeDtypeStruct(((N_ASSIGN + MOE_TM) * SLAB, LANES), F32),
        grid_spec=grid_spec,
        compiler_params=_params(("arbitrary",)),
        name="moe_experts",
    )(*plan, inv, xs, w_gu, b_gu.reshape(DEPTH, N_EXPERTS, 1, 2 * D_FF), w_down,
      b_down.reshape(DEPTH, N_EXPERTS, 1, D_MODEL))


def _final_combine_kernel(out_ref, y_ref, tp_ref, g_ref, fg_ref, n_ctx_ref, n_lat_ref):
    chunks = _moe_combine_chunks(out_ref, y_ref, tp_ref, g_ref)
    ss = jnp.zeros((TM, 1), F32)
    for yc in chunks:
        ss = ss + jnp.sum(yc * yc, axis=-1, keepdims=True)
    inv = lax.rsqrt(ss * (1.0 / D_MODEL) + EPS)

    def store(n_ref):
        for c in range(SLAB):
            cs = slice(c * LANES, (c + 1) * LANES)
            n_ref[:, cs] = chunks[c] * inv * fg_ref[:, cs]

    @pl.when(_is_ctx_tile())
    def _():
        store(n_ctx_ref)

    @pl.when(jnp.logical_not(_is_ctx_tile()))
    def _():
        store(n_lat_ref)


def _final_combine(y, out_slabs, top_p, gate, final_gain):
    tile = pl.BlockSpec((TM, D_MODEL), lambda i: (i, 0))
    return pl.pallas_call(
        _final_combine_kernel,
        out_shape=[jax.ShapeDtypeStruct((T_CTX, D_MODEL), F32), jax.ShapeDtypeStruct((T_LAT, D_MODEL), F32)],
        grid=(N_TILES,),
        in_specs=[pl.BlockSpec((TOP_K * TM * SLAB, LANES), lambda i: (i, 0)), tile,
                  pl.BlockSpec((TM, LANES), lambda i: (i, 0)),
                  pl.BlockSpec((None, 1, D_MODEL), lambda i: (_mod_row(i), 0, 0)),
                  pl.BlockSpec((1, D_MODEL), lambda i: (0, 0))],
        out_specs=[pl.BlockSpec((TM, D_MODEL), lambda i: (jnp.minimum(i, CTX_TILES - 1), 0)),
                   pl.BlockSpec((TM, D_MODEL), lambda i: (jnp.maximum(i - CTX_TILES, 0), 0))],
        compiler_params=_params(("arbitrary",)),
        name="moe_combine",
    )(out_slabs, y, top_p, gate, final_gain.reshape(1, D_MODEL))


def _moe_experts_layer(layer, routed, w_gu, b_gu, w_down, b_down):
    h_slabs, top_i, top_p, rank, counts = routed
    pos, fill, experts_plan = _route_plan(top_i[:, :TOP_K].reshape(-1), rank[:, :TOP_K].reshape(-1),
                                          counts[0, :N_EXPERTS].astype(jnp.int32))
    xs, inv = _dispatch(h_slabs, pos, fill)
    return _moe_experts(layer, xs, experts_plan, inv, w_gu, b_gu, w_down, b_down), top_p


def kernel(x_prompt, x_sample, c, cache_a_k, cache_a_v, state_b_C, state_b_n, state_b_m, cache_c_k, cache_c_v, cache_d_k, cache_d_v, c_ctx, w_mod, b_mod, norm1_g, norm2_g, w_in_even, w_out_even, a_q_gain, a_k_gain, b_gate_bias, b_norm_gain, w_in_odd, w_out_odd, c_rpb, d_lambda, d_norm_gain, router_w, router_b, expert_w_gu, expert_b_gu, expert_w_down, expert_b_down, final_norm_g):
    y = (x_prompt.reshape(T_CTX, D_MODEL), x_sample.reshape(T_LAT, D_MODEL))
    cond = jnp.zeros((MOD_ROWS, D_MODEL), F32).at[0].set(c_ctx).at[1:1 + DEC_BATCH].set(c)
    mod = _modulation(cond, w_mod, b_mod).reshape(DEPTH, MOD_ROWS, 6, 1, D_MODEL)
    rope_cos, rope_sin = _rope_tables()
    scale = HD ** -0.5
    outs = {}

    for layer in range(DEPTH):
        sh1, sc1, g1, sh2, sc2, g2 = (mod[layer, :, k] for k in range(6))
        j = layer // 2
        if layer % 2 == 0:
            w = w_in_even[j]
            sizes = np.cumsum([0, 512, 128, 128, 512, 512, 512, 512, 16])
            aq, ak, av, bq, bk, bv, bo, bg = (w[:, sizes[k]:sizes[k + 1]] for k in range(8))
            w_in = jnp.concatenate([aq, bo, bq, bk, bv, ak, av, bg, jnp.zeros((D_MODEL, EV_N - EV_BG - 16), F32)],
                                   axis=1).astype(BF16)
            qg = jnp.tile(a_q_gain[j], LANES // HD).reshape(1, LANES)
            kg = jnp.tile(a_k_gain[j], LANES // HD).reshape(1, LANES)
            specs = ((EV_AQ, 512, 0, True, scale, BF16, None), (EV_AK, 128, 1, True, 1.0, BF16, None),
                     (EV_AV, 128, None, False, 1.0, BF16, None),
                     (EV_AK, 128, 1, False, 1.0, F32, A_KV), (EV_AV, 128, None, False, 1.0, F32, A_KV))
            y, p, (qa, ka, va, new_ak, new_av) = _project(y, norm1_g[layer], sc1, sh1, w_in, rope_cos, rope_sin,
                                                          [qg, kg], specs)
            oa_ctx = _attention(qa, ka, va, ctx=True, group=A_HEADS // A_KV, n_kv=A_KV)
            oa_lat = _attention(qa, ka, va, ctx=False, group=A_HEADS // A_KV, n_kv=A_KV,
                                cache=(cache_a_k, cache_a_v, j))
            zc = jnp.zeros((BATCH, 2, B_HEADS, B_DK, B_DV), F32)
            zn = jnp.zeros((BATCH, 2, B_HEADS, 1, B_DK), F32)
            zm = jnp.zeros((BATCH, 2, B_HEADS, 1, LANES), F32)
            hb_ctx, bC, bn, bm = _mlstm(p, b_gate_bias[j], zc, zn, zm, ctx=True)
            m0 = jnp.broadcast_to(state_b_m[:, j][..., None, None], (DEC_BATCH, 2, B_HEADS, 1, LANES))
            hb_lat, _, _, _ = _mlstm(p, b_gate_bias[j], state_b_C[:, j], state_b_n[:, j][:, :, :, None, :], m0,
                                     ctx=False)
            y, *routed = _merge_router(y, g1, w_out_even[j].astype(BF16), (oa_ctx, oa_lat), (hb_ctx, hb_lat),
                                       norm2_g[layer], sc2, sh2, router_w[layer], router_b[layer],
                                       p=p, norm_gain=b_norm_gain[j])
            outs.setdefault("a_k", []).append(new_ak)
            outs.setdefault("a_v", []).append(new_av)
            outs.setdefault("b_C", []).append(bC)
            outs.setdefault("b_n", []).append(bn[:, :, :, 0, :])
            outs.setdefault("b_m", []).append(bm[:, :, :, 0, 0])
        else:
            specs = ((0, 512, None, False, scale, BF16, None), (512, 512, None, False, 1.0, BF16, None),
                     (1024, 512, None, False, 1.0, BF16, None), (1536, 512, None, True, scale, BF16, None),
                     (2048, 512, None, True, 1.0, BF16, None), (2560, 512, None, False, 1.0, BF16, None),
                     (512, 512, None, False, 1.0, F32, C_HEADS), (1024, 512, None, False, 1.0, F32, C_HEADS),
                     (2048, 512, None, False, 1.0, F32, 2 * D_HEADS), (2560, 512, None, False, 1.0, F32, D_HEADS))
            y, p, (qc, kc, vc, qd, kd, vd, new_ck, new_cv, new_dk, new_dv) = _project(
                y, norm1_g[layer], sc1, sh1, w_in_odd[j].astype(BF16), rope_cos, rope_sin, [], specs)
            lam_init = 0.8 - 0.6 * math.exp(-0.3 * layer)
            lp = d_lambda[j].astype(F32)
            lam = jnp.exp(jnp.sum(lp[0] * lp[1])) - jnp.exp(jnp.sum(lp[2] * lp[3])) + lam_init
            lam_vec = jnp.stack([lam, jnp.asarray(1.0 - lam_init, F32)]).astype(F32)
            diff = (lam_vec, d_norm_gain[j])
            oc_ctx = _attention(qc, kc, vc, ctx=True, n_kv=C_HEADS)
            od_ctx = _attention(qd, kd, vd, ctx=True, diff=diff)
            bias = _neighbourhood_bias(c_rpb[j])
            oc_lat = _attention(qc, kc, vc, ctx=False, n_kv=C_HEADS, bias=bias,
                                cache=(cache_c_k, cache_c_v, j))
            kd_cache = cache_d_k.reshape(DEC_BATCH, -1, 2 * D_HEADS, PAST_LEN, HD)
            od_lat = _attention(qd, kd, vd, ctx=False, diff=diff, cache=(kd_cache, cache_d_v, j))
            y, *routed = _merge_router(y, g1, w_out_odd[j].astype(BF16), (oc_ctx, oc_lat), (od_ctx, od_lat),
                                       norm2_g[layer], sc2, sh2, router_w[layer], router_b[layer])
            outs.setdefault("c_k", []).append(new_ck)
            outs.setdefault("c_v", []).append(new_cv)
            outs.setdefault("d_k", []).append(new_dk.reshape(BATCH, D_HEADS, 2, SEQ, HD))
            outs.setdefault("d_v", []).append(new_dv)
        out_slabs, top_p = _moe_experts_layer(layer, routed, expert_w_gu, expert_b_gu, expert_w_down,
                                              expert_b_down)
        y = (y, out_slabs, top_p, g2)
    y_prompt, y_sample = _final_combine(*y, final_norm_g)
    stack = lambda k: jnp.stack(outs[k], axis=1)
    return (y_prompt.reshape(BATCH, SEQ, D_MODEL), y_sample.reshape(DEC_BATCH, DEC_SEQ, D_MODEL),
            stack("a_k"), stack("a_v"), stack("b_C"), stack("b_n"), stack("b_m"),
            stack("c_k"), stack("c_v"), stack("d_k"), stack("d_v"))
```

```python
import functools
import math

import numpy as np
import jax
import jax.numpy as jnp
from jax import lax
from jax.experimental import pallas as pl
from jax.experimental.pallas import tpu as pltpu

D_MODEL = 1024
BATCH = 32
SEQ = 256
DEPTH = 2
DEC_BATCH = 8
DEC_SEQ = 1024
PAST_LEN = 512
GRID_W = 64
HD = 64
A_HEADS = 8
A_KV = 2
B_HEADS = 4
B_DK = 128
B_DV = 128
B_CHUNK = 128
C_HEADS = 8
NA_ROWS = 8
NA_COLS = 16
D_HEADS = 4
D_VDIM = 2 * HD
N_EXPERTS = 32
TOP_K = 4
D_FF = 1024
SWIGLU_LIMIT = 7.0
SWIGLU_ALPHA = 1.702
ROPE_THETA = 10000.0
EPS = 1e-6

F32 = jnp.float32
BF16 = jnp.bfloat16
HIGHEST = lax.Precision.HIGHEST

T_CTX = BATCH * SEQ
T_LAT = DEC_BATCH * DEC_SEQ
T_ALL = T_CTX + T_LAT
TM = 256
CTX_TILES = T_CTX // TM
LAT_TILES_PER_BATCH = DEC_SEQ // TM
N_TILES = T_ALL // TM
MOD_ROWS = 16
LANES = 128
NEG_BIG = -1e30
MOE_TM = 256
N_ASSIGN = T_ALL * TOP_K
MOE_ROWS = N_ASSIGN + N_EXPERTS * MOE_TM
MOE_TILES = MOE_ROWS // MOE_TM
VMEM_LIMIT = 56 * 1024 * 1024
assert TM == SEQ and DEC_SEQ % TM == 0 and MOE_TM == TM

EV_AQ, EV_BO, EV_BQ, EV_BK, EV_BV, EV_AK, EV_AV, EV_BG = 0, 512, 1024, 1536, 2048, 2560, 2688, 2816
EV_N = 2944


def _params(sem, vmem=VMEM_LIMIT):
    return pltpu.CompilerParams(dimension_semantics=sem, vmem_limit_bytes=vmem)


def _mod_row(i):
    return jnp.where(i < CTX_TILES, 0, 1 + (i - CTX_TILES) // LAT_TILES_PER_BATCH)


def _rope_block(i):
    return jnp.where(i < CTX_TILES, LAT_TILES_PER_BATCH, (i - CTX_TILES) % LAT_TILES_PER_BATCH)


def _dot(a, b, precision=None):
    return jnp.dot(a, b, preferred_element_type=F32, precision=precision)


def _dot_nt(a, b):
    return lax.dot_general(a, b, (((1,), (1,)), ((), ())), preferred_element_type=F32)


def _dot_tn(a, b):
    return lax.dot_general(a, b, (((0,), (0,)), ((), ())), preferred_element_type=F32)


def _modulation_kernel(c_ref, w_ref, b_ref, o_ref):
    c = c_ref[...]
    s = c * jax.nn.sigmoid(c)
    o_ref[...] = _dot(s, w_ref[...], HIGHEST) + b_ref[...]


def _modulation(cond, w_mod, b_mod):
    tn = 1536
    return pl.pallas_call(
        _modulation_kernel,
        out_shape=jax.ShapeDtypeStruct((DEPTH, MOD_ROWS, 6 * D_MODEL), F32),
        grid=(DEPTH, 6 * D_MODEL // tn),
        in_specs=[
            pl.BlockSpec((MOD_ROWS, D_MODEL), lambda l, j: (0, 0)),
            pl.BlockSpec((None, D_MODEL, tn), lambda l, j: (l, 0, j)),
            pl.BlockSpec((None, 1, tn), lambda l, j: (l, 0, j)),
        ],
        out_specs=pl.BlockSpec((None, MOD_ROWS, tn), lambda l, j: (l, 0, j)),
        compiler_params=_params(("parallel", "parallel")),
        name="modulation",
    )(cond, w_mod, b_mod.reshape(DEPTH, 1, 6 * D_MODEL))


def _norm_mod(y, g, sc, sh):
    ms = jnp.mean(y * y, axis=-1, keepdims=True)
    return (y * lax.rsqrt(ms + EPS) * g) * (1.0 + sc) + sh


def _rope_rotate(x):
    w = x.shape[-1]
    lane = lax.broadcasted_iota(jnp.int32, x.shape, 1)
    nxt = pltpu.roll(x, w - 1, 1)
    prv = pltpu.roll(x, 1, 1)
    return jnp.where((lane & 1) == 0, -nxt, prv)


def _is_ctx_tile():
    return pl.program_id(0) < CTX_TILES


def _pair_specs(width):
    return [pl.BlockSpec((TM, width), lambda i: (jnp.minimum(i, CTX_TILES - 1), 0)),
            pl.BlockSpec((TM, width), lambda i: (jnp.maximum(i - CTX_TILES, 0), 0))]


def _pair_value(ctx_ref, lat_ref, cols=slice(None)):
    return jnp.where(_is_ctx_tile(), ctx_ref[:, cols], lat_ref[:, cols])


def _moe_combine_chunks(out_ref, y_ref, tp_ref, g_ref):
    tp = tp_ref[...]
    chunks = []
    for c in range(SLAB):
        cs = slice(c * LANES, (c + 1) * LANES)
        acc = tp[:, 0:1] * out_ref[pl.ds(c, TM, stride=SLAB), :]
        for k in range(1, TOP_K):
            acc = acc + tp[:, k:k + 1] * out_ref[pl.ds(k * TM * SLAB + c, TM, stride=SLAB), :]
        chunks.append(y_ref[:, cs] + g_ref[:, cs] * acc)
    return chunks


def _norm_proj_kernel(*refs, specs, n_gain, y_mode):
    n_y = {"single": 1, "pair": 2, "combine": 4}[y_mode]
    g_ref, sc_ref, sh_ref, w_ref, cos_ref, sin_ref, bd_ref = refs[n_y:n_y + 7]
    gain_refs = refs[n_y + 7:n_y + 7 + n_gain]
    p_ref = refs[n_y + 7 + n_gain]
    out_refs = refs[n_y + 8 + n_gain:]
    if y_mode == "combine":
        y = jnp.concatenate(_moe_combine_chunks(*refs[:4]), axis=1)
        out_refs[0][...] = y
        out_refs = out_refs[1:]
    elif y_mode == "pair":
        y = _pair_value(refs[0], refs[1])
    else:
        y = refs[0][...]
    h = _norm_mod(y, g_ref[...], sc_ref[...], sh_ref[...])
    p_ref[...] = _dot(h.astype(BF16), w_ref[...])
    cos = cos_ref[...]
    sin = sin_ref[...]
    for (col, width, gi, rope, scale, _, heads), o_ref in zip(specs, out_refs):
        for c0 in range(0, width, LANES):
            x = p_ref[:, col + c0:col + c0 + LANES]
            if gi is not None:
                xx = x * x
                xx_hi = xx.astype(BF16)
                xx_lo = (xx - xx_hi.astype(F32)).astype(BF16)
                ss = _dot(xx_hi, bd_ref[...]) + _dot(xx_lo, bd_ref[...])
                x = x * lax.rsqrt(ss * (1.0 / HD) + EPS) * gain_refs[gi][...]
            if rope:
                x = x * cos + _rope_rotate(x) * sin
            if scale != 1.0:
                x = x * scale
            if heads is None:
                o_ref[:, c0:c0 + LANES] = x.astype(o_ref.dtype)
            else:
                hw = width // heads
                per = LANES // hw

                @pl.when(_is_ctx_tile())
                def _(x=x, o_ref=o_ref, c0=c0, hw=hw, per=per):
                    for u in range(per):
                        o_ref[(c0 // LANES) * per + u] = x[:, u * hw:(u + 1) * hw].astype(o_ref.dtype)


def _norm_proj(y, gain, scale, shift, w_bf16, rope_cos, rope_sin, gains, specs):
    n = w_bf16.shape[1]
    tile = pl.BlockSpec((TM, D_MODEL), lambda i: (i, 0))
    vec = pl.BlockSpec((None, 1, D_MODEL), lambda i: (_mod_row(i), 0, 0))
    if not isinstance(y, tuple):
        y_mode, ys, in_specs = "single", [y], [tile]
    elif len(y) == 2:
        y_mode, ys, in_specs = "pair", list(y), _pair_specs(D_MODEL)
    else:
        y_prev, out_slabs, top_p, gate = y
        y_mode, ys = "combine", [out_slabs, y_prev, top_p, gate]
        in_specs = [pl.BlockSpec((TOP_K * TM * SLAB, LANES), lambda i: (i, 0)), tile,
                    pl.BlockSpec((TM, LANES), lambda i: (i, 0)), vec]
    bd = jnp.asarray(np.kron(np.eye(LANES // HD), np.ones((HD, HD))), BF16)
    rope_spec = pl.BlockSpec((TM, LANES), lambda i: (_rope_block(i), 0))
    in_specs += [pl.BlockSpec((1, D_MODEL), lambda i: (0, 0)),
                 vec, vec, pl.BlockSpec((D_MODEL, n), lambda i: (0, 0)),
                 rope_spec, rope_spec, pl.BlockSpec((LANES, LANES), lambda i: (0, 0))]
    in_specs += [pl.BlockSpec((1, LANES), lambda i: (0, 0)) for _ in gains]
    out_shape = [jax.ShapeDtypeStruct((T_ALL, n), F32)]
    out_specs = [pl.BlockSpec((TM, n), lambda i: (i, 0))]
    if y_mode == "combine":
        out_shape.append(jax.ShapeDtypeStruct((T_ALL, D_MODEL), F32))
        out_specs.append(tile)
    for (_, width, _, _, _, dtype, heads) in specs:
        if heads is None:
            out_shape.append(jax.ShapeDtypeStruct((T_ALL, width), dtype))
            out_specs.append(pl.BlockSpec((TM, width), lambda i: (i, 0)))
        else:
            out_shape.append(jax.ShapeDtypeStruct((BATCH, heads, SEQ, width // heads), dtype))
            out_specs.append(pl.BlockSpec((None, heads, SEQ, width // heads),
                                          lambda i: (jnp.minimum(i, CTX_TILES - 1), 0, 0, 0)))
    return pl.pallas_call(
        functools.partial(_norm_proj_kernel, specs=specs, n_gain=len(gains), y_mode=y_mode),
        out_shape=out_shape,
        grid=(N_TILES,),
        in_specs=in_specs,
        out_specs=out_specs,
        compiler_params=_params(("arbitrary",)),
        name="norm_proj",
    )(*ys, gain.reshape(1, D_MODEL), scale, shift, w_bf16, rope_cos, rope_sin, bd, *gains)


def _project(y, *args):
    res = _norm_proj(y, *args)
    if isinstance(y, tuple) and len(y) == 4:
        return res[1], res[0], res[2:]
    return y, res[0], res[1:]


def _rope_tables():
    half = HD // 2
    freqs = 1.0 / (ROPE_THETA ** (jnp.arange(0, half, 2, dtype=F32) / half))
    t = jnp.arange(DEC_SEQ)
    rows = (t // GRID_W).astype(F32)
    cols = (t % GRID_W).astype(F32)
    ang = jnp.concatenate([rows[:, None] * freqs, cols[:, None] * freqs], axis=-1)
    cos = jnp.repeat(jnp.cos(ang), 2, axis=-1)
    sin = jnp.repeat(jnp.sin(ang), 2, axis=-1)
    cos = jnp.concatenate([jnp.tile(cos, (1, LANES // HD)), jnp.ones((TM, LANES), F32)], axis=0)
    sin = jnp.concatenate([jnp.tile(sin, (1, LANES // HD)), jnp.zeros((TM, LANES), F32)], axis=0)
    return cos, sin


def _lane_slice(ref, h, width=HD):
    per = LANES // width
    blk = ref[:, (h // per) * LANES:(h // per + 1) * LANES]
    if per == 1:
        return blk
    return blk[:, (h % per) * width:(h % per + 1) * width]


def _softmax_parts(scores):
    m = None
    for s in scores:
        ms = jnp.max(s, axis=-1, keepdims=True)
        m = ms if m is None else jnp.maximum(m, ms)
    ps = [jnp.exp(s - m) for s in scores]
    l = None
    for p in ps:
        ls = jnp.sum(p, axis=-1, keepdims=True)
        l = ls if l is None else l + ls
    return ps, l


def _attn_std_kernel(*refs, group, n_kv, has_cache, has_bias, bq):
    it = iter(refs)
    q_ref, kn_ref, vn_ref = next(it), next(it), next(it)
    kc_ref = vc_ref = b_ref = None
    if has_cache:
        kc_ref, vc_ref = next(it), next(it)
    if has_bias:
        b_ref = next(it)
    o_ref = next(it)
    outs = []
    for g in range(n_kv):
        qs = jnp.concatenate([_lane_slice(q_ref, g * group + j) for j in range(group)], axis=0)
        kn = _lane_slice(kn_ref, g)
        vn = _lane_slice(vn_ref, g)
        s_new = _dot_nt(qs, kn)
        if has_bias:
            s_new = s_new + b_ref[g]
        scores = [s_new]
        if has_cache:
            scores.append(_dot_nt(qs, kc_ref[g]))
        ps, l = _softmax_parts(scores)
        o = _dot(ps[0].astype(BF16), vn)
        if has_cache:
            o = o + _dot(ps[1].astype(BF16), vc_ref[g])
        o = o / l
        for j in range(group):
            outs.append(o[j * bq:(j + 1) * bq])
    o_ref[...] = jnp.concatenate(outs, axis=1).astype(o_ref.dtype)


def _attn_diff_kernel(*refs, has_cache):
    it = iter(refs)
    lam_ref, q_ref, kn_ref, vn_ref = next(it), next(it), next(it), next(it)
    kc_ref = vc_ref = None
    if has_cache:
        kc_ref, vc_ref = next(it), next(it)
    g_ref, o_ref = next(it), next(it)
    lam = lam_ref[0]
    post = lam_ref[1]
    outs = []
    for h in range(D_HEADS):
        pd_new, pd_c = None, None
        for j in range(2):
            f = 2 * h + j
            qs = _lane_slice(q_ref, f)
            scores = [_dot_nt(qs, _lane_slice(kn_ref, f))]
            if has_cache:
                scores.append(_dot_nt(qs, kc_ref[f]))
            ps, l = _softmax_parts(scores)
            r = 1.0 / l
            if j == 0:
                pd_new = ps[0] * r
                pd_c = ps[1] * r if has_cache else None
            else:
                r = r * lam
                pd_new = pd_new - ps[0] * r
                pd_c = pd_c - ps[1] * r if has_cache else None
        o = _dot(pd_new.astype(BF16), _lane_slice(vn_ref, h, D_VDIM))
        if has_cache:
            o = o + _dot(pd_c.astype(BF16), vc_ref[h])
        ms = jnp.mean(o * o, axis=-1, keepdims=True)
        outs.append(o * lax.rsqrt(ms + EPS) * g_ref[...] * post)
    o_ref[...] = jnp.concatenate(outs, axis=1).astype(o_ref.dtype)


def _attention(q, kn, vn, *, ctx, group=1, n_kv=1, cache=None, bias=None, diff=None, bq=256):
    if ctx:
        nb, sq, row0 = BATCH, SEQ, 0
    else:
        nb, sq, row0 = DEC_BATCH, DEC_SEQ, T_CTX
    nq = sq // bq
    qb0 = row0 // bq
    kb0 = row0 // sq
    wq, wk, wv = q.shape[1], kn.shape[1], vn.shape[1]
    in_specs = [
        pl.BlockSpec((bq, wq), lambda b, i: (qb0 + b * nq + i, 0)),
        pl.BlockSpec((sq, wk), lambda b, i: (kb0 + b, 0)),
        pl.BlockSpec((sq, wv), lambda b, i: (kb0 + b, 0)),
    ]
    args = [q, kn, vn]
    if cache is not None:
        kc, vc = cache
        in_specs += [pl.BlockSpec((None,) + kc.shape[1:], lambda b, i: (b, 0, 0, 0)),
                     pl.BlockSpec((None,) + vc.shape[1:], lambda b, i: (b, 0, 0, 0))]
        args += [kc, vc]
    if diff is None:
        if bias is not None:
            in_specs.append(pl.BlockSpec((bias.shape[0], bq, sq), lambda b, i: (0, i, 0)))
            args.append(bias)
        body = functools.partial(_attn_std_kernel, group=group, n_kv=n_kv, has_cache=cache is not None,
                                 has_bias=bias is not None, bq=bq)
    else:
        lam_vec, gain = diff
        in_specs = [pl.BlockSpec(memory_space=pltpu.SMEM)] + in_specs
        args = [lam_vec] + args
        in_specs.append(pl.BlockSpec((1, D_VDIM), lambda b, i: (0, 0)))
        args.append(gain.reshape(1, D_VDIM))
        body = functools.partial(_attn_diff_kernel, has_cache=cache is not None)
    return pl.pallas_call(
        body,
        out_shape=jax.ShapeDtypeStruct((nb * sq, 512), BF16),
        grid=(nb, nq),
        in_specs=in_specs,
        out_specs=pl.BlockSpec((bq, 512), lambda b, i: (b * nq + i, 0)),
        compiler_params=_params(("parallel", "parallel")),
        name="attention",
    )(*args)


GRID_ROWS = DEC_SEQ // GRID_W
NA_WIN_ROWS = min(NA_ROWS, GRID_ROWS)


def _na_bias_kernel(t_ref, o_ref):
    outside = jnp.full((GRID_W, GRID_W), NEG_BIG, F32)
    for qr in range(GRID_ROWS):
        r0 = min(max(qr - NA_WIN_ROWS // 2, 0), GRID_ROWS - NA_WIN_ROWS)
        parts = [t_ref[kr - qr + (NA_ROWS - 1)] if r0 <= kr < r0 + NA_WIN_ROWS else outside
                 for kr in range(GRID_ROWS)]
        o_ref[qr * GRID_W:(qr + 1) * GRID_W, :] = jnp.concatenate(parts, axis=1)


def _neighbourhood_bias(rpb):
    c = np.arange(GRID_W)
    c0 = np.clip(c - NA_COLS // 2, 0, GRID_W - NA_COLS)
    col_ok = (c[None, :] >= c0[:, None]) & (c[None, :] < c0[:, None] + NA_COLS)
    dc = np.clip(c[None, :] - c[:, None], 1 - NA_COLS, NA_COLS - 1) + (NA_COLS - 1)
    oh_c = jnp.asarray(dc[..., None] == np.arange(2 * NA_COLS - 1), F32)
    by_col = jnp.einsum("hrd,qkd->hrqk", rpb.astype(F32), oh_c, precision=HIGHEST)
    by_col = jnp.where(jnp.asarray(col_ok), by_col, NEG_BIG)
    n_dr = 2 * NA_ROWS - 1
    return pl.pallas_call(
        _na_bias_kernel,
        out_shape=jax.ShapeDtypeStruct((C_HEADS, DEC_SEQ, DEC_SEQ), F32),
        grid=(C_HEADS,),
        in_specs=[pl.BlockSpec((None, n_dr, GRID_W, GRID_W), lambda h: (h, 0, 0, 0))],
        out_specs=pl.BlockSpec((None, DEC_SEQ, DEC_SEQ), lambda h: (h, 0, 0)),
        compiler_params=_params(("parallel",)),
        name="na_bias",
    )(by_col)


def _log_sigmoid(x):
    return jnp.minimum(x, 0.0) - jnp.log1p(jnp.exp(-jnp.abs(x)))


def _mlstm_kernel(q_ref, k_ref, v_ref, g_ref, gb_ref, c0_ref, n0_ref, m0_ref, h_ref, c_ref, n_ref, m_ref, hb_ref,
                  *, seq):
    L = B_CHUNK
    nc = seq // L
    row = lax.broadcasted_iota(jnp.int32, (L, L), 0)
    col = lax.broadcasted_iota(jnp.int32, (L, L), 1)
    keeps = (col <= row, col >= row)
    k_scale = B_DK ** -0.5
    c_ref[...] = c0_ref[...]
    n_ref[...] = n0_ref[...]
    m_ref[...] = m0_ref[...]

    def step(j):
        for d in range(2):
            keep = keeps[d]
            c = j if d == 0 else nc - 1 - j
            off = c * L if isinstance(c, int) else pl.multiple_of(c * L, L)
            gates = g_ref[pl.ds(off, L), :] + gb_ref[...]
            cum = _dot(keep.astype(F32), _log_sigmoid(gates), HIGHEST)
            cum_t = cum.T
            gates_t = gates.T
            out_ref = h_ref if d == 0 else hb_ref
            for h in range(B_HEADS):
                ci = (2 * d) * B_HEADS + h
                cf = (2 * d + 1) * B_HEADS + h
                hs = slice(h * B_DK, (h + 1) * B_DK)
                C = c_ref[d, h]
                n = n_ref[d, h]
                m = m_ref[d, h][:, 0:1]
                qc = q_ref[pl.ds(off, L), hs]
                kc = k_ref[pl.ds(off, L), hs] * k_scale
                vc = v_ref[pl.ds(off, L), hs]
                b_col = cum[:, cf:cf + 1]
                i_col = gates[:, ci:ci + 1]
                b_row = cum_t[cf:cf + 1, :]
                i_row = gates_t[ci:ci + 1, :]
                dlog = jnp.where(keep, b_col - b_row + i_row, -jnp.inf)
                inter = b_col + m
                m_t = jnp.maximum(inter, jnp.max(dlog, axis=-1, keepdims=True))
                w_intra = jnp.exp(dlog - m_t)
                w_inter = jnp.exp(inter - m_t)
                qb = qc.astype(BF16)
                vb = vc.astype(BF16)
                qk = _dot_nt(qb, kc.astype(BF16)) * w_intra
                num = _dot(qk.astype(BF16), vb) + w_inter * _dot(qb, C.astype(BF16))
                den = jnp.sum(qk, axis=-1, keepdims=True) + w_inter * jnp.sum(qc * n, axis=-1, keepdims=True)
                out_ref[pl.ds(off, L), hs] = num / jnp.maximum(jnp.abs(den), jnp.exp(-m_t))
                b_last = b_col[L - 1:L, :] if d == 0 else b_col[0:1, :]
                end_col = b_last - b_col + i_col
                m_new = jnp.maximum(b_last + m, jnp.max(end_col, axis=0, keepdims=True))
                w_end = jnp.exp(end_col - m_new)
                decay = jnp.exp(b_last + m - m_new)
                kw = kc * w_end
                c_ref[d, h] = decay * C + _dot_tn(kw.astype(BF16), vb)
                n_ref[d, h] = decay * n + jnp.sum(kw, axis=0, keepdims=True)
                m_ref[d, h] = jnp.broadcast_to(m_new, (1, LANES))

    if nc <= 2:
        for j in range(nc):
            step(j)
    else:
        def body(j, carry):
            step(j)
            return carry

        lax.fori_loop(0, nc, body, 0)
    h_ref[...] = h_ref[...] + hb_ref[...]


def _mlstm(p, gate_bias, c0, n0, m0, *, ctx):
    if ctx:
        nb, seq, blk0 = BATCH, SEQ, 0
    else:
        nb, seq, blk0 = DEC_BATCH, DEC_SEQ, T_CTX // DEC_SEQ
    w = B_HEADS * B_DK

    def cols(c0_, width):
        return pl.BlockSpec((seq, width), lambda b: (blk0 + b, c0_ // width))

    gb = jnp.zeros((1, LANES), F32).at[0, :4 * B_HEADS].set(gate_bias.reshape(-1).astype(F32))
    st = lambda shape: pl.BlockSpec((None,) + shape, lambda b: (b, 0, 0, 0, 0))
    return pl.pallas_call(
        functools.partial(_mlstm_kernel, seq=seq),
        out_shape=[
            jax.ShapeDtypeStruct((nb * seq, w), F32),
            jax.ShapeDtypeStruct((nb, 2, B_HEADS, B_DK, B_DV), F32),
            jax.ShapeDtypeStruct((nb, 2, B_HEADS, 1, B_DK), F32),
            jax.ShapeDtypeStruct((nb, 2, B_HEADS, 1, LANES), F32),
        ],
        grid=(nb,),
        in_specs=[
            cols(EV_BQ, w), cols(EV_BK, w), cols(EV_BV, w), cols(EV_BG, LANES),
            pl.BlockSpec((1, LANES), lambda b: (0, 0)),
            st((2, B_HEADS, B_DK, B_DV)), st((2, B_HEADS, 1, B_DK)), st((2, B_HEADS, 1, LANES)),
        ],
        out_specs=[
            pl.BlockSpec((seq, w), lambda b: (b, 0)),
            st((2, B_HEADS, B_DK, B_DV)), st((2, B_HEADS, 1, B_DK)), st((2, B_HEADS, 1, LANES)),
        ],
        scratch_shapes=[pltpu.VMEM((seq, w), F32)],
        compiler_params=_params(("parallel",)),
        name="mlstm",
    )(p, p, p, p, gb, c0, n0, m0)


def _merge_value(refs, even, y_pair):
    a_ctx, a_lat, b_ctx, b_lat = refs[:4]
    rest = refs[4:]
    if even:
        bo_ref, ng_ref = rest[:2]
        rest = rest[2:]
        parts = [_pair_value(a_ctx, a_lat)]
        for h in range(B_HEADS):
            hs = slice(h * B_DV, (h + 1) * B_DV)
            x = _pair_value(b_ctx, b_lat, hs)
            ms = jnp.mean(x * x, axis=-1, keepdims=True)
            xn = x * lax.rsqrt(ms + EPS) * ng_ref[:, hs]
            parts.append((jax.nn.sigmoid(bo_ref[:, hs]) * xn).astype(BF16))
    else:
        parts = [_pair_value(a_ctx, a_lat), _pair_value(b_ctx, b_lat)]
    w_ref = rest[0]
    y = _pair_value(rest[1], rest[2]) if y_pair else rest[1][...]
    g_ref = rest[-1]
    cat = jnp.concatenate(parts, axis=1)
    return y + g_ref[...] * _dot(cat, w_ref[...])


def _merge_specs(y, gate, w_bf16, a, b, p, norm_gain):
    in_specs = _pair_specs(512) + _pair_specs(512)
    args = [*a, *b]
    if p is not None:
        in_specs += [pl.BlockSpec((TM, 512), lambda i: (i, EV_BO // 512)), pl.BlockSpec((1, 512), lambda i: (0, 0))]
        args += [p, norm_gain.reshape(1, 512)]
    in_specs.append(pl.BlockSpec((D_MODEL, D_MODEL), lambda i: (0, 0)))
    args.append(w_bf16)
    if isinstance(y, tuple):
        in_specs += _pair_specs(D_MODEL)
        args += list(y)
    else:
        in_specs.append(pl.BlockSpec((TM, D_MODEL), lambda i: (i, 0)))
        args.append(y)
    in_specs.append(pl.BlockSpec((None, 1, D_MODEL), lambda i: (_mod_row(i), 0, 0)))
    args.append(gate)
    return in_specs, args


SLAB = D_MODEL // LANES


def _load_slabs(ref, rows):
    return jnp.concatenate([ref[pl.ds(c, rows, stride=SLAB), :] for c in range(SLAB)], axis=1)


def _store_slabs(ref, x):
    for c in range(SLAB):
        ref[pl.ds(c, x.shape[0], stride=SLAB), :] = x[:, c * LANES:(c + 1) * LANES]


def _slab(ref, idx):
    return ref.at[pl.ds(pl.multiple_of(idx * SLAB, SLAB), SLAB)]


def _merge_router_kernel(*refs, even, y_pair):
    n_merge = 4 + (2 if even else 0) + 1 + (2 if y_pair else 1) + 1
    merge_refs = refs[:n_merge]
    (g_ref, sc_ref, sh_ref, whi_ref, wlo_ref, b_ref,
     y_out_ref, h_ref, ti_ref, tp_ref, rk_ref, cnt_ref, base_ref) = refs[n_merge:]

    @pl.when(pl.program_id(0) == 0)
    def _():
        base_ref[...] = jnp.zeros(base_ref.shape, F32)

    y = _merge_value(merge_refs, even, y_pair)
    y_out_ref[...] = y
    h = _norm_mod(y, g_ref[...], sc_ref[...], sh_ref[...])
    _store_slabs(h_ref, h)
    h_hi = h.astype(BF16)
    h_lo = (h - h_hi.astype(F32)).astype(BF16)
    logits = (_dot(h_hi, whi_ref[...]) + (_dot(h_hi, wlo_ref[...]) + _dot(h_lo, whi_ref[...]))
              + b_ref[...])
    lane = lax.broadcasted_iota(jnp.int32, logits.shape, 1)
    lane_f = lane.astype(F32)
    vals, idxs = [], []
    for _ in range(TOP_K):
        mx = jnp.max(logits, axis=-1, keepdims=True)
        ix = jnp.min(jnp.where(logits == mx, lane_f, float(LANES)), axis=-1, keepdims=True)
        vals.append(mx)
        idxs.append(ix)
        logits = jnp.where(lane_f == ix, -jnp.inf, logits)
    es = [jnp.exp(v - vals[0]) for v in vals]
    tot = es[0] + es[1] + es[2] + es[3]
    ti = jnp.zeros(logits.shape, F32)
    tp = jnp.zeros(logits.shape, F32)
    for k in range(TOP_K):
        ti = jnp.where(lane == k, idxs[k], ti)
        tp = jnp.where(lane == k, es[k] / tot, tp)
    ti_ref[...] = ti.T[0:8, :].astype(jnp.int32)
    tp_ref[...] = tp
    onehots = [(lane_f == ix).astype(F32) for ix in idxs]
    cnt = onehots[0] + onehots[1] + onehots[2] + onehots[3]
    row = lax.broadcasted_iota(jnp.int32, (TM, TM), 0)
    col = lax.broadcasted_iota(jnp.int32, (TM, TM), 1)
    before = _dot((col < row).astype(BF16), cnt.astype(BF16)) + base_ref[...]
    rk = jnp.zeros(logits.shape, F32)
    for k in range(TOP_K):
        rk = jnp.where(lane == k, jnp.sum(onehots[k] * before, axis=-1, keepdims=True), rk)
    rk_ref[...] = rk.T[0:8, :].astype(jnp.int32)
    base_ref[...] = base_ref[...] + jnp.sum(cnt, axis=0, keepdims=True)
    cnt_ref[...] = base_ref[...]


def _merge_router(y, gate, w_out_bf16, a, b, gain, scale, shift, rw, rb, *, p=None, norm_gain=None):
    merge_in_specs, merge_args = _merge_specs(y, gate, w_out_bf16, a, b, p, norm_gain)
    vec = pl.BlockSpec((None, 1, D_MODEL), lambda i: (_mod_row(i), 0, 0))
    rw_p = jnp.zeros((D_MODEL, LANES), F32).at[:, :N_EXPERTS].set(rw)
    rb_p = jnp.full((1, LANES), NEG_BIG, F32).at[0, :N_EXPERTS].set(rb)
    rw_hi = rw_p.astype(BF16)
    tile = lambda w: pl.BlockSpec((TM, w), lambda i: (i, 0))
    by_choice = pl.BlockSpec((None, 8, TM), lambda i: (i, 0, 0))
    return pl.pallas_call(
        functools.partial(_merge_router_kernel, even=p is not None, y_pair=isinstance(y, tuple)),
        out_shape=[jax.ShapeDtypeStruct((T_ALL, D_MODEL), F32),
                   jax.ShapeDtypeStruct((T_ALL * SLAB, LANES), F32),
                   jax.ShapeDtypeStruct((N_TILES, 8, TM), jnp.int32),
                   jax.ShapeDtypeStruct((T_ALL, LANES), F32),
                   jax.ShapeDtypeStruct((N_TILES, 8, TM), jnp.int32),
                   jax.ShapeDtypeStruct((1, LANES), F32)],
        grid=(N_TILES,),
        in_specs=merge_in_specs + [
            pl.BlockSpec((1, D_MODEL), lambda i: (0, 0)), vec, vec,
            pl.BlockSpec((D_MODEL, LANES), lambda i: (0, 0)), pl.BlockSpec((D_MODEL, LANES), lambda i: (0, 0)),
            pl.BlockSpec((1, LANES), lambda i: (0, 0))],
        out_specs=[tile(D_MODEL), pl.BlockSpec((TM * SLAB, LANES), lambda i: (i, 0)), by_choice, tile(LANES),
                   by_choice, pl.BlockSpec((1, LANES), lambda i: (0, 0))],
        scratch_shapes=[pltpu.VMEM((1, LANES), F32)],
        compiler_params=_params(("arbitrary",)),
        name="merge_router",
    )(*merge_args, gain.reshape(1, D_MODEL), scale, shift, rw_hi, (rw_p - rw_hi.astype(F32)).astype(BF16), rb_p)


def _route_plan(top_i, rank, counts):
    experts = jnp.arange(N_EXPERTS, dtype=jnp.int32)
    padded = ((counts + MOE_TM - 1) // MOE_TM) * MOE_TM
    seg_end = jnp.cumsum(padded)
    seg_start = seg_end - padded
    pos = rank
    for e in range(N_EXPERTS - 1):
        pos = pos + jnp.where(top_i > e, padded[e], 0)
    n_active = seg_end[-1] // MOE_TM
    fill = jnp.concatenate([seg_start + counts, padded - counts, n_active[None]]).astype(jnp.int32)
    tile_start = jnp.arange(MOE_TILES, dtype=jnp.int32) * MOE_TM
    tile_expert = jnp.sum((seg_end[None, :] <= tile_start[:, None]).astype(jnp.int32), axis=1)
    last = jnp.sum((seg_end <= (n_active - 1) * MOE_TM).astype(jnp.int32))
    tile_expert = jnp.minimum(jnp.where(tile_start < seg_end[-1], tile_expert, last), N_EXPERTS - 1)
    owns = (padded > 0).astype(jnp.int32)
    run_of_expert = jnp.cumsum(owns) - 1
    run_expert = jnp.sum(jnp.where((run_of_expert[None, :] == experts[:, None]) & (owns[None, :] > 0),
                                   experts[None, :], 0), axis=1)
    runs = jnp.concatenate([run_expert, jnp.sum(owns)[None]]).astype(jnp.int32)
    tile_run = jnp.sum(jnp.where(tile_expert[:, None] == experts[None, :], run_of_expert[None, :], 0), axis=1)
    experts_plan = (tile_expert.astype(jnp.int32), n_active.reshape(1).astype(jnp.int32),
                    tile_run.astype(jnp.int32), runs)
    return pos.astype(jnp.int32), fill, experts_plan


DMA_UNROLL = 4
DMA_QUEUES = 2


def _wait_slabs(ref, n_slabs, sem):
    view = ref.at[pl.ds(0, n_slabs * SLAB)]
    pltpu.make_async_copy(view, view, sem).wait()


def _dispatch_kernel(pos_ref, fill_ref, h_ref, xs_ref, inv_ref, sem):
    i = pl.program_id(0)
    base = i * (TM * TOP_K)

    def issue(j, carry):
        for u in range(DMA_UNROLL):
            t = j * DMA_UNROLL + u
            for k in range(TOP_K):
                a = base + k * TM + t
                slot = pos_ref[a]
                inv_ref[slot] = a
                pltpu.make_async_copy(_slab(h_ref, t), _slab(xs_ref, slot), sem).start(priority=k % DMA_QUEUES)
        return carry

    lax.fori_loop(0, TM // DMA_UNROLL, issue, 0)

    @pl.when(i == 0)
    def _():
        def per_expert(e, total):
            start = fill_ref[e]
            n = fill_ref[N_EXPERTS + e]

            def one(r, carry):
                slot = start + r
                inv_ref[slot] = N_ASSIGN + slot % MOE_TM
                pltpu.make_async_copy(_slab(h_ref, 0), _slab(xs_ref, slot), sem).start()
                return carry

            lax.fori_loop(0, n, one, 0)
            return total + n

        total = lax.fori_loop(0, N_EXPERTS, per_expert, 0)

        n_active = fill_ref[2 * N_EXPERTS]

        def unused_tile(ti, carry):
            dst = xs_ref.at[pl.ds(pl.multiple_of(ti * (MOE_TM * SLAB), MOE_TM * SLAB), MOE_TM * SLAB)]
            pltpu.make_async_copy(h_ref, dst, sem).start()
            return carry

        lax.fori_loop(n_active, MOE_TILES, unused_tile, 0)

        def unused_slot(slot, carry):
            inv_ref[slot] = N_ASSIGN
            return carry

        lax.fori_loop(n_active * MOE_TM, MOE_ROWS, unused_slot, 0)
        total = total + (MOE_TILES - n_active) * MOE_TM

        @pl.when(total > 0)
        def _():
            _wait_slabs(xs_ref, total, sem)

    _wait_slabs(xs_ref, TM * TOP_K, sem)


def _dispatch(h_slabs, pos, fill):
    grid_spec = pltpu.PrefetchScalarGridSpec(
        num_scalar_prefetch=2,
        grid=(N_TILES,),
        in_specs=[pl.BlockSpec((TM * SLAB, LANES), lambda i, pos, fill: (i, 0))],
        out_specs=[pl.BlockSpec(memory_space=pl.ANY), pl.BlockSpec(memory_space=pltpu.SMEM)],
        scratch_shapes=[pltpu.SemaphoreType.DMA],
    )
    return pl.pallas_call(
        _dispatch_kernel,
        out_shape=[jax.ShapeDtypeStruct((MOE_ROWS * SLAB, LANES), F32),
                   jax.ShapeDtypeStruct((MOE_ROWS,), jnp.int32)],
        grid_spec=grid_spec,
        compiler_params=_params(("arbitrary",)),
        name="moe_dispatch",
    )(pos, fill, h_slabs)


MOE_HALF = MOE_TM // 2


def _moe_kernel(te_ref, na_ref, ts_ref, ex_ref, inv_ref, x_ref, wgu_hbm, bgu_ref, wd_hbm, bd_ref, out_hbm,
                wgu_f32, wd_f32, wgu_bf, wd_bf, obuf0, obuf1, sems, osems, *, layer):
    i = pl.program_id(0)
    s = ts_ref[i]
    first = (i == 0) | (s != ts_ref[jnp.maximum(i - 1, 0)])
    n_active = na_ref[0]

    def start_rows(buf, half, tile):
        for r in range(MOE_HALF):
            dst = _slab(out_hbm, inv_ref[tile * MOE_TM + half * MOE_HALF + r])
            pltpu.make_async_copy(buf.at[pl.ds(r * SLAB, SLAB)], dst, osems.at[half]).start(
                priority=r % DMA_QUEUES)

    def wait_rows(half):
        _wait_slabs(out_hbm, MOE_HALF, osems.at[half])

    def ffn_tile(deferred):
        if deferred:
            start_rows(obuf1, 1, i - 1)
            wait_rows(0)
        x = _load_slabs(x_ref, MOE_TM).astype(BF16)
        gu = _dot(x, wgu_bf[...]) + bgu_ref[...]
        gate = jnp.minimum(gu[:, :D_FF], SWIGLU_LIMIT)
        up = jnp.clip(gu[:, D_FF:], -SWIGLU_LIMIT, SWIGLU_LIMIT)
        act = ((up + 1.0) * gate * jax.nn.sigmoid(SWIGLU_ALPHA * gate)).astype(BF16)
        o0 = _dot(act[:MOE_HALF], wd_bf[...]) + bd_ref[...]
        _store_slabs(obuf0, o0)
        start_rows(obuf0, 0, i)
        o1 = _dot(act[MOE_HALF:], wd_bf[...]) + bd_ref[...]
        if deferred:
            wait_rows(1)
        _store_slabs(obuf1, o1)

    def weight_copies(slot):
        e = ex_ref[slot]
        b = slot % 2
        return (pltpu.make_async_copy(wgu_hbm.at[layer, e], wgu_f32.at[b], sems.at[0, b]),
                pltpu.make_async_copy(wd_hbm.at[layer, e], wd_f32.at[b], sems.at[1, b]))

    @pl.when(i == 0)
    def _():
        for cp in weight_copies(0):
            cp.start()

    @pl.when(first)
    def _():
        for cp in weight_copies(s):
            cp.wait()

        @pl.when(s + 1 < ex_ref[N_EXPERTS])
        def _():
            for cp in weight_copies(s + 1):
                cp.start()

        b = s % 2
        wgu_bf[...] = wgu_f32[b].astype(BF16)
        wd_bf[...] = wd_f32[b].astype(BF16)

    @pl.when(i == 0)
    def _():
        ffn_tile(False)

    @pl.when((i > 0) & (i < n_active))
    def _():
        ffn_tile(True)

    @pl.when(i == n_active)
    def _():
        wait_rows(0)
        start_rows(obuf1, 1, i - 1)
        wait_rows(1)
        for half, buf in enumerate((obuf0, obuf1)):
            dst = out_hbm.at[pl.ds((N_ASSIGN + half * MOE_HALF) * SLAB, MOE_HALF * SLAB)]
            pltpu.make_async_copy(buf, dst, osems.at[half]).start()
        wait_rows(0)
        wait_rows(1)


def _moe_experts(layer, xs, plan, inv, w_gu, b_gu, w_down, b_down):
    const = lambda i, te, na, ts, ex, inv: (layer, te[i], 0, 0)
    grid_spec = pltpu.PrefetchScalarGridSpec(
        num_scalar_prefetch=5,
        grid=(MOE_TILES,),
        in_specs=[
            pl.BlockSpec((MOE_TM * SLAB, LANES), lambda i, te, na, ts, ex, inv: (jnp.minimum(i, na[0] - 1), 0)),
            pl.BlockSpec(memory_space=pl.ANY),
            pl.BlockSpec((None, None, 1, 2 * D_FF), const),
            pl.BlockSpec(memory_space=pl.ANY),
            pl.BlockSpec((None, None, 1, D_MODEL), const),
        ],
        out_specs=pl.BlockSpec(memory_space=pl.ANY),
        scratch_shapes=[pltpu.VMEM((2, D_MODEL, 2 * D_FF), F32), pltpu.VMEM((2, D_FF, D_MODEL), F32),
                        pltpu.VMEM((D_MODEL, 2 * D_FF), BF16), pltpu.VMEM((D_FF, D_MODEL), BF16),
                        pltpu.VMEM((MOE_HALF * SLAB, LANES), F32), pltpu.VMEM((MOE_HALF * SLAB, LANES), F32),
                        pltpu.SemaphoreType.DMA((2, 2)), pltpu.SemaphoreType.DMA((2,))],
    )
    return pl.pallas_call(
        functools.partial(_moe_kernel, layer=layer),
        out_shape=jax.ShapeDtypeStruct(((N_ASSIGN + MOE_TM) * SLAB, LANES), F32),
        grid_spec=grid_spec,
        compiler_params=_params(("arbitrary",)),
        name="moe_experts",
    )(*plan, inv, xs, w_gu, b_gu.reshape(DEPTH, N_EXPERTS, 1, 2 * D_FF), w_down,
      b_down.reshape(DEPTH, N_EXPERTS, 1, D_MODEL))


def _final_combine_kernel(out_ref, y_ref, tp_ref, g_ref, fg_ref, n_ctx_ref, n_lat_ref):
    chunks = _moe_combine_chunks(out_ref, y_ref, tp_ref, g_ref)
    ss = jnp.zeros((TM, 1), F32)
    for yc in chunks:
        ss = ss + jnp.sum(yc * yc, axis=-1, keepdims=True)
    inv = lax.rsqrt(ss * (1.0 / D_MODEL) + EPS)

    def store(n_ref):
        for c in range(SLAB):
            cs = slice(c * LANES, (c + 1) * LANES)
            n_ref[:, cs] = chunks[c] * inv * fg_ref[:, cs]

    @pl.when(_is_ctx_tile())
    def _():
        store(n_ctx_ref)

    @pl.when(jnp.logical_not(_is_ctx_tile()))
    def _():
        store(n_lat_ref)


def _final_combine(y, out_slabs, top_p, gate, final_gain):
    tile = pl.BlockSpec((TM, D_MODEL), lambda i: (i, 0))
    return pl.pallas_call(
        _final_combine_kernel,
        out_shape=[jax.ShapeDtypeStruct((T_CTX, D_MODEL), F32), jax.ShapeDtypeStruct((T_LAT, D_MODEL), F32)],
        grid=(N_TILES,),
        in_specs=[pl.BlockSpec((TOP_K * TM * SLAB, LANES), lambda i: (i, 0)), tile,
                  pl.BlockSpec((TM, LANES), lambda i: (i, 0)),
                  pl.BlockSpec((None, 1, D_MODEL), lambda i: (_mod_row(i), 0, 0)),
                  pl.BlockSpec((1, D_MODEL), lambda i: (0, 0))],
        out_specs=[pl.BlockSpec((TM, D_MODEL), lambda i: (jnp.minimum(i, CTX_TILES - 1), 0)),
                   pl.BlockSpec((TM, D_MODEL), lambda i: (jnp.maximum(i - CTX_TILES, 0), 0))],
        compiler_params=_params(("arbitrary",)),
        name="moe_combine",
    )(out_slabs, y, top_p, gate, final_gain.reshape(1, D_MODEL))


def _moe_experts_layer(layer, routed, w_gu, b_gu, w_down, b_down):
    h_slabs, top_i, top_p, rank, counts = routed
    pos, fill, experts_plan = _route_plan(top_i[:, :TOP_K].reshape(-1), rank[:, :TOP_K].reshape(-1),
                                          counts[0, :N_EXPERTS].astype(jnp.int32))
    xs, inv = _dispatch(h_slabs, pos, fill)
    return _moe_experts(layer, xs, experts_plan, inv, w_gu, b_gu, w_down, b_down), top_p


def kernel(x_prompt, x_sample, c, cache_a_k, cache_a_v, state_b_C, state_b_n, state_b_m, cache_c_k, cache_c_v, cache_d_k, cache_d_v, c_ctx, w_mod, b_mod, norm1_g, norm2_g, w_in_even, w_out_even, a_q_gain, a_k_gain, b_gate_bias, b_norm_gain, w_in_odd, w_out_odd, c_rpb, d_lambda, d_norm_gain, router_w, router_b, expert_w_gu, expert_b_gu, expert_w_down, expert_b_down, final_norm_g):
    y = (x_prompt.reshape(T_CTX, D_MODEL), x_sample.reshape(T_LAT, D_MODEL))
    cond = jnp.zeros((MOD_ROWS, D_MODEL), F32).at[0].set(c_ctx).at[1:1 + DEC_BATCH].set(c)
    mod = _modulation(cond, w_mod, b_mod).reshape(DEPTH, MOD_ROWS, 6, 1, D_MODEL)
    rope_cos, rope_sin = _rope_tables()
    scale = HD ** -0.5
    outs = {}

    for layer in range(DEPTH):
        sh1, sc1, g1, sh2, sc2, g2 = (mod[layer, :, k] for k in range(6))
        j = layer // 2
        if layer % 2 == 0:
            w = w_in_even[j]
            sizes = np.cumsum([0, 512, 128, 128, 512, 512, 512, 512, 16])
            aq, ak, av, bq, bk, bv, bo, bg = (w[:, sizes[k]:sizes[k + 1]] for k in range(8))
            w_in = jnp.concatenate([aq, bo, bq, bk, bv, ak, av, bg, jnp.zeros((D_MODEL, EV_N - EV_BG - 16), F32)],
                                   axis=1).astype(BF16)
            qg = jnp.tile(a_q_gain[j], LANES // HD).reshape(1, LANES)
            kg = jnp.tile(a_k_gain[j], LANES // HD).reshape(1, LANES)
            specs = ((EV_AQ, 512, 0, True, scale, BF16, None), (EV_AK, 128, 1, True, 1.0, BF16, None),
                     (EV_AV, 128, None, False, 1.0, BF16, None),
                     (EV_AK, 128, 1, False, 1.0, F32, A_KV), (EV_AV, 128, None, False, 1.0, F32, A_KV))
            y, p, (qa, ka, va, new_ak, new_av) = _project(y, norm1_g[layer], sc1, sh1, w_in, rope_cos, rope_sin,
                                                          [qg, kg], specs)
            oa_ctx = _attention(qa, ka, va, ctx=True, group=A_HEADS // A_KV, n_kv=A_KV)
            cache = (cache_a_k[:, j].astype(BF16), cache_a_v[:, j].astype(BF16))
            oa_lat = _attention(qa, ka, va, ctx=False, group=A_HEADS // A_KV, n_kv=A_KV, cache=cache)
            zc = jnp.zeros((BATCH, 2, B_HEADS, B_DK, B_DV), F32)
            zn = jnp.zeros((BATCH, 2, B_HEADS, 1, B_DK), F32)
            zm = jnp.zeros((BATCH, 2, B_HEADS, 1, LANES), F32)
            hb_ctx, bC, bn, bm = _mlstm(p, b_gate_bias[j], zc, zn, zm, ctx=True)
            m0 = jnp.broadcast_to(state_b_m[:, j][..., None, None], (DEC_BATCH, 2, B_HEADS, 1, LANES))
            hb_lat, _, _, _ = _mlstm(p, b_gate_bias[j], state_b_C[:, j], state_b_n[:, j][:, :, :, None, :], m0,
                                     ctx=False)
            y, *routed = _merge_router(y, g1, w_out_even[j].astype(BF16), (oa_ctx, oa_lat), (hb_ctx, hb_lat),
                                       norm2_g[layer], sc2, sh2, router_w[layer], router_b[layer],
                                       p=p, norm_gain=b_norm_gain[j])
            outs.setdefault("a_k", []).append(new_ak)
            outs.setdefault("a_v", []).append(new_av)
            outs.setdefault("b_C", []).append(bC)
            outs.setdefault("b_n", []).append(bn[:, :, :, 0, :])
            outs.setdefault("b_m", []).append(bm[:, :, :, 0, 0])
        else:
            specs = ((0, 512, None, False, scale, BF16, None), (512, 512, None, False, 1.0, BF16, None),
                     (1024, 512, None, False, 1.0, BF16, None), (1536, 512, None, True, scale, BF16, None),
                     (2048, 512, None, True, 1.0, BF16, None), (2560, 512, None, False, 1.0, BF16, None),
                     (512, 512, None, False, 1.0, F32, C_HEADS), (1024, 512, None, False, 1.0, F32, C_HEADS),
                     (2048, 512, None, False, 1.0, F32, 2 * D_HEADS), (2560, 512, None, False, 1.0, F32, D_HEADS))
            y, p, (qc, kc, vc, qd, kd, vd, new_ck, new_cv, new_dk, new_dv) = _project(
                y, norm1_g[layer], sc1, sh1, w_in_odd[j].astype(BF16), rope_cos, rope_sin, [], specs)
            lam_init = 0.8 - 0.6 * math.exp(-0.3 * layer)
            lp = d_lambda[j].astype(F32)
            lam = jnp.exp(jnp.sum(lp[0] * lp[1])) - jnp.exp(jnp.sum(lp[2] * lp[3])) + lam_init
            lam_vec = jnp.stack([lam, jnp.asarray(1.0 - lam_init, F32)]).astype(F32)
            diff = (lam_vec, d_norm_gain[j])
            oc_ctx = _attention(qc, kc, vc, ctx=True, n_kv=C_HEADS)
            od_ctx = _attention(qd, kd, vd, ctx=True, diff=diff)
            bias = _neighbourhood_bias(c_rpb[j])
            oc_lat = _attention(qc, kc, vc, ctx=False, n_kv=C_HEADS, bias=bias,
                                cache=(cache_c_k[:, j].astype(BF16), cache_c_v[:, j].astype(BF16)))
            kd_cache = cache_d_k[:, j].reshape(DEC_BATCH, 2 * D_HEADS, PAST_LEN, HD).astype(BF16)
            od_lat = _attention(qd, kd, vd, ctx=False, diff=diff, cache=(kd_cache, cache_d_v[:, j].astype(BF16)))
            y, *routed = _merge_router(y, g1, w_out_odd[j].astype(BF16), (oc_ctx, oc_lat), (od_ctx, od_lat),
                                       norm2_g[layer], sc2, sh2, router_w[layer], router_b[layer])
            outs.setdefault("c_k", []).append(new_ck)
            outs.setdefault("c_v", []).append(new_cv)
            outs.setdefault("d_k", []).append(new_dk.reshape(BATCH, D_HEADS, 2, SEQ, HD))
            outs.setdefault("d_v", []).append(new_dv)
        out_slabs, top_p = _moe_experts_layer(layer, routed, expert_w_gu, expert_b_gu, expert_w_down,
                                              expert_b_down)
        y = (y, out_slabs, top_p, g2)
    y_prompt, y_sample = _final_combine(*y, final_norm_g)
    stack = lambda k: jnp.stack(outs[k], axis=1)
    return (y_prompt.reshape(BATCH, SEQ, D_MODEL), y_sample.reshape(DEC_BATCH, DEC_SEQ, D_MODEL),
            stack("a_k"), stack("a_v"), stack("b_C"), stack("b_n"), stack("b_m"),
            stack("c_k"), stack("c_v"), stack("d_k"), stack("d_v"))
```

```python
import functools
import math

import numpy as np
import jax
import jax.numpy as jnp
from jax import lax
from jax.experimental import pallas as pl
from jax.experimental.pallas import tpu as pltpu

D_MODEL = 1024
BATCH = 32
SEQ = 256
DEPTH = 2
DEC_BATCH = 8
DEC_SEQ = 1024
PAST_LEN = 512
GRID_W = 64
HD = 64
A_HEADS = 8
A_KV = 2
B_HEADS = 4
B_DK = 128
B_DV = 128
B_CHUNK = 128
C_HEADS = 8
NA_ROWS = 8
NA_COLS = 16
D_HEADS = 4
D_VDIM = 2 * HD
N_EXPERTS = 32
TOP_K = 4
D_FF = 1024
SWIGLU_LIMIT = 7.0
SWIGLU_ALPHA = 1.702
ROPE_THETA = 10000.0
EPS = 1e-6

F32 = jnp.float32
BF16 = jnp.bfloat16
HIGHEST = lax.Precision.HIGHEST

T_CTX = BATCH * SEQ
T_LAT = DEC_BATCH * DEC_SEQ
T_ALL = T_CTX + T_LAT
TM = 256
CTX_TILES = T_CTX // TM
LAT_TILES_PER_BATCH = DEC_SEQ // TM
N_TILES = T_ALL // TM
MOD_ROWS = 16
LANES = 128
NEG_BIG = -1e30
MOE_TM = 256
N_ASSIGN = T_ALL * TOP_K
MOE_ROWS = N_ASSIGN + N_EXPERTS * MOE_TM
MOE_TILES = MOE_ROWS // MOE_TM
VMEM_LIMIT = 56 * 1024 * 1024
assert TM == SEQ and DEC_SEQ % TM == 0 and MOE_TM == TM

EV_AQ, EV_BO, EV_BQ, EV_BK, EV_BV, EV_AK, EV_AV, EV_BG = 0, 512, 1024, 1536, 2048, 2560, 2688, 2816
EV_N = 2944


def _params(sem, vmem=VMEM_LIMIT):
    return pltpu.CompilerParams(dimension_semantics=sem, vmem_limit_bytes=vmem)


def _mod_row(i):
    return jnp.where(i < CTX_TILES, 0, 1 + (i - CTX_TILES) // LAT_TILES_PER_BATCH)


def _rope_block(i):
    return jnp.where(i < CTX_TILES, LAT_TILES_PER_BATCH, (i - CTX_TILES) % LAT_TILES_PER_BATCH)


def _dot(a, b, precision=None):
    return jnp.dot(a, b, preferred_element_type=F32, precision=precision)


def _dot_nt(a, b):
    return lax.dot_general(a, b, (((1,), (1,)), ((), ())), preferred_element_type=F32)


def _dot_tn(a, b):
    return lax.dot_general(a, b, (((0,), (0,)), ((), ())), preferred_element_type=F32)


def _modulation_kernel(c_ref, w_ref, b_ref, o_ref):
    c = c_ref[...]
    s = c * jax.nn.sigmoid(c)
    o_ref[...] = _dot(s, w_ref[...], HIGHEST) + b_ref[...]


def _modulation(cond, w_mod, b_mod):
    tn = 1536
    return pl.pallas_call(
        _modulation_kernel,
        out_shape=jax.ShapeDtypeStruct((DEPTH, MOD_ROWS, 6 * D_MODEL), F32),
        grid=(DEPTH, 6 * D_MODEL // tn),
        in_specs=[
            pl.BlockSpec((MOD_ROWS, D_MODEL), lambda l, j: (0, 0)),
            pl.BlockSpec((None, D_MODEL, tn), lambda l, j: (l, 0, j)),
            pl.BlockSpec((None, 1, tn), lambda l, j: (l, 0, j)),
        ],
        out_specs=pl.BlockSpec((None, MOD_ROWS, tn), lambda l, j: (l, 0, j)),
        compiler_params=_params(("parallel", "parallel")),
        name="modulation",
    )(cond, w_mod, b_mod.reshape(DEPTH, 1, 6 * D_MODEL))


def _norm_mod(y, g, sc, sh):
    ms = jnp.mean(y * y, axis=-1, keepdims=True)
    return (y * lax.rsqrt(ms + EPS) * g) * (1.0 + sc) + sh


def _rope_rotate(x):
    w = x.shape[-1]
    lane = lax.broadcasted_iota(jnp.int32, x.shape, 1)
    nxt = pltpu.roll(x, w - 1, 1)
    prv = pltpu.roll(x, 1, 1)
    return jnp.where((lane & 1) == 0, -nxt, prv)


def _is_ctx_tile():
    return pl.program_id(0) < CTX_TILES


def _pair_specs(width):
    return [pl.BlockSpec((TM, width), lambda i: (jnp.minimum(i, CTX_TILES - 1), 0)),
            pl.BlockSpec((TM, width), lambda i: (jnp.maximum(i - CTX_TILES, 0), 0))]


def _pair_value(ctx_ref, lat_ref, cols=slice(None)):
    return jnp.where(_is_ctx_tile(), ctx_ref[:, cols], lat_ref[:, cols])


def _moe_combine_chunks(out_ref, y_ref, tp_ref, g_ref):
    tp = tp_ref[...]
    chunks = []
    for c in range(SLAB):
        cs = slice(c * LANES, (c + 1) * LANES)
        acc = tp[:, 0:1] * out_ref[pl.ds(c, TM, stride=SLAB), :]
        for k in range(1, TOP_K):
            acc = acc + tp[:, k:k + 1] * out_ref[pl.ds(k * TM * SLAB + c, TM, stride=SLAB), :]
        chunks.append(y_ref[:, cs] + g_ref[:, cs] * acc)
    return chunks


def _norm_proj_kernel(*refs, specs, n_gain, y_mode):
    n_y = {"single": 1, "pair": 2, "combine": 4}[y_mode]
    g_ref, sc_ref, sh_ref, w_ref, cos_ref, sin_ref, bd_ref = refs[n_y:n_y + 7]
    gain_refs = refs[n_y + 7:n_y + 7 + n_gain]
    p_ref = refs[n_y + 7 + n_gain]
    out_refs = refs[n_y + 8 + n_gain:]
    if y_mode == "combine":
        y = jnp.concatenate(_moe_combine_chunks(*refs[:4]), axis=1)
        out_refs[0][...] = y
        out_refs = out_refs[1:]
    elif y_mode == "pair":
        y = _pair_value(refs[0], refs[1])
    else:
        y = refs[0][...]
    h = _norm_mod(y, g_ref[...], sc_ref[...], sh_ref[...])
    p_ref[...] = _dot(h.astype(BF16), w_ref[...])
    cos = cos_ref[...]
    sin = sin_ref[...]
    for (col, width, gi, rope, scale, _, heads), o_ref in zip(specs, out_refs):
        for c0 in range(0, width, LANES):
            x = p_ref[:, col + c0:col + c0 + LANES]
            if gi is not None:
                xx = x * x
                xx_hi = xx.astype(BF16)
                xx_lo = (xx - xx_hi.astype(F32)).astype(BF16)
                ss = _dot(xx_hi, bd_ref[...]) + _dot(xx_lo, bd_ref[...])
                x = x * lax.rsqrt(ss * (1.0 / HD) + EPS) * gain_refs[gi][...]
            if rope:
                x = x * cos + _rope_rotate(x) * sin
            if scale != 1.0:
                x = x * scale
            if heads is None:
                o_ref[:, c0:c0 + LANES] = x.astype(o_ref.dtype)
            else:
                hw = width // heads
                per = LANES // hw

                @pl.when(_is_ctx_tile())
                def _(x=x, o_ref=o_ref, c0=c0, hw=hw, per=per):
                    for u in range(per):
                        o_ref[(c0 // LANES) * per + u] = x[:, u * hw:(u + 1) * hw].astype(o_ref.dtype)


def _norm_proj(y, gain, scale, shift, w_bf16, rope_cos, rope_sin, gains, specs):
    n = w_bf16.shape[1]
    tile = pl.BlockSpec((TM, D_MODEL), lambda i: (i, 0))
    vec = pl.BlockSpec((None, 1, D_MODEL), lambda i: (_mod_row(i), 0, 0))
    if not isinstance(y, tuple):
        y_mode, ys, in_specs = "single", [y], [tile]
    elif len(y) == 2:
        y_mode, ys, in_specs = "pair", list(y), _pair_specs(D_MODEL)
    else:
        y_prev, out_slabs, top_p, gate = y
        y_mode, ys = "combine", [out_slabs, y_prev, top_p, gate]
        in_specs = [pl.BlockSpec((TOP_K * TM * SLAB, LANES), lambda i: (i, 0)), tile,
                    pl.BlockSpec((TM, LANES), lambda i: (i, 0)), vec]
    bd = jnp.asarray(np.kron(np.eye(LANES // HD), np.ones((HD, HD))), BF16)
    rope_spec = pl.BlockSpec((TM, LANES), lambda i: (_rope_block(i), 0))
    in_specs += [pl.BlockSpec((1, D_MODEL), lambda i: (0, 0)),
                 vec, vec, pl.BlockSpec((D_MODEL, n), lambda i: (0, 0)),
                 rope_spec, rope_spec, pl.BlockSpec((LANES, LANES), lambda i: (0, 0))]
    in_specs += [pl.BlockSpec((1, LANES), lambda i: (0, 0)) for _ in gains]
    out_shape = [jax.ShapeDtypeStruct((T_ALL, n), F32)]
    out_specs = [pl.BlockSpec((TM, n), lambda i: (i, 0))]
    if y_mode == "combine":
        out_shape.append(jax.ShapeDtypeStruct((T_ALL, D_MODEL), F32))
        out_specs.append(tile)
    for (_, width, _, _, _, dtype, heads) in specs:
        if heads is None:
            out_shape.append(jax.ShapeDtypeStruct((T_ALL, width), dtype))
            out_specs.append(pl.BlockSpec((TM, width), lambda i: (i, 0)))
        else:
            out_shape.append(jax.ShapeDtypeStruct((BATCH, heads, SEQ, width // heads), dtype))
            out_specs.append(pl.BlockSpec((None, heads, SEQ, width // heads),
                                          lambda i: (jnp.minimum(i, CTX_TILES - 1), 0, 0, 0)))
    return pl.pallas_call(
        functools.partial(_norm_proj_kernel, specs=specs, n_gain=len(gains), y_mode=y_mode),
        out_shape=out_shape,
        grid=(N_TILES,),
        in_specs=in_specs,
        out_specs=out_specs,
        compiler_params=_params(("arbitrary",)),
        name="norm_proj",
    )(*ys, gain.reshape(1, D_MODEL), scale, shift, w_bf16, rope_cos, rope_sin, bd, *gains)


def _project(y, *args):
    res = _norm_proj(y, *args)
    if isinstance(y, tuple) and len(y) == 4:
        return res[1], res[0], res[2:]
    return y, res[0], res[1:]


def _rope_tables():
    half = HD // 2
    freqs = 1.0 / (ROPE_THETA ** (jnp.arange(0, half, 2, dtype=F32) / half))
    t = jnp.arange(DEC_SEQ)
    rows = (t // GRID_W).astype(F32)
    cols = (t % GRID_W).astype(F32)
    ang = jnp.concatenate([rows[:, None] * freqs, cols[:, None] * freqs], axis=-1)
    cos = jnp.repeat(jnp.cos(ang), 2, axis=-1)
    sin = jnp.repeat(jnp.sin(ang), 2, axis=-1)
    cos = jnp.concatenate([jnp.tile(cos, (1, LANES // HD)), jnp.ones((TM, LANES), F32)], axis=0)
    sin = jnp.concatenate([jnp.tile(sin, (1, LANES // HD)), jnp.zeros((TM, LANES), F32)], axis=0)
    return cos, sin


def _lane_slice(ref, h, width=HD):
    per = LANES // width
    blk = ref[:, (h // per) * LANES:(h // per + 1) * LANES]
    if per == 1:
        return blk
    return blk[:, (h % per) * width:(h % per + 1) * width]


def _softmax_parts(scores):
    m = None
    for s in scores:
        ms = jnp.max(s, axis=-1, keepdims=True)
        m = ms if m is None else jnp.maximum(m, ms)
    ps = [jnp.exp(s - m) for s in scores]
    l = None
    for p in ps:
        ls = jnp.sum(p, axis=-1, keepdims=True)
        l = ls if l is None else l + ls
    return ps, l


def _attn_std_kernel(*refs, group, n_kv, has_cache, has_bias, bq):
    it = iter(refs)
    q_ref, kn_ref, vn_ref = next(it), next(it), next(it)
    kc_ref = vc_ref = b_ref = None
    if has_cache:
        kc_ref, vc_ref = next(it), next(it)
    if has_bias:
        b_ref = next(it)
    o_ref = next(it)
    outs = []
    for g in range(n_kv):
        qs = jnp.concatenate([_lane_slice(q_ref, g * group + j) for j in range(group)], axis=0)
        kn = _lane_slice(kn_ref, g)
        vn = _lane_slice(vn_ref, g)
        s_new = _dot_nt(qs, kn)
        if has_bias:
            s_new = s_new + b_ref[g]
        scores = [s_new]
        if has_cache:
            scores.append(_dot_nt(qs, kc_ref[g]))
        ps, l = _softmax_parts(scores)
        o = _dot(ps[0].astype(BF16), vn)
        if has_cache:
            o = o + _dot(ps[1].astype(BF16), vc_ref[g])
        o = o / l
        for j in range(group):
            outs.append(o[j * bq:(j + 1) * bq])
    o_ref[...] = jnp.concatenate(outs, axis=1).astype(o_ref.dtype)


def _attn_diff_kernel(*refs, has_cache):
    it = iter(refs)
    lam_ref, q_ref, kn_ref, vn_ref = next(it), next(it), next(it), next(it)
    kc_ref = vc_ref = None
    if has_cache:
        kc_ref, vc_ref = next(it), next(it)
    g_ref, o_ref = next(it), next(it)
    lam = lam_ref[0]
    post = lam_ref[1]
    outs = []
    for h in range(D_HEADS):
        pd_new, pd_c = None, None
        for j in range(2):
            f = 2 * h + j
            qs = _lane_slice(q_ref, f)
            scores = [_dot_nt(qs, _lane_slice(kn_ref, f))]
            if has_cache:
                scores.append(_dot_nt(qs, kc_ref[f]))
            ps, l = _softmax_parts(scores)
            r = 1.0 / l
            if j == 0:
                pd_new = ps[0] * r
                pd_c = ps[1] * r if has_cache else None
            else:
                r = r * lam
                pd_new = pd_new - ps[0] * r
                pd_c = pd_c - ps[1] * r if has_cache else None
        o = _dot(pd_new.astype(BF16), _lane_slice(vn_ref, h, D_VDIM))
        if has_cache:
            o = o + _dot(pd_c.astype(BF16), vc_ref[h])
        ms = jnp.mean(o * o, axis=-1, keepdims=True)
        outs.append(o * lax.rsqrt(ms + EPS) * g_ref[...] * post)
    o_ref[...] = jnp.concatenate(outs, axis=1).astype(o_ref.dtype)


def _attention(q, kn, vn, *, ctx, group=1, n_kv=1, cache=None, bias=None, diff=None, bq=256):
    if ctx:
        nb, sq, row0 = BATCH, SEQ, 0
    else:
        nb, sq, row0 = DEC_BATCH, DEC_SEQ, T_CTX
    nq = sq // bq
    qb0 = row0 // bq
    kb0 = row0 // sq
    wq, wk, wv = q.shape[1], kn.shape[1], vn.shape[1]
    in_specs = [
        pl.BlockSpec((bq, wq), lambda b, i: (qb0 + b * nq + i, 0)),
        pl.BlockSpec((sq, wk), lambda b, i: (kb0 + b, 0)),
        pl.BlockSpec((sq, wv), lambda b, i: (kb0 + b, 0)),
    ]
    args = [q, kn, vn]
    if cache is not None:
        kc, vc = cache
        in_specs += [pl.BlockSpec((None,) + kc.shape[1:], lambda b, i: (b, 0, 0, 0)),
                     pl.BlockSpec((None,) + vc.shape[1:], lambda b, i: (b, 0, 0, 0))]
        args += [kc, vc]
    if diff is None:
        if bias is not None:
            in_specs.append(pl.BlockSpec((bias.shape[0], bq, sq), lambda b, i: (0, i, 0)))
            args.append(bias)
        body = functools.partial(_attn_std_kernel, group=group, n_kv=n_kv, has_cache=cache is not None,
                                 has_bias=bias is not None, bq=bq)
    else:
        lam_vec, gain = diff
        in_specs = [pl.BlockSpec(memory_space=pltpu.SMEM)] + in_specs
        args = [lam_vec] + args
        in_specs.append(pl.BlockSpec((1, D_VDIM), lambda b, i: (0, 0)))
        args.append(gain.reshape(1, D_VDIM))
        body = functools.partial(_attn_diff_kernel, has_cache=cache is not None)
    return pl.pallas_call(
        body,
        out_shape=jax.ShapeDtypeStruct((nb * sq, 512), BF16),
        grid=(nb, nq),
        in_specs=in_specs,
        out_specs=pl.BlockSpec((bq, 512), lambda b, i: (b * nq + i, 0)),
        compiler_params=_params(("parallel", "parallel")),
        name="attention",
    )(*args)


GRID_ROWS = DEC_SEQ // GRID_W
NA_WIN_ROWS = min(NA_ROWS, GRID_ROWS)


def _na_bias_kernel(t_ref, o_ref):
    outside = jnp.full((GRID_W, GRID_W), NEG_BIG, F32)
    for qr in range(GRID_ROWS):
        r0 = min(max(qr - NA_WIN_ROWS // 2, 0), GRID_ROWS - NA_WIN_ROWS)
        parts = [t_ref[kr - qr + (NA_ROWS - 1)] if r0 <= kr < r0 + NA_WIN_ROWS else outside
                 for kr in range(GRID_ROWS)]
        o_ref[qr * GRID_W:(qr + 1) * GRID_W, :] = jnp.concatenate(parts, axis=1)


def _neighbourhood_bias(rpb):
    c = np.arange(GRID_W)
    c0 = np.clip(c - NA_COLS // 2, 0, GRID_W - NA_COLS)
    col_ok = (c[None, :] >= c0[:, None]) & (c[None, :] < c0[:, None] + NA_COLS)
    dc = np.clip(c[None, :] - c[:, None], 1 - NA_COLS, NA_COLS - 1) + (NA_COLS - 1)
    oh_c = jnp.asarray(dc[..., None] == np.arange(2 * NA_COLS - 1), F32)
    by_col = jnp.einsum("hrd,qkd->hrqk", rpb.astype(F32), oh_c, precision=HIGHEST)
    by_col = jnp.where(jnp.asarray(col_ok), by_col, NEG_BIG)
    n_dr = 2 * NA_ROWS - 1
    return pl.pallas_call(
        _na_bias_kernel,
        out_shape=jax.ShapeDtypeStruct((C_HEADS, DEC_SEQ, DEC_SEQ), F32),
        grid=(C_HEADS,),
        in_specs=[pl.BlockSpec((None, n_dr, GRID_W, GRID_W), lambda h: (h, 0, 0, 0))],
        out_specs=pl.BlockSpec((None, DEC_SEQ, DEC_SEQ), lambda h: (h, 0, 0)),
        compiler_params=_params(("parallel",)),
        name="na_bias",
    )(by_col)


def _log_sigmoid(x):
    return jnp.minimum(x, 0.0) - jnp.log1p(jnp.exp(-jnp.abs(x)))


def _mlstm_kernel(q_ref, k_ref, v_ref, g_ref, gb_ref, c0_ref, n0_ref, m0_ref, h_ref, c_ref, n_ref, m_ref, hb_ref,
                  *, seq):
    L = B_CHUNK
    nc = seq // L
    row = lax.broadcasted_iota(jnp.int32, (L, L), 0)
    col = lax.broadcasted_iota(jnp.int32, (L, L), 1)
    keeps = (col <= row, col >= row)
    k_scale = B_DK ** -0.5
    c_ref[...] = c0_ref[...]
    n_ref[...] = n0_ref[...]
    m_ref[...] = m0_ref[...]

    def step(j):
        for d in range(2):
            keep = keeps[d]
            c = j if d == 0 else nc - 1 - j
            off = c * L if isinstance(c, int) else pl.multiple_of(c * L, L)
            gates = g_ref[pl.ds(off, L), :] + gb_ref[...]
            cum = _dot(keep.astype(F32), _log_sigmoid(gates), HIGHEST)
            cum_t = cum.T
            gates_t = gates.T
            out_ref = h_ref if d == 0 else hb_ref
            for h in range(B_HEADS):
                ci = (2 * d) * B_HEADS + h
                cf = (2 * d + 1) * B_HEADS + h
                hs = slice(h * B_DK, (h + 1) * B_DK)
                C = c_ref[d, h]
                n = n_ref[d, h]
                m = m_ref[d, h][:, 0:1]
                qc = q_ref[pl.ds(off, L), hs]
                kc = k_ref[pl.ds(off, L), hs] * k_scale
                vc = v_ref[pl.ds(off, L), hs]
                b_col = cum[:, cf:cf + 1]
                i_col = gates[:, ci:ci + 1]
                b_row = cum_t[cf:cf + 1, :]
                i_row = gates_t[ci:ci + 1, :]
                dlog = jnp.where(keep, b_col - b_row + i_row, -jnp.inf)
                inter = b_col + m
                m_t = jnp.maximum(inter, jnp.max(dlog, axis=-1, keepdims=True))
                w_intra = jnp.exp(dlog - m_t)
                w_inter = jnp.exp(inter - m_t)
                qb = qc.astype(BF16)
                vb = vc.astype(BF16)
                qk = _dot_nt(qb, kc.astype(BF16)) * w_intra
                num = _dot(qk.astype(BF16), vb) + w_inter * _dot(qb, C.astype(BF16))
                den = jnp.sum(qk, axis=-1, keepdims=True) + w_inter * jnp.sum(qc * n, axis=-1, keepdims=True)
                out_ref[pl.ds(off, L), hs] = num / jnp.maximum(jnp.abs(den), jnp.exp(-m_t))
                b_last = b_col[L - 1:L, :] if d == 0 else b_col[0:1, :]
                end_col = b_last - b_col + i_col
                m_new = jnp.maximum(b_last + m, jnp.max(end_col, axis=0, keepdims=True))
                w_end = jnp.exp(end_col - m_new)
                decay = jnp.exp(b_last + m - m_new)
                kw = kc * w_end
                c_ref[d, h] = decay * C + _dot_tn(kw.astype(BF16), vb)
                n_ref[d, h] = decay * n + jnp.sum(kw, axis=0, keepdims=True)
                m_ref[d, h] = jnp.broadcast_to(m_new, (1, LANES))

    if nc <= 2:
        for j in range(nc):
            step(j)
    else:
        def body(j, carry):
            step(j)
            return carry

        lax.fori_loop(0, nc, body, 0)
    h_ref[...] = h_ref[...] + hb_ref[...]


def _mlstm(p, gate_bias, c0, n0, m0, *, ctx):
    if ctx:
        nb, seq, blk0 = BATCH, SEQ, 0
    else:
        nb, seq, blk0 = DEC_BATCH, DEC_SEQ, T_CTX // DEC_SEQ
    w = B_HEADS * B_DK

    def cols(c0_, width):
        return pl.BlockSpec((seq, width), lambda b: (blk0 + b, c0_ // width))

    gb = jnp.zeros((1, LANES), F32).at[0, :4 * B_HEADS].set(gate_bias.reshape(-1).astype(F32))
    st = lambda shape: pl.BlockSpec((None,) + shape, lambda b: (b, 0, 0, 0, 0))
    return pl.pallas_call(
        functools.partial(_mlstm_kernel, seq=seq),
        out_shape=[
            jax.ShapeDtypeStruct((nb * seq, w), F32),
            jax.ShapeDtypeStruct((nb, 2, B_HEADS, B_DK, B_DV), F32),
            jax.ShapeDtypeStruct((nb, 2, B_HEADS, 1, B_DK), F32),
            jax.ShapeDtypeStruct((nb, 2, B_HEADS, 1, LANES), F32),
        ],
        grid=(nb,),
        in_specs=[
            cols(EV_BQ, w), cols(EV_BK, w), cols(EV_BV, w), cols(EV_BG, LANES),
            pl.BlockSpec((1, LANES), lambda b: (0, 0)),
            st((2, B_HEADS, B_DK, B_DV)), st((2, B_HEADS, 1, B_DK)), st((2, B_HEADS, 1, LANES)),
        ],
        out_specs=[
            pl.BlockSpec((seq, w), lambda b: (b, 0)),
            st((2, B_HEADS, B_DK, B_DV)), st((2, B_HEADS, 1, B_DK)), st((2, B_HEADS, 1, LANES)),
        ],
        scratch_shapes=[pltpu.VMEM((seq, w), F32)],
        compiler_params=_params(("parallel",)),
        name="mlstm",
    )(p, p, p, p, gb, c0, n0, m0)


def _merge_value(refs, even, y_pair):
    a_ctx, a_lat, b_ctx, b_lat = refs[:4]
    rest = refs[4:]
    if even:
        bo_ref, ng_ref = rest[:2]
        rest = rest[2:]
        parts = [_pair_value(a_ctx, a_lat)]
        for h in range(B_HEADS):
            hs = slice(h * B_DV, (h + 1) * B_DV)
            x = _pair_value(b_ctx, b_lat, hs)
            ms = jnp.mean(x * x, axis=-1, keepdims=True)
            xn = x * lax.rsqrt(ms + EPS) * ng_ref[:, hs]
            parts.append((jax.nn.sigmoid(bo_ref[:, hs]) * xn).astype(BF16))
    else:
        parts = [_pair_value(a_ctx, a_lat), _pair_value(b_ctx, b_lat)]
    w_ref = rest[0]
    y = _pair_value(rest[1], rest[2]) if y_pair else rest[1][...]
    g_ref = rest[-1]
    cat = jnp.concatenate(parts, axis=1)
    return y + g_ref[...] * _dot(cat, w_ref[...])


def _merge_specs(y, gate, w_bf16, a, b, p, norm_gain):
    in_specs = _pair_specs(512) + _pair_specs(512)
    args = [*a, *b]
    if p is not None:
        in_specs += [pl.BlockSpec((TM, 512), lambda i: (i, EV_BO // 512)), pl.BlockSpec((1, 512), lambda i: (0, 0))]
        args += [p, norm_gain.reshape(1, 512)]
    in_specs.append(pl.BlockSpec((D_MODEL, D_MODEL), lambda i: (0, 0)))
    args.append(w_bf16)
    if isinstance(y, tuple):
        in_specs += _pair_specs(D_MODEL)
        args += list(y)
    else:
        in_specs.append(pl.BlockSpec((TM, D_MODEL), lambda i: (i, 0)))
        args.append(y)
    in_specs.append(pl.BlockSpec((None, 1, D_MODEL), lambda i: (_mod_row(i), 0, 0)))
    args.append(gate)
    return in_specs, args


SLAB = D_MODEL // LANES


def _load_slabs(ref, rows):
    return jnp.concatenate([ref[pl.ds(c, rows, stride=SLAB), :] for c in range(SLAB)], axis=1)


def _store_slabs(ref, x):
    for c in range(SLAB):
        ref[pl.ds(c, x.shape[0], stride=SLAB), :] = x[:, c * LANES:(c + 1) * LANES]


def _slab(ref, idx):
    return ref.at[pl.ds(pl.multiple_of(idx * SLAB, SLAB), SLAB)]


def _merge_router_kernel(*refs, even, y_pair):
    n_merge = 4 + (2 if even else 0) + 1 + (2 if y_pair else 1) + 1
    merge_refs = refs[:n_merge]
    (g_ref, sc_ref, sh_ref, whi_ref, wlo_ref, b_ref,
     y_out_ref, h_ref, ti_ref, tp_ref, rk_ref, cnt_ref, base_ref) = refs[n_merge:]

    @pl.when(pl.program_id(0) == 0)
    def _():
        base_ref[...] = jnp.zeros(base_ref.shape, F32)

    y = _merge_value(merge_refs, even, y_pair)
    y_out_ref[...] = y
    h = _norm_mod(y, g_ref[...], sc_ref[...], sh_ref[...])
    _store_slabs(h_ref, h)
    h_hi = h.astype(BF16)
    h_lo = (h - h_hi.astype(F32)).astype(BF16)
    logits = (_dot(h_hi, whi_ref[...]) + (_dot(h_hi, wlo_ref[...]) + _dot(h_lo, whi_ref[...]))
              + b_ref[...])
    lane = lax.broadcasted_iota(jnp.int32, logits.shape, 1)
    lane_f = lane.astype(F32)
    vals, idxs = [], []
    for _ in range(TOP_K):
        mx = jnp.max(logits, axis=-1, keepdims=True)
        ix = jnp.min(jnp.where(logits == mx, lane_f, float(LANES)), axis=-1, keepdims=True)
        vals.append(mx)
        idxs.append(ix)
        logits = jnp.where(lane_f == ix, -jnp.inf, logits)
    es = [jnp.exp(v - vals[0]) for v in vals]
    tot = es[0] + es[1] + es[2] + es[3]
    ti = jnp.zeros(logits.shape, F32)
    tp = jnp.zeros(logits.shape, F32)
    for k in range(TOP_K):
        ti = jnp.where(lane == k, idxs[k], ti)
        tp = jnp.where(lane == k, es[k] / tot, tp)
    ti_ref[...] = ti.T[0:8, :].astype(jnp.int32)
    tp_ref[...] = tp
    onehots = [(lane_f == ix).astype(F32) for ix in idxs]
    cnt = onehots[0] + onehots[1] + onehots[2] + onehots[3]
    row = lax.broadcasted_iota(jnp.int32, (TM, TM), 0)
    col = lax.broadcasted_iota(jnp.int32, (TM, TM), 1)
    before = _dot((col < row).astype(BF16), cnt.astype(BF16)) + base_ref[...]
    rk = jnp.zeros(logits.shape, F32)
    for k in range(TOP_K):
        rk = jnp.where(lane == k, jnp.sum(onehots[k] * before, axis=-1, keepdims=True), rk)
    rk_ref[...] = rk.T[0:8, :].astype(jnp.int32)
    base_ref[...] = base_ref[...] + jnp.sum(cnt, axis=0, keepdims=True)
    cnt_ref[...] = base_ref[...]


def _merge_router(y, gate, w_out_bf16, a, b, gain, scale, shift, rw, rb, *, p=None, norm_gain=None):
    merge_in_specs, merge_args = _merge_specs(y, gate, w_out_bf16, a, b, p, norm_gain)
    vec = pl.BlockSpec((None, 1, D_MODEL), lambda i: (_mod_row(i), 0, 0))
    rw_p = jnp.zeros((D_MODEL, LANES), F32).at[:, :N_EXPERTS].set(rw)
    rb_p = jnp.full((1, LANES), NEG_BIG, F32).at[0, :N_EXPERTS].set(rb)
    rw_hi = rw_p.astype(BF16)
    tile = lambda w: pl.BlockSpec((TM, w), lambda i: (i, 0))
    by_choice = pl.BlockSpec((None, 8, TM), lambda i: (i, 0, 0))
    return pl.pallas_call(
        functools.partial(_merge_router_kernel, even=p is not None, y_pair=isinstance(y, tuple)),
        out_shape=[jax.ShapeDtypeStruct((T_ALL, D_MODEL), F32),
                   jax.ShapeDtypeStruct((T_ALL * SLAB, LANES), F32),
                   jax.ShapeDtypeStruct((N_TILES, 8, TM), jnp.int32),
                   jax.ShapeDtypeStruct((T_ALL, LANES), F32),
                   jax.ShapeDtypeStruct((N_TILES, 8, TM), jnp.int32),
                   jax.ShapeDtypeStruct((1, LANES), F32)],
        grid=(N_TILES,),
        in_specs=merge_in_specs + [
            pl.BlockSpec((1, D_MODEL), lambda i: (0, 0)), vec, vec,
            pl.BlockSpec((D_MODEL, LANES), lambda i: (0, 0)), pl.BlockSpec((D_MODEL, LANES), lambda i: (0, 0)),
            pl.BlockSpec((1, LANES), lambda i: (0, 0))],
        out_specs=[tile(D_MODEL), pl.BlockSpec((TM * SLAB, LANES), lambda i: (i, 0)), by_choice, tile(LANES),
                   by_choice, pl.BlockSpec((1, LANES), lambda i: (0, 0))],
        scratch_shapes=[pltpu.VMEM((1, LANES), F32)],
        compiler_params=_params(("arbitrary",)),
        name="merge_router",
    )(*merge_args, gain.reshape(1, D_MODEL), scale, shift, rw_hi, (rw_p - rw_hi.astype(F32)).astype(BF16), rb_p)


def _route_plan(top_i, rank, counts):
    experts = jnp.arange(N_EXPERTS, dtype=jnp.int32)
    padded = ((counts + MOE_TM - 1) // MOE_TM) * MOE_TM
    seg_end = jnp.cumsum(padded)
    seg_start = seg_end - padded
    pos = rank
    for e in range(N_EXPERTS - 1):
        pos = pos + jnp.where(top_i > e, padded[e], 0)
    n_active = seg_end[-1] // MOE_TM
    fill = jnp.concatenate([seg_start + counts, padded - counts, n_active[None]]).astype(jnp.int32)
    tile_start = jnp.arange(MOE_TILES, dtype=jnp.int32) * MOE_TM
    tile_expert = jnp.sum((seg_end[None, :] <= tile_start[:, None]).astype(jnp.int32), axis=1)
    last = jnp.sum((seg_end <= (n_active - 1) * MOE_TM).astype(jnp.int32))
    tile_expert = jnp.minimum(jnp.where(tile_start < seg_end[-1], tile_expert, last), N_EXPERTS - 1)
    owns = (padded > 0).astype(jnp.int32)
    run_of_expert = jnp.cumsum(owns) - 1
    run_expert = jnp.sum(jnp.where((run_of_expert[None, :] == experts[:, None]) & (owns[None, :] > 0),
                                   experts[None, :], 0), axis=1)
    runs = jnp.concatenate([run_expert, jnp.sum(owns)[None]]).astype(jnp.int32)
    tile_run = jnp.sum(jnp.where(tile_expert[:, None] == experts[None, :], run_of_expert[None, :], 0), axis=1)
    experts_plan = (tile_expert.astype(jnp.int32), n_active.reshape(1).astype(jnp.int32),
                    tile_run.astype(jnp.int32), runs)
    return pos.astype(jnp.int32), fill, experts_plan


DMA_UNROLL = 4
DMA_QUEUES = 2
DISPATCH_TILES = 4


def _wait_slabs(ref, n_slabs, sem):
    view = ref.at[pl.ds(0, n_slabs * SLAB)]
    pltpu.make_async_copy(view, view, sem).wait()


def _dispatch_kernel(pos_ref, fill_ref, h_ref, xs_ref, inv_ref, sem):
    i = pl.program_id(0)

    def issue(j, carry):
        for u in range(DMA_UNROLL):
            row = j * DMA_UNROLL + u
            base = (i * DISPATCH_TILES + row // TM) * (TM * TOP_K) + row % TM
            for k in range(TOP_K):
                a = base + k * TM
                slot = pos_ref[a]
                inv_ref[slot] = a
                pltpu.make_async_copy(_slab(h_ref, row), _slab(xs_ref, slot), sem).start(priority=k % DMA_QUEUES)
        return carry

    lax.fori_loop(0, DISPATCH_TILES * TM // DMA_UNROLL, issue, 0)

    @pl.when(i == 0)
    def _():
        def per_expert(e, total):
            start = fill_ref[e]
            n = fill_ref[N_EXPERTS + e]

            def one(r, carry):
                slot = start + r
                inv_ref[slot] = N_ASSIGN + slot % MOE_TM
                pltpu.make_async_copy(_slab(h_ref, 0), _slab(xs_ref, slot), sem).start()
                return carry

            lax.fori_loop(0, n, one, 0)
            return total + n

        total = lax.fori_loop(0, N_EXPERTS, per_expert, 0)

        n_active = fill_ref[2 * N_EXPERTS]

        def unused_tile(ti, carry):
            dst = xs_ref.at[pl.ds(pl.multiple_of(ti * (MOE_TM * SLAB), MOE_TM * SLAB), MOE_TM * SLAB)]
            pltpu.make_async_copy(h_ref.at[pl.ds(0, MOE_TM * SLAB)], dst, sem).start()
            return carry

        lax.fori_loop(n_active, MOE_TILES, unused_tile, 0)

        def unused_slot(slot, carry):
            inv_ref[slot] = N_ASSIGN
            return carry

        lax.fori_loop(n_active * MOE_TM, MOE_ROWS, unused_slot, 0)
        total = total + (MOE_TILES - n_active) * MOE_TM

        @pl.when(total > 0)
        def _():
            _wait_slabs(xs_ref, total, sem)

    _wait_slabs(xs_ref, DISPATCH_TILES * TM * TOP_K, sem)


def _dispatch(h_slabs, pos, fill):
    grid_spec = pltpu.PrefetchScalarGridSpec(
        num_scalar_prefetch=2,
        grid=(N_TILES // DISPATCH_TILES,),
        in_specs=[pl.BlockSpec((DISPATCH_TILES * TM * SLAB, LANES), lambda i, pos, fill: (i, 0))],
        out_specs=[pl.BlockSpec(memory_space=pl.ANY), pl.BlockSpec(memory_space=pltpu.SMEM)],
        scratch_shapes=[pltpu.SemaphoreType.DMA],
    )
    return pl.pallas_call(
        _dispatch_kernel,
        out_shape=[jax.ShapeDtypeStruct((MOE_ROWS * SLAB, LANES), F32),
                   jax.ShapeDtypeStruct((MOE_ROWS,), jnp.int32)],
        grid_spec=grid_spec,
        compiler_params=_params(("arbitrary",)),
        name="moe_dispatch",
    )(pos, fill, h_slabs)


MOE_HALF = MOE_TM // 2


def _moe_kernel(te_ref, na_ref, ts_ref, ex_ref, inv_ref, x_ref, wgu_hbm, bgu_ref, wd_hbm, bd_ref, out_hbm,
                wgu_f32, wd_f32, wgu_bf, wd_bf, obuf0, obuf1, sems, osems, *, layer):
    i = pl.program_id(0)
    s = ts_ref[i]
    first = (i == 0) | (s != ts_ref[jnp.maximum(i - 1, 0)])
    n_active = na_ref[0]

    def start_rows(buf, half, tile):
        for r in range(MOE_HALF):
            dst = _slab(out_hbm, inv_ref[tile * MOE_TM + half * MOE_HALF + r])
            pltpu.make_async_copy(buf.at[pl.ds(r * SLAB, SLAB)], dst, osems.at[half]).start(
                priority=r % DMA_QUEUES)

    def wait_rows(half):
        _wait_slabs(out_hbm, MOE_HALF, osems.at[half])

    def ffn_tile(deferred):
        if deferred:
            start_rows(obuf1, 1, i - 1)
            wait_rows(0)
        x = _load_slabs(x_ref, MOE_TM).astype(BF16)
        gu = _dot(x, wgu_bf[...]) + bgu_ref[...]
        gate = jnp.minimum(gu[:, :D_FF], SWIGLU_LIMIT)
        up = jnp.clip(gu[:, D_FF:], -SWIGLU_LIMIT, SWIGLU_LIMIT)
        act = ((up + 1.0) * gate * jax.nn.sigmoid(SWIGLU_ALPHA * gate)).astype(BF16)
        o0 = _dot(act[:MOE_HALF], wd_bf[...]) + bd_ref[...]
        _store_slabs(obuf0, o0)
        start_rows(obuf0, 0, i)
        o1 = _dot(act[MOE_HALF:], wd_bf[...]) + bd_ref[...]
        if deferred:
            wait_rows(1)
        _store_slabs(obuf1, o1)

    def weight_copies(slot):
        e = ex_ref[slot]
        b = slot % 2
        return (pltpu.make_async_copy(wgu_hbm.at[layer, e], wgu_f32.at[b], sems.at[0, b]),
                pltpu.make_async_copy(wd_hbm.at[layer, e], wd_f32.at[b], sems.at[1, b]))

    @pl.when(i == 0)
    def _():
        for cp in weight_copies(0):
            cp.start(priority=1)

    @pl.when(first)
    def _():
        for cp in weight_copies(s):
            cp.wait()

        @pl.when(s + 1 < ex_ref[N_EXPERTS])
        def _():
            for cp in weight_copies(s + 1):
                cp.start(priority=1)

        b = s % 2
        wgu_bf[...] = wgu_f32[b].astype(BF16)
        wd_bf[...] = wd_f32[b].astype(BF16)

    @pl.when(i == 0)
    def _():
        ffn_tile(False)

    @pl.when((i > 0) & (i < n_active))
    def _():
        ffn_tile(True)

    @pl.when(i == n_active)
    def _():
        wait_rows(0)
        start_rows(obuf1, 1, i - 1)
        wait_rows(1)
        for half, buf in enumerate((obuf0, obuf1)):
            dst = out_hbm.at[pl.ds((N_ASSIGN + half * MOE_HALF) * SLAB, MOE_HALF * SLAB)]
            pltpu.make_async_copy(buf, dst, osems.at[half]).start()
        wait_rows(0)
        wait_rows(1)


def _moe_experts(layer, xs, plan, inv, w_gu, b_gu, w_down, b_down):
    const = lambda i, te, na, ts, ex, inv: (layer, te[i], 0, 0)
    grid_spec = pltpu.PrefetchScalarGridSpec(
        num_scalar_prefetch=5,
        grid=(MOE_TILES,),
        in_specs=[
            pl.BlockSpec((MOE_TM * SLAB, LANES), lambda i, te, na, ts, ex, inv: (jnp.minimum(i, na[0] - 1), 0)),
            pl.BlockSpec(memory_space=pl.ANY),
            pl.BlockSpec((None, None, 1, 2 * D_FF), const),
            pl.BlockSpec(memory_space=pl.ANY),
            pl.BlockSpec((None, None, 1, D_MODEL), const),
        ],
        out_specs=pl.BlockSpec(memory_space=pl.ANY),
        scratch_shapes=[pltpu.VMEM((2, D_MODEL, 2 * D_FF), F32), pltpu.VMEM((2, D_FF, D_MODEL), F32),
                        pltpu.VMEM((D_MODEL, 2 * D_FF), BF16), pltpu.VMEM((D_FF, D_MODEL), BF16),
                        pltpu.VMEM((MOE_HALF * SLAB, LANES), F32), pltpu.VMEM((MOE_HALF * SLAB, LANES), F32),
                        pltpu.SemaphoreType.DMA((2, 2)), pltpu.SemaphoreType.DMA((2,))],
    )
    return pl.pallas_call(
        functools.partial(_moe_kernel, layer=layer),
        out_shape=jax.ShapeDtypeStruct(((N_ASSIGN + MOE_TM) * SLAB, LANES), F32),
        grid_spec=grid_spec,
        compiler_params=_params(("arbitrary",)),
        name="moe_experts",
    )(*plan, inv, xs, w_gu, b_gu.reshape(DEPTH, N_EXPERTS, 1, 2 * D_FF), w_down,
      b_down.reshape(DEPTH, N_EXPERTS, 1, D_MODEL))


def _final_combine_kernel(out_ref, y_ref, tp_ref, g_ref, fg_ref, n_ctx_ref, n_lat_ref):
    chunks = _moe_combine_chunks(out_ref, y_ref, tp_ref, g_ref)
    ss = jnp.zeros((TM, 1), F32)
    for yc in chunks:
        ss = ss + jnp.sum(yc * yc, axis=-1, keepdims=True)
    inv = lax.rsqrt(ss * (1.0 / D_MODEL) + EPS)

    def store(n_ref):
        for c in range(SLAB):
            cs = slice(c * LANES, (c + 1) * LANES)
            n_ref[:, cs] = chunks[c] * inv * fg_ref[:, cs]

    @pl.when(_is_ctx_tile())
    def _():
        store(n_ctx_ref)

    @pl.when(jnp.logical_not(_is_ctx_tile()))
    def _():
        store(n_lat_ref)


def _final_combine(y, out_slabs, top_p, gate, final_gain):
    tile = pl.BlockSpec((TM, D_MODEL), lambda i: (i, 0))
    return pl.pallas_call(
        _final_combine_kernel,
        out_shape=[jax.ShapeDtypeStruct((T_CTX, D_MODEL), F32), jax.ShapeDtypeStruct((T_LAT, D_MODEL), F32)],
        grid=(N_TILES,),
        in_specs=[pl.BlockSpec((TOP_K * TM * SLAB, LANES), lambda i: (i, 0)), tile,
                  pl.BlockSpec((TM, LANES), lambda i: (i, 0)),
                  pl.BlockSpec((None, 1, D_MODEL), lambda i: (_mod_row(i), 0, 0)),
                  pl.BlockSpec((1, D_MODEL), lambda i: (0, 0))],
        out_specs=[pl.BlockSpec((TM, D_MODEL), lambda i: (jnp.minimum(i, CTX_TILES - 1), 0)),
                   pl.BlockSpec((TM, D_MODEL), lambda i: (jnp.maximum(i - CTX_TILES, 0), 0))],
        compiler_params=_params(("arbitrary",)),
        name="moe_combine",
    )(out_slabs, y, top_p, gate, final_gain.reshape(1, D_MODEL))


def _moe_experts_layer(layer, routed, w_gu, b_gu, w_down, b_down):
    h_slabs, top_i, top_p, rank, counts = routed
    pos, fill, experts_plan = _route_plan(top_i[:, :TOP_K].reshape(-1), rank[:, :TOP_K].reshape(-1),
                                          counts[0, :N_EXPERTS].astype(jnp.int32))
    xs, inv = _dispatch(h_slabs, pos, fill)
    return _moe_experts(layer, xs, experts_plan, inv, w_gu, b_gu, w_down, b_down), top_p


def kernel(x_prompt, x_sample, c, cache_a_k, cache_a_v, state_b_C, state_b_n, state_b_m, cache_c_k, cache_c_v, cache_d_k, cache_d_v, c_ctx, w_mod, b_mod, norm1_g, norm2_g, w_in_even, w_out_even, a_q_gain, a_k_gain, b_gate_bias, b_norm_gain, w_in_odd, w_out_odd, c_rpb, d_lambda, d_norm_gain, router_w, router_b, expert_w_gu, expert_b_gu, expert_w_down, expert_b_down, final_norm_g):
    y = (x_prompt.reshape(T_CTX, D_MODEL), x_sample.reshape(T_LAT, D_MODEL))
    cond = jnp.zeros((MOD_ROWS, D_MODEL), F32).at[0].set(c_ctx).at[1:1 + DEC_BATCH].set(c)
    mod = _modulation(cond, w_mod, b_mod).reshape(DEPTH, MOD_ROWS, 6, 1, D_MODEL)
    rope_cos, rope_sin = _rope_tables()
    scale = HD ** -0.5
    outs = {}

    for layer in range(DEPTH):
        sh1, sc1, g1, sh2, sc2, g2 = (mod[layer, :, k] for k in range(6))
        j = layer // 2
        if layer % 2 == 0:
            w = w_in_even[j]
            sizes = np.cumsum([0, 512, 128, 128, 512, 512, 512, 512, 16])
            aq, ak, av, bq, bk, bv, bo, bg = (w[:, sizes[k]:sizes[k + 1]] for k in range(8))
            w_in = jnp.concatenate([aq, bo, bq, bk, bv, ak, av, bg, jnp.zeros((D_MODEL, EV_N - EV_BG - 16), F32)],
                                   axis=1).astype(BF16)
            qg = jnp.tile(a_q_gain[j], LANES // HD).reshape(1, LANES)
            kg = jnp.tile(a_k_gain[j], LANES // HD).reshape(1, LANES)
            specs = ((EV_AQ, 512, 0, True, scale, BF16, None), (EV_AK, 128, 1, True, 1.0, BF16, None),
                     (EV_AV, 128, None, False, 1.0, BF16, None),
                     (EV_AK, 128, 1, False, 1.0, F32, A_KV), (EV_AV, 128, None, False, 1.0, F32, A_KV))
            y, p, (qa, ka, va, new_ak, new_av) = _project(y, norm1_g[layer], sc1, sh1, w_in, rope_cos, rope_sin,
                                                          [qg, kg], specs)
            oa_ctx = _attention(qa, ka, va, ctx=True, group=A_HEADS // A_KV, n_kv=A_KV)
            cache = (cache_a_k[:, j].astype(BF16), cache_a_v[:, j].astype(BF16))
            oa_lat = _attention(qa, ka, va, ctx=False, group=A_HEADS // A_KV, n_kv=A_KV, cache=cache)
            zc = jnp.zeros((BATCH, 2, B_HEADS, B_DK, B_DV), F32)
            zn = jnp.zeros((BATCH, 2, B_HEADS, 1, B_DK), F32)
            zm = jnp.zeros((BATCH, 2, B_HEADS, 1, LANES), F32)
            hb_ctx, bC, bn, bm = _mlstm(p, b_gate_bias[j], zc, zn, zm, ctx=True)
            m0 = jnp.broadcast_to(state_b_m[:, j][..., None, None], (DEC_BATCH, 2, B_HEADS, 1, LANES))
            hb_lat, _, _, _ = _mlstm(p, b_gate_bias[j], state_b_C[:, j], state_b_n[:, j][:, :, :, None, :], m0,
                                     ctx=False)
            y, *routed = _merge_router(y, g1, w_out_even[j].astype(BF16), (oa_ctx, oa_lat), (hb_ctx, hb_lat),
                                       norm2_g[layer], sc2, sh2, router_w[layer], router_b[layer],
                                       p=p, norm_gain=b_norm_gain[j])
            outs.setdefault("a_k", []).append(new_ak)
            outs.setdefault("a_v", []).append(new_av)
            outs.setdefault("b_C", []).append(bC)
            outs.setdefault("b_n", []).append(bn[:, :, :, 0, :])
            outs.setdefault("b_m", []).append(bm[:, :, :, 0, 0])
        else:
            specs = ((0, 512, None, False, scale, BF16, None), (512, 512, None, False, 1.0, BF16, None),
                     (1024, 512, None, False, 1.0, BF16, None), (1536, 512, None, True, scale, BF16, None),
                     (2048, 512, None, True, 1.0, BF16, None), (2560, 512, None, False, 1.0, BF16, None),
                     (512, 512, None, False, 1.0, F32, C_HEADS), (1024, 512, None, False, 1.0, F32, C_HEADS),
                     (2048, 512, None, False, 1.0, F32, 2 * D_HEADS), (2560, 512, None, False, 1.0, F32, D_HEADS))
            y, p, (qc, kc, vc, qd, kd, vd, new_ck, new_cv, new_dk, new_dv) = _project(
                y, norm1_g[layer], sc1, sh1, w_in_odd[j].astype(BF16), rope_cos, rope_sin, [], specs)
            lam_init = 0.8 - 0.6 * math.exp(-0.3 * layer)
            lp = d_lambda[j].astype(F32)
            lam = jnp.exp(jnp.sum(lp[0] * lp[1])) - jnp.exp(jnp.sum(lp[2] * lp[3])) + lam_init
            lam_vec = jnp.stack([lam, jnp.asarray(1.0 - lam_init, F32)]).astype(F32)
            diff = (lam_vec, d_norm_gain[j])
            oc_ctx = _attention(qc, kc, vc, ctx=True, n_kv=C_HEADS)
            od_ctx = _attention(qd, kd, vd, ctx=True, diff=diff)
            bias = _neighbourhood_bias(c_rpb[j])
            oc_lat = _attention(qc, kc, vc, ctx=False, n_kv=C_HEADS, bias=bias,
                                cache=(cache_c_k[:, j].astype(BF16), cache_c_v[:, j].astype(BF16)))
            kd_cache = cache_d_k[:, j].reshape(DEC_BATCH, 2 * D_HEADS, PAST_LEN, HD).astype(BF16)
            od_lat = _attention(qd, kd, vd, ctx=False, diff=diff, cache=(kd_cache, cache_d_v[:, j].astype(BF16)))
            y, *routed = _merge_router(y, g1, w_out_odd[j].astype(BF16), (oc_ctx, oc_lat), (od_ctx, od_lat),
                                       norm2_g[layer], sc2, sh2, router_w[layer], router_b[layer])
            outs.setdefault("c_k", []).append(new_ck)
            outs.setdefault("c_v", []).append(new_cv)
            outs.setdefault("d_k", []).append(new_dk.reshape(BATCH, D_HEADS, 2, SEQ, HD))
            outs.setdefault("d_v", []).append(new_dv)
        out_slabs, top_p = _moe_experts_layer(layer, routed, expert_w_gu, expert_b_gu, expert_w_down,
                                              expert_b_down)
        y = (y, out_slabs, top_p, g2)
    y_prompt, y_sample = _final_combine(*y, final_norm_g)
    stack = lambda k: jnp.stack(outs[k], axis=1)
    return (y_prompt.reshape(BATCH, SEQ, D_MODEL), y_sample.reshape(DEC_BATCH, DEC_SEQ, D_MODEL),
            stack("a_k"), stack("a_v"), stack("b_C"), stack("b_n"), stack("b_m"),
            stack("c_k"), stack("c_v"), stack("d_k"), stack("d_v"))
```

```python
import functools
import math

import numpy as np
import jax
import jax.numpy as jnp
from jax import lax
from jax.experimental import pallas as pl
from jax.experimental.pallas import tpu as pltpu

D_MODEL = 1024
BATCH = 32
SEQ = 256
DEPTH = 2
DEC_BATCH = 8
DEC_SEQ = 1024
PAST_LEN = 512
GRID_W = 64
HD = 64
A_HEADS = 8
A_KV = 2
B_HEADS = 4
B_DK = 128
B_DV = 128
B_CHUNK = 128
C_HEADS = 8
NA_ROWS = 8
NA_COLS = 16
D_HEADS = 4
D_VDIM = 2 * HD
N_EXPERTS = 32
TOP_K = 4
D_FF = 1024
SWIGLU_LIMIT = 7.0
SWIGLU_ALPHA = 1.702
ROPE_THETA = 10000.0
EPS = 1e-6

F32 = jnp.float32
BF16 = jnp.bfloat16
HIGHEST = lax.Precision.HIGHEST

T_CTX = BATCH * SEQ
T_LAT = DEC_BATCH * DEC_SEQ
T_ALL = T_CTX + T_LAT
TM = 256
CTX_TILES = T_CTX // TM
LAT_TILES_PER_BATCH = DEC_SEQ // TM
N_TILES = T_ALL // TM
MOD_ROWS = 16
LANES = 128
NEG_BIG = -1e30
MOE_TM = 256
N_ASSIGN = T_ALL * TOP_K
MOE_ROWS = N_ASSIGN + N_EXPERTS * MOE_TM
MOE_TILES = MOE_ROWS // MOE_TM
VMEM_LIMIT = 56 * 1024 * 1024
assert TM == SEQ and DEC_SEQ % TM == 0 and MOE_TM == TM

EV_AQ, EV_BO, EV_BQ, EV_BK, EV_BV, EV_AK, EV_AV, EV_BG = 0, 512, 1024, 1536, 2048, 2560, 2688, 2816
EV_N = 2944


def _params(sem, vmem=VMEM_LIMIT):
    return pltpu.CompilerParams(dimension_semantics=sem, vmem_limit_bytes=vmem)


def _mod_row(i):
    return jnp.where(i < CTX_TILES, 0, 1 + (i - CTX_TILES) // LAT_TILES_PER_BATCH)


def _rope_block(i):
    return jnp.where(i < CTX_TILES, LAT_TILES_PER_BATCH, (i - CTX_TILES) % LAT_TILES_PER_BATCH)


def _dot(a, b, precision=None):
    return jnp.dot(a, b, preferred_element_type=F32, precision=precision)


def _dot_nt(a, b):
    return lax.dot_general(a, b, (((1,), (1,)), ((), ())), preferred_element_type=F32)


def _dot_tn(a, b):
    return lax.dot_general(a, b, (((0,), (0,)), ((), ())), preferred_element_type=F32)


def _modulation_kernel(c_ref, w_ref, b_ref, o_ref):
    c = c_ref[...]
    s = c * jax.nn.sigmoid(c)
    o_ref[...] = _dot(s, w_ref[...], HIGHEST) + b_ref[...]


def _modulation(cond, w_mod, b_mod):
    tn = 1536
    return pl.pallas_call(
        _modulation_kernel,
        out_shape=jax.ShapeDtypeStruct((DEPTH, MOD_ROWS, 6 * D_MODEL), F32),
        grid=(DEPTH, 6 * D_MODEL // tn),
        in_specs=[
            pl.BlockSpec((MOD_ROWS, D_MODEL), lambda l, j: (0, 0)),
            pl.BlockSpec((None, D_MODEL, tn), lambda l, j: (l, 0, j)),
            pl.BlockSpec((None, 1, tn), lambda l, j: (l, 0, j)),
        ],
        out_specs=pl.BlockSpec((None, MOD_ROWS, tn), lambda l, j: (l, 0, j)),
        compiler_params=_params(("parallel", "parallel")),
        name="modulation",
    )(cond, w_mod, b_mod.reshape(DEPTH, 1, 6 * D_MODEL))


def _norm_mod(y, g, sc, sh):
    ms = jnp.mean(y * y, axis=-1, keepdims=True)
    return (y * lax.rsqrt(ms + EPS) * g) * (1.0 + sc) + sh


def _rope_rotate(x):
    w = x.shape[-1]
    lane = lax.broadcasted_iota(jnp.int32, x.shape, 1)
    nxt = pltpu.roll(x, w - 1, 1)
    prv = pltpu.roll(x, 1, 1)
    return jnp.where((lane & 1) == 0, -nxt, prv)


def _is_ctx_tile():
    return pl.program_id(0) < CTX_TILES


def _pair_specs(width):
    return [pl.BlockSpec((TM, width), lambda i: (jnp.minimum(i, CTX_TILES - 1), 0)),
            pl.BlockSpec((TM, width), lambda i: (jnp.maximum(i - CTX_TILES, 0), 0))]


def _pair_value(ctx_ref, lat_ref, cols=slice(None)):
    return jnp.where(_is_ctx_tile(), ctx_ref[:, cols], lat_ref[:, cols])


def _moe_combine_chunks(out_ref, y_ref, tp_ref, g_ref):
    tp = tp_ref[...]
    chunks = []
    for c in range(SLAB):
        cs = slice(c * LANES, (c + 1) * LANES)
        acc = tp[:, 0:1] * out_ref[pl.ds(c, TM, stride=SLAB), :]
        for k in range(1, TOP_K):
            acc = acc + tp[:, k:k + 1] * out_ref[pl.ds(k * TM * SLAB + c, TM, stride=SLAB), :]
        chunks.append(y_ref[:, cs] + g_ref[:, cs] * acc)
    return chunks


def _norm_proj_kernel(*refs, specs, n_gain, y_mode):
    n_y = {"single": 1, "pair": 2, "combine": 4}[y_mode]
    g_ref, sc_ref, sh_ref, w_ref, cos_ref, sin_ref, bd_ref = refs[n_y:n_y + 7]
    gain_refs = refs[n_y + 7:n_y + 7 + n_gain]
    p_ref = refs[n_y + 7 + n_gain]
    out_refs = refs[n_y + 8 + n_gain:]
    if y_mode == "combine":
        y = jnp.concatenate(_moe_combine_chunks(*refs[:4]), axis=1)
        out_refs[0][...] = y
        out_refs = out_refs[1:]
    elif y_mode == "pair":
        y = _pair_value(refs[0], refs[1])
    else:
        y = refs[0][...]
    h = _norm_mod(y, g_ref[...], sc_ref[...], sh_ref[...])
    p_ref[...] = _dot(h.astype(BF16), w_ref[...])
    cos = cos_ref[...]
    sin = sin_ref[...]
    for (col, width, gi, rope, scale, _, heads), o_ref in zip(specs, out_refs):
        for c0 in range(0, width, LANES):
            x = p_ref[:, col + c0:col + c0 + LANES]
            if gi is not None:
                xx = x * x
                xx_hi = xx.astype(BF16)
                xx_lo = (xx - xx_hi.astype(F32)).astype(BF16)
                ss = _dot(xx_hi, bd_ref[...]) + _dot(xx_lo, bd_ref[...])
                x = x * lax.rsqrt(ss * (1.0 / HD) + EPS) * gain_refs[gi][...]
            if rope:
                x = x * cos + _rope_rotate(x) * sin
            if scale != 1.0:
                x = x * scale
            if heads is None:
                o_ref[:, c0:c0 + LANES] = x.astype(o_ref.dtype)
            else:
                hw = width // heads
                per = LANES // hw

                @pl.when(_is_ctx_tile())
                def _(x=x, o_ref=o_ref, c0=c0, hw=hw, per=per):
                    for u in range(per):
                        o_ref[(c0 // LANES) * per + u] = x[:, u * hw:(u + 1) * hw].astype(o_ref.dtype)


def _norm_proj(y, gain, scale, shift, w_bf16, rope_cos, rope_sin, gains, specs):
    n = w_bf16.shape[1]
    tile = pl.BlockSpec((TM, D_MODEL), lambda i: (i, 0))
    vec = pl.BlockSpec((None, 1, D_MODEL), lambda i: (_mod_row(i), 0, 0))
    if not isinstance(y, tuple):
        y_mode, ys, in_specs = "single", [y], [tile]
    elif len(y) == 2:
        y_mode, ys, in_specs = "pair", list(y), _pair_specs(D_MODEL)
    else:
        y_prev, out_slabs, top_p, gate = y
        y_mode, ys = "combine", [out_slabs, y_prev, top_p, gate]
        in_specs = [pl.BlockSpec((TOP_K * TM * SLAB, LANES), lambda i: (i, 0)), tile,
                    pl.BlockSpec((TM, LANES), lambda i: (i, 0)), vec]
    bd = jnp.asarray(np.kron(np.eye(LANES // HD), np.ones((HD, HD))), BF16)
    rope_spec = pl.BlockSpec((TM, LANES), lambda i: (_rope_block(i), 0))
    in_specs += [pl.BlockSpec((1, D_MODEL), lambda i: (0, 0)),
                 vec, vec, pl.BlockSpec((D_MODEL, n), lambda i: (0, 0)),
                 rope_spec, rope_spec, pl.BlockSpec((LANES, LANES), lambda i: (0, 0))]
    in_specs += [pl.BlockSpec((1, LANES), lambda i: (0, 0)) for _ in gains]
    out_shape = [jax.ShapeDtypeStruct((T_ALL, n), F32)]
    out_specs = [pl.BlockSpec((TM, n), lambda i: (i, 0))]
    if y_mode == "combine":
        out_shape.append(jax.ShapeDtypeStruct((T_ALL, D_MODEL), F32))
        out_specs.append(tile)
    for (_, width, _, _, _, dtype, heads) in specs:
        if heads is None:
            out_shape.append(jax.ShapeDtypeStruct((T_ALL, width), dtype))
            out_specs.append(pl.BlockSpec((TM, width), lambda i: (i, 0)))
        else:
            out_shape.append(jax.ShapeDtypeStruct((BATCH, heads, SEQ, width // heads), dtype))
            out_specs.append(pl.BlockSpec((None, heads, SEQ, width // heads),
                                          lambda i: (jnp.minimum(i, CTX_TILES - 1), 0, 0, 0)))
    return pl.pallas_call(
        functools.partial(_norm_proj_kernel, specs=specs, n_gain=len(gains), y_mode=y_mode),
        out_shape=out_shape,
        grid=(N_TILES,),
        in_specs=in_specs,
        out_specs=out_specs,
        compiler_params=_params(("arbitrary",)),
        name="norm_proj",
    )(*ys, gain.reshape(1, D_MODEL), scale, shift, w_bf16, rope_cos, rope_sin, bd, *gains)


def _project(y, *args):
    res = _norm_proj(y, *args)
    if isinstance(y, tuple) and len(y) == 4:
        return res[1], res[0], res[2:]
    return y, res[0], res[1:]


def _rope_tables():
    half = HD // 2
    freqs = 1.0 / (ROPE_THETA ** (jnp.arange(0, half, 2, dtype=F32) / half))
    t = jnp.arange(DEC_SEQ)
    rows = (t // GRID_W).astype(F32)
    cols = (t % GRID_W).astype(F32)
    ang = jnp.concatenate([rows[:, None] * freqs, cols[:, None] * freqs], axis=-1)
    cos = jnp.repeat(jnp.cos(ang), 2, axis=-1)
    sin = jnp.repeat(jnp.sin(ang), 2, axis=-1)
    cos = jnp.concatenate([jnp.tile(cos, (1, LANES // HD)), jnp.ones((TM, LANES), F32)], axis=0)
    sin = jnp.concatenate([jnp.tile(sin, (1, LANES // HD)), jnp.zeros((TM, LANES), F32)], axis=0)
    return cos, sin


def _lane_slice(ref, h, width=HD):
    per = LANES // width
    blk = ref[:, (h // per) * LANES:(h // per + 1) * LANES]
    if per == 1:
        return blk
    return blk[:, (h % per) * width:(h % per + 1) * width]


def _softmax_parts(scores):
    m = None
    for s in scores:
        ms = jnp.max(s, axis=-1, keepdims=True)
        m = ms if m is None else jnp.maximum(m, ms)
    ps = [jnp.exp(s - m) for s in scores]
    l = None
    for p in ps:
        ls = jnp.sum(p, axis=-1, keepdims=True)
        l = ls if l is None else l + ls
    return ps, l


def _attn_std_kernel(*refs, group, n_kv, has_cache, has_bias, bq):
    it = iter(refs)
    q_ref, kn_ref, vn_ref = next(it), next(it), next(it)
    kc_ref = vc_ref = b_ref = None
    if has_cache:
        kc_ref, vc_ref = next(it), next(it)
    if has_bias:
        b_ref = next(it)
    o_ref = next(it)
    outs = []
    for g in range(n_kv):
        qs = jnp.concatenate([_lane_slice(q_ref, g * group + j) for j in range(group)], axis=0)
        kn = _lane_slice(kn_ref, g)
        vn = _lane_slice(vn_ref, g)
        s_new = _dot_nt(qs, kn)
        if has_bias:
            s_new = s_new + b_ref[g]
        scores = [s_new]
        if has_cache:
            scores.append(_dot_nt(qs, kc_ref[g]))
        ps, l = _softmax_parts(scores)
        o = _dot(ps[0].astype(BF16), vn)
        if has_cache:
            o = o + _dot(ps[1].astype(BF16), vc_ref[g])
        o = o / l
        for j in range(group):
            outs.append(o[j * bq:(j + 1) * bq])
    o_ref[...] = jnp.concatenate(outs, axis=1).astype(o_ref.dtype)


def _attn_diff_kernel(*refs, has_cache):
    it = iter(refs)
    lam_ref, q_ref, kn_ref, vn_ref = next(it), next(it), next(it), next(it)
    kc_ref = vc_ref = None
    if has_cache:
        kc_ref, vc_ref = next(it), next(it)
    g_ref, o_ref = next(it), next(it)
    lam = lam_ref[0]
    post = lam_ref[1]
    outs = []
    for h in range(D_HEADS):
        pd_new, pd_c = None, None
        for j in range(2):
            f = 2 * h + j
            qs = _lane_slice(q_ref, f)
            scores = [_dot_nt(qs, _lane_slice(kn_ref, f))]
            if has_cache:
                scores.append(_dot_nt(qs, kc_ref[f]))
            ps, l = _softmax_parts(scores)
            r = 1.0 / l
            if j == 0:
                pd_new = ps[0] * r
                pd_c = ps[1] * r if has_cache else None
            else:
                r = r * lam
                pd_new = pd_new - ps[0] * r
                pd_c = pd_c - ps[1] * r if has_cache else None
        o = _dot(pd_new.astype(BF16), _lane_slice(vn_ref, h, D_VDIM))
        if has_cache:
            o = o + _dot(pd_c.astype(BF16), vc_ref[h])
        ms = jnp.mean(o * o, axis=-1, keepdims=True)
        outs.append(o * lax.rsqrt(ms + EPS) * g_ref[...] * post)
    o_ref[...] = jnp.concatenate(outs, axis=1).astype(o_ref.dtype)


def _attention(q, kn, vn, *, ctx, group=1, n_kv=1, cache=None, bias=None, diff=None, bq=256):
    if ctx:
        nb, sq, row0 = BATCH, SEQ, 0
    else:
        nb, sq, row0 = DEC_BATCH, DEC_SEQ, T_CTX
    nq = sq // bq
    qb0 = row0 // bq
    kb0 = row0 // sq
    wq, wk, wv = q.shape[1], kn.shape[1], vn.shape[1]
    in_specs = [
        pl.BlockSpec((bq, wq), lambda b, i: (qb0 + b * nq + i, 0)),
        pl.BlockSpec((sq, wk), lambda b, i: (kb0 + b, 0)),
        pl.BlockSpec((sq, wv), lambda b, i: (kb0 + b, 0)),
    ]
    args = [q, kn, vn]
    if cache is not None:
        kc, vc = cache
        in_specs += [pl.BlockSpec((None,) + kc.shape[1:], lambda b, i: (b, 0, 0, 0)),
                     pl.BlockSpec((None,) + vc.shape[1:], lambda b, i: (b, 0, 0, 0))]
        args += [kc, vc]
    if diff is None:
        if bias is not None:
            in_specs.append(pl.BlockSpec((bias.shape[0], bq, sq), lambda b, i: (0, i, 0)))
            args.append(bias)
        body = functools.partial(_attn_std_kernel, group=group, n_kv=n_kv, has_cache=cache is not None,
                                 has_bias=bias is not None, bq=bq)
    else:
        lam_vec, gain = diff
        in_specs = [pl.BlockSpec(memory_space=pltpu.SMEM)] + in_specs
        args = [lam_vec] + args
        in_specs.append(pl.BlockSpec((1, D_VDIM), lambda b, i: (0, 0)))
        args.append(gain.reshape(1, D_VDIM))
        body = functools.partial(_attn_diff_kernel, has_cache=cache is not None)
    return pl.pallas_call(
        body,
        out_shape=jax.ShapeDtypeStruct((nb * sq, 512), BF16),
        grid=(nb, nq),
        in_specs=in_specs,
        out_specs=pl.BlockSpec((bq, 512), lambda b, i: (b * nq + i, 0)),
        compiler_params=_params(("parallel", "parallel")),
        name="attention",
    )(*args)


GRID_ROWS = DEC_SEQ // GRID_W
NA_WIN_ROWS = min(NA_ROWS, GRID_ROWS)


def _na_bias_kernel(t_ref, o_ref):
    outside = jnp.full((GRID_W, GRID_W), NEG_BIG, F32)
    for qr in range(GRID_ROWS):
        r0 = min(max(qr - NA_WIN_ROWS // 2, 0), GRID_ROWS - NA_WIN_ROWS)
        parts = [t_ref[kr - qr + (NA_ROWS - 1)] if r0 <= kr < r0 + NA_WIN_ROWS else outside
                 for kr in range(GRID_ROWS)]
        o_ref[qr * GRID_W:(qr + 1) * GRID_W, :] = jnp.concatenate(parts, axis=1)


def _neighbourhood_bias(rpb):
    c = np.arange(GRID_W)
    c0 = np.clip(c - NA_COLS // 2, 0, GRID_W - NA_COLS)
    col_ok = (c[None, :] >= c0[:, None]) & (c[None, :] < c0[:, None] + NA_COLS)
    dc = np.clip(c[None, :] - c[:, None], 1 - NA_COLS, NA_COLS - 1) + (NA_COLS - 1)
    oh_c = jnp.asarray(dc[..., None] == np.arange(2 * NA_COLS - 1), F32)
    by_col = jnp.einsum("hrd,qkd->hrqk", rpb.astype(F32), oh_c, precision=HIGHEST)
    by_col = jnp.where(jnp.asarray(col_ok), by_col, NEG_BIG)
    n_dr = 2 * NA_ROWS - 1
    return pl.pallas_call(
        _na_bias_kernel,
        out_shape=jax.ShapeDtypeStruct((C_HEADS, DEC_SEQ, DEC_SEQ), F32),
        grid=(C_HEADS,),
        in_specs=[pl.BlockSpec((None, n_dr, GRID_W, GRID_W), lambda h: (h, 0, 0, 0))],
        out_specs=pl.BlockSpec((None, DEC_SEQ, DEC_SEQ), lambda h: (h, 0, 0)),
        compiler_params=_params(("parallel",)),
        name="na_bias",
    )(by_col)


def _log_sigmoid(x):
    return jnp.minimum(x, 0.0) - jnp.log1p(jnp.exp(-jnp.abs(x)))


def _mlstm_kernel(q_ref, k_ref, v_ref, g_ref, gb_ref, c0_ref, n0_ref, m0_ref, h_ref, c_ref, n_ref, m_ref, hb_ref,
                  *, seq):
    L = B_CHUNK
    nc = seq // L
    row = lax.broadcasted_iota(jnp.int32, (L, L), 0)
    col = lax.broadcasted_iota(jnp.int32, (L, L), 1)
    keeps = (col <= row, col >= row)
    k_scale = B_DK ** -0.5
    c_ref[...] = c0_ref[...]
    n_ref[...] = n0_ref[...]
    m_ref[...] = m0_ref[...]

    def step(j):
        for d in range(2):
            keep = keeps[d]
            c = j if d == 0 else nc - 1 - j
            off = c * L if isinstance(c, int) else pl.multiple_of(c * L, L)
            gates = g_ref[pl.ds(off, L), :] + gb_ref[...]
            cum = _dot(keep.astype(F32), _log_sigmoid(gates), HIGHEST)
            cum_t = cum.T
            gates_t = gates.T
            out_ref = h_ref if d == 0 else hb_ref
            for h in range(B_HEADS):
                ci = (2 * d) * B_HEADS + h
                cf = (2 * d + 1) * B_HEADS + h
                hs = slice(h * B_DK, (h + 1) * B_DK)
                C = c_ref[d, h]
                n = n_ref[d, h]
                m = m_ref[d, h][:, 0:1]
                qc = q_ref[pl.ds(off, L), hs]
                kc = k_ref[pl.ds(off, L), hs] * k_scale
                vc = v_ref[pl.ds(off, L), hs]
                b_col = cum[:, cf:cf + 1]
                i_col = gates[:, ci:ci + 1]
                b_row = cum_t[cf:cf + 1, :]
                i_row = gates_t[ci:ci + 1, :]
                dlog = jnp.where(keep, b_col - b_row + i_row, -jnp.inf)
                inter = b_col + m
                m_t = jnp.maximum(inter, jnp.max(dlog, axis=-1, keepdims=True))
                w_intra = jnp.exp(dlog - m_t)
                w_inter = jnp.exp(inter - m_t)
                qb = qc.astype(BF16)
                vb = vc.astype(BF16)
                qk = _dot_nt(qb, kc.astype(BF16)) * w_intra
                num = _dot(qk.astype(BF16), vb) + w_inter * _dot(qb, C.astype(BF16))
                den = jnp.sum(qk, axis=-1, keepdims=True) + w_inter * jnp.sum(qc * n, axis=-1, keepdims=True)
                out_ref[pl.ds(off, L), hs] = num / jnp.maximum(jnp.abs(den), jnp.exp(-m_t))
                b_last = b_col[L - 1:L, :] if d == 0 else b_col[0:1, :]
                end_col = b_last - b_col + i_col
                m_new = jnp.maximum(b_last + m, jnp.max(end_col, axis=0, keepdims=True))
                w_end = jnp.exp(end_col - m_new)
                decay = jnp.exp(b_last + m - m_new)
                kw = kc * w_end
                c_ref[d, h] = decay * C + _dot_tn(kw.astype(BF16), vb)
                n_ref[d, h] = decay * n + jnp.sum(kw, axis=0, keepdims=True)
                m_ref[d, h] = jnp.broadcast_to(m_new, (1, LANES))

    if nc <= 2:
        for j in range(nc):
            step(j)
    else:
        def body(j, carry):
            step(j)
            return carry

        lax.fori_loop(0, nc, body, 0)
    h_ref[...] = h_ref[...] + hb_ref[...]


def _mlstm(p, gate_bias, c0, n0, m0, *, ctx):
    if ctx:
        nb, seq, blk0 = BATCH, SEQ, 0
    else:
        nb, seq, blk0 = DEC_BATCH, DEC_SEQ, T_CTX // DEC_SEQ
    w = B_HEADS * B_DK

    def cols(c0_, width):
        return pl.BlockSpec((seq, width), lambda b: (blk0 + b, c0_ // width))

    gb = jnp.zeros((1, LANES), F32).at[0, :4 * B_HEADS].set(gate_bias.reshape(-1).astype(F32))
    st = lambda shape: pl.BlockSpec((None,) + shape, lambda b: (b, 0, 0, 0, 0))
    return pl.pallas_call(
        functools.partial(_mlstm_kernel, seq=seq),
        out_shape=[
            jax.ShapeDtypeStruct((nb * seq, w), F32),
            jax.ShapeDtypeStruct((nb, 2, B_HEADS, B_DK, B_DV), F32),
            jax.ShapeDtypeStruct((nb, 2, B_HEADS, 1, B_DK), F32),
            jax.ShapeDtypeStruct((nb, 2, B_HEADS, 1, LANES), F32),
        ],
        grid=(nb,),
        in_specs=[
            cols(EV_BQ, w), cols(EV_BK, w), cols(EV_BV, w), cols(EV_BG, LANES),
            pl.BlockSpec((1, LANES), lambda b: (0, 0)),
            st((2, B_HEADS, B_DK, B_DV)), st((2, B_HEADS, 1, B_DK)), st((2, B_HEADS, 1, LANES)),
        ],
        out_specs=[
            pl.BlockSpec((seq, w), lambda b: (b, 0)),
            st((2, B_HEADS, B_DK, B_DV)), st((2, B_HEADS, 1, B_DK)), st((2, B_HEADS, 1, LANES)),
        ],
        scratch_shapes=[pltpu.VMEM((seq, w), F32)],
        compiler_params=_params(("parallel",)),
        name="mlstm",
    )(p, p, p, p, gb, c0, n0, m0)


def _merge_value(refs, even, y_pair):
    a_ctx, a_lat, b_ctx, b_lat = refs[:4]
    rest = refs[4:]
    if even:
        bo_ref, ng_ref = rest[:2]
        rest = rest[2:]
        parts = [_pair_value(a_ctx, a_lat)]
        for h in range(B_HEADS):
            hs = slice(h * B_DV, (h + 1) * B_DV)
            x = _pair_value(b_ctx, b_lat, hs)
            ms = jnp.mean(x * x, axis=-1, keepdims=True)
            xn = x * lax.rsqrt(ms + EPS) * ng_ref[:, hs]
            parts.append((jax.nn.sigmoid(bo_ref[:, hs]) * xn).astype(BF16))
    else:
        parts = [_pair_value(a_ctx, a_lat), _pair_value(b_ctx, b_lat)]
    w_ref = rest[0]
    y = _pair_value(rest[1], rest[2]) if y_pair else rest[1][...]
    g_ref = rest[-1]
    cat = jnp.concatenate(parts, axis=1)
    return y + g_ref[...] * _dot(cat, w_ref[...])


def _merge_specs(y, gate, w_bf16, a, b, p, norm_gain):
    in_specs = _pair_specs(512) + _pair_specs(512)
    args = [*a, *b]
    if p is not None:
        in_specs += [pl.BlockSpec((TM, 512), lambda i: (i, EV_BO // 512)), pl.BlockSpec((1, 512), lambda i: (0, 0))]
        args += [p, norm_gain.reshape(1, 512)]
    in_specs.append(pl.BlockSpec((D_MODEL, D_MODEL), lambda i: (0, 0)))
    args.append(w_bf16)
    if isinstance(y, tuple):
        in_specs += _pair_specs(D_MODEL)
        args += list(y)
    else:
        in_specs.append(pl.BlockSpec((TM, D_MODEL), lambda i: (i, 0)))
        args.append(y)
    in_specs.append(pl.BlockSpec((None, 1, D_MODEL), lambda i: (_mod_row(i), 0, 0)))
    args.append(gate)
    return in_specs, args


SLAB = D_MODEL // LANES


def _load_slabs(ref, rows):
    return jnp.concatenate([ref[pl.ds(c, rows, stride=SLAB), :] for c in range(SLAB)], axis=1)


def _store_slabs(ref, x):
    for c in range(SLAB):
        ref[pl.ds(c, x.shape[0], stride=SLAB), :] = x[:, c * LANES:(c + 1) * LANES]


def _slab(ref, idx):
    return ref.at[pl.ds(pl.multiple_of(idx * SLAB, SLAB), SLAB)]


def _merge_router_kernel(*refs, even, y_pair):
    n_merge = 4 + (2 if even else 0) + 1 + (2 if y_pair else 1) + 1
    merge_refs = refs[:n_merge]
    (g_ref, sc_ref, sh_ref, whi_ref, wlo_ref, b_ref,
     y_out_ref, h_ref, ti_ref, tp_ref, rk_ref, cnt_ref, base_ref) = refs[n_merge:]

    @pl.when(pl.program_id(0) == 0)
    def _():
        base_ref[...] = jnp.zeros(base_ref.shape, F32)

    y = _merge_value(merge_refs, even, y_pair)
    y_out_ref[...] = y
    h = _norm_mod(y, g_ref[...], sc_ref[...], sh_ref[...])
    _store_slabs(h_ref, h)
    h_hi = h.astype(BF16)
    h_lo = (h - h_hi.astype(F32)).astype(BF16)
    logits = (_dot(h_hi, whi_ref[...]) + (_dot(h_hi, wlo_ref[...]) + _dot(h_lo, whi_ref[...]))
              + b_ref[...])
    lane = lax.broadcasted_iota(jnp.int32, logits.shape, 1)
    lane_f = lane.astype(F32)
    vals, idxs = [], []
    for _ in range(TOP_K):
        mx = jnp.max(logits, axis=-1, keepdims=True)
        ix = jnp.min(jnp.where(logits == mx, lane_f, float(LANES)), axis=-1, keepdims=True)
        vals.append(mx)
        idxs.append(ix)
        logits = jnp.where(lane_f == ix, -jnp.inf, logits)
    es = [jnp.exp(v - vals[0]) for v in vals]
    tot = es[0] + es[1] + es[2] + es[3]
    ti = jnp.zeros(logits.shape, F32)
    tp = jnp.zeros(logits.shape, F32)
    for k in range(TOP_K):
        ti = jnp.where(lane == k, idxs[k], ti)
        tp = jnp.where(lane == k, es[k] / tot, tp)
    ti_ref[...] = ti.T[0:8, :].astype(jnp.int32)
    tp_ref[...] = tp
    onehots = [(lane_f == ix).astype(F32) for ix in idxs]
    cnt = onehots[0] + onehots[1] + onehots[2] + onehots[3]
    row = lax.broadcasted_iota(jnp.int32, (TM, TM), 0)
    col = lax.broadcasted_iota(jnp.int32, (TM, TM), 1)
    before = _dot((col < row).astype(BF16), cnt.astype(BF16)) + base_ref[...]
    rk = jnp.zeros(logits.shape, F32)
    for k in range(TOP_K):
        rk = jnp.where(lane == k, jnp.sum(onehots[k] * before, axis=-1, keepdims=True), rk)
    rk_ref[...] = rk.T[0:8, :].astype(jnp.int32)
    base_ref[...] = base_ref[...] + jnp.sum(cnt, axis=0, keepdims=True)
    cnt_ref[...] = base_ref[...]


def _merge_router(y, gate, w_out_bf16, a, b, gain, scale, shift, rw, rb, *, p=None, norm_gain=None):
    merge_in_specs, merge_args = _merge_specs(y, gate, w_out_bf16, a, b, p, norm_gain)
    vec = pl.BlockSpec((None, 1, D_MODEL), lambda i: (_mod_row(i), 0, 0))
    rw_p = jnp.zeros((D_MODEL, LANES), F32).at[:, :N_EXPERTS].set(rw)
    rb_p = jnp.full((1, LANES), NEG_BIG, F32).at[0, :N_EXPERTS].set(rb)
    rw_hi = rw_p.astype(BF16)
    tile = lambda w: pl.BlockSpec((TM, w), lambda i: (i, 0))
    by_choice = pl.BlockSpec((None, 8, TM), lambda i: (i, 0, 0))
    return pl.pallas_call(
        functools.partial(_merge_router_kernel, even=p is not None, y_pair=isinstance(y, tuple)),
        out_shape=[jax.ShapeDtypeStruct((T_ALL, D_MODEL), F32),
                   jax.ShapeDtypeStruct((T_ALL * SLAB, LANES), F32),
                   jax.ShapeDtypeStruct((N_TILES, 8, TM), jnp.int32),
                   jax.ShapeDtypeStruct((T_ALL, LANES), F32),
                   jax.ShapeDtypeStruct((N_TILES, 8, TM), jnp.int32),
                   jax.ShapeDtypeStruct((1, LANES), F32)],
        grid=(N_TILES,),
        in_specs=merge_in_specs + [
            pl.BlockSpec((1, D_MODEL), lambda i: (0, 0)), vec, vec,
            pl.BlockSpec((D_MODEL, LANES), lambda i: (0, 0)), pl.BlockSpec((D_MODEL, LANES), lambda i: (0, 0)),
            pl.BlockSpec((1, LANES), lambda i: (0, 0))],
        out_specs=[tile(D_MODEL), pl.BlockSpec((TM * SLAB, LANES), lambda i: (i, 0)), by_choice, tile(LANES),
                   by_choice, pl.BlockSpec((1, LANES), lambda i: (0, 0))],
        scratch_shapes=[pltpu.VMEM((1, LANES), F32)],
        compiler_params=_params(("arbitrary",)),
        name="merge_router",
    )(*merge_args, gain.reshape(1, D_MODEL), scale, shift, rw_hi, (rw_p - rw_hi.astype(F32)).astype(BF16), rb_p)


def _route_plan(top_i, rank, counts):
    experts = jnp.arange(N_EXPERTS, dtype=jnp.int32)
    padded = ((counts + MOE_TM - 1) // MOE_TM) * MOE_TM
    seg_end = jnp.cumsum(padded)
    seg_start = seg_end - padded
    pos = rank
    for e in range(N_EXPERTS - 1):
        pos = pos + jnp.where(top_i > e, padded[e], 0)
    n_active = seg_end[-1] // MOE_TM
    fill = jnp.concatenate([seg_start + counts, padded - counts, n_active[None]]).astype(jnp.int32)
    tile_start = jnp.arange(MOE_TILES, dtype=jnp.int32) * MOE_TM
    tile_expert = jnp.sum((seg_end[None, :] <= tile_start[:, None]).astype(jnp.int32), axis=1)
    last = jnp.sum((seg_end <= (n_active - 1) * MOE_TM).astype(jnp.int32))
    tile_expert = jnp.minimum(jnp.where(tile_start < seg_end[-1], tile_expert, last), N_EXPERTS - 1)
    owns = (padded > 0).astype(jnp.int32)
    run_of_expert = jnp.cumsum(owns) - 1
    run_expert = jnp.sum(jnp.where((run_of_expert[None, :] == experts[:, None]) & (owns[None, :] > 0),
                                   experts[None, :], 0), axis=1)
    runs = jnp.concatenate([run_expert, jnp.sum(owns)[None]]).astype(jnp.int32)
    tile_run = jnp.sum(jnp.where(tile_expert[:, None] == experts[None, :], run_of_expert[None, :], 0), axis=1)
    experts_plan = (tile_expert.astype(jnp.int32), n_active.reshape(1).astype(jnp.int32),
                    tile_run.astype(jnp.int32), runs)
    return pos.astype(jnp.int32), fill, experts_plan


DMA_UNROLL = 4
DMA_QUEUES = 2


def _wait_slabs(ref, n_slabs, sem):
    view = ref.at[pl.ds(0, n_slabs * SLAB)]
    pltpu.make_async_copy(view, view, sem).wait()


def _dispatch_kernel(pos_ref, fill_ref, h_hbm, xs_ref, inv_ref, hbuf, in_sems, sems):
    i = pl.program_id(0)
    base = i * (TM * TOP_K)
    sem = sems.at[i % 2]

    def tile_fetch(tile):
        src = h_hbm.at[pl.ds(pl.multiple_of(tile * (TM * SLAB), TM * SLAB), TM * SLAB)]
        return pltpu.make_async_copy(src, hbuf.at[tile % 3], in_sems.at[tile % 3])

    @pl.when(i == 0)
    def _():
        tile_fetch(0).start()

    tile_fetch(i).wait()

    @pl.when(i + 1 < N_TILES)
    def _():
        tile_fetch(i + 1).start()

    h_ref = hbuf.at[i % 3]

    def issue(j, carry):
        for u in range(DMA_UNROLL):
            t = j * DMA_UNROLL + u
            for k in range(TOP_K):
                a = base + k * TM + t
                slot = pos_ref[a]
                inv_ref[slot] = a
                pltpu.make_async_copy(_slab(h_ref, t), _slab(xs_ref, slot), sem).start(priority=k % DMA_QUEUES)
        return carry

    lax.fori_loop(0, TM // DMA_UNROLL, issue, 0)

    @pl.when(i == 0)
    def _():
        def per_expert(e, total):
            start = fill_ref[e]
            n = fill_ref[N_EXPERTS + e]

            def one(r, carry):
                slot = start + r
                inv_ref[slot] = N_ASSIGN + slot % MOE_TM
                pltpu.make_async_copy(_slab(h_ref, 0), _slab(xs_ref, slot), sem).start()
                return carry

            lax.fori_loop(0, n, one, 0)
            return total + n

        total = lax.fori_loop(0, N_EXPERTS, per_expert, 0)

        n_active = fill_ref[2 * N_EXPERTS]

        def unused_tile(ti, carry):
            dst = xs_ref.at[pl.ds(pl.multiple_of(ti * (MOE_TM * SLAB), MOE_TM * SLAB), MOE_TM * SLAB)]
            pltpu.make_async_copy(h_ref, dst, sem).start()
            return carry

        lax.fori_loop(n_active, MOE_TILES, unused_tile, 0)

        def unused_slot(slot, carry):
            inv_ref[slot] = N_ASSIGN
            return carry

        lax.fori_loop(n_active * MOE_TM, MOE_ROWS, unused_slot, 0)
        total = total + (MOE_TILES - n_active) * MOE_TM

        @pl.when(total > 0)
        def _():
            _wait_slabs(xs_ref, total, sem)

    @pl.when(i > 0)
    def _():
        _wait_slabs(xs_ref, TM * TOP_K, sems.at[(i + 1) % 2])

    @pl.when(i == N_TILES - 1)
    def _():
        _wait_slabs(xs_ref, TM * TOP_K, sem)


def _dispatch(h_slabs, pos, fill):
    grid_spec = pltpu.PrefetchScalarGridSpec(
        num_scalar_prefetch=2,
        grid=(N_TILES,),
        in_specs=[pl.BlockSpec(memory_space=pl.ANY)],
        out_specs=[pl.BlockSpec(memory_space=pl.ANY), pl.BlockSpec(memory_space=pltpu.SMEM)],
        scratch_shapes=[pltpu.VMEM((3, TM * SLAB, LANES), F32), pltpu.SemaphoreType.DMA((3,)),
                        pltpu.SemaphoreType.DMA((2,))],
    )
    return pl.pallas_call(
        _dispatch_kernel,
        out_shape=[jax.ShapeDtypeStruct((MOE_ROWS * SLAB, LANES), F32),
                   jax.ShapeDtypeStruct((MOE_ROWS,), jnp.int32)],
        grid_spec=grid_spec,
        compiler_params=_params(("arbitrary",)),
        name="moe_dispatch",
    )(pos, fill, h_slabs)


MOE_HALF = MOE_TM // 2


def _moe_kernel(te_ref, na_ref, ts_ref, ex_ref, inv_ref, x_ref, wgu_hbm, bgu_ref, wd_hbm, bd_ref, out_hbm,
                wgu_f32, wd_f32, wgu_bf, wd_bf, obuf0, obuf1, sems, osems, *, layer):
    i = pl.program_id(0)
    s = ts_ref[i]
    first = (i == 0) | (s != ts_ref[jnp.maximum(i - 1, 0)])
    n_active = na_ref[0]

    def start_rows(buf, half, tile):
        for r in range(MOE_HALF):
            dst = _slab(out_hbm, inv_ref[tile * MOE_TM + half * MOE_HALF + r])
            pltpu.make_async_copy(buf.at[pl.ds(r * SLAB, SLAB)], dst, osems.at[half]).start(
                priority=r % DMA_QUEUES)

    def wait_rows(half):
        _wait_slabs(out_hbm, MOE_HALF, osems.at[half])

    def ffn_tile(deferred):
        if deferred:
            start_rows(obuf1, 1, i - 1)
            wait_rows(0)
        x = _load_slabs(x_ref, MOE_TM).astype(BF16)
        gu = _dot(x, wgu_bf[...]) + bgu_ref[...]
        gate = jnp.minimum(gu[:, :D_FF], SWIGLU_LIMIT)
        up = jnp.clip(gu[:, D_FF:], -SWIGLU_LIMIT, SWIGLU_LIMIT)
        act = ((up + 1.0) * gate * jax.nn.sigmoid(SWIGLU_ALPHA * gate)).astype(BF16)
        o0 = _dot(act[:MOE_HALF], wd_bf[...]) + bd_ref[...]
        _store_slabs(obuf0, o0)
        start_rows(obuf0, 0, i)
        o1 = _dot(act[MOE_HALF:], wd_bf[...]) + bd_ref[...]
        if deferred:
            wait_rows(1)
        _store_slabs(obuf1, o1)

    def weight_copies(slot):
        e = ex_ref[slot]
        b = slot % 2
        return (pltpu.make_async_copy(wgu_hbm.at[layer, e], wgu_f32.at[b], sems.at[0, b]),
                pltpu.make_async_copy(wd_hbm.at[layer, e], wd_f32.at[b], sems.at[1, b]))

    @pl.when(i == 0)
    def _():
        for cp in weight_copies(0):
            cp.start(priority=1)

    @pl.when(first)
    def _():
        for cp in weight_copies(s):
            cp.wait()

        @pl.when(s + 1 < ex_ref[N_EXPERTS])
        def _():
            for cp in weight_copies(s + 1):
                cp.start(priority=1)

        b = s % 2
        wgu_bf[...] = wgu_f32[b].astype(BF16)
        wd_bf[...] = wd_f32[b].astype(BF16)

    @pl.when(i == 0)
    def _():
        ffn_tile(False)

    @pl.when((i > 0) & (i < n_active))
    def _():
        ffn_tile(True)

    @pl.when(i == n_active)
    def _():
        wait_rows(0)
        start_rows(obuf1, 1, i - 1)
        wait_rows(1)
        for half, buf in enumerate((obuf0, obuf1)):
            dst = out_hbm.at[pl.ds((N_ASSIGN + half * MOE_HALF) * SLAB, MOE_HALF * SLAB)]
            pltpu.make_async_copy(buf, dst, osems.at[half]).start()
        wait_rows(0)
        wait_rows(1)


def _moe_experts(layer, xs, plan, inv, w_gu, b_gu, w_down, b_down):
    const = lambda i, te, na, ts, ex, inv: (layer, te[i], 0, 0)
    grid_spec = pltpu.PrefetchScalarGridSpec(
        num_scalar_prefetch=5,
        grid=(MOE_TILES,),
        in_specs=[
            pl.BlockSpec((MOE_TM * SLAB, LANES), lambda i, te, na, ts, ex, inv: (jnp.minimum(i, na[0] - 1), 0)),
            pl.BlockSpec(memory_space=pl.ANY),
            pl.BlockSpec((None, None, 1, 2 * D_FF), const),
            pl.BlockSpec(memory_space=pl.ANY),
            pl.BlockSpec((None, None, 1, D_MODEL), const),
        ],
        out_specs=pl.BlockSpec(memory_space=pl.ANY),
        scratch_shapes=[pltpu.VMEM((2, D_MODEL, 2 * D_FF), F32), pltpu.VMEM((2, D_FF, D_MODEL), F32),
                        pltpu.VMEM((D_MODEL, 2 * D_FF), BF16), pltpu.VMEM((D_FF, D_MODEL), BF16),
                        pltpu.VMEM((MOE_HALF * SLAB, LANES), F32), pltpu.VMEM((MOE_HALF * SLAB, LANES), F32),
                        pltpu.SemaphoreType.DMA((2, 2)), pltpu.SemaphoreType.DMA((2,))],
    )
    return pl.pallas_call(
        functools.partial(_moe_kernel, layer=layer),
        out_shape=jax.ShapeDtypeStruct(((N_ASSIGN + MOE_TM) * SLAB, LANES), F32),
        grid_spec=grid_spec,
        compiler_params=_params(("arbitrary",)),
        name="moe_experts",
    )(*plan, inv, xs, w_gu, b_gu.reshape(DEPTH, N_EXPERTS, 1, 2 * D_FF), w_down,
      b_down.reshape(DEPTH, N_EXPERTS, 1, D_MODEL))


def _final_combine_kernel(out_ref, y_ref, tp_ref, g_ref, fg_ref, n_ctx_ref, n_lat_ref):
    chunks = _moe_combine_chunks(out_ref, y_ref, tp_ref, g_ref)
    ss = jnp.zeros((TM, 1), F32)
    for yc in chunks:
        ss = ss + jnp.sum(yc * yc, axis=-1, keepdims=True)
    inv = lax.rsqrt(ss * (1.0 / D_MODEL) + EPS)

    def store(n_ref):
        for c in range(SLAB):
            cs = slice(c * LANES, (c + 1) * LANES)
            n_ref[:, cs] = chunks[c] * inv * fg_ref[:, cs]

    @pl.when(_is_ctx_tile())
    def _():
        store(n_ctx_ref)

    @pl.when(jnp.logical_not(_is_ctx_tile()))
    def _():
        store(n_lat_ref)


def _final_combine(y, out_slabs, top_p, gate, final_gain):
    tile = pl.BlockSpec((TM, D_MODEL), lambda i: (i, 0))
    return pl.pallas_call(
        _final_combine_kernel,
        out_shape=[jax.ShapeDtypeStruct((T_CTX, D_MODEL), F32), jax.ShapeDtypeStruct((T_LAT, D_MODEL), F32)],
        grid=(N_TILES,),
        in_specs=[pl.BlockSpec((TOP_K * TM * SLAB, LANES), lambda i: (i, 0)), tile,
                  pl.BlockSpec((TM, LANES), lambda i: (i, 0)),
                  pl.BlockSpec((None, 1, D_MODEL), lambda i: (_mod_row(i), 0, 0)),
                  pl.BlockSpec((1, D_MODEL), lambda i: (0, 0))],
        out_specs=[pl.BlockSpec((TM, D_MODEL), lambda i: (jnp.minimum(i, CTX_TILES - 1), 0)),
                   pl.BlockSpec((TM, D_MODEL), lambda i: (jnp.maximum(i - CTX_TILES, 0), 0))],
        compiler_params=_params(("arbitrary",)),
        name="moe_combine",
    )(out_slabs, y, top_p, gate, final_gain.reshape(1, D_MODEL))


def _moe_experts_layer(layer, routed, w_gu, b_gu, w_down, b_down):
    h_slabs, top_i, top_p, rank, counts = routed
    pos, fill, experts_plan = _route_plan(top_i[:, :TOP_K].reshape(-1), rank[:, :TOP_K].reshape(-1),
                                          counts[0, :N_EXPERTS].astype(jnp.int32))
    xs, inv = _dispatch(h_slabs, pos, fill)
    return _moe_experts(layer, xs, experts_plan, inv, w_gu, b_gu, w_down, b_down), top_p


def kernel(x_prompt, x_sample, c, cache_a_k, cache_a_v, state_b_C, state_b_n, state_b_m, cache_c_k, cache_c_v, cache_d_k, cache_d_v, c_ctx, w_mod, b_mod, norm1_g, norm2_g, w_in_even, w_out_even, a_q_gain, a_k_gain, b_gate_bias, b_norm_gain, w_in_odd, w_out_odd, c_rpb, d_lambda, d_norm_gain, router_w, router_b, expert_w_gu, expert_b_gu, expert_w_down, expert_b_down, final_norm_g):
    y = (x_prompt.reshape(T_CTX, D_MODEL), x_sample.reshape(T_LAT, D_MODEL))
    cond = jnp.zeros((MOD_ROWS, D_MODEL), F32).at[0].set(c_ctx).at[1:1 + DEC_BATCH].set(c)
    mod = _modulation(cond, w_mod, b_mod).reshape(DEPTH, MOD_ROWS, 6, 1, D_MODEL)
    rope_cos, rope_sin = _rope_tables()
    scale = HD ** -0.5
    outs = {}

    for layer in range(DEPTH):
        sh1, sc1, g1, sh2, sc2, g2 = (mod[layer, :, k] for k in range(6))
        j = layer // 2
        if layer % 2 == 0:
            w = w_in_even[j]
            sizes = np.cumsum([0, 512, 128, 128, 512, 512, 512, 512, 16])
            aq, ak, av, bq, bk, bv, bo, bg = (w[:, sizes[k]:sizes[k + 1]] for k in range(8))
            w_in = jnp.concatenate([aq, bo, bq, bk, bv, ak, av, bg, jnp.zeros((D_MODEL, EV_N - EV_BG - 16), F32)],
                                   axis=1).astype(BF16)
            qg = jnp.tile(a_q_gain[j], LANES // HD).reshape(1, LANES)
            kg = jnp.tile(a_k_gain[j], LANES // HD).reshape(1, LANES)
            specs = ((EV_AQ, 512, 0, True, scale, BF16, None), (EV_AK, 128, 1, True, 1.0, BF16, None),
                     (EV_AV, 128, None, False, 1.0, BF16, None),
                     (EV_AK, 128, 1, False, 1.0, F32, A_KV), (EV_AV, 128, None, False, 1.0, F32, A_KV))
            y, p, (qa, ka, va, new_ak, new_av) = _project(y, norm1_g[layer], sc1, sh1, w_in, rope_cos, rope_sin,
                                                          [qg, kg], specs)
            oa_ctx = _attention(qa, ka, va, ctx=True, group=A_HEADS // A_KV, n_kv=A_KV)
            cache = (cache_a_k[:, j].astype(BF16), cache_a_v[:, j].astype(BF16))
            oa_lat = _attention(qa, ka, va, ctx=False, group=A_HEADS // A_KV, n_kv=A_KV, cache=cache)
            zc = jnp.zeros((BATCH, 2, B_HEADS, B_DK, B_DV), F32)
            zn = jnp.zeros((BATCH, 2, B_HEADS, 1, B_DK), F32)
            zm = jnp.zeros((BATCH, 2, B_HEADS, 1, LANES), F32)
            hb_ctx, bC, bn, bm = _mlstm(p, b_gate_bias[j], zc, zn, zm, ctx=True)
            m0 = jnp.broadcast_to(state_b_m[:, j][..., None, None], (DEC_BATCH, 2, B_HEADS, 1, LANES))
            hb_lat, _, _, _ = _mlstm(p, b_gate_bias[j], state_b_C[:, j], state_b_n[:, j][:, :, :, None, :], m0,
                                     ctx=False)
            y, *routed = _merge_router(y, g1, w_out_even[j].astype(BF16), (oa_ctx, oa_lat), (hb_ctx, hb_lat),
                                       norm2_g[layer], sc2, sh2, router_w[layer], router_b[layer],
                                       p=p, norm_gain=b_norm_gain[j])
            outs.setdefault("a_k", []).append(new_ak)
            outs.setdefault("a_v", []).append(new_av)
            outs.setdefault("b_C", []).append(bC)
            outs.setdefault("b_n", []).append(bn[:, :, :, 0, :])
            outs.setdefault("b_m", []).append(bm[:, :, :, 0, 0])
        else:
            specs = ((0, 512, None, False, scale, BF16, None), (512, 512, None, False, 1.0, BF16, None),
                     (1024, 512, None, False, 1.0, BF16, None), (1536, 512, None, True, scale, BF16, None),
                     (2048, 512, None, True, 1.0, BF16, None), (2560, 512, None, False, 1.0, BF16, None),
                     (512, 512, None, False, 1.0, F32, C_HEADS), (1024, 512, None, False, 1.0, F32, C_HEADS),
                     (2048, 512, None, False, 1.0, F32, 2 * D_HEADS), (2560, 512, None, False, 1.0, F32, D_HEADS))
            y, p, (qc, kc, vc, qd, kd, vd, new_ck, new_cv, new_dk, new_dv) = _project(
                y, norm1_g[layer], sc1, sh1, w_in_odd[j].astype(BF16), rope_cos, rope_sin, [], specs)
            lam_init = 0.8 - 0.6 * math.exp(-0.3 * layer)
            lp = d_lambda[j].astype(F32)
            lam = jnp.exp(jnp.sum(lp[0] * lp[1])) - jnp.exp(jnp.sum(lp[2] * lp[3])) + lam_init
            lam_vec = jnp.stack([lam, jnp.asarray(1.0 - lam_init, F32)]).astype(F32)
            diff = (lam_vec, d_norm_gain[j])
            oc_ctx = _attention(qc, kc, vc, ctx=True, n_kv=C_HEADS)
            od_ctx = _attention(qd, kd, vd, ctx=True, diff=diff)
            bias = _neighbourhood_bias(c_rpb[j])
            oc_lat = _attention(qc, kc, vc, ctx=False, n_kv=C_HEADS, bias=bias,
                                cache=(cache_c_k[:, j].astype(BF16), cache_c_v[:, j].astype(BF16)))
            kd_cache = cache_d_k[:, j].reshape(DEC_BATCH, 2 * D_HEADS, PAST_LEN, HD).astype(BF16)
            od_lat = _attention(qd, kd, vd, ctx=False, diff=diff, cache=(kd_cache, cache_d_v[:, j].astype(BF16)))
            y, *routed = _merge_router(y, g1, w_out_odd[j].astype(BF16), (oc_ctx, oc_lat), (od_ctx, od_lat),
                                       norm2_g[layer], sc2, sh2, router_w[layer], router_b[layer])
            outs.setdefault("c_k", []).append(new_ck)
            outs.setdefault("c_v", []).append(new_cv)
            outs.setdefault("d_k", []).append(new_dk.reshape(BATCH, D_HEADS, 2, SEQ, HD))
            outs.setdefault("d_v", []).append(new_dv)
        out_slabs, top_p = _moe_experts_layer(layer, routed, expert_w_gu, expert_b_gu, expert_w_down,
                                              expert_b_down)
        y = (y, out_slabs, top_p, g2)
    y_prompt, y_sample = _final_combine(*y, final_norm_g)
    stack = lambda k: jnp.stack(outs[k], axis=1)
    return (y_prompt.reshape(BATCH, SEQ, D_MODEL), y_sample.reshape(DEC_BATCH, DEC_SEQ, D_MODEL),
            stack("a_k"), stack("a_v"), stack("b_C"), stack("b_n"), stack("b_m"),
            stack("c_k"), stack("c_v"), stack("d_k"), stack("d_v"))
```

```python
import functools
import math

import numpy as np
import jax
import jax.numpy as jnp
from jax import lax
from jax.experimental import pallas as pl
from jax.experimental.pallas import tpu as pltpu

D_MODEL = 1024
BATCH = 32
SEQ = 256
DEPTH = 2
DEC_BATCH = 8
DEC_SEQ = 1024
PAST_LEN = 512
GRID_W = 64
HD = 64
A_HEADS = 8
A_KV = 2
B_HEADS = 4
B_DK = 128
B_DV = 128
B_CHUNK = 128
C_HEADS = 8
NA_ROWS = 8
NA_COLS = 16
D_HEADS = 4
D_VDIM = 2 * HD
N_EXPERTS = 32
TOP_K = 4
D_FF = 1024
SWIGLU_LIMIT = 7.0
SWIGLU_ALPHA = 1.702
ROPE_THETA = 10000.0
EPS = 1e-6

F32 = jnp.float32
BF16 = jnp.bfloat16
HIGHEST = lax.Precision.HIGHEST

T_CTX = BATCH * SEQ
T_LAT = DEC_BATCH * DEC_SEQ
T_ALL = T_CTX + T_LAT
TM = 256
CTX_TILES = T_CTX // TM
LAT_TILES_PER_BATCH = DEC_SEQ // TM
N_TILES = T_ALL // TM
MOD_ROWS = 16
LANES = 128
NEG_BIG = -1e30
MOE_TM = 256
N_ASSIGN = T_ALL * TOP_K
MOE_ROWS = N_ASSIGN + N_EXPERTS * MOE_TM
MOE_TILES = MOE_ROWS // MOE_TM
VMEM_LIMIT = 56 * 1024 * 1024
assert TM == SEQ and DEC_SEQ % TM == 0 and MOE_TM == TM

EV_AQ, EV_BO, EV_BQ, EV_BK, EV_BV, EV_AK, EV_AV, EV_BG = 0, 512, 1024, 1536, 2048, 2560, 2688, 2816
EV_N = 2944


def _params(sem, vmem=VMEM_LIMIT):
    return pltpu.CompilerParams(dimension_semantics=sem, vmem_limit_bytes=vmem)


def _mod_row(i):
    return jnp.where(i < CTX_TILES, 0, 1 + (i - CTX_TILES) // LAT_TILES_PER_BATCH)


def _rope_block(i):
    return jnp.where(i < CTX_TILES, LAT_TILES_PER_BATCH, (i - CTX_TILES) % LAT_TILES_PER_BATCH)


def _dot(a, b, precision=None):
    return jnp.dot(a, b, preferred_element_type=F32, precision=precision)


def _dot_nt(a, b):
    return lax.dot_general(a, b, (((1,), (1,)), ((), ())), preferred_element_type=F32)


def _dot_tn(a, b):
    return lax.dot_general(a, b, (((0,), (0,)), ((), ())), preferred_element_type=F32)


def _modulation_kernel(c_ref, w_ref, b_ref, o_ref):
    c = c_ref[...]
    s = c * jax.nn.sigmoid(c)
    o_ref[...] = _dot(s, w_ref[...], HIGHEST) + b_ref[...]


def _modulation(cond, w_mod, b_mod):
    tn = 1536
    return pl.pallas_call(
        _modulation_kernel,
        out_shape=jax.ShapeDtypeStruct((DEPTH, MOD_ROWS, 6 * D_MODEL), F32),
        grid=(DEPTH, 6 * D_MODEL // tn),
        in_specs=[
            pl.BlockSpec((MOD_ROWS, D_MODEL), lambda l, j: (0, 0)),
            pl.BlockSpec((None, D_MODEL, tn), lambda l, j: (l, 0, j)),
            pl.BlockSpec((None, 1, tn), lambda l, j: (l, 0, j)),
        ],
        out_specs=pl.BlockSpec((None, MOD_ROWS, tn), lambda l, j: (l, 0, j)),
        compiler_params=_params(("parallel", "parallel")),
        name="modulation",
    )(cond, w_mod, b_mod.reshape(DEPTH, 1, 6 * D_MODEL))


def _norm_mod(y, g, sc, sh):
    ms = jnp.mean(y * y, axis=-1, keepdims=True)
    return (y * lax.rsqrt(ms + EPS) * g) * (1.0 + sc) + sh


def _rope_rotate(x):
    w = x.shape[-1]
    lane = lax.broadcasted_iota(jnp.int32, x.shape, 1)
    nxt = pltpu.roll(x, w - 1, 1)
    prv = pltpu.roll(x, 1, 1)
    return jnp.where((lane & 1) == 0, -nxt, prv)


def _is_ctx_tile():
    return pl.program_id(0) < CTX_TILES


def _pair_specs(width):
    return [pl.BlockSpec((TM, width), lambda i: (jnp.minimum(i, CTX_TILES - 1), 0)),
            pl.BlockSpec((TM, width), lambda i: (jnp.maximum(i - CTX_TILES, 0), 0))]


def _pair_value(ctx_ref, lat_ref, cols=slice(None)):
    return jnp.where(_is_ctx_tile(), ctx_ref[:, cols], lat_ref[:, cols])


def _moe_combine_chunks(out_ref, y_ref, tp_ref, g_ref):
    tp = tp_ref[...]
    chunks = []
    for c in range(SLAB):
        cs = slice(c * LANES, (c + 1) * LANES)
        acc = tp[:, 0:1] * out_ref[pl.ds(c, TM, stride=SLAB), :]
        for k in range(1, TOP_K):
            acc = acc + tp[:, k:k + 1] * out_ref[pl.ds(k * TM * SLAB + c, TM, stride=SLAB), :]
        chunks.append(y_ref[:, cs] + g_ref[:, cs] * acc)
    return chunks


def _norm_proj_kernel(*refs, specs, n_gain, y_mode, keep_p):
    n_y = {"single": 1, "pair": 2, "combine": 4}[y_mode]
    g_ref, sc_ref, sh_ref, w_ref, cos_ref, sin_ref, bd_ref = refs[n_y:n_y + 7]
    gain_refs = refs[n_y + 7:n_y + 7 + n_gain]
    n_in = n_y + 7 + n_gain
    if keep_p:
        p_ref, out_refs = refs[n_in], refs[n_in + 1:]
    else:
        p_ref, out_refs = refs[-1], refs[n_in:-1]
    if y_mode == "combine":
        y = jnp.concatenate(_moe_combine_chunks(*refs[:4]), axis=1)
        out_refs[0][...] = y
        out_refs = out_refs[1:]
    elif y_mode == "pair":
        y = _pair_value(refs[0], refs[1])
    else:
        y = refs[0][...]
    h = _norm_mod(y, g_ref[...], sc_ref[...], sh_ref[...])
    p_ref[...] = _dot(h.astype(BF16), w_ref[...])
    cos = cos_ref[...]
    sin = sin_ref[...]
    for (col, width, gi, rope, scale, _, heads), o_ref in zip(specs, out_refs):
        for c0 in range(0, width, LANES):
            x = p_ref[:, col + c0:col + c0 + LANES]
            if gi is not None:
                xx = x * x
                xx_hi = xx.astype(BF16)
                xx_lo = (xx - xx_hi.astype(F32)).astype(BF16)
                ss = _dot(xx_hi, bd_ref[...]) + _dot(xx_lo, bd_ref[...])
                x = x * lax.rsqrt(ss * (1.0 / HD) + EPS) * gain_refs[gi][...]
            if rope:
                x = x * cos + _rope_rotate(x) * sin
            if scale != 1.0:
                x = x * scale
            if heads is None:
                o_ref[:, c0:c0 + LANES] = x.astype(o_ref.dtype)
            else:
                hw = width // heads
                per = LANES // hw

                @pl.when(_is_ctx_tile())
                def _(x=x, o_ref=o_ref, c0=c0, hw=hw, per=per):
                    for u in range(per):
                        o_ref[(c0 // LANES) * per + u] = x[:, u * hw:(u + 1) * hw].astype(o_ref.dtype)


def _norm_proj(y, gain, scale, shift, w_bf16, rope_cos, rope_sin, gains, specs, keep_p=True):
    n = w_bf16.shape[1]
    tile = pl.BlockSpec((TM, D_MODEL), lambda i: (i, 0))
    vec = pl.BlockSpec((None, 1, D_MODEL), lambda i: (_mod_row(i), 0, 0))
    if not isinstance(y, tuple):
        y_mode, ys, in_specs = "single", [y], [tile]
    elif len(y) == 2:
        y_mode, ys, in_specs = "pair", list(y), _pair_specs(D_MODEL)
    else:
        y_prev, out_slabs, top_p, gate = y
        y_mode, ys = "combine", [out_slabs, y_prev, top_p, gate]
        in_specs = [pl.BlockSpec((TOP_K * TM * SLAB, LANES), lambda i: (i, 0)), tile,
                    pl.BlockSpec((TM, LANES), lambda i: (i, 0)), vec]
    bd = jnp.asarray(np.kron(np.eye(LANES // HD), np.ones((HD, HD))), BF16)
    rope_spec = pl.BlockSpec((TM, LANES), lambda i: (_rope_block(i), 0))
    in_specs += [pl.BlockSpec((1, D_MODEL), lambda i: (0, 0)),
                 vec, vec, pl.BlockSpec((D_MODEL, n), lambda i: (0, 0)),
                 rope_spec, rope_spec, pl.BlockSpec((LANES, LANES), lambda i: (0, 0))]
    in_specs += [pl.BlockSpec((1, LANES), lambda i: (0, 0)) for _ in gains]
    out_shape = [jax.ShapeDtypeStruct((T_ALL, n), F32)] if keep_p else []
    out_specs = [pl.BlockSpec((TM, n), lambda i: (i, 0))] if keep_p else []
    if y_mode == "combine":
        out_shape.append(jax.ShapeDtypeStruct((T_ALL, D_MODEL), F32))
        out_specs.append(tile)
    for (_, width, _, _, _, dtype, heads) in specs:
        if heads is None:
            out_shape.append(jax.ShapeDtypeStruct((T_ALL, width), dtype))
            out_specs.append(pl.BlockSpec((TM, width), lambda i: (i, 0)))
        else:
            out_shape.append(jax.ShapeDtypeStruct((BATCH, heads, SEQ, width // heads), dtype))
            out_specs.append(pl.BlockSpec((None, heads, SEQ, width // heads),
                                          lambda i: (jnp.minimum(i, CTX_TILES - 1), 0, 0, 0)))
    return pl.pallas_call(
        functools.partial(_norm_proj_kernel, specs=specs, n_gain=len(gains), y_mode=y_mode, keep_p=keep_p),
        out_shape=out_shape,
        grid=(N_TILES,),
        in_specs=in_specs,
        out_specs=out_specs,
        scratch_shapes=[] if keep_p else [pltpu.VMEM((TM, n), F32)],
        compiler_params=_params(("arbitrary",)),
        name="norm_proj",
    )(*ys, gain.reshape(1, D_MODEL), scale, shift, w_bf16, rope_cos, rope_sin, bd, *gains)


def _project(y, *args, keep_p=True):
    res = list(_norm_proj(y, *args, keep_p=keep_p))
    p = res.pop(0) if keep_p else None
    if isinstance(y, tuple) and len(y) == 4:
        y = res.pop(0)
    return y, p, res


def _rope_tables():
    half = HD // 2
    freqs = 1.0 / (ROPE_THETA ** (jnp.arange(0, half, 2, dtype=F32) / half))
    t = jnp.arange(DEC_SEQ)
    rows = (t // GRID_W).astype(F32)
    cols = (t % GRID_W).astype(F32)
    ang = jnp.concatenate([rows[:, None] * freqs, cols[:, None] * freqs], axis=-1)
    cos = jnp.repeat(jnp.cos(ang), 2, axis=-1)
    sin = jnp.repeat(jnp.sin(ang), 2, axis=-1)
    cos = jnp.concatenate([jnp.tile(cos, (1, LANES // HD)), jnp.ones((TM, LANES), F32)], axis=0)
    sin = jnp.concatenate([jnp.tile(sin, (1, LANES // HD)), jnp.zeros((TM, LANES), F32)], axis=0)
    return cos, sin


def _lane_slice(ref, h, width=HD):
    per = LANES // width
    blk = ref[:, (h // per) * LANES:(h // per + 1) * LANES]
    if per == 1:
        return blk
    return blk[:, (h % per) * width:(h % per + 1) * width]


def _softmax_parts(scores):
    m = None
    for s in scores:
        ms = jnp.max(s, axis=-1, keepdims=True)
        m = ms if m is None else jnp.maximum(m, ms)
    ps = [jnp.exp(s - m) for s in scores]
    l = None
    for p in ps:
        ls = jnp.sum(p, axis=-1, keepdims=True)
        l = ls if l is None else l + ls
    return ps, l


def _attn_std_kernel(*refs, group, n_kv, has_cache, has_bias, bq):
    it = iter(refs)
    q_ref, kn_ref, vn_ref = next(it), next(it), next(it)
    kc_ref = vc_ref = b_ref = None
    if has_cache:
        kc_ref, vc_ref = next(it), next(it)
    if has_bias:
        b_ref = next(it)
    o_ref = next(it)
    outs = []
    for g in range(n_kv):
        qs = jnp.concatenate([_lane_slice(q_ref, g * group + j) for j in range(group)], axis=0)
        kn = _lane_slice(kn_ref, g)
        vn = _lane_slice(vn_ref, g)
        s_new = _dot_nt(qs, kn)
        if has_bias:
            s_new = s_new + b_ref[g]
        scores = [s_new]
        if has_cache:
            scores.append(_dot_nt(qs, kc_ref[g]))
        ps, l = _softmax_parts(scores)
        o = _dot(ps[0].astype(BF16), vn)
        if has_cache:
            o = o + _dot(ps[1].astype(BF16), vc_ref[g])
        o = o / l
        for j in range(group):
            outs.append(o[j * bq:(j + 1) * bq])
    o_ref[...] = jnp.concatenate(outs, axis=1).astype(o_ref.dtype)


def _attn_diff_kernel(*refs, has_cache):
    it = iter(refs)
    lam_ref, q_ref, kn_ref, vn_ref = next(it), next(it), next(it), next(it)
    kc_ref = vc_ref = None
    if has_cache:
        kc_ref, vc_ref = next(it), next(it)
    g_ref, o_ref = next(it), next(it)
    lam = lam_ref[0]
    post = lam_ref[1]
    outs = []
    for h in range(D_HEADS):
        pd_new, pd_c = None, None
        for j in range(2):
            f = 2 * h + j
            qs = _lane_slice(q_ref, f)
            scores = [_dot_nt(qs, _lane_slice(kn_ref, f))]
            if has_cache:
                scores.append(_dot_nt(qs, kc_ref[f]))
            ps, l = _softmax_parts(scores)
            r = 1.0 / l
            if j == 0:
                pd_new = ps[0] * r
                pd_c = ps[1] * r if has_cache else None
            else:
                r = r * lam
                pd_new = pd_new - ps[0] * r
                pd_c = pd_c - ps[1] * r if has_cache else None
        o = _dot(pd_new.astype(BF16), _lane_slice(vn_ref, h, D_VDIM))
        if has_cache:
            o = o + _dot(pd_c.astype(BF16), vc_ref[h])
        ms = jnp.mean(o * o, axis=-1, keepdims=True)
        outs.append(o * lax.rsqrt(ms + EPS) * g_ref[...] * post)
    o_ref[...] = jnp.concatenate(outs, axis=1).astype(o_ref.dtype)


def _attention(q, kn, vn, *, ctx, group=1, n_kv=1, cache=None, bias=None, diff=None, bq=256):
    if ctx:
        nb, sq, row0 = BATCH, SEQ, 0
    else:
        nb, sq, row0 = DEC_BATCH, DEC_SEQ, T_CTX
    nq = sq // bq
    qb0 = row0 // bq
    kb0 = row0 // sq
    wq, wk, wv = q.shape[1], kn.shape[1], vn.shape[1]
    in_specs = [
        pl.BlockSpec((bq, wq), lambda b, i: (qb0 + b * nq + i, 0)),
        pl.BlockSpec((sq, wk), lambda b, i: (kb0 + b, 0)),
        pl.BlockSpec((sq, wv), lambda b, i: (kb0 + b, 0)),
    ]
    args = [q, kn, vn]
    if cache is not None:
        kc, vc = cache
        in_specs += [pl.BlockSpec((None,) + kc.shape[1:], lambda b, i: (b, 0, 0, 0)),
                     pl.BlockSpec((None,) + vc.shape[1:], lambda b, i: (b, 0, 0, 0))]
        args += [kc, vc]
    if diff is None:
        if bias is not None:
            in_specs.append(pl.BlockSpec((bias.shape[0], bq, sq), lambda b, i: (0, i, 0)))
            args.append(bias)
        body = functools.partial(_attn_std_kernel, group=group, n_kv=n_kv, has_cache=cache is not None,
                                 has_bias=bias is not None, bq=bq)
    else:
        lam_vec, gain = diff
        in_specs = [pl.BlockSpec(memory_space=pltpu.SMEM)] + in_specs
        args = [lam_vec] + args
        in_specs.append(pl.BlockSpec((1, D_VDIM), lambda b, i: (0, 0)))
        args.append(gain.reshape(1, D_VDIM))
        body = functools.partial(_attn_diff_kernel, has_cache=cache is not None)
    return pl.pallas_call(
        body,
        out_shape=jax.ShapeDtypeStruct((nb * sq, 512), BF16),
        grid=(nb, nq),
        in_specs=in_specs,
        out_specs=pl.BlockSpec((bq, 512), lambda b, i: (b * nq + i, 0)),
        compiler_params=_params(("parallel", "parallel")),
        name="attention",
    )(*args)


GRID_ROWS = DEC_SEQ // GRID_W
NA_WIN_ROWS = min(NA_ROWS, GRID_ROWS)


def _na_bias_kernel(t_ref, o_ref):
    outside = jnp.full((GRID_W, GRID_W), NEG_BIG, F32)
    for qr in range(GRID_ROWS):
        r0 = min(max(qr - NA_WIN_ROWS // 2, 0), GRID_ROWS - NA_WIN_ROWS)
        parts = [t_ref[kr - qr + (NA_ROWS - 1)] if r0 <= kr < r0 + NA_WIN_ROWS else outside
                 for kr in range(GRID_ROWS)]
        o_ref[qr * GRID_W:(qr + 1) * GRID_W, :] = jnp.concatenate(parts, axis=1)


def _neighbourhood_bias(rpb):
    c = np.arange(GRID_W)
    c0 = np.clip(c - NA_COLS // 2, 0, GRID_W - NA_COLS)
    col_ok = (c[None, :] >= c0[:, None]) & (c[None, :] < c0[:, None] + NA_COLS)
    dc = np.clip(c[None, :] - c[:, None], 1 - NA_COLS, NA_COLS - 1) + (NA_COLS - 1)
    oh_c = jnp.asarray(dc[..., None] == np.arange(2 * NA_COLS - 1), F32)
    by_col = jnp.einsum("hrd,qkd->hrqk", rpb.astype(F32), oh_c, precision=HIGHEST)
    by_col = jnp.where(jnp.asarray(col_ok), by_col, NEG_BIG)
    n_dr = 2 * NA_ROWS - 1
    return pl.pallas_call(
        _na_bias_kernel,
        out_shape=jax.ShapeDtypeStruct((C_HEADS, DEC_SEQ, DEC_SEQ), F32),
        grid=(C_HEADS,),
        in_specs=[pl.BlockSpec((None, n_dr, GRID_W, GRID_W), lambda h: (h, 0, 0, 0))],
        out_specs=pl.BlockSpec((None, DEC_SEQ, DEC_SEQ), lambda h: (h, 0, 0)),
        compiler_params=_params(("parallel",)),
        name="na_bias",
    )(by_col)


def _log_sigmoid(x):
    return jnp.minimum(x, 0.0) - jnp.log1p(jnp.exp(-jnp.abs(x)))


def _mlstm_kernel(q_ref, k_ref, v_ref, g_ref, gb_ref, c0_ref, n0_ref, m0_ref, h_ref, c_ref, n_ref, m_ref, hb_ref,
                  *, seq):
    L = B_CHUNK
    nc = seq // L
    row = lax.broadcasted_iota(jnp.int32, (L, L), 0)
    col = lax.broadcasted_iota(jnp.int32, (L, L), 1)
    keeps = (col <= row, col >= row)
    k_scale = B_DK ** -0.5
    c_ref[...] = c0_ref[...]
    n_ref[...] = n0_ref[...]
    m_ref[...] = m0_ref[...]

    def step(j):
        for d in range(2):
            keep = keeps[d]
            c = j if d == 0 else nc - 1 - j
            off = c * L if isinstance(c, int) else pl.multiple_of(c * L, L)
            gates = g_ref[pl.ds(off, L), :] + gb_ref[...]
            cum = _dot(keep.astype(F32), _log_sigmoid(gates), HIGHEST)
            cum_t = cum.T
            gates_t = gates.T
            out_ref = h_ref if d == 0 else hb_ref
            for h in range(B_HEADS):
                ci = (2 * d) * B_HEADS + h
                cf = (2 * d + 1) * B_HEADS + h
                hs = slice(h * B_DK, (h + 1) * B_DK)
                C = c_ref[d, h]
                n = n_ref[d, h]
                m = m_ref[d, h][:, 0:1]
                qc = q_ref[pl.ds(off, L), hs]
                kc = k_ref[pl.ds(off, L), hs] * k_scale
                vc = v_ref[pl.ds(off, L), hs]
                b_col = cum[:, cf:cf + 1]
                i_col = gates[:, ci:ci + 1]
                b_row = cum_t[cf:cf + 1, :]
                i_row = gates_t[ci:ci + 1, :]
                dlog = jnp.where(keep, b_col - b_row + i_row, -jnp.inf)
                inter = b_col + m
                m_t = jnp.maximum(inter, jnp.max(dlog, axis=-1, keepdims=True))
                w_intra = jnp.exp(dlog - m_t)
                w_inter = jnp.exp(inter - m_t)
                qb = qc.astype(BF16)
                vb = vc.astype(BF16)
                qk = _dot_nt(qb, kc.astype(BF16)) * w_intra
                num = _dot(qk.astype(BF16), vb) + w_inter * _dot(qb, C.astype(BF16))
                den = jnp.sum(qk, axis=-1, keepdims=True) + w_inter * jnp.sum(qc * n, axis=-1, keepdims=True)
                out_ref[pl.ds(off, L), hs] = num / jnp.maximum(jnp.abs(den), jnp.exp(-m_t))
                b_last = b_col[L - 1:L, :] if d == 0 else b_col[0:1, :]
                end_col = b_last - b_col + i_col
                m_new = jnp.maximum(b_last + m, jnp.max(end_col, axis=0, keepdims=True))
                w_end = jnp.exp(end_col - m_new)
                decay = jnp.exp(b_last + m - m_new)
                kw = kc * w_end
                c_ref[d, h] = decay * C + _dot_tn(kw.astype(BF16), vb)
                n_ref[d, h] = decay * n + jnp.sum(kw, axis=0, keepdims=True)
                m_ref[d, h] = jnp.broadcast_to(m_new, (1, LANES))

    if nc <= 2:
        for j in range(nc):
            step(j)
    else:
        def body(j, carry):
            step(j)
            return carry

        lax.fori_loop(0, nc, body, 0)
    h_ref[...] = h_ref[...] + hb_ref[...]


def _mlstm(p, gate_bias, c0, n0, m0, *, ctx):
    if ctx:
        nb, seq, blk0 = BATCH, SEQ, 0
    else:
        nb, seq, blk0 = DEC_BATCH, DEC_SEQ, T_CTX // DEC_SEQ
    w = B_HEADS * B_DK

    def cols(c0_, width):
        return pl.BlockSpec((seq, width), lambda b: (blk0 + b, c0_ // width))

    gb = jnp.zeros((1, LANES), F32).at[0, :4 * B_HEADS].set(gate_bias.reshape(-1).astype(F32))
    st = lambda shape: pl.BlockSpec((None,) + shape, lambda b: (b, 0, 0, 0, 0))
    return pl.pallas_call(
        functools.partial(_mlstm_kernel, seq=seq),
        out_shape=[
            jax.ShapeDtypeStruct((nb * seq, w), F32),
            jax.ShapeDtypeStruct((nb, 2, B_HEADS, B_DK, B_DV), F32),
            jax.ShapeDtypeStruct((nb, 2, B_HEADS, 1, B_DK), F32),
            jax.ShapeDtypeStruct((nb, 2, B_HEADS, 1, LANES), F32),
        ],
        grid=(nb,),
        in_specs=[
            cols(EV_BQ, w), cols(EV_BK, w), cols(EV_BV, w), cols(EV_BG, LANES),
            pl.BlockSpec((1, LANES), lambda b: (0, 0)),
            st((2, B_HEADS, B_DK, B_DV)), st((2, B_HEADS, 1, B_DK)), st((2, B_HEADS, 1, LANES)),
        ],
        out_specs=[
            pl.BlockSpec((seq, w), lambda b: (b, 0)),
            st((2, B_HEADS, B_DK, B_DV)), st((2, B_HEADS, 1, B_DK)), st((2, B_HEADS, 1, LANES)),
        ],
        scratch_shapes=[pltpu.VMEM((seq, w), F32)],
        compiler_params=_params(("parallel",)),
        name="mlstm",
    )(p, p, p, p, gb, c0, n0, m0)


def _merge_value(refs, even, y_pair):
    a_ctx, a_lat, b_ctx, b_lat = refs[:4]
    rest = refs[4:]
    if even:
        bo_ref, ng_ref = rest[:2]
        rest = rest[2:]
        parts = [_pair_value(a_ctx, a_lat)]
        for h in range(B_HEADS):
            hs = slice(h * B_DV, (h + 1) * B_DV)
            x = _pair_value(b_ctx, b_lat, hs)
            ms = jnp.mean(x * x, axis=-1, keepdims=True)
            xn = x * lax.rsqrt(ms + EPS) * ng_ref[:, hs]
            parts.append((jax.nn.sigmoid(bo_ref[:, hs]) * xn).astype(BF16))
    else:
        parts = [_pair_value(a_ctx, a_lat), _pair_value(b_ctx, b_lat)]
    w_ref = rest[0]
    y = _pair_value(rest[1], rest[2]) if y_pair else rest[1][...]
    g_ref = rest[-1]
    cat = jnp.concatenate(parts, axis=1)
    return y + g_ref[...] * _dot(cat, w_ref[...])


def _merge_specs(y, gate, w_bf16, a, b, p, norm_gain):
    in_specs = _pair_specs(512) + _pair_specs(512)
    args = [*a, *b]
    if p is not None:
        in_specs += [pl.BlockSpec((TM, 512), lambda i: (i, EV_BO // 512)), pl.BlockSpec((1, 512), lambda i: (0, 0))]
        args += [p, norm_gain.reshape(1, 512)]
    in_specs.append(pl.BlockSpec((D_MODEL, D_MODEL), lambda i: (0, 0)))
    args.append(w_bf16)
    if isinstance(y, tuple):
        in_specs += _pair_specs(D_MODEL)
        args += list(y)
    else:
        in_specs.append(pl.BlockSpec((TM, D_MODEL), lambda i: (i, 0)))
        args.append(y)
    in_specs.append(pl.BlockSpec((None, 1, D_MODEL), lambda i: (_mod_row(i), 0, 0)))
    args.append(gate)
    return in_specs, args


SLAB = D_MODEL // LANES


def _load_slabs(ref, rows):
    return jnp.concatenate([ref[pl.ds(c, rows, stride=SLAB), :] for c in range(SLAB)], axis=1)


def _store_slabs(ref, x):
    for c in range(SLAB):
        ref[pl.ds(c, x.shape[0], stride=SLAB), :] = x[:, c * LANES:(c + 1) * LANES]


def _slab(ref, idx):
    return ref.at[pl.ds(pl.multiple_of(idx * SLAB, SLAB), SLAB)]


def _merge_router_kernel(*refs, even, y_pair):
    n_merge = 4 + (2 if even else 0) + 1 + (2 if y_pair else 1) + 1
    merge_refs = refs[:n_merge]
    (g_ref, sc_ref, sh_ref, whi_ref, wlo_ref, b_ref,
     y_out_ref, h_ref, ti_ref, tp_ref, rk_ref, cnt_ref, base_ref) = refs[n_merge:]

    @pl.when(pl.program_id(0) == 0)
    def _():
        base_ref[...] = jnp.zeros(base_ref.shape, F32)

    y = _merge_value(merge_refs, even, y_pair)
    y_out_ref[...] = y
    h = _norm_mod(y, g_ref[...], sc_ref[...], sh_ref[...])
    _store_slabs(h_ref, h)
    h_hi = h.astype(BF16)
    h_lo = (h - h_hi.astype(F32)).astype(BF16)
    logits = (_dot(h_hi, whi_ref[...]) + (_dot(h_hi, wlo_ref[...]) + _dot(h_lo, whi_ref[...]))
              + b_ref[...])
    lane = lax.broadcasted_iota(jnp.int32, logits.shape, 1)
    lane_f = lane.astype(F32)
    vals, idxs = [], []
    for _ in range(TOP_K):
        mx = jnp.max(logits, axis=-1, keepdims=True)
        ix = jnp.min(jnp.where(logits == mx, lane_f, float(LANES)), axis=-1, keepdims=True)
        vals.append(mx)
        idxs.append(ix)
        logits = jnp.where(lane_f == ix, -jnp.inf, logits)
    es = [jnp.exp(v - vals[0]) for v in vals]
    tot = es[0] + es[1] + es[2] + es[3]
    ti = jnp.zeros(logits.shape, F32)
    tp = jnp.zeros(logits.shape, F32)
    for k in range(TOP_K):
        ti = jnp.where(lane == k, idxs[k], ti)
        tp = jnp.where(lane == k, es[k] / tot, tp)
    ti_ref[...] = ti.T[0:8, :].astype(jnp.int32)
    tp_ref[...] = tp
    onehots = [(lane_f == ix).astype(F32) for ix in idxs]
    cnt = onehots[0] + onehots[1] + onehots[2] + onehots[3]
    row = lax.broadcasted_iota(jnp.int32, (TM, TM), 0)
    col = lax.broadcasted_iota(jnp.int32, (TM, TM), 1)
    before = _dot((col < row).astype(BF16), cnt.astype(BF16)) + base_ref[...]
    rk = jnp.zeros(logits.shape, F32)
    for k in range(TOP_K):
        rk = jnp.where(lane == k, jnp.sum(onehots[k] * before, axis=-1, keepdims=True), rk)
    rk_ref[...] = rk.T[0:8, :].astype(jnp.int32)
    base_ref[...] = base_ref[...] + jnp.sum(cnt, axis=0, keepdims=True)
    cnt_ref[...] = base_ref[...]


def _merge_router(y, gate, w_out_bf16, a, b, gain, scale, shift, rw, rb, *, p=None, norm_gain=None):
    merge_in_specs, merge_args = _merge_specs(y, gate, w_out_bf16, a, b, p, norm_gain)
    vec = pl.BlockSpec((None, 1, D_MODEL), lambda i: (_mod_row(i), 0, 0))
    rw_p = jnp.zeros((D_MODEL, LANES), F32).at[:, :N_EXPERTS].set(rw)
    rb_p = jnp.full((1, LANES), NEG_BIG, F32).at[0, :N_EXPERTS].set(rb)
    rw_hi = rw_p.astype(BF16)
    tile = lambda w: pl.BlockSpec((TM, w), lambda i: (i, 0))
    by_choice = pl.BlockSpec((None, 8, TM), lambda i: (i, 0, 0))
    return pl.pallas_call(
        functools.partial(_merge_router_kernel, even=p is not None, y_pair=isinstance(y, tuple)),
        out_shape=[jax.ShapeDtypeStruct((T_ALL, D_MODEL), F32),
                   jax.ShapeDtypeStruct((T_ALL * SLAB, LANES), F32),
                   jax.ShapeDtypeStruct((N_TILES, 8, TM), jnp.int32),
                   jax.ShapeDtypeStruct((T_ALL, LANES), F32),
                   jax.ShapeDtypeStruct((N_TILES, 8, TM), jnp.int32),
                   jax.ShapeDtypeStruct((1, LANES), F32)],
        grid=(N_TILES,),
        in_specs=merge_in_specs + [
            pl.BlockSpec((1, D_MODEL), lambda i: (0, 0)), vec, vec,
            pl.BlockSpec((D_MODEL, LANES), lambda i: (0, 0)), pl.BlockSpec((D_MODEL, LANES), lambda i: (0, 0)),
            pl.BlockSpec((1, LANES), lambda i: (0, 0))],
        out_specs=[tile(D_MODEL), pl.BlockSpec((TM * SLAB, LANES), lambda i: (i, 0)), by_choice, tile(LANES),
                   by_choice, pl.BlockSpec((1, LANES), lambda i: (0, 0))],
        scratch_shapes=[pltpu.VMEM((1, LANES), F32)],
        compiler_params=_params(("arbitrary",)),
        name="merge_router",
    )(*merge_args, gain.reshape(1, D_MODEL), scale, shift, rw_hi, (rw_p - rw_hi.astype(F32)).astype(BF16), rb_p)


def _route_plan(top_i, rank, counts):
    experts = jnp.arange(N_EXPERTS, dtype=jnp.int32)
    padded = ((counts + MOE_TM - 1) // MOE_TM) * MOE_TM
    seg_end = jnp.cumsum(padded)
    seg_start = seg_end - padded
    pos = rank
    for e in range(N_EXPERTS - 1):
        pos = pos + jnp.where(top_i > e, padded[e], 0)
    n_active = seg_end[-1] // MOE_TM
    fill = jnp.concatenate([seg_start + counts, padded - counts, n_active[None]]).astype(jnp.int32)
    tile_start = jnp.arange(MOE_TILES, dtype=jnp.int32) * MOE_TM
    tile_expert = jnp.sum((seg_end[None, :] <= tile_start[:, None]).astype(jnp.int32), axis=1)
    last = jnp.sum((seg_end <= (n_active - 1) * MOE_TM).astype(jnp.int32))
    tile_expert = jnp.minimum(jnp.where(tile_start < seg_end[-1], tile_expert, last), N_EXPERTS - 1)
    owns = (padded > 0).astype(jnp.int32)
    run_of_expert = jnp.cumsum(owns) - 1
    run_expert = jnp.sum(jnp.where((run_of_expert[None, :] == experts[:, None]) & (owns[None, :] > 0),
                                   experts[None, :], 0), axis=1)
    runs = jnp.concatenate([run_expert, jnp.sum(owns)[None]]).astype(jnp.int32)
    tile_run = jnp.sum(jnp.where(tile_expert[:, None] == experts[None, :], run_of_expert[None, :], 0), axis=1)
    experts_plan = (tile_expert.astype(jnp.int32), n_active.reshape(1).astype(jnp.int32),
                    tile_run.astype(jnp.int32), runs)
    return pos.astype(jnp.int32), fill, experts_plan


DMA_UNROLL = 4
DMA_QUEUES = 2


def _wait_slabs(ref, n_slabs, sem):
    view = ref.at[pl.ds(0, n_slabs * SLAB)]
    pltpu.make_async_copy(view, view, sem).wait()


def _dispatch_kernel(pos_ref, fill_ref, h_hbm, xs_ref, inv_ref, hbuf, in_sems, sems):
    i = pl.program_id(0)
    base = i * (TM * TOP_K)
    sem = sems.at[i % 2]

    def tile_fetch(tile):
        src = h_hbm.at[pl.ds(pl.multiple_of(tile * (TM * SLAB), TM * SLAB), TM * SLAB)]
        return pltpu.make_async_copy(src, hbuf.at[tile % 3], in_sems.at[tile % 3])

    @pl.when(i == 0)
    def _():
        tile_fetch(0).start()

    tile_fetch(i).wait()

    @pl.when(i + 1 < N_TILES)
    def _():
        tile_fetch(i + 1).start()

    h_ref = hbuf.at[i % 3]

    def issue(j, carry):
        for u in range(DMA_UNROLL):
            t = j * DMA_UNROLL + u
            for k in range(TOP_K):
                a = base + k * TM + t
                slot = pos_ref[a]
                inv_ref[slot] = a
                pltpu.make_async_copy(_slab(h_ref, t), _slab(xs_ref, slot), sem).start(priority=k % DMA_QUEUES)
        return carry

    lax.fori_loop(0, TM // DMA_UNROLL, issue, 0)

    @pl.when(i == 0)
    def _():
        def per_expert(e, total):
            start = fill_ref[e]
            n = fill_ref[N_EXPERTS + e]

            def one(r, carry):
                slot = start + r
                inv_ref[slot] = N_ASSIGN + slot % MOE_TM
                pltpu.make_async_copy(_slab(h_ref, 0), _slab(xs_ref, slot), sem).start()
                return carry

            lax.fori_loop(0, n, one, 0)
            return total + n

        total = lax.fori_loop(0, N_EXPERTS, per_expert, 0)

        n_active = fill_ref[2 * N_EXPERTS]

        def unused_tile(ti, carry):
            dst = xs_ref.at[pl.ds(pl.multiple_of(ti * (MOE_TM * SLAB), MOE_TM * SLAB), MOE_TM * SLAB)]
            pltpu.make_async_copy(h_ref, dst, sem).start()
            return carry

        lax.fori_loop(n_active, MOE_TILES, unused_tile, 0)

        def unused_slot(slot, carry):
            inv_ref[slot] = N_ASSIGN
            return carry

        lax.fori_loop(n_active * MOE_TM, MOE_ROWS, unused_slot, 0)
        total = total + (MOE_TILES - n_active) * MOE_TM

        @pl.when(total > 0)
        def _():
            _wait_slabs(xs_ref, total, sem)

    @pl.when(i > 0)
    def _():
        _wait_slabs(xs_ref, TM * TOP_K, sems.at[(i + 1) % 2])

    @pl.when(i == N_TILES - 1)
    def _():
        _wait_slabs(xs_ref, TM * TOP_K, sem)


def _dispatch(h_slabs, pos, fill):
    grid_spec = pltpu.PrefetchScalarGridSpec(
        num_scalar_prefetch=2,
        grid=(N_TILES,),
        in_specs=[pl.BlockSpec(memory_space=pl.ANY)],
        out_specs=[pl.BlockSpec(memory_space=pl.ANY), pl.BlockSpec(memory_space=pltpu.SMEM)],
        scratch_shapes=[pltpu.VMEM((3, TM * SLAB, LANES), F32), pltpu.SemaphoreType.DMA((3,)),
                        pltpu.SemaphoreType.DMA((2,))],
    )
    return pl.pallas_call(
        _dispatch_kernel,
        out_shape=[jax.ShapeDtypeStruct((MOE_ROWS * SLAB, LANES), F32),
                   jax.ShapeDtypeStruct((MOE_ROWS,), jnp.int32)],
        grid_spec=grid_spec,
        compiler_params=_params(("arbitrary",)),
        name="moe_dispatch",
    )(pos, fill, h_slabs)


MOE_HALF = MOE_TM // 2


def _moe_kernel(te_ref, na_ref, ts_ref, ex_ref, inv_ref, x_ref, wgu_hbm, bgu_ref, wd_hbm, bd_ref, out_hbm,
                wgu_f32, wd_f32, wgu_bf, wd_bf, obuf0, obuf1, sems, osems, *, layer):
    i = pl.program_id(0)
    s = ts_ref[i]
    first = (i == 0) | (s != ts_ref[jnp.maximum(i - 1, 0)])
    n_active = na_ref[0]

    def start_rows(buf, half, tile):
        for r in range(MOE_HALF):
            dst = _slab(out_hbm, inv_ref[tile * MOE_TM + half * MOE_HALF + r])
            pltpu.make_async_copy(buf.at[pl.ds(r * SLAB, SLAB)], dst, osems.at[half]).start(
                priority=r % DMA_QUEUES)

    def wait_rows(half):
        _wait_slabs(out_hbm, MOE_HALF, osems.at[half])

    def ffn_tile(deferred):
        if deferred:
            start_rows(obuf1, 1, i - 1)
            wait_rows(0)
        x = _load_slabs(x_ref, MOE_TM).astype(BF16)
        gu = _dot(x, wgu_bf[...]) + bgu_ref[...]
        gate = jnp.minimum(gu[:, :D_FF], SWIGLU_LIMIT)
        up = jnp.clip(gu[:, D_FF:], -SWIGLU_LIMIT, SWIGLU_LIMIT)
        act = ((up + 1.0) * gate * jax.nn.sigmoid(SWIGLU_ALPHA * gate)).astype(BF16)
        o0 = _dot(act[:MOE_HALF], wd_bf[...]) + bd_ref[...]
        _store_slabs(obuf0, o0)
        start_rows(obuf0, 0, i)
        o1 = _dot(act[MOE_HALF:], wd_bf[...]) + bd_ref[...]
        if deferred:
            wait_rows(1)
        _store_slabs(obuf1, o1)

    def weight_copies(slot):
        e = ex_ref[slot]
        b = slot % 2
        return (pltpu.make_async_copy(wgu_hbm.at[layer, e], wgu_f32.at[b], sems.at[0, b]),
                pltpu.make_async_copy(wd_hbm.at[layer, e], wd_f32.at[b], sems.at[1, b]))

    @pl.when(i == 0)
    def _():
        for cp in weight_copies(0):
            cp.start(priority=1)

    @pl.when(first)
    def _():
        for cp in weight_copies(s):
            cp.wait()

        @pl.when(s + 1 < ex_ref[N_EXPERTS])
        def _():
            for cp in weight_copies(s + 1):
                cp.start(priority=1)

        b = s % 2
        wgu_bf[...] = wgu_f32[b].astype(BF16)
        wd_bf[...] = wd_f32[b].astype(BF16)

    @pl.when(i == 0)
    def _():
        ffn_tile(False)

    @pl.when((i > 0) & (i < n_active))
    def _():
        ffn_tile(True)

    @pl.when(i == n_active)
    def _():
        wait_rows(0)
        start_rows(obuf1, 1, i - 1)
        wait_rows(1)
        for half, buf in enumerate((obuf0, obuf1)):
            dst = out_hbm.at[pl.ds((N_ASSIGN + half * MOE_HALF) * SLAB, MOE_HALF * SLAB)]
            pltpu.make_async_copy(buf, dst, osems.at[half]).start()
        wait_rows(0)
        wait_rows(1)


def _moe_experts(layer, xs, plan, inv, w_gu, b_gu, w_down, b_down):
    const = lambda i, te, na, ts, ex, inv: (layer, te[i], 0, 0)
    grid_spec = pltpu.PrefetchScalarGridSpec(
        num_scalar_prefetch=5,
        grid=(MOE_TILES,),
        in_specs=[
            pl.BlockSpec((MOE_TM * SLAB, LANES), lambda i, te, na, ts, ex, inv: (jnp.minimum(i, na[0] - 1), 0)),
            pl.BlockSpec(memory_space=pl.ANY),
            pl.BlockSpec((None, None, 1, 2 * D_FF), const),
            pl.BlockSpec(memory_space=pl.ANY),
            pl.BlockSpec((None, None, 1, D_MODEL), const),
        ],
        out_specs=pl.BlockSpec(memory_space=pl.ANY),
        scratch_shapes=[pltpu.VMEM((2, D_MODEL, 2 * D_FF), F32), pltpu.VMEM((2, D_FF, D_MODEL), F32),
                        pltpu.VMEM((D_MODEL, 2 * D_FF), BF16), pltpu.VMEM((D_FF, D_MODEL), BF16),
                        pltpu.VMEM((MOE_HALF * SLAB, LANES), F32), pltpu.VMEM((MOE_HALF * SLAB, LANES), F32),
                        pltpu.SemaphoreType.DMA((2, 2)), pltpu.SemaphoreType.DMA((2,))],
    )
    return pl.pallas_call(
        functools.partial(_moe_kernel, layer=layer),
        out_shape=jax.ShapeDtypeStruct(((N_ASSIGN + MOE_TM) * SLAB, LANES), F32),
        grid_spec=grid_spec,
        compiler_params=_params(("arbitrary",)),
        name="moe_experts",
    )(*plan, inv, xs, w_gu, b_gu.reshape(DEPTH, N_EXPERTS, 1, 2 * D_FF), w_down,
      b_down.reshape(DEPTH, N_EXPERTS, 1, D_MODEL))


def _final_combine_kernel(out_ref, y_ref, tp_ref, g_ref, fg_ref, n_ctx_ref, n_lat_ref):
    chunks = _moe_combine_chunks(out_ref, y_ref, tp_ref, g_ref)
    ss = jnp.zeros((TM, 1), F32)
    for yc in chunks:
        ss = ss + jnp.sum(yc * yc, axis=-1, keepdims=True)
    inv = lax.rsqrt(ss * (1.0 / D_MODEL) + EPS)

    def store(n_ref):
        for c in range(SLAB):
            cs = slice(c * LANES, (c + 1) * LANES)
            n_ref[:, cs] = chunks[c] * inv * fg_ref[:, cs]

    @pl.when(_is_ctx_tile())
    def _():
        store(n_ctx_ref)

    @pl.when(jnp.logical_not(_is_ctx_tile()))
    def _():
        store(n_lat_ref)


def _final_combine(y, out_slabs, top_p, gate, final_gain):
    tile = pl.BlockSpec((TM, D_MODEL), lambda i: (i, 0))
    return pl.pallas_call(
        _final_combine_kernel,
        out_shape=[jax.ShapeDtypeStruct((T_CTX, D_MODEL), F32), jax.ShapeDtypeStruct((T_LAT, D_MODEL), F32)],
        grid=(N_TILES,),
        in_specs=[pl.BlockSpec((TOP_K * TM * SLAB, LANES), lambda i: (i, 0)), tile,
                  pl.BlockSpec((TM, LANES), lambda i: (i, 0)),
                  pl.BlockSpec((None, 1, D_MODEL), lambda i: (_mod_row(i), 0, 0)),
                  pl.BlockSpec((1, D_MODEL), lambda i: (0, 0))],
        out_specs=[pl.BlockSpec((TM, D_MODEL), lambda i: (jnp.minimum(i, CTX_TILES - 1), 0)),
                   pl.BlockSpec((TM, D_MODEL), lambda i: (jnp.maximum(i - CTX_TILES, 0), 0))],
        compiler_params=_params(("arbitrary",)),
        name="moe_combine",
    )(out_slabs, y, top_p, gate, final_gain.reshape(1, D_MODEL))


def _moe_experts_layer(layer, routed, w_gu, b_gu, w_down, b_down):
    h_slabs, top_i, top_p, rank, counts = routed
    pos, fill, experts_plan = _route_plan(top_i[:, :TOP_K].reshape(-1), rank[:, :TOP_K].reshape(-1),
                                          counts[0, :N_EXPERTS].astype(jnp.int32))
    xs, inv = _dispatch(h_slabs, pos, fill)
    return _moe_experts(layer, xs, experts_plan, inv, w_gu, b_gu, w_down, b_down), top_p


def kernel(x_prompt, x_sample, c, cache_a_k, cache_a_v, state_b_C, state_b_n, state_b_m, cache_c_k, cache_c_v, cache_d_k, cache_d_v, c_ctx, w_mod, b_mod, norm1_g, norm2_g, w_in_even, w_out_even, a_q_gain, a_k_gain, b_gate_bias, b_norm_gain, w_in_odd, w_out_odd, c_rpb, d_lambda, d_norm_gain, router_w, router_b, expert_w_gu, expert_b_gu, expert_w_down, expert_b_down, final_norm_g):
    y = (x_prompt.reshape(T_CTX, D_MODEL), x_sample.reshape(T_LAT, D_MODEL))
    cond = jnp.zeros((MOD_ROWS, D_MODEL), F32).at[0].set(c_ctx).at[1:1 + DEC_BATCH].set(c)
    mod = _modulation(cond, w_mod, b_mod).reshape(DEPTH, MOD_ROWS, 6, 1, D_MODEL)
    rope_cos, rope_sin = _rope_tables()
    scale = HD ** -0.5
    outs = {}

    for layer in range(DEPTH):
        sh1, sc1, g1, sh2, sc2, g2 = (mod[layer, :, k] for k in range(6))
        j = layer // 2
        if layer % 2 == 0:
            w = w_in_even[j]
            sizes = np.cumsum([0, 512, 128, 128, 512, 512, 512, 512, 16])
            aq, ak, av, bq, bk, bv, bo, bg = (w[:, sizes[k]:sizes[k + 1]] for k in range(8))
            w_in = jnp.concatenate([aq, bo, bq, bk, bv, ak, av, bg, jnp.zeros((D_MODEL, EV_N - EV_BG - 16), F32)],
                                   axis=1).astype(BF16)
            qg = jnp.tile(a_q_gain[j], LANES // HD).reshape(1, LANES)
            kg = jnp.tile(a_k_gain[j], LANES // HD).reshape(1, LANES)
            specs = ((EV_AQ, 512, 0, True, scale, BF16, None), (EV_AK, 128, 1, True, 1.0, BF16, None),
                     (EV_AV, 128, None, False, 1.0, BF16, None),
                     (EV_AK, 128, 1, False, 1.0, F32, A_KV), (EV_AV, 128, None, False, 1.0, F32, A_KV))
            y, p, (qa, ka, va, new_ak, new_av) = _project(y, norm1_g[layer], sc1, sh1, w_in, rope_cos, rope_sin,
                                                          [qg, kg], specs)
            oa_ctx = _attention(qa, ka, va, ctx=True, group=A_HEADS // A_KV, n_kv=A_KV)
            cache = (cache_a_k[:, j].astype(BF16), cache_a_v[:, j].astype(BF16))
            oa_lat = _attention(qa, ka, va, ctx=False, group=A_HEADS // A_KV, n_kv=A_KV, cache=cache)
            zc = jnp.zeros((BATCH, 2, B_HEADS, B_DK, B_DV), F32)
            zn = jnp.zeros((BATCH, 2, B_HEADS, 1, B_DK), F32)
            zm = jnp.zeros((BATCH, 2, B_HEADS, 1, LANES), F32)
            hb_ctx, bC, bn, bm = _mlstm(p, b_gate_bias[j], zc, zn, zm, ctx=True)
            m0 = jnp.broadcast_to(state_b_m[:, j][..., None, None], (DEC_BATCH, 2, B_HEADS, 1, LANES))
            hb_lat, _, _, _ = _mlstm(p, b_gate_bias[j], state_b_C[:, j], state_b_n[:, j][:, :, :, None, :], m0,
                                     ctx=False)
            y, *routed = _merge_router(y, g1, w_out_even[j].astype(BF16), (oa_ctx, oa_lat), (hb_ctx, hb_lat),
                                       norm2_g[layer], sc2, sh2, router_w[layer], router_b[layer],
                                       p=p, norm_gain=b_norm_gain[j])
            outs.setdefault("a_k", []).append(new_ak)
            outs.setdefault("a_v", []).append(new_av)
            outs.setdefault("b_C", []).append(bC)
            outs.setdefault("b_n", []).append(bn[:, :, :, 0, :])
            outs.setdefault("b_m", []).append(bm[:, :, :, 0, 0])
        else:
            specs = ((0, 512, None, False, scale, BF16, None), (512, 512, None, False, 1.0, BF16, None),
                     (1024, 512, None, False, 1.0, BF16, None), (1536, 512, None, True, scale, BF16, None),
                     (2048, 512, None, True, 1.0, BF16, None), (2560, 512, None, False, 1.0, BF16, None),
                     (512, 512, None, False, 1.0, F32, C_HEADS), (1024, 512, None, False, 1.0, F32, C_HEADS),
                     (2048, 512, None, False, 1.0, F32, 2 * D_HEADS), (2560, 512, None, False, 1.0, F32, D_HEADS))
            y, p, (qc, kc, vc, qd, kd, vd, new_ck, new_cv, new_dk, new_dv) = _project(
                y, norm1_g[layer], sc1, sh1, w_in_odd[j].astype(BF16), rope_cos, rope_sin, [], specs, keep_p=False)
            lam_init = 0.8 - 0.6 * math.exp(-0.3 * layer)
            lp = d_lambda[j].astype(F32)
            lam = jnp.exp(jnp.sum(lp[0] * lp[1])) - jnp.exp(jnp.sum(lp[2] * lp[3])) + lam_init
            lam_vec = jnp.stack([lam, jnp.asarray(1.0 - lam_init, F32)]).astype(F32)
            diff = (lam_vec, d_norm_gain[j])
            oc_ctx = _attention(qc, kc, vc, ctx=True, n_kv=C_HEADS)
            od_ctx = _attention(qd, kd, vd, ctx=True, diff=diff)
            bias = _neighbourhood_bias(c_rpb[j])
            oc_lat = _attention(qc, kc, vc, ctx=False, n_kv=C_HEADS, bias=bias,
                                cache=(cache_c_k[:, j].astype(BF16), cache_c_v[:, j].astype(BF16)))
            kd_cache = cache_d_k[:, j].reshape(DEC_BATCH, 2 * D_HEADS, PAST_LEN, HD).astype(BF16)
            od_lat = _attention(qd, kd, vd, ctx=False, diff=diff, cache=(kd_cache, cache_d_v[:, j].astype(BF16)))
            y, *routed = _merge_router(y, g1, w_out_odd[j].astype(BF16), (oc_ctx, oc_lat), (od_ctx, od_lat),
                                       norm2_g[layer], sc2, sh2, router_w[layer], router_b[layer])
            outs.setdefault("c_k", []).append(new_ck)
            outs.setdefault("c_v", []).append(new_cv)
            outs.setdefault("d_k", []).append(new_dk.reshape(BATCH, D_HEADS, 2, SEQ, HD))
            outs.setdefault("d_v", []).append(new_dv)
        out_slabs, top_p = _moe_experts_layer(layer, routed, expert_w_gu, expert_b_gu, expert_w_down,
                                              expert_b_down)
        y = (y, out_slabs, top_p, g2)
    y_prompt, y_sample = _final_combine(*y, final_norm_g)
    stack = lambda k: jnp.stack(outs[k], axis=1)
    return (y_prompt.reshape(BATCH, SEQ, D_MODEL), y_sample.reshape(DEC_BATCH, DEC_SEQ, D_MODEL),
            stack("a_k"), stack("a_v"), stack("b_C"), stack("b_n"), stack("b_m"),
            stack("c_k"), stack("c_v"), stack("d_k"), stack("d_v"))
```

```python
import functools
import math

import numpy as np
import jax
import jax.numpy as jnp
from jax import lax
from jax.experimental import pallas as pl
from jax.experimental.pallas import tpu as pltpu

D_MODEL = 1024
BATCH = 32
SEQ = 256
DEPTH = 2
DEC_BATCH = 8
DEC_SEQ = 1024
PAST_LEN = 512
GRID_W = 64
HD = 64
A_HEADS = 8
A_KV = 2
B_HEADS = 4
B_DK = 128
B_DV = 128
B_CHUNK = 128
C_HEADS = 8
NA_ROWS = 8
NA_COLS = 16
D_HEADS = 4
D_VDIM = 2 * HD
N_EXPERTS = 32
TOP_K = 4
D_FF = 1024
SWIGLU_LIMIT = 7.0
SWIGLU_ALPHA = 1.702
ROPE_THETA = 10000.0
EPS = 1e-6

F32 = jnp.float32
BF16 = jnp.bfloat16
HIGHEST = lax.Precision.HIGHEST

T_CTX = BATCH * SEQ
T_LAT = DEC_BATCH * DEC_SEQ
T_ALL = T_CTX + T_LAT
TM = 256
CTX_TILES = T_CTX // TM
LAT_TILES_PER_BATCH = DEC_SEQ // TM
N_TILES = T_ALL // TM
MOD_ROWS = 16
LANES = 128
NEG_BIG = -1e30
MOE_TM = 256
N_ASSIGN = T_ALL * TOP_K
MOE_ROWS = N_ASSIGN + N_EXPERTS * MOE_TM
MOE_TILES = MOE_ROWS // MOE_TM
VMEM_LIMIT = 56 * 1024 * 1024
assert TM == SEQ and DEC_SEQ % TM == 0 and MOE_TM == TM

EV_BO, EV_BQ, EV_BK, EV_BV, EV_BG, EV_AQ, EV_AK, EV_AV = 0, 512, 1024, 1536, 2048, 2176, 2688, 2816
EV_KEEP = EV_AQ
EV_N = 2944


def _params(sem, vmem=VMEM_LIMIT):
    return pltpu.CompilerParams(dimension_semantics=sem, vmem_limit_bytes=vmem)


def _mod_row(i):
    return jnp.where(i < CTX_TILES, 0, 1 + (i - CTX_TILES) // LAT_TILES_PER_BATCH)


def _rope_block(i):
    return jnp.where(i < CTX_TILES, LAT_TILES_PER_BATCH, (i - CTX_TILES) % LAT_TILES_PER_BATCH)


def _dot(a, b, precision=None):
    return jnp.dot(a, b, preferred_element_type=F32, precision=precision)


def _dot_nt(a, b):
    return lax.dot_general(a, b, (((1,), (1,)), ((), ())), preferred_element_type=F32)


def _dot_tn(a, b):
    return lax.dot_general(a, b, (((0,), (0,)), ((), ())), preferred_element_type=F32)


def _modulation_kernel(c_ref, w_ref, b_ref, o_ref):
    c = c_ref[...]
    s = c * jax.nn.sigmoid(c)
    o_ref[...] = _dot(s, w_ref[...], HIGHEST) + b_ref[...]


def _modulation(cond, w_mod, b_mod):
    tn = 1536
    return pl.pallas_call(
        _modulation_kernel,
        out_shape=jax.ShapeDtypeStruct((DEPTH, MOD_ROWS, 6 * D_MODEL), F32),
        grid=(DEPTH, 6 * D_MODEL // tn),
        in_specs=[
            pl.BlockSpec((MOD_ROWS, D_MODEL), lambda l, j: (0, 0)),
            pl.BlockSpec((None, D_MODEL, tn), lambda l, j: (l, 0, j)),
            pl.BlockSpec((None, 1, tn), lambda l, j: (l, 0, j)),
        ],
        out_specs=pl.BlockSpec((None, MOD_ROWS, tn), lambda l, j: (l, 0, j)),
        compiler_params=_params(("parallel", "parallel")),
        name="modulation",
    )(cond, w_mod, b_mod.reshape(DEPTH, 1, 6 * D_MODEL))


def _norm_mod(y, g, sc, sh):
    ms = jnp.mean(y * y, axis=-1, keepdims=True)
    return (y * lax.rsqrt(ms + EPS) * g) * (1.0 + sc) + sh


def _rope_rotate(x):
    w = x.shape[-1]
    lane = lax.broadcasted_iota(jnp.int32, x.shape, 1)
    nxt = pltpu.roll(x, w - 1, 1)
    prv = pltpu.roll(x, 1, 1)
    return jnp.where((lane & 1) == 0, -nxt, prv)


def _is_ctx_tile():
    return pl.program_id(0) < CTX_TILES


def _pair_specs(width):
    return [pl.BlockSpec((TM, width), lambda i: (jnp.minimum(i, CTX_TILES - 1), 0)),
            pl.BlockSpec((TM, width), lambda i: (jnp.maximum(i - CTX_TILES, 0), 0))]


def _pair_value(ctx_ref, lat_ref, cols=slice(None)):
    return jnp.where(_is_ctx_tile(), ctx_ref[:, cols], lat_ref[:, cols])


def _moe_combine_chunks(out_ref, y_ref, tp_ref, g_ref):
    tp = tp_ref[...]
    chunks = []
    for c in range(SLAB):
        cs = slice(c * LANES, (c + 1) * LANES)
        acc = tp[:, 0:1] * out_ref[pl.ds(c, TM, stride=SLAB), :]
        for k in range(1, TOP_K):
            acc = acc + tp[:, k:k + 1] * out_ref[pl.ds(k * TM * SLAB + c, TM, stride=SLAB), :]
        chunks.append(y_ref[:, cs] + g_ref[:, cs] * acc)
    return chunks


def _norm_proj_kernel(*refs, specs, n_gain, y_mode, keep_p):
    n_y = {"single": 1, "pair": 2, "combine": 4}[y_mode]
    g_ref, sc_ref, sh_ref, w_ref, cos_ref, sin_ref, bd_ref = refs[n_y:n_y + 7]
    gain_refs = refs[n_y + 7:n_y + 7 + n_gain]
    n_in = n_y + 7 + n_gain
    if keep_p is True:
        p_ref, out_refs, p_part = refs[n_in], refs[n_in + 1:], None
    elif keep_p:
        p_ref, out_refs, p_part = refs[-1], refs[n_in + 1:-1], refs[n_in]
    else:
        p_ref, out_refs, p_part = refs[-1], refs[n_in:-1], None
    if y_mode == "combine":
        y = jnp.concatenate(_moe_combine_chunks(*refs[:4]), axis=1)
        out_refs[0][...] = y
        out_refs = out_refs[1:]
    elif y_mode == "pair":
        y = _pair_value(refs[0], refs[1])
    else:
        y = refs[0][...]
    h = _norm_mod(y, g_ref[...], sc_ref[...], sh_ref[...])
    p_ref[...] = _dot(h.astype(BF16), w_ref[...])
    if p_part is not None:
        p_part[...] = p_ref[:, :p_part.shape[1]]
    cos = cos_ref[...]
    sin = sin_ref[...]
    for (col, width, gi, rope, scale, _, heads), o_ref in zip(specs, out_refs):
        for c0 in range(0, width, LANES):
            x = p_ref[:, col + c0:col + c0 + LANES]
            if gi is not None:
                xx = x * x
                xx_hi = xx.astype(BF16)
                xx_lo = (xx - xx_hi.astype(F32)).astype(BF16)
                ss = _dot(xx_hi, bd_ref[...]) + _dot(xx_lo, bd_ref[...])
                x = x * lax.rsqrt(ss * (1.0 / HD) + EPS) * gain_refs[gi][...]
            if rope:
                x = x * cos + _rope_rotate(x) * sin
            if scale != 1.0:
                x = x * scale
            if heads is None:
                o_ref[:, c0:c0 + LANES] = x.astype(o_ref.dtype)
            else:
                hw = width // heads
                per = LANES // hw

                @pl.when(_is_ctx_tile())
                def _(x=x, o_ref=o_ref, c0=c0, hw=hw, per=per):
                    for u in range(per):
                        o_ref[(c0 // LANES) * per + u] = x[:, u * hw:(u + 1) * hw].astype(o_ref.dtype)


def _norm_proj(y, gain, scale, shift, w_bf16, rope_cos, rope_sin, gains, specs, keep_p=True):
    n = w_bf16.shape[1]
    tile = pl.BlockSpec((TM, D_MODEL), lambda i: (i, 0))
    vec = pl.BlockSpec((None, 1, D_MODEL), lambda i: (_mod_row(i), 0, 0))
    if not isinstance(y, tuple):
        y_mode, ys, in_specs = "single", [y], [tile]
    elif len(y) == 2:
        y_mode, ys, in_specs = "pair", list(y), _pair_specs(D_MODEL)
    else:
        y_prev, out_slabs, top_p, gate = y
        y_mode, ys = "combine", [out_slabs, y_prev, top_p, gate]
        in_specs = [pl.BlockSpec((TOP_K * TM * SLAB, LANES), lambda i: (i, 0)), tile,
                    pl.BlockSpec((TM, LANES), lambda i: (i, 0)), vec]
    bd = jnp.asarray(np.kron(np.eye(LANES // HD), np.ones((HD, HD))), BF16)
    rope_spec = pl.BlockSpec((TM, LANES), lambda i: (_rope_block(i), 0))
    in_specs += [pl.BlockSpec((1, D_MODEL), lambda i: (0, 0)),
                 vec, vec, pl.BlockSpec((D_MODEL, n), lambda i: (0, 0)),
                 rope_spec, rope_spec, pl.BlockSpec((LANES, LANES), lambda i: (0, 0))]
    in_specs += [pl.BlockSpec((1, LANES), lambda i: (0, 0)) for _ in gains]
    p_cols = n if keep_p is True else int(keep_p)
    out_shape = [jax.ShapeDtypeStruct((T_ALL, p_cols), F32)] if p_cols else []
    out_specs = [pl.BlockSpec((TM, p_cols), lambda i: (i, 0))] if p_cols else []
    if y_mode == "combine":
        out_shape.append(jax.ShapeDtypeStruct((T_ALL, D_MODEL), F32))
        out_specs.append(tile)
    for (_, width, _, _, _, dtype, heads) in specs:
        if heads is None:
            out_shape.append(jax.ShapeDtypeStruct((T_ALL, width), dtype))
            out_specs.append(pl.BlockSpec((TM, width), lambda i: (i, 0)))
        else:
            out_shape.append(jax.ShapeDtypeStruct((BATCH, heads, SEQ, width // heads), dtype))
            out_specs.append(pl.BlockSpec((None, heads, SEQ, width // heads),
                                          lambda i: (jnp.minimum(i, CTX_TILES - 1), 0, 0, 0)))
    return pl.pallas_call(
        functools.partial(_norm_proj_kernel, specs=specs, n_gain=len(gains), y_mode=y_mode, keep_p=keep_p),
        out_shape=out_shape,
        grid=(N_TILES,),
        in_specs=in_specs,
        out_specs=out_specs,
        scratch_shapes=[] if keep_p is True else [pltpu.VMEM((TM, n), F32)],
        compiler_params=_params(("arbitrary",)),
        name="norm_proj",
    )(*ys, gain.reshape(1, D_MODEL), scale, shift, w_bf16, rope_cos, rope_sin, bd, *gains)


def _project(y, *args, keep_p=True):
    res = list(_norm_proj(y, *args, keep_p=keep_p))
    p = res.pop(0) if keep_p else None
    if isinstance(y, tuple) and len(y) == 4:
        y = res.pop(0)
    return y, p, res


def _rope_tables():
    half = HD // 2
    freqs = 1.0 / (ROPE_THETA ** (jnp.arange(0, half, 2, dtype=F32) / half))
    t = jnp.arange(DEC_SEQ)
    rows = (t // GRID_W).astype(F32)
    cols = (t % GRID_W).astype(F32)
    ang = jnp.concatenate([rows[:, None] * freqs, cols[:, None] * freqs], axis=-1)
    cos = jnp.repeat(jnp.cos(ang), 2, axis=-1)
    sin = jnp.repeat(jnp.sin(ang), 2, axis=-1)
    cos = jnp.concatenate([jnp.tile(cos, (1, LANES // HD)), jnp.ones((TM, LANES), F32)], axis=0)
    sin = jnp.concatenate([jnp.tile(sin, (1, LANES // HD)), jnp.zeros((TM, LANES), F32)], axis=0)
    return cos, sin


def _lane_slice(ref, h, width=HD):
    per = LANES // width
    blk = ref[:, (h // per) * LANES:(h // per + 1) * LANES]
    if per == 1:
        return blk
    return blk[:, (h % per) * width:(h % per + 1) * width]


def _softmax_parts(scores):
    m = None
    for s in scores:
        ms = jnp.max(s, axis=-1, keepdims=True)
        m = ms if m is None else jnp.maximum(m, ms)
    ps = [jnp.exp(s - m) for s in scores]
    l = None
    for p in ps:
        ls = jnp.sum(p, axis=-1, keepdims=True)
        l = ls if l is None else l + ls
    return ps, l


def _attn_std_kernel(*refs, group, n_kv, has_cache, has_bias, bq):
    it = iter(refs)
    q_ref, kn_ref, vn_ref = next(it), next(it), next(it)
    kc_ref = vc_ref = b_ref = None
    if has_cache:
        kc_ref, vc_ref = next(it), next(it)
    if has_bias:
        b_ref = next(it)
    o_ref = next(it)
    outs = []
    for g in range(n_kv):
        qs = jnp.concatenate([_lane_slice(q_ref, g * group + j) for j in range(group)], axis=0)
        kn = _lane_slice(kn_ref, g)
        vn = _lane_slice(vn_ref, g)
        s_new = _dot_nt(qs, kn)
        if has_bias:
            s_new = s_new + b_ref[g]
        scores = [s_new]
        if has_cache:
            scores.append(_dot_nt(qs, kc_ref[g]))
        ps, l = _softmax_parts(scores)
        o = _dot(ps[0].astype(BF16), vn)
        if has_cache:
            o = o + _dot(ps[1].astype(BF16), vc_ref[g])
        o = o / l
        for j in range(group):
            outs.append(o[j * bq:(j + 1) * bq])
    o_ref[...] = jnp.concatenate(outs, axis=1).astype(o_ref.dtype)


def _attn_diff_kernel(*refs, has_cache):
    it = iter(refs)
    lam_ref, q_ref, kn_ref, vn_ref = next(it), next(it), next(it), next(it)
    kc_ref = vc_ref = None
    if has_cache:
        kc_ref, vc_ref = next(it), next(it)
    g_ref, o_ref = next(it), next(it)
    lam = lam_ref[0]
    post = lam_ref[1]
    outs = []
    for h in range(D_HEADS):
        pd_new, pd_c = None, None
        for j in range(2):
            f = 2 * h + j
            qs = _lane_slice(q_ref, f)
            scores = [_dot_nt(qs, _lane_slice(kn_ref, f))]
            if has_cache:
                scores.append(_dot_nt(qs, kc_ref[f]))
            ps, l = _softmax_parts(scores)
            r = 1.0 / l
            if j == 0:
                pd_new = ps[0] * r
                pd_c = ps[1] * r if has_cache else None
            else:
                r = r * lam
                pd_new = pd_new - ps[0] * r
                pd_c = pd_c - ps[1] * r if has_cache else None
        o = _dot(pd_new.astype(BF16), _lane_slice(vn_ref, h, D_VDIM))
        if has_cache:
            o = o + _dot(pd_c.astype(BF16), vc_ref[h])
        ms = jnp.mean(o * o, axis=-1, keepdims=True)
        outs.append(o * lax.rsqrt(ms + EPS) * g_ref[...] * post)
    o_ref[...] = jnp.concatenate(outs, axis=1).astype(o_ref.dtype)


def _attention(q, kn, vn, *, ctx, group=1, n_kv=1, cache=None, bias=None, diff=None, bq=256):
    if ctx:
        nb, sq, row0 = BATCH, SEQ, 0
    else:
        nb, sq, row0 = DEC_BATCH, DEC_SEQ, T_CTX
    nq = sq // bq
    qb0 = row0 // bq
    kb0 = row0 // sq
    wq, wk, wv = q.shape[1], kn.shape[1], vn.shape[1]
    in_specs = [
        pl.BlockSpec((bq, wq), lambda b, i: (qb0 + b * nq + i, 0)),
        pl.BlockSpec((sq, wk), lambda b, i: (kb0 + b, 0)),
        pl.BlockSpec((sq, wv), lambda b, i: (kb0 + b, 0)),
    ]
    args = [q, kn, vn]
    if cache is not None:
        kc, vc = cache
        in_specs += [pl.BlockSpec((None,) + kc.shape[1:], lambda b, i: (b, 0, 0, 0)),
                     pl.BlockSpec((None,) + vc.shape[1:], lambda b, i: (b, 0, 0, 0))]
        args += [kc, vc]
    if diff is None:
        if bias is not None:
            in_specs.append(pl.BlockSpec((bias.shape[0], bq, sq), lambda b, i: (0, i, 0)))
            args.append(bias)
        body = functools.partial(_attn_std_kernel, group=group, n_kv=n_kv, has_cache=cache is not None,
                                 has_bias=bias is not None, bq=bq)
    else:
        lam_vec, gain = diff
        in_specs = [pl.BlockSpec(memory_space=pltpu.SMEM)] + in_specs
        args = [lam_vec] + args
        in_specs.append(pl.BlockSpec((1, D_VDIM), lambda b, i: (0, 0)))
        args.append(gain.reshape(1, D_VDIM))
        body = functools.partial(_attn_diff_kernel, has_cache=cache is not None)
    return pl.pallas_call(
        body,
        out_shape=jax.ShapeDtypeStruct((nb * sq, 512), BF16),
        grid=(nb, nq),
        in_specs=in_specs,
        out_specs=pl.BlockSpec((bq, 512), lambda b, i: (b * nq + i, 0)),
        compiler_params=_params(("parallel", "parallel")),
        name="attention",
    )(*args)


GRID_ROWS = DEC_SEQ // GRID_W
NA_WIN_ROWS = min(NA_ROWS, GRID_ROWS)


def _na_bias_kernel(t_ref, o_ref):
    outside = jnp.full((GRID_W, GRID_W), NEG_BIG, F32)
    for qr in range(GRID_ROWS):
        r0 = min(max(qr - NA_WIN_ROWS // 2, 0), GRID_ROWS - NA_WIN_ROWS)
        parts = [t_ref[kr - qr + (NA_ROWS - 1)] if r0 <= kr < r0 + NA_WIN_ROWS else outside
                 for kr in range(GRID_ROWS)]
        o_ref[qr * GRID_W:(qr + 1) * GRID_W, :] = jnp.concatenate(parts, axis=1)


def _neighbourhood_bias(rpb):
    c = np.arange(GRID_W)
    c0 = np.clip(c - NA_COLS // 2, 0, GRID_W - NA_COLS)
    col_ok = (c[None, :] >= c0[:, None]) & (c[None, :] < c0[:, None] + NA_COLS)
    dc = np.clip(c[None, :] - c[:, None], 1 - NA_COLS, NA_COLS - 1) + (NA_COLS - 1)
    oh_c = jnp.asarray(dc[..., None] == np.arange(2 * NA_COLS - 1), F32)
    by_col = jnp.einsum("hrd,qkd->hrqk", rpb.astype(F32), oh_c, precision=HIGHEST)
    by_col = jnp.where(jnp.asarray(col_ok), by_col, NEG_BIG)
    n_dr = 2 * NA_ROWS - 1
    return pl.pallas_call(
        _na_bias_kernel,
        out_shape=jax.ShapeDtypeStruct((C_HEADS, DEC_SEQ, DEC_SEQ), F32),
        grid=(C_HEADS,),
        in_specs=[pl.BlockSpec((None, n_dr, GRID_W, GRID_W), lambda h: (h, 0, 0, 0))],
        out_specs=pl.BlockSpec((None, DEC_SEQ, DEC_SEQ), lambda h: (h, 0, 0)),
        compiler_params=_params(("parallel",)),
        name="na_bias",
    )(by_col)


def _log_sigmoid(x):
    return jnp.minimum(x, 0.0) - jnp.log1p(jnp.exp(-jnp.abs(x)))


def _mlstm_kernel(q_ref, k_ref, v_ref, g_ref, gb_ref, c0_ref, n0_ref, m0_ref, h_ref, c_ref, n_ref, m_ref, hb_ref,
                  *, seq):
    L = B_CHUNK
    nc = seq // L
    row = lax.broadcasted_iota(jnp.int32, (L, L), 0)
    col = lax.broadcasted_iota(jnp.int32, (L, L), 1)
    keeps = (col <= row, col >= row)
    k_scale = B_DK ** -0.5
    c_ref[...] = c0_ref[...]
    n_ref[...] = n0_ref[...]
    m_ref[...] = m0_ref[...]

    def step(j):
        for d in range(2):
            keep = keeps[d]
            c = j if d == 0 else nc - 1 - j
            off = c * L if isinstance(c, int) else pl.multiple_of(c * L, L)
            gates = g_ref[pl.ds(off, L), :] + gb_ref[...]
            cum = _dot(keep.astype(F32), _log_sigmoid(gates), HIGHEST)
            cum_t = cum.T
            gates_t = gates.T
            out_ref = h_ref if d == 0 else hb_ref
            for h in range(B_HEADS):
                ci = (2 * d) * B_HEADS + h
                cf = (2 * d + 1) * B_HEADS + h
                hs = slice(h * B_DK, (h + 1) * B_DK)
                C = c_ref[d, h]
                n = n_ref[d, h]
                m = m_ref[d, h][:, 0:1]
                qc = q_ref[pl.ds(off, L), hs]
                kc = k_ref[pl.ds(off, L), hs] * k_scale
                vc = v_ref[pl.ds(off, L), hs]
                b_col = cum[:, cf:cf + 1]
                i_col = gates[:, ci:ci + 1]
                b_row = cum_t[cf:cf + 1, :]
                i_row = gates_t[ci:ci + 1, :]
                dlog = jnp.where(keep, b_col - b_row + i_row, -jnp.inf)
                inter = b_col + m
                m_t = jnp.maximum(inter, jnp.max(dlog, axis=-1, keepdims=True))
                w_intra = jnp.exp(dlog - m_t)
                w_inter = jnp.exp(inter - m_t)
                qb = qc.astype(BF16)
                vb = vc.astype(BF16)
                qk = _dot_nt(qb, kc.astype(BF16)) * w_intra
                num = _dot(qk.astype(BF16), vb) + w_inter * _dot(qb, C.astype(BF16))
                den = jnp.sum(qk, axis=-1, keepdims=True) + w_inter * jnp.sum(qc * n, axis=-1, keepdims=True)
                out_ref[pl.ds(off, L), hs] = num / jnp.maximum(jnp.abs(den), jnp.exp(-m_t))
                b_last = b_col[L - 1:L, :] if d == 0 else b_col[0:1, :]
                end_col = b_last - b_col + i_col
                m_new = jnp.maximum(b_last + m, jnp.max(end_col, axis=0, keepdims=True))
                w_end = jnp.exp(end_col - m_new)
                decay = jnp.exp(b_last + m - m_new)
                kw = kc * w_end
                c_ref[d, h] = decay * C + _dot_tn(kw.astype(BF16), vb)
                n_ref[d, h] = decay * n + jnp.sum(kw, axis=0, keepdims=True)
                m_ref[d, h] = jnp.broadcast_to(m_new, (1, LANES))

    if nc <= 2:
        for j in range(nc):
            step(j)
    else:
        def body(j, carry):
            step(j)
            return carry

        lax.fori_loop(0, nc, body, 0)
    h_ref[...] = h_ref[...] + hb_ref[...]


def _mlstm(p, gate_bias, c0, n0, m0, *, ctx):
    if ctx:
        nb, seq, blk0 = BATCH, SEQ, 0
    else:
        nb, seq, blk0 = DEC_BATCH, DEC_SEQ, T_CTX // DEC_SEQ
    w = B_HEADS * B_DK

    def cols(c0_, width):
        return pl.BlockSpec((seq, width), lambda b: (blk0 + b, c0_ // width))

    gb = jnp.zeros((1, LANES), F32).at[0, :4 * B_HEADS].set(gate_bias.reshape(-1).astype(F32))
    st = lambda shape: pl.BlockSpec((None,) + shape, lambda b: (b, 0, 0, 0, 0))
    return pl.pallas_call(
        functools.partial(_mlstm_kernel, seq=seq),
        out_shape=[
            jax.ShapeDtypeStruct((nb * seq, w), F32),
            jax.ShapeDtypeStruct((nb, 2, B_HEADS, B_DK, B_DV), F32),
            jax.ShapeDtypeStruct((nb, 2, B_HEADS, 1, B_DK), F32),
            jax.ShapeDtypeStruct((nb, 2, B_HEADS, 1, LANES), F32),
        ],
        grid=(nb,),
        in_specs=[
            cols(EV_BQ, w), cols(EV_BK, w), cols(EV_BV, w), cols(EV_BG, LANES),
            pl.BlockSpec((1, LANES), lambda b: (0, 0)),
            st((2, B_HEADS, B_DK, B_DV)), st((2, B_HEADS, 1, B_DK)), st((2, B_HEADS, 1, LANES)),
        ],
        out_specs=[
            pl.BlockSpec((seq, w), lambda b: (b, 0)),
            st((2, B_HEADS, B_DK, B_DV)), st((2, B_HEADS, 1, B_DK)), st((2, B_HEADS, 1, LANES)),
        ],
        scratch_shapes=[pltpu.VMEM((seq, w), F32)],
        compiler_params=_params(("parallel",)),
        name="mlstm",
    )(p, p, p, p, gb, c0, n0, m0)


def _merge_value(refs, even, y_pair):
    a_ctx, a_lat, b_ctx, b_lat = refs[:4]
    rest = refs[4:]
    if even:
        bo_ref, ng_ref = rest[:2]
        rest = rest[2:]
        parts = [_pair_value(a_ctx, a_lat)]
        for h in range(B_HEADS):
            hs = slice(h * B_DV, (h + 1) * B_DV)
            x = _pair_value(b_ctx, b_lat, hs)
            ms = jnp.mean(x * x, axis=-1, keepdims=True)
            xn = x * lax.rsqrt(ms + EPS) * ng_ref[:, hs]
            parts.append((jax.nn.sigmoid(bo_ref[:, hs]) * xn).astype(BF16))
    else:
        parts = [_pair_value(a_ctx, a_lat), _pair_value(b_ctx, b_lat)]
    w_ref = rest[0]
    y = _pair_value(rest[1], rest[2]) if y_pair else rest[1][...]
    g_ref = rest[-1]
    cat = jnp.concatenate(parts, axis=1)
    return y + g_ref[...] * _dot(cat, w_ref[...])


def _merge_specs(y, gate, w_bf16, a, b, p, norm_gain):
    in_specs = _pair_specs(512) + _pair_specs(512)
    args = [*a, *b]
    if p is not None:
        in_specs += [pl.BlockSpec((TM, 512), lambda i: (i, EV_BO // 512)), pl.BlockSpec((1, 512), lambda i: (0, 0))]
        args += [p, norm_gain.reshape(1, 512)]
    in_specs.append(pl.BlockSpec((D_MODEL, D_MODEL), lambda i: (0, 0)))
    args.append(w_bf16)
    if isinstance(y, tuple):
        in_specs += _pair_specs(D_MODEL)
        args += list(y)
    else:
        in_specs.append(pl.BlockSpec((TM, D_MODEL), lambda i: (i, 0)))
        args.append(y)
    in_specs.append(pl.BlockSpec((None, 1, D_MODEL), lambda i: (_mod_row(i), 0, 0)))
    args.append(gate)
    return in_specs, args


SLAB = D_MODEL // LANES


def _load_slabs(ref, rows):
    return jnp.concatenate([ref[pl.ds(c, rows, stride=SLAB), :] for c in range(SLAB)], axis=1)


def _store_slabs(ref, x):
    for c in range(SLAB):
        ref[pl.ds(c, x.shape[0], stride=SLAB), :] = x[:, c * LANES:(c + 1) * LANES]


def _slab(ref, idx):
    return ref.at[pl.ds(pl.multiple_of(idx * SLAB, SLAB), SLAB)]


def _merge_router_kernel(*refs, even, y_pair):
    n_merge = 4 + (2 if even else 0) + 1 + (2 if y_pair else 1) + 1
    merge_refs = refs[:n_merge]
    (g_ref, sc_ref, sh_ref, whi_ref, wlo_ref, b_ref,
     y_out_ref, h_ref, ti_ref, tp_ref, rk_ref, cnt_ref, base_ref) = refs[n_merge:]

    @pl.when(pl.program_id(0) == 0)
    def _():
        base_ref[...] = jnp.zeros(base_ref.shape, F32)

    y = _merge_value(merge_refs, even, y_pair)
    y_out_ref[...] = y
    h = _norm_mod(y, g_ref[...], sc_ref[...], sh_ref[...])
    _store_slabs(h_ref, h)
    h_hi = h.astype(BF16)
    h_lo = (h - h_hi.astype(F32)).astype(BF16)
    logits = (_dot(h_hi, whi_ref[...]) + (_dot(h_hi, wlo_ref[...]) + _dot(h_lo, whi_ref[...]))
              + b_ref[...])
    lane = lax.broadcasted_iota(jnp.int32, logits.shape, 1)
    lane_f = lane.astype(F32)
    vals, idxs = [], []
    for _ in range(TOP_K):
        mx = jnp.max(logits, axis=-1, keepdims=True)
        ix = jnp.min(jnp.where(logits == mx, lane_f, float(LANES)), axis=-1, keepdims=True)
        vals.append(mx)
        idxs.append(ix)
        logits = jnp.where(lane_f == ix, -jnp.inf, logits)
    es = [jnp.exp(v - vals[0]) for v in vals]
    tot = es[0] + es[1] + es[2] + es[3]
    ti = jnp.zeros(logits.shape, F32)
    tp = jnp.zeros(logits.shape, F32)
    for k in range(TOP_K):
        ti = jnp.where(lane == k, idxs[k], ti)
        tp = jnp.where(lane == k, es[k] / tot, tp)
    ti_ref[...] = ti.T[0:8, :].astype(jnp.int32)
    tp_ref[...] = tp
    onehots = [(lane_f == ix).astype(F32) for ix in idxs]
    cnt = onehots[0] + onehots[1] + onehots[2] + onehots[3]
    row = lax.broadcasted_iota(jnp.int32, (TM, TM), 0)
    col = lax.broadcasted_iota(jnp.int32, (TM, TM), 1)
    before = _dot((col < row).astype(BF16), cnt.astype(BF16)) + base_ref[...]
    rk = jnp.zeros(logits.shape, F32)
    for k in range(TOP_K):
        rk = jnp.where(lane == k, jnp.sum(onehots[k] * before, axis=-1, keepdims=True), rk)
    rk_ref[...] = rk.T[0:8, :].astype(jnp.int32)
    base_ref[...] = base_ref[...] + jnp.sum(cnt, axis=0, keepdims=True)
    cnt_ref[...] = base_ref[...]


def _merge_router(y, gate, w_out_bf16, a, b, gain, scale, shift, rw, rb, *, p=None, norm_gain=None):
    merge_in_specs, merge_args = _merge_specs(y, gate, w_out_bf16, a, b, p, norm_gain)
    vec = pl.BlockSpec((None, 1, D_MODEL), lambda i: (_mod_row(i), 0, 0))
    rw_p = jnp.zeros((D_MODEL, LANES), F32).at[:, :N_EXPERTS].set(rw)
    rb_p = jnp.full((1, LANES), NEG_BIG, F32).at[0, :N_EXPERTS].set(rb)
    rw_hi = rw_p.astype(BF16)
    tile = lambda w: pl.BlockSpec((TM, w), lambda i: (i, 0))
    by_choice = pl.BlockSpec((None, 8, TM), lambda i: (i, 0, 0))
    return pl.pallas_call(
        functools.partial(_merge_router_kernel, even=p is not None, y_pair=isinstance(y, tuple)),
        out_shape=[jax.ShapeDtypeStruct((T_ALL, D_MODEL), F32),
                   jax.ShapeDtypeStruct((T_ALL * SLAB, LANES), F32),
                   jax.ShapeDtypeStruct((N_TILES, 8, TM), jnp.int32),
                   jax.ShapeDtypeStruct((T_ALL, LANES), F32),
                   jax.ShapeDtypeStruct((N_TILES, 8, TM), jnp.int32),
                   jax.ShapeDtypeStruct((1, LANES), F32)],
        grid=(N_TILES,),
        in_specs=merge_in_specs + [
            pl.BlockSpec((1, D_MODEL), lambda i: (0, 0)), vec, vec,
            pl.BlockSpec((D_MODEL, LANES), lambda i: (0, 0)), pl.BlockSpec((D_MODEL, LANES), lambda i: (0, 0)),
            pl.BlockSpec((1, LANES), lambda i: (0, 0))],
        out_specs=[tile(D_MODEL), pl.BlockSpec((TM * SLAB, LANES), lambda i: (i, 0)), by_choice, tile(LANES),
                   by_choice, pl.BlockSpec((1, LANES), lambda i: (0, 0))],
        scratch_shapes=[pltpu.VMEM((1, LANES), F32)],
        compiler_params=_params(("arbitrary",)),
        name="merge_router",
    )(*merge_args, gain.reshape(1, D_MODEL), scale, shift, rw_hi, (rw_p - rw_hi.astype(F32)).astype(BF16), rb_p)


def _route_plan(top_i, rank, counts):
    experts = jnp.arange(N_EXPERTS, dtype=jnp.int32)
    padded = ((counts + MOE_TM - 1) // MOE_TM) * MOE_TM
    seg_end = jnp.cumsum(padded)
    seg_start = seg_end - padded
    pos = rank
    for e in range(N_EXPERTS - 1):
        pos = pos + jnp.where(top_i > e, padded[e], 0)
    n_active = seg_end[-1] // MOE_TM
    fill = jnp.concatenate([seg_start + counts, padded - counts, n_active[None]]).astype(jnp.int32)
    tile_start = jnp.arange(MOE_TILES, dtype=jnp.int32) * MOE_TM
    tile_expert = jnp.sum((seg_end[None, :] <= tile_start[:, None]).astype(jnp.int32), axis=1)
    last = jnp.sum((seg_end <= (n_active - 1) * MOE_TM).astype(jnp.int32))
    tile_expert = jnp.minimum(jnp.where(tile_start < seg_end[-1], tile_expert, last), N_EXPERTS - 1)
    owns = (padded > 0).astype(jnp.int32)
    run_of_expert = jnp.cumsum(owns) - 1
    run_expert = jnp.sum(jnp.where((run_of_expert[None, :] == experts[:, None]) & (owns[None, :] > 0),
                                   experts[None, :], 0), axis=1)
    runs = jnp.concatenate([run_expert, jnp.sum(owns)[None]]).astype(jnp.int32)
    tile_run = jnp.sum(jnp.where(tile_expert[:, None] == experts[None, :], run_of_expert[None, :], 0), axis=1)
    experts_plan = (tile_expert.astype(jnp.int32), n_active.reshape(1).astype(jnp.int32),
                    tile_run.astype(jnp.int32), runs)
    return pos.astype(jnp.int32), fill, experts_plan


DMA_UNROLL = 4
DMA_QUEUES = 2


def _wait_slabs(ref, n_slabs, sem):
    view = ref.at[pl.ds(0, n_slabs * SLAB)]
    pltpu.make_async_copy(view, view, sem).wait()


def _dispatch_kernel(pos_ref, fill_ref, h_hbm, xs_ref, inv_ref, hbuf, in_sems, sems):
    i = pl.program_id(0)
    base = i * (TM * TOP_K)
    sem = sems.at[i % 2]

    def tile_fetch(tile):
        src = h_hbm.at[pl.ds(pl.multiple_of(tile * (TM * SLAB), TM * SLAB), TM * SLAB)]
        return pltpu.make_async_copy(src, hbuf.at[tile % 3], in_sems.at[tile % 3])

    @pl.when(i == 0)
    def _():
        tile_fetch(0).start()

    tile_fetch(i).wait()

    @pl.when(i + 1 < N_TILES)
    def _():
        tile_fetch(i + 1).start()

    h_ref = hbuf.at[i % 3]

    def issue(j, carry):
        for u in range(DMA_UNROLL):
            t = j * DMA_UNROLL + u
            for k in range(TOP_K):
                a = base + k * TM + t
                slot = pos_ref[a]
                inv_ref[slot] = a
                pltpu.make_async_copy(_slab(h_ref, t), _slab(xs_ref, slot), sem).start(priority=k % DMA_QUEUES)
        return carry

    lax.fori_loop(0, TM // DMA_UNROLL, issue, 0)

    @pl.when(i == 0)
    def _():
        def per_expert(e, total):
            start = fill_ref[e]
            n = fill_ref[N_EXPERTS + e]

            def one(r, carry):
                slot = start + r
                inv_ref[slot] = N_ASSIGN + slot % MOE_TM
                pltpu.make_async_copy(_slab(h_ref, 0), _slab(xs_ref, slot), sem).start()
                return carry

            lax.fori_loop(0, n, one, 0)
            return total + n

        total = lax.fori_loop(0, N_EXPERTS, per_expert, 0)

        n_active = fill_ref[2 * N_EXPERTS]

        def unused_tile(ti, carry):
            dst = xs_ref.at[pl.ds(pl.multiple_of(ti * (MOE_TM * SLAB), MOE_TM * SLAB), MOE_TM * SLAB)]
            pltpu.make_async_copy(h_ref, dst, sem).start()
            return carry

        lax.fori_loop(n_active, MOE_TILES, unused_tile, 0)

        def unused_slot(slot, carry):
            inv_ref[slot] = N_ASSIGN
            return carry

        lax.fori_loop(n_active * MOE_TM, MOE_ROWS, unused_slot, 0)
        total = total + (MOE_TILES - n_active) * MOE_TM

        @pl.when(total > 0)
        def _():
            _wait_slabs(xs_ref, total, sem)

    @pl.when(i > 0)
    def _():
        _wait_slabs(xs_ref, TM * TOP_K, sems.at[(i + 1) % 2])

    @pl.when(i == N_TILES - 1)
    def _():
        _wait_slabs(xs_ref, TM * TOP_K, sem)


def _dispatch(h_slabs, pos, fill):
    grid_spec = pltpu.PrefetchScalarGridSpec(
        num_scalar_prefetch=2,
        grid=(N_TILES,),
        in_specs=[pl.BlockSpec(memory_space=pl.ANY)],
        out_specs=[pl.BlockSpec(memory_space=pl.ANY), pl.BlockSpec(memory_space=pltpu.SMEM)],
        scratch_shapes=[pltpu.VMEM((3, TM * SLAB, LANES), F32), pltpu.SemaphoreType.DMA((3,)),
                        pltpu.SemaphoreType.DMA((2,))],
    )
    return pl.pallas_call(
        _dispatch_kernel,
        out_shape=[jax.ShapeDtypeStruct((MOE_ROWS * SLAB, LANES), F32),
                   jax.ShapeDtypeStruct((MOE_ROWS,), jnp.int32)],
        grid_spec=grid_spec,
        compiler_params=_params(("arbitrary",)),
        name="moe_dispatch",
    )(pos, fill, h_slabs)


MOE_HALF = MOE_TM // 2


def _moe_kernel(te_ref, na_ref, ts_ref, ex_ref, inv_ref, x_ref, wgu_hbm, bgu_ref, wd_hbm, bd_ref, out_hbm,
                wgu_f32, wd_f32, wgu_bf, wd_bf, obuf0, obuf1, sems, osems, *, layer):
    i = pl.program_id(0)
    s = ts_ref[i]
    first = (i == 0) | (s != ts_ref[jnp.maximum(i - 1, 0)])
    n_active = na_ref[0]

    def start_rows(buf, half, tile):
        for r in range(MOE_HALF):
            dst = _slab(out_hbm, inv_ref[tile * MOE_TM + half * MOE_HALF + r])
            pltpu.make_async_copy(buf.at[pl.ds(r * SLAB, SLAB)], dst, osems.at[half]).start(
                priority=r % DMA_QUEUES)

    def wait_rows(half):
        _wait_slabs(out_hbm, MOE_HALF, osems.at[half])

    def ffn_tile(deferred):
        if deferred:
            start_rows(obuf1, 1, i - 1)
            wait_rows(0)
        x = _load_slabs(x_ref, MOE_TM).astype(BF16)
        gu = _dot(x, wgu_bf[...]) + bgu_ref[...]
        gate = jnp.minimum(gu[:, :D_FF], SWIGLU_LIMIT)
        up = jnp.clip(gu[:, D_FF:], -SWIGLU_LIMIT, SWIGLU_LIMIT)
        act = ((up + 1.0) * gate * jax.nn.sigmoid(SWIGLU_ALPHA * gate)).astype(BF16)
        o0 = _dot(act[:MOE_HALF], wd_bf[...]) + bd_ref[...]
        _store_slabs(obuf0, o0)
        start_rows(obuf0, 0, i)
        o1 = _dot(act[MOE_HALF:], wd_bf[...]) + bd_ref[...]
        if deferred:
            wait_rows(1)
        _store_slabs(obuf1, o1)

    def weight_copies(slot):
        e = ex_ref[slot]
        b = slot % 2
        return (pltpu.make_async_copy(wgu_hbm.at[layer, e], wgu_f32.at[b], sems.at[0, b]),
                pltpu.make_async_copy(wd_hbm.at[layer, e], wd_f32.at[b], sems.at[1, b]))

    @pl.when(i == 0)
    def _():
        for cp in weight_copies(0):
            cp.start(priority=1)

    @pl.when(first)
    def _():
        for cp in weight_copies(s):
            cp.wait()

        @pl.when(s + 1 < ex_ref[N_EXPERTS])
        def _():
            for cp in weight_copies(s + 1):
                cp.start(priority=1)

        b = s % 2
        wgu_bf[...] = wgu_f32[b].astype(BF16)
        wd_bf[...] = wd_f32[b].astype(BF16)

    @pl.when(i == 0)
    def _():
        ffn_tile(False)

    @pl.when((i > 0) & (i < n_active))
    def _():
        ffn_tile(True)

    @pl.when(i == n_active)
    def _():
        wait_rows(0)
        start_rows(obuf1, 1, i - 1)
        wait_rows(1)
        for half, buf in enumerate((obuf0, obuf1)):
            dst = out_hbm.at[pl.ds((N_ASSIGN + half * MOE_HALF) * SLAB, MOE_HALF * SLAB)]
            pltpu.make_async_copy(buf, dst, osems.at[half]).start()
        wait_rows(0)
        wait_rows(1)


def _moe_experts(layer, xs, plan, inv, w_gu, b_gu, w_down, b_down):
    const = lambda i, te, na, ts, ex, inv: (layer, te[i], 0, 0)
    grid_spec = pltpu.PrefetchScalarGridSpec(
        num_scalar_prefetch=5,
        grid=(MOE_TILES,),
        in_specs=[
            pl.BlockSpec((MOE_TM * SLAB, LANES), lambda i, te, na, ts, ex, inv: (jnp.minimum(i, na[0] - 1), 0)),
            pl.BlockSpec(memory_space=pl.ANY),
            pl.BlockSpec((None, None, 1, 2 * D_FF), const),
            pl.BlockSpec(memory_space=pl.ANY),
            pl.BlockSpec((None, None, 1, D_MODEL), const),
        ],
        out_specs=pl.BlockSpec(memory_space=pl.ANY),
        scratch_shapes=[pltpu.VMEM((2, D_MODEL, 2 * D_FF), F32), pltpu.VMEM((2, D_FF, D_MODEL), F32),
                        pltpu.VMEM((D_MODEL, 2 * D_FF), BF16), pltpu.VMEM((D_FF, D_MODEL), BF16),
                        pltpu.VMEM((MOE_HALF * SLAB, LANES), F32), pltpu.VMEM((MOE_HALF * SLAB, LANES), F32),
                        pltpu.SemaphoreType.DMA((2, 2)), pltpu.SemaphoreType.DMA((2,))],
    )
    return pl.pallas_call(
        functools.partial(_moe_kernel, layer=layer),
        out_shape=jax.ShapeDtypeStruct(((N_ASSIGN + MOE_TM) * SLAB, LANES), F32),
        grid_spec=grid_spec,
        compiler_params=_params(("arbitrary",)),
        name="moe_experts",
    )(*plan, inv, xs, w_gu, b_gu.reshape(DEPTH, N_EXPERTS, 1, 2 * D_FF), w_down,
      b_down.reshape(DEPTH, N_EXPERTS, 1, D_MODEL))


def _final_combine_kernel(out_ref, y_ref, tp_ref, g_ref, fg_ref, n_ctx_ref, n_lat_ref):
    chunks = _moe_combine_chunks(out_ref, y_ref, tp_ref, g_ref)
    ss = jnp.zeros((TM, 1), F32)
    for yc in chunks:
        ss = ss + jnp.sum(yc * yc, axis=-1, keepdims=True)
    inv = lax.rsqrt(ss * (1.0 / D_MODEL) + EPS)

    def store(n_ref):
        for c in range(SLAB):
            cs = slice(c * LANES, (c + 1) * LANES)
            n_ref[:, cs] = chunks[c] * inv * fg_ref[:, cs]

    @pl.when(_is_ctx_tile())
    def _():
        store(n_ctx_ref)

    @pl.when(jnp.logical_not(_is_ctx_tile()))
    def _():
        store(n_lat_ref)


def _final_combine(y, out_slabs, top_p, gate, final_gain):
    tile = pl.BlockSpec((TM, D_MODEL), lambda i: (i, 0))
    return pl.pallas_call(
        _final_combine_kernel,
        out_shape=[jax.ShapeDtypeStruct((T_CTX, D_MODEL), F32), jax.ShapeDtypeStruct((T_LAT, D_MODEL), F32)],
        grid=(N_TILES,),
        in_specs=[pl.BlockSpec((TOP_K * TM * SLAB, LANES), lambda i: (i, 0)), tile,
                  pl.BlockSpec((TM, LANES), lambda i: (i, 0)),
                  pl.BlockSpec((None, 1, D_MODEL), lambda i: (_mod_row(i), 0, 0)),
                  pl.BlockSpec((1, D_MODEL), lambda i: (0, 0))],
        out_specs=[pl.BlockSpec((TM, D_MODEL), lambda i: (jnp.minimum(i, CTX_TILES - 1), 0)),
                   pl.BlockSpec((TM, D_MODEL), lambda i: (jnp.maximum(i - CTX_TILES, 0), 0))],
        compiler_params=_params(("arbitrary",)),
        name="moe_combine",
    )(out_slabs, y, top_p, gate, final_gain.reshape(1, D_MODEL))


def _moe_experts_layer(layer, routed, w_gu, b_gu, w_down, b_down):
    h_slabs, top_i, top_p, rank, counts = routed
    pos, fill, experts_plan = _route_plan(top_i[:, :TOP_K].reshape(-1), rank[:, :TOP_K].reshape(-1),
                                          counts[0, :N_EXPERTS].astype(jnp.int32))
    xs, inv = _dispatch(h_slabs, pos, fill)
    return _moe_experts(layer, xs, experts_plan, inv, w_gu, b_gu, w_down, b_down), top_p


def kernel(x_prompt, x_sample, c, cache_a_k, cache_a_v, state_b_C, state_b_n, state_b_m, cache_c_k, cache_c_v, cache_d_k, cache_d_v, c_ctx, w_mod, b_mod, norm1_g, norm2_g, w_in_even, w_out_even, a_q_gain, a_k_gain, b_gate_bias, b_norm_gain, w_in_odd, w_out_odd, c_rpb, d_lambda, d_norm_gain, router_w, router_b, expert_w_gu, expert_b_gu, expert_w_down, expert_b_down, final_norm_g):
    y = (x_prompt.reshape(T_CTX, D_MODEL), x_sample.reshape(T_LAT, D_MODEL))
    cond = jnp.zeros((MOD_ROWS, D_MODEL), F32).at[0].set(c_ctx).at[1:1 + DEC_BATCH].set(c)
    mod = _modulation(cond, w_mod, b_mod).reshape(DEPTH, MOD_ROWS, 6, 1, D_MODEL)
    rope_cos, rope_sin = _rope_tables()
    scale = HD ** -0.5
    outs = {}

    for layer in range(DEPTH):
        sh1, sc1, g1, sh2, sc2, g2 = (mod[layer, :, k] for k in range(6))
        j = layer // 2
        if layer % 2 == 0:
            w = w_in_even[j]
            sizes = np.cumsum([0, 512, 128, 128, 512, 512, 512, 512, 16])
            aq, ak, av, bq, bk, bv, bo, bg = (w[:, sizes[k]:sizes[k + 1]] for k in range(8))
            w_in = jnp.concatenate([bo, bq, bk, bv, bg, jnp.zeros((D_MODEL, EV_AQ - EV_BG - 16), F32), aq, ak, av],
                                   axis=1).astype(BF16)
            qg = jnp.tile(a_q_gain[j], LANES // HD).reshape(1, LANES)
            kg = jnp.tile(a_k_gain[j], LANES // HD).reshape(1, LANES)
            specs = ((EV_AQ, 512, 0, True, scale, BF16, None), (EV_AK, 128, 1, True, 1.0, BF16, None),
                     (EV_AV, 128, None, False, 1.0, BF16, None),
                     (EV_AK, 128, 1, False, 1.0, F32, A_KV), (EV_AV, 128, None, False, 1.0, F32, A_KV))
            y, p, (qa, ka, va, new_ak, new_av) = _project(y, norm1_g[layer], sc1, sh1, w_in, rope_cos, rope_sin,
                                                          [qg, kg], specs, keep_p=EV_KEEP)
            oa_ctx = _attention(qa, ka, va, ctx=True, group=A_HEADS // A_KV, n_kv=A_KV)
            cache = (cache_a_k[:, j].astype(BF16), cache_a_v[:, j].astype(BF16))
            oa_lat = _attention(qa, ka, va, ctx=False, group=A_HEADS // A_KV, n_kv=A_KV, cache=cache)
            zc = jnp.zeros((BATCH, 2, B_HEADS, B_DK, B_DV), F32)
            zn = jnp.zeros((BATCH, 2, B_HEADS, 1, B_DK), F32)
            zm = jnp.zeros((BATCH, 2, B_HEADS, 1, LANES), F32)
            hb_ctx, bC, bn, bm = _mlstm(p, b_gate_bias[j], zc, zn, zm, ctx=True)
            m0 = jnp.broadcast_to(state_b_m[:, j][..., None, None], (DEC_BATCH, 2, B_HEADS, 1, LANES))
            hb_lat, _, _, _ = _mlstm(p, b_gate_bias[j], state_b_C[:, j], state_b_n[:, j][:, :, :, None, :], m0,
                                     ctx=False)
            y, *routed = _merge_router(y, g1, w_out_even[j].astype(BF16), (oa_ctx, oa_lat), (hb_ctx, hb_lat),
                                       norm2_g[layer], sc2, sh2, router_w[layer], router_b[layer],
                                       p=p, norm_gain=b_norm_gain[j])
            outs.setdefault("a_k", []).append(new_ak)
            outs.setdefault("a_v", []).append(new_av)
            outs.setdefault("b_C", []).append(bC)
            outs.setdefault("b_n", []).append(bn[:, :, :, 0, :])
            outs.setdefault("b_m", []).append(bm[:, :, :, 0, 0])
        else:
            specs = ((0, 512, None, False, scale, BF16, None), (512, 512, None, False, 1.0, BF16, None),
                     (1024, 512, None, False, 1.0, BF16, None), (1536, 512, None, True, scale, BF16, None),
                     (2048, 512, None, True, 1.0, BF16, None), (2560, 512, None, False, 1.0, BF16, None),
                     (512, 512, None, False, 1.0, F32, C_HEADS), (1024, 512, None, False, 1.0, F32, C_HEADS),
                     (2048, 512, None, False, 1.0, F32, 2 * D_HEADS), (2560, 512, None, False, 1.0, F32, D_HEADS))
            y, p, (qc, kc, vc, qd, kd, vd, new_ck, new_cv, new_dk, new_dv) = _project(
                y, norm1_g[layer], sc1, sh1, w_in_odd[j].astype(BF16), rope_cos, rope_sin, [], specs, keep_p=False)
            lam_init = 0.8 - 0.6 * math.exp(-0.3 * layer)
            lp = d_lambda[j].astype(F32)
            lam = jnp.exp(jnp.sum(lp[0] * lp[1])) - jnp.exp(jnp.sum(lp[2] * lp[3])) + lam_init
            lam_vec = jnp.stack([lam, jnp.asarray(1.0 - lam_init, F32)]).astype(F32)
            diff = (lam_vec, d_norm_gain[j])
            oc_ctx = _attention(qc, kc, vc, ctx=True, n_kv=C_HEADS)
            od_ctx = _attention(qd, kd, vd, ctx=True, diff=diff)
            bias = _neighbourhood_bias(c_rpb[j])
            oc_lat = _attention(qc, kc, vc, ctx=False, n_kv=C_HEADS, bias=bias,
                                cache=(cache_c_k[:, j].astype(BF16), cache_c_v[:, j].astype(BF16)))
            kd_cache = cache_d_k[:, j].reshape(DEC_BATCH, 2 * D_HEADS, PAST_LEN, HD).astype(BF16)
            od_lat = _attention(qd, kd, vd, ctx=False, diff=diff, cache=(kd_cache, cache_d_v[:, j].astype(BF16)))
            y, *routed = _merge_router(y, g1, w_out_odd[j].astype(BF16), (oc_ctx, oc_lat), (od_ctx, od_lat),
                                       norm2_g[layer], sc2, sh2, router_w[layer], router_b[layer])
            outs.setdefault("c_k", []).append(new_ck)
            outs.setdefault("c_v", []).append(new_cv)
            outs.setdefault("d_k", []).append(new_dk.reshape(BATCH, D_HEADS, 2, SEQ, HD))
            outs.setdefault("d_v", []).append(new_dv)
        out_slabs, top_p = _moe_experts_layer(layer, routed, expert_w_gu, expert_b_gu, expert_w_down,
                                              expert_b_down)
        y = (y, out_slabs, top_p, g2)
    y_prompt, y_sample = _final_combine(*y, final_norm_g)
    stack = lambda k: jnp.stack(outs[k], axis=1)
    return (y_prompt.reshape(BATCH, SEQ, D_MODEL), y_sample.reshape(DEC_BATCH, DEC_SEQ, D_MODEL),
            stack("a_k"), stack("a_v"), stack("b_C"), stack("b_n"), stack("b_m"),
            stack("c_k"), stack("c_v"), stack("d_k"), stack("d_v"))
```
